```python
import jax, jax.numpy as jnp
from jax import lax
import numpy as np

D_MODEL = 1024
BATCH = 8
SEQ = 8192
DEPTH = 1

CONF_WIDTH = 1024
CONF_KERNEL = 31
GDN_HEADS = 8
GDN_HEAD_K = 128
GDN_HEAD_V = 128
GDN_KEY_WIDTH = GDN_HEADS * GDN_HEAD_K
GDN_VAL_WIDTH = GDN_HEADS * GDN_HEAD_V
GDN_CONV = 4
GDN_CHUNK = 64
LN_EPS = 1e-5
RMS_EPS = 1e-6
L2_EPS = 1e-6
DN_ALPHA = (2 * DEPTH) ** 0.25
DN_BETA = (8 * DEPTH) ** -0.25

IN_SPLITS = (CONF_WIDTH, CONF_WIDTH, CONF_WIDTH,
             GDN_KEY_WIDTH, GDN_KEY_WIDTH, GDN_VAL_WIDTH, GDN_VAL_WIDTH,
             GDN_HEADS, GDN_HEADS,
             D_MODEL, D_MODEL)
IN_WIDTH = sum(IN_SPLITS)

kernel_name = "hybrid_conformer_gdn_deepnorm"


def _split_cols(t, widths):
    out, start = [], 0
    for w in widths:
        out.append(t[..., start:start + w])
        start += w
    return out


def _layernorm(x, g, b):
    xf = x.astype(jnp.float32)
    mu = xf.mean(-1, keepdims=True)
    var = jnp.square(xf - mu).mean(-1, keepdims=True)
    y = (xf - mu) * lax.rsqrt(var + LN_EPS) * g.astype(jnp.float32) + b.astype(jnp.float32)
    return y.astype(x.dtype)


def _causal_depthwise_conv(x, w, b=None):
    K, C = w.shape
    y = lax.conv_general_dilated(
        x, w[:, None, :].astype(x.dtype), window_strides=(1,), padding=((K - 1, 0),),
        dimension_numbers=('NWC', 'WIO', 'NWC'), feature_group_count=C)
    if b is not None:
        y = y + b.astype(x.dtype)
    return y


def _chunk_gated_delta_rule(q, k, v, g, beta):
    B_, S, H, Dk = q.shape
    Dv = v.shape[-1]
    C = GDN_CHUNK
    N = S // C

    def chunk4(t):
        return jnp.moveaxis(t.reshape(B_, N, C, H, t.shape[-1]), 3, 2)

    def chunk3(t):
        return jnp.moveaxis(t.reshape(B_, N, C, H), 3, 2)

    q, k, v = chunk4(q), chunk4(k), chunk4(v)
    g, beta = chunk3(g), chunk3(beta)
    gc = jnp.cumsum(g, axis=-1)

    idx = jnp.arange(C)
    causal = idx[:, None] >= idx[None, :]
    strict = idx[:, None] > idx[None, :]
    decay = jnp.exp(jnp.where(causal, gc[..., :, None] - gc[..., None, :], -jnp.inf))

    kb = k * beta[..., None]
    L = jnp.where(strict, jnp.einsum('bnhik,bnhjk->bnhij', kb, k) * decay, 0.0)
    eye = jnp.eye(C, dtype=q.dtype)
    rhs = jnp.concatenate([v * beta[..., None], kb * jnp.exp(gc)[..., None]], axis=-1)
    sol = lax.linalg.triangular_solve(eye + L, rhs, left_side=True, lower=True, unit_diagonal=True)
    u, w = sol[..., :Dv], sol[..., Dv:]

    intra = jnp.where(causal, jnp.einsum('bnhik,bnhjk->bnhij', q, k) * decay, 0.0)
    q_dec = q * jnp.exp(gc)[..., None]
    g_last = gc[..., -1]
    k_dec = k * jnp.exp(g_last[..., None] - gc)[..., None]
    chunk_decay = jnp.exp(g_last)

    def step(state, inp):
        q_i, k_i, u_i, w_i, a_i, d_i = inp
        v_new = u_i - jnp.einsum('bhck,bhkv->bhcv', w_i, state)
        o = jnp.einsum('bhck,bhkv->bhcv', q_i, state) + jnp.einsum('bhij,bhjv->bhiv', a_i, v_new)
        state = state * d_i[..., None, None] + jnp.einsum('bhck,bhcv->bhkv', k_i, v_new)
        return state, o

    xs = tuple(jnp.moveaxis(t, 1, 0) for t in (q_dec, k_dec, u, w, intra, chunk_decay))
    state0 = jnp.zeros((B_, H, Dk, Dv), q.dtype)
    _, o = lax.scan(step, state0, xs)
    o = jnp.moveaxis(o, 0, 1)
    return jnp.moveaxis(o, 2, 3).reshape(B_, S, H, Dv)


def _fwd_setup_inputs(seed: int = 0) -> dict:
    key = jax.random.key(seed)
    ks = jax.random.split(key, 16)
    f32 = jnp.float32
    x = jax.random.normal(ks[0], (BATCH, SEQ, D_MODEL), f32)
    col_scale = jnp.concatenate([
        jnp.full((w,), DN_BETA if i in (0, 5) else 1.0, f32) for i, w in enumerate(IN_SPLITS)])
    w_in = jax.random.normal(ks[1], (D_MODEL, IN_WIDTH), f32) * D_MODEL ** -0.5 * col_scale
    conf_dw_w = jax.random.normal(ks[2], (CONF_KERNEL, CONF_WIDTH), f32) * CONF_KERNEL ** -0.5
    conf_dw_b = 0.01 * jax.random.normal(ks[3], (CONF_WIDTH,), f32)
    conf_ln_g = 1.0 + 0.02 * jax.random.normal(ks[4], (CONF_WIDTH,), f32)
    conf_ln_b = 0.02 * jax.random.normal(ks[5], (CONF_WIDTH,), f32)
    conf_w_out = jax.random.normal(ks[6], (CONF_WIDTH, D_MODEL), f32) * CONF_WIDTH ** -0.5 * DN_BETA
    gdn_conv_w = jax.random.normal(ks[7], (GDN_CONV, 2 * GDN_KEY_WIDTH + GDN_VAL_WIDTH), f32) * GDN_CONV ** -0.5
    gdn_A_log = jnp.log(jax.random.uniform(ks[8], (GDN_HEADS,), f32, 1.0, 16.0))
    dt = jnp.exp(jax.random.uniform(ks[9], (GDN_HEADS,), f32, np.log(1e-3), np.log(1e-1)))
    gdn_dt_bias = dt + jnp.log(-jnp.expm1(-dt))
    gdn_norm_g = 1.0 + 0.02 * jax.random.normal(ks[10], (GDN_HEAD_V,), f32)
    gdn_w_out = jax.random.normal(ks[11], (GDN_VAL_WIDTH, D_MODEL), f32) * GDN_VAL_WIDTH ** -0.5 * DN_BETA
    w_o = jax.random.normal(ks[12], (D_MODEL, D_MODEL), f32) * D_MODEL ** -0.5 * DN_BETA
    post_ln_g = 1.0 + 0.02 * jax.random.normal(ks[13], (D_MODEL,), f32)
    post_ln_b = 0.02 * jax.random.normal(ks[14], (D_MODEL,), f32)
    return {"x": x, "w_in": w_in, "conf_dw_w": conf_dw_w, "conf_dw_b": conf_dw_b,
            "conf_ln_g": conf_ln_g, "conf_ln_b": conf_ln_b, "conf_w_out": conf_w_out,
            "gdn_conv_w": gdn_conv_w, "gdn_A_log": gdn_A_log, "gdn_dt_bias": gdn_dt_bias,
            "gdn_norm_g": gdn_norm_g, "gdn_w_out": gdn_w_out, "w_o": w_o,
            "post_ln_g": post_ln_g, "post_ln_b": post_ln_b}


def _fwd_reference(x, w_in, conf_dw_w, conf_dw_b, conf_ln_g, conf_ln_b, conf_w_out,
              gdn_conv_w, gdn_A_log, gdn_dt_bias, gdn_norm_g, gdn_w_out, w_o,
              post_ln_g, post_ln_b):
    f32 = jnp.float32
    for _ in range(DEPTH):
        B_, S, _ = x.shape
        proj = jnp.einsum('bsd,de->bse', x, w_in)
        (c_val, c_glu, c_z, q, k, v, g_z, b_logit, a_logit,
         gate_c, gate_g) = _split_cols(proj, IN_SPLITS)

        a = c_val * jax.nn.sigmoid(c_glu)
        a = _causal_depthwise_conv(a, conf_dw_w, conf_dw_b)
        a = jax.nn.silu(_layernorm(a, conf_ln_g, conf_ln_b))
        y_conf = jnp.einsum('bsc,cd->bsd', a * jax.nn.silu(c_z), conf_w_out)

        qkv = jax.nn.silu(_causal_depthwise_conv(jnp.concatenate([q, k, v], axis=-1), gdn_conv_w))
        q, k, v = _split_cols(qkv, (GDN_KEY_WIDTH, GDN_KEY_WIDTH, GDN_VAL_WIDTH))
        q = q.reshape(B_, S, GDN_HEADS, GDN_HEAD_K).astype(f32)
        k = k.reshape(B_, S, GDN_HEADS, GDN_HEAD_K).astype(f32)
        v = v.reshape(B_, S, GDN_HEADS, GDN_HEAD_V).astype(f32)
        q = q * lax.rsqrt(jnp.sum(q * q, -1, keepdims=True) + L2_EPS) * (GDN_HEAD_K ** -0.5)
        k = k * lax.rsqrt(jnp.sum(k * k, -1, keepdims=True) + L2_EPS)
        beta = jax.nn.sigmoid(b_logit.astype(f32))
        g = -jnp.exp(gdn_A_log.astype(f32)) * jax.nn.softplus(a_logit.astype(f32) + gdn_dt_bias.astype(f32))
        o = _chunk_gated_delta_rule(q, k, v, g, beta)
        o = o * lax.rsqrt(jnp.mean(o * o, -1, keepdims=True) + RMS_EPS) * gdn_norm_g.astype(f32)
        o = o.reshape(B_, S, GDN_VAL_WIDTH).astype(x.dtype) * jax.nn.silu(g_z)
        y_gdn = jnp.einsum('bsc,cd->bsd', o, gdn_w_out)

        h = jax.nn.sigmoid(gate_c) * y_conf + jax.nn.sigmoid(gate_g) * y_gdn
        sub = jnp.einsum('bsd,de->bse', h, w_o)
        x = _layernorm(DN_ALPHA * x + sub, post_ln_g, post_ln_b)
    return x


import jax as _jax
import jax.numpy as _jnp

TWIN_FORMAT = 'train_step'
FWD_PARAMS = ['x', 'w_in', 'conf_dw_w', 'conf_dw_b', 'conf_ln_g', 'conf_ln_b', 'conf_w_out', 'gdn_conv_w', 'gdn_A_log', 'gdn_dt_bias', 'gdn_norm_g', 'gdn_w_out', 'w_o', 'post_ln_g', 'post_ln_b']
TWIN_WEIGHTS = ['w_in', 'conf_dw_w', 'conf_dw_b', 'conf_ln_g', 'conf_ln_b', 'conf_w_out', 'gdn_conv_w', 'gdn_A_log', 'gdn_dt_bias', 'gdn_norm_g', 'gdn_w_out', 'w_o', 'post_ln_g', 'post_ln_b']
TWIN_DIFF_INPUT = 'x'
TWIN_INPUTS = ['x', 'w_in', 'conf_dw_w', 'conf_dw_b', 'conf_ln_g', 'conf_ln_b', 'conf_w_out', 'gdn_conv_w', 'gdn_A_log', 'gdn_dt_bias', 'gdn_norm_g', 'gdn_w_out', 'w_o', 'post_ln_g', 'post_ln_b', 'loss_target', 'm_w_in', 'm_conf_dw_w', 'm_conf_dw_b', 'm_conf_ln_g', 'm_conf_ln_b', 'm_conf_w_out', 'm_gdn_conv_w', 'm_gdn_A_log', 'm_gdn_dt_bias', 'm_gdn_norm_g', 'm_gdn_w_out', 'm_w_o', 'm_post_ln_g', 'm_post_ln_b', 'v_w_in', 'v_conf_dw_w', 'v_conf_dw_b', 'v_conf_ln_g', 'v_conf_ln_b', 'v_conf_w_out', 'v_gdn_conv_w', 'v_gdn_A_log', 'v_gdn_dt_bias', 'v_gdn_norm_g', 'v_gdn_w_out', 'v_w_o', 'v_post_ln_g', 'v_post_ln_b']
TWIN_OUTPUTS = ['loss', 'grad_x', 'grad_w_in', 'grad_conf_dw_w', 'grad_conf_dw_b', 'grad_conf_ln_g', 'grad_conf_ln_b', 'grad_conf_w_out', 'grad_gdn_conv_w', 'grad_gdn_A_log', 'grad_gdn_dt_bias', 'grad_gdn_norm_g', 'grad_gdn_w_out', 'grad_w_o', 'grad_post_ln_g', 'grad_post_ln_b', 'delta_w_in', 'delta_conf_dw_w', 'delta_conf_dw_b', 'delta_conf_ln_g', 'delta_conf_ln_b', 'delta_conf_w_out', 'delta_gdn_conv_w', 'delta_gdn_A_log', 'delta_gdn_dt_bias', 'delta_gdn_norm_g', 'delta_gdn_w_out', 'delta_w_o', 'delta_post_ln_g', 'delta_post_ln_b', 'new_m_w_in', 'new_m_conf_dw_w', 'new_m_conf_dw_b', 'new_m_conf_ln_g', 'new_m_conf_ln_b', 'new_m_conf_w_out', 'new_m_gdn_conv_w', 'new_m_gdn_A_log', 'new_m_gdn_dt_bias', 'new_m_gdn_norm_g', 'new_m_gdn_w_out', 'new_m_w_o', 'new_m_post_ln_g', 'new_m_post_ln_b', 'new_v_w_in', 'new_v_conf_dw_w', 'new_v_conf_dw_b', 'new_v_conf_ln_g', 'new_v_conf_ln_b', 'new_v_conf_w_out', 'new_v_gdn_conv_w', 'new_v_gdn_A_log', 'new_v_gdn_dt_bias', 'new_v_gdn_norm_g', 'new_v_gdn_w_out', 'new_v_w_o', 'new_v_post_ln_g', 'new_v_post_ln_b']
TWIN_LEAF_KINDS = {'loss': 'loss', 'grad_x': 'grad_x', 'grad_w_in': 'grad_w', 'grad_conf_dw_w': 'grad_w', 'grad_conf_dw_b': 'grad_w', 'grad_conf_ln_g': 'grad_w', 'grad_conf_ln_b': 'grad_w', 'grad_conf_w_out': 'grad_w', 'grad_gdn_conv_w': 'grad_w', 'grad_gdn_A_log': 'grad_w', 'grad_gdn_dt_bias': 'grad_w', 'grad_gdn_norm_g': 'grad_w', 'grad_gdn_w_out': 'grad_w', 'grad_w_o': 'grad_w', 'grad_post_ln_g': 'grad_w', 'grad_post_ln_b': 'grad_w', 'delta_w_in': 'delta_w', 'delta_conf_dw_w': 'delta_w', 'delta_conf_dw_b': 'delta_w', 'delta_conf_ln_g': 'delta_w', 'delta_conf_ln_b': 'delta_w', 'delta_conf_w_out': 'delta_w', 'delta_gdn_conv_w': 'delta_w', 'delta_gdn_A_log': 'delta_w', 'delta_gdn_dt_bias': 'delta_w', 'delta_gdn_norm_g': 'delta_w', 'delta_gdn_w_out': 'delta_w', 'delta_w_o': 'delta_w', 'delta_post_ln_g': 'delta_w', 'delta_post_ln_b': 'delta_w', 'new_m_w_in': 'new_m', 'new_m_conf_dw_w': 'new_m', 'new_m_conf_dw_b': 'new_m', 'new_m_conf_ln_g': 'new_m', 'new_m_conf_ln_b': 'new_m', 'new_m_conf_w_out': 'new_m', 'new_m_gdn_conv_w': 'new_m', 'new_m_gdn_A_log': 'new_m', 'new_m_gdn_dt_bias': 'new_m', 'new_m_gdn_norm_g': 'new_m', 'new_m_gdn_w_out': 'new_m', 'new_m_w_o': 'new_m', 'new_m_post_ln_g': 'new_m', 'new_m_post_ln_b': 'new_m', 'new_v_w_in': 'new_v', 'new_v_conf_dw_w': 'new_v', 'new_v_conf_dw_b': 'new_v', 'new_v_conf_ln_g': 'new_v', 'new_v_conf_ln_b': 'new_v', 'new_v_conf_w_out': 'new_v', 'new_v_gdn_conv_w': 'new_v', 'new_v_gdn_A_log': 'new_v', 'new_v_gdn_dt_bias': 'new_v', 'new_v_gdn_norm_g': 'new_v', 'new_v_gdn_w_out': 'new_v', 'new_v_w_o': 'new_v', 'new_v_post_ln_g': 'new_v', 'new_v_post_ln_b': 'new_v'}


def _forward(args):
    return _fwd_reference(*[args[k] for k in FWD_PARAMS])


def _output_shape():
    def fwd():
        inp = _fwd_setup_inputs(0)
        return _fwd_reference(*[inp[k] for k in FWD_PARAMS])
    out = _jax.eval_shape(fwd)
    return out.shape, out.dtype

N_MICROBATCH = 1
ADAM_LR = 0.001
ADAM_B1 = 0.9
ADAM_B2 = 0.999
ADAM_EPS = 1e-08
ADAM_WD = 0.01
ADAM_STEP = 10
PER_EXAMPLE_BATCH_AXIS = {'x': 0, 'loss_target': 0}
SHARED_INPUTS = []
_WEIGHT_DTYPES = {'w_in': _jnp.float32, 'conf_dw_w': _jnp.float32, 'conf_dw_b': _jnp.float32, 'conf_ln_g': _jnp.float32, 'conf_ln_b': _jnp.float32, 'conf_w_out': _jnp.float32, 'gdn_conv_w': _jnp.float32, 'gdn_A_log': _jnp.float32, 'gdn_dt_bias': _jnp.float32, 'gdn_norm_g': _jnp.float32, 'gdn_w_out': _jnp.float32, 'w_o': _jnp.float32, 'post_ln_g': _jnp.float32, 'post_ln_b': _jnp.float32}
MOMENT_SCALE = {'w_in': 1.944708e-02, 'conf_dw_w': 1.489048e-02, 'conf_dw_b': 5.433113e-02, 'conf_ln_g': 1.799912e-02, 'conf_ln_b': 1.787853e-02, 'conf_w_out': 2.493491e-02, 'gdn_conv_w': 1.932860e-02, 'gdn_A_log': 1.029848e-01, 'gdn_dt_bias': 9.834776e-02, 'gdn_norm_g': 6.294637e-02, 'gdn_w_out': 4.141594e-02, 'w_o': 4.895071e-02, 'post_ln_g': 6.399247e+01, 'post_ln_b': 1.228537e+00}


def _to_microbatches(a, axis):
    t = _jnp.moveaxis(a, axis, 0)
    t = t.reshape((N_MICROBATCH, t.shape[0] // N_MICROBATCH) + t.shape[1:])
    return _jnp.moveaxis(t, 1, axis + 1)


def setup_inputs(seed: int = 0) -> dict:
    inp = _fwd_setup_inputs(seed)
    key = _jax.random.fold_in(_jax.random.key(seed), 7919)
    shape, _ = _output_shape()
    out = dict(inp)
    out["loss_target"] = _jax.random.normal(_jax.random.fold_in(key, 0), shape, _jnp.float32)
    for i, name in enumerate(TWIN_WEIGHTS):
        w = inp[name].astype(_jnp.float32)
        if MOMENT_SCALE is None:
            s = _jnp.sqrt(_jnp.mean(_jnp.square(w)) + 1e-30)
        else:
            s = MOMENT_SCALE[name]
        km, kv = _jax.random.split(_jax.random.fold_in(key, i + 1))
        out[name] = w
        out["m_" + name] = s * _jax.random.normal(km, w.shape, _jnp.float32)
        out["v_" + name] = (s * s) * _jax.random.uniform(kv, w.shape, _jnp.float32, 0.5, 1.5)
    if N_MICROBATCH > 1:
        for name, axis in PER_EXAMPLE_BATCH_AXIS.items():
            out[name] = _to_microbatches(out[name], axis)
    return {'x': out['x'], 'w_in': out['w_in'], 'conf_dw_w': out['conf_dw_w'], 'conf_dw_b': out['conf_dw_b'], 'conf_ln_g': out['conf_ln_g'], 'conf_ln_b': out['conf_ln_b'], 'conf_w_out': out['conf_w_out'], 'gdn_conv_w': out['gdn_conv_w'], 'gdn_A_log': out['gdn_A_log'], 'gdn_dt_bias': out['gdn_dt_bias'], 'gdn_norm_g': out['gdn_norm_g'], 'gdn_w_out': out['gdn_w_out'], 'w_o': out['w_o'], 'post_ln_g': out['post_ln_g'], 'post_ln_b': out['post_ln_b'], 'loss_target': out['loss_target'], 'm_w_in': out['m_w_in'], 'm_conf_dw_w': out['m_conf_dw_w'], 'm_conf_dw_b': out['m_conf_dw_b'], 'm_conf_ln_g': out['m_conf_ln_g'], 'm_conf_ln_b': out['m_conf_ln_b'], 'm_conf_w_out': out['m_conf_w_out'], 'm_gdn_conv_w': out['m_gdn_conv_w'], 'm_gdn_A_log': out['m_gdn_A_log'], 'm_gdn_dt_bias': out['m_gdn_dt_bias'], 'm_gdn_norm_g': out['m_gdn_norm_g'], 'm_gdn_w_out': out['m_gdn_w_out'], 'm_w_o': out['m_w_o'], 'm_post_ln_g': out['m_post_ln_g'], 'm_post_ln_b': out['m_post_ln_b'], 'v_w_in': out['v_w_in'], 'v_conf_dw_w': out['v_conf_dw_w'], 'v_conf_dw_b': out['v_conf_dw_b'], 'v_conf_ln_g': out['v_conf_ln_g'], 'v_conf_ln_b': out['v_conf_ln_b'], 'v_conf_w_out': out['v_conf_w_out'], 'v_gdn_conv_w': out['v_gdn_conv_w'], 'v_gdn_A_log': out['v_gdn_A_log'], 'v_gdn_dt_bias': out['v_gdn_dt_bias'], 'v_gdn_norm_g': out['v_gdn_norm_g'], 'v_gdn_w_out': out['v_gdn_w_out'], 'v_w_o': out['v_w_o'], 'v_post_ln_g': out['v_post_ln_g'], 'v_post_ln_b': out['v_post_ln_b']}


def _loss(weights, diff, rest, loss_target):
    with _jax.named_scope("forward"):
        args = {**rest, TWIN_DIFF_INPUT: diff, **{k: w.astype(_WEIGHT_DTYPES[k]) for k, w in weights.items()}}
        y = _forward(args)
    with _jax.named_scope("loss_head"):
        err = _jnp.square(y.astype(_jnp.float32) - loss_target)
        return 0.5 * _jnp.sum(_jnp.mean(err, axis=-1)) if err.ndim else 0.5 * err


def _adamw(w, g, m, v):
    m = ADAM_B1 * m + (1.0 - ADAM_B1) * g
    v = ADAM_B2 * v + (1.0 - ADAM_B2) * _jnp.square(g)
    m_hat = m / (1.0 - ADAM_B1 ** ADAM_STEP)
    v_hat = v / (1.0 - ADAM_B2 ** ADAM_STEP)
    delta = -ADAM_LR * (m_hat / (_jnp.sqrt(v_hat) + ADAM_EPS) + ADAM_WD * w)
    return delta, m, v


def reference(x, w_in, conf_dw_w, conf_dw_b, conf_ln_g, conf_ln_b, conf_w_out, gdn_conv_w, gdn_A_log, gdn_dt_bias, gdn_norm_g, gdn_w_out, w_o, post_ln_g, post_ln_b, loss_target, m_w_in, m_conf_dw_w, m_conf_dw_b, m_conf_ln_g, m_conf_ln_b, m_conf_w_out, m_gdn_conv_w, m_gdn_A_log, m_gdn_dt_bias, m_gdn_norm_g, m_gdn_w_out, m_w_o, m_post_ln_g, m_post_ln_b, v_w_in, v_conf_dw_w, v_conf_dw_b, v_conf_ln_g, v_conf_ln_b, v_conf_w_out, v_gdn_conv_w, v_gdn_A_log, v_gdn_dt_bias, v_gdn_norm_g, v_gdn_w_out, v_w_o, v_post_ln_g, v_post_ln_b):
    given = dict(x=x, w_in=w_in, conf_dw_w=conf_dw_w, conf_dw_b=conf_dw_b, conf_ln_g=conf_ln_g, conf_ln_b=conf_ln_b, conf_w_out=conf_w_out, gdn_conv_w=gdn_conv_w, gdn_A_log=gdn_A_log, gdn_dt_bias=gdn_dt_bias, gdn_norm_g=gdn_norm_g, gdn_w_out=gdn_w_out, w_o=w_o, post_ln_g=post_ln_g, post_ln_b=post_ln_b, loss_target=loss_target, m_w_in=m_w_in, m_conf_dw_w=m_conf_dw_w, m_conf_dw_b=m_conf_dw_b, m_conf_ln_g=m_conf_ln_g, m_conf_ln_b=m_conf_ln_b, m_conf_w_out=m_conf_w_out, m_gdn_conv_w=m_gdn_conv_w, m_gdn_A_log=m_gdn_A_log, m_gdn_dt_bias=m_gdn_dt_bias, m_gdn_norm_g=m_gdn_norm_g, m_gdn_w_out=m_gdn_w_out, m_w_o=m_w_o, m_post_ln_g=m_post_ln_g, m_post_ln_b=m_post_ln_b, v_w_in=v_w_in, v_conf_dw_w=v_conf_dw_w, v_conf_dw_b=v_conf_dw_b, v_conf_ln_g=v_conf_ln_g, v_conf_ln_b=v_conf_ln_b, v_conf_w_out=v_conf_w_out, v_gdn_conv_w=v_gdn_conv_w, v_gdn_A_log=v_gdn_A_log, v_gdn_dt_bias=v_gdn_dt_bias, v_gdn_norm_g=v_gdn_norm_g, v_gdn_w_out=v_gdn_w_out, v_w_o=v_w_o, v_post_ln_g=v_post_ln_g, v_post_ln_b=v_post_ln_b)
    weights = {n: given[n] for n in TWIN_WEIGHTS}
    shared = {n: given[n] for n in SHARED_INPUTS}
    per_example = {n: given[n] for n in ['x']}
    grad_fn = _jax.value_and_grad(_loss, argnums=(0, 1))

    def one_microbatch(ex, loss_target):
        ex = dict(ex)
        diff = ex.pop(TWIN_DIFF_INPUT)
        return grad_fn(weights, diff, {**shared, **ex}, loss_target)

    if N_MICROBATCH == 1:
        loss, (grad_w, grad_x) = one_microbatch(per_example, given["loss_target"])
    else:
        def body(carry, xs):
            loss_sum, grad_sum = carry
            l_k, (gw_k, gx_k) = one_microbatch(xs[0], xs[1])
            with _jax.named_scope("update"):
                return (loss_sum + l_k, _jax.tree.map(_jnp.add, grad_sum, gw_k)), gx_k

        init = (_jnp.zeros((), _jnp.float32), _jax.tree.map(_jnp.zeros_like, weights))
        (loss, grad_w), grad_x = _jax.lax.scan(body, init, (per_example, given["loss_target"]))
    with _jax.named_scope("update"):
        delta_w, new_m, new_v = {}, {}, {}
        for n in TWIN_WEIGHTS:
            delta_w[n], new_m[n], new_v[n] = _adamw(weights[n], grad_w[n], given["m_" + n], given["v_" + n])
    return (loss, grad_x, *[grad_w[n] for n in TWIN_WEIGHTS], *[delta_w[n] for n in TWIN_WEIGHTS],
            *[new_m[n] for n in TWIN_WEIGHTS], *[new_v[n] for n in TWIN_WEIGHTS])
```

```python
import functools

import jax
import jax.numpy as jnp
from jax import lax
from jax.experimental import pallas as pl
from jax.experimental.pallas import tpu as pltpu

f32 = jnp.float32
bf16 = jnp.bfloat16
HI = lax.Precision.HIGHEST
MESH = pl.DeviceIdType.MESH

D = 1024
NH = 8
HD = 128
CH = 64
KC = 31
KG = 4
HALO_C = 32
HALO_G = 8
LN_EPS = 1e-5
RMS_EPS = 1e-6
L2_EPS = 1e-6
DN_ALPHA = 2.0 ** 0.25
N_CHIPS = 4
W_IN_COLS = 9232
W_IN_BLK = W_IN_COLS // N_CHIPS
SQ_BLK = D // N_CHIPS
PACK_ROWS = 3200
HALF_ROWS = PACK_ROWS // 2
PACK_TILE = 320
VMEM_LIMIT = 52 * 1024 * 1024
TOKEN_TILE = 256
GATHER_ROWS = 3104

ADAM_LR = 0.001
ADAM_B1 = 0.9
ADAM_B2 = 0.999
ADAM_EPS = 1e-08
ADAM_WD = 0.01
ADAM_STEP = 10


def _sigmoid(x):
    return 1.0 / (1.0 + jnp.exp(-x))


def _silu_and_grad(x):
    s = _sigmoid(x)
    return x * s, s * (1.0 + x * (1.0 - s))


def _cparams(*sem):
    return pltpu.CompilerParams(dimension_semantics=sem, vmem_limit_bytes=VMEM_LIMIT)


def _mm_multi(a_list, b_list, addend=None, *, out_dtype, tm, tn, name, precision=None):
    n_pairs = len(a_list)
    m = a_list[0].shape[0]
    n = b_list[0].shape[1]
    has_add = addend is not None

    def body(*refs):
        a_refs = refs[:n_pairs]
        b_refs = refs[n_pairs:2 * n_pairs]
        o_ref = refs[-1]
        acc = None
        for a_ref, b_ref in zip(a_refs, b_refs):
            if precision is None:
                p = jnp.dot(a_ref[...].astype(bf16), b_ref[...].astype(bf16), preferred_element_type=f32)
            else:
                p = jnp.dot(a_ref[...], b_ref[...], preferred_element_type=f32, precision=precision)
            acc = p if acc is None else acc + p
        if has_add:
            acc = acc + refs[2 * n_pairs][...]
        o_ref[...] = acc.astype(out_dtype)

    in_specs = [pl.BlockSpec((tm, a.shape[1]), lambda j, i: (i, 0)) for a in a_list]
    in_specs += [pl.BlockSpec((b.shape[0], tn), lambda j, i: (0, j)) for b in b_list]
    args = list(a_list) + list(b_list)
    if has_add:
        in_specs.append(pl.BlockSpec((tm, tn), lambda j, i: (i, j)))
        args.append(addend)
    return pl.pallas_call(
        body, name=name, grid=(n // tn, m // tm),
        in_specs=in_specs, out_specs=pl.BlockSpec((tm, tn), lambda j, i: (i, j)),
        out_shape=jax.ShapeDtypeStruct((m, n), out_dtype),
        compiler_params=_cparams("parallel", "parallel"),
    )(*args)


def _mm_kloop(a, b, *, tm, tn, tk, name, precision=None):
    m, k = a.shape
    n = b.shape[1]
    nk = k // tk

    def body(a_ref, b_ref, o_ref):
        @pl.when(pl.program_id(2) == 0)
        def _():
            o_ref[...] = jnp.zeros_like(o_ref)
        if precision is None:
            o_ref[...] += jnp.dot(a_ref[...].astype(bf16), b_ref[...].astype(bf16), preferred_element_type=f32)
        else:
            o_ref[...] += jnp.dot(a_ref[...], b_ref[...], preferred_element_type=f32, precision=precision)

    return pl.pallas_call(
        body, name=name, grid=(n // tn, m // tm, nk),
        in_specs=[pl.BlockSpec((tm, tk), lambda j, i, kk: (i, kk)), pl.BlockSpec((tk, tn), lambda j, i, kk: (kk, j))],
        out_specs=pl.BlockSpec((tm, tn), lambda j, i, kk: (i, j)),
        out_shape=jax.ShapeDtypeStruct((m, n), f32),
        compiler_params=_cparams("parallel", "parallel", "arbitrary"),
    )(a, b)


def _conv_branch_core(ext_ref, n_rows, cz, w_ref, b_ref, g_ref, bb_ref):
    acc = jnp.broadcast_to(b_ref[...], (n_rows, D))
    for k in range(KC):
        acc = acc + w_ref[k:k + 1, :] * ext_ref[pl.ds(HALO_C - (KC - 1) + k, n_rows), :]
    mu = jnp.mean(acc, axis=-1, keepdims=True)
    cen = acc - mu
    var = jnp.mean(cen * cen, axis=-1, keepdims=True)
    rstd = lax.rsqrt(var + LN_EPS)
    xhat = cen * rstd
    ln = xhat * g_ref[...] + bb_ref[...]
    s, ds = _silu_and_grad(ln)
    zc, dzc = _silu_and_grad(cz)
    return xhat, rstd, s, ds, zc, dzc


def _conv_fwd(p_conv, dw_w, dw_b, ln_g, ln_b, *, tt):
    t = p_conv.shape[0]
    hb = tt // HALO_C

    def body(cv_ref, cg_ref, cz_ref, cvh_ref, cgh_ref, w_ref, b_ref, g_ref, bb_ref, u_ref, ext_ref):
        first = pl.program_id(0) == 0
        halo = cvh_ref[...] * _sigmoid(cgh_ref[...])
        ext_ref[0:HALO_C, :] = jnp.where(first, 0.0, halo)
        ext_ref[HALO_C:, :] = cv_ref[...] * _sigmoid(cg_ref[...])
        _, _, s, _, zc, _ = _conv_branch_core(ext_ref, tt, cz_ref[...], w_ref, b_ref, g_ref, bb_ref)
        u_ref[...] = (s * zc).astype(bf16)

    def main(col):
        return pl.BlockSpec((tt, D), lambda i: (i, col))

    def prev(col):
        return pl.BlockSpec((HALO_C, D), lambda i: (jnp.maximum(i * hb - 1, 0), col))

    vec = pl.BlockSpec((1, D), lambda i: (0, 0))
    return pl.pallas_call(
        body, name="conv_fwd", grid=(t // tt,),
        in_specs=[main(0), main(1), main(2), prev(0), prev(1),
                  pl.BlockSpec((HALO_C, D), lambda i: (0, 0)), vec, vec, vec],
        out_specs=pl.BlockSpec((tt, D), lambda i: (i, 0)),
        out_shape=jax.ShapeDtypeStruct((t, D), bf16),
        scratch_shapes=[pltpu.VMEM((tt + HALO_C, D), f32)],
        compiler_params=_cparams("parallel"),
    )(p_conv, p_conv, p_conv, p_conv, p_conv, dw_w, dw_b, ln_g, ln_b)


def _conv_bwd(p_conv, du, dw_w, dw_b, ln_g, ln_b, *, tt):
    t = p_conv.shape[0]
    hb = tt // HALO_C
    n_tiles = t // tt
    last_hb = t // HALO_C - 1
    ne = tt + HALO_C

    def body(cv_ref, cg_ref, cz_ref, du_ref, cvp_ref, cgp_ref, cvn_ref, cgn_ref, czn_ref, dun_ref,
             w_ref, b_ref, g_ref, bb_ref, dp_ref, dww_ref, dvec_ref, ext_ref, da1_ref):
        i = pl.program_id(0)
        first = i == 0
        last = i == n_tiles - 1

        @pl.when(first)
        def _():
            dww_ref[...] = jnp.zeros_like(dww_ref)
            dvec_ref[...] = jnp.zeros_like(dvec_ref)

        sig = _sigmoid(cg_ref[...])
        ext_ref[0:HALO_C, :] = jnp.where(first, 0.0, cvp_ref[...] * _sigmoid(cgp_ref[...]))
        ext_ref[HALO_C:HALO_C + tt, :] = cv_ref[...] * sig
        ext_ref[HALO_C + tt:, :] = cvn_ref[...] * _sigmoid(cgn_ref[...])
        cz = jnp.concatenate([cz_ref[...], czn_ref[...]], axis=0)
        du_all = jnp.concatenate([du_ref[...], jnp.where(last, 0.0, dun_ref[...])], axis=0)
        xhat, rstd, s, ds, zc, dzc = _conv_branch_core(ext_ref, ne, cz, w_ref, b_ref, g_ref, bb_ref)
        dln = du_all * zc * ds
        dxhat = dln * g_ref[...]
        da1 = rstd * (dxhat - jnp.mean(dxhat, axis=-1, keepdims=True)
                      - xhat * jnp.mean(dxhat * xhat, axis=-1, keepdims=True))
        da1_ref[...] = da1
        dcz = (du_all * s * dzc)[:tt]
        dvec_ref[0:1, :] += jnp.sum(da1[:tt], axis=0, keepdims=True)
        dvec_ref[1:2, :] += jnp.sum((dln * xhat)[:tt], axis=0, keepdims=True)
        dvec_ref[2:3, :] += jnp.sum(dln[:tt], axis=0, keepdims=True)
        da1_t = da1[:tt]
        da0 = jnp.zeros((tt, D), f32)
        for k in range(KC):
            da0 = da0 + w_ref[k:k + 1, :] * da1_ref[pl.ds(KC - 1 - k, tt), :]
            dww_ref[k:k + 1, :] += jnp.sum(
                da1_t * ext_ref[pl.ds(HALO_C - (KC - 1) + k, tt), :], axis=0, keepdims=True)
        cv = cv_ref[...]
        dp_ref[:, 0:D] = (da0 * sig).astype(bf16)
        dp_ref[:, D:2 * D] = (da0 * cv * sig * (1.0 - sig)).astype(bf16)
        dp_ref[:, 2 * D:] = dcz.astype(bf16)

    def main(col):
        return pl.BlockSpec((tt, D), lambda i: (i, col))

    def prev(col):
        return pl.BlockSpec((HALO_C, D), lambda i: (jnp.maximum(i * hb - 1, 0), col))

    def nxt(col):
        return pl.BlockSpec((HALO_C, D), lambda i: (jnp.minimum((i + 1) * hb, last_hb), col))

    vec = pl.BlockSpec((1, D), lambda i: (0, 0))
    return pl.pallas_call(
        body, name="conv_bwd", grid=(n_tiles,),
        in_specs=[main(0), main(1), main(2), main(0), prev(0), prev(1), nxt(0), nxt(1), nxt(2), nxt(0),
                  pl.BlockSpec((HALO_C, D), lambda i: (0, 0)), vec, vec, vec],
        out_specs=[pl.BlockSpec((tt, 3 * D), lambda i: (i, 0)),
                   pl.BlockSpec((HALO_C, D), lambda i: (0, 0)),
                   pl.BlockSpec((8, D), lambda i: (0, 0))],
        out_shape=[jax.ShapeDtypeStruct((t, 3 * D), bf16), jax.ShapeDtypeStruct((HALO_C, D), f32),
                   jax.ShapeDtypeStruct((8, D), f32)],
        scratch_shapes=[pltpu.VMEM((HALO_C + tt + HALO_C, D), f32), pltpu.VMEM((ne, D), f32)],
        compiler_params=_cparams("arbitrary"),
    )(p_conv, p_conv, p_conv, du, p_conv, p_conv, p_conv, p_conv, p_conv, du, dw_w, dw_b, ln_g, ln_b)


def _head_selectors():
    r = lax.broadcasted_iota(jnp.int32, (HD, D), 0)
    c = lax.broadcasted_iota(jnp.int32, (HD, D), 1) // HD
    return jnp.where(r == c, 1.0, 0.0).astype(f32), jnp.where(r == c + NH, 1.0, 0.0).astype(f32)


def _dot_hi(a, b):
    return lax.dot_general(a, b, (((1,), (0,)), ((), ())), precision=HI, preferred_element_type=f32)


def _dot_hi_nt(a, b):
    return lax.dot_general(a, b, (((1,), (1,)), ((), ())), precision=HI, preferred_element_type=f32)


def _dot_hi_tn(a, b):
    return lax.dot_general(a, b, (((0,), (0,)), ((), ())), precision=HI, preferred_element_type=f32)


def _softplus_and_sigmoid(x):
    e = jnp.exp(-jnp.abs(x))
    log1p = jnp.where(e < 1e-2, e * (1.0 - e * (0.5 - e * (1.0 / 3.0 - 0.25 * e))), jnp.log(1.0 + e))
    return jnp.maximum(x, 0.0) + log1p, _sigmoid(x)


def _gdn_short_conv(ext_ref, n_rows, w_ref):
    pre = jnp.zeros((n_rows, D), f32)
    for k in range(KG):
        pre = pre + w_ref[k:k + 1, :] * ext_ref[pl.ds(HALO_G - (KG - 1) + k, n_rows), :]
    return pre


def _l2norm_heads(act, scale):
    outs, rs = [], []
    for h in range(NH):
        a = act[:, h * HD:(h + 1) * HD]
        r = lax.rsqrt(jnp.sum(a * a, axis=-1, keepdims=True) + L2_EPS)
        outs.append(a * (r * scale))
        rs.append(jnp.broadcast_to(r, a.shape))
    return jnp.concatenate(outs, axis=-1), jnp.concatenate(rs, axis=-1)


def _gdn_pre_fwd(p_qkv, p_ba, cw, alog_b, dt_b, *, tt):
    t = p_qkv.shape[0]
    hb = tt // HALO_G

    def body(q_ref, k_ref, v_ref, qh_ref, kh_ref, vh_ref, ba_ref, wq_ref, wk_ref, wv_ref, al_ref, dt_ref,
             qn_ref, kn_ref, va_ref, gb_ref, bb_ref, ext_ref):
        first = pl.program_id(0) == 0

        def conv_act(x_ref, xh_ref, w_ref):
            ext_ref[0:HALO_G, :] = jnp.where(first, 0.0, xh_ref[...])
            ext_ref[HALO_G:, :] = x_ref[...]
            pre = _gdn_short_conv(ext_ref, tt, w_ref)
            return pre * _sigmoid(pre)

        qn_ref[...] = _l2norm_heads(conv_act(q_ref, qh_ref, wq_ref), HD ** -0.5)[0]
        kn_ref[...] = _l2norm_heads(conv_act(k_ref, kh_ref, wk_ref), 1.0)[0]
        va_ref[...] = conv_act(v_ref, vh_ref, wv_ref)
        sel_b, sel_a = _head_selectors()
        ba = ba_ref[...]
        bb_ref[...] = _sigmoid(_dot_hi(ba, sel_b))
        sp, _ = _softplus_and_sigmoid(_dot_hi(ba, sel_a) + dt_ref[...])
        gb_ref[...] = -jnp.exp(al_ref[...]) * sp

    def main(col):
        return pl.BlockSpec((tt, D), lambda i: (i, col))

    def prev(col):
        return pl.BlockSpec((HALO_G, D), lambda i: (jnp.maximum(i * hb - 1, 0), col))

    def wspec(col):
        return pl.BlockSpec((8, D), lambda i: (0, col))

    vec = pl.BlockSpec((1, D), lambda i: (0, 0))
    sds = jax.ShapeDtypeStruct((t, D), f32)
    return pl.pallas_call(
        body, name="gdn_pre_fwd", grid=(t // tt,),
        in_specs=[main(0), main(1), main(2), prev(0), prev(1), prev(2),
                  pl.BlockSpec((tt, HD), lambda i: (i, 0)), wspec(0), wspec(1), wspec(2), vec, vec],
        out_specs=[pl.BlockSpec((tt, D), lambda i: (i, 0))] * 5,
        out_shape=[sds] * 5,
        scratch_shapes=[pltpu.VMEM((tt + HALO_G, D), f32)],
        compiler_params=_cparams("parallel"),
    )(p_qkv, p_qkv, p_qkv, p_qkv, p_qkv, p_qkv, p_ba, cw, cw, cw, alog_b, dt_b)


def _gdn_pre_bwd(p_qkv, p_ba, cw, alog_b, dt_b, dqn, dkn, dva, dgb, dbb, *, tt):
    t = p_qkv.shape[0]
    hb = tt // HALO_G
    n_tiles = t // tt
    last_hb = t // HALO_G - 1
    ne = tt + HALO_G

    def body(q_ref, k_ref, v_ref, qp_ref, kp_ref, vp_ref, qx_ref, kx_ref, vx_ref,
             dq_ref, dk_ref, dv_ref, dqx_ref, dkx_ref, dvx_ref, ba_ref, dgb_ref, dbb_ref,
             wq_ref, wk_ref, wv_ref, al_ref, dt_ref,
             dp_ref, dba_ref, dcw_ref, dad_ref, ext_ref, dpre_ref):
        i = pl.program_id(0)
        first = i == 0
        last = i == n_tiles - 1

        @pl.when(first)
        def _():
            dcw_ref[...] = jnp.zeros_like(dcw_ref)
            dad_ref[...] = jnp.zeros_like(dad_ref)

        def one(x_ref, xp_ref, xx_ref, d_ref, dx_ref, w_ref, col, scale):
            ext_ref[0:HALO_G, :] = jnp.where(first, 0.0, xp_ref[...])
            ext_ref[HALO_G:HALO_G + tt, :] = x_ref[...]
            ext_ref[HALO_G + tt:, :] = xx_ref[...]
            pre = _gdn_short_conv(ext_ref, ne, w_ref)
            act, dact = _silu_and_grad(pre)
            d_out = jnp.concatenate([d_ref[...], jnp.where(last, 0.0, dx_ref[...])], axis=0)
            if scale is None:
                d_act = d_out
            else:
                parts = []
                for h in range(NH):
                    a = act[:, h * HD:(h + 1) * HD]
                    dn = d_out[:, h * HD:(h + 1) * HD]
                    r = lax.rsqrt(jnp.sum(a * a, axis=-1, keepdims=True) + L2_EPS)
                    parts.append(scale * r * (dn - a * (r * r) * jnp.sum(dn * a, axis=-1, keepdims=True)))
                d_act = jnp.concatenate(parts, axis=-1)
            dpre = d_act * dact
            dpre_ref[...] = dpre
            dpre_t = dpre[:tt]
            draw = jnp.zeros((tt, D), f32)
            for k in range(KG):
                draw = draw + w_ref[k:k + 1, :] * dpre_ref[pl.ds(KG - 1 - k, tt), :]
                dcw_ref[k:k + 1, col * D:(col + 1) * D] += jnp.sum(
                    dpre_t * ext_ref[pl.ds(HALO_G - (KG - 1) + k, tt), :], axis=0, keepdims=True)
            dp_ref[:, col * D:(col + 1) * D] = draw.astype(bf16)

        one(q_ref, qp_ref, qx_ref, dq_ref, dqx_ref, wq_ref, 0, HD ** -0.5)
        one(k_ref, kp_ref, kx_ref, dk_ref, dkx_ref, wk_ref, 1, 1.0)
        one(v_ref, vp_ref, vx_ref, dv_ref, dvx_ref, wv_ref, 2, None)

        sel_b, sel_a = _head_selectors()
        ba = ba_ref[...]
        beta = _sigmoid(_dot_hi(ba, sel_b))
        sp, sg = _softplus_and_sigmoid(_dot_hi(ba, sel_a) + dt_ref[...])
        neg_a = -jnp.exp(al_ref[...])
        dgb_v = dgb_ref[...]
        d_bl = dbb_ref[...] * beta * (1.0 - beta)
        d_al = dgb_v * neg_a * sg
        dba_ref[...] = _dot_hi_nt(d_bl, sel_b) + _dot_hi_nt(d_al, sel_a)
        d_alog = jnp.sum(dgb_v * neg_a * sp, axis=0, keepdims=True)
        d_dt = jnp.sum(d_al, axis=0, keepdims=True)
        dad_ref[0:1, :] += _dot_hi_nt(d_alog, sel_b)
        dad_ref[1:2, :] += _dot_hi_nt(d_dt, sel_b)

    def main(col):
        return pl.BlockSpec((tt, D), lambda i: (i, col))

    def prev(col):
        return pl.BlockSpec((HALO_G, D), lambda i: (jnp.maximum(i * hb - 1, 0), col))

    def nxt(col):
        return pl.BlockSpec((HALO_G, D), lambda i: (jnp.minimum((i + 1) * hb, last_hb), col))

    def wspec(col):
        return pl.BlockSpec((8, D), lambda i: (0, col))

    vec = pl.BlockSpec((1, D), lambda i: (0, 0))
    return pl.pallas_call(
        body, name="gdn_pre_bwd", grid=(n_tiles,),
        in_specs=[main(0), main(1), main(2), prev(0), prev(1), prev(2), nxt(0), nxt(1), nxt(2),
                  main(0), main(0), main(0), nxt(0), nxt(0), nxt(0),
                  pl.BlockSpec((tt, HD), lambda i: (i, 0)), main(0), main(0),
                  wspec(0), wspec(1), wspec(2), vec, vec],
        out_specs=[pl.BlockSpec((tt, 3 * D), lambda i: (i, 0)), pl.BlockSpec((tt, HD), lambda i: (i, 0)),
                   pl.BlockSpec((8, 3 * D), lambda i: (0, 0)), pl.BlockSpec((8, HD), lambda i: (0, 0))],
        out_shape=[jax.ShapeDtypeStruct((t, 3 * D), bf16), jax.ShapeDtypeStruct((t, HD), f32),
                   jax.ShapeDtypeStruct((8, 3 * D), f32), jax.ShapeDtypeStruct((8, HD), f32)],
        scratch_shapes=[pltpu.VMEM((HALO_G + tt + HALO_G, D), f32), pltpu.VMEM((ne, D), f32)],
        compiler_params=_cparams("arbitrary"),
    )(p_qkv, p_qkv, p_qkv, p_qkv, p_qkv, p_qkv, p_qkv, p_qkv, p_qkv,
      dqn, dkn, dva, dqn, dkn, dva, p_ba, dgb, dbb, cw, cw, cw, alog_b, dt_b)


def _chunk_fn(q, k, v, gb, bb, s):
    r = lax.broadcasted_iota(jnp.int32, (CH, CH), 0)
    c = lax.broadcasted_iota(jnp.int32, (CH, CH), 1)
    causal = r >= c
    strict = r > c
    ltri = jnp.where(causal, 1.0, 0.0).astype(f32)
    utri = jnp.where(r <= c, 1.0, 0.0).astype(f32)
    eye = jnp.where(r == c, 1.0, 0.0).astype(f32)
    gc = _dot_hi(ltri, gb)
    gc_row = _dot_hi(jnp.ones((CH, CH), f32), gb[:, :CH] * utri)
    diff = gc[:, :CH] - gc_row
    decay = jnp.where(causal, jnp.exp(jnp.where(causal, diff, 0.0)), 0.0)
    kb = k * bb
    egc = jnp.exp(gc)
    m = -jnp.where(strict, _dot_hi_nt(kb, k) * decay, 0.0)
    p = eye + m
    mp = m
    for _ in range(5):
        mp = _dot_hi(mp, mp)
        p = p + _dot_hi(p, mp)
    u = _dot_hi(p, v * bb)
    w = _dot_hi(p, kb * egc)
    intra = jnp.where(causal, _dot_hi_nt(q, k) * decay, 0.0)
    g_last = gc[CH - 1:CH, :]
    k_dec = k * jnp.exp(g_last - gc)
    v_new = u - _dot_hi(w, s)
    o = _dot_hi(q * egc, s) + _dot_hi(intra, v_new)
    s_new = s * jnp.exp(g_last) + _dot_hi_tn(k_dec, v_new)
    return o, s_new


def _gdn_scan_fwd(qn, kn, va, gb, bb, *, tt):
    t = qn.shape[0]
    cpb = tt // CH

    def body(q_ref, k_ref, v_ref, g_ref, b_ref, o_ref, st_ref, s_scr):
        @pl.when(pl.program_id(0) == 0)
        def _():
            s_scr[...] = jnp.zeros_like(s_scr)

        def step(ci, carry):
            rows = pl.ds(pl.multiple_of(ci * CH, CH), CH)
            for h in range(NH):
                cols = slice(h * HD, (h + 1) * HD)
                s = s_scr[h]
                st_ref[ci, h] = s
                o, s_new = _chunk_fn(q_ref[rows, cols], k_ref[rows, cols], v_ref[rows, cols],
                                     g_ref[rows, cols], b_ref[rows, cols], s)
                o_ref[rows, cols] = o
                s_scr[h] = s_new
            return carry

        lax.fori_loop(0, cpb, step, 0)

    blk = pl.BlockSpec((tt, D), lambda i: (i, 0))
    return pl.pallas_call(
        body, name="gdn_scan_fwd", grid=(t // tt,),
        in_specs=[blk] * 5,
        out_specs=[blk, pl.BlockSpec((cpb, NH, HD, HD), lambda i: (i, 0, 0, 0))],
        out_shape=[jax.ShapeDtypeStruct((t, D), f32), jax.ShapeDtypeStruct((t // CH, NH, HD, HD), f32)],
        scratch_shapes=[pltpu.VMEM((NH, HD, HD), f32)],
        compiler_params=_cparams("arbitrary"),
    )(qn, kn, va, gb, bb)


def _gdn_scan_bwd(qn, kn, va, gb, bb, states, do, *, tt):
    t = qn.shape[0]
    nblk = t // tt
    cpb = tt // CH

    def body(q_ref, k_ref, v_ref, g_ref, b_ref, st_ref, do_ref, dq_ref, dk_ref, dv_ref, dg_ref, db_ref, ds_scr):
        @pl.when(pl.program_id(0) == 0)
        def _():
            ds_scr[...] = jnp.zeros_like(ds_scr)

        def step(j, carry):
            ci = cpb - 1 - j
            rows = pl.ds(pl.multiple_of(ci * CH, CH), CH)
            for h in range(NH):
                cols = slice(h * HD, (h + 1) * HD)
                _, vjp = jax.vjp(_chunk_fn, q_ref[rows, cols], k_ref[rows, cols], v_ref[rows, cols],
                                 g_ref[rows, cols], b_ref[rows, cols], st_ref[ci, h])
                dq, dk, dv, dg, db, ds = vjp((do_ref[rows, cols], ds_scr[h]))
                dq_ref[rows, cols] = dq
                dk_ref[rows, cols] = dk
                dv_ref[rows, cols] = dv
                dg_ref[rows, cols] = dg
                db_ref[rows, cols] = db
                ds_scr[h] = ds
            return carry

        lax.fori_loop(0, cpb, step, 0)

    blk = pl.BlockSpec((tt, D), lambda i: (nblk - 1 - i, 0))
    sblk = pl.BlockSpec((cpb, NH, HD, HD), lambda i: (nblk - 1 - i, 0, 0, 0))
    sds = jax.ShapeDtypeStruct((t, D), f32)
    return pl.pallas_call(
        body, name="gdn_scan_bwd", grid=(nblk,),
        in_specs=[blk] * 5 + [sblk, blk],
        out_specs=[blk] * 5, out_shape=[sds] * 5,
        scratch_shapes=[pltpu.VMEM((NH, HD, HD), f32)],
        compiler_params=_cparams("arbitrary"),
    )(qn, kn, va, gb, bb, states, do)


def _rms_heads(o):
    ons, rs = [], []
    for h in range(NH):
        a = o[:, h * HD:(h + 1) * HD]
        r = lax.rsqrt(jnp.mean(a * a, axis=-1, keepdims=True) + RMS_EPS)
        ons.append(a * r)
        rs.append(jnp.broadcast_to(r, a.shape))
    return jnp.concatenate(ons, axis=-1), jnp.concatenate(rs, axis=-1)


def _gdn_post_fwd(o, p_gz, ng_b, *, tt):
    t = o.shape[0]

    def body(o_ref, gz_ref, ng_ref, og_ref):
        on, _ = _rms_heads(o_ref[...])
        z, _ = _silu_and_grad(gz_ref[...])
        og_ref[...] = (on * ng_ref[...] * z).astype(bf16)

    blk = pl.BlockSpec((tt, D), lambda i: (i, 0))
    return pl.pallas_call(
        body, name="gdn_post_fwd", grid=(t // tt,),
        in_specs=[blk, blk, pl.BlockSpec((1, D), lambda i: (0, 0))],
        out_specs=blk, out_shape=jax.ShapeDtypeStruct((t, D), bf16),
        compiler_params=_cparams("parallel"),
    )(o, p_gz, ng_b)


def _gdn_post_bwd(o, p_gz, ng_b, dog, *, tt):
    t = o.shape[0]

    def body(o_ref, gz_ref, ng_ref, dog_ref, do_ref, dgz_ref, dng_ref):
        @pl.when(pl.program_id(0) == 0)
        def _():
            dng_ref[...] = jnp.zeros_like(dng_ref)

        on, r = _rms_heads(o_ref[...])
        z, dz = _silu_and_grad(gz_ref[...])
        dog_v = dog_ref[...]
        ng = ng_ref[...]
        dgz_ref[...] = (dog_v * on * ng * dz).astype(bf16)
        dy = dog_v * z
        dng_all = jnp.sum(dy * on, axis=0, keepdims=True)
        dng = dng_all[:, 0:HD]
        for h in range(1, NH):
            dng = dng + dng_all[:, h * HD:(h + 1) * HD]
        dng_ref[0:1, :] += dng
        don = dy * ng
        prod = don * on
        parts = []
        for h in range(NH):
            sl = slice(h * HD, (h + 1) * HD)
            parts.append(don[:, sl] - on[:, sl] * jnp.mean(prod[:, sl], axis=-1, keepdims=True))
        do_ref[...] = r * jnp.concatenate(parts, axis=-1)

    blk = pl.BlockSpec((tt, D), lambda i: (i, 0))
    return pl.pallas_call(
        body, name="gdn_post_bwd", grid=(t // tt,),
        in_specs=[blk, blk, pl.BlockSpec((1, D), lambda i: (0, 0)), blk],
        out_specs=[blk, blk, pl.BlockSpec((8, HD), lambda i: (0, 0))],
        out_shape=[jax.ShapeDtypeStruct((t, D), f32), jax.ShapeDtypeStruct((t, D), bf16),
                   jax.ShapeDtypeStruct((8, HD), f32)],
        compiler_params=_cparams("arbitrary"),
    )(o, p_gz, ng_b, dog)


def _merge(x, y_conf, y_gdn, p_gates, target, w_o, w_o_t, ln_g, ln_b, *, tt):
    t = x.shape[0]

    def body(x_ref, yc_ref, yg_ref, gc_ref, gg_ref, tg_ref, w_ref, wt_ref, g_ref, b_ref,
             loss_ref, dxd_ref, dyc_ref, dyg_ref, dpg_ref, h_ref, dz_ref, dvec_ref):
        @pl.when(pl.program_id(0) == 0)
        def _():
            loss_ref[...] = jnp.zeros_like(loss_ref)
            dvec_ref[...] = jnp.zeros_like(dvec_ref)

        sc = _sigmoid(gc_ref[...])
        sg = _sigmoid(gg_ref[...])
        yc = yc_ref[...]
        yg = yg_ref[...]
        h = (sc * yc + sg * yg).astype(bf16)
        h_ref[...] = h
        z = DN_ALPHA * x_ref[...] + jnp.dot(h, w_ref[...], preferred_element_type=f32)
        mu = jnp.mean(z, axis=-1, keepdims=True)
        cen = z - mu
        rstd = lax.rsqrt(jnp.mean(cen * cen, axis=-1, keepdims=True) + LN_EPS)
        xhat = cen * rstd
        err = xhat * g_ref[...] + b_ref[...] - tg_ref[...]
        loss_ref[...] += 0.5 / D * jnp.sum(err * err)
        dy = err * (1.0 / D)
        dvec_ref[0:1, :] += jnp.sum(dy * xhat, axis=0, keepdims=True)
        dvec_ref[1:2, :] += jnp.sum(dy, axis=0, keepdims=True)
        dxhat = dy * g_ref[...]
        dz = rstd * (dxhat - jnp.mean(dxhat, axis=-1, keepdims=True)
                     - xhat * jnp.mean(dxhat * xhat, axis=-1, keepdims=True))
        dxd_ref[...] = DN_ALPHA * dz
        dz_b = dz.astype(bf16)
        dz_ref[...] = dz_b
        dh = jnp.dot(dz_b, wt_ref[...], preferred_element_type=f32)
        dyc_ref[...] = (dh * sc).astype(bf16)
        dyg_ref[...] = (dh * sg).astype(bf16)
        dpg_ref[:, 0:D] = (dh * yc * sc * (1.0 - sc)).astype(bf16)
        dpg_ref[:, D:] = (dh * yg * sg * (1.0 - sg)).astype(bf16)

    blk = pl.BlockSpec((tt, D), lambda i: (i, 0))
    wblk = pl.BlockSpec((D, D), lambda i: (0, 0))
    vec = pl.BlockSpec((1, D), lambda i: (0, 0))
    return pl.pallas_call(
        body, name="merge_norm_loss", grid=(t // tt,),
        in_specs=[blk, blk, blk, pl.BlockSpec((tt, D), lambda i: (i, 0)), pl.BlockSpec((tt, D), lambda i: (i, 1)),
                  blk, wblk, wblk, vec, vec],
        out_specs=[pl.BlockSpec((8, HD), lambda i: (0, 0)), blk, blk, blk,
                   pl.BlockSpec((tt, 2 * D), lambda i: (i, 0)), blk, blk, pl.BlockSpec((8, D), lambda i: (0, 0))],
        out_shape=[jax.ShapeDtypeStruct((8, HD), f32), jax.ShapeDtypeStruct((t, D), f32),
                   jax.ShapeDtypeStruct((t, D), bf16), jax.ShapeDtypeStruct((t, D), bf16),
                   jax.ShapeDtypeStruct((t, 2 * D), bf16), jax.ShapeDtypeStruct((t, D), bf16),
                   jax.ShapeDtypeStruct((t, D), bf16), jax.ShapeDtypeStruct((8, D), f32)],
        compiler_params=_cparams("arbitrary"),
    )(x, y_conf, y_gdn, p_gates, p_gates, target, w_o, w_o_t, ln_g, ln_b)


def _place():
    return lax.axis_index("x"), lax.axis_index("y"), lax.axis_index("c")


def _sibling_exchange(a, name):
    def body(a_ref, o_ref, send_sem, recv_sem):
        x, y, c = _place()
        cp = pltpu.make_async_remote_copy(src_ref=a_ref, dst_ref=o_ref, send_sem=send_sem, recv_sem=recv_sem,
                                          device_id=(x, y, 1 - c), device_id_type=MESH)
        cp.start()
        cp.wait()

    return pl.pallas_call(
        body, name=name,
        in_specs=[pl.BlockSpec(memory_space=pl.ANY)], out_specs=pl.BlockSpec(memory_space=pl.ANY),
        out_shape=jax.ShapeDtypeStruct(a.shape, a.dtype),
        scratch_shapes=[pltpu.SemaphoreType.DMA, pltpu.SemaphoreType.DMA],
    )(a)


def _chip_exchange(a, name, scatter):
    blk_shape = a.shape[1:] if scatter else a.shape

    def body(a_ref, o_ref, send_sems, recv_sems, local_sem):
        x, y, c = _place()
        me = 2 * x + y
        peers = [(1 - x, y), (x, 1 - y), (1 - x, 1 - y)]

        def src(j):
            return a_ref.at[j] if scatter else a_ref

        own = pltpu.make_async_copy(src(me), o_ref.at[me], local_sem)
        own.start()
        copies = []
        for n, (px, py) in enumerate(peers):
            cp = pltpu.make_async_remote_copy(
                src_ref=src(2 * px + py), dst_ref=o_ref.at[me], send_sem=send_sems.at[n], recv_sem=recv_sems.at[n],
                device_id=(px, py, c), device_id_type=MESH)
            cp.start()
            copies.append(cp)
        for n, (px, py) in enumerate(peers):
            pltpu.make_async_remote_copy(
                src_ref=src(me), dst_ref=o_ref.at[2 * px + py], send_sem=send_sems.at[n], recv_sem=recv_sems.at[n],
                device_id=(px, py, c), device_id_type=MESH).wait_recv()
        for cp in copies:
            cp.wait_send()
        own.wait()

    return pl.pallas_call(
        body, name=name,
        in_specs=[pl.BlockSpec(memory_space=pl.ANY)], out_specs=pl.BlockSpec(memory_space=pl.ANY),
        out_shape=jax.ShapeDtypeStruct((N_CHIPS,) + tuple(blk_shape), a.dtype),
        scratch_shapes=[pltpu.SemaphoreType.DMA((3,)), pltpu.SemaphoreType.DMA((3,)), pltpu.SemaphoreType.DMA],
    )(a)


def _add4(a, b):
    rows = a.shape[1]

    def body(a_ref, b_ref, o_ref):
        o_ref[...] = a_ref[...] + b_ref[...]

    blk = pl.BlockSpec((1, PACK_TILE, D), lambda j, i: (j, i, 0))
    return pl.pallas_call(
        body, name="grad_pair_sum", grid=(N_CHIPS, rows // PACK_TILE),
        in_specs=[blk, blk], out_specs=blk, out_shape=jax.ShapeDtypeStruct(a.shape, f32),
        compiler_params=_cparams("parallel", "parallel"),
    )(a, b)


def _sum_slots(a):
    rows = a.shape[1]

    def body(a_ref, o_ref):
        o_ref[...] = ((a_ref[0] + a_ref[1]) + a_ref[2]) + a_ref[3]

    return pl.pallas_call(
        body, name="grad_chip_sum", grid=(rows // PACK_TILE,),
        in_specs=[pl.BlockSpec((N_CHIPS, PACK_TILE, D), lambda i: (0, i, 0))],
        out_specs=pl.BlockSpec((PACK_TILE, D), lambda i: (i, 0)),
        out_shape=jax.ShapeDtypeStruct((rows, D), f32),
        compiler_params=_cparams("parallel"),
    )(a)


def _adamw(w, g, m, v):
    rows = w.shape[0]
    c1 = 1.0 / (1.0 - ADAM_B1 ** ADAM_STEP)
    c2 = 1.0 / (1.0 - ADAM_B2 ** ADAM_STEP)

    def body(w_ref, g_ref, m_ref, v_ref, d_ref, mo_ref, vo_ref):
        g_v = g_ref[...]
        m_new = ADAM_B1 * m_ref[...] + (1.0 - ADAM_B1) * g_v
        v_new = ADAM_B2 * v_ref[...] + (1.0 - ADAM_B2) * (g_v * g_v)
        mo_ref[...] = m_new
        vo_ref[...] = v_new
        d_ref[...] = -ADAM_LR * ((m_new * c1) / (jnp.sqrt(v_new * c2) + ADAM_EPS) + ADAM_WD * w_ref[...])

    blk = pl.BlockSpec((PACK_TILE, D), lambda i: (i, 0))
    sds = jax.ShapeDtypeStruct((rows, D), f32)
    return pl.pallas_call(
        body, name="adamw", grid=(rows // PACK_TILE,),
        in_specs=[blk] * 4, out_specs=[blk] * 3, out_shape=[sds] * 3,
        compiler_params=_cparams("parallel"),
    )(w, g, m, v)


_VEC_NAMES = ("conf_dw_b", "conf_ln_g", "conf_ln_b", "post_ln_g", "post_ln_b")
R_SQ = W_IN_BLK
R_DW = R_SQ + 3 * SQ_BLK
R_CW = R_DW + 8
R_VEC = R_CW + 4
R_SMALL = R_VEC + 5


def _pack_shard(w_in, conf_w_out, gdn_w_out, w_o, conf_dw_w, gdn_conv_w, vecs, a_log, dt_bias, norm_g):
    dw = jnp.pad(conf_dw_w.reshape(-1), (0, 8 * D - KC * SQ_BLK)).reshape(8, D)
    cw = jnp.pad(gdn_conv_w.reshape(-1), (0, D)).reshape(4, D)
    small = jnp.pad(jnp.concatenate([a_log, dt_bias, norm_g]), (0, D - 2 * NH - HD)).reshape(1, D)
    parts = [w_in.reshape(W_IN_BLK, D), conf_w_out, gdn_w_out, w_o, dw, cw] + [v.reshape(1, D) for v in vecs] + [small]
    packed = jnp.concatenate(parts, axis=0)
    return jnp.pad(packed, ((0, PACK_ROWS - packed.shape[0]), (0, 0)))


def _unpack_shard(p):
    w_in = p[0:W_IN_BLK].reshape(D, W_IN_BLK)
    conf_w_out = p[R_SQ:R_SQ + SQ_BLK]
    gdn_w_out = p[R_SQ + SQ_BLK:R_SQ + 2 * SQ_BLK]
    w_o = p[R_SQ + 2 * SQ_BLK:R_DW]
    conf_dw_w = p[R_DW:R_DW + 8].reshape(-1)[:KC * SQ_BLK].reshape(KC, SQ_BLK)
    gdn_conv_w = p[R_CW:R_CW + 3].reshape(KG, 3 * SQ_BLK)
    vecs = [p[R_VEC + n] for n in range(5)]
    small = p[R_SMALL]
    return dict(w_in=w_in, conf_w_out=conf_w_out, gdn_w_out=gdn_w_out, w_o=w_o, conf_dw_w=conf_dw_w,
                gdn_conv_w=gdn_conv_w, conf_dw_b=vecs[0], conf_ln_g=vecs[1], conf_ln_b=vecs[2],
                post_ln_g=vecs[3], post_ln_b=vecs[4], gdn_A_log=small[0:NH], gdn_dt_bias=small[NH:2 * NH],
                gdn_norm_g=small[2 * NH:2 * NH + HD])


_WEIGHT_ORDER = ("w_in", "conf_dw_w", "conf_dw_b", "conf_ln_g", "conf_ln_b", "conf_w_out", "gdn_conv_w",
                 "gdn_A_log", "gdn_dt_bias", "gdn_norm_g", "gdn_w_out", "w_o", "post_ln_g", "post_ln_b")


def _gather_weights(w_in, conf_w_out, gdn_w_out, w_o, conf_dw_w, gdn_conv_w):
    c = lax.axis_index("c")
    big = jnp.concatenate([w_in.reshape(W_IN_BLK, D), conf_w_out, gdn_w_out, w_o], axis=0).astype(bf16)
    big = jnp.pad(big, ((0, GATHER_ROWS - big.shape[0]), (0, 0)))
    half = big.shape[0] // 2
    mine = lax.dynamic_slice_in_dim(big, c * half, half, axis=0)
    got = _chip_exchange(mine, "weight_gather_chips", scatter=False)
    other = _sibling_exchange(got, "weight_gather_sibling")
    lo = jnp.where(c == 0, got, other)
    hi = jnp.where(c == 0, other, got)
    full = jnp.concatenate([lo, hi], axis=1)
    w_in_full = full[:, 0:W_IN_BLK].reshape(N_CHIPS, D, W_IN_BLK).transpose(1, 0, 2).reshape(D, W_IN_COLS)
    sq = [full[:, R_SQ + n * SQ_BLK:R_SQ + (n + 1) * SQ_BLK].reshape(D, D) for n in range(3)]
    small = jnp.concatenate([jnp.pad(conf_dw_w.reshape(-1), (0, 8 * D - KC * SQ_BLK)).reshape(8, D),
                             jnp.pad(gdn_conv_w.reshape(-1), (0, 5 * D)).reshape(8, D)], axis=0)
    small_all = _chip_exchange(small, "conv_weight_gather", scatter=False)
    dw_full = small_all[:, 0:8].reshape(N_CHIPS, 8 * D)[:, :KC * SQ_BLK].reshape(N_CHIPS, KC, SQ_BLK)
    dw_full = dw_full.transpose(1, 0, 2).reshape(KC, D)
    cw_full = small_all[:, 8:11].reshape(N_CHIPS, KG, 3 * SQ_BLK).transpose(1, 0, 2).reshape(KG, 3 * D)
    return w_in_full, sq[0], sq[1], sq[2], dw_full, cw_full


def _reduce_scatter(g_all):
    c = lax.axis_index("c")
    keep = lax.dynamic_slice_in_dim(g_all, c * HALF_ROWS, HALF_ROWS, axis=1)
    give = lax.dynamic_slice_in_dim(g_all, (1 - c) * HALF_ROWS, HALF_ROWS, axis=1)
    pair = _add4(keep, _sibling_exchange(give, "grad_sibling_halves"))
    tot = _sum_slots(_chip_exchange(pair, "grad_chip_scatter", scatter=True))
    other = _sibling_exchange(tot, "grad_sibling_result")
    lo = jnp.where(c == 0, tot, other)
    hi = jnp.where(c == 0, other, tot)
    return jnp.concatenate([lo, hi], axis=0)


def _by_chip(full, cols):
    return full.reshape(full.shape[0], N_CHIPS, cols).transpose(1, 0, 2)


def _local_step(x2, tgt, w_full, wc_out, wg_out, wo_full, dw_full, cw_full, conf_dw_b, conf_ln_g, conf_ln_b,
                gdn_A_log, gdn_dt_bias, gdn_norm_g, post_ln_g, post_ln_b):
    t = x2.shape[0]
    tt = min(TOKEN_TILE, t)
    tm = min(512, t)

    w_conv, w_qkv, w_gz = w_full[:, 0:3 * D], w_full[:, 3 * D:6 * D], w_full[:, 6 * D:7 * D]
    w_gates = w_full[:, 7 * D + 2 * NH:]
    dw_pad = jnp.pad(dw_full, ((0, HALO_C - KC), (0, 0)))
    cw_pad = jnp.pad(cw_full, ((0, 8 - KG), (0, 0)))
    row = lambda v: v.reshape(1, D)
    alog_b = row(jnp.repeat(gdn_A_log, HD))
    dt_b = row(jnp.repeat(gdn_dt_bias, HD))
    ng_b = row(jnp.tile(gdn_norm_g, NH))
    w_ba = jnp.pad(w_full[:, 7 * D:7 * D + 2 * NH], ((0, 0), (0, HD - 2 * NH)))

    x_b = x2.astype(bf16)
    x_t = x_b.T

    p_conv = _mm_multi([x_b], [w_conv], out_dtype=f32, tm=tm, tn=1024, name="proj_conv")
    p_qkv = _mm_multi([x_b], [w_qkv], out_dtype=f32, tm=tm, tn=1024, name="proj_qkv")
    p_gz = _mm_multi([x_b], [w_gz], out_dtype=f32, tm=tm, tn=1024, name="proj_gz")
    p_gates = _mm_multi([x_b], [w_gates], out_dtype=f32, tm=tm, tn=1024, name="proj_gates")
    p_ba = _mm_multi([x_b], [w_ba], out_dtype=f32, tm=tm, tn=HD, name="proj_ba")

    u = _conv_fwd(p_conv, dw_pad, row(conf_dw_b), row(conf_ln_g), row(conf_ln_b), tt=tt)
    y_conf = _mm_multi([u], [wc_out], out_dtype=f32, tm=tm, tn=1024, name="conf_out")

    qn, kn, va, gb, bb = _gdn_pre_fwd(p_qkv, p_ba, cw_pad, alog_b, dt_b, tt=tt)
    o, states = _gdn_scan_fwd(qn, kn, va, gb, bb, tt=tt)
    og = _gdn_post_fwd(o, p_gz, ng_b, tt=tt)
    y_gdn = _mm_multi([og], [wg_out], out_dtype=f32, tm=tm, tn=1024, name="gdn_out")

    loss_blk, dxd, dyc, dyg, dp_gates, h, dz, dpost = _merge(
        x2, y_conf, y_gdn, p_gates, tgt, wo_full, wo_full.T, row(post_ln_g), row(post_ln_b), tt=tt)

    d_wo = _mm_kloop(h.T, dz, tm=tm, tn=1024, tk=min(1024, t), name="grad_w_o")
    du = _mm_multi([dyc], [wc_out.T], out_dtype=f32, tm=tm, tn=1024, name="conf_out_bwd")
    d_wc = _mm_kloop(u.T, dyc, tm=tm, tn=1024, tk=min(1024, t), name="grad_conf_w_out")
    dog = _mm_multi([dyg], [wg_out.T], out_dtype=f32, tm=tm, tn=1024, name="gdn_out_bwd")
    d_wg = _mm_kloop(og.T, dyg, tm=tm, tn=1024, tk=min(1024, t), name="grad_gdn_w_out")

    dp_conv, d_dww, dconv_vec = _conv_bwd(p_conv, du, dw_pad, row(conf_dw_b), row(conf_ln_g), row(conf_ln_b), tt=tt)

    do, dp_gz, dng = _gdn_post_bwd(o, p_gz, ng_b, dog, tt=tt)
    dqn, dkn, dva, dgb, dbb = _gdn_scan_bwd(qn, kn, va, gb, bb, states, do, tt=tt)
    dp_qkv, dp_ba, d_cw, d_ad = _gdn_pre_bwd(p_qkv, p_ba, cw_pad, alog_b, dt_b, dqn, dkn, dva, dgb, dbb, tt=tt)
    dp_ba_b = dp_ba.astype(bf16)

    grad_x = _mm_multi(
        [dp_conv, dp_qkv, dp_gz, dp_gates, dp_ba_b],
        [w_conv.T, w_qkv.T, w_gz.T, w_gates.T, w_ba.T], dxd,
        out_dtype=f32, tm=min(256, t), tn=512, name="grad_x")

    tk = min(1024, t)
    d_w_conv = _mm_kloop(x_t, dp_conv, tm=tm, tn=1024, tk=tk, name="grad_w_in_conv")
    d_w_qkv = _mm_kloop(x_t, dp_qkv, tm=tm, tn=1024, tk=tk, name="grad_w_in_qkv")
    d_w_gz = _mm_kloop(x_t, dp_gz, tm=tm, tn=1024, tk=tk, name="grad_w_in_gz")
    d_w_gates = _mm_kloop(x_t, dp_gates, tm=tm, tn=1024, tk=tk, name="grad_w_in_gates")
    d_w_ba = _mm_kloop(x_t, dp_ba_b, tm=tm, tn=HD, tk=tk, name="grad_w_in_ba")
    d_w_in = jnp.concatenate([d_w_conv, d_w_qkv, d_w_gz, d_w_ba[:, :2 * NH], d_w_gates], axis=1)

    return (loss_blk[0, 0], grad_x, d_w_in, d_wc, d_wg, d_wo, d_dww, d_cw, dconv_vec, dpost, d_ad, dng)


def kernel(x, w_in, conf_dw_w, conf_dw_b, conf_ln_g, conf_ln_b, conf_w_out, gdn_conv_w, gdn_A_log, gdn_dt_bias, gdn_norm_g, gdn_w_out, w_o, post_ln_g, post_ln_b, loss_target, m_w_in, m_conf_dw_w, m_conf_dw_b, m_conf_ln_g, m_conf_ln_b, m_conf_w_out, m_gdn_conv_w, m_gdn_A_log, m_gdn_dt_bias, m_gdn_norm_g, m_gdn_w_out, m_w_o, m_post_ln_g, m_post_ln_b, v_w_in, v_conf_dw_w, v_conf_dw_b, v_conf_ln_g, v_conf_ln_b, v_conf_w_out, v_gdn_conv_w, v_gdn_A_log, v_gdn_dt_bias, v_gdn_norm_g, v_gdn_w_out, v_w_o, v_post_ln_g, v_post_ln_b):
    x2 = x.reshape(x.shape[-2], D)
    tgt = loss_target.reshape(x2.shape)
    w_full, wc_out, wg_out, wo_full, dw_full, cw_full = _gather_weights(
        w_in, conf_w_out, gdn_w_out, w_o, conf_dw_w, gdn_conv_w)
    (loss_part, grad_x, d_w_in, d_wc, d_wg, d_wo, d_dww, d_cw, dconv_vec, dpost, d_ad, dng) = _local_step(
        x2, tgt, w_full, wc_out, wg_out, wo_full, dw_full, cw_full, conf_dw_b, conf_ln_g, conf_ln_b,
        gdn_A_log, gdn_dt_bias, gdn_norm_g, post_ln_g, post_ln_b)
    loss = lax.psum(loss_part, ("x", "y", "c"))

    win_c = _by_chip(d_w_in, W_IN_BLK)
    dww_c = _by_chip(d_dww[:KC], SQ_BLK)
    dcw_c = _by_chip(d_cw[:KG], 3 * SQ_BLK)
    vecs = [dconv_vec[0], dconv_vec[1], dconv_vec[2], dpost[0], dpost[1]]
    g_all = jnp.stack([
        _pack_shard(win_c[j], d_wc[j * SQ_BLK:(j + 1) * SQ_BLK], d_wg[j * SQ_BLK:(j + 1) * SQ_BLK],
                    d_wo[j * SQ_BLK:(j + 1) * SQ_BLK], dww_c[j], dcw_c[j], vecs,
                    d_ad[0, :NH], d_ad[1, :NH], dng[0])
        for j in range(N_CHIPS)])
    g_sum = _reduce_scatter(g_all)

    w_p = _pack_shard(w_in, conf_w_out, gdn_w_out, w_o, conf_dw_w, gdn_conv_w,
                      [conf_dw_b, conf_ln_g, conf_ln_b, post_ln_g, post_ln_b], gdn_A_log, gdn_dt_bias, gdn_norm_g)
    m_p = _pack_shard(m_w_in, m_conf_w_out, m_gdn_w_out, m_w_o, m_conf_dw_w, m_gdn_conv_w,
                      [m_conf_dw_b, m_conf_ln_g, m_conf_ln_b, m_post_ln_g, m_post_ln_b],
                      m_gdn_A_log, m_gdn_dt_bias, m_gdn_norm_g)
    v_p = _pack_shard(v_w_in, v_conf_w_out, v_gdn_w_out, v_w_o, v_conf_dw_w, v_gdn_conv_w,
                      [v_conf_dw_b, v_conf_ln_g, v_conf_ln_b, v_post_ln_g, v_post_ln_b],
                      v_gdn_A_log, v_gdn_dt_bias, v_gdn_norm_g)
    delta_p, m_new_p, v_new_p = _adamw(w_p, g_sum, m_p, v_p)

    grads, deltas, new_m, new_v = (_unpack_shard(p) for p in (g_sum, delta_p, m_new_p, v_new_p))
    out = [loss, grad_x.reshape(x.shape)]
    for d in (grads, deltas, new_m, new_v):
        out += [d[n] for n in _WEIGHT_ORDER]
    return tuple(out)
```

```python
import functools

import jax
import jax.numpy as jnp
from jax import lax
from jax.experimental import pallas as pl
from jax.experimental.pallas import tpu as pltpu

f32 = jnp.float32
bf16 = jnp.bfloat16
HI = lax.Precision.HIGHEST
MESH = pl.DeviceIdType.MESH

D = 1024
NH = 8
HD = 128
CH = 64
KC = 31
KG = 4
HALO_C = 32
HALO_G = 8
LN_EPS = 1e-5
RMS_EPS = 1e-6
L2_EPS = 1e-6
DN_ALPHA = 2.0 ** 0.25
N_CHIPS = 4
W_IN_COLS = 9232
W_IN_BLK = W_IN_COLS // N_CHIPS
SQ_BLK = D // N_CHIPS
VMEM_LIMIT = 52 * 1024 * 1024
TOKEN_TILE = 256

ADAM_LR = 0.001
ADAM_B1 = 0.9
ADAM_B2 = 0.999
ADAM_EPS = 1e-08
ADAM_WD = 0.01
ADAM_STEP = 10


def _sigmoid(x):
    return 1.0 / (1.0 + jnp.exp(-x))


def _silu_and_grad(x):
    s = _sigmoid(x)
    return x * s, s * (1.0 + x * (1.0 - s))


def _cparams(*sem):
    return pltpu.CompilerParams(dimension_semantics=sem, vmem_limit_bytes=VMEM_LIMIT)


def _mm_multi(a_list, b_list, addend=None, *, out_dtype, tm, tn, name, precision=None):
    n_pairs = len(a_list)
    m = a_list[0].shape[0]
    n = b_list[0].shape[1]
    has_add = addend is not None

    def body(*refs):
        a_refs = refs[:n_pairs]
        b_refs = refs[n_pairs:2 * n_pairs]
        o_ref = refs[-1]
        acc = None
        for a_ref, b_ref in zip(a_refs, b_refs):
            if precision is None:
                p = jnp.dot(a_ref[...].astype(bf16), b_ref[...].astype(bf16), preferred_element_type=f32)
            else:
                p = jnp.dot(a_ref[...], b_ref[...], preferred_element_type=f32, precision=precision)
            acc = p if acc is None else acc + p
        if has_add:
            acc = acc + refs[2 * n_pairs][...]
        o_ref[...] = acc.astype(out_dtype)

    in_specs = [pl.BlockSpec((tm, a.shape[1]), lambda j, i: (i, 0)) for a in a_list]
    in_specs += [pl.BlockSpec((b.shape[0], tn), lambda j, i: (0, j)) for b in b_list]
    args = list(a_list) + list(b_list)
    if has_add:
        in_specs.append(pl.BlockSpec((tm, tn), lambda j, i: (i, j)))
        args.append(addend)
    return pl.pallas_call(
        body, name=name, grid=(n // tn, m // tm),
        in_specs=in_specs, out_specs=pl.BlockSpec((tm, tn), lambda j, i: (i, j)),
        out_shape=jax.ShapeDtypeStruct((m, n), out_dtype),
        compiler_params=_cparams("parallel", "parallel"),
    )(*args)


def _mm_kloop(a, b, *, tm, tn, tk, name, precision=None):
    m, k = a.shape
    n = b.shape[1]
    nk = k // tk

    def body(a_ref, b_ref, o_ref):
        @pl.when(pl.program_id(2) == 0)
        def _():
            o_ref[...] = jnp.zeros_like(o_ref)
        if precision is None:
            o_ref[...] += jnp.dot(a_ref[...].astype(bf16), b_ref[...].astype(bf16), preferred_element_type=f32)
        else:
            o_ref[...] += jnp.dot(a_ref[...], b_ref[...], preferred_element_type=f32, precision=precision)

    return pl.pallas_call(
        body, name=name, grid=(n // tn, m // tm, nk),
        in_specs=[pl.BlockSpec((tm, tk), lambda j, i, kk: (i, kk)), pl.BlockSpec((tk, tn), lambda j, i, kk: (kk, j))],
        out_specs=pl.BlockSpec((tm, tn), lambda j, i, kk: (i, j)),
        out_shape=jax.ShapeDtypeStruct((m, n), f32),
        compiler_params=_cparams("parallel", "parallel", "arbitrary"),
    )(a, b)


def _conv_branch_core(ext_ref, n_rows, cz, w_ref, b_ref, g_ref, bb_ref):
    acc = jnp.broadcast_to(b_ref[...], (n_rows, D))
    for k in range(KC):
        acc = acc + w_ref[k:k + 1, :] * ext_ref[pl.ds(HALO_C - (KC - 1) + k, n_rows), :]
    mu = jnp.mean(acc, axis=-1, keepdims=True)
    cen = acc - mu
    var = jnp.mean(cen * cen, axis=-1, keepdims=True)
    rstd = lax.rsqrt(var + LN_EPS)
    xhat = cen * rstd
    ln = xhat * g_ref[...] + bb_ref[...]
    s, ds = _silu_and_grad(ln)
    zc, dzc = _silu_and_grad(cz)
    return xhat, rstd, s, ds, zc, dzc


def _conv_fwd(p_conv, dw_w, dw_b, ln_g, ln_b, *, tt):
    t = p_conv.shape[0]
    hb = tt // HALO_C

    def body(cv_ref, cg_ref, cz_ref, cvh_ref, cgh_ref, w_ref, b_ref, g_ref, bb_ref, u_ref, ext_ref):
        first = pl.program_id(0) == 0
        halo = cvh_ref[...] * _sigmoid(cgh_ref[...])
        ext_ref[0:HALO_C, :] = jnp.where(first, 0.0, halo)
        ext_ref[HALO_C:, :] = cv_ref[...] * _sigmoid(cg_ref[...])
        _, _, s, _, zc, _ = _conv_branch_core(ext_ref, tt, cz_ref[...], w_ref, b_ref, g_ref, bb_ref)
        u_ref[...] = (s * zc).astype(bf16)

    def main(col):
        return pl.BlockSpec((tt, D), lambda i: (i, col))

    def prev(col):
        return pl.BlockSpec((HALO_C, D), lambda i: (jnp.maximum(i * hb - 1, 0), col))

    vec = pl.BlockSpec((1, D), lambda i: (0, 0))
    return pl.pallas_call(
        body, name="conv_fwd", grid=(t // tt,),
        in_specs=[main(0), main(1), main(2), prev(0), prev(1),
                  pl.BlockSpec((HALO_C, D), lambda i: (0, 0)), vec, vec, vec],
        out_specs=pl.BlockSpec((tt, D), lambda i: (i, 0)),
        out_shape=jax.ShapeDtypeStruct((t, D), bf16),
        scratch_shapes=[pltpu.VMEM((tt + HALO_C, D), f32)],
        compiler_params=_cparams("parallel"),
    )(p_conv, p_conv, p_conv, p_conv, p_conv, dw_w, dw_b, ln_g, ln_b)


def _conv_bwd(p_conv, du, dw_w, dw_b, ln_g, ln_b, *, tt):
    t = p_conv.shape[0]
    hb = tt // HALO_C
    n_tiles = t // tt
    last_hb = t // HALO_C - 1
    ne = tt + HALO_C

    def body(cv_ref, cg_ref, cz_ref, du_ref, cvp_ref, cgp_ref, cvn_ref, cgn_ref, czn_ref, dun_ref,
             w_ref, b_ref, g_ref, bb_ref, dp_ref, dww_ref, dvec_ref, ext_ref, da1_ref):
        i = pl.program_id(0)
        first = i == 0
        last = i == n_tiles - 1

        @pl.when(first)
        def _():
            dww_ref[...] = jnp.zeros_like(dww_ref)
            dvec_ref[...] = jnp.zeros_like(dvec_ref)

        sig = _sigmoid(cg_ref[...])
        ext_ref[0:HALO_C, :] = jnp.where(first, 0.0, cvp_ref[...] * _sigmoid(cgp_ref[...]))
        ext_ref[HALO_C:HALO_C + tt, :] = cv_ref[...] * sig
        ext_ref[HALO_C + tt:, :] = cvn_ref[...] * _sigmoid(cgn_ref[...])
        cz = jnp.concatenate([cz_ref[...], czn_ref[...]], axis=0)
        du_all = jnp.concatenate([du_ref[...], jnp.where(last, 0.0, dun_ref[...])], axis=0)
        xhat, rstd, s, ds, zc, dzc = _conv_branch_core(ext_ref, ne, cz, w_ref, b_ref, g_ref, bb_ref)
        dln = du_all * zc * ds
        dxhat = dln * g_ref[...]
        da1 = rstd * (dxhat - jnp.mean(dxhat, axis=-1, keepdims=True)
                      - xhat * jnp.mean(dxhat * xhat, axis=-1, keepdims=True))
        da1_ref[...] = da1
        dcz = (du_all * s * dzc)[:tt]
        dvec_ref[0:1, :] += jnp.sum(da1[:tt], axis=0, keepdims=True)
        dvec_ref[1:2, :] += jnp.sum((dln * xhat)[:tt], axis=0, keepdims=True)
        dvec_ref[2:3, :] += jnp.sum(dln[:tt], axis=0, keepdims=True)
        da1_t = da1[:tt]
        da0 = jnp.zeros((tt, D), f32)
        for k in range(KC):
            da0 = da0 + w_ref[k:k + 1, :] * da1_ref[pl.ds(KC - 1 - k, tt), :]
            dww_ref[k:k + 1, :] += jnp.sum(
                da1_t * ext_ref[pl.ds(HALO_C - (KC - 1) + k, tt), :], axis=0, keepdims=True)
        cv = cv_ref[...]
        dp_ref[:, 0:D] = (da0 * sig).astype(bf16)
        dp_ref[:, D:2 * D] = (da0 * cv * sig * (1.0 - sig)).astype(bf16)
        dp_ref[:, 2 * D:] = dcz.astype(bf16)

    def main(col):
        return pl.BlockSpec((tt, D), lambda i: (i, col))

    def prev(col):
        return pl.BlockSpec((HALO_C, D), lambda i: (jnp.maximum(i * hb - 1, 0), col))

    def nxt(col):
        return pl.BlockSpec((HALO_C, D), lambda i: (jnp.minimum((i + 1) * hb, last_hb), col))

    vec = pl.BlockSpec((1, D), lambda i: (0, 0))
    return pl.pallas_call(
        body, name="conv_bwd", grid=(n_tiles,),
        in_specs=[main(0), main(1), main(2), main(0), prev(0), prev(1), nxt(0), nxt(1), nxt(2), nxt(0),
                  pl.BlockSpec((HALO_C, D), lambda i: (0, 0)), vec, vec, vec],
        out_specs=[pl.BlockSpec((tt, 3 * D), lambda i: (i, 0)),
                   pl.BlockSpec((HALO_C, D), lambda i: (0, 0)),
                   pl.BlockSpec((8, D), lambda i: (0, 0))],
        out_shape=[jax.ShapeDtypeStruct((t, 3 * D), bf16), jax.ShapeDtypeStruct((HALO_C, D), f32),
                   jax.ShapeDtypeStruct((8, D), f32)],
        scratch_shapes=[pltpu.VMEM((HALO_C + tt + HALO_C, D), f32), pltpu.VMEM((ne, D), f32)],
        compiler_params=_cparams("arbitrary"),
    )(p_conv, p_conv, p_conv, du, p_conv, p_conv, p_conv, p_conv, p_conv, du, dw_w, dw_b, ln_g, ln_b)


def _head_selectors():
    r = lax.broadcasted_iota(jnp.int32, (HD, D), 0)
    c = lax.broadcasted_iota(jnp.int32, (HD, D), 1) // HD
    return jnp.where(r == c, 1.0, 0.0).astype(f32), jnp.where(r == c + NH, 1.0, 0.0).astype(f32)


def _dot_hi(a, b):
    return lax.dot_general(a, b, (((1,), (0,)), ((), ())), precision=HI, preferred_element_type=f32)


def _dot_hi_nt(a, b):
    return lax.dot_general(a, b, (((1,), (1,)), ((), ())), precision=HI, preferred_element_type=f32)


def _dot_hi_tn(a, b):
    return lax.dot_general(a, b, (((0,), (0,)), ((), ())), precision=HI, preferred_element_type=f32)


def _chunk_tri(n, lower):
    r = lax.broadcasted_iota(jnp.int32, (n, n), 0)
    c = lax.broadcasted_iota(jnp.int32, (n, n), 1)
    tri = (r >= c) if lower else (r <= c)
    return jnp.where(tri & (r // CH == c // CH), 1.0, 0.0).astype(f32)


def _softplus_and_sigmoid(x):
    e = jnp.exp(-jnp.abs(x))
    log1p = jnp.where(e < 1e-2, e * (1.0 - e * (0.5 - e * (1.0 / 3.0 - 0.25 * e))), jnp.log(1.0 + e))
    return jnp.maximum(x, 0.0) + log1p, _sigmoid(x)


def _gdn_short_conv(ext_ref, n_rows, w_ref):
    pre = jnp.zeros((n_rows, D), f32)
    for k in range(KG):
        pre = pre + w_ref[k:k + 1, :] * ext_ref[pl.ds(HALO_G - (KG - 1) + k, n_rows), :]
    return pre


def _l2norm_heads(act, scale):
    outs, rs = [], []
    for h in range(NH):
        a = act[:, h * HD:(h + 1) * HD]
        r = lax.rsqrt(jnp.sum(a * a, axis=-1, keepdims=True) + L2_EPS)
        outs.append(a * (r * scale))
        rs.append(jnp.broadcast_to(r, a.shape))
    return jnp.concatenate(outs, axis=-1), jnp.concatenate(rs, axis=-1)


def _gdn_pre_fwd(p_qkv, p_ba, cw, alog_b, dt_b, *, tt):
    t = p_qkv.shape[0]
    hb = tt // HALO_G

    def body(q_ref, k_ref, v_ref, qh_ref, kh_ref, vh_ref, ba_ref, wq_ref, wk_ref, wv_ref, al_ref, dt_ref,
             qn_ref, kn_ref, va_ref, gb_ref, bb_ref, ext_ref):
        first = pl.program_id(0) == 0

        def conv_act(x_ref, xh_ref, w_ref):
            ext_ref[0:HALO_G, :] = jnp.where(first, 0.0, xh_ref[...])
            ext_ref[HALO_G:, :] = x_ref[...]
            pre = _gdn_short_conv(ext_ref, tt, w_ref)
            return pre * _sigmoid(pre)

        qn_ref[...] = _l2norm_heads(conv_act(q_ref, qh_ref, wq_ref), HD ** -0.5)[0]
        kn_ref[...] = _l2norm_heads(conv_act(k_ref, kh_ref, wk_ref), 1.0)[0]
        va_ref[...] = conv_act(v_ref, vh_ref, wv_ref)
        sel_b, sel_a = _head_selectors()
        ba = ba_ref[...]
        bb_ref[...] = _sigmoid(_dot_hi(ba, sel_b))
        sp, _ = _softplus_and_sigmoid(_dot_hi(ba, sel_a) + dt_ref[...])
        gb_ref[...] = _dot_hi(_chunk_tri(tt, lower=True), -jnp.exp(al_ref[...]) * sp)

    def main(col):
        return pl.BlockSpec((tt, D), lambda i: (i, col))

    def prev(col):
        return pl.BlockSpec((HALO_G, D), lambda i: (jnp.maximum(i * hb - 1, 0), col))

    def wspec(col):
        return pl.BlockSpec((8, D), lambda i: (0, col))

    vec = pl.BlockSpec((1, D), lambda i: (0, 0))
    sds = jax.ShapeDtypeStruct((t, D), f32)
    return pl.pallas_call(
        body, name="gdn_pre_fwd", grid=(t // tt,),
        in_specs=[main(0), main(1), main(2), prev(0), prev(1), prev(2),
                  pl.BlockSpec((tt, HD), lambda i: (i, 0)), wspec(0), wspec(1), wspec(2), vec, vec],
        out_specs=[pl.BlockSpec((tt, D), lambda i: (i, 0))] * 5,
        out_shape=[sds] * 5,
        scratch_shapes=[pltpu.VMEM((tt + HALO_G, D), f32)],
        compiler_params=_cparams("parallel"),
    )(p_qkv, p_qkv, p_qkv, p_qkv, p_qkv, p_qkv, p_ba, cw, cw, cw, alog_b, dt_b)


def _gdn_pre_bwd(p_qkv, p_ba, cw, alog_b, dt_b, dqn, dkn, dva, dgb, dbb, *, tt):
    t = p_qkv.shape[0]
    hb = tt // HALO_G
    n_tiles = t // tt
    last_hb = t // HALO_G - 1
    ne = tt + HALO_G

    def body(q_ref, k_ref, v_ref, qp_ref, kp_ref, vp_ref, qx_ref, kx_ref, vx_ref,
             dq_ref, dk_ref, dv_ref, dqx_ref, dkx_ref, dvx_ref, ba_ref, dgb_ref, dbb_ref,
             wq_ref, wk_ref, wv_ref, al_ref, dt_ref,
             dp_ref, dba_ref, dcw_ref, dad_ref, ext_ref, dpre_ref):
        i = pl.program_id(0)
        first = i == 0
        last = i == n_tiles - 1

        @pl.when(first)
        def _():
            dcw_ref[...] = jnp.zeros_like(dcw_ref)
            dad_ref[...] = jnp.zeros_like(dad_ref)

        def one(x_ref, xp_ref, xx_ref, d_ref, dx_ref, w_ref, col, scale):
            ext_ref[0:HALO_G, :] = jnp.where(first, 0.0, xp_ref[...])
            ext_ref[HALO_G:HALO_G + tt, :] = x_ref[...]
            ext_ref[HALO_G + tt:, :] = xx_ref[...]
            pre = _gdn_short_conv(ext_ref, ne, w_ref)
            act, dact = _silu_and_grad(pre)
            d_out = jnp.concatenate([d_ref[...], jnp.where(last, 0.0, dx_ref[...])], axis=0)
            if scale is None:
                d_act = d_out
            else:
                parts = []
                for h in range(NH):
                    a = act[:, h * HD:(h + 1) * HD]
                    dn = d_out[:, h * HD:(h + 1) * HD]
                    r = lax.rsqrt(jnp.sum(a * a, axis=-1, keepdims=True) + L2_EPS)
                    parts.append(scale * r * (dn - a * (r * r) * jnp.sum(dn * a, axis=-1, keepdims=True)))
                d_act = jnp.concatenate(parts, axis=-1)
            dpre = d_act * dact
            dpre_ref[...] = dpre
            dpre_t = dpre[:tt]
            draw = jnp.zeros((tt, D), f32)
            for k in range(KG):
                draw = draw + w_ref[k:k + 1, :] * dpre_ref[pl.ds(KG - 1 - k, tt), :]
                dcw_ref[k:k + 1, col * D:(col + 1) * D] += jnp.sum(
                    dpre_t * ext_ref[pl.ds(HALO_G - (KG - 1) + k, tt), :], axis=0, keepdims=True)
            dp_ref[:, col * D:(col + 1) * D] = draw.astype(bf16)

        one(q_ref, qp_ref, qx_ref, dq_ref, dqx_ref, wq_ref, 0, HD ** -0.5)
        one(k_ref, kp_ref, kx_ref, dk_ref, dkx_ref, wk_ref, 1, 1.0)
        one(v_ref, vp_ref, vx_ref, dv_ref, dvx_ref, wv_ref, 2, None)

        sel_b, sel_a = _head_selectors()
        ba = ba_ref[...]
        beta = _sigmoid(_dot_hi(ba, sel_b))
        sp, sg = _softplus_and_sigmoid(_dot_hi(ba, sel_a) + dt_ref[...])
        neg_a = -jnp.exp(al_ref[...])
        dgb_v = _dot_hi(_chunk_tri(tt, lower=False), dgb_ref[...])
        d_bl = dbb_ref[...] * beta * (1.0 - beta)
        d_al = dgb_v * neg_a * sg
        dba_ref[...] = _dot_hi_nt(d_bl, sel_b) + _dot_hi_nt(d_al, sel_a)
        d_alog = jnp.sum(dgb_v * neg_a * sp, axis=0, keepdims=True)
        d_dt = jnp.sum(d_al, axis=0, keepdims=True)
        dad_ref[0:1, :] += _dot_hi_nt(d_alog, sel_b)
        dad_ref[1:2, :] += _dot_hi_nt(d_dt, sel_b)

    def main(col):
        return pl.BlockSpec((tt, D), lambda i: (i, col))

    def prev(col):
        return pl.BlockSpec((HALO_G, D), lambda i: (jnp.maximum(i * hb - 1, 0), col))

    def nxt(col):
        return pl.BlockSpec((HALO_G, D), lambda i: (jnp.minimum((i + 1) * hb, last_hb), col))

    def wspec(col):
        return pl.BlockSpec((8, D), lambda i: (0, col))

    vec = pl.BlockSpec((1, D), lambda i: (0, 0))
    return pl.pallas_call(
        body, name="gdn_pre_bwd", grid=(n_tiles,),
        in_specs=[main(0), main(1), main(2), prev(0), prev(1), prev(2), nxt(0), nxt(1), nxt(2),
                  main(0), main(0), main(0), nxt(0), nxt(0), nxt(0),
                  pl.BlockSpec((tt, HD), lambda i: (i, 0)), main(0), main(0),
                  wspec(0), wspec(1), wspec(2), vec, vec],
        out_specs=[pl.BlockSpec((tt, 3 * D), lambda i: (i, 0)), pl.BlockSpec((tt, HD), lambda i: (i, 0)),
                   pl.BlockSpec((8, 3 * D), lambda i: (0, 0)), pl.BlockSpec((8, HD), lambda i: (0, 0))],
        out_shape=[jax.ShapeDtypeStruct((t, 3 * D), bf16), jax.ShapeDtypeStruct((t, HD), f32),
                   jax.ShapeDtypeStruct((8, 3 * D), f32), jax.ShapeDtypeStruct((8, HD), f32)],
        scratch_shapes=[pltpu.VMEM((HALO_G + tt + HALO_G, D), f32), pltpu.VMEM((ne, D), f32)],
        compiler_params=_cparams("arbitrary"),
    )(p_qkv, p_qkv, p_qkv, p_qkv, p_qkv, p_qkv, p_qkv, p_qkv, p_qkv,
      dqn, dkn, dva, dqn, dkn, dva, p_ba, dgb, dbb, cw, cw, cw, alog_b, dt_b)


def _dot_b(a, b, dims):
    return lax.dot_general(a.astype(bf16), b.astype(bf16), (dims, ((), ())), preferred_element_type=f32)


_NN = ((1,), (0,))
_NT = ((1,), (1,))
_TN = ((0,), (0,))


def _chunk_fn(qs, ks, vs, gcs, bbs, ss):
    heads = range(len(qs))
    r = lax.broadcasted_iota(jnp.int32, (CH, CH), 0)
    c = lax.broadcasted_iota(jnp.int32, (CH, CH), 1)
    causal = r >= c
    strict = r > c
    eye = jnp.where(r == c, 1.0, 0.0).astype(f32)
    gc_row = [gcs[h].T[:CH, :] for h in heads]
    decay = [jnp.where(causal, jnp.exp(jnp.where(causal, gcs[h][:, :CH] - gc_row[h], 0.0)), 0.0) for h in heads]
    kb = [ks[h] * bbs[h] for h in heads]
    egc = [jnp.exp(gcs[h]) for h in heads]
    kk = [_dot_b(kb[h], ks[h], _NT) for h in heads]
    qk = [_dot_b(qs[h], ks[h], _NT) for h in heads]
    m = [-jnp.where(strict, kk[h] * decay[h], 0.0) for h in heads]
    p = [eye + m[h] for h in heads]
    mp = m
    for _ in range(5):
        mp = [_dot_b(mp[h], mp[h], _NN) for h in heads]
        pm = [_dot_b(p[h], mp[h], _NN) for h in heads]
        p = [p[h] + pm[h] for h in heads]
    u = [_dot_b(p[h], vs[h] * bbs[h], _NN) for h in heads]
    w = [_dot_b(p[h], kb[h] * egc[h], _NN) for h in heads]
    intra = [jnp.where(causal, qk[h] * decay[h], 0.0) for h in heads]
    g_last = [gcs[h][CH - 1:CH, :] for h in heads]
    k_dec = [ks[h] * jnp.exp(g_last[h] - gcs[h]) for h in heads]
    ws = [_dot_b(w[h], ss[h], _NN) for h in heads]
    qs_s = [_dot_b(qs[h] * egc[h], ss[h], _NN) for h in heads]
    v_new = [u[h] - ws[h] for h in heads]
    iv = [_dot_b(intra[h], v_new[h], _NN) for h in heads]
    kv = [_dot_b(k_dec[h], v_new[h], _TN) for h in heads]
    o = tuple(qs_s[h] + iv[h] for h in heads)
    s_new = tuple(ss[h] * jnp.exp(g_last[h]) + kv[h] for h in heads)
    return o, s_new


def _head_cols():
    return [slice(h * HD, (h + 1) * HD) for h in range(NH)]


def _gdn_scan_fwd(qn, kn, va, gb, bb, *, tt):
    t = qn.shape[0]
    cpb = tt // CH

    def body(q_ref, k_ref, v_ref, g_ref, b_ref, o_ref, st_ref, s_scr):
        @pl.when(pl.program_id(0) == 0)
        def _():
            s_scr[...] = jnp.zeros_like(s_scr)

        def step(ci, carry):
            rows = pl.ds(pl.multiple_of(ci * CH, CH), CH)
            cols = _head_cols()
            ss = tuple(s_scr[h] for h in range(NH))
            for h in range(NH):
                st_ref[ci, h] = ss[h]
            o, s_new = _chunk_fn(*(tuple(ref[rows, cl] for cl in cols) for ref in (q_ref, k_ref, v_ref, g_ref, b_ref)), ss)
            for h in range(NH):
                o_ref[rows, cols[h]] = o[h]
                s_scr[h] = s_new[h]
            return carry

        lax.fori_loop(0, cpb, step, 0)

    blk = pl.BlockSpec((tt, D), lambda i: (i, 0))
    return pl.pallas_call(
        body, name="gdn_scan_fwd", grid=(t // tt,),
        in_specs=[blk] * 5,
        out_specs=[blk, pl.BlockSpec((cpb, NH, HD, HD), lambda i: (i, 0, 0, 0))],
        out_shape=[jax.ShapeDtypeStruct((t, D), f32), jax.ShapeDtypeStruct((t // CH, NH, HD, HD), f32)],
        scratch_shapes=[pltpu.VMEM((NH, HD, HD), f32)],
        compiler_params=_cparams("arbitrary"),
    )(qn, kn, va, gb, bb)


def _gdn_scan_bwd(qn, kn, va, gb, bb, states, do, *, tt):
    t = qn.shape[0]
    nblk = t // tt
    cpb = tt // CH

    def body(q_ref, k_ref, v_ref, g_ref, b_ref, st_ref, do_ref, dq_ref, dk_ref, dv_ref, dg_ref, db_ref, ds_scr):
        @pl.when(pl.program_id(0) == 0)
        def _():
            ds_scr[...] = jnp.zeros_like(ds_scr)

        def step(j, carry):
            ci = cpb - 1 - j
            rows = pl.ds(pl.multiple_of(ci * CH, CH), CH)
            cols = _head_cols()
            _, vjp = jax.vjp(_chunk_fn, *(tuple(ref[rows, cl] for cl in cols) for ref in (q_ref, k_ref, v_ref, g_ref, b_ref)),
                             tuple(st_ref[ci, h] for h in range(NH)))
            grads = vjp((tuple(do_ref[rows, cl] for cl in cols), tuple(ds_scr[h] for h in range(NH))))
            for h in range(NH):
                for ref, g in zip((dq_ref, dk_ref, dv_ref, dg_ref, db_ref), grads[:5]):
                    ref[rows, cols[h]] = g[h]
                ds_scr[h] = grads[5][h]
            return carry

        lax.fori_loop(0, cpb, step, 0)

    blk = pl.BlockSpec((tt, D), lambda i: (nblk - 1 - i, 0))
    sblk = pl.BlockSpec((cpb, NH, HD, HD), lambda i: (nblk - 1 - i, 0, 0, 0))
    sds = jax.ShapeDtypeStruct((t, D), f32)
    return pl.pallas_call(
        body, name="gdn_scan_bwd", grid=(nblk,),
        in_specs=[blk] * 5 + [sblk, blk],
        out_specs=[blk] * 5, out_shape=[sds] * 5,
        scratch_shapes=[pltpu.VMEM((NH, HD, HD), f32)],
        compiler_params=_cparams("arbitrary"),
    )(qn, kn, va, gb, bb, states, do)


def _rms_heads(o):
    ons, rs = [], []
    for h in range(NH):
        a = o[:, h * HD:(h + 1) * HD]
        r = lax.rsqrt(jnp.mean(a * a, axis=-1, keepdims=True) + RMS_EPS)
        ons.append(a * r)
        rs.append(jnp.broadcast_to(r, a.shape))
    return jnp.concatenate(ons, axis=-1), jnp.concatenate(rs, axis=-1)


def _gdn_post_fwd(o, p_gz, ng_b, *, tt):
    t = o.shape[0]

    def body(o_ref, gz_ref, ng_ref, og_ref):
        on, _ = _rms_heads(o_ref[...])
        z, _ = _silu_and_grad(gz_ref[...])
        og_ref[...] = (on * ng_ref[...] * z).astype(bf16)

    blk = pl.BlockSpec((tt, D), lambda i: (i, 0))
    return pl.pallas_call(
        body, name="gdn_post_fwd", grid=(t // tt,),
        in_specs=[blk, blk, pl.BlockSpec((1, D), lambda i: (0, 0))],
        out_specs=blk, out_shape=jax.ShapeDtypeStruct((t, D), bf16),
        compiler_params=_cparams("parallel"),
    )(o, p_gz, ng_b)


def _gdn_post_bwd(o, p_gz, ng_b, dog, *, tt):
    t = o.shape[0]

    def body(o_ref, gz_ref, ng_ref, dog_ref, do_ref, dgz_ref, dng_ref):
        @pl.when(pl.program_id(0) == 0)
        def _():
            dng_ref[...] = jnp.zeros_like(dng_ref)

        on, r = _rms_heads(o_ref[...])
        z, dz = _silu_and_grad(gz_ref[...])
        dog_v = dog_ref[...]
        ng = ng_ref[...]
        dgz_ref[...] = (dog_v * on * ng * dz).astype(bf16)
        dy = dog_v * z
        dng_all = jnp.sum(dy * on, axis=0, keepdims=True)
        dng = dng_all[:, 0:HD]
        for h in range(1, NH):
            dng = dng + dng_all[:, h * HD:(h + 1) * HD]
        dng_ref[0:1, :] += dng
        don = dy * ng
        prod = don * on
        parts = []
        for h in range(NH):
            sl = slice(h * HD, (h + 1) * HD)
            parts.append(don[:, sl] - on[:, sl] * jnp.mean(prod[:, sl], axis=-1, keepdims=True))
        do_ref[...] = r * jnp.concatenate(parts, axis=-1)

    blk = pl.BlockSpec((tt, D), lambda i: (i, 0))
    return pl.pallas_call(
        body, name="gdn_post_bwd", grid=(t // tt,),
        in_specs=[blk, blk, pl.BlockSpec((1, D), lambda i: (0, 0)), blk],
        out_specs=[blk, blk, pl.BlockSpec((8, HD), lambda i: (0, 0))],
        out_shape=[jax.ShapeDtypeStruct((t, D), f32), jax.ShapeDtypeStruct((t, D), bf16),
                   jax.ShapeDtypeStruct((8, HD), f32)],
        compiler_params=_cparams("arbitrary"),
    )(o, p_gz, ng_b, dog)


def _merge(x, y_conf, y_gdn, p_gates, target, w_o, w_o_t, ln_g, ln_b, *, tt):
    t = x.shape[0]

    def body(x_ref, yc_ref, yg_ref, gc_ref, gg_ref, tg_ref, w_ref, wt_ref, g_ref, b_ref,
             loss_ref, dxd_ref, dyc_ref, dyg_ref, dpg_ref, h_ref, dz_ref, dvec_ref):
        @pl.when(pl.program_id(0) == 0)
        def _():
            loss_ref[...] = jnp.zeros_like(loss_ref)
            dvec_ref[...] = jnp.zeros_like(dvec_ref)

        sc = _sigmoid(gc_ref[...])
        sg = _sigmoid(gg_ref[...])
        yc = yc_ref[...]
        yg = yg_ref[...]
        h = (sc * yc + sg * yg).astype(bf16)
        h_ref[...] = h
        z = DN_ALPHA * x_ref[...] + jnp.dot(h, w_ref[...], preferred_element_type=f32)
        mu = jnp.mean(z, axis=-1, keepdims=True)
        cen = z - mu
        rstd = lax.rsqrt(jnp.mean(cen * cen, axis=-1, keepdims=True) + LN_EPS)
        xhat = cen * rstd
        err = xhat * g_ref[...] + b_ref[...] - tg_ref[...]
        loss_ref[...] += 0.5 / D * jnp.sum(err * err)
        dy = err * (1.0 / D)
        dvec_ref[0:1, :] += jnp.sum(dy * xhat, axis=0, keepdims=True)
        dvec_ref[1:2, :] += jnp.sum(dy, axis=0, keepdims=True)
        dxhat = dy * g_ref[...]
        dz = rstd * (dxhat - jnp.mean(dxhat, axis=-1, keepdims=True)
                     - xhat * jnp.mean(dxhat * xhat, axis=-1, keepdims=True))
        dxd_ref[...] = DN_ALPHA * dz
        dz_b = dz.astype(bf16)
        dz_ref[...] = dz_b
        dh = jnp.dot(dz_b, wt_ref[...], preferred_element_type=f32)
        dyc_ref[...] = (dh * sc).astype(bf16)
        dyg_ref[...] = (dh * sg).astype(bf16)
        dpg_ref[:, 0:D] = (dh * yc * sc * (1.0 - sc)).astype(bf16)
        dpg_ref[:, D:] = (dh * yg * sg * (1.0 - sg)).astype(bf16)

    blk = pl.BlockSpec((tt, D), lambda i: (i, 0))
    wblk = pl.BlockSpec((D, D), lambda i: (0, 0))
    vec = pl.BlockSpec((1, D), lambda i: (0, 0))
    return pl.pallas_call(
        body, name="merge_norm_loss", grid=(t // tt,),
        in_specs=[blk, blk, blk, pl.BlockSpec((tt, D), lambda i: (i, 0)), pl.BlockSpec((tt, D), lambda i: (i, 1)),
                  blk, wblk, wblk, vec, vec],
        out_specs=[pl.BlockSpec((8, HD), lambda i: (0, 0)), blk, blk, blk,
                   pl.BlockSpec((tt, 2 * D), lambda i: (i, 0)), blk, blk, pl.BlockSpec((8, D), lambda i: (0, 0))],
        out_shape=[jax.ShapeDtypeStruct((8, HD), f32), jax.ShapeDtypeStruct((t, D), f32),
                   jax.ShapeDtypeStruct((t, D), bf16), jax.ShapeDtypeStruct((t, D), bf16),
                   jax.ShapeDtypeStruct((t, 2 * D), bf16), jax.ShapeDtypeStruct((t, D), bf16),
                   jax.ShapeDtypeStruct((t, D), bf16), jax.ShapeDtypeStruct((8, D), f32)],
        compiler_params=_cparams("arbitrary"),
    )(x, y_conf, y_gdn, p_gates, p_gates, target, w_o, w_o_t, ln_g, ln_b)


def _place():
    return lax.axis_index("x"), lax.axis_index("y"), lax.axis_index("c")


def _any_specs(n):
    return [pl.BlockSpec(memory_space=pl.ANY)] * n


def _sibling_merge(arrs, name, take_other_half=False):
    k = len(arrs)

    def body(*refs):
        a_refs, o_refs = refs[:k], refs[k:2 * k]
        send_sems, recv_sems, local_sems = refs[2 * k:]
        x, y, c = _place()

        def rows(ref, start, n):
            return ref.at[(slice(None),) * (len(ref.shape) - 2) + (pl.ds(start, n), slice(None))]

        sends, owns = [], []
        for i in range(k):
            if take_other_half:
                n = a_refs[i].shape[-2] // 2
                src = rows(a_refs[i], (1 - c) * n, n)
                dst = o_refs[i]
            else:
                n = a_refs[i].shape[-2]
                src = a_refs[i]
                dst = rows(o_refs[i], c * n, n)
                own = pltpu.make_async_copy(src, dst, local_sems.at[i])
                own.start()
                owns.append(own)
            cp = pltpu.make_async_remote_copy(src_ref=src, dst_ref=dst, send_sem=send_sems.at[i],
                                              recv_sem=recv_sems.at[i], device_id=(x, y, 1 - c), device_id_type=MESH)
            cp.start()
            sends.append(cp)
        for cp in sends:
            cp.wait()
        for own in owns:
            own.wait()

    def out_sds(a):
        rows = a.shape[-2] // 2 if take_other_half else a.shape[-2] * 2
        return jax.ShapeDtypeStruct(a.shape[:-2] + (rows, a.shape[-1]), a.dtype)

    return pl.pallas_call(
        body, name=name, in_specs=_any_specs(k), out_specs=_any_specs(k),
        out_shape=[out_sds(a) for a in arrs],
        scratch_shapes=[pltpu.SemaphoreType.DMA((k,)), pltpu.SemaphoreType.DMA((k,)), pltpu.SemaphoreType.DMA((k,))],
    )(*arrs)


def _chip_exchange(arrs, name, scatter):
    k = len(arrs)

    def body(*refs):
        a_refs, o_refs = refs[:k], refs[k:2 * k]
        send_sems, recv_sems, local_sems = refs[2 * k:]
        x, y, c = _place()
        me = 2 * x + y
        peers = [(1 - x, y), (x, 1 - y), (1 - x, 1 - y)]

        def src(i, j):
            return a_refs[i].at[j] if scatter else a_refs[i]

        def copy(i, n, send_j, slot):
            px, py = peers[n]
            return pltpu.make_async_remote_copy(
                src_ref=src(i, send_j), dst_ref=o_refs[i].at[slot], send_sem=send_sems.at[3 * i + n],
                recv_sem=recv_sems.at[3 * i + n], device_id=(px, py, c), device_id_type=MESH)

        owns = [pltpu.make_async_copy(src(i, me), o_refs[i].at[me], local_sems.at[i]) for i in range(k)]
        for own in owns:
            own.start()
        sends = [copy(i, n, 2 * peers[n][0] + peers[n][1], me) for n in range(3) for i in range(k)]
        for cp in sends:
            cp.start()
        for n in range(3):
            for i in range(k):
                copy(i, n, me, 2 * peers[n][0] + peers[n][1]).wait_recv()
        for cp in sends:
            cp.wait_send()
        for own in owns:
            own.wait()

    def out_sds(a):
        return jax.ShapeDtypeStruct((N_CHIPS,) + tuple(a.shape[1:] if scatter else a.shape), a.dtype)

    return pl.pallas_call(
        body, name=name, in_specs=_any_specs(k), out_specs=_any_specs(k),
        out_shape=[out_sds(a) for a in arrs],
        scratch_shapes=[pltpu.SemaphoreType.DMA((3 * k,)), pltpu.SemaphoreType.DMA((3 * k,)),
                        pltpu.SemaphoreType.DMA((k,))],
    )(*arrs)


def _pair_sum(g_all, got, c_arr, name):
    n, w = got.shape[1:]
    tile = n // 4
    n_tiles = n // tile

    def body(c_ref, a_ref, b_ref, o_ref):
        o_ref[...] = a_ref[...] + b_ref[...]

    return pl.pallas_call(
        body, name=name,
        grid_spec=pltpu.PrefetchScalarGridSpec(
            num_scalar_prefetch=1, grid=(N_CHIPS, n_tiles),
            in_specs=[pl.BlockSpec((1, tile, w), lambda j, i, c_ref: (j, c_ref[0] * n_tiles + i, 0)),
                      pl.BlockSpec((1, tile, w), lambda j, i, c_ref: (j, i, 0))],
            out_specs=pl.BlockSpec((1, tile, w), lambda j, i, c_ref: (j, i, 0))),
        out_shape=jax.ShapeDtypeStruct(got.shape, f32),
        compiler_params=_cparams("parallel", "parallel"),
    )(c_arr, g_all, got)


def _sum_slots(a, name):
    n, w = a.shape[1:]
    tile = n // 4

    def body(a_ref, o_ref):
        o_ref[...] = ((a_ref[0] + a_ref[1]) + a_ref[2]) + a_ref[3]

    return pl.pallas_call(
        body, name=name, grid=(n // tile,),
        in_specs=[pl.BlockSpec((N_CHIPS, tile, w), lambda i: (0, i, 0))],
        out_specs=pl.BlockSpec((tile, w), lambda i: (i, 0)),
        out_shape=jax.ShapeDtypeStruct((n, w), f32),
        compiler_params=_cparams("parallel"),
    )(a)


def _adamw(w, g, m, v, name):
    rows, width = w.shape
    tile = rows // 8
    c1 = 1.0 / (1.0 - ADAM_B1 ** ADAM_STEP)
    c2 = 1.0 / (1.0 - ADAM_B2 ** ADAM_STEP)

    def body(w_ref, g_ref, m_ref, v_ref, d_ref, mo_ref, vo_ref):
        g_v = g_ref[...]
        m_new = ADAM_B1 * m_ref[...] + (1.0 - ADAM_B1) * g_v
        v_new = ADAM_B2 * v_ref[...] + (1.0 - ADAM_B2) * (g_v * g_v)
        mo_ref[...] = m_new
        vo_ref[...] = v_new
        d_ref[...] = -ADAM_LR * ((m_new * c1) / (jnp.sqrt(v_new * c2) + ADAM_EPS) + ADAM_WD * w_ref[...])

    blk = pl.BlockSpec((tile, width), lambda i: (i, 0))
    sds = jax.ShapeDtypeStruct((rows, width), f32)
    return pl.pallas_call(
        body, name=name, grid=(rows // tile,),
        in_specs=[blk] * 4, out_specs=[blk] * 3, out_shape=[sds] * 3,
        compiler_params=_cparams("parallel"),
    )(w, g, m, v)


R_DW = 3 * SQ_BLK
R_CW = R_DW + 8
R_VEC = R_CW + 8
R_SMALL = R_VEC + 8
REST_ROWS = 896


def _pack_small(conf_dw_w, gdn_conv_w, vecs, a_log, dt_bias, norm_g):
    dw = jnp.pad(conf_dw_w.reshape(-1), (0, 8 * D - KC * SQ_BLK)).reshape(8, D)
    cw = jnp.pad(gdn_conv_w.reshape(-1), (0, 5 * D)).reshape(8, D)
    vec = jnp.pad(jnp.stack(vecs), ((0, 3), (0, 0)))
    small = jnp.pad(jnp.concatenate([a_log, dt_bias, norm_g]), (0, D - 2 * NH - HD)).reshape(1, D)
    return jnp.pad(jnp.concatenate([dw, cw, vec, small], axis=0), ((0, REST_ROWS - R_SMALL - 1), (0, 0)))


def _pack_rest(conf_w_out, gdn_w_out, w_o, small):
    return jnp.concatenate([conf_w_out, gdn_w_out, w_o, small], axis=0)


def _unpack_rest(p):
    conf_dw_w = p[R_DW:R_DW + 8].reshape(-1)[:KC * SQ_BLK].reshape(KC, SQ_BLK)
    gdn_conv_w = p[R_CW:R_CW + 3].reshape(KG, 3 * SQ_BLK)
    small = p[R_SMALL]
    return dict(conf_w_out=p[0:SQ_BLK], gdn_w_out=p[SQ_BLK:2 * SQ_BLK], w_o=p[2 * SQ_BLK:R_DW],
                conf_dw_w=conf_dw_w, gdn_conv_w=gdn_conv_w, conf_dw_b=p[R_VEC], conf_ln_g=p[R_VEC + 1],
                conf_ln_b=p[R_VEC + 2], post_ln_g=p[R_VEC + 3], post_ln_b=p[R_VEC + 4],
                gdn_A_log=small[0:NH], gdn_dt_bias=small[NH:2 * NH], gdn_norm_g=small[2 * NH:2 * NH + HD])


_WEIGHT_ORDER = ("w_in", "conf_dw_w", "conf_dw_b", "conf_ln_g", "conf_ln_b", "conf_w_out", "gdn_conv_w",
                 "gdn_A_log", "gdn_dt_bias", "gdn_norm_g", "gdn_w_out", "w_o", "post_ln_g", "post_ln_b")


def _gather_weights(w_in, conf_w_out, gdn_w_out, w_o, conf_dw_w, gdn_conv_w):
    c = lax.axis_index("c")
    sq = jnp.concatenate([conf_w_out, gdn_w_out, w_o], axis=0).astype(bf16)
    w_half = lax.dynamic_slice_in_dim(w_in.astype(bf16), c * (D // 2), D // 2, axis=0)
    sq_half = lax.dynamic_slice_in_dim(sq, c * (sq.shape[0] // 2), sq.shape[0] // 2, axis=0)
    small = jnp.concatenate([jnp.pad(conf_dw_w.reshape(-1), (0, 8 * D - KC * SQ_BLK)).reshape(8, D),
                             jnp.pad(gdn_conv_w.reshape(-1), (0, 5 * D)).reshape(8, D)], axis=0)
    got_w, got_sq, small_all = _chip_exchange([w_half, sq_half, small], "weight_gather_chips", scatter=False)
    w4, sq4 = _sibling_merge([got_w, got_sq], "weight_gather_sibling")
    sq_full = [sq4[:, n * SQ_BLK:(n + 1) * SQ_BLK].reshape(D, D) for n in range(3)]
    dw_full = small_all[:, 0:8].reshape(N_CHIPS, 8 * D)[:, :KC * SQ_BLK].reshape(N_CHIPS, KC, SQ_BLK)
    dw_full = dw_full.transpose(1, 0, 2).reshape(KC, D)
    cw_full = small_all[:, 8:11].reshape(N_CHIPS, KG, 3 * SQ_BLK).transpose(1, 0, 2).reshape(KG, 3 * D)
    return w4, sq_full[0], sq_full[1], sq_full[2], dw_full, cw_full


def _w_in_cols(w4, lo, hi):
    parts = []
    for j in range(N_CHIPS):
        a, b = max(lo, j * W_IN_BLK), min(hi, (j + 1) * W_IN_BLK)
        if a < b:
            parts.append(w4[j, :, a - j * W_IN_BLK:b - j * W_IN_BLK])
    return parts[0] if len(parts) == 1 else jnp.concatenate(parts, axis=1)


def _w_in_by_chip(pieces):
    chips = []
    for j in range(N_CHIPS):
        lo, hi = j * W_IN_BLK, (j + 1) * W_IN_BLK
        parts = []
        for start, arr in pieces:
            a, b = max(lo, start), min(hi, start + arr.shape[1])
            if a < b:
                parts.append(arr[:, a - start:b - start])
        chips.append(jnp.concatenate(parts, axis=1))
    return jnp.stack(chips)


def _reduce_scatter(g_w, g_rest):
    c_arr = lax.axis_index("c").astype(jnp.int32).reshape(1)
    got_w, got_r = _sibling_merge([g_w, g_rest], "grad_sibling_halves", take_other_half=True)
    pair_w = _pair_sum(g_w, got_w, c_arr, "grad_pair_sum_w_in")
    pair_r = _pair_sum(g_rest, got_r, c_arr, "grad_pair_sum_rest")
    all_w, all_r = _chip_exchange([pair_w, pair_r], "grad_chip_scatter", scatter=True)
    return _sibling_merge([_sum_slots(all_w, "grad_chip_sum_w_in"), _sum_slots(all_r, "grad_chip_sum_rest")],
                          "grad_sibling_result")


def _local_step(x2, tgt, w4, wc_out, wg_out, wo_full, dw_full, cw_full, conf_dw_b, conf_ln_g, conf_ln_b,
                gdn_A_log, gdn_dt_bias, gdn_norm_g, post_ln_g, post_ln_b):
    t = x2.shape[0]
    tt = min(TOKEN_TILE, t)
    tm = min(512, t)

    w_conv, w_qkv, w_gz = _w_in_cols(w4, 0, 3 * D), _w_in_cols(w4, 3 * D, 6 * D), _w_in_cols(w4, 6 * D, 7 * D)
    w_gates = _w_in_cols(w4, 7 * D + 2 * NH, W_IN_COLS)
    dw_pad = jnp.pad(dw_full, ((0, HALO_C - KC), (0, 0)))
    cw_pad = jnp.pad(cw_full, ((0, 8 - KG), (0, 0)))
    row = lambda v: v.reshape(1, D)
    alog_b = row(jnp.repeat(gdn_A_log, HD))
    dt_b = row(jnp.repeat(gdn_dt_bias, HD))
    ng_b = row(jnp.tile(gdn_norm_g, NH))
    w_ba = jnp.pad(_w_in_cols(w4, 7 * D, 7 * D + 2 * NH), ((0, 0), (0, HD - 2 * NH)))

    x_b = x2.astype(bf16)
    x_t = x_b.T

    p_conv = _mm_multi([x_b], [w_conv], out_dtype=f32, tm=tm, tn=1024, name="proj_conv")
    p_qkv = _mm_multi([x_b], [w_qkv], out_dtype=f32, tm=tm, tn=1024, name="proj_qkv")
    p_gz = _mm_multi([x_b], [w_gz], out_dtype=f32, tm=tm, tn=1024, name="proj_gz")
    p_gates = _mm_multi([x_b], [w_gates], out_dtype=f32, tm=tm, tn=1024, name="proj_gates")
    p_ba = _mm_multi([x_b], [w_ba], out_dtype=f32, tm=tm, tn=HD, name="proj_ba")

    u = _conv_fwd(p_conv, dw_pad, row(conf_dw_b), row(conf_ln_g), row(conf_ln_b), tt=tt)
    y_conf = _mm_multi([u], [wc_out], out_dtype=f32, tm=tm, tn=1024, name="conf_out")

    qn, kn, va, gb, bb = _gdn_pre_fwd(p_qkv, p_ba, cw_pad, alog_b, dt_b, tt=tt)
    o, states = _gdn_scan_fwd(qn, kn, va, gb, bb, tt=tt)
    og = _gdn_post_fwd(o, p_gz, ng_b, tt=tt)
    y_gdn = _mm_multi([og], [wg_out], out_dtype=f32, tm=tm, tn=1024, name="gdn_out")

    loss_blk, dxd, dyc, dyg, dp_gates, h, dz, dpost = _merge(
        x2, y_conf, y_gdn, p_gates, tgt, wo_full, wo_full.T, row(post_ln_g), row(post_ln_b), tt=tt)

    d_wo = _mm_kloop(h.T, dz, tm=tm, tn=1024, tk=min(1024, t), name="grad_w_o")
    du = _mm_multi([dyc], [wc_out.T], out_dtype=f32, tm=tm, tn=1024, name="conf_out_bwd")
    d_wc = _mm_kloop(u.T, dyc, tm=tm, tn=1024, tk=min(1024, t), name="grad_conf_w_out")
    dog = _mm_multi([dyg], [wg_out.T], out_dtype=f32, tm=tm, tn=1024, name="gdn_out_bwd")
    d_wg = _mm_kloop(og.T, dyg, tm=tm, tn=1024, tk=min(1024, t), name="grad_gdn_w_out")

    dp_conv, d_dww, dconv_vec = _conv_bwd(p_conv, du, dw_pad, row(conf_dw_b), row(conf_ln_g), row(conf_ln_b), tt=tt)

    do, dp_gz, dng = _gdn_post_bwd(o, p_gz, ng_b, dog, tt=tt)
    dqn, dkn, dva, dgb, dbb = _gdn_scan_bwd(qn, kn, va, gb, bb, states, do, tt=tt)
    dp_qkv, dp_ba, d_cw, d_ad = _gdn_pre_bwd(p_qkv, p_ba, cw_pad, alog_b, dt_b, dqn, dkn, dva, dgb, dbb, tt=tt)
    dp_ba_b = dp_ba.astype(bf16)

    grad_x = _mm_multi(
        [dp_conv, dp_qkv, dp_gz, dp_gates, dp_ba_b],
        [w_conv.T, w_qkv.T, w_gz.T, w_gates.T, w_ba.T], dxd,
        out_dtype=f32, tm=min(256, t), tn=512, name="grad_x")

    tk = min(1024, t)
    d_w_conv = _mm_kloop(x_t, dp_conv, tm=tm, tn=1024, tk=tk, name="grad_w_in_conv")
    d_w_qkv = _mm_kloop(x_t, dp_qkv, tm=tm, tn=1024, tk=tk, name="grad_w_in_qkv")
    d_w_gz = _mm_kloop(x_t, dp_gz, tm=tm, tn=1024, tk=tk, name="grad_w_in_gz")
    d_w_gates = _mm_kloop(x_t, dp_gates, tm=tm, tn=1024, tk=tk, name="grad_w_in_gates")
    d_w_ba = _mm_kloop(x_t, dp_ba_b, tm=tm, tn=HD, tk=tk, name="grad_w_in_ba")
    d_w_in = _w_in_by_chip([(0, d_w_conv), (3 * D, d_w_qkv), (6 * D, d_w_gz), (7 * D, d_w_ba[:, :2 * NH]),
                            (7 * D + 2 * NH, d_w_gates)])

    return (loss_blk[0, 0], grad_x, d_w_in, d_wc, d_wg, d_wo, d_dww, d_cw, dconv_vec, dpost, d_ad, dng)


def kernel(x, w_in, conf_dw_w, conf_dw_b, conf_ln_g, conf_ln_b, conf_w_out, gdn_conv_w, gdn_A_log, gdn_dt_bias, gdn_norm_g, gdn_w_out, w_o, post_ln_g, post_ln_b, loss_target, m_w_in, m_conf_dw_w, m_conf_dw_b, m_conf_ln_g, m_conf_ln_b, m_conf_w_out, m_gdn_conv_w, m_gdn_A_log, m_gdn_dt_bias, m_gdn_norm_g, m_gdn_w_out, m_w_o, m_post_ln_g, m_post_ln_b, v_w_in, v_conf_dw_w, v_conf_dw_b, v_conf_ln_g, v_conf_ln_b, v_conf_w_out, v_gdn_conv_w, v_gdn_A_log, v_gdn_dt_bias, v_gdn_norm_g, v_gdn_w_out, v_w_o, v_post_ln_g, v_post_ln_b):
    x2 = x.reshape(x.shape[-2], D)
    tgt = loss_target.reshape(x2.shape)
    w4, wc_out, wg_out, wo_full, dw_full, cw_full = _gather_weights(
        w_in, conf_w_out, gdn_w_out, w_o, conf_dw_w, gdn_conv_w)
    (loss_part, grad_x, d_w_in, d_wc, d_wg, d_wo, d_dww, d_cw, dconv_vec, dpost, d_ad, dng) = _local_step(
        x2, tgt, w4, wc_out, wg_out, wo_full, dw_full, cw_full, conf_dw_b, conf_ln_g, conf_ln_b,
        gdn_A_log, gdn_dt_bias, gdn_norm_g, post_ln_g, post_ln_b)
    loss = lax.psum(loss_part, ("x", "y", "c"))

    dww_c = d_dww[:KC].reshape(KC, N_CHIPS, SQ_BLK)
    dcw_c = d_cw[:KG].reshape(KG, N_CHIPS, 3 * SQ_BLK)
    vecs = [dconv_vec[0], dconv_vec[1], dconv_vec[2], dpost[0], dpost[1]]
    g_rest = jnp.stack([
        _pack_rest(d_wc[j * SQ_BLK:(j + 1) * SQ_BLK], d_wg[j * SQ_BLK:(j + 1) * SQ_BLK], d_wo[j * SQ_BLK:(j + 1) * SQ_BLK],
                   _pack_small(dww_c[:, j], dcw_c[:, j], vecs, d_ad[0, :NH], d_ad[1, :NH], dng[0]))
        for j in range(N_CHIPS)])
    g_w_in, g_rest = _reduce_scatter(d_w_in, g_rest)

    def rest_of(w_c, w_g, w_oo, dw, cw, b1, g1, b2, g2, b3, a_log, dt_bias, norm_g):
        return _pack_rest(w_c, w_g, w_oo, _pack_small(dw, cw, [b1, g1, b2, g2, b3], a_log, dt_bias, norm_g))

    w_r = rest_of(conf_w_out, gdn_w_out, w_o, conf_dw_w, gdn_conv_w, conf_dw_b, conf_ln_g, conf_ln_b,
                  post_ln_g, post_ln_b, gdn_A_log, gdn_dt_bias, gdn_norm_g)
    m_r = rest_of(m_conf_w_out, m_gdn_w_out, m_w_o, m_conf_dw_w, m_gdn_conv_w, m_conf_dw_b, m_conf_ln_g, m_conf_ln_b,
                  m_post_ln_g, m_post_ln_b, m_gdn_A_log, m_gdn_dt_bias, m_gdn_norm_g)
    v_r = rest_of(v_conf_w_out, v_gdn_w_out, v_w_o, v_conf_dw_w, v_gdn_conv_w, v_conf_dw_b, v_conf_ln_g, v_conf_ln_b,
                  v_post_ln_g, v_post_ln_b, v_gdn_A_log, v_gdn_dt_bias, v_gdn_norm_g)
    upd_w_in = _adamw(w_in, g_w_in, m_w_in, v_w_in, "adamw_w_in")
    upd_rest = _adamw(w_r, g_rest, m_r, v_r, "adamw_rest")

    out = [loss, grad_x.reshape(x.shape)]
    for big, rest in zip((g_w_in,) + tuple(upd_w_in), (g_rest,) + tuple(upd_rest)):
        d = dict(_unpack_rest(rest), w_in=big)
        out += [d[n] for n in _WEIGHT_ORDER]
    return tuple(out)
```

```python
import functools

import jax
import jax.numpy as jnp
from jax import lax
from jax.experimental import pallas as pl
from jax.experimental.pallas import tpu as pltpu

f32 = jnp.float32
bf16 = jnp.bfloat16
HI = lax.Precision.HIGHEST
MESH = pl.DeviceIdType.MESH

D = 1024
NH = 8
HD = 128
CH = 64
KC = 31
KG = 4
HALO_C = 32
HALO_G = 8
LANE = 128
STRIP = 32
N_SHIFT = 7
LN_EPS = 1e-5
RMS_EPS = 1e-6
L2_EPS = 1e-6
DN_ALPHA = 2.0 ** 0.25
N_CHIPS = 4
W_IN_COLS = 9232
W_IN_BLK = W_IN_COLS // N_CHIPS
SQ_BLK = D // N_CHIPS
VMEM_LIMIT = 52 * 1024 * 1024
TOKEN_TILE = 256

ADAM_LR = 0.001
ADAM_B1 = 0.9
ADAM_B2 = 0.999
ADAM_EPS = 1e-08
ADAM_WD = 0.01
ADAM_STEP = 10


def _sigmoid(x):
    return 1.0 / (1.0 + jnp.exp(-x))


def _silu_and_grad(x):
    s = _sigmoid(x)
    return x * s, s * (1.0 + x * (1.0 - s))


def _cparams(*sem):
    return pltpu.CompilerParams(dimension_semantics=sem, vmem_limit_bytes=VMEM_LIMIT)


def _mm_multi(a_list, b_list, addend=None, *, out_dtype, tm, tn, name, precision=None):
    n_pairs = len(a_list)
    m = a_list[0].shape[0]
    n = b_list[0].shape[1]
    has_add = addend is not None

    def body(*refs):
        a_refs = refs[:n_pairs]
        b_refs = refs[n_pairs:2 * n_pairs]
        o_ref = refs[-1]
        acc = None
        for a_ref, b_ref in zip(a_refs, b_refs):
            if precision is None:
                p = jnp.dot(a_ref[...].astype(bf16), b_ref[...].astype(bf16), preferred_element_type=f32)
            else:
                p = jnp.dot(a_ref[...], b_ref[...], preferred_element_type=f32, precision=precision)
            acc = p if acc is None else acc + p
        if has_add:
            acc = acc + refs[2 * n_pairs][...]
        o_ref[...] = acc.astype(out_dtype)

    in_specs = [pl.BlockSpec((tm, a.shape[1]), lambda j, i: (i, 0)) for a in a_list]
    in_specs += [pl.BlockSpec((b.shape[0], tn), lambda j, i: (0, j)) for b in b_list]
    args = list(a_list) + list(b_list)
    if has_add:
        in_specs.append(pl.BlockSpec((tm, tn), lambda j, i: (i, j)))
        args.append(addend)
    return pl.pallas_call(
        body, name=name, grid=(n // tn, m // tm),
        in_specs=in_specs, out_specs=pl.BlockSpec((tm, tn), lambda j, i: (i, j)),
        out_shape=jax.ShapeDtypeStruct((m, n), out_dtype),
        compiler_params=_cparams("parallel", "parallel"),
    )(*args)


def _mm_kloop(a, b, *, tm, tn, tk, name, precision=None):
    m, k = a.shape
    n = b.shape[1]
    nk = k // tk

    def body(a_ref, b_ref, o_ref):
        @pl.when(pl.program_id(2) == 0)
        def _():
            o_ref[...] = jnp.zeros_like(o_ref)
        if precision is None:
            o_ref[...] += jnp.dot(a_ref[...].astype(bf16), b_ref[...].astype(bf16), preferred_element_type=f32)
        else:
            o_ref[...] += jnp.dot(a_ref[...], b_ref[...], preferred_element_type=f32, precision=precision)

    return pl.pallas_call(
        body, name=name, grid=(n // tn, m // tm, nk),
        in_specs=[pl.BlockSpec((tm, tk), lambda j, i, kk: (i, kk)), pl.BlockSpec((tk, tn), lambda j, i, kk: (kk, j))],
        out_specs=pl.BlockSpec((tm, tn), lambda j, i, kk: (i, j)),
        out_shape=jax.ShapeDtypeStruct((m, n), f32),
        compiler_params=_cparams("parallel", "parallel", "arbitrary"),
    )(a, b)


def _shift_copies(src_ref, sh_ref, n):
    for b in range(1, 8):
        sh_ref[b - 1, 0:n, :] = src_ref[pl.ds(b, n), :]


def _by_residue(offs):
    groups = {}
    for k, off in enumerate(offs):
        groups.setdefault(off % 8, []).append((k, off // 8))
    return groups


def _slab(src_ref, sh_ref, b, r0, n, lanes):
    ref = src_ref if b == 0 else sh_ref.at[b - 1]
    return ref[r0:r0 + n, lanes]


def _tap_conv(out_ref, n_rows, src_ref, sh_ref, w_ref, offs, bias_ref=None):
    groups = _by_residue(offs)
    for j in range(D // LANE):
        lanes = slice(j * LANE, (j + 1) * LANE)
        wv = [w_ref[k:k + 1, lanes] for k in range(len(offs))]
        for r0 in range(0, n_rows, STRIP):
            n = min(STRIP, n_rows - r0)
            accs = [jnp.zeros((n, LANE), f32) if bias_ref is None else jnp.broadcast_to(bias_ref[0:1, lanes], (n, LANE)),
                    jnp.zeros((n, LANE), f32)]
            m = 0
            for b, taps in groups.items():
                a_lo = min(a for _, a in taps)
                a_hi = max(a for _, a in taps)
                wide = _slab(src_ref, sh_ref, b, r0 + 8 * a_lo, 8 * (a_hi - a_lo) + n, lanes)
                for k, a in taps:
                    accs[m % 2] = accs[m % 2] + wv[k] * wide[8 * (a - a_lo):8 * (a - a_lo) + n]
                    m += 1
            out_ref[r0:r0 + n, lanes] = accs[0] + accs[1]


def _tap_corr(dw_ref, n_rows, lhs_ref, src_ref, sh_ref, offs):
    groups = _by_residue(offs)
    for j in range(D // LANE):
        lanes = slice(j * LANE, (j + 1) * LANE)
        accs = [jnp.zeros((8, LANE), f32) for _ in offs]
        for r0 in range(0, n_rows, STRIP):
            n = min(STRIP, n_rows - r0)
            d = lhs_ref[r0:r0 + n, lanes]
            for b, taps in groups.items():
                a_lo = min(a for _, a in taps)
                a_hi = max(a for _, a in taps)
                wide = _slab(src_ref, sh_ref, b, r0 + 8 * a_lo, 8 * (a_hi - a_lo) + n, lanes)
                for k, a in taps:
                    prod = d * wide[8 * (a - a_lo):8 * (a - a_lo) + n]
                    part = prod[0:8]
                    for q in range(1, n // 8):
                        part = part + prod[8 * q:8 * q + 8]
                    accs[k] = accs[k] + part
        for k in range(len(offs)):
            dw_ref[k:k + 1, lanes] += jnp.sum(accs[k], axis=0, keepdims=True)


_FWD_OFFS = [HALO_C - (KC - 1) + k for k in range(KC)]
_BWD_OFFS = [KC - 1 - k for k in range(KC)]


def _norm_act(a1, cz, g_ref, bb_ref):
    mu = jnp.mean(a1, axis=-1, keepdims=True)
    cen = a1 - mu
    var = jnp.mean(cen * cen, axis=-1, keepdims=True)
    rstd = lax.rsqrt(var + LN_EPS)
    xhat = cen * rstd
    ln = xhat * g_ref[...] + bb_ref[...]
    s, ds = _silu_and_grad(ln)
    zc, dzc = _silu_and_grad(cz)
    return xhat, rstd, s, ds, zc, dzc


def _conv_fwd(p_conv, dw_w, dw_b, ln_g, ln_b, *, tt):
    t = p_conv.shape[0]
    hb = tt // HALO_C

    def body(cv_ref, cg_ref, cz_ref, cvh_ref, cgh_ref, w_ref, b_ref, g_ref, bb_ref, u_ref, ext_ref, sh_ref, a1_ref):
        first = pl.program_id(0) == 0
        halo = cvh_ref[...] * _sigmoid(cgh_ref[...])
        ext_ref[0:HALO_C, :] = jnp.where(first, 0.0, halo)
        ext_ref[HALO_C:, :] = cv_ref[...] * _sigmoid(cg_ref[...])
        _shift_copies(ext_ref, sh_ref, tt + HALO_C - 8)
        _tap_conv(a1_ref, tt, ext_ref, sh_ref, w_ref, _FWD_OFFS, b_ref)
        _, _, s, _, zc, _ = _norm_act(a1_ref[...], cz_ref[...], g_ref, bb_ref)
        u_ref[...] = (s * zc).astype(bf16)

    def main(col):
        return pl.BlockSpec((tt, D), lambda i: (i, col))

    def prev(col):
        return pl.BlockSpec((HALO_C, D), lambda i: (jnp.maximum(i * hb - 1, 0), col))

    vec = pl.BlockSpec((1, D), lambda i: (0, 0))
    return pl.pallas_call(
        body, name="conv_fwd", grid=(t // tt,),
        in_specs=[main(0), main(1), main(2), prev(0), prev(1),
                  pl.BlockSpec((HALO_C, D), lambda i: (0, 0)), vec, vec, vec],
        out_specs=pl.BlockSpec((tt, D), lambda i: (i, 0)),
        out_shape=jax.ShapeDtypeStruct((t, D), bf16),
        scratch_shapes=[pltpu.VMEM((tt + HALO_C, D), f32), pltpu.VMEM((N_SHIFT, tt + HALO_C - 8, D), f32),
                        pltpu.VMEM((tt, D), f32)],
        compiler_params=_cparams("parallel"),
    )(p_conv, p_conv, p_conv, p_conv, p_conv, dw_w, dw_b, ln_g, ln_b)


def _conv_bwd(p_conv, du, dw_w, dw_b, ln_g, ln_b, *, tt):
    t = p_conv.shape[0]
    hb = tt // HALO_C
    n_tiles = t // tt
    last_hb = t // HALO_C - 1
    ne = tt + HALO_C

    def body(cv_ref, cg_ref, cz_ref, du_ref, cvp_ref, cgp_ref, cvn_ref, cgn_ref, czn_ref, dun_ref,
             w_ref, b_ref, g_ref, bb_ref, dp_ref, dww_ref, dvec_ref, ext_ref, sh_ref, a1_ref, da1_ref, da0_ref):
        i = pl.program_id(0)
        first = i == 0
        last = i == n_tiles - 1

        @pl.when(first)
        def _():
            dww_ref[...] = jnp.zeros_like(dww_ref)
            dvec_ref[...] = jnp.zeros_like(dvec_ref)

        sig = _sigmoid(cg_ref[...])
        ext_ref[0:HALO_C, :] = jnp.where(first, 0.0, cvp_ref[...] * _sigmoid(cgp_ref[...]))
        ext_ref[HALO_C:HALO_C + tt, :] = cv_ref[...] * sig
        ext_ref[HALO_C + tt:, :] = cvn_ref[...] * _sigmoid(cgn_ref[...])
        _shift_copies(ext_ref, sh_ref, ne + HALO_C - 8)
        _tap_conv(a1_ref, ne, ext_ref, sh_ref, w_ref, _FWD_OFFS, b_ref)
        cz = jnp.concatenate([cz_ref[...], czn_ref[...]], axis=0)
        du_all = jnp.concatenate([du_ref[...], jnp.where(last, 0.0, dun_ref[...])], axis=0)
        xhat, rstd, s, ds, zc, dzc = _norm_act(a1_ref[...], cz, g_ref, bb_ref)
        dln = du_all * zc * ds
        dxhat = dln * g_ref[...]
        da1 = rstd * (dxhat - jnp.mean(dxhat, axis=-1, keepdims=True)
                      - xhat * jnp.mean(dxhat * xhat, axis=-1, keepdims=True))
        da1_ref[...] = da1
        dcz = (du_all * s * dzc)[:tt]
        dvec_ref[0:1, :] += jnp.sum(da1[:tt], axis=0, keepdims=True)
        dvec_ref[1:2, :] += jnp.sum((dln * xhat)[:tt], axis=0, keepdims=True)
        dvec_ref[2:3, :] += jnp.sum(dln[:tt], axis=0, keepdims=True)
        _tap_corr(dww_ref, tt, da1_ref, ext_ref, sh_ref, _FWD_OFFS)
        _shift_copies(da1_ref, sh_ref, ne - 8)
        _tap_conv(da0_ref, tt, da1_ref, sh_ref, w_ref, _BWD_OFFS)
        da0 = da0_ref[...]
        cv = cv_ref[...]
        dp_ref[:, 0:D] = (da0 * sig).astype(bf16)
        dp_ref[:, D:2 * D] = (da0 * cv * sig * (1.0 - sig)).astype(bf16)
        dp_ref[:, 2 * D:] = dcz.astype(bf16)

    def main(col):
        return pl.BlockSpec((tt, D), lambda i: (i, col))

    def prev(col):
        return pl.BlockSpec((HALO_C, D), lambda i: (jnp.maximum(i * hb - 1, 0), col))

    def nxt(col):
        return pl.BlockSpec((HALO_C, D), lambda i: (jnp.minimum((i + 1) * hb, last_hb), col))

    vec = pl.BlockSpec((1, D), lambda i: (0, 0))
    return pl.pallas_call(
        body, name="conv_bwd", grid=(n_tiles,),
        in_specs=[main(0), main(1), main(2), main(0), prev(0), prev(1), nxt(0), nxt(1), nxt(2), nxt(0),
                  pl.BlockSpec((HALO_C, D), lambda i: (0, 0)), vec, vec, vec],
        out_specs=[pl.BlockSpec((tt, 3 * D), lambda i: (i, 0)),
                   pl.BlockSpec((HALO_C, D), lambda i: (0, 0)),
                   pl.BlockSpec((8, D), lambda i: (0, 0))],
        out_shape=[jax.ShapeDtypeStruct((t, 3 * D), bf16), jax.ShapeDtypeStruct((HALO_C, D), f32),
                   jax.ShapeDtypeStruct((8, D), f32)],
        scratch_shapes=[pltpu.VMEM((ne + HALO_C, D), f32), pltpu.VMEM((N_SHIFT, ne + HALO_C - 8, D), f32),
                        pltpu.VMEM((ne, D), f32), pltpu.VMEM((ne, D), f32), pltpu.VMEM((tt, D), f32)],
        compiler_params=_cparams("arbitrary"),
    )(p_conv, p_conv, p_conv, du, p_conv, p_conv, p_conv, p_conv, p_conv, du, dw_w, dw_b, ln_g, ln_b)


def _head_selectors():
    r = lax.broadcasted_iota(jnp.int32, (HD, D), 0)
    c = lax.broadcasted_iota(jnp.int32, (HD, D), 1) // HD
    return jnp.where(r == c, 1.0, 0.0).astype(f32), jnp.where(r == c + NH, 1.0, 0.0).astype(f32)


def _dot_hi(a, b):
    return lax.dot_general(a, b, (((1,), (0,)), ((), ())), precision=HI, preferred_element_type=f32)


def _dot_hi_nt(a, b):
    return lax.dot_general(a, b, (((1,), (1,)), ((), ())), precision=HI, preferred_element_type=f32)


def _dot_hi_tn(a, b):
    return lax.dot_general(a, b, (((0,), (0,)), ((), ())), precision=HI, preferred_element_type=f32)


def _chunk_tri(n, lower):
    r = lax.broadcasted_iota(jnp.int32, (n, n), 0)
    c = lax.broadcasted_iota(jnp.int32, (n, n), 1)
    tri = (r >= c) if lower else (r <= c)
    return jnp.where(tri & (r // CH == c // CH), 1.0, 0.0).astype(f32)


def _softplus_and_sigmoid(x):
    e = jnp.exp(-jnp.abs(x))
    log1p = jnp.where(e < 1e-2, e * (1.0 - e * (0.5 - e * (1.0 / 3.0 - 0.25 * e))), jnp.log(1.0 + e))
    return jnp.maximum(x, 0.0) + log1p, _sigmoid(x)


def _gdn_short_conv(ext_ref, n_rows, w_ref):
    pre = jnp.zeros((n_rows, D), f32)
    for k in range(KG):
        pre = pre + w_ref[k:k + 1, :] * ext_ref[pl.ds(HALO_G - (KG - 1) + k, n_rows), :]
    return pre


def _l2norm_heads(act, scale):
    outs, rs = [], []
    for h in range(NH):
        a = act[:, h * HD:(h + 1) * HD]
        r = lax.rsqrt(jnp.sum(a * a, axis=-1, keepdims=True) + L2_EPS)
        outs.append(a * (r * scale))
        rs.append(jnp.broadcast_to(r, a.shape))
    return jnp.concatenate(outs, axis=-1), jnp.concatenate(rs, axis=-1)


def _gdn_pre_fwd(p_qkv, p_ba, cw, alog_b, dt_b, *, tt):
    t = p_qkv.shape[0]
    hb = tt // HALO_G

    def body(q_ref, k_ref, v_ref, qh_ref, kh_ref, vh_ref, ba_ref, wq_ref, wk_ref, wv_ref, al_ref, dt_ref,
             qn_ref, kn_ref, va_ref, gb_ref, bb_ref, ext_ref):
        first = pl.program_id(0) == 0

        def conv_act(x_ref, xh_ref, w_ref):
            ext_ref[0:HALO_G, :] = jnp.where(first, 0.0, xh_ref[...])
            ext_ref[HALO_G:, :] = x_ref[...]
            pre = _gdn_short_conv(ext_ref, tt, w_ref)
            return pre * _sigmoid(pre)

        qn_ref[...] = _l2norm_heads(conv_act(q_ref, qh_ref, wq_ref), HD ** -0.5)[0]
        kn_ref[...] = _l2norm_heads(conv_act(k_ref, kh_ref, wk_ref), 1.0)[0]
        va_ref[...] = conv_act(v_ref, vh_ref, wv_ref)
        sel_b, sel_a = _head_selectors()
        ba = ba_ref[...]
        bb_ref[...] = _sigmoid(_dot_hi(ba, sel_b))
        sp, _ = _softplus_and_sigmoid(_dot_hi(ba, sel_a) + dt_ref[...])
        gb_ref[...] = _dot_hi(_chunk_tri(tt, lower=True), -jnp.exp(al_ref[...]) * sp)

    def main(col):
        return pl.BlockSpec((tt, D), lambda i: (i, col))

    def prev(col):
        return pl.BlockSpec((HALO_G, D), lambda i: (jnp.maximum(i * hb - 1, 0), col))

    def wspec(col):
        return pl.BlockSpec((8, D), lambda i: (0, col))

    vec = pl.BlockSpec((1, D), lambda i: (0, 0))
    sds = jax.ShapeDtypeStruct((t, D), f32)
    return pl.pallas_call(
        body, name="gdn_pre_fwd", grid=(t // tt,),
        in_specs=[main(0), main(1), main(2), prev(0), prev(1), prev(2),
                  pl.BlockSpec((tt, HD), lambda i: (i, 0)), wspec(0), wspec(1), wspec(2), vec, vec],
        out_specs=[pl.BlockSpec((tt, D), lambda i: (i, 0))] * 5,
        out_shape=[sds] * 5,
        scratch_shapes=[pltpu.VMEM((tt + HALO_G, D), f32)],
        compiler_params=_cparams("parallel"),
    )(p_qkv, p_qkv, p_qkv, p_qkv, p_qkv, p_qkv, p_ba, cw, cw, cw, alog_b, dt_b)


def _gdn_pre_bwd(p_qkv, p_ba, cw, alog_b, dt_b, dqn, dkn, dva, dgb, dbb, *, tt):
    t = p_qkv.shape[0]
    hb = tt // HALO_G
    n_tiles = t // tt
    last_hb = t // HALO_G - 1
    ne = tt + HALO_G

    def body(q_ref, k_ref, v_ref, qp_ref, kp_ref, vp_ref, qx_ref, kx_ref, vx_ref,
             dq_ref, dk_ref, dv_ref, dqx_ref, dkx_ref, dvx_ref, ba_ref, dgb_ref, dbb_ref,
             wq_ref, wk_ref, wv_ref, al_ref, dt_ref,
             dp_ref, dba_ref, dcw_ref, dad_ref, ext_ref, dpre_ref):
        i = pl.program_id(0)
        first = i == 0
        last = i == n_tiles - 1

        @pl.when(first)
        def _():
            dcw_ref[...] = jnp.zeros_like(dcw_ref)
            dad_ref[...] = jnp.zeros_like(dad_ref)

        def one(x_ref, xp_ref, xx_ref, d_ref, dx_ref, w_ref, col, scale):
            ext_ref[0:HALO_G, :] = jnp.where(first, 0.0, xp_ref[...])
            ext_ref[HALO_G:HALO_G + tt, :] = x_ref[...]
            ext_ref[HALO_G + tt:, :] = xx_ref[...]
            pre = _gdn_short_conv(ext_ref, ne, w_ref)
            act, dact = _silu_and_grad(pre)
            d_out = jnp.concatenate([d_ref[...], jnp.where(last, 0.0, dx_ref[...])], axis=0)
            if scale is None:
                d_act = d_out
            else:
                parts = []
                for h in range(NH):
                    a = act[:, h * HD:(h + 1) * HD]
                    dn = d_out[:, h * HD:(h + 1) * HD]
                    r = lax.rsqrt(jnp.sum(a * a, axis=-1, keepdims=True) + L2_EPS)
                    parts.append(scale * r * (dn - a * (r * r) * jnp.sum(dn * a, axis=-1, keepdims=True)))
                d_act = jnp.concatenate(parts, axis=-1)
            dpre = d_act * dact
            dpre_ref[...] = dpre
            dpre_t = dpre[:tt]
            draw = jnp.zeros((tt, D), f32)
            for k in range(KG):
                draw = draw + w_ref[k:k + 1, :] * dpre_ref[pl.ds(KG - 1 - k, tt), :]
                dcw_ref[k:k + 1, col * D:(col + 1) * D] += jnp.sum(
                    dpre_t * ext_ref[pl.ds(HALO_G - (KG - 1) + k, tt), :], axis=0, keepdims=True)
            dp_ref[:, col * D:(col + 1) * D] = draw.astype(bf16)

        one(q_ref, qp_ref, qx_ref, dq_ref, dqx_ref, wq_ref, 0, HD ** -0.5)
        one(k_ref, kp_ref, kx_ref, dk_ref, dkx_ref, wk_ref, 1, 1.0)
        one(v_ref, vp_ref, vx_ref, dv_ref, dvx_ref, wv_ref, 2, None)

        sel_b, sel_a = _head_selectors()
        ba = ba_ref[...]
        beta = _sigmoid(_dot_hi(ba, sel_b))
        sp, sg = _softplus_and_sigmoid(_dot_hi(ba, sel_a) + dt_ref[...])
        neg_a = -jnp.exp(al_ref[...])
        dgb_v = _dot_hi(_chunk_tri(tt, lower=False), dgb_ref[...])
        d_bl = dbb_ref[...] * beta * (1.0 - beta)
        d_al = dgb_v * neg_a * sg
        dba_ref[...] = _dot_hi_nt(d_bl, sel_b) + _dot_hi_nt(d_al, sel_a)
        d_alog = jnp.sum(dgb_v * neg_a * sp, axis=0, keepdims=True)
        d_dt = jnp.sum(d_al, axis=0, keepdims=True)
        dad_ref[0:1, :] += _dot_hi_nt(d_alog, sel_b)
        dad_ref[1:2, :] += _dot_hi_nt(d_dt, sel_b)

    def main(col):
        return pl.BlockSpec((tt, D), lambda i: (i, col))

    def prev(col):
        return pl.BlockSpec((HALO_G, D), lambda i: (jnp.maximum(i * hb - 1, 0), col))

    def nxt(col):
        return pl.BlockSpec((HALO_G, D), lambda i: (jnp.minimum((i + 1) * hb, last_hb), col))

    def wspec(col):
        return pl.BlockSpec((8, D), lambda i: (0, col))

    vec = pl.BlockSpec((1, D), lambda i: (0, 0))
    return pl.pallas_call(
        body, name="gdn_pre_bwd", grid=(n_tiles,),
        in_specs=[main(0), main(1), main(2), prev(0), prev(1), prev(2), nxt(0), nxt(1), nxt(2),
                  main(0), main(0), main(0), nxt(0), nxt(0), nxt(0),
                  pl.BlockSpec((tt, HD), lambda i: (i, 0)), main(0), main(0),
                  wspec(0), wspec(1), wspec(2), vec, vec],
        out_specs=[pl.BlockSpec((tt, 3 * D), lambda i: (i, 0)), pl.BlockSpec((tt, HD), lambda i: (i, 0)),
                   pl.BlockSpec((8, 3 * D), lambda i: (0, 0)), pl.BlockSpec((8, HD), lambda i: (0, 0))],
        out_shape=[jax.ShapeDtypeStruct((t, 3 * D), bf16), jax.ShapeDtypeStruct((t, HD), f32),
                   jax.ShapeDtypeStruct((8, 3 * D), f32), jax.ShapeDtypeStruct((8, HD), f32)],
        scratch_shapes=[pltpu.VMEM((HALO_G + tt + HALO_G, D), f32), pltpu.VMEM((ne, D), f32)],
        compiler_params=_cparams("arbitrary"),
    )(p_qkv, p_qkv, p_qkv, p_qkv, p_qkv, p_qkv, p_qkv, p_qkv, p_qkv,
      dqn, dkn, dva, dqn, dkn, dva, p_ba, dgb, dbb, cw, cw, cw, alog_b, dt_b)


def _dot_b(a, b, dims):
    return lax.dot_general(a.astype(bf16), b.astype(bf16), (dims, ((), ())), preferred_element_type=f32)


_NN = ((1,), (0,))
_NT = ((1,), (1,))
_TN = ((0,), (0,))


def _chunk_fn(qs, ks, vs, gcs, bbs, ss):
    heads = range(len(qs))
    r = lax.broadcasted_iota(jnp.int32, (CH, CH), 0)
    c = lax.broadcasted_iota(jnp.int32, (CH, CH), 1)
    causal = r >= c
    strict = r > c
    eye = jnp.where(r == c, 1.0, 0.0).astype(f32)
    gc_row = [gcs[h].T[:CH, :] for h in heads]
    decay = [jnp.where(causal, jnp.exp(jnp.where(causal, gcs[h][:, :CH] - gc_row[h], 0.0)), 0.0) for h in heads]
    kb = [ks[h] * bbs[h] for h in heads]
    egc = [jnp.exp(gcs[h]) for h in heads]
    kk = [_dot_b(kb[h], ks[h], _NT) for h in heads]
    qk = [_dot_b(qs[h], ks[h], _NT) for h in heads]
    m = [-jnp.where(strict, kk[h] * decay[h], 0.0) for h in heads]
    p = [eye + m[h] for h in heads]
    mp = m
    for _ in range(5):
        mp = [_dot_b(mp[h], mp[h], _NN) for h in heads]
        pm = [_dot_b(p[h], mp[h], _NN) for h in heads]
        p = [p[h] + pm[h] for h in heads]
    u = [_dot_b(p[h], vs[h] * bbs[h], _NN) for h in heads]
    w = [_dot_b(p[h], kb[h] * egc[h], _NN) for h in heads]
    intra = [jnp.where(causal, qk[h] * decay[h], 0.0) for h in heads]
    g_last = [gcs[h][CH - 1:CH, :] for h in heads]
    k_dec = [ks[h] * jnp.exp(g_last[h] - gcs[h]) for h in heads]
    ws = [_dot_b(w[h], ss[h], _NN) for h in heads]
    qs_s = [_dot_b(qs[h] * egc[h], ss[h], _NN) for h in heads]
    v_new = [u[h] - ws[h] for h in heads]
    iv = [_dot_b(intra[h], v_new[h], _NN) for h in heads]
    kv = [_dot_b(k_dec[h], v_new[h], _TN) for h in heads]
    o = tuple(qs_s[h] + iv[h] for h in heads)
    s_new = tuple(ss[h] * jnp.exp(g_last[h]) + kv[h] for h in heads)
    return o, s_new


def _head_cols():
    return [slice(h * HD, (h + 1) * HD) for h in range(NH)]


def _gdn_scan_fwd(qn, kn, va, gb, bb, *, tt):
    t = qn.shape[0]
    cpb = tt // CH

    def body(q_ref, k_ref, v_ref, g_ref, b_ref, o_ref, st_ref, s_scr):
        @pl.when(pl.program_id(0) == 0)
        def _():
            s_scr[...] = jnp.zeros_like(s_scr)

        def step(ci, carry):
            rows = pl.ds(pl.multiple_of(ci * CH, CH), CH)
            cols = _head_cols()
            ss = tuple(s_scr[h] for h in range(NH))
            for h in range(NH):
                st_ref[ci, h] = ss[h]
            o, s_new = _chunk_fn(*(tuple(ref[rows, cl] for cl in cols) for ref in (q_ref, k_ref, v_ref, g_ref, b_ref)), ss)
            for h in range(NH):
                o_ref[rows, cols[h]] = o[h]
                s_scr[h] = s_new[h]
            return carry

        lax.fori_loop(0, cpb, step, 0)

    blk = pl.BlockSpec((tt, D), lambda i: (i, 0))
    return pl.pallas_call(
        body, name="gdn_scan_fwd", grid=(t // tt,),
        in_specs=[blk] * 5,
        out_specs=[blk, pl.BlockSpec((cpb, NH, HD, HD), lambda i: (i, 0, 0, 0))],
        out_shape=[jax.ShapeDtypeStruct((t, D), f32), jax.ShapeDtypeStruct((t // CH, NH, HD, HD), f32)],
        scratch_shapes=[pltpu.VMEM((NH, HD, HD), f32)],
        compiler_params=_cparams("arbitrary"),
    )(qn, kn, va, gb, bb)


def _gdn_scan_bwd(qn, kn, va, gb, bb, states, do, *, tt):
    t = qn.shape[0]
    nblk = t // tt
    cpb = tt // CH

    def body(q_ref, k_ref, v_ref, g_ref, b_ref, st_ref, do_ref, dq_ref, dk_ref, dv_ref, dg_ref, db_ref, ds_scr):
        @pl.when(pl.program_id(0) == 0)
        def _():
            ds_scr[...] = jnp.zeros_like(ds_scr)

        def step(j, carry):
            ci = cpb - 1 - j
            rows = pl.ds(pl.multiple_of(ci * CH, CH), CH)
            cols = _head_cols()
            _, vjp = jax.vjp(_chunk_fn, *(tuple(ref[rows, cl] for cl in cols) for ref in (q_ref, k_ref, v_ref, g_ref, b_ref)),
                             tuple(st_ref[ci, h] for h in range(NH)))
            grads = vjp((tuple(do_ref[rows, cl] for cl in cols), tuple(ds_scr[h] for h in range(NH))))
            for h in range(NH):
                for ref, g in zip((dq_ref, dk_ref, dv_ref, dg_ref, db_ref), grads[:5]):
                    ref[rows, cols[h]] = g[h]
                ds_scr[h] = grads[5][h]
            return carry

        lax.fori_loop(0, cpb, step, 0)

    blk = pl.BlockSpec((tt, D), lambda i: (nblk - 1 - i, 0))
    sblk = pl.BlockSpec((cpb, NH, HD, HD), lambda i: (nblk - 1 - i, 0, 0, 0))
    sds = jax.ShapeDtypeStruct((t, D), f32)
    return pl.pallas_call(
        body, name="gdn_scan_bwd", grid=(nblk,),
        in_specs=[blk] * 5 + [sblk, blk],
        out_specs=[blk] * 5, out_shape=[sds] * 5,
        scratch_shapes=[pltpu.VMEM((NH, HD, HD), f32)],
        compiler_params=_cparams("arbitrary"),
    )(qn, kn, va, gb, bb, states, do)


def _rms_heads(o):
    ons, rs = [], []
    for h in range(NH):
        a = o[:, h * HD:(h + 1) * HD]
        r = lax.rsqrt(jnp.mean(a * a, axis=-1, keepdims=True) + RMS_EPS)
        ons.append(a * r)
        rs.append(jnp.broadcast_to(r, a.shape))
    return jnp.concatenate(ons, axis=-1), jnp.concatenate(rs, axis=-1)


def _gdn_post_fwd(o, p_gz, ng_b, *, tt):
    t = o.shape[0]

    def body(o_ref, gz_ref, ng_ref, og_ref):
        on, _ = _rms_heads(o_ref[...])
        z, _ = _silu_and_grad(gz_ref[...])
        og_ref[...] = (on * ng_ref[...] * z).astype(bf16)

    blk = pl.BlockSpec((tt, D), lambda i: (i, 0))
    return pl.pallas_call(
        body, name="gdn_post_fwd", grid=(t // tt,),
        in_specs=[blk, blk, pl.BlockSpec((1, D), lambda i: (0, 0))],
        out_specs=blk, out_shape=jax.ShapeDtypeStruct((t, D), bf16),
        compiler_params=_cparams("parallel"),
    )(o, p_gz, ng_b)


def _gdn_post_bwd(o, p_gz, ng_b, dog, *, tt):
    t = o.shape[0]

    def body(o_ref, gz_ref, ng_ref, dog_ref, do_ref, dgz_ref, dng_ref):
        @pl.when(pl.program_id(0) == 0)
        def _():
            dng_ref[...] = jnp.zeros_like(dng_ref)

        on, r = _rms_heads(o_ref[...])
        z, dz = _silu_and_grad(gz_ref[...])
        dog_v = dog_ref[...]
        ng = ng_ref[...]
        dgz_ref[...] = (dog_v * on * ng * dz).astype(bf16)
        dy = dog_v * z
        dng_all = jnp.sum(dy * on, axis=0, keepdims=True)
        dng = dng_all[:, 0:HD]
        for h in range(1, NH):
            dng = dng + dng_all[:, h * HD:(h + 1) * HD]
        dng_ref[0:1, :] += dng
        don = dy * ng
        prod = don * on
        parts = []
        for h in range(NH):
            sl = slice(h * HD, (h + 1) * HD)
            parts.append(don[:, sl] - on[:, sl] * jnp.mean(prod[:, sl], axis=-1, keepdims=True))
        do_ref[...] = r * jnp.concatenate(parts, axis=-1)

    blk = pl.BlockSpec((tt, D), lambda i: (i, 0))
    return pl.pallas_call(
        body, name="gdn_post_bwd", grid=(t // tt,),
        in_specs=[blk, blk, pl.BlockSpec((1, D), lambda i: (0, 0)), blk],
        out_specs=[blk, blk, pl.BlockSpec((8, HD), lambda i: (0, 0))],
        out_shape=[jax.ShapeDtypeStruct((t, D), f32), jax.ShapeDtypeStruct((t, D), bf16),
                   jax.ShapeDtypeStruct((8, HD), f32)],
        compiler_params=_cparams("arbitrary"),
    )(o, p_gz, ng_b, dog)


def _merge(x, y_conf, y_gdn, p_gates, target, w_o, w_o_t, ln_g, ln_b, *, tt):
    t = x.shape[0]

    def body(x_ref, yc_ref, yg_ref, gc_ref, gg_ref, tg_ref, w_ref, wt_ref, g_ref, b_ref,
             loss_ref, dxd_ref, dyc_ref, dyg_ref, dpg_ref, h_ref, dz_ref, dvec_ref):
        @pl.when(pl.program_id(0) == 0)
        def _():
            loss_ref[...] = jnp.zeros_like(loss_ref)
            dvec_ref[...] = jnp.zeros_like(dvec_ref)

        sc = _sigmoid(gc_ref[...])
        sg = _sigmoid(gg_ref[...])
        yc = yc_ref[...]
        yg = yg_ref[...]
        h = (sc * yc + sg * yg).astype(bf16)
        h_ref[...] = h
        z = DN_ALPHA * x_ref[...] + jnp.dot(h, w_ref[...], preferred_element_type=f32)
        mu = jnp.mean(z, axis=-1, keepdims=True)
        cen = z - mu
        rstd = lax.rsqrt(jnp.mean(cen * cen, axis=-1, keepdims=True) + LN_EPS)
        xhat = cen * rstd
        err = xhat * g_ref[...] + b_ref[...] - tg_ref[...]
        loss_ref[...] += 0.5 / D * jnp.sum(err * err)
        dy = err * (1.0 / D)
        dvec_ref[0:1, :] += jnp.sum(dy * xhat, axis=0, keepdims=True)
        dvec_ref[1:2, :] += jnp.sum(dy, axis=0, keepdims=True)
        dxhat = dy * g_ref[...]
        dz = rstd * (dxhat - jnp.mean(dxhat, axis=-1, keepdims=True)
                     - xhat * jnp.mean(dxhat * xhat, axis=-1, keepdims=True))
        dxd_ref[...] = DN_ALPHA * dz
        dz_b = dz.astype(bf16)
        dz_ref[...] = dz_b
        dh = jnp.dot(dz_b, wt_ref[...], preferred_element_type=f32)
        dyc_ref[...] = (dh * sc).astype(bf16)
        dyg_ref[...] = (dh * sg).astype(bf16)
        dpg_ref[:, 0:D] = (dh * yc * sc * (1.0 - sc)).astype(bf16)
        dpg_ref[:, D:] = (dh * yg * sg * (1.0 - sg)).astype(bf16)

    blk = pl.BlockSpec((tt, D), lambda i: (i, 0))
    wblk = pl.BlockSpec((D, D), lambda i: (0, 0))
    vec = pl.BlockSpec((1, D), lambda i: (0, 0))
    return pl.pallas_call(
        body, name="merge_norm_loss", grid=(t // tt,),
        in_specs=[blk, blk, blk, pl.BlockSpec((tt, D), lambda i: (i, 0)), pl.BlockSpec((tt, D), lambda i: (i, 1)),
                  blk, wblk, wblk, vec, vec],
        out_specs=[pl.BlockSpec((8, HD), lambda i: (0, 0)), blk, blk, blk,
                   pl.BlockSpec((tt, 2 * D), lambda i: (i, 0)), blk, blk, pl.BlockSpec((8, D), lambda i: (0, 0))],
        out_shape=[jax.ShapeDtypeStruct((8, HD), f32), jax.ShapeDtypeStruct((t, D), f32),
                   jax.ShapeDtypeStruct((t, D), bf16), jax.ShapeDtypeStruct((t, D), bf16),
                   jax.ShapeDtypeStruct((t, 2 * D), bf16), jax.ShapeDtypeStruct((t, D), bf16),
                   jax.ShapeDtypeStruct((t, D), bf16), jax.ShapeDtypeStruct((8, D), f32)],
        compiler_params=_cparams("arbitrary"),
    )(x, y_conf, y_gdn, p_gates, p_gates, target, w_o, w_o_t, ln_g, ln_b)


def _place():
    return lax.axis_index("x"), lax.axis_index("y"), lax.axis_index("c")


def _any_specs(n):
    return [pl.BlockSpec(memory_space=pl.ANY)] * n


def _sibling_merge(arrs, name, take_other_half=False):
    k = len(arrs)

    def body(*refs):
        a_refs, o_refs = refs[:k], refs[k:2 * k]
        send_sems, recv_sems, local_sems = refs[2 * k:]
        x, y, c = _place()
        sends, owns = [], []
        for i in range(k):
            if take_other_half:
                n = a_refs[i].shape[-2] // 2
                lead = (slice(None),) * (len(a_refs[i].shape) - 2)
                src = a_refs[i].at[lead + (pl.ds((1 - c) * n, n), slice(None))]
                dst = o_refs[i]
            else:
                src = a_refs[i]
                dst = o_refs[i].at[c]
                own = pltpu.make_async_copy(src, dst, local_sems.at[i])
                own.start()
                owns.append(own)
            cp = pltpu.make_async_remote_copy(src_ref=src, dst_ref=dst, send_sem=send_sems.at[i],
                                              recv_sem=recv_sems.at[i], device_id=(x, y, 1 - c), device_id_type=MESH)
            cp.start()
            sends.append(cp)
        for cp in sends:
            cp.wait()
        for own in owns:
            own.wait()

    def out_sds(a):
        if take_other_half:
            return jax.ShapeDtypeStruct(a.shape[:-2] + (a.shape[-2] // 2, a.shape[-1]), a.dtype)
        return jax.ShapeDtypeStruct((2,) + a.shape, a.dtype)

    return pl.pallas_call(
        body, name=name, in_specs=_any_specs(k), out_specs=_any_specs(k),
        out_shape=[out_sds(a) for a in arrs],
        scratch_shapes=[pltpu.SemaphoreType.DMA((k,)), pltpu.SemaphoreType.DMA((k,)), pltpu.SemaphoreType.DMA((k,))],
    )(*arrs)


def _chip_exchange(arrs, name, scatter):
    k = len(arrs)

    def body(*refs):
        a_refs, o_refs = refs[:k], refs[k:2 * k]
        send_sems, recv_sems, local_sems = refs[2 * k:]
        x, y, c = _place()
        me = 2 * x + y
        peers = [(1 - x, y), (x, 1 - y), (1 - x, 1 - y)]

        def src(i, j):
            return a_refs[i].at[j] if scatter else a_refs[i]

        def copy(i, n, send_j, slot):
            px, py = peers[n]
            return pltpu.make_async_remote_copy(
                src_ref=src(i, send_j), dst_ref=o_refs[i].at[slot], send_sem=send_sems.at[3 * i + n],
                recv_sem=recv_sems.at[3 * i + n], device_id=(px, py, c), device_id_type=MESH)

        owns = [pltpu.make_async_copy(src(i, me), o_refs[i].at[me], local_sems.at[i]) for i in range(k)]
        for own in owns:
            own.start()
        sends = [copy(i, n, 2 * peers[n][0] + peers[n][1], me) for n in range(3) for i in range(k)]
        for cp in sends:
            cp.start()
        for n in range(3):
            for i in range(k):
                copy(i, n, me, 2 * peers[n][0] + peers[n][1]).wait_recv()
        for cp in sends:
            cp.wait_send()
        for own in owns:
            own.wait()

    def out_sds(a):
        return jax.ShapeDtypeStruct((N_CHIPS,) + tuple(a.shape[1:] if scatter else a.shape), a.dtype)

    return pl.pallas_call(
        body, name=name, in_specs=_any_specs(k), out_specs=_any_specs(k),
        out_shape=[out_sds(a) for a in arrs],
        scratch_shapes=[pltpu.SemaphoreType.DMA((3 * k,)), pltpu.SemaphoreType.DMA((3 * k,)),
                        pltpu.SemaphoreType.DMA((k,))],
    )(*arrs)


def _pair_sum(g_all, got, c_arr, name, out_dtype):
    n, w = got.shape[1:]
    tile = n // 4
    n_tiles = n // tile

    def body(c_ref, a_ref, b_ref, o_ref):
        o_ref[...] = (a_ref[...] + b_ref[...]).astype(out_dtype)

    return pl.pallas_call(
        body, name=name,
        grid_spec=pltpu.PrefetchScalarGridSpec(
            num_scalar_prefetch=1, grid=(N_CHIPS, n_tiles),
            in_specs=[pl.BlockSpec((1, tile, w), lambda j, i, c_ref: (j, c_ref[0] * n_tiles + i, 0)),
                      pl.BlockSpec((1, tile, w), lambda j, i, c_ref: (j, i, 0))],
            out_specs=pl.BlockSpec((1, tile, w), lambda j, i, c_ref: (j, i, 0))),
        out_shape=jax.ShapeDtypeStruct(got.shape, out_dtype),
        compiler_params=_cparams("parallel", "parallel"),
    )(c_arr, g_all, got)


def _sum_slots(a, name):
    n, w = a.shape[1:]
    tile = n // 4

    def body(a_ref, o_ref):
        o_ref[...] = ((a_ref[0].astype(f32) + a_ref[1].astype(f32)) + a_ref[2].astype(f32)) + a_ref[3].astype(f32)

    return pl.pallas_call(
        body, name=name, grid=(n // tile,),
        in_specs=[pl.BlockSpec((N_CHIPS, tile, w), lambda i: (0, i, 0))],
        out_specs=pl.BlockSpec((tile, w), lambda i: (i, 0)),
        out_shape=jax.ShapeDtypeStruct((n, w), f32),
        compiler_params=_cparams("parallel"),
    )(a)


def _adamw(w, g, m, v, name):
    rows, width = w.shape
    tile = rows // 8
    c1 = 1.0 / (1.0 - ADAM_B1 ** ADAM_STEP)
    c2 = 1.0 / (1.0 - ADAM_B2 ** ADAM_STEP)

    def body(w_ref, g_ref, m_ref, v_ref, d_ref, mo_ref, vo_ref):
        g_v = g_ref[...]
        m_new = ADAM_B1 * m_ref[...] + (1.0 - ADAM_B1) * g_v
        v_new = ADAM_B2 * v_ref[...] + (1.0 - ADAM_B2) * (g_v * g_v)
        mo_ref[...] = m_new
        vo_ref[...] = v_new
        d_ref[...] = -ADAM_LR * ((m_new * c1) / (jnp.sqrt(v_new * c2) + ADAM_EPS) + ADAM_WD * w_ref[...])

    blk = pl.BlockSpec((tile, width), lambda i: (i, 0))
    sds = jax.ShapeDtypeStruct((rows, width), f32)
    return pl.pallas_call(
        body, name=name, grid=(rows // tile,),
        in_specs=[blk] * 4, out_specs=[blk] * 3, out_shape=[sds] * 3,
        compiler_params=_cparams("parallel"),
    )(w, g, m, v)


R_DW = 3 * SQ_BLK
R_CW = R_DW + 8
R_VEC = R_CW + 8
R_SMALL = R_VEC + 8
REST_ROWS = 896


def _pack_small(conf_dw_w, gdn_conv_w, vecs, a_log, dt_bias, norm_g):
    dw = jnp.pad(conf_dw_w.reshape(-1), (0, 8 * D - KC * SQ_BLK)).reshape(8, D)
    cw = jnp.pad(gdn_conv_w.reshape(-1), (0, 5 * D)).reshape(8, D)
    vec = jnp.pad(jnp.stack(vecs), ((0, 3), (0, 0)))
    small = jnp.pad(jnp.concatenate([a_log, dt_bias, norm_g]), (0, D - 2 * NH - HD)).reshape(1, D)
    return jnp.pad(jnp.concatenate([dw, cw, vec, small], axis=0), ((0, REST_ROWS - R_SMALL - 1), (0, 0)))


def _pack_rest(conf_w_out, gdn_w_out, w_o, small):
    return jnp.concatenate([conf_w_out, gdn_w_out, w_o, small], axis=0)


def _unpack_rest(p):
    conf_dw_w = p[R_DW:R_DW + 8].reshape(-1)[:KC * SQ_BLK].reshape(KC, SQ_BLK)
    gdn_conv_w = p[R_CW:R_CW + 3].reshape(KG, 3 * SQ_BLK)
    small = p[R_SMALL]
    return dict(conf_w_out=p[0:SQ_BLK], gdn_w_out=p[SQ_BLK:2 * SQ_BLK], w_o=p[2 * SQ_BLK:R_DW],
                conf_dw_w=conf_dw_w, gdn_conv_w=gdn_conv_w, conf_dw_b=p[R_VEC], conf_ln_g=p[R_VEC + 1],
                conf_ln_b=p[R_VEC + 2], post_ln_g=p[R_VEC + 3], post_ln_b=p[R_VEC + 4],
                gdn_A_log=small[0:NH], gdn_dt_bias=small[NH:2 * NH], gdn_norm_g=small[2 * NH:2 * NH + HD])


_WEIGHT_ORDER = ("w_in", "conf_dw_w", "conf_dw_b", "conf_ln_g", "conf_ln_b", "conf_w_out", "gdn_conv_w",
                 "gdn_A_log", "gdn_dt_bias", "gdn_norm_g", "gdn_w_out", "w_o", "post_ln_g", "post_ln_b")


def _gather_weights(w_in, conf_w_out, gdn_w_out, w_o, conf_dw_w, gdn_conv_w):
    c = lax.axis_index("c")
    sq = jnp.concatenate([conf_w_out, gdn_w_out, w_o], axis=0).astype(bf16)
    w_half = lax.dynamic_slice_in_dim(w_in.astype(bf16), c * (D // 2), D // 2, axis=0)
    sq_half = lax.dynamic_slice_in_dim(sq, c * (sq.shape[0] // 2), sq.shape[0] // 2, axis=0)
    small = jnp.concatenate([jnp.pad(conf_dw_w.reshape(-1), (0, 8 * D - KC * SQ_BLK)).reshape(8, D),
                             jnp.pad(gdn_conv_w.reshape(-1), (0, 5 * D)).reshape(8, D)], axis=0)
    got_w, got_sq, small_all = _chip_exchange([w_half, sq_half, small], "weight_gather_chips", scatter=False)
    w2, sq2 = _sibling_merge([got_w, got_sq], "weight_gather_sibling")
    w4 = jnp.concatenate([w2[0], w2[1]], axis=1)
    sq4 = jnp.concatenate([sq2[0], sq2[1]], axis=1)
    sq_full = [sq4[:, n * SQ_BLK:(n + 1) * SQ_BLK].reshape(D, D) for n in range(3)]
    dw_full = small_all[:, 0:8].reshape(N_CHIPS, 8 * D)[:, :KC * SQ_BLK].reshape(N_CHIPS, KC, SQ_BLK)
    dw_full = dw_full.transpose(1, 0, 2).reshape(KC, D)
    cw_full = small_all[:, 8:11].reshape(N_CHIPS, KG, 3 * SQ_BLK).transpose(1, 0, 2).reshape(KG, 3 * D)
    return w4, sq_full[0], sq_full[1], sq_full[2], dw_full, cw_full


def _w_in_cols(w4, lo, hi):
    parts = []
    for j in range(N_CHIPS):
        a, b = max(lo, j * W_IN_BLK), min(hi, (j + 1) * W_IN_BLK)
        if a < b:
            parts.append(w4[j, :, a - j * W_IN_BLK:b - j * W_IN_BLK])
    return parts[0] if len(parts) == 1 else jnp.concatenate(parts, axis=1)


def _w_in_by_chip(pieces):
    chips = []
    for j in range(N_CHIPS):
        lo, hi = j * W_IN_BLK, (j + 1) * W_IN_BLK
        parts = []
        for start, arr in pieces:
            a, b = max(lo, start), min(hi, start + arr.shape[1])
            if a < b:
                parts.append(arr[:, a - start:b - start])
        chips.append(jnp.concatenate(parts, axis=1))
    return jnp.stack(chips)


def _reduce_scatter(g_w, g_rest):
    c_arr = lax.axis_index("c").astype(jnp.int32).reshape(1)
    got_w, got_r = _sibling_merge([g_w, g_rest], "grad_sibling_halves", take_other_half=True)
    pair_w = _pair_sum(g_w, got_w, c_arr, "grad_pair_sum_w_in", bf16)
    pair_r = _pair_sum(g_rest, got_r, c_arr, "grad_pair_sum_rest", f32)
    all_w, all_r = _chip_exchange([pair_w, pair_r], "grad_chip_scatter", scatter=True)
    tot_w, tot_r = _sibling_merge([_sum_slots(all_w, "grad_chip_sum_w_in"), _sum_slots(all_r, "grad_chip_sum_rest")],
                                  "grad_sibling_result")
    return tot_w.reshape(D, W_IN_BLK), tot_r.reshape(REST_ROWS, D)


def _local_step(x2, tgt, w4, wc_out, wg_out, wo_full, dw_full, cw_full, conf_dw_b, conf_ln_g, conf_ln_b,
                gdn_A_log, gdn_dt_bias, gdn_norm_g, post_ln_g, post_ln_b):
    t = x2.shape[0]
    tt = min(TOKEN_TILE, t)
    tm = min(512, t)

    w_conv, w_qkv, w_gz = _w_in_cols(w4, 0, 3 * D), _w_in_cols(w4, 3 * D, 6 * D), _w_in_cols(w4, 6 * D, 7 * D)
    w_gates = _w_in_cols(w4, 7 * D + 2 * NH, W_IN_COLS)
    dw_pad = jnp.pad(dw_full, ((0, HALO_C - KC), (0, 0)))
    cw_pad = jnp.pad(cw_full, ((0, 8 - KG), (0, 0)))
    row = lambda v: v.reshape(1, D)
    alog_b = row(jnp.repeat(gdn_A_log, HD))
    dt_b = row(jnp.repeat(gdn_dt_bias, HD))
    ng_b = row(jnp.tile(gdn_norm_g, NH))
    w_ba = jnp.pad(_w_in_cols(w4, 7 * D, 7 * D + 2 * NH), ((0, 0), (0, HD - 2 * NH)))

    x_b = x2.astype(bf16)
    x_t = x_b.T

    p_conv = _mm_multi([x_b], [w_conv], out_dtype=f32, tm=tm, tn=1024, name="proj_conv")
    p_qkv = _mm_multi([x_b], [w_qkv], out_dtype=f32, tm=tm, tn=1024, name="proj_qkv")
    p_gz = _mm_multi([x_b], [w_gz], out_dtype=f32, tm=tm, tn=1024, name="proj_gz")
    p_gates = _mm_multi([x_b], [w_gates], out_dtype=f32, tm=tm, tn=1024, name="proj_gates")
    p_ba = _mm_multi([x_b], [w_ba], out_dtype=f32, tm=tm, tn=HD, name="proj_ba")

    u = _conv_fwd(p_conv, dw_pad, row(conf_dw_b), row(conf_ln_g), row(conf_ln_b), tt=tt)
    y_conf = _mm_multi([u], [wc_out], out_dtype=f32, tm=tm, tn=1024, name="conf_out")

    qn, kn, va, gb, bb = _gdn_pre_fwd(p_qkv, p_ba, cw_pad, alog_b, dt_b, tt=tt)
    o, states = _gdn_scan_fwd(qn, kn, va, gb, bb, tt=tt)
    og = _gdn_post_fwd(o, p_gz, ng_b, tt=tt)
    y_gdn = _mm_multi([og], [wg_out], out_dtype=f32, tm=tm, tn=1024, name="gdn_out")

    loss_blk, dxd, dyc, dyg, dp_gates, h, dz, dpost = _merge(
        x2, y_conf, y_gdn, p_gates, tgt, wo_full, wo_full.T, row(post_ln_g), row(post_ln_b), tt=tt)

    d_wo = _mm_kloop(h.T, dz, tm=tm, tn=1024, tk=min(1024, t), name="grad_w_o")
    du = _mm_multi([dyc], [wc_out.T], out_dtype=f32, tm=tm, tn=1024, name="conf_out_bwd")
    d_wc = _mm_kloop(u.T, dyc, tm=tm, tn=1024, tk=min(1024, t), name="grad_conf_w_out")
    dog = _mm_multi([dyg], [wg_out.T], out_dtype=f32, tm=tm, tn=1024, name="gdn_out_bwd")
    d_wg = _mm_kloop(og.T, dyg, tm=tm, tn=1024, tk=min(1024, t), name="grad_gdn_w_out")

    dp_conv, d_dww, dconv_vec = _conv_bwd(p_conv, du, dw_pad, row(conf_dw_b), row(conf_ln_g), row(conf_ln_b), tt=tt)

    do, dp_gz, dng = _gdn_post_bwd(o, p_gz, ng_b, dog, tt=tt)
    dqn, dkn, dva, dgb, dbb = _gdn_scan_bwd(qn, kn, va, gb, bb, states, do, tt=tt)
    dp_qkv, dp_ba, d_cw, d_ad = _gdn_pre_bwd(p_qkv, p_ba, cw_pad, alog_b, dt_b, dqn, dkn, dva, dgb, dbb, tt=tt)
    dp_ba_b = dp_ba.astype(bf16)

    grad_x = _mm_multi(
        [dp_conv, dp_qkv, dp_gz, dp_gates, dp_ba_b],
        [w_conv.T, w_qkv.T, w_gz.T, w_gates.T, w_ba.T], dxd,
        out_dtype=f32, tm=min(256, t), tn=512, name="grad_x")

    tk = min(1024, t)
    d_w_conv = _mm_kloop(x_t, dp_conv, tm=tm, tn=1024, tk=tk, name="grad_w_in_conv")
    d_w_qkv = _mm_kloop(x_t, dp_qkv, tm=tm, tn=1024, tk=tk, name="grad_w_in_qkv")
    d_w_gz = _mm_kloop(x_t, dp_gz, tm=tm, tn=1024, tk=tk, name="grad_w_in_gz")
    d_w_gates = _mm_kloop(x_t, dp_gates, tm=tm, tn=1024, tk=tk, name="grad_w_in_gates")
    d_w_ba = _mm_kloop(x_t, dp_ba_b, tm=tm, tn=HD, tk=tk, name="grad_w_in_ba")
    d_w_in = _w_in_by_chip([(0, d_w_conv), (3 * D, d_w_qkv), (6 * D, d_w_gz), (7 * D, d_w_ba[:, :2 * NH]),
                            (7 * D + 2 * NH, d_w_gates)])

    return (loss_blk[0, 0], grad_x, d_w_in, d_wc, d_wg, d_wo, d_dww, d_cw, dconv_vec, dpost, d_ad, dng)


def kernel(x, w_in, conf_dw_w, conf_dw_b, conf_ln_g, conf_ln_b, conf_w_out, gdn_conv_w, gdn_A_log, gdn_dt_bias, gdn_norm_g, gdn_w_out, w_o, post_ln_g, post_ln_b, loss_target, m_w_in, m_conf_dw_w, m_conf_dw_b, m_conf_ln_g, m_conf_ln_b, m_conf_w_out, m_gdn_conv_w, m_gdn_A_log, m_gdn_dt_bias, m_gdn_norm_g, m_gdn_w_out, m_w_o, m_post_ln_g, m_post_ln_b, v_w_in, v_conf_dw_w, v_conf_dw_b, v_conf_ln_g, v_conf_ln_b, v_conf_w_out, v_gdn_conv_w, v_gdn_A_log, v_gdn_dt_bias, v_gdn_norm_g, v_gdn_w_out, v_w_o, v_post_ln_g, v_post_ln_b):
    x2 = x.reshape(x.shape[-2], D)
    tgt = loss_target.reshape(x2.shape)
    w4, wc_out, wg_out, wo_full, dw_full, cw_full = _gather_weights(
        w_in, conf_w_out, gdn_w_out, w_o, conf_dw_w, gdn_conv_w)
    (loss_part, grad_x, d_w_in, d_wc, d_wg, d_wo, d_dww, d_cw, dconv_vec, dpost, d_ad, dng) = _local_step(
        x2, tgt, w4, wc_out, wg_out, wo_full, dw_full, cw_full, conf_dw_b, conf_ln_g, conf_ln_b,
        gdn_A_log, gdn_dt_bias, gdn_norm_g, post_ln_g, post_ln_b)
    loss = lax.psum(loss_part, ("x", "y", "c"))

    dww_c = d_dww[:KC].reshape(KC, N_CHIPS, SQ_BLK)
    dcw_c = d_cw[:KG].reshape(KG, N_CHIPS, 3 * SQ_BLK)
    vecs = [dconv_vec[0], dconv_vec[1], dconv_vec[2], dpost[0], dpost[1]]
    g_rest = jnp.stack([
        _pack_rest(d_wc[j * SQ_BLK:(j + 1) * SQ_BLK], d_wg[j * SQ_BLK:(j + 1) * SQ_BLK], d_wo[j * SQ_BLK:(j + 1) * SQ_BLK],
                   _pack_small(dww_c[:, j], dcw_c[:, j], vecs, d_ad[0, :NH], d_ad[1, :NH], dng[0]))
        for j in range(N_CHIPS)])
    g_w_in, g_rest = _reduce_scatter(d_w_in, g_rest)

    def rest_of(w_c, w_g, w_oo, dw, cw, b1, g1, b2, g2, b3, a_log, dt_bias, norm_g):
        return _pack_rest(w_c, w_g, w_oo, _pack_small(dw, cw, [b1, g1, b2, g2, b3], a_log, dt_bias, norm_g))

    w_r = rest_of(conf_w_out, gdn_w_out, w_o, conf_dw_w, gdn_conv_w, conf_dw_b, conf_ln_g, conf_ln_b,
                  post_ln_g, post_ln_b, gdn_A_log, gdn_dt_bias, gdn_norm_g)
    m_r = rest_of(m_conf_w_out, m_gdn_w_out, m_w_o, m_conf_dw_w, m_gdn_conv_w, m_conf_dw_b, m_conf_ln_g, m_conf_ln_b,
                  m_post_ln_g, m_post_ln_b, m_gdn_A_log, m_gdn_dt_bias, m_gdn_norm_g)
    v_r = rest_of(v_conf_w_out, v_gdn_w_out, v_w_o, v_conf_dw_w, v_gdn_conv_w, v_conf_dw_b, v_conf_ln_g, v_conf_ln_b,
                  v_post_ln_g, v_post_ln_b, v_gdn_A_log, v_gdn_dt_bias, v_gdn_norm_g)
    upd_w_in = _adamw(w_in, g_w_in, m_w_in, v_w_in, "adamw_w_in")
    upd_rest = _adamw(w_r, g_rest, m_r, v_r, "adamw_rest")

    out = [loss, grad_x.reshape(x.shape)]
    for big, rest in zip((g_w_in,) + tuple(upd_w_in), (g_rest,) + tuple(upd_rest)):
        d = dict(_unpack_rest(rest), w_in=big)
        out += [d[n] for n in _WEIGHT_ORDER]
    return tuple(out)
```

```python
import functools

import jax
import jax.numpy as jnp
from jax import lax
from jax.experimental import pallas as pl
from jax.experimental.pallas import tpu as pltpu

f32 = jnp.float32
bf16 = jnp.bfloat16
HI = lax.Precision.HIGHEST
MESH = pl.DeviceIdType.MESH

D = 1024
NH = 8
HD = 128
CH = 64
KC = 31
KG = 4
HALO_C = 32
HALO_G = 8
LANE = 128
STRIP = 32
N_SHIFT = 7
LN_EPS = 1e-5
RMS_EPS = 1e-6
L2_EPS = 1e-6
DN_ALPHA = 2.0 ** 0.25
N_CHIPS = 4
W_IN_COLS = 9232
W_IN_BLK = W_IN_COLS // N_CHIPS
SQ_BLK = D // N_CHIPS
VMEM_LIMIT = 52 * 1024 * 1024
TOKEN_TILE = 256

ADAM_LR = 0.001
ADAM_B1 = 0.9
ADAM_B2 = 0.999
ADAM_EPS = 1e-08
ADAM_WD = 0.01
ADAM_STEP = 10


def _sigmoid(x):
    return 1.0 / (1.0 + jnp.exp(-x))


def _silu_and_grad(x):
    s = _sigmoid(x)
    return x * s, s * (1.0 + x * (1.0 - s))


_NN = ((1,), (0,))
_NT = ((1,), (1,))
_TN = ((0,), (0,))


def _cparams(*sem):
    return pltpu.CompilerParams(dimension_semantics=sem, vmem_limit_bytes=VMEM_LIMIT)


def _mm_multi(a_list, b_list, addend=None, *, out_dtype, tm, tn, name, rhs_t=False):
    n_pairs = len(a_list)
    m = a_list[0].shape[0]
    n = b_list[0].shape[0 if rhs_t else 1]
    has_add = addend is not None
    dims = (_NT if rhs_t else _NN, ((), ()))

    def body(*refs):
        a_refs = refs[:n_pairs]
        b_refs = refs[n_pairs:2 * n_pairs]
        o_ref = refs[-1]
        acc = None
        for a_ref, b_ref in zip(a_refs, b_refs):
            p = lax.dot_general(a_ref[...].astype(bf16), b_ref[...].astype(bf16), dims, preferred_element_type=f32)
            acc = p if acc is None else acc + p
        if has_add:
            acc = acc + refs[2 * n_pairs][...]
        o_ref[...] = acc.astype(out_dtype)

    in_specs = [pl.BlockSpec((tm, a.shape[1]), lambda j, i: (i, 0)) for a in a_list]
    if rhs_t:
        in_specs += [pl.BlockSpec((tn, b.shape[1]), lambda j, i: (j, 0)) for b in b_list]
    else:
        in_specs += [pl.BlockSpec((b.shape[0], tn), lambda j, i: (0, j)) for b in b_list]
    args = list(a_list) + list(b_list)
    if has_add:
        in_specs.append(pl.BlockSpec((tm, tn), lambda j, i: (i, j)))
        args.append(addend)
    return pl.pallas_call(
        body, name=name, grid=(n // tn, m // tm),
        in_specs=in_specs, out_specs=pl.BlockSpec((tm, tn), lambda j, i: (i, j)),
        out_shape=jax.ShapeDtypeStruct((m, n), out_dtype),
        compiler_params=_cparams("parallel", "parallel"),
    )(*args)


def _mm_kloop(a, b, *, tm, tn, tk, name):
    k, m = a.shape
    n = b.shape[1]
    nk = k // tk

    def body(a_ref, b_ref, o_ref):
        @pl.when(pl.program_id(2) == 0)
        def _():
            o_ref[...] = jnp.zeros_like(o_ref)
        o_ref[...] += lax.dot_general(a_ref[...].astype(bf16), b_ref[...].astype(bf16), (_TN, ((), ())),
                                      preferred_element_type=f32)

    return pl.pallas_call(
        body, name=name, grid=(n // tn, m // tm, nk),
        in_specs=[pl.BlockSpec((tk, tm), lambda j, i, kk: (kk, i)), pl.BlockSpec((tk, tn), lambda j, i, kk: (kk, j))],
        out_specs=pl.BlockSpec((tm, tn), lambda j, i, kk: (i, j)),
        out_shape=jax.ShapeDtypeStruct((m, n), f32),
        compiler_params=_cparams("parallel", "parallel", "arbitrary"),
    )(a, b)


def _shift_copies(src_ref, sh_ref, n):
    for b in range(1, 8):
        sh_ref[b - 1, 0:n, :] = src_ref[pl.ds(b, n), :]


def _by_residue(offs):
    groups = {}
    for k, off in enumerate(offs):
        groups.setdefault(off % 8, []).append((k, off // 8))
    return groups


def _slab(src_ref, sh_ref, b, r0, n, lanes):
    ref = src_ref if b == 0 else sh_ref.at[b - 1]
    return ref[r0:r0 + n, lanes]


def _tap_conv(out_ref, n_rows, src_ref, sh_ref, w_ref, offs, bias_ref=None):
    groups = _by_residue(offs)
    for j in range(D // LANE):
        lanes = slice(j * LANE, (j + 1) * LANE)
        wv = [w_ref[k:k + 1, lanes] for k in range(len(offs))]
        for r0 in range(0, n_rows, STRIP):
            n = min(STRIP, n_rows - r0)
            accs = [jnp.zeros((n, LANE), f32) if bias_ref is None else jnp.broadcast_to(bias_ref[0:1, lanes], (n, LANE)),
                    jnp.zeros((n, LANE), f32)]
            m = 0
            for b, taps in groups.items():
                a_lo = min(a for _, a in taps)
                a_hi = max(a for _, a in taps)
                wide = _slab(src_ref, sh_ref, b, r0 + 8 * a_lo, 8 * (a_hi - a_lo) + n, lanes)
                for k, a in taps:
                    accs[m % 2] = accs[m % 2] + wv[k] * wide[8 * (a - a_lo):8 * (a - a_lo) + n]
                    m += 1
            out_ref[r0:r0 + n, lanes] = accs[0] + accs[1]


def _tap_corr(dw_ref, n_rows, lhs_ref, src_ref, sh_ref, offs):
    groups = _by_residue(offs)
    for j in range(D // LANE):
        lanes = slice(j * LANE, (j + 1) * LANE)
        accs = [jnp.zeros((8, LANE), f32) for _ in offs]
        for r0 in range(0, n_rows, STRIP):
            n = min(STRIP, n_rows - r0)
            d = lhs_ref[r0:r0 + n, lanes]
            for b, taps in groups.items():
                a_lo = min(a for _, a in taps)
                a_hi = max(a for _, a in taps)
                wide = _slab(src_ref, sh_ref, b, r0 + 8 * a_lo, 8 * (a_hi - a_lo) + n, lanes)
                for k, a in taps:
                    prod = d * wide[8 * (a - a_lo):8 * (a - a_lo) + n]
                    part = prod[0:8]
                    for q in range(1, n // 8):
                        part = part + prod[8 * q:8 * q + 8]
                    accs[k] = accs[k] + part
        for k in range(len(offs)):
            dw_ref[k:k + 1, lanes] += jnp.sum(accs[k], axis=0, keepdims=True)


_FWD_OFFS = [HALO_C - (KC - 1) + k for k in range(KC)]
_BWD_OFFS = [KC - 1 - k for k in range(KC)]


def _norm_act(a1, cz, g_ref, bb_ref):
    mu = jnp.mean(a1, axis=-1, keepdims=True)
    cen = a1 - mu
    var = jnp.mean(cen * cen, axis=-1, keepdims=True)
    rstd = lax.rsqrt(var + LN_EPS)
    xhat = cen * rstd
    ln = xhat * g_ref[...] + bb_ref[...]
    s, ds = _silu_and_grad(ln)
    zc, dzc = _silu_and_grad(cz)
    return xhat, rstd, s, ds, zc, dzc


def _conv_fwd(p_conv, dw_w, dw_b, ln_g, ln_b, *, tt):
    t = p_conv.shape[0]
    hb = tt // HALO_C

    def body(cv_ref, cg_ref, cz_ref, cvh_ref, cgh_ref, w_ref, b_ref, g_ref, bb_ref, u_ref, ext_ref, sh_ref, a1_ref):
        first = pl.program_id(0) == 0
        halo = cvh_ref[...] * _sigmoid(cgh_ref[...])
        ext_ref[0:HALO_C, :] = jnp.where(first, 0.0, halo)
        ext_ref[HALO_C:, :] = cv_ref[...] * _sigmoid(cg_ref[...])
        _shift_copies(ext_ref, sh_ref, tt + HALO_C - 8)
        _tap_conv(a1_ref, tt, ext_ref, sh_ref, w_ref, _FWD_OFFS, b_ref)
        _, _, s, _, zc, _ = _norm_act(a1_ref[...], cz_ref[...], g_ref, bb_ref)
        u_ref[...] = (s * zc).astype(bf16)

    def main(col):
        return pl.BlockSpec((tt, D), lambda i: (i, col))

    def prev(col):
        return pl.BlockSpec((HALO_C, D), lambda i: (jnp.maximum(i * hb - 1, 0), col))

    vec = pl.BlockSpec((1, D), lambda i: (0, 0))
    return pl.pallas_call(
        body, name="conv_fwd", grid=(t // tt,),
        in_specs=[main(0), main(1), main(2), prev(0), prev(1),
                  pl.BlockSpec((HALO_C, D), lambda i: (0, 0)), vec, vec, vec],
        out_specs=pl.BlockSpec((tt, D), lambda i: (i, 0)),
        out_shape=jax.ShapeDtypeStruct((t, D), bf16),
        scratch_shapes=[pltpu.VMEM((tt + HALO_C, D), f32), pltpu.VMEM((N_SHIFT, tt + HALO_C - 8, D), f32),
                        pltpu.VMEM((tt, D), f32)],
        compiler_params=_cparams("parallel"),
    )(p_conv, p_conv, p_conv, p_conv, p_conv, dw_w, dw_b, ln_g, ln_b)


def _conv_bwd(p_conv, du, dw_w, dw_b, ln_g, ln_b, *, tt):
    t = p_conv.shape[0]
    hb = tt // HALO_C
    n_tiles = t // tt
    last_hb = t // HALO_C - 1
    ne = tt + HALO_C

    def body(cv_ref, cg_ref, cz_ref, du_ref, cvp_ref, cgp_ref, cvn_ref, cgn_ref, czn_ref, dun_ref,
             w_ref, b_ref, g_ref, bb_ref, dp_ref, dww_ref, dvec_ref, ext_ref, sh_ref, a1_ref, da1_ref, da0_ref):
        i = pl.program_id(0)
        first = i == 0
        last = i == n_tiles - 1

        @pl.when(first)
        def _():
            dww_ref[...] = jnp.zeros_like(dww_ref)
            dvec_ref[...] = jnp.zeros_like(dvec_ref)

        sig = _sigmoid(cg_ref[...])
        ext_ref[0:HALO_C, :] = jnp.where(first, 0.0, cvp_ref[...] * _sigmoid(cgp_ref[...]))
        ext_ref[HALO_C:HALO_C + tt, :] = cv_ref[...] * sig
        ext_ref[HALO_C + tt:, :] = cvn_ref[...] * _sigmoid(cgn_ref[...])
        _shift_copies(ext_ref, sh_ref, ne + HALO_C - 8)
        _tap_conv(a1_ref, ne, ext_ref, sh_ref, w_ref, _FWD_OFFS, b_ref)
        cz = jnp.concatenate([cz_ref[...], czn_ref[...]], axis=0)
        du_all = jnp.concatenate([du_ref[...], jnp.where(last, 0.0, dun_ref[...])], axis=0)
        xhat, rstd, s, ds, zc, dzc = _norm_act(a1_ref[...], cz, g_ref, bb_ref)
        dln = du_all * zc * ds
        dxhat = dln * g_ref[...]
        da1 = rstd * (dxhat - jnp.mean(dxhat, axis=-1, keepdims=True)
                      - xhat * jnp.mean(dxhat * xhat, axis=-1, keepdims=True))
        da1_ref[...] = da1
        dcz = (du_all * s * dzc)[:tt]
        dvec_ref[0:1, :] += jnp.sum(da1[:tt], axis=0, keepdims=True)
        dvec_ref[1:2, :] += jnp.sum((dln * xhat)[:tt], axis=0, keepdims=True)
        dvec_ref[2:3, :] += jnp.sum(dln[:tt], axis=0, keepdims=True)
        _tap_corr(dww_ref, tt, da1_ref, ext_ref, sh_ref, _FWD_OFFS)
        _shift_copies(da1_ref, sh_ref, ne - 8)
        _tap_conv(da0_ref, tt, da1_ref, sh_ref, w_ref, _BWD_OFFS)
        da0 = da0_ref[...]
        cv = cv_ref[...]
        dp_ref[:, 0:D] = (da0 * sig).astype(bf16)
        dp_ref[:, D:2 * D] = (da0 * cv * sig * (1.0 - sig)).astype(bf16)
        dp_ref[:, 2 * D:] = dcz.astype(bf16)

    def main(col):
        return pl.BlockSpec((tt, D), lambda i: (i, col))

    def prev(col):
        return pl.BlockSpec((HALO_C, D), lambda i: (jnp.maximum(i * hb - 1, 0), col))

    def nxt(col):
        return pl.BlockSpec((HALO_C, D), lambda i: (jnp.minimum((i + 1) * hb, last_hb), col))

    vec = pl.BlockSpec((1, D), lambda i: (0, 0))
    return pl.pallas_call(
        body, name="conv_bwd", grid=(n_tiles,),
        in_specs=[main(0), main(1), main(2), main(0), prev(0), prev(1), nxt(0), nxt(1), nxt(2), nxt(0),
                  pl.BlockSpec((HALO_C, D), lambda i: (0, 0)), vec, vec, vec],
        out_specs=[pl.BlockSpec((tt, 3 * D), lambda i: (i, 0)),
                   pl.BlockSpec((HALO_C, D), lambda i: (0, 0)),
                   pl.BlockSpec((8, D), lambda i: (0, 0))],
        out_shape=[jax.ShapeDtypeStruct((t, 3 * D), bf16), jax.ShapeDtypeStruct((HALO_C, D), f32),
                   jax.ShapeDtypeStruct((8, D), f32)],
        scratch_shapes=[pltpu.VMEM((ne + HALO_C, D), f32), pltpu.VMEM((N_SHIFT, ne + HALO_C - 8, D), f32),
                        pltpu.VMEM((ne, D), f32), pltpu.VMEM((ne, D), f32), pltpu.VMEM((tt, D), f32)],
        compiler_params=_cparams("arbitrary"),
    )(p_conv, p_conv, p_conv, du, p_conv, p_conv, p_conv, p_conv, p_conv, du, dw_w, dw_b, ln_g, ln_b)


def _head_selectors():
    r = lax.broadcasted_iota(jnp.int32, (HD, D), 0)
    c = lax.broadcasted_iota(jnp.int32, (HD, D), 1) // HD
    return jnp.where(r == c, 1.0, 0.0).astype(f32), jnp.where(r == c + NH, 1.0, 0.0).astype(f32)


def _dot_hi(a, b):
    return lax.dot_general(a, b, (((1,), (0,)), ((), ())), precision=HI, preferred_element_type=f32)


def _dot_hi_nt(a, b):
    return lax.dot_general(a, b, (((1,), (1,)), ((), ())), precision=HI, preferred_element_type=f32)


def _dot_hi_tn(a, b):
    return lax.dot_general(a, b, (((0,), (0,)), ((), ())), precision=HI, preferred_element_type=f32)


def _chunk_tri(n, lower):
    r = lax.broadcasted_iota(jnp.int32, (n, n), 0)
    c = lax.broadcasted_iota(jnp.int32, (n, n), 1)
    tri = (r >= c) if lower else (r <= c)
    return jnp.where(tri & (r // CH == c // CH), 1.0, 0.0).astype(f32)


def _softplus_and_sigmoid(x):
    e = jnp.exp(-jnp.abs(x))
    log1p = jnp.where(e < 1e-2, e * (1.0 - e * (0.5 - e * (1.0 / 3.0 - 0.25 * e))), jnp.log(1.0 + e))
    return jnp.maximum(x, 0.0) + log1p, _sigmoid(x)


def _gdn_short_conv(ext_ref, n_rows, w_ref):
    pre = jnp.zeros((n_rows, D), f32)
    for k in range(KG):
        pre = pre + w_ref[k:k + 1, :] * ext_ref[pl.ds(HALO_G - (KG - 1) + k, n_rows), :]
    return pre


def _l2norm_heads(act, scale):
    outs, rs = [], []
    for h in range(NH):
        a = act[:, h * HD:(h + 1) * HD]
        r = lax.rsqrt(jnp.sum(a * a, axis=-1, keepdims=True) + L2_EPS)
        outs.append(a * (r * scale))
        rs.append(jnp.broadcast_to(r, a.shape))
    return jnp.concatenate(outs, axis=-1), jnp.concatenate(rs, axis=-1)


def _gdn_pre_fwd(p_qkv, p_ba, cw, alog_b, dt_b, *, tt):
    t = p_qkv.shape[0]
    hb = tt // HALO_G

    def body(q_ref, k_ref, v_ref, qh_ref, kh_ref, vh_ref, ba_ref, wq_ref, wk_ref, wv_ref, al_ref, dt_ref,
             qn_ref, kn_ref, va_ref, gb_ref, bb_ref, ext_ref):
        first = pl.program_id(0) == 0

        def conv_act(x_ref, xh_ref, w_ref):
            ext_ref[0:HALO_G, :] = jnp.where(first, 0.0, xh_ref[...])
            ext_ref[HALO_G:, :] = x_ref[...]
            pre = _gdn_short_conv(ext_ref, tt, w_ref)
            return pre * _sigmoid(pre)

        qn_ref[...] = _l2norm_heads(conv_act(q_ref, qh_ref, wq_ref), HD ** -0.5)[0]
        kn_ref[...] = _l2norm_heads(conv_act(k_ref, kh_ref, wk_ref), 1.0)[0]
        va_ref[...] = conv_act(v_ref, vh_ref, wv_ref)
        sel_b, sel_a = _head_selectors()
        ba = ba_ref[...]
        bb_ref[...] = _sigmoid(_dot_hi(ba, sel_b))
        sp, _ = _softplus_and_sigmoid(_dot_hi(ba, sel_a) + dt_ref[...])
        gb_ref[...] = _dot_hi(_chunk_tri(tt, lower=True), -jnp.exp(al_ref[...]) * sp)

    def main(col):
        return pl.BlockSpec((tt, D), lambda i: (i, col))

    def prev(col):
        return pl.BlockSpec((HALO_G, D), lambda i: (jnp.maximum(i * hb - 1, 0), col))

    def wspec(col):
        return pl.BlockSpec((8, D), lambda i: (0, col))

    vec = pl.BlockSpec((1, D), lambda i: (0, 0))
    sds = jax.ShapeDtypeStruct((t, D), f32)
    return pl.pallas_call(
        body, name="gdn_pre_fwd", grid=(t // tt,),
        in_specs=[main(0), main(1), main(2), prev(0), prev(1), prev(2),
                  pl.BlockSpec((tt, HD), lambda i: (i, 0)), wspec(0), wspec(1), wspec(2), vec, vec],
        out_specs=[pl.BlockSpec((tt, D), lambda i: (i, 0))] * 5,
        out_shape=[sds] * 5,
        scratch_shapes=[pltpu.VMEM((tt + HALO_G, D), f32)],
        compiler_params=_cparams("parallel"),
    )(p_qkv, p_qkv, p_qkv, p_qkv, p_qkv, p_qkv, p_ba, cw, cw, cw, alog_b, dt_b)


def _gdn_pre_bwd(p_qkv, p_ba, cw, alog_b, dt_b, dqn, dkn, dva, dgb, dbb, *, tt):
    t = p_qkv.shape[0]
    hb = tt // HALO_G
    n_tiles = t // tt
    last_hb = t // HALO_G - 1
    ne = tt + HALO_G

    def body(q_ref, k_ref, v_ref, qp_ref, kp_ref, vp_ref, qx_ref, kx_ref, vx_ref,
             dq_ref, dk_ref, dv_ref, dqx_ref, dkx_ref, dvx_ref, ba_ref, dgb_ref, dbb_ref,
             wq_ref, wk_ref, wv_ref, al_ref, dt_ref,
             dp_ref, dba_ref, dcw_ref, dad_ref, ext_ref, dpre_ref):
        i = pl.program_id(0)
        first = i == 0
        last = i == n_tiles - 1

        @pl.when(first)
        def _():
            dcw_ref[...] = jnp.zeros_like(dcw_ref)
            dad_ref[...] = jnp.zeros_like(dad_ref)

        def one(x_ref, xp_ref, xx_ref, d_ref, dx_ref, w_ref, col, scale):
            ext_ref[0:HALO_G, :] = jnp.where(first, 0.0, xp_ref[...])
            ext_ref[HALO_G:HALO_G + tt, :] = x_ref[...]
            ext_ref[HALO_G + tt:, :] = xx_ref[...]
            pre = _gdn_short_conv(ext_ref, ne, w_ref)
            act, dact = _silu_and_grad(pre)
            d_out = jnp.concatenate([d_ref[...], jnp.where(last, 0.0, dx_ref[...])], axis=0)
            if scale is None:
                d_act = d_out
            else:
                parts = []
                for h in range(NH):
                    a = act[:, h * HD:(h + 1) * HD]
                    dn = d_out[:, h * HD:(h + 1) * HD]
                    r = lax.rsqrt(jnp.sum(a * a, axis=-1, keepdims=True) + L2_EPS)
                    parts.append(scale * r * (dn - a * (r * r) * jnp.sum(dn * a, axis=-1, keepdims=True)))
                d_act = jnp.concatenate(parts, axis=-1)
            dpre = d_act * dact
            dpre_ref[...] = dpre
            dpre_t = dpre[:tt]
            draw = jnp.zeros((tt, D), f32)
            for k in range(KG):
                draw = draw + w_ref[k:k + 1, :] * dpre_ref[pl.ds(KG - 1 - k, tt), :]
                dcw_ref[k:k + 1, col * D:(col + 1) * D] += jnp.sum(
                    dpre_t * ext_ref[pl.ds(HALO_G - (KG - 1) + k, tt), :], axis=0, keepdims=True)
            dp_ref[:, col * D:(col + 1) * D] = draw.astype(bf16)

        one(q_ref, qp_ref, qx_ref, dq_ref, dqx_ref, wq_ref, 0, HD ** -0.5)
        one(k_ref, kp_ref, kx_ref, dk_ref, dkx_ref, wk_ref, 1, 1.0)
        one(v_ref, vp_ref, vx_ref, dv_ref, dvx_ref, wv_ref, 2, None)

        sel_b, sel_a = _head_selectors()
        ba = ba_ref[...]
        beta = _sigmoid(_dot_hi(ba, sel_b))
        sp, sg = _softplus_and_sigmoid(_dot_hi(ba, sel_a) + dt_ref[...])
        neg_a = -jnp.exp(al_ref[...])
        dgb_v = _dot_hi(_chunk_tri(tt, lower=False), dgb_ref[...])
        d_bl = dbb_ref[...] * beta * (1.0 - beta)
        d_al = dgb_v * neg_a * sg
        dba_ref[...] = _dot_hi_nt(d_bl, sel_b) + _dot_hi_nt(d_al, sel_a)
        d_alog = jnp.sum(dgb_v * neg_a * sp, axis=0, keepdims=True)
        d_dt = jnp.sum(d_al, axis=0, keepdims=True)
        dad_ref[0:1, :] += _dot_hi_nt(d_alog, sel_b)
        dad_ref[1:2, :] += _dot_hi_nt(d_dt, sel_b)

    def main(col):
        return pl.BlockSpec((tt, D), lambda i: (i, col))

    def prev(col):
        return pl.BlockSpec((HALO_G, D), lambda i: (jnp.maximum(i * hb - 1, 0), col))

    def nxt(col):
        return pl.BlockSpec((HALO_G, D), lambda i: (jnp.minimum((i + 1) * hb, last_hb), col))

    def wspec(col):
        return pl.BlockSpec((8, D), lambda i: (0, col))

    vec = pl.BlockSpec((1, D), lambda i: (0, 0))
    return pl.pallas_call(
        body, name="gdn_pre_bwd", grid=(n_tiles,),
        in_specs=[main(0), main(1), main(2), prev(0), prev(1), prev(2), nxt(0), nxt(1), nxt(2),
                  main(0), main(0), main(0), nxt(0), nxt(0), nxt(0),
                  pl.BlockSpec((tt, HD), lambda i: (i, 0)), main(0), main(0),
                  wspec(0), wspec(1), wspec(2), vec, vec],
        out_specs=[pl.BlockSpec((tt, 3 * D), lambda i: (i, 0)), pl.BlockSpec((tt, HD), lambda i: (i, 0)),
                   pl.BlockSpec((8, 3 * D), lambda i: (0, 0)), pl.BlockSpec((8, HD), lambda i: (0, 0))],
        out_shape=[jax.ShapeDtypeStruct((t, 3 * D), bf16), jax.ShapeDtypeStruct((t, HD), f32),
                   jax.ShapeDtypeStruct((8, 3 * D), f32), jax.ShapeDtypeStruct((8, HD), f32)],
        scratch_shapes=[pltpu.VMEM((HALO_G + tt + HALO_G, D), f32), pltpu.VMEM((ne, D), f32)],
        compiler_params=_cparams("arbitrary"),
    )(p_qkv, p_qkv, p_qkv, p_qkv, p_qkv, p_qkv, p_qkv, p_qkv, p_qkv,
      dqn, dkn, dva, dqn, dkn, dva, p_ba, dgb, dbb, cw, cw, cw, alog_b, dt_b)


def _dot_b(a, b, dims):
    return lax.dot_general(a.astype(bf16), b.astype(bf16), (dims, ((), ())), preferred_element_type=f32)


def _chunk_fn(qs, ks, vs, gcs, bbs, ss):
    heads = range(len(qs))
    r = lax.broadcasted_iota(jnp.int32, (CH, CH), 0)
    c = lax.broadcasted_iota(jnp.int32, (CH, CH), 1)
    causal = r >= c
    strict = r > c
    eye = jnp.where(r == c, 1.0, 0.0).astype(f32)
    gc_row = [gcs[h].T[:CH, :] for h in heads]
    decay = [jnp.where(causal, jnp.exp(jnp.where(causal, gcs[h][:, :CH] - gc_row[h], 0.0)), 0.0) for h in heads]
    kb = [ks[h] * bbs[h] for h in heads]
    egc = [jnp.exp(gcs[h]) for h in heads]
    kk = [_dot_b(kb[h], ks[h], _NT) for h in heads]
    qk = [_dot_b(qs[h], ks[h], _NT) for h in heads]
    m = [-jnp.where(strict, kk[h] * decay[h], 0.0) for h in heads]
    p = [eye + m[h] for h in heads]
    mp = m
    for _ in range(5):
        mp = [_dot_b(mp[h], mp[h], _NN) for h in heads]
        pm = [_dot_b(p[h], mp[h], _NN) for h in heads]
        p = [p[h] + pm[h] for h in heads]
    u = [_dot_b(p[h], vs[h] * bbs[h], _NN) for h in heads]
    w = [_dot_b(p[h], kb[h] * egc[h], _NN) for h in heads]
    intra = [jnp.where(causal, qk[h] * decay[h], 0.0) for h in heads]
    g_last = [gcs[h][CH - 1:CH, :] for h in heads]
    k_dec = [ks[h] * jnp.exp(g_last[h] - gcs[h]) for h in heads]
    ws = [_dot_b(w[h], ss[h], _NN) for h in heads]
    qs_s = [_dot_b(qs[h] * egc[h], ss[h], _NN) for h in heads]
    v_new = [u[h] - ws[h] for h in heads]
    iv = [_dot_b(intra[h], v_new[h], _NN) for h in heads]
    kv = [_dot_b(k_dec[h], v_new[h], _TN) for h in heads]
    o = tuple(qs_s[h] + iv[h] for h in heads)
    s_new = tuple(ss[h] * jnp.exp(g_last[h]) + kv[h] for h in heads)
    return o, s_new


def _head_cols():
    return [slice(h * HD, (h + 1) * HD) for h in range(NH)]


def _gdn_scan_fwd(qn, kn, va, gb, bb, *, tt):
    t = qn.shape[0]
    cpb = tt // CH

    def body(q_ref, k_ref, v_ref, g_ref, b_ref, o_ref, st_ref, s_scr):
        @pl.when(pl.program_id(0) == 0)
        def _():
            s_scr[...] = jnp.zeros_like(s_scr)

        def step(ci, carry):
            rows = pl.ds(pl.multiple_of(ci * CH, CH), CH)
            cols = _head_cols()
            ss = tuple(s_scr[h] for h in range(NH))
            for h in range(NH):
                st_ref[ci, h] = ss[h]
            o, s_new = _chunk_fn(*(tuple(ref[rows, cl] for cl in cols) for ref in (q_ref, k_ref, v_ref, g_ref, b_ref)), ss)
            for h in range(NH):
                o_ref[rows, cols[h]] = o[h]
                s_scr[h] = s_new[h]
            return carry

        lax.fori_loop(0, cpb, step, 0)

    blk = pl.BlockSpec((tt, D), lambda i: (i, 0))
    return pl.pallas_call(
        body, name="gdn_scan_fwd", grid=(t // tt,),
        in_specs=[blk] * 5,
        out_specs=[blk, pl.BlockSpec((cpb, NH, HD, HD), lambda i: (i, 0, 0, 0))],
        out_shape=[jax.ShapeDtypeStruct((t, D), f32), jax.ShapeDtypeStruct((t // CH, NH, HD, HD), f32)],
        scratch_shapes=[pltpu.VMEM((NH, HD, HD), f32)],
        compiler_params=_cparams("arbitrary"),
    )(qn, kn, va, gb, bb)


def _gdn_scan_bwd(qn, kn, va, gb, bb, states, do, *, tt):
    t = qn.shape[0]
    nblk = t // tt
    cpb = tt // CH

    def body(q_ref, k_ref, v_ref, g_ref, b_ref, st_ref, do_ref, dq_ref, dk_ref, dv_ref, dg_ref, db_ref, ds_scr):
        @pl.when(pl.program_id(0) == 0)
        def _():
            ds_scr[...] = jnp.zeros_like(ds_scr)

        def step(j, carry):
            ci = cpb - 1 - j
            rows = pl.ds(pl.multiple_of(ci * CH, CH), CH)
            cols = _head_cols()
            _, vjp = jax.vjp(_chunk_fn, *(tuple(ref[rows, cl] for cl in cols) for ref in (q_ref, k_ref, v_ref, g_ref, b_ref)),
                             tuple(st_ref[ci, h] for h in range(NH)))
            grads = vjp((tuple(do_ref[rows, cl] for cl in cols), tuple(ds_scr[h] for h in range(NH))))
            for h in range(NH):
                for ref, g in zip((dq_ref, dk_ref, dv_ref, dg_ref, db_ref), grads[:5]):
                    ref[rows, cols[h]] = g[h]
                ds_scr[h] = grads[5][h]
            return carry

        lax.fori_loop(0, cpb, step, 0)

    blk = pl.BlockSpec((tt, D), lambda i: (nblk - 1 - i, 0))
    sblk = pl.BlockSpec((cpb, NH, HD, HD), lambda i: (nblk - 1 - i, 0, 0, 0))
    sds = jax.ShapeDtypeStruct((t, D), f32)
    return pl.pallas_call(
        body, name="gdn_scan_bwd", grid=(nblk,),
        in_specs=[blk] * 5 + [sblk, blk],
        out_specs=[blk] * 5, out_shape=[sds] * 5,
        scratch_shapes=[pltpu.VMEM((NH, HD, HD), f32)],
        compiler_params=_cparams("arbitrary"),
    )(qn, kn, va, gb, bb, states, do)


def _rms_heads(o):
    ons, rs = [], []
    for h in range(NH):
        a = o[:, h * HD:(h + 1) * HD]
        r = lax.rsqrt(jnp.mean(a * a, axis=-1, keepdims=True) + RMS_EPS)
        ons.append(a * r)
        rs.append(jnp.broadcast_to(r, a.shape))
    return jnp.concatenate(ons, axis=-1), jnp.concatenate(rs, axis=-1)


def _gdn_post_fwd(o, p_gz, ng_b, *, tt):
    t = o.shape[0]

    def body(o_ref, gz_ref, ng_ref, og_ref):
        on, _ = _rms_heads(o_ref[...])
        z, _ = _silu_and_grad(gz_ref[...])
        og_ref[...] = (on * ng_ref[...] * z).astype(bf16)

    blk = pl.BlockSpec((tt, D), lambda i: (i, 0))
    return pl.pallas_call(
        body, name="gdn_post_fwd", grid=(t // tt,),
        in_specs=[blk, blk, pl.BlockSpec((1, D), lambda i: (0, 0))],
        out_specs=blk, out_shape=jax.ShapeDtypeStruct((t, D), bf16),
        compiler_params=_cparams("parallel"),
    )(o, p_gz, ng_b)


def _gdn_post_bwd(o, p_gz, ng_b, dog, *, tt):
    t = o.shape[0]

    def body(o_ref, gz_ref, ng_ref, dog_ref, do_ref, dgz_ref, dng_ref):
        @pl.when(pl.program_id(0) == 0)
        def _():
            dng_ref[...] = jnp.zeros_like(dng_ref)

        on, r = _rms_heads(o_ref[...])
        z, dz = _silu_and_grad(gz_ref[...])
        dog_v = dog_ref[...]
        ng = ng_ref[...]
        dgz_ref[...] = (dog_v * on * ng * dz).astype(bf16)
        dy = dog_v * z
        dng_all = jnp.sum(dy * on, axis=0, keepdims=True)
        dng = dng_all[:, 0:HD]
        for h in range(1, NH):
            dng = dng + dng_all[:, h * HD:(h + 1) * HD]
        dng_ref[0:1, :] += dng
        don = dy * ng
        prod = don * on
        parts = []
        for h in range(NH):
            sl = slice(h * HD, (h + 1) * HD)
            parts.append(don[:, sl] - on[:, sl] * jnp.mean(prod[:, sl], axis=-1, keepdims=True))
        do_ref[...] = r * jnp.concatenate(parts, axis=-1)

    blk = pl.BlockSpec((tt, D), lambda i: (i, 0))
    return pl.pallas_call(
        body, name="gdn_post_bwd", grid=(t // tt,),
        in_specs=[blk, blk, pl.BlockSpec((1, D), lambda i: (0, 0)), blk],
        out_specs=[blk, blk, pl.BlockSpec((8, HD), lambda i: (0, 0))],
        out_shape=[jax.ShapeDtypeStruct((t, D), f32), jax.ShapeDtypeStruct((t, D), bf16),
                   jax.ShapeDtypeStruct((8, HD), f32)],
        compiler_params=_cparams("arbitrary"),
    )(o, p_gz, ng_b, dog)


def _merge(x, y_conf, y_gdn, p_gates, target, w_o, ln_g, ln_b, *, tt):
    t = x.shape[0]

    def body(x_ref, yc_ref, yg_ref, gc_ref, gg_ref, tg_ref, w_ref, g_ref, b_ref,
             loss_ref, dxd_ref, dyc_ref, dyg_ref, dpg_ref, h_ref, dz_ref, dvec_ref):
        @pl.when(pl.program_id(0) == 0)
        def _():
            loss_ref[...] = jnp.zeros_like(loss_ref)
            dvec_ref[...] = jnp.zeros_like(dvec_ref)

        sc = _sigmoid(gc_ref[...])
        sg = _sigmoid(gg_ref[...])
        yc = yc_ref[...]
        yg = yg_ref[...]
        h = (sc * yc + sg * yg).astype(bf16)
        h_ref[...] = h
        z = DN_ALPHA * x_ref[...] + jnp.dot(h, w_ref[...], preferred_element_type=f32)
        mu = jnp.mean(z, axis=-1, keepdims=True)
        cen = z - mu
        rstd = lax.rsqrt(jnp.mean(cen * cen, axis=-1, keepdims=True) + LN_EPS)
        xhat = cen * rstd
        err = xhat * g_ref[...] + b_ref[...] - tg_ref[...]
        loss_ref[...] += 0.5 / D * jnp.sum(err * err)
        dy = err * (1.0 / D)
        dvec_ref[0:1, :] += jnp.sum(dy * xhat, axis=0, keepdims=True)
        dvec_ref[1:2, :] += jnp.sum(dy, axis=0, keepdims=True)
        dxhat = dy * g_ref[...]
        dz = rstd * (dxhat - jnp.mean(dxhat, axis=-1, keepdims=True)
                     - xhat * jnp.mean(dxhat * xhat, axis=-1, keepdims=True))
        dxd_ref[...] = DN_ALPHA * dz
        dz_b = dz.astype(bf16)
        dz_ref[...] = dz_b
        dh = lax.dot_general(dz_b, w_ref[...], (_NT, ((), ())), preferred_element_type=f32)
        dyc_ref[...] = (dh * sc).astype(bf16)
        dyg_ref[...] = (dh * sg).astype(bf16)
        dpg_ref[:, 0:D] = (dh * yc * sc * (1.0 - sc)).astype(bf16)
        dpg_ref[:, D:] = (dh * yg * sg * (1.0 - sg)).astype(bf16)

    blk = pl.BlockSpec((tt, D), lambda i: (i, 0))
    wblk = pl.BlockSpec((D, D), lambda i: (0, 0))
    vec = pl.BlockSpec((1, D), lambda i: (0, 0))
    return pl.pallas_call(
        body, name="merge_norm_loss", grid=(t // tt,),
        in_specs=[blk, blk, blk, pl.BlockSpec((tt, D), lambda i: (i, 0)), pl.BlockSpec((tt, D), lambda i: (i, 1)),
                  blk, wblk, vec, vec],
        out_specs=[pl.BlockSpec((8, HD), lambda i: (0, 0)), blk, blk, blk,
                   pl.BlockSpec((tt, 2 * D), lambda i: (i, 0)), blk, blk, pl.BlockSpec((8, D), lambda i: (0, 0))],
        out_shape=[jax.ShapeDtypeStruct((8, HD), f32), jax.ShapeDtypeStruct((t, D), f32),
                   jax.ShapeDtypeStruct((t, D), bf16), jax.ShapeDtypeStruct((t, D), bf16),
                   jax.ShapeDtypeStruct((t, 2 * D), bf16), jax.ShapeDtypeStruct((t, D), bf16),
                   jax.ShapeDtypeStruct((t, D), bf16), jax.ShapeDtypeStruct((8, D), f32)],
        compiler_params=_cparams("arbitrary"),
    )(x, y_conf, y_gdn, p_gates, p_gates, target, w_o, ln_g, ln_b)


def _place():
    return lax.axis_index("x"), lax.axis_index("y"), lax.axis_index("c")


def _any_specs(n):
    return [pl.BlockSpec(memory_space=pl.ANY)] * n


def _sibling_merge(arrs, name, take_other_half=False):
    k = len(arrs)

    def body(*refs):
        a_refs, o_refs = refs[:k], refs[k:2 * k]
        send_sems, recv_sems = refs[2 * k:]
        x, y, c = _place()
        sends = []
        for i in range(k):
            src = a_refs[i]
            if take_other_half:
                n = a_refs[i].shape[-2] // 2
                lead = (slice(None),) * (len(a_refs[i].shape) - 2)
                src = a_refs[i].at[lead + (pl.ds((1 - c) * n, n), slice(None))]
            cp = pltpu.make_async_remote_copy(src_ref=src, dst_ref=o_refs[i], send_sem=send_sems.at[i],
                                              recv_sem=recv_sems.at[i], device_id=(x, y, 1 - c), device_id_type=MESH)
            cp.start()
            sends.append(cp)
        for cp in sends:
            cp.wait()

    def out_sds(a):
        rows = a.shape[-2] // 2 if take_other_half else a.shape[-2]
        return jax.ShapeDtypeStruct(a.shape[:-2] + (rows, a.shape[-1]), a.dtype)

    return pl.pallas_call(
        body, name=name, in_specs=_any_specs(k), out_specs=_any_specs(k),
        out_shape=[out_sds(a) for a in arrs],
        scratch_shapes=[pltpu.SemaphoreType.DMA((k,)), pltpu.SemaphoreType.DMA((k,))],
    )(*arrs)


def _join_halves(mine, other):
    c = lax.axis_index("c")
    return jnp.concatenate([jnp.where(c == 0, mine, other), jnp.where(c == 0, other, mine)], axis=-2)


def _chip_exchange(arrs, name, scatter):
    k = len(arrs)

    def body(*refs):
        a_refs, o_refs = refs[:k], refs[k:2 * k]
        send_sems, recv_sems, local_sems = refs[2 * k:]
        x, y, c = _place()
        me = 2 * x + y
        peers = [(1 - x, y), (x, 1 - y), (1 - x, 1 - y)]

        def src(i, j):
            return a_refs[i].at[j] if scatter else a_refs[i]

        def copy(i, n, send_j, slot):
            px, py = peers[n]
            return pltpu.make_async_remote_copy(
                src_ref=src(i, send_j), dst_ref=o_refs[i].at[slot], send_sem=send_sems.at[3 * i + n],
                recv_sem=recv_sems.at[3 * i + n], device_id=(px, py, c), device_id_type=MESH)

        owns = [pltpu.make_async_copy(src(i, me), o_refs[i].at[me], local_sems.at[i]) for i in range(k)]
        for own in owns:
            own.start()
        sends = [copy(i, n, 2 * peers[n][0] + peers[n][1], me) for n in range(3) for i in range(k)]
        for cp in sends:
            cp.start()
        for n in range(3):
            for i in range(k):
                copy(i, n, me, 2 * peers[n][0] + peers[n][1]).wait_recv()
        for cp in sends:
            cp.wait_send()
        for own in owns:
            own.wait()

    def out_sds(a):
        return jax.ShapeDtypeStruct((N_CHIPS,) + tuple(a.shape[1:] if scatter else a.shape), a.dtype)

    return pl.pallas_call(
        body, name=name, in_specs=_any_specs(k), out_specs=_any_specs(k),
        out_shape=[out_sds(a) for a in arrs],
        scratch_shapes=[pltpu.SemaphoreType.DMA((3 * k,)), pltpu.SemaphoreType.DMA((3 * k,)),
                        pltpu.SemaphoreType.DMA((k,))],
    )(*arrs)


def _pair_sum(g_all, got, c_arr, name, out_dtype):
    n, w = got.shape[1:]
    tile = n // 4
    n_tiles = n // tile

    def body(c_ref, a_ref, b_ref, o_ref):
        o_ref[...] = (a_ref[...] + b_ref[...]).astype(out_dtype)

    return pl.pallas_call(
        body, name=name,
        grid_spec=pltpu.PrefetchScalarGridSpec(
            num_scalar_prefetch=1, grid=(N_CHIPS, n_tiles),
            in_specs=[pl.BlockSpec((1, tile, w), lambda j, i, c_ref: (j, c_ref[0] * n_tiles + i, 0)),
                      pl.BlockSpec((1, tile, w), lambda j, i, c_ref: (j, i, 0))],
            out_specs=pl.BlockSpec((1, tile, w), lambda j, i, c_ref: (j, i, 0))),
        out_shape=jax.ShapeDtypeStruct(got.shape, out_dtype),
        compiler_params=_cparams("parallel", "parallel"),
    )(c_arr, g_all, got)


def _sum_slots(a, name):
    n, w = a.shape[1:]
    tile = n // 4

    def body(a_ref, o_ref):
        o_ref[...] = ((a_ref[0].astype(f32) + a_ref[1].astype(f32)) + a_ref[2].astype(f32)) + a_ref[3].astype(f32)

    return pl.pallas_call(
        body, name=name, grid=(n // tile,),
        in_specs=[pl.BlockSpec((N_CHIPS, tile, w), lambda i: (0, i, 0))],
        out_specs=pl.BlockSpec((tile, w), lambda i: (i, 0)),
        out_shape=jax.ShapeDtypeStruct((n, w), f32),
        compiler_params=_cparams("parallel"),
    )(a)


def _adamw(w, g, m, v, name):
    rows, width = w.shape
    tile = rows // 8
    c1 = 1.0 / (1.0 - ADAM_B1 ** ADAM_STEP)
    c2 = 1.0 / (1.0 - ADAM_B2 ** ADAM_STEP)

    def body(w_ref, g_ref, m_ref, v_ref, d_ref, mo_ref, vo_ref):
        g_v = g_ref[...]
        m_new = ADAM_B1 * m_ref[...] + (1.0 - ADAM_B1) * g_v
        v_new = ADAM_B2 * v_ref[...] + (1.0 - ADAM_B2) * (g_v * g_v)
        mo_ref[...] = m_new
        vo_ref[...] = v_new
        d_ref[...] = -ADAM_LR * ((m_new * c1) / (jnp.sqrt(v_new * c2) + ADAM_EPS) + ADAM_WD * w_ref[...])

    blk = pl.BlockSpec((tile, width), lambda i: (i, 0))
    sds = jax.ShapeDtypeStruct((rows, width), f32)
    return pl.pallas_call(
        body, name=name, grid=(rows // tile,),
        in_specs=[blk] * 4, out_specs=[blk] * 3, out_shape=[sds] * 3,
        compiler_params=_cparams("parallel"),
    )(w, g, m, v)


R_DW = 3 * SQ_BLK
R_CW = R_DW + 8
R_VEC = R_CW + 8
R_SMALL = R_VEC + 8
REST_ROWS = 896


def _pack_small(conf_dw_w, gdn_conv_w, vecs, a_log, dt_bias, norm_g):
    dw = jnp.pad(conf_dw_w.reshape(-1), (0, 8 * D - KC * SQ_BLK)).reshape(8, D)
    cw = jnp.pad(gdn_conv_w.reshape(-1), (0, 5 * D)).reshape(8, D)
    vec = jnp.pad(jnp.stack(vecs), ((0, 3), (0, 0)))
    small = jnp.pad(jnp.concatenate([a_log, dt_bias, norm_g]), (0, D - 2 * NH - HD)).reshape(1, D)
    return jnp.pad(jnp.concatenate([dw, cw, vec, small], axis=0), ((0, REST_ROWS - R_SMALL - 1), (0, 0)))


def _pack_rest(conf_w_out, gdn_w_out, w_o, small):
    return jnp.concatenate([conf_w_out, gdn_w_out, w_o, small], axis=0)


def _unpack_rest(p):
    conf_dw_w = p[R_DW:R_DW + 8].reshape(-1)[:KC * SQ_BLK].reshape(KC, SQ_BLK)
    gdn_conv_w = p[R_CW:R_CW + 3].reshape(KG, 3 * SQ_BLK)
    small = p[R_SMALL]
    return dict(conf_w_out=p[0:SQ_BLK], gdn_w_out=p[SQ_BLK:2 * SQ_BLK], w_o=p[2 * SQ_BLK:R_DW],
                conf_dw_w=conf_dw_w, gdn_conv_w=gdn_conv_w, conf_dw_b=p[R_VEC], conf_ln_g=p[R_VEC + 1],
                conf_ln_b=p[R_VEC + 2], post_ln_g=p[R_VEC + 3], post_ln_b=p[R_VEC + 4],
                gdn_A_log=small[0:NH], gdn_dt_bias=small[NH:2 * NH], gdn_norm_g=small[2 * NH:2 * NH + HD])


_WEIGHT_ORDER = ("w_in", "conf_dw_w", "conf_dw_b", "conf_ln_g", "conf_ln_b", "conf_w_out", "gdn_conv_w",
                 "gdn_A_log", "gdn_dt_bias", "gdn_norm_g", "gdn_w_out", "w_o", "post_ln_g", "post_ln_b")


def _gather_weights(w_in, conf_w_out, gdn_w_out, w_o, conf_dw_w, gdn_conv_w):
    c = lax.axis_index("c")
    sq = jnp.concatenate([conf_w_out, gdn_w_out, w_o], axis=0).astype(bf16)
    w_half = lax.dynamic_slice_in_dim(w_in.astype(bf16), c * (D // 2), D // 2, axis=0)
    sq_half = lax.dynamic_slice_in_dim(sq, c * (sq.shape[0] // 2), sq.shape[0] // 2, axis=0)
    small = jnp.concatenate([jnp.pad(conf_dw_w.reshape(-1), (0, 8 * D - KC * SQ_BLK)).reshape(8, D),
                             jnp.pad(gdn_conv_w.reshape(-1), (0, 5 * D)).reshape(8, D)], axis=0)
    got_w, got_sq, small_all = _chip_exchange([w_half, sq_half, small], "weight_gather_chips", scatter=False)
    oth_w, oth_sq = _sibling_merge([got_w, got_sq], "weight_gather_sibling")
    w4 = _join_halves(got_w, oth_w)
    sq4 = _join_halves(got_sq, oth_sq)
    sq_full = [sq4[:, n * SQ_BLK:(n + 1) * SQ_BLK].reshape(D, D) for n in range(3)]
    dw_full = small_all[:, 0:8].reshape(N_CHIPS, 8 * D)[:, :KC * SQ_BLK].reshape(N_CHIPS, KC, SQ_BLK)
    dw_full = dw_full.transpose(1, 0, 2).reshape(KC, D)
    cw_full = small_all[:, 8:11].reshape(N_CHIPS, KG, 3 * SQ_BLK).transpose(1, 0, 2).reshape(KG, 3 * D)
    return w4, sq_full[0], sq_full[1], sq_full[2], dw_full, cw_full


def _w_in_cols(w4, lo, hi):
    parts = []
    for j in range(N_CHIPS):
        a, b = max(lo, j * W_IN_BLK), min(hi, (j + 1) * W_IN_BLK)
        if a < b:
            parts.append(w4[j, :, a - j * W_IN_BLK:b - j * W_IN_BLK])
    return parts[0] if len(parts) == 1 else jnp.concatenate(parts, axis=1)


def _w_in_by_chip(pieces):
    chips = []
    for j in range(N_CHIPS):
        lo, hi = j * W_IN_BLK, (j + 1) * W_IN_BLK
        parts = []
        for start, arr in pieces:
            a, b = max(lo, start), min(hi, start + arr.shape[1])
            if a < b:
                parts.append(arr[:, a - start:b - start])
        chips.append(jnp.concatenate(parts, axis=1))
    return jnp.stack(chips)


def _reduce_scatter(g_w, g_rest):
    c_arr = lax.axis_index("c").astype(jnp.int32).reshape(1)
    got_w, got_r = _sibling_merge([g_w, g_rest], "grad_sibling_halves", take_other_half=True)
    pair_w = _pair_sum(g_w, got_w, c_arr, "grad_pair_sum_w_in", bf16)
    pair_r = _pair_sum(g_rest, got_r, c_arr, "grad_pair_sum_rest", f32)
    all_w, all_r = _chip_exchange([pair_w, pair_r], "grad_chip_scatter", scatter=True)
    tot_w, tot_r = _sum_slots(all_w, "grad_chip_sum_w_in"), _sum_slots(all_r, "grad_chip_sum_rest")
    oth_w, oth_r = _sibling_merge([tot_w, tot_r], "grad_sibling_result")
    return _join_halves(tot_w, oth_w), _join_halves(tot_r, oth_r)


def _local_step(x2, tgt, w4, wc_out, wg_out, wo_full, dw_full, cw_full, conf_dw_b, conf_ln_g, conf_ln_b,
                gdn_A_log, gdn_dt_bias, gdn_norm_g, post_ln_g, post_ln_b):
    t = x2.shape[0]
    tt = min(TOKEN_TILE, t)
    tm = min(512, t)

    w_conv, w_qkv, w_gz = _w_in_cols(w4, 0, 3 * D), _w_in_cols(w4, 3 * D, 6 * D), _w_in_cols(w4, 6 * D, 7 * D)
    w_gates = _w_in_cols(w4, 7 * D + 2 * NH, W_IN_COLS)
    dw_pad = jnp.pad(dw_full, ((0, HALO_C - KC), (0, 0)))
    cw_pad = jnp.pad(cw_full, ((0, 8 - KG), (0, 0)))
    row = lambda v: v.reshape(1, D)
    alog_b = row(jnp.repeat(gdn_A_log, HD))
    dt_b = row(jnp.repeat(gdn_dt_bias, HD))
    ng_b = row(jnp.tile(gdn_norm_g, NH))
    w_ba = jnp.pad(_w_in_cols(w4, 7 * D, 7 * D + 2 * NH), ((0, 0), (0, HD - 2 * NH)))

    x_b = x2.astype(bf16)

    p_conv = _mm_multi([x_b], [w_conv], out_dtype=f32, tm=tm, tn=1024, name="proj_conv")
    p_qkv = _mm_multi([x_b], [w_qkv], out_dtype=f32, tm=tm, tn=1024, name="proj_qkv")
    p_gz = _mm_multi([x_b], [w_gz], out_dtype=f32, tm=tm, tn=1024, name="proj_gz")
    p_gates = _mm_multi([x_b], [w_gates], out_dtype=f32, tm=tm, tn=1024, name="proj_gates")
    p_ba = _mm_multi([x_b], [w_ba], out_dtype=f32, tm=tm, tn=HD, name="proj_ba")

    u = _conv_fwd(p_conv, dw_pad, row(conf_dw_b), row(conf_ln_g), row(conf_ln_b), tt=tt)
    y_conf = _mm_multi([u], [wc_out], out_dtype=f32, tm=tm, tn=1024, name="conf_out")

    qn, kn, va, gb, bb = _gdn_pre_fwd(p_qkv, p_ba, cw_pad, alog_b, dt_b, tt=tt)
    o, states = _gdn_scan_fwd(qn, kn, va, gb, bb, tt=tt)
    og = _gdn_post_fwd(o, p_gz, ng_b, tt=tt)
    y_gdn = _mm_multi([og], [wg_out], out_dtype=f32, tm=tm, tn=1024, name="gdn_out")

    loss_blk, dxd, dyc, dyg, dp_gates, h, dz, dpost = _merge(
        x2, y_conf, y_gdn, p_gates, tgt, wo_full, row(post_ln_g), row(post_ln_b), tt=tt)

    d_wo = _mm_kloop(h, dz, tm=tm, tn=1024, tk=min(1024, t), name="grad_w_o")
    du = _mm_multi([dyc], [wc_out], out_dtype=f32, tm=tm, tn=1024, name="conf_out_bwd", rhs_t=True)
    d_wc = _mm_kloop(u, dyc, tm=tm, tn=1024, tk=min(1024, t), name="grad_conf_w_out")
    dog = _mm_multi([dyg], [wg_out], out_dtype=f32, tm=tm, tn=1024, name="gdn_out_bwd", rhs_t=True)
    d_wg = _mm_kloop(og, dyg, tm=tm, tn=1024, tk=min(1024, t), name="grad_gdn_w_out")

    dp_conv, d_dww, dconv_vec = _conv_bwd(p_conv, du, dw_pad, row(conf_dw_b), row(conf_ln_g), row(conf_ln_b), tt=tt)

    do, dp_gz, dng = _gdn_post_bwd(o, p_gz, ng_b, dog, tt=tt)
    dqn, dkn, dva, dgb, dbb = _gdn_scan_bwd(qn, kn, va, gb, bb, states, do, tt=tt)
    dp_qkv, dp_ba, d_cw, d_ad = _gdn_pre_bwd(p_qkv, p_ba, cw_pad, alog_b, dt_b, dqn, dkn, dva, dgb, dbb, tt=tt)
    dp_ba_b = dp_ba.astype(bf16)

    grad_x = _mm_multi(
        [dp_conv, dp_qkv, dp_gz, dp_gates, dp_ba_b],
        [w_conv, w_qkv, w_gz, w_gates, w_ba], dxd,
        out_dtype=f32, tm=min(256, t), tn=512, name="grad_x", rhs_t=True)

    tk = min(1024, t)
    d_w_conv = _mm_kloop(x_b, dp_conv, tm=tm, tn=1024, tk=tk, name="grad_w_in_conv")
    d_w_qkv = _mm_kloop(x_b, dp_qkv, tm=tm, tn=1024, tk=tk, name="grad_w_in_qkv")
    d_w_gz = _mm_kloop(x_b, dp_gz, tm=tm, tn=1024, tk=tk, name="grad_w_in_gz")
    d_w_gates = _mm_kloop(x_b, dp_gates, tm=tm, tn=1024, tk=tk, name="grad_w_in_gates")
    d_w_ba = _mm_kloop(x_b, dp_ba_b, tm=tm, tn=HD, tk=tk, name="grad_w_in_ba")
    d_w_in = _w_in_by_chip([(0, d_w_conv), (3 * D, d_w_qkv), (6 * D, d_w_gz), (7 * D, d_w_ba[:, :2 * NH]),
                            (7 * D + 2 * NH, d_w_gates)])

    return (loss_blk[0, 0], grad_x, d_w_in, d_wc, d_wg, d_wo, d_dww, d_cw, dconv_vec, dpost, d_ad, dng)


def kernel(x, w_in, conf_dw_w, conf_dw_b, conf_ln_g, conf_ln_b, conf_w_out, gdn_conv_w, gdn_A_log, gdn_dt_bias, gdn_norm_g, gdn_w_out, w_o, post_ln_g, post_ln_b, loss_target, m_w_in, m_conf_dw_w, m_conf_dw_b, m_conf_ln_g, m_conf_ln_b, m_conf_w_out, m_gdn_conv_w, m_gdn_A_log, m_gdn_dt_bias, m_gdn_norm_g, m_gdn_w_out, m_w_o, m_post_ln_g, m_post_ln_b, v_w_in, v_conf_dw_w, v_conf_dw_b, v_conf_ln_g, v_conf_ln_b, v_conf_w_out, v_gdn_conv_w, v_gdn_A_log, v_gdn_dt_bias, v_gdn_norm_g, v_gdn_w_out, v_w_o, v_post_ln_g, v_post_ln_b):
    x2 = x.reshape(x.shape[-2], D)
    tgt = loss_target.reshape(x2.shape)
    w4, wc_out, wg_out, wo_full, dw_full, cw_full = _gather_weights(
        w_in, conf_w_out, gdn_w_out, w_o, conf_dw_w, gdn_conv_w)
    (loss_part, grad_x, d_w_in, d_wc, d_wg, d_wo, d_dww, d_cw, dconv_vec, dpost, d_ad, dng) = _local_step(
        x2, tgt, w4, wc_out, wg_out, wo_full, dw_full, cw_full, conf_dw_b, conf_ln_g, conf_ln_b,
        gdn_A_log, gdn_dt_bias, gdn_norm_g, post_ln_g, post_ln_b)
    loss = lax.psum(loss_part, ("x", "y", "c"))

    dww_c = d_dww[:KC].reshape(KC, N_CHIPS, SQ_BLK)
    dcw_c = d_cw[:KG].reshape(KG, N_CHIPS, 3 * SQ_BLK)
    vecs = [dconv_vec[0], dconv_vec[1], dconv_vec[2], dpost[0], dpost[1]]
    g_rest = jnp.stack([
        _pack_rest(d_wc[j * SQ_BLK:(j + 1) * SQ_BLK], d_wg[j * SQ_BLK:(j + 1) * SQ_BLK], d_wo[j * SQ_BLK:(j + 1) * SQ_BLK],
                   _pack_small(dww_c[:, j], dcw_c[:, j], vecs, d_ad[0, :NH], d_ad[1, :NH], dng[0]))
        for j in range(N_CHIPS)])
    g_w_in, g_rest = _reduce_scatter(d_w_in, g_rest)

    def rest_of(w_c, w_g, w_oo, dw, cw, b1, g1, b2, g2, b3, a_log, dt_bias, norm_g):
        return _pack_rest(w_c, w_g, w_oo, _pack_small(dw, cw, [b1, g1, b2, g2, b3], a_log, dt_bias, norm_g))

    w_r = rest_of(conf_w_out, gdn_w_out, w_o, conf_dw_w, gdn_conv_w, conf_dw_b, conf_ln_g, conf_ln_b,
                  post_ln_g, post_ln_b, gdn_A_log, gdn_dt_bias, gdn_norm_g)
    m_r = rest_of(m_conf_w_out, m_gdn_w_out, m_w_o, m_conf_dw_w, m_gdn_conv_w, m_conf_dw_b, m_conf_ln_g, m_conf_ln_b,
                  m_post_ln_g, m_post_ln_b, m_gdn_A_log, m_gdn_dt_bias, m_gdn_norm_g)
    v_r = rest_of(v_conf_w_out, v_gdn_w_out, v_w_o, v_conf_dw_w, v_gdn_conv_w, v_conf_dw_b, v_conf_ln_g, v_conf_ln_b,
                  v_post_ln_g, v_post_ln_b, v_gdn_A_log, v_gdn_dt_bias, v_gdn_norm_g)
    upd_w_in = _adamw(w_in, g_w_in, m_w_in, v_w_in, "adamw_w_in")
    upd_rest = _adamw(w_r, g_rest, m_r, v_r, "adamw_rest")

    out = [loss, grad_x.reshape(x.shape)]
    for big, rest in zip((g_w_in,) + tuple(upd_w_in), (g_rest,) + tuple(upd_rest)):
        d = dict(_unpack_rest(rest), w_in=big)
        out += [d[n] for n in _WEIGHT_ORDER]
    return tuple(out)
```

```python
import functools

import jax
import jax.numpy as jnp
from jax import lax
from jax.experimental import pallas as pl
from jax.experimental.pallas import tpu as pltpu

f32 = jnp.float32
bf16 = jnp.bfloat16
HI = lax.Precision.HIGHEST
MESH = pl.DeviceIdType.MESH

D = 1024
NH = 8
HD = 128
CH = 64
KC = 31
KG = 4
HALO_C = 32
HALO_G = 8
LANE = 128
STRIP = 32
N_SHIFT = 7
LN_EPS = 1e-5
RMS_EPS = 1e-6
L2_EPS = 1e-6
DN_ALPHA = 2.0 ** 0.25
N_CHIPS = 4
W_IN_COLS = 9232
W_IN_BLK = W_IN_COLS // N_CHIPS
SQ_BLK = D // N_CHIPS
VMEM_LIMIT = 52 * 1024 * 1024
TOKEN_TILE = 256

ADAM_LR = 0.001
ADAM_B1 = 0.9
ADAM_B2 = 0.999
ADAM_EPS = 1e-08
ADAM_WD = 0.01
ADAM_STEP = 10


def _sigmoid(x):
    return 1.0 / (1.0 + jnp.exp(-x))


def _silu_and_grad(x):
    s = _sigmoid(x)
    return x * s, s * (1.0 + x * (1.0 - s))


_NN = ((1,), (0,))
_NT = ((1,), (1,))
_TN = ((0,), (0,))


def _cparams(*sem):
    return pltpu.CompilerParams(dimension_semantics=sem, vmem_limit_bytes=VMEM_LIMIT)


def _mm_multi(a_list, b_list, addend=None, *, out_dtype, tm, tn, name, rhs_t=False):
    n_pairs = len(a_list)
    m = a_list[0].shape[0]
    n = b_list[0].shape[0 if rhs_t else 1]
    has_add = addend is not None
    dims = (_NT if rhs_t else _NN, ((), ()))

    def body(*refs):
        a_refs = refs[:n_pairs]
        b_refs = refs[n_pairs:2 * n_pairs]
        o_ref = refs[-1]
        acc = None
        for a_ref, b_ref in zip(a_refs, b_refs):
            p = lax.dot_general(a_ref[...].astype(bf16), b_ref[...].astype(bf16), dims, preferred_element_type=f32)
            acc = p if acc is None else acc + p
        if has_add:
            acc = acc + refs[2 * n_pairs][...]
        o_ref[...] = acc.astype(out_dtype)

    in_specs = [pl.BlockSpec((tm, a.shape[1]), lambda j, i: (i, 0)) for a in a_list]
    if rhs_t:
        in_specs += [pl.BlockSpec((tn, b.shape[1]), lambda j, i: (j, 0)) for b in b_list]
    else:
        in_specs += [pl.BlockSpec((b.shape[0], tn), lambda j, i: (0, j)) for b in b_list]
    args = list(a_list) + list(b_list)
    if has_add:
        in_specs.append(pl.BlockSpec((tm, tn), lambda j, i: (i, j)))
        args.append(addend)
    return pl.pallas_call(
        body, name=name, grid=(n // tn, m // tm),
        in_specs=in_specs, out_specs=pl.BlockSpec((tm, tn), lambda j, i: (i, j)),
        out_shape=jax.ShapeDtypeStruct((m, n), out_dtype),
        compiler_params=_cparams("parallel", "parallel"),
    )(*args)


def _mm_kloop(a, b, *, tm, tn, tk, name):
    k, m = a.shape
    n = b.shape[1]
    nk = k // tk

    def body(a_ref, b_ref, o_ref):
        @pl.when(pl.program_id(2) == 0)
        def _():
            o_ref[...] = jnp.zeros_like(o_ref)
        o_ref[...] += lax.dot_general(a_ref[...].astype(bf16), b_ref[...].astype(bf16), (_TN, ((), ())),
                                      preferred_element_type=f32)

    return pl.pallas_call(
        body, name=name, grid=(n // tn, m // tm, nk),
        in_specs=[pl.BlockSpec((tk, tm), lambda j, i, kk: (kk, i)), pl.BlockSpec((tk, tn), lambda j, i, kk: (kk, j))],
        out_specs=pl.BlockSpec((tm, tn), lambda j, i, kk: (i, j)),
        out_shape=jax.ShapeDtypeStruct((m, n), f32),
        compiler_params=_cparams("parallel", "parallel", "arbitrary"),
    )(a, b)


def _shift_copies(src_ref, sh_ref, n, shifts=tuple(range(1, 8))):
    for i, b in enumerate(shifts):
        sh_ref[i, 0:n, :] = src_ref[pl.ds(b, n), :]


def _by_residue(offs):
    groups = {}
    for k, off in enumerate(offs):
        groups.setdefault(off % 8, []).append((k, off // 8))
    return groups


def _slab(src_ref, sh_ref, shifts, b, r0, n, lanes):
    ref = src_ref if b == 0 else sh_ref.at[shifts.index(b)]
    return ref[r0:r0 + n, lanes]


def _tap_conv(out_ref, n_rows, src_ref, sh_ref, w_ref, offs, bias_ref=None, shifts=tuple(range(1, 8))):
    groups = _by_residue(offs)
    for j in range(D // LANE):
        lanes = slice(j * LANE, (j + 1) * LANE)
        wv = [w_ref[k:k + 1, lanes] for k in range(len(offs))]
        for r0 in range(0, n_rows, STRIP):
            n = min(STRIP, n_rows - r0)
            accs = [jnp.zeros((n, LANE), f32) if bias_ref is None else jnp.broadcast_to(bias_ref[0:1, lanes], (n, LANE)),
                    jnp.zeros((n, LANE), f32)]
            m = 0
            for b, taps in groups.items():
                a_lo = min(a for _, a in taps)
                a_hi = max(a for _, a in taps)
                wide = _slab(src_ref, sh_ref, shifts, b, r0 + 8 * a_lo, 8 * (a_hi - a_lo) + n, lanes)
                for k, a in taps:
                    accs[m % 2] = accs[m % 2] + wv[k] * wide[8 * (a - a_lo):8 * (a - a_lo) + n]
                    m += 1
            out_ref[r0:r0 + n, lanes] = accs[0] + accs[1]


def _tap_corr(dw_ref, n_rows, lhs_ref, src_ref, sh_ref, offs, shifts=tuple(range(1, 8))):
    groups = _by_residue(offs)
    for j in range(D // LANE):
        lanes = slice(j * LANE, (j + 1) * LANE)
        accs = [jnp.zeros((8, LANE), f32) for _ in offs]
        for r0 in range(0, n_rows, STRIP):
            n = min(STRIP, n_rows - r0)
            d = lhs_ref[r0:r0 + n, lanes]
            for b, taps in groups.items():
                a_lo = min(a for _, a in taps)
                a_hi = max(a for _, a in taps)
                wide = _slab(src_ref, sh_ref, shifts, b, r0 + 8 * a_lo, 8 * (a_hi - a_lo) + n, lanes)
                for k, a in taps:
                    prod = d * wide[8 * (a - a_lo):8 * (a - a_lo) + n]
                    part = prod[0:8]
                    for q in range(1, n // 8):
                        part = part + prod[8 * q:8 * q + 8]
                    accs[k] = accs[k] + part
        for k in range(len(offs)):
            dw_ref[k:k + 1, lanes] += jnp.sum(accs[k], axis=0, keepdims=True)


_FWD_OFFS = [HALO_C - (KC - 1) + k for k in range(KC)]
_BWD_OFFS = [KC - 1 - k for k in range(KC)]


def _norm_act(a1, cz, g_ref, bb_ref):
    mu = jnp.mean(a1, axis=-1, keepdims=True)
    cen = a1 - mu
    var = jnp.mean(cen * cen, axis=-1, keepdims=True)
    rstd = lax.rsqrt(var + LN_EPS)
    xhat = cen * rstd
    ln = xhat * g_ref[...] + bb_ref[...]
    s, ds = _silu_and_grad(ln)
    zc, dzc = _silu_and_grad(cz)
    return xhat, rstd, s, ds, zc, dzc


def _conv_fwd(p_conv, dw_w, dw_b, ln_g, ln_b, *, tt):
    t = p_conv.shape[0]
    hb = tt // HALO_C

    def body(cv_ref, cg_ref, cz_ref, cvh_ref, cgh_ref, w_ref, b_ref, g_ref, bb_ref, u_ref, ext_ref, sh_ref, a1_ref):
        first = pl.program_id(0) == 0
        halo = cvh_ref[...] * _sigmoid(cgh_ref[...])
        ext_ref[0:HALO_C, :] = jnp.where(first, 0.0, halo)
        ext_ref[HALO_C:, :] = cv_ref[...] * _sigmoid(cg_ref[...])
        _shift_copies(ext_ref, sh_ref, tt + HALO_C - 8)
        _tap_conv(a1_ref, tt, ext_ref, sh_ref, w_ref, _FWD_OFFS, b_ref)
        _, _, s, _, zc, _ = _norm_act(a1_ref[...], cz_ref[...], g_ref, bb_ref)
        u_ref[...] = (s * zc).astype(bf16)

    def main(col):
        return pl.BlockSpec((tt, D), lambda i: (i, col))

    def prev(col):
        return pl.BlockSpec((HALO_C, D), lambda i: (jnp.maximum(i * hb - 1, 0), col))

    vec = pl.BlockSpec((1, D), lambda i: (0, 0))
    return pl.pallas_call(
        body, name="conv_fwd", grid=(t // tt,),
        in_specs=[main(0), main(1), main(2), prev(0), prev(1),
                  pl.BlockSpec((HALO_C, D), lambda i: (0, 0)), vec, vec, vec],
        out_specs=pl.BlockSpec((tt, D), lambda i: (i, 0)),
        out_shape=jax.ShapeDtypeStruct((t, D), bf16),
        scratch_shapes=[pltpu.VMEM((tt + HALO_C, D), f32), pltpu.VMEM((N_SHIFT, tt + HALO_C - 8, D), f32),
                        pltpu.VMEM((tt, D), f32)],
        compiler_params=_cparams("parallel"),
    )(p_conv, p_conv, p_conv, p_conv, p_conv, dw_w, dw_b, ln_g, ln_b)


def _conv_bwd(p_conv, du, dw_w, dw_b, ln_g, ln_b, *, tt):
    t = p_conv.shape[0]
    hb = tt // HALO_C
    n_tiles = t // tt
    last_hb = t // HALO_C - 1
    ne = tt + HALO_C

    def body(cv_ref, cg_ref, cz_ref, du_ref, cvp_ref, cgp_ref, cvn_ref, cgn_ref, czn_ref, dun_ref,
             w_ref, b_ref, g_ref, bb_ref, dp_ref, dww_ref, dvec_ref, ext_ref, sh_ref, a1_ref, da1_ref, da0_ref):
        i = pl.program_id(0)
        first = i == 0
        last = i == n_tiles - 1

        @pl.when(first)
        def _():
            dww_ref[...] = jnp.zeros_like(dww_ref)
            dvec_ref[...] = jnp.zeros_like(dvec_ref)

        sig = _sigmoid(cg_ref[...])
        ext_ref[0:HALO_C, :] = jnp.where(first, 0.0, cvp_ref[...] * _sigmoid(cgp_ref[...]))
        ext_ref[HALO_C:HALO_C + tt, :] = cv_ref[...] * sig
        ext_ref[HALO_C + tt:, :] = cvn_ref[...] * _sigmoid(cgn_ref[...])
        _shift_copies(ext_ref, sh_ref, ne + HALO_C - 8)
        _tap_conv(a1_ref, ne, ext_ref, sh_ref, w_ref, _FWD_OFFS, b_ref)
        cz = jnp.concatenate([cz_ref[...], czn_ref[...]], axis=0)
        du_all = jnp.concatenate([du_ref[...], jnp.where(last, 0.0, dun_ref[...])], axis=0)
        xhat, rstd, s, ds, zc, dzc = _norm_act(a1_ref[...], cz, g_ref, bb_ref)
        dln = du_all * zc * ds
        dxhat = dln * g_ref[...]
        da1 = rstd * (dxhat - jnp.mean(dxhat, axis=-1, keepdims=True)
                      - xhat * jnp.mean(dxhat * xhat, axis=-1, keepdims=True))
        da1_ref[...] = da1
        dcz = (du_all * s * dzc)[:tt]
        dvec_ref[0:1, :] += jnp.sum(da1[:tt], axis=0, keepdims=True)
        dvec_ref[1:2, :] += jnp.sum((dln * xhat)[:tt], axis=0, keepdims=True)
        dvec_ref[2:3, :] += jnp.sum(dln[:tt], axis=0, keepdims=True)
        _tap_corr(dww_ref, tt, da1_ref, ext_ref, sh_ref, _FWD_OFFS)
        _shift_copies(da1_ref, sh_ref, ne - 8)
        _tap_conv(da0_ref, tt, da1_ref, sh_ref, w_ref, _BWD_OFFS)
        da0 = da0_ref[...]
        cv = cv_ref[...]
        dp_ref[:, 0:D] = (da0 * sig).astype(bf16)
        dp_ref[:, D:2 * D] = (da0 * cv * sig * (1.0 - sig)).astype(bf16)
        dp_ref[:, 2 * D:] = dcz.astype(bf16)

    def main(col):
        return pl.BlockSpec((tt, D), lambda i: (i, col))

    def prev(col):
        return pl.BlockSpec((HALO_C, D), lambda i: (jnp.maximum(i * hb - 1, 0), col))

    def nxt(col):
        return pl.BlockSpec((HALO_C, D), lambda i: (jnp.minimum((i + 1) * hb, last_hb), col))

    vec = pl.BlockSpec((1, D), lambda i: (0, 0))
    return pl.pallas_call(
        body, name="conv_bwd", grid=(n_tiles,),
        in_specs=[main(0), main(1), main(2), main(0), prev(0), prev(1), nxt(0), nxt(1), nxt(2), nxt(0),
                  pl.BlockSpec((HALO_C, D), lambda i: (0, 0)), vec, vec, vec],
        out_specs=[pl.BlockSpec((tt, 3 * D), lambda i: (i, 0)),
                   pl.BlockSpec((HALO_C, D), lambda i: (0, 0)),
                   pl.BlockSpec((8, D), lambda i: (0, 0))],
        out_shape=[jax.ShapeDtypeStruct((t, 3 * D), bf16), jax.ShapeDtypeStruct((HALO_C, D), f32),
                   jax.ShapeDtypeStruct((8, D), f32)],
        scratch_shapes=[pltpu.VMEM((ne + HALO_C, D), f32), pltpu.VMEM((N_SHIFT, ne + HALO_C - 8, D), f32),
                        pltpu.VMEM((ne, D), f32), pltpu.VMEM((ne, D), f32), pltpu.VMEM((tt, D), f32)],
        compiler_params=_cparams("arbitrary"),
    )(p_conv, p_conv, p_conv, du, p_conv, p_conv, p_conv, p_conv, p_conv, du, dw_w, dw_b, ln_g, ln_b)


def _dot_hi(a, b):
    return lax.dot_general(a, b, (((1,), (0,)), ((), ())), precision=HI, preferred_element_type=f32)


def _chunk_tri(n, lower):
    r = lax.broadcasted_iota(jnp.int32, (n, n), 0)
    c = lax.broadcasted_iota(jnp.int32, (n, n), 1)
    tri = (r >= c) if lower else (r <= c)
    return jnp.where(tri & (r // CH == c // CH), 1.0, 0.0).astype(f32)


def _softplus_and_sigmoid(x):
    e = jnp.exp(-jnp.abs(x))
    log1p = jnp.where(e < 1e-2, e * (1.0 - e * (0.5 - e * (1.0 / 3.0 - 0.25 * e))), jnp.log(1.0 + e))
    return jnp.maximum(x, 0.0) + log1p, _sigmoid(x)


_G_FWD_OFFS = [HALO_G - (KG - 1) + k for k in range(KG)]
_G_FWD_SHIFTS = (5, 6, 7)
_G_BWD_OFFS = [KG - 1 - k for k in range(KG)]
_G_BWD_SHIFTS = (1, 2, 3)


def _gdn_short_conv(pre_ref, ext_ref, sh_ref, n_rows, w_ref):
    _shift_copies(ext_ref, sh_ref, n_rows, _G_FWD_SHIFTS)
    _tap_conv(pre_ref, n_rows, ext_ref, sh_ref, w_ref, _G_FWD_OFFS, shifts=_G_FWD_SHIFTS)
    return pre_ref[...]


def _l2norm_heads(act, scale):
    outs, rs = [], []
    for h in range(NH):
        a = act[:, h * HD:(h + 1) * HD]
        r = lax.rsqrt(jnp.sum(a * a, axis=-1, keepdims=True) + L2_EPS)
        outs.append(a * (r * scale))
        rs.append(jnp.broadcast_to(r, a.shape))
    return jnp.concatenate(outs, axis=-1), jnp.concatenate(rs, axis=-1)


def _gate_math(ba, al_ref, dt_ref):
    lane = lax.broadcasted_iota(jnp.int32, ba.shape, 1)
    is_b = lane < NH
    is_a = (lane >= NH) & (lane < 2 * NH)
    sp, sg = _softplus_and_sigmoid(ba + dt_ref[...])
    neg_a = -jnp.exp(al_ref[...])
    return is_b, is_a, _sigmoid(ba), neg_a * sp, sg, neg_a


def _gdn_pre_fwd(p_qkv, p_ba, cw, alog_v, dt_v, *, tt):
    t = p_qkv.shape[0]
    hb = tt // HALO_G

    def body(q_ref, k_ref, v_ref, qh_ref, kh_ref, vh_ref, ba_ref, wq_ref, wk_ref, wv_ref, al_ref, dt_ref,
             qn_ref, kn_ref, va_ref, gt_ref, ext_ref, sh_ref, pre_ref):
        first = pl.program_id(0) == 0

        def conv_act(x_ref, xh_ref, w_ref):
            ext_ref[0:HALO_G, :] = jnp.where(first, 0.0, xh_ref[...])
            ext_ref[HALO_G:, :] = x_ref[...]
            pre = _gdn_short_conv(pre_ref, ext_ref, sh_ref, tt, w_ref)
            return pre * _sigmoid(pre)

        qn_ref[...] = _l2norm_heads(conv_act(q_ref, qh_ref, wq_ref), HD ** -0.5)[0]
        kn_ref[...] = _l2norm_heads(conv_act(k_ref, kh_ref, wk_ref), 1.0)[0]
        va_ref[...] = conv_act(v_ref, vh_ref, wv_ref)
        is_b, is_a, beta, g, _, _ = _gate_math(ba_ref[...], al_ref, dt_ref)
        gc = _dot_hi(_chunk_tri(tt, lower=True), jnp.where(is_a, g, 0.0))
        gt_ref[...] = jnp.where(is_b, beta, gc)

    def main(col):
        return pl.BlockSpec((tt, D), lambda i: (i, col))

    def prev(col):
        return pl.BlockSpec((HALO_G, D), lambda i: (jnp.maximum(i * hb - 1, 0), col))

    def wspec(col):
        return pl.BlockSpec((8, D), lambda i: (0, col))

    vec = pl.BlockSpec((1, HD), lambda i: (0, 0))
    gblk = pl.BlockSpec((tt, HD), lambda i: (i, 0))
    sds = jax.ShapeDtypeStruct((t, D), f32)
    return pl.pallas_call(
        body, name="gdn_pre_fwd", grid=(t // tt,),
        in_specs=[main(0), main(1), main(2), prev(0), prev(1), prev(2), gblk, wspec(0), wspec(1), wspec(2), vec, vec],
        out_specs=[pl.BlockSpec((tt, D), lambda i: (i, 0))] * 3 + [gblk],
        out_shape=[sds] * 3 + [jax.ShapeDtypeStruct((t, HD), f32)],
        scratch_shapes=[pltpu.VMEM((tt + HALO_G, D), f32), pltpu.VMEM((KG - 1, tt, D), f32), pltpu.VMEM((tt, D), f32)],
        compiler_params=_cparams("parallel"),
    )(p_qkv, p_qkv, p_qkv, p_qkv, p_qkv, p_qkv, p_ba, cw, cw, cw, alog_v, dt_v)


def _gdn_pre_bwd(p_qkv, p_ba, cw, alog_v, dt_v, dqn, dkn, dva, dgt, *, tt):
    t = p_qkv.shape[0]
    hb = tt // HALO_G
    n_tiles = t // tt
    last_hb = t // HALO_G - 1
    ne = tt + HALO_G

    def body(q_ref, k_ref, v_ref, qp_ref, kp_ref, vp_ref, qx_ref, kx_ref, vx_ref,
             dq_ref, dk_ref, dv_ref, dqx_ref, dkx_ref, dvx_ref, ba_ref, dgt_ref,
             wq_ref, wk_ref, wv_ref, al_ref, dt_ref,
             dp_ref, dba_ref, dcw_ref, dad_ref, ext_ref, sh_ref, pre_ref, dpre_ref, draw_ref):
        i = pl.program_id(0)
        first = i == 0
        last = i == n_tiles - 1

        @pl.when(first)
        def _():
            dcw_ref[...] = jnp.zeros_like(dcw_ref)
            dad_ref[...] = jnp.zeros_like(dad_ref)

        def one(x_ref, xp_ref, xx_ref, d_ref, dx_ref, w_ref, col, scale):
            ext_ref[0:HALO_G, :] = jnp.where(first, 0.0, xp_ref[...])
            ext_ref[HALO_G:HALO_G + tt, :] = x_ref[...]
            ext_ref[HALO_G + tt:, :] = xx_ref[...]
            pre = _gdn_short_conv(pre_ref, ext_ref, sh_ref, ne, w_ref)
            act, dact = _silu_and_grad(pre)
            d_out = jnp.concatenate([d_ref[...], jnp.where(last, 0.0, dx_ref[...])], axis=0)
            if scale is None:
                d_act = d_out
            else:
                parts = []
                for h in range(NH):
                    a = act[:, h * HD:(h + 1) * HD]
                    dn = d_out[:, h * HD:(h + 1) * HD]
                    r = lax.rsqrt(jnp.sum(a * a, axis=-1, keepdims=True) + L2_EPS)
                    parts.append(scale * r * (dn - a * (r * r) * jnp.sum(dn * a, axis=-1, keepdims=True)))
                d_act = jnp.concatenate(parts, axis=-1)
            dpre_ref[...] = d_act * dact
            _tap_corr(dcw_ref.at[:, col * D:(col + 1) * D], tt, dpre_ref, ext_ref, sh_ref, _G_FWD_OFFS, shifts=_G_FWD_SHIFTS)
            _shift_copies(dpre_ref, sh_ref, tt, _G_BWD_SHIFTS)
            _tap_conv(draw_ref, tt, dpre_ref, sh_ref, w_ref, _G_BWD_OFFS, shifts=_G_BWD_SHIFTS)
            dp_ref[:, col * D:(col + 1) * D] = draw_ref[...].astype(bf16)

        one(q_ref, qp_ref, qx_ref, dq_ref, dqx_ref, wq_ref, 0, HD ** -0.5)
        one(k_ref, kp_ref, kx_ref, dk_ref, dkx_ref, wk_ref, 1, 1.0)
        one(v_ref, vp_ref, vx_ref, dv_ref, dvx_ref, wv_ref, 2, None)

        is_b, is_a, beta, g, sg, neg_a = _gate_math(ba_ref[...], al_ref, dt_ref)
        dgt_v = dgt_ref[...]
        dg = _dot_hi(_chunk_tri(tt, lower=False), jnp.where(is_a, dgt_v, 0.0))
        d_al = jnp.where(is_a, dg * neg_a * sg, 0.0)
        dba_ref[...] = jnp.where(is_b, dgt_v * beta * (1.0 - beta), d_al)
        dad_ref[0:1, :] += jnp.sum(jnp.where(is_a, dg * g, 0.0), axis=0, keepdims=True)
        dad_ref[1:2, :] += jnp.sum(d_al, axis=0, keepdims=True)

    def main(col):
        return pl.BlockSpec((tt, D), lambda i: (i, col))

    def prev(col):
        return pl.BlockSpec((HALO_G, D), lambda i: (jnp.maximum(i * hb - 1, 0), col))

    def nxt(col):
        return pl.BlockSpec((HALO_G, D), lambda i: (jnp.minimum((i + 1) * hb, last_hb), col))

    def wspec(col):
        return pl.BlockSpec((8, D), lambda i: (0, col))

    vec = pl.BlockSpec((1, HD), lambda i: (0, 0))
    gblk = pl.BlockSpec((tt, HD), lambda i: (i, 0))
    return pl.pallas_call(
        body, name="gdn_pre_bwd", grid=(n_tiles,),
        in_specs=[main(0), main(1), main(2), prev(0), prev(1), prev(2), nxt(0), nxt(1), nxt(2),
                  main(0), main(0), main(0), nxt(0), nxt(0), nxt(0), gblk, gblk,
                  wspec(0), wspec(1), wspec(2), vec, vec],
        out_specs=[pl.BlockSpec((tt, 3 * D), lambda i: (i, 0)), pl.BlockSpec((tt, HD), lambda i: (i, 0)),
                   pl.BlockSpec((8, 3 * D), lambda i: (0, 0)), pl.BlockSpec((8, HD), lambda i: (0, 0))],
        out_shape=[jax.ShapeDtypeStruct((t, 3 * D), bf16), jax.ShapeDtypeStruct((t, HD), f32),
                   jax.ShapeDtypeStruct((8, 3 * D), f32), jax.ShapeDtypeStruct((8, HD), f32)],
        scratch_shapes=[pltpu.VMEM((HALO_G + tt + HALO_G, D), f32), pltpu.VMEM((KG - 1, ne, D), f32),
                        pltpu.VMEM((ne, D), f32), pltpu.VMEM((ne, D), f32), pltpu.VMEM((tt, D), f32)],
        compiler_params=_cparams("arbitrary"),
    )(p_qkv, p_qkv, p_qkv, p_qkv, p_qkv, p_qkv, p_qkv, p_qkv, p_qkv,
      dqn, dkn, dva, dqn, dkn, dva, p_ba, dgt, cw, cw, cw, alog_v, dt_v)


def _dot_b(a, b, dims):
    return lax.dot_general(a.astype(bf16), b.astype(bf16), (dims, ((), ())), preferred_element_type=f32)


def _chunk_fn(qs, ks, vs, gcs, bbs, ss):
    heads = range(len(qs))
    r = lax.broadcasted_iota(jnp.int32, (CH, CH), 0)
    c = lax.broadcasted_iota(jnp.int32, (CH, CH), 1)
    causal = r >= c
    strict = r > c
    eye = jnp.where(r == c, 1.0, 0.0).astype(f32)
    gc_row = [gcs[h].T[:CH, :] for h in heads]
    decay = [jnp.where(causal, jnp.exp(jnp.where(causal, gcs[h][:, :CH] - gc_row[h], 0.0)), 0.0) for h in heads]
    kb = [ks[h] * bbs[h] for h in heads]
    egc = [jnp.exp(gcs[h]) for h in heads]
    kk = [_dot_b(kb[h], ks[h], _NT) for h in heads]
    qk = [_dot_b(qs[h], ks[h], _NT) for h in heads]
    m = [-jnp.where(strict, kk[h] * decay[h], 0.0) for h in heads]
    p = [eye + m[h] for h in heads]
    mp = m
    for _ in range(5):
        mp = [_dot_b(mp[h], mp[h], _NN) for h in heads]
        pm = [_dot_b(p[h], mp[h], _NN) for h in heads]
        p = [p[h] + pm[h] for h in heads]
    u = [_dot_b(p[h], vs[h] * bbs[h], _NN) for h in heads]
    w = [_dot_b(p[h], kb[h] * egc[h], _NN) for h in heads]
    intra = [jnp.where(causal, qk[h] * decay[h], 0.0) for h in heads]
    g_last = [gcs[h][CH - 1:CH, :] for h in heads]
    k_dec = [ks[h] * jnp.exp(g_last[h] - gcs[h]) for h in heads]
    ws = [_dot_b(w[h], ss[h], _NN) for h in heads]
    qs_s = [_dot_b(qs[h] * egc[h], ss[h], _NN) for h in heads]
    v_new = [u[h] - ws[h] for h in heads]
    iv = [_dot_b(intra[h], v_new[h], _NN) for h in heads]
    kv = [_dot_b(k_dec[h], v_new[h], _TN) for h in heads]
    o = tuple(qs_s[h] + iv[h] for h in heads)
    s_new = tuple(ss[h] * jnp.exp(g_last[h]) + kv[h] for h in heads)
    return o, s_new


def _head_cols():
    return [slice(h * HD, (h + 1) * HD) for h in range(NH)]


def _head_gates(gt):
    gcs = tuple(jnp.broadcast_to(gt[:, NH + h:NH + h + 1], (CH, HD)) for h in range(NH))
    bbs = tuple(jnp.broadcast_to(gt[:, h:h + 1], (CH, HD)) for h in range(NH))
    return gcs, bbs


def _gdn_scan_fwd(qn, kn, va, gates, *, tt):
    t = qn.shape[0]
    cpb = tt // CH

    def body(q_ref, k_ref, v_ref, gt_ref, o_ref, st_ref, s_scr):
        @pl.when(pl.program_id(0) == 0)
        def _():
            s_scr[...] = jnp.zeros_like(s_scr)

        def step(ci, carry):
            rows = pl.ds(pl.multiple_of(ci * CH, CH), CH)
            cols = _head_cols()
            ss = tuple(s_scr[h] for h in range(NH))
            for h in range(NH):
                st_ref[ci, h] = ss[h]
            gcs, bbs = _head_gates(gt_ref[rows, :])
            o, s_new = _chunk_fn(*(tuple(ref[rows, cl] for cl in cols) for ref in (q_ref, k_ref, v_ref)), gcs, bbs, ss)
            for h in range(NH):
                o_ref[rows, cols[h]] = o[h]
                s_scr[h] = s_new[h]
            return carry

        lax.fori_loop(0, cpb, step, 0)

    blk = pl.BlockSpec((tt, D), lambda i: (i, 0))
    return pl.pallas_call(
        body, name="gdn_scan_fwd", grid=(t // tt,),
        in_specs=[blk] * 3 + [pl.BlockSpec((tt, HD), lambda i: (i, 0))],
        out_specs=[blk, pl.BlockSpec((cpb, NH, HD, HD), lambda i: (i, 0, 0, 0))],
        out_shape=[jax.ShapeDtypeStruct((t, D), f32), jax.ShapeDtypeStruct((t // CH, NH, HD, HD), f32)],
        scratch_shapes=[pltpu.VMEM((NH, HD, HD), f32)],
        compiler_params=_cparams("arbitrary"),
    )(qn, kn, va, gates)


def _gdn_scan_bwd(qn, kn, va, gates, states, do, *, tt):
    t = qn.shape[0]
    nblk = t // tt
    cpb = tt // CH

    def body(q_ref, k_ref, v_ref, gt_ref, st_ref, do_ref, dq_ref, dk_ref, dv_ref, dgt_ref, ds_scr):
        @pl.when(pl.program_id(0) == 0)
        def _():
            ds_scr[...] = jnp.zeros_like(ds_scr)

        def step(j, carry):
            ci = cpb - 1 - j
            rows = pl.ds(pl.multiple_of(ci * CH, CH), CH)
            cols = _head_cols()
            gcs, bbs = _head_gates(gt_ref[rows, :])
            _, vjp = jax.vjp(_chunk_fn, *(tuple(ref[rows, cl] for cl in cols) for ref in (q_ref, k_ref, v_ref)),
                             gcs, bbs, tuple(st_ref[ci, h] for h in range(NH)))
            grads = vjp((tuple(do_ref[rows, cl] for cl in cols), tuple(ds_scr[h] for h in range(NH))))
            lane = lax.broadcasted_iota(jnp.int32, (CH, HD), 1)
            dgt = jnp.zeros((CH, HD), f32)
            for h in range(NH):
                for ref, g in zip((dq_ref, dk_ref, dv_ref), grads[:3]):
                    ref[rows, cols[h]] = g[h]
                dgt = dgt + jnp.where(lane == NH + h, jnp.sum(grads[3][h], axis=-1, keepdims=True), 0.0)
                dgt = dgt + jnp.where(lane == h, jnp.sum(grads[4][h], axis=-1, keepdims=True), 0.0)
                ds_scr[h] = grads[5][h]
            dgt_ref[rows, :] = dgt
            return carry

        lax.fori_loop(0, cpb, step, 0)

    blk = pl.BlockSpec((tt, D), lambda i: (nblk - 1 - i, 0))
    sblk = pl.BlockSpec((cpb, NH, HD, HD), lambda i: (nblk - 1 - i, 0, 0, 0))
    sds = jax.ShapeDtypeStruct((t, D), f32)
    gblk = pl.BlockSpec((tt, HD), lambda i: (nblk - 1 - i, 0))
    return pl.pallas_call(
        body, name="gdn_scan_bwd", grid=(nblk,),
        in_specs=[blk] * 3 + [gblk, sblk, blk],
        out_specs=[blk] * 3 + [gblk], out_shape=[sds] * 3 + [jax.ShapeDtypeStruct((t, HD), f32)],
        scratch_shapes=[pltpu.VMEM((NH, HD, HD), f32)],
        compiler_params=_cparams("arbitrary"),
    )(qn, kn, va, gates, states, do)


def _rms_heads(o):
    ons, rs = [], []
    for h in range(NH):
        a = o[:, h * HD:(h + 1) * HD]
        r = lax.rsqrt(jnp.mean(a * a, axis=-1, keepdims=True) + RMS_EPS)
        ons.append(a * r)
        rs.append(jnp.broadcast_to(r, a.shape))
    return jnp.concatenate(ons, axis=-1), jnp.concatenate(rs, axis=-1)


def _gdn_post_fwd(o, p_gz, ng_b, *, tt):
    t = o.shape[0]

    def body(o_ref, gz_ref, ng_ref, og_ref):
        on, _ = _rms_heads(o_ref[...])
        z, _ = _silu_and_grad(gz_ref[...])
        og_ref[...] = (on * ng_ref[...] * z).astype(bf16)

    blk = pl.BlockSpec((tt, D), lambda i: (i, 0))
    return pl.pallas_call(
        body, name="gdn_post_fwd", grid=(t // tt,),
        in_specs=[blk, blk, pl.BlockSpec((1, D), lambda i: (0, 0))],
        out_specs=blk, out_shape=jax.ShapeDtypeStruct((t, D), bf16),
        compiler_params=_cparams("parallel"),
    )(o, p_gz, ng_b)


def _gdn_post_bwd(o, p_gz, ng_b, dog, *, tt):
    t = o.shape[0]

    def body(o_ref, gz_ref, ng_ref, dog_ref, do_ref, dgz_ref, dng_ref):
        @pl.when(pl.program_id(0) == 0)
        def _():
            dng_ref[...] = jnp.zeros_like(dng_ref)

        on, r = _rms_heads(o_ref[...])
        z, dz = _silu_and_grad(gz_ref[...])
        dog_v = dog_ref[...]
        ng = ng_ref[...]
        dgz_ref[...] = (dog_v * on * ng * dz).astype(bf16)
        dy = dog_v * z
        dng_all = jnp.sum(dy * on, axis=0, keepdims=True)
        dng = dng_all[:, 0:HD]
        for h in range(1, NH):
            dng = dng + dng_all[:, h * HD:(h + 1) * HD]
        dng_ref[0:1, :] += dng
        don = dy * ng
        prod = don * on
        parts = []
        for h in range(NH):
            sl = slice(h * HD, (h + 1) * HD)
            parts.append(don[:, sl] - on[:, sl] * jnp.mean(prod[:, sl], axis=-1, keepdims=True))
        do_ref[...] = r * jnp.concatenate(parts, axis=-1)

    blk = pl.BlockSpec((tt, D), lambda i: (i, 0))
    return pl.pallas_call(
        body, name="gdn_post_bwd", grid=(t // tt,),
        in_specs=[blk, blk, pl.BlockSpec((1, D), lambda i: (0, 0)), blk],
        out_specs=[blk, blk, pl.BlockSpec((8, HD), lambda i: (0, 0))],
        out_shape=[jax.ShapeDtypeStruct((t, D), f32), jax.ShapeDtypeStruct((t, D), bf16),
                   jax.ShapeDtypeStruct((8, HD), f32)],
        compiler_params=_cparams("arbitrary"),
    )(o, p_gz, ng_b, dog)


def _merge(x, y_conf, y_gdn, p_gates, target, w_o, ln_g, ln_b, *, tt):
    t = x.shape[0]

    def body(x_ref, yc_ref, yg_ref, gc_ref, gg_ref, tg_ref, w_ref, g_ref, b_ref,
             loss_ref, dxd_ref, dyc_ref, dyg_ref, dpg_ref, h_ref, dz_ref, dvec_ref):
        @pl.when(pl.program_id(0) == 0)
        def _():
            loss_ref[...] = jnp.zeros_like(loss_ref)
            dvec_ref[...] = jnp.zeros_like(dvec_ref)

        sc = _sigmoid(gc_ref[...])
        sg = _sigmoid(gg_ref[...])
        yc = yc_ref[...]
        yg = yg_ref[...]
        h = (sc * yc + sg * yg).astype(bf16)
        h_ref[...] = h
        z = DN_ALPHA * x_ref[...] + jnp.dot(h, w_ref[...], preferred_element_type=f32)
        mu = jnp.mean(z, axis=-1, keepdims=True)
        cen = z - mu
        rstd = lax.rsqrt(jnp.mean(cen * cen, axis=-1, keepdims=True) + LN_EPS)
        xhat = cen * rstd
        err = xhat * g_ref[...] + b_ref[...] - tg_ref[...]
        loss_ref[...] += 0.5 / D * jnp.sum(err * err)
        dy = err * (1.0 / D)
        dvec_ref[0:1, :] += jnp.sum(dy * xhat, axis=0, keepdims=True)
        dvec_ref[1:2, :] += jnp.sum(dy, axis=0, keepdims=True)
        dxhat = dy * g_ref[...]
        dz = rstd * (dxhat - jnp.mean(dxhat, axis=-1, keepdims=True)
                     - xhat * jnp.mean(dxhat * xhat, axis=-1, keepdims=True))
        dxd_ref[...] = DN_ALPHA * dz
        dz_b = dz.astype(bf16)
        dz_ref[...] = dz_b
        dh = lax.dot_general(dz_b, w_ref[...], (_NT, ((), ())), preferred_element_type=f32)
        dyc_ref[...] = (dh * sc).astype(bf16)
        dyg_ref[...] = (dh * sg).astype(bf16)
        dpg_ref[:, 0:D] = (dh * yc * sc * (1.0 - sc)).astype(bf16)
        dpg_ref[:, D:] = (dh * yg * sg * (1.0 - sg)).astype(bf16)

    blk = pl.BlockSpec((tt, D), lambda i: (i, 0))
    wblk = pl.BlockSpec((D, D), lambda i: (0, 0))
    vec = pl.BlockSpec((1, D), lambda i: (0, 0))
    return pl.pallas_call(
        body, name="merge_norm_loss", grid=(t // tt,),
        in_specs=[blk, blk, blk, pl.BlockSpec((tt, D), lambda i: (i, 0)), pl.BlockSpec((tt, D), lambda i: (i, 1)),
                  blk, wblk, vec, vec],
        out_specs=[pl.BlockSpec((8, HD), lambda i: (0, 0)), blk, blk, blk,
                   pl.BlockSpec((tt, 2 * D), lambda i: (i, 0)), blk, blk, pl.BlockSpec((8, D), lambda i: (0, 0))],
        out_shape=[jax.ShapeDtypeStruct((8, HD), f32), jax.ShapeDtypeStruct((t, D), f32),
                   jax.ShapeDtypeStruct((t, D), bf16), jax.ShapeDtypeStruct((t, D), bf16),
                   jax.ShapeDtypeStruct((t, 2 * D), bf16), jax.ShapeDtypeStruct((t, D), bf16),
                   jax.ShapeDtypeStruct((t, D), bf16), jax.ShapeDtypeStruct((8, D), f32)],
        compiler_params=_cparams("arbitrary"),
    )(x, y_conf, y_gdn, p_gates, p_gates, target, w_o, ln_g, ln_b)


def _place():
    return lax.axis_index("x"), lax.axis_index("y"), lax.axis_index("c")


def _any_specs(n):
    return [pl.BlockSpec(memory_space=pl.ANY)] * n


def _sibling_merge(arrs, name, take_other_half=False):
    k = len(arrs)

    def body(*refs):
        a_refs, o_refs = refs[:k], refs[k:2 * k]
        send_sems, recv_sems = refs[2 * k:]
        x, y, c = _place()
        sends = []
        for i in range(k):
            src = a_refs[i]
            if take_other_half:
                n = a_refs[i].shape[-2] // 2
                lead = (slice(None),) * (len(a_refs[i].shape) - 2)
                src = a_refs[i].at[lead + (pl.ds((1 - c) * n, n), slice(None))]
            cp = pltpu.make_async_remote_copy(src_ref=src, dst_ref=o_refs[i], send_sem=send_sems.at[i],
                                              recv_sem=recv_sems.at[i], device_id=(x, y, 1 - c), device_id_type=MESH)
            cp.start()
            sends.append(cp)
        for cp in sends:
            cp.wait()

    def out_sds(a):
        rows = a.shape[-2] // 2 if take_other_half else a.shape[-2]
        return jax.ShapeDtypeStruct(a.shape[:-2] + (rows, a.shape[-1]), a.dtype)

    return pl.pallas_call(
        body, name=name, in_specs=_any_specs(k), out_specs=_any_specs(k),
        out_shape=[out_sds(a) for a in arrs],
        scratch_shapes=[pltpu.SemaphoreType.DMA((k,)), pltpu.SemaphoreType.DMA((k,))],
    )(*arrs)


def _join_halves(mine, other):
    c = lax.axis_index("c")
    return jnp.concatenate([jnp.where(c == 0, mine, other), jnp.where(c == 0, other, mine)], axis=-2)


def _chip_exchange(arrs, name, scatter):
    k = len(arrs)

    def body(*refs):
        a_refs, o_refs = refs[:k], refs[k:2 * k]
        send_sems, recv_sems, local_sems = refs[2 * k:]
        x, y, c = _place()
        me = 2 * x + y
        peers = [(1 - x, y), (x, 1 - y), (1 - x, 1 - y)]

        def src(i, j):
            return a_refs[i].at[j] if scatter else a_refs[i]

        def copy(i, n, send_j, slot):
            px, py = peers[n]
            return pltpu.make_async_remote_copy(
                src_ref=src(i, send_j), dst_ref=o_refs[i].at[slot], send_sem=send_sems.at[3 * i + n],
                recv_sem=recv_sems.at[3 * i + n], device_id=(px, py, c), device_id_type=MESH)

        owns = [pltpu.make_async_copy(src(i, me), o_refs[i].at[me], local_sems.at[i]) for i in range(k)]
        for own in owns:
            own.start()
        sends = [copy(i, n, 2 * peers[n][0] + peers[n][1], me) for n in range(3) for i in range(k)]
        for cp in sends:
            cp.start()
        for n in range(3):
            for i in range(k):
                copy(i, n, me, 2 * peers[n][0] + peers[n][1]).wait_recv()
        for cp in sends:
            cp.wait_send()
        for own in owns:
            own.wait()

    def out_sds(a):
        return jax.ShapeDtypeStruct((N_CHIPS,) + tuple(a.shape[1:] if scatter else a.shape), a.dtype)

    return pl.pallas_call(
        body, name=name, in_specs=_any_specs(k), out_specs=_any_specs(k),
        out_shape=[out_sds(a) for a in arrs],
        scratch_shapes=[pltpu.SemaphoreType.DMA((3 * k,)), pltpu.SemaphoreType.DMA((3 * k,)),
                        pltpu.SemaphoreType.DMA((k,))],
    )(*arrs)


def _pair_sum(g_all, got, c_arr, name, out_dtype):
    n, w = got.shape[1:]
    tile = n // 4
    n_tiles = n // tile

    def body(c_ref, a_ref, b_ref, o_ref):
        o_ref[...] = (a_ref[...] + b_ref[...]).astype(out_dtype)

    return pl.pallas_call(
        body, name=name,
        grid_spec=pltpu.PrefetchScalarGridSpec(
            num_scalar_prefetch=1, grid=(N_CHIPS, n_tiles),
            in_specs=[pl.BlockSpec((1, tile, w), lambda j, i, c_ref: (j, c_ref[0] * n_tiles + i, 0)),
                      pl.BlockSpec((1, tile, w), lambda j, i, c_ref: (j, i, 0))],
            out_specs=pl.BlockSpec((1, tile, w), lambda j, i, c_ref: (j, i, 0))),
        out_shape=jax.ShapeDtypeStruct(got.shape, out_dtype),
        compiler_params=_cparams("parallel", "parallel"),
    )(c_arr, g_all, got)


def _sum_slots(a, name):
    n, w = a.shape[1:]
    tile = n // 4

    def body(a_ref, o_ref):
        o_ref[...] = ((a_ref[0].astype(f32) + a_ref[1].astype(f32)) + a_ref[2].astype(f32)) + a_ref[3].astype(f32)

    return pl.pallas_call(
        body, name=name, grid=(n // tile,),
        in_specs=[pl.BlockSpec((N_CHIPS, tile, w), lambda i: (0, i, 0))],
        out_specs=pl.BlockSpec((tile, w), lambda i: (i, 0)),
        out_shape=jax.ShapeDtypeStruct((n, w), f32),
        compiler_params=_cparams("parallel"),
    )(a)


def _adamw(w, g, m, v, name):
    rows, width = w.shape
    tile = rows // 8
    c1 = 1.0 / (1.0 - ADAM_B1 ** ADAM_STEP)
    c2 = 1.0 / (1.0 - ADAM_B2 ** ADAM_STEP)

    def body(w_ref, g_ref, m_ref, v_ref, d_ref, mo_ref, vo_ref):
        g_v = g_ref[...]
        m_new = ADAM_B1 * m_ref[...] + (1.0 - ADAM_B1) * g_v
        v_new = ADAM_B2 * v_ref[...] + (1.0 - ADAM_B2) * (g_v * g_v)
        mo_ref[...] = m_new
        vo_ref[...] = v_new
        d_ref[...] = -ADAM_LR * ((m_new * c1) / (jnp.sqrt(v_new * c2) + ADAM_EPS) + ADAM_WD * w_ref[...])

    blk = pl.BlockSpec((tile, width), lambda i: (i, 0))
    sds = jax.ShapeDtypeStruct((rows, width), f32)
    return pl.pallas_call(
        body, name=name, grid=(rows // tile,),
        in_specs=[blk] * 4, out_specs=[blk] * 3, out_shape=[sds] * 3,
        compiler_params=_cparams("parallel"),
    )(w, g, m, v)


R_DW = 3 * SQ_BLK
R_CW = R_DW + 8
R_VEC = R_CW + 8
R_SMALL = R_VEC + 8
REST_ROWS = 896


def _pack_small(conf_dw_w, gdn_conv_w, vecs, a_log, dt_bias, norm_g):
    dw = jnp.pad(conf_dw_w.reshape(-1), (0, 8 * D - KC * SQ_BLK)).reshape(8, D)
    cw = jnp.pad(gdn_conv_w.reshape(-1), (0, 5 * D)).reshape(8, D)
    vec = jnp.pad(jnp.stack(vecs), ((0, 3), (0, 0)))
    small = jnp.pad(jnp.concatenate([a_log, dt_bias, norm_g]), (0, D - 2 * NH - HD)).reshape(1, D)
    return jnp.pad(jnp.concatenate([dw, cw, vec, small], axis=0), ((0, REST_ROWS - R_SMALL - 1), (0, 0)))


def _pack_rest(conf_w_out, gdn_w_out, w_o, small):
    return jnp.concatenate([conf_w_out, gdn_w_out, w_o, small], axis=0)


def _unpack_rest(p):
    conf_dw_w = p[R_DW:R_DW + 8].reshape(-1)[:KC * SQ_BLK].reshape(KC, SQ_BLK)
    gdn_conv_w = p[R_CW:R_CW + 3].reshape(KG, 3 * SQ_BLK)
    small = p[R_SMALL]
    return dict(conf_w_out=p[0:SQ_BLK], gdn_w_out=p[SQ_BLK:2 * SQ_BLK], w_o=p[2 * SQ_BLK:R_DW],
                conf_dw_w=conf_dw_w, gdn_conv_w=gdn_conv_w, conf_dw_b=p[R_VEC], conf_ln_g=p[R_VEC + 1],
                conf_ln_b=p[R_VEC + 2], post_ln_g=p[R_VEC + 3], post_ln_b=p[R_VEC + 4],
                gdn_A_log=small[0:NH], gdn_dt_bias=small[NH:2 * NH], gdn_norm_g=small[2 * NH:2 * NH + HD])


_WEIGHT_ORDER = ("w_in", "conf_dw_w", "conf_dw_b", "conf_ln_g", "conf_ln_b", "conf_w_out", "gdn_conv_w",
                 "gdn_A_log", "gdn_dt_bias", "gdn_norm_g", "gdn_w_out", "w_o", "post_ln_g", "post_ln_b")


def _gather_weights(w_in, conf_w_out, gdn_w_out, w_o, conf_dw_w, gdn_conv_w):
    c = lax.axis_index("c")
    sq = jnp.concatenate([conf_w_out, gdn_w_out, w_o], axis=0).astype(bf16)
    w_half = lax.dynamic_slice_in_dim(w_in.astype(bf16), c * (D // 2), D // 2, axis=0)
    sq_half = lax.dynamic_slice_in_dim(sq, c * (sq.shape[0] // 2), sq.shape[0] // 2, axis=0)
    small = jnp.concatenate([jnp.pad(conf_dw_w.reshape(-1), (0, 8 * D - KC * SQ_BLK)).reshape(8, D),
                             jnp.pad(gdn_conv_w.reshape(-1), (0, 5 * D)).reshape(8, D)], axis=0)
    got_w, got_sq, small_all = _chip_exchange([w_half, sq_half, small], "weight_gather_chips", scatter=False)
    oth_w, oth_sq = _sibling_merge([got_w, got_sq], "weight_gather_sibling")
    w4 = _join_halves(got_w, oth_w)
    sq4 = _join_halves(got_sq, oth_sq)
    sq_full = [sq4[:, n * SQ_BLK:(n + 1) * SQ_BLK].reshape(D, D) for n in range(3)]
    dw_full = small_all[:, 0:8].reshape(N_CHIPS, 8 * D)[:, :KC * SQ_BLK].reshape(N_CHIPS, KC, SQ_BLK)
    dw_full = dw_full.transpose(1, 0, 2).reshape(KC, D)
    cw_full = small_all[:, 8:11].reshape(N_CHIPS, KG, 3 * SQ_BLK).transpose(1, 0, 2).reshape(KG, 3 * D)
    return w4, sq_full[0], sq_full[1], sq_full[2], dw_full, cw_full


def _w_in_cols(w4, lo, hi):
    parts = []
    for j in range(N_CHIPS):
        a, b = max(lo, j * W_IN_BLK), min(hi, (j + 1) * W_IN_BLK)
        if a < b:
            parts.append(w4[j, :, a - j * W_IN_BLK:b - j * W_IN_BLK])
    return parts[0] if len(parts) == 1 else jnp.concatenate(parts, axis=1)


def _w_in_by_chip(pieces):
    chips = []
    for j in range(N_CHIPS):
        lo, hi = j * W_IN_BLK, (j + 1) * W_IN_BLK
        parts = []
        for start, arr in pieces:
            a, b = max(lo, start), min(hi, start + arr.shape[1])
            if a < b:
                parts.append(arr[:, a - start:b - start])
        chips.append(jnp.concatenate(parts, axis=1))
    return jnp.stack(chips)


def _reduce_scatter(g_w, g_rest):
    c_arr = lax.axis_index("c").astype(jnp.int32).reshape(1)
    got_w, got_r = _sibling_merge([g_w, g_rest], "grad_sibling_halves", take_other_half=True)
    pair_w = _pair_sum(g_w, got_w, c_arr, "grad_pair_sum_w_in", bf16)
    pair_r = _pair_sum(g_rest, got_r, c_arr, "grad_pair_sum_rest", f32)
    all_w, all_r = _chip_exchange([pair_w, pair_r], "grad_chip_scatter", scatter=True)
    tot_w, tot_r = _sum_slots(all_w, "grad_chip_sum_w_in"), _sum_slots(all_r, "grad_chip_sum_rest")
    oth_w, oth_r = _sibling_merge([tot_w, tot_r], "grad_sibling_result")
    return _join_halves(tot_w, oth_w), _join_halves(tot_r, oth_r)


def _local_step(x2, tgt, w4, wc_out, wg_out, wo_full, dw_full, cw_full, conf_dw_b, conf_ln_g, conf_ln_b,
                gdn_A_log, gdn_dt_bias, gdn_norm_g, post_ln_g, post_ln_b):
    t = x2.shape[0]
    tt = min(TOKEN_TILE, t)
    tm = min(512, t)

    w_conv, w_qkv, w_gz = _w_in_cols(w4, 0, 3 * D), _w_in_cols(w4, 3 * D, 6 * D), _w_in_cols(w4, 6 * D, 7 * D)
    w_gates = _w_in_cols(w4, 7 * D + 2 * NH, W_IN_COLS)
    dw_pad = jnp.pad(dw_full, ((0, HALO_C - KC), (0, 0)))
    cw_pad = jnp.pad(cw_full, ((0, 8 - KG), (0, 0)))
    row = lambda v: v.reshape(1, D)
    alog_v = jnp.pad(gdn_A_log, (NH, HD - 2 * NH)).reshape(1, HD)
    dt_v = jnp.pad(gdn_dt_bias, (NH, HD - 2 * NH)).reshape(1, HD)
    ng_b = row(jnp.tile(gdn_norm_g, NH))
    w_ba = jnp.pad(_w_in_cols(w4, 7 * D, 7 * D + 2 * NH), ((0, 0), (0, HD - 2 * NH)))

    x_b = x2.astype(bf16)

    p_conv = _mm_multi([x_b], [w_conv], out_dtype=f32, tm=tm, tn=1024, name="proj_conv")
    p_qkv = _mm_multi([x_b], [w_qkv], out_dtype=f32, tm=tm, tn=1024, name="proj_qkv")
    p_gz = _mm_multi([x_b], [w_gz], out_dtype=f32, tm=tm, tn=1024, name="proj_gz")
    p_gates = _mm_multi([x_b], [w_gates], out_dtype=f32, tm=tm, tn=1024, name="proj_gates")
    p_ba = _mm_multi([x_b], [w_ba], out_dtype=f32, tm=tm, tn=HD, name="proj_ba")

    u = _conv_fwd(p_conv, dw_pad, row(conf_dw_b), row(conf_ln_g), row(conf_ln_b), tt=tt)
    y_conf = _mm_multi([u], [wc_out], out_dtype=f32, tm=tm, tn=1024, name="conf_out")

    qn, kn, va, gates = _gdn_pre_fwd(p_qkv, p_ba, cw_pad, alog_v, dt_v, tt=tt)
    o, states = _gdn_scan_fwd(qn, kn, va, gates, tt=tt)
    og = _gdn_post_fwd(o, p_gz, ng_b, tt=tt)
    y_gdn = _mm_multi([og], [wg_out], out_dtype=f32, tm=tm, tn=1024, name="gdn_out")

    loss_blk, dxd, dyc, dyg, dp_gates, h, dz, dpost = _merge(
        x2, y_conf, y_gdn, p_gates, tgt, wo_full, row(post_ln_g), row(post_ln_b), tt=tt)

    d_wo = _mm_kloop(h, dz, tm=D, tn=1024, tk=min(512, t), name="grad_w_o")
    du = _mm_multi([dyc], [wc_out], out_dtype=f32, tm=tm, tn=1024, name="conf_out_bwd", rhs_t=True)
    d_wc = _mm_kloop(u, dyc, tm=D, tn=1024, tk=min(512, t), name="grad_conf_w_out")
    dog = _mm_multi([dyg], [wg_out], out_dtype=f32, tm=tm, tn=1024, name="gdn_out_bwd", rhs_t=True)
    d_wg = _mm_kloop(og, dyg, tm=D, tn=1024, tk=min(512, t), name="grad_gdn_w_out")

    dp_conv, d_dww, dconv_vec = _conv_bwd(p_conv, du, dw_pad, row(conf_dw_b), row(conf_ln_g), row(conf_ln_b), tt=tt)

    do, dp_gz, dng = _gdn_post_bwd(o, p_gz, ng_b, dog, tt=tt)
    dqn, dkn, dva, dgates = _gdn_scan_bwd(qn, kn, va, gates, states, do, tt=tt)
    dp_qkv, dp_ba, d_cw, d_ad = _gdn_pre_bwd(p_qkv, p_ba, cw_pad, alog_v, dt_v, dqn, dkn, dva, dgates, tt=tt)
    dp_ba_b = dp_ba.astype(bf16)

    grad_x = _mm_multi(
        [dp_conv, dp_qkv, dp_gz, dp_gates, dp_ba_b],
        [w_conv, w_qkv, w_gz, w_gates, w_ba], dxd,
        out_dtype=f32, tm=min(256, t), tn=512, name="grad_x", rhs_t=True)

    tk = min(512, t)
    d_w_conv = _mm_kloop(x_b, dp_conv, tm=D, tn=1024, tk=tk, name="grad_w_in_conv")
    d_w_qkv = _mm_kloop(x_b, dp_qkv, tm=D, tn=1024, tk=tk, name="grad_w_in_qkv")
    d_w_gz = _mm_kloop(x_b, dp_gz, tm=D, tn=1024, tk=tk, name="grad_w_in_gz")
    d_w_gates = _mm_kloop(x_b, dp_gates, tm=D, tn=1024, tk=tk, name="grad_w_in_gates")
    d_w_ba = _mm_kloop(x_b, dp_ba_b, tm=D, tn=HD, tk=tk, name="grad_w_in_ba")
    d_w_in = _w_in_by_chip([(0, d_w_conv), (3 * D, d_w_qkv), (6 * D, d_w_gz), (7 * D, d_w_ba[:, :2 * NH]),
                            (7 * D + 2 * NH, d_w_gates)])

    return (loss_blk[0, 0], grad_x, d_w_in, d_wc, d_wg, d_wo, d_dww, d_cw, dconv_vec, dpost, d_ad, dng)


def kernel(x, w_in, conf_dw_w, conf_dw_b, conf_ln_g, conf_ln_b, conf_w_out, gdn_conv_w, gdn_A_log, gdn_dt_bias, gdn_norm_g, gdn_w_out, w_o, post_ln_g, post_ln_b, loss_target, m_w_in, m_conf_dw_w, m_conf_dw_b, m_conf_ln_g, m_conf_ln_b, m_conf_w_out, m_gdn_conv_w, m_gdn_A_log, m_gdn_dt_bias, m_gdn_norm_g, m_gdn_w_out, m_w_o, m_post_ln_g, m_post_ln_b, v_w_in, v_conf_dw_w, v_conf_dw_b, v_conf_ln_g, v_conf_ln_b, v_conf_w_out, v_gdn_conv_w, v_gdn_A_log, v_gdn_dt_bias, v_gdn_norm_g, v_gdn_w_out, v_w_o, v_post_ln_g, v_post_ln_b):
    x2 = x.reshape(x.shape[-2], D)
    tgt = loss_target.reshape(x2.shape)
    w4, wc_out, wg_out, wo_full, dw_full, cw_full = _gather_weights(
        w_in, conf_w_out, gdn_w_out, w_o, conf_dw_w, gdn_conv_w)
    (loss_part, grad_x, d_w_in, d_wc, d_wg, d_wo, d_dww, d_cw, dconv_vec, dpost, d_ad, dng) = _local_step(
        x2, tgt, w4, wc_out, wg_out, wo_full, dw_full, cw_full, conf_dw_b, conf_ln_g, conf_ln_b,
        gdn_A_log, gdn_dt_bias, gdn_norm_g, post_ln_g, post_ln_b)
    loss = lax.psum(loss_part, ("x", "y", "c"))

    dww_c = d_dww[:KC].reshape(KC, N_CHIPS, SQ_BLK)
    dcw_c = d_cw[:KG].reshape(KG, N_CHIPS, 3 * SQ_BLK)
    vecs = [dconv_vec[0], dconv_vec[1], dconv_vec[2], dpost[0], dpost[1]]
    g_rest = jnp.stack([
        _pack_rest(d_wc[j * SQ_BLK:(j + 1) * SQ_BLK], d_wg[j * SQ_BLK:(j + 1) * SQ_BLK], d_wo[j * SQ_BLK:(j + 1) * SQ_BLK],
                   _pack_small(dww_c[:, j], dcw_c[:, j], vecs, d_ad[0, NH:2 * NH], d_ad[1, NH:2 * NH], dng[0]))
        for j in range(N_CHIPS)])
    g_w_in, g_rest = _reduce_scatter(d_w_in, g_rest)

    def rest_of(w_c, w_g, w_oo, dw, cw, b1, g1, b2, g2, b3, a_log, dt_bias, norm_g):
        return _pack_rest(w_c, w_g, w_oo, _pack_small(dw, cw, [b1, g1, b2, g2, b3], a_log, dt_bias, norm_g))

    w_r = rest_of(conf_w_out, gdn_w_out, w_o, conf_dw_w, gdn_conv_w, conf_dw_b, conf_ln_g, conf_ln_b,
                  post_ln_g, post_ln_b, gdn_A_log, gdn_dt_bias, gdn_norm_g)
    m_r = rest_of(m_conf_w_out, m_gdn_w_out, m_w_o, m_conf_dw_w, m_gdn_conv_w, m_conf_dw_b, m_conf_ln_g, m_conf_ln_b,
                  m_post_ln_g, m_post_ln_b, m_gdn_A_log, m_gdn_dt_bias, m_gdn_norm_g)
    v_r = rest_of(v_conf_w_out, v_gdn_w_out, v_w_o, v_conf_dw_w, v_gdn_conv_w, v_conf_dw_b, v_conf_ln_g, v_conf_ln_b,
                  v_post_ln_g, v_post_ln_b, v_gdn_A_log, v_gdn_dt_bias, v_gdn_norm_g)
    upd_w_in = _adamw(w_in, g_w_in, m_w_in, v_w_in, "adamw_w_in")
    upd_rest = _adamw(w_r, g_rest, m_r, v_r, "adamw_rest")

    out = [loss, grad_x.reshape(x.shape)]
    for big, rest in zip((g_w_in,) + tuple(upd_w_in), (g_rest,) + tuple(upd_rest)):
        d = dict(_unpack_rest(rest), w_in=big)
        out += [d[n] for n in _WEIGHT_ORDER]
    return tuple(out)
```

```python
import functools

import jax
import jax.numpy as jnp
from jax import lax
from jax.experimental import pallas as pl
from jax.experimental.pallas import tpu as pltpu

f32 = jnp.float32
bf16 = jnp.bfloat16
HI = lax.Precision.HIGHEST
MESH = pl.DeviceIdType.MESH

D = 1024
NH = 8
HD = 128
CH = 64
KC = 31
KG = 4
HALO_C = 32
HALO_G = 8
LANE = 128
STRIP = 32
N_SHIFT = 7
LN_EPS = 1e-5
RMS_EPS = 1e-6
L2_EPS = 1e-6
DN_ALPHA = 2.0 ** 0.25
N_CHIPS = 4
W_IN_COLS = 9232
W_IN_BLK = W_IN_COLS // N_CHIPS
SQ_BLK = D // N_CHIPS
VMEM_LIMIT = 52 * 1024 * 1024
TOKEN_TILE = 256

ADAM_LR = 0.001
ADAM_B1 = 0.9
ADAM_B2 = 0.999
ADAM_EPS = 1e-08
ADAM_WD = 0.01
ADAM_STEP = 10


def _sigmoid(x):
    return 1.0 / (1.0 + jnp.exp(-x))


def _silu_and_grad(x):
    s = _sigmoid(x)
    return x * s, s * (1.0 + x * (1.0 - s))


_NN = ((1,), (0,))
_NT = ((1,), (1,))
_TN = ((0,), (0,))


def _cparams(*sem):
    return pltpu.CompilerParams(dimension_semantics=sem, vmem_limit_bytes=VMEM_LIMIT)


def _mm_multi(a_list, b_list, addend=None, *, out_dtype, tm, tn, name, rhs_t=False):
    n_pairs = len(a_list)
    m = a_list[0].shape[0]
    n = b_list[0].shape[0 if rhs_t else 1]
    has_add = addend is not None
    dims = (_NT if rhs_t else _NN, ((), ()))

    def body(*refs):
        a_refs = refs[:n_pairs]
        b_refs = refs[n_pairs:2 * n_pairs]
        o_ref = refs[-1]
        acc = None
        for a_ref, b_ref in zip(a_refs, b_refs):
            p = lax.dot_general(a_ref[...].astype(bf16), b_ref[...].astype(bf16), dims, preferred_element_type=f32)
            acc = p if acc is None else acc + p
        if has_add:
            acc = acc + refs[2 * n_pairs][...]
        o_ref[...] = acc.astype(out_dtype)

    in_specs = [pl.BlockSpec((tm, a.shape[1]), lambda j, i: (i, 0)) for a in a_list]
    if rhs_t:
        in_specs += [pl.BlockSpec((tn, b.shape[1]), lambda j, i: (j, 0)) for b in b_list]
    else:
        in_specs += [pl.BlockSpec((b.shape[0], tn), lambda j, i: (0, j)) for b in b_list]
    args = list(a_list) + list(b_list)
    if has_add:
        in_specs.append(pl.BlockSpec((tm, tn), lambda j, i: (i, j)))
        args.append(addend)
    return pl.pallas_call(
        body, name=name, grid=(n // tn, m // tm),
        in_specs=in_specs, out_specs=pl.BlockSpec((tm, tn), lambda j, i: (i, j)),
        out_shape=jax.ShapeDtypeStruct((m, n), out_dtype),
        compiler_params=_cparams("parallel", "parallel"),
    )(*args)


def _mm_kloop(a, b, *, tm, tn, tk, name):
    k, m = a.shape
    n = b.shape[1]
    nk = k // tk

    def body(a_ref, b_ref, o_ref):
        @pl.when(pl.program_id(2) == 0)
        def _():
            o_ref[...] = jnp.zeros_like(o_ref)
        o_ref[...] += lax.dot_general(a_ref[...].astype(bf16), b_ref[...].astype(bf16), (_TN, ((), ())),
                                      preferred_element_type=f32)

    return pl.pallas_call(
        body, name=name, grid=(n // tn, m // tm, nk),
        in_specs=[pl.BlockSpec((tk, tm), lambda j, i, kk: (kk, i)), pl.BlockSpec((tk, tn), lambda j, i, kk: (kk, j))],
        out_specs=pl.BlockSpec((tm, tn), lambda j, i, kk: (i, j)),
        out_shape=jax.ShapeDtypeStruct((m, n), f32),
        compiler_params=_cparams("parallel", "parallel", "arbitrary"),
    )(a, b)


def _shift_copies(src_ref, sh_ref, n, shifts=tuple(range(1, 8))):
    for i, b in enumerate(shifts):
        sh_ref[i, 0:n, :] = src_ref[pl.ds(b, n), :]


def _by_residue(offs):
    groups = {}
    for k, off in enumerate(offs):
        groups.setdefault(off % 8, []).append((k, off // 8))
    return groups


def _slab(src_ref, sh_ref, shifts, b, r0, n, lanes):
    ref = src_ref if b == 0 else sh_ref.at[shifts.index(b)]
    return ref[r0:r0 + n, lanes]


def _tap_conv(out_ref, n_rows, src_ref, sh_ref, w_ref, offs, bias_ref=None, shifts=tuple(range(1, 8))):
    groups = _by_residue(offs)
    for j in range(D // LANE):
        lanes = slice(j * LANE, (j + 1) * LANE)
        wv = [w_ref[k:k + 1, lanes] for k in range(len(offs))]
        for r0 in range(0, n_rows, STRIP):
            n = min(STRIP, n_rows - r0)
            accs = [jnp.zeros((n, LANE), f32) if bias_ref is None else jnp.broadcast_to(bias_ref[0:1, lanes], (n, LANE)),
                    jnp.zeros((n, LANE), f32)]
            m = 0
            for b, taps in groups.items():
                a_lo = min(a for _, a in taps)
                a_hi = max(a for _, a in taps)
                wide = _slab(src_ref, sh_ref, shifts, b, r0 + 8 * a_lo, 8 * (a_hi - a_lo) + n, lanes)
                for k, a in taps:
                    accs[m % 2] = accs[m % 2] + wv[k] * wide[8 * (a - a_lo):8 * (a - a_lo) + n]
                    m += 1
            out_ref[r0:r0 + n, lanes] = accs[0] + accs[1]


def _tap_corr(dw_ref, n_rows, lhs_ref, src_ref, sh_ref, offs, shifts=tuple(range(1, 8))):
    groups = _by_residue(offs)
    for j in range(D // LANE):
        lanes = slice(j * LANE, (j + 1) * LANE)
        accs = [jnp.zeros((8, LANE), f32) for _ in offs]
        for r0 in range(0, n_rows, STRIP):
            n = min(STRIP, n_rows - r0)
            d = lhs_ref[r0:r0 + n, lanes]
            for b, taps in groups.items():
                a_lo = min(a for _, a in taps)
                a_hi = max(a for _, a in taps)
                wide = _slab(src_ref, sh_ref, shifts, b, r0 + 8 * a_lo, 8 * (a_hi - a_lo) + n, lanes)
                for k, a in taps:
                    prod = d * wide[8 * (a - a_lo):8 * (a - a_lo) + n]
                    part = prod[0:8]
                    for q in range(1, n // 8):
                        part = part + prod[8 * q:8 * q + 8]
                    accs[k] = accs[k] + part
        for k in range(len(offs)):
            dw_ref[k:k + 1, lanes] += jnp.sum(accs[k], axis=0, keepdims=True)


_FWD_OFFS = [HALO_C - (KC - 1) + k for k in range(KC)]
_BWD_OFFS = [KC - 1 - k for k in range(KC)]


def _norm_act(a1, cz, g_ref, bb_ref):
    mu = jnp.mean(a1, axis=-1, keepdims=True)
    cen = a1 - mu
    var = jnp.mean(cen * cen, axis=-1, keepdims=True)
    rstd = lax.rsqrt(var + LN_EPS)
    xhat = cen * rstd
    ln = xhat * g_ref[...] + bb_ref[...]
    s, ds = _silu_and_grad(ln)
    zc, dzc = _silu_and_grad(cz)
    return xhat, rstd, s, ds, zc, dzc


def _conv_fwd(p_conv, dw_w, dw_b, ln_g, ln_b, *, tt):
    t = p_conv.shape[0]
    hb = tt // HALO_C

    def body(cv_ref, cg_ref, cz_ref, cvh_ref, cgh_ref, w_ref, b_ref, g_ref, bb_ref, u_ref, ext_ref, sh_ref, a1_ref):
        first = pl.program_id(0) == 0
        halo = cvh_ref[...] * _sigmoid(cgh_ref[...])
        ext_ref[0:HALO_C, :] = jnp.where(first, 0.0, halo)
        ext_ref[HALO_C:, :] = cv_ref[...] * _sigmoid(cg_ref[...])
        _shift_copies(ext_ref, sh_ref, tt + HALO_C - 8)
        _tap_conv(a1_ref, tt, ext_ref, sh_ref, w_ref, _FWD_OFFS, b_ref)
        _, _, s, _, zc, _ = _norm_act(a1_ref[...], cz_ref[...], g_ref, bb_ref)
        u_ref[...] = (s * zc).astype(bf16)

    def main(col):
        return pl.BlockSpec((tt, D), lambda i: (i, col))

    def prev(col):
        return pl.BlockSpec((HALO_C, D), lambda i: (jnp.maximum(i * hb - 1, 0), col))

    vec = pl.BlockSpec((1, D), lambda i: (0, 0))
    return pl.pallas_call(
        body, name="conv_fwd", grid=(t // tt,),
        in_specs=[main(0), main(1), main(2), prev(0), prev(1),
                  pl.BlockSpec((HALO_C, D), lambda i: (0, 0)), vec, vec, vec],
        out_specs=pl.BlockSpec((tt, D), lambda i: (i, 0)),
        out_shape=jax.ShapeDtypeStruct((t, D), bf16),
        scratch_shapes=[pltpu.VMEM((tt + HALO_C, D), f32), pltpu.VMEM((N_SHIFT, tt + HALO_C - 8, D), f32),
                        pltpu.VMEM((tt, D), f32)],
        compiler_params=_cparams("parallel"),
    )(p_conv, p_conv, p_conv, p_conv, p_conv, dw_w, dw_b, ln_g, ln_b)


def _conv_bwd(p_conv, du, dw_w, dw_b, ln_g, ln_b, *, tt):
    t = p_conv.shape[0]
    hb = tt // HALO_C
    n_tiles = t // tt
    last_hb = t // HALO_C - 1
    ne = tt + HALO_C

    def body(cv_ref, cg_ref, cz_ref, du_ref, cvp_ref, cgp_ref, cvn_ref, cgn_ref, czn_ref, dun_ref,
             w_ref, b_ref, g_ref, bb_ref, dp_ref, dww_ref, dvec_ref, ext_ref, sh_ref, a1_ref, da1_ref, da0_ref):
        i = pl.program_id(0)
        first = i == 0
        last = i == n_tiles - 1

        @pl.when(first)
        def _():
            dww_ref[...] = jnp.zeros_like(dww_ref)
            dvec_ref[...] = jnp.zeros_like(dvec_ref)

        sig = _sigmoid(cg_ref[...])
        ext_ref[0:HALO_C, :] = jnp.where(first, 0.0, cvp_ref[...] * _sigmoid(cgp_ref[...]))
        ext_ref[HALO_C:HALO_C + tt, :] = cv_ref[...] * sig
        ext_ref[HALO_C + tt:, :] = cvn_ref[...] * _sigmoid(cgn_ref[...])
        _shift_copies(ext_ref, sh_ref, ne + HALO_C - 8)
        _tap_conv(a1_ref, ne, ext_ref, sh_ref, w_ref, _FWD_OFFS, b_ref)
        cz = jnp.concatenate([cz_ref[...], czn_ref[...]], axis=0)
        du_all = jnp.concatenate([du_ref[...], jnp.where(last, 0.0, dun_ref[...])], axis=0)
        xhat, rstd, s, ds, zc, dzc = _norm_act(a1_ref[...], cz, g_ref, bb_ref)
        dln = du_all * zc * ds
        dxhat = dln * g_ref[...]
        da1 = rstd * (dxhat - jnp.mean(dxhat, axis=-1, keepdims=True)
                      - xhat * jnp.mean(dxhat * xhat, axis=-1, keepdims=True))
        da1_ref[...] = da1
        dcz = (du_all * s * dzc)[:tt]
        dvec_ref[0:1, :] += jnp.sum(da1[:tt], axis=0, keepdims=True)
        dvec_ref[1:2, :] += jnp.sum((dln * xhat)[:tt], axis=0, keepdims=True)
        dvec_ref[2:3, :] += jnp.sum(dln[:tt], axis=0, keepdims=True)
        _tap_corr(dww_ref, tt, da1_ref, ext_ref, sh_ref, _FWD_OFFS)
        _shift_copies(da1_ref, sh_ref, ne - 8)
        _tap_conv(da0_ref, tt, da1_ref, sh_ref, w_ref, _BWD_OFFS)
        da0 = da0_ref[...]
        cv = cv_ref[...]
        dp_ref[:, 0:D] = (da0 * sig).astype(bf16)
        dp_ref[:, D:2 * D] = (da0 * cv * sig * (1.0 - sig)).astype(bf16)
        dp_ref[:, 2 * D:] = dcz.astype(bf16)

    def main(col):
        return pl.BlockSpec((tt, D), lambda i: (i, col))

    def prev(col):
        return pl.BlockSpec((HALO_C, D), lambda i: (jnp.maximum(i * hb - 1, 0), col))

    def nxt(col):
        return pl.BlockSpec((HALO_C, D), lambda i: (jnp.minimum((i + 1) * hb, last_hb), col))

    vec = pl.BlockSpec((1, D), lambda i: (0, 0))
    return pl.pallas_call(
        body, name="conv_bwd", grid=(n_tiles,),
        in_specs=[main(0), main(1), main(2), main(0), prev(0), prev(1), nxt(0), nxt(1), nxt(2), nxt(0),
                  pl.BlockSpec((HALO_C, D), lambda i: (0, 0)), vec, vec, vec],
        out_specs=[pl.BlockSpec((tt, 3 * D), lambda i: (i, 0)),
                   pl.BlockSpec((HALO_C, D), lambda i: (0, 0)),
                   pl.BlockSpec((8, D), lambda i: (0, 0))],
        out_shape=[jax.ShapeDtypeStruct((t, 3 * D), bf16), jax.ShapeDtypeStruct((HALO_C, D), f32),
                   jax.ShapeDtypeStruct((8, D), f32)],
        scratch_shapes=[pltpu.VMEM((ne + HALO_C, D), f32), pltpu.VMEM((N_SHIFT, ne + HALO_C - 8, D), f32),
                        pltpu.VMEM((ne, D), f32), pltpu.VMEM((ne, D), f32), pltpu.VMEM((tt, D), f32)],
        compiler_params=_cparams("arbitrary"),
    )(p_conv, p_conv, p_conv, du, p_conv, p_conv, p_conv, p_conv, p_conv, du, dw_w, dw_b, ln_g, ln_b)


def _dot_hi(a, b):
    return lax.dot_general(a, b, (((1,), (0,)), ((), ())), precision=HI, preferred_element_type=f32)


def _chunk_tri(n, lower):
    r = lax.broadcasted_iota(jnp.int32, (n, n), 0)
    c = lax.broadcasted_iota(jnp.int32, (n, n), 1)
    tri = (r >= c) if lower else (r <= c)
    return jnp.where(tri & (r // CH == c // CH), 1.0, 0.0).astype(f32)


def _softplus_and_sigmoid(x):
    e = jnp.exp(-jnp.abs(x))
    log1p = jnp.where(e < 1e-2, e * (1.0 - e * (0.5 - e * (1.0 / 3.0 - 0.25 * e))), jnp.log(1.0 + e))
    return jnp.maximum(x, 0.0) + log1p, _sigmoid(x)


_G_FWD_OFFS = [HALO_G - (KG - 1) + k for k in range(KG)]
_G_FWD_SHIFTS = (5, 6, 7)
_G_BWD_OFFS = [KG - 1 - k for k in range(KG)]
_G_BWD_SHIFTS = (1, 2, 3)


def _gdn_short_conv(pre_ref, ext_ref, sh_ref, n_rows, w_ref):
    _shift_copies(ext_ref, sh_ref, n_rows, _G_FWD_SHIFTS)
    _tap_conv(pre_ref, n_rows, ext_ref, sh_ref, w_ref, _G_FWD_OFFS, shifts=_G_FWD_SHIFTS)
    return pre_ref[...]


def _l2norm_heads(act, scale):
    outs, rs = [], []
    for h in range(NH):
        a = act[:, h * HD:(h + 1) * HD]
        r = lax.rsqrt(jnp.sum(a * a, axis=-1, keepdims=True) + L2_EPS)
        outs.append(a * (r * scale))
        rs.append(jnp.broadcast_to(r, a.shape))
    return jnp.concatenate(outs, axis=-1), jnp.concatenate(rs, axis=-1)


def _gate_math(ba, al_ref, dt_ref):
    lane = lax.broadcasted_iota(jnp.int32, ba.shape, 1)
    is_b = lane < NH
    is_a = (lane >= NH) & (lane < 2 * NH)
    sp, sg = _softplus_and_sigmoid(ba + dt_ref[...])
    neg_a = -jnp.exp(al_ref[...])
    return is_b, is_a, _sigmoid(ba), neg_a * sp, sg, neg_a


def _gdn_pre_fwd(p_qkv, p_ba, cw, alog_v, dt_v, *, tt):
    t = p_qkv.shape[0]
    hb = tt // HALO_G

    def body(q_ref, k_ref, v_ref, qh_ref, kh_ref, vh_ref, ba_ref, wq_ref, wk_ref, wv_ref, al_ref, dt_ref,
             qn_ref, kn_ref, va_ref, gt_ref, ext_ref, sh_ref, pre_ref):
        first = pl.program_id(0) == 0

        def conv_act(x_ref, xh_ref, w_ref):
            ext_ref[0:HALO_G, :] = jnp.where(first, 0.0, xh_ref[...])
            ext_ref[HALO_G:, :] = x_ref[...]
            pre = _gdn_short_conv(pre_ref, ext_ref, sh_ref, tt, w_ref)
            return pre * _sigmoid(pre)

        qn_ref[...] = _l2norm_heads(conv_act(q_ref, qh_ref, wq_ref), HD ** -0.5)[0]
        kn_ref[...] = _l2norm_heads(conv_act(k_ref, kh_ref, wk_ref), 1.0)[0]
        va_ref[...] = conv_act(v_ref, vh_ref, wv_ref)
        is_b, is_a, beta, g, _, _ = _gate_math(ba_ref[...], al_ref, dt_ref)
        gc = _dot_hi(_chunk_tri(tt, lower=True), jnp.where(is_a, g, 0.0))
        gt_ref[...] = jnp.where(is_b, beta, gc)

    def main(col):
        return pl.BlockSpec((tt, D), lambda i: (i, col))

    def prev(col):
        return pl.BlockSpec((HALO_G, D), lambda i: (jnp.maximum(i * hb - 1, 0), col))

    def wspec(col):
        return pl.BlockSpec((8, D), lambda i: (0, col))

    vec = pl.BlockSpec((1, HD), lambda i: (0, 0))
    gblk = pl.BlockSpec((tt, HD), lambda i: (i, 0))
    sds = jax.ShapeDtypeStruct((t, D), f32)
    return pl.pallas_call(
        body, name="gdn_pre_fwd", grid=(t // tt,),
        in_specs=[main(0), main(1), main(2), prev(0), prev(1), prev(2), gblk, wspec(0), wspec(1), wspec(2), vec, vec],
        out_specs=[pl.BlockSpec((tt, D), lambda i: (i, 0))] * 3 + [gblk],
        out_shape=[sds] * 3 + [jax.ShapeDtypeStruct((t, HD), f32)],
        scratch_shapes=[pltpu.VMEM((tt + HALO_G, D), f32), pltpu.VMEM((KG - 1, tt, D), f32), pltpu.VMEM((tt, D), f32)],
        compiler_params=_cparams("parallel"),
    )(p_qkv, p_qkv, p_qkv, p_qkv, p_qkv, p_qkv, p_ba, cw, cw, cw, alog_v, dt_v)


def _gdn_pre_bwd(p_qkv, p_ba, cw, alog_v, dt_v, dqn, dkn, dva, dgt, *, tt):
    t = p_qkv.shape[0]
    hb = tt // HALO_G
    n_tiles = t // tt
    last_hb = t // HALO_G - 1
    ne = tt + HALO_G

    def body(q_ref, k_ref, v_ref, qp_ref, kp_ref, vp_ref, qx_ref, kx_ref, vx_ref,
             dq_ref, dk_ref, dv_ref, dqx_ref, dkx_ref, dvx_ref, ba_ref, dgt_ref,
             wq_ref, wk_ref, wv_ref, al_ref, dt_ref,
             dp_ref, dba_ref, dcw_ref, dad_ref, ext_ref, sh_ref, pre_ref, dpre_ref, draw_ref):
        i = pl.program_id(0)
        first = i == 0
        last = i == n_tiles - 1

        @pl.when(first)
        def _():
            dcw_ref[...] = jnp.zeros_like(dcw_ref)
            dad_ref[...] = jnp.zeros_like(dad_ref)

        def one(x_ref, xp_ref, xx_ref, d_ref, dx_ref, w_ref, col, scale):
            ext_ref[0:HALO_G, :] = jnp.where(first, 0.0, xp_ref[...])
            ext_ref[HALO_G:HALO_G + tt, :] = x_ref[...]
            ext_ref[HALO_G + tt:, :] = xx_ref[...]
            pre = _gdn_short_conv(pre_ref, ext_ref, sh_ref, ne, w_ref)
            act, dact = _silu_and_grad(pre)
            d_out = jnp.concatenate([d_ref[...], jnp.where(last, 0.0, dx_ref[...])], axis=0)
            if scale is None:
                d_act = d_out
            else:
                parts = []
                for h in range(NH):
                    a = act[:, h * HD:(h + 1) * HD]
                    dn = d_out[:, h * HD:(h + 1) * HD]
                    r = lax.rsqrt(jnp.sum(a * a, axis=-1, keepdims=True) + L2_EPS)
                    parts.append(scale * r * (dn - a * (r * r) * jnp.sum(dn * a, axis=-1, keepdims=True)))
                d_act = jnp.concatenate(parts, axis=-1)
            dpre_ref[...] = d_act * dact
            _tap_corr(dcw_ref.at[:, col * D:(col + 1) * D], tt, dpre_ref, ext_ref, sh_ref, _G_FWD_OFFS, shifts=_G_FWD_SHIFTS)
            _shift_copies(dpre_ref, sh_ref, tt, _G_BWD_SHIFTS)
            _tap_conv(draw_ref, tt, dpre_ref, sh_ref, w_ref, _G_BWD_OFFS, shifts=_G_BWD_SHIFTS)
            dp_ref[:, col * D:(col + 1) * D] = draw_ref[...].astype(bf16)

        one(q_ref, qp_ref, qx_ref, dq_ref, dqx_ref, wq_ref, 0, HD ** -0.5)
        one(k_ref, kp_ref, kx_ref, dk_ref, dkx_ref, wk_ref, 1, 1.0)
        one(v_ref, vp_ref, vx_ref, dv_ref, dvx_ref, wv_ref, 2, None)

        is_b, is_a, beta, g, sg, neg_a = _gate_math(ba_ref[...], al_ref, dt_ref)
        dgt_v = dgt_ref[...]
        dg = _dot_hi(_chunk_tri(tt, lower=False), jnp.where(is_a, dgt_v, 0.0))
        d_al = jnp.where(is_a, dg * neg_a * sg, 0.0)
        dba_ref[...] = jnp.where(is_b, dgt_v * beta * (1.0 - beta), d_al)
        dad_ref[0:1, :] += jnp.sum(jnp.where(is_a, dg * g, 0.0), axis=0, keepdims=True)
        dad_ref[1:2, :] += jnp.sum(d_al, axis=0, keepdims=True)

    def main(col):
        return pl.BlockSpec((tt, D), lambda i: (i, col))

    def prev(col):
        return pl.BlockSpec((HALO_G, D), lambda i: (jnp.maximum(i * hb - 1, 0), col))

    def nxt(col):
        return pl.BlockSpec((HALO_G, D), lambda i: (jnp.minimum((i + 1) * hb, last_hb), col))

    def wspec(col):
        return pl.BlockSpec((8, D), lambda i: (0, col))

    vec = pl.BlockSpec((1, HD), lambda i: (0, 0))
    gblk = pl.BlockSpec((tt, HD), lambda i: (i, 0))
    return pl.pallas_call(
        body, name="gdn_pre_bwd", grid=(n_tiles,),
        in_specs=[main(0), main(1), main(2), prev(0), prev(1), prev(2), nxt(0), nxt(1), nxt(2),
                  main(0), main(0), main(0), nxt(0), nxt(0), nxt(0), gblk, gblk,
                  wspec(0), wspec(1), wspec(2), vec, vec],
        out_specs=[pl.BlockSpec((tt, 3 * D), lambda i: (i, 0)), pl.BlockSpec((tt, HD), lambda i: (i, 0)),
                   pl.BlockSpec((8, 3 * D), lambda i: (0, 0)), pl.BlockSpec((8, HD), lambda i: (0, 0))],
        out_shape=[jax.ShapeDtypeStruct((t, 3 * D), bf16), jax.ShapeDtypeStruct((t, HD), f32),
                   jax.ShapeDtypeStruct((8, 3 * D), f32), jax.ShapeDtypeStruct((8, HD), f32)],
        scratch_shapes=[pltpu.VMEM((HALO_G + tt + HALO_G, D), f32), pltpu.VMEM((KG - 1, ne, D), f32),
                        pltpu.VMEM((ne, D), f32), pltpu.VMEM((ne, D), f32), pltpu.VMEM((tt, D), f32)],
        compiler_params=_cparams("arbitrary"),
    )(p_qkv, p_qkv, p_qkv, p_qkv, p_qkv, p_qkv, p_qkv, p_qkv, p_qkv,
      dqn, dkn, dva, dqn, dkn, dva, p_ba, dgt, cw, cw, cw, alog_v, dt_v)


def _dot_b(a, b, dims):
    return lax.dot_general(a.astype(bf16), b.astype(bf16), (dims, ((), ())), preferred_element_type=f32)


def _inverse_by_doubling(ms):
    heads = range(len(ms))
    r = lax.broadcasted_iota(jnp.int32, (CH, CH), 0)
    c = lax.broadcasted_iota(jnp.int32, (CH, CH), 1)
    eye = jnp.where(r == c, 1.0, 0.0).astype(f32)
    p = [eye + ms[h] for h in heads]
    mp = ms
    for _ in range(5):
        mp = [_dot_b(mp[h], mp[h], _NN) for h in heads]
        pm = [_dot_b(p[h], mp[h], _NN) for h in heads]
        p = [p[h] + pm[h] for h in heads]
    return tuple(p)


@jax.custom_vjp
def _known_inverse(ms, ps):
    return ps


def _known_inverse_fwd(ms, ps):
    return ps, ps


def _known_inverse_bwd(ps, cts):
    heads = range(len(ps))
    left = [_dot_b(ps[h], cts[h], _TN) for h in heads]
    return tuple(_dot_b(left[h], ps[h], _NT) for h in heads), tuple(jnp.zeros_like(p) for p in ps)


_known_inverse.defvjp(_known_inverse_fwd, _known_inverse_bwd)


def _chunk_fn(qs, ks, vs, gcs, bbs, ss, ps=None):
    heads = range(len(qs))
    r = lax.broadcasted_iota(jnp.int32, (CH, CH), 0)
    c = lax.broadcasted_iota(jnp.int32, (CH, CH), 1)
    causal = r >= c
    strict = r > c
    gc_row = [gcs[h].T[:CH, :] for h in heads]
    decay = [jnp.where(causal, jnp.exp(jnp.where(causal, gcs[h][:, :CH] - gc_row[h], 0.0)), 0.0) for h in heads]
    kb = [ks[h] * bbs[h] for h in heads]
    egc = [jnp.exp(gcs[h]) for h in heads]
    kk = [_dot_b(kb[h], ks[h], _NT) for h in heads]
    qk = [_dot_b(qs[h], ks[h], _NT) for h in heads]
    m = tuple(-jnp.where(strict, kk[h] * decay[h], 0.0) for h in heads)
    p = _inverse_by_doubling(m) if ps is None else _known_inverse(m, ps)
    u = [_dot_b(p[h], vs[h] * bbs[h], _NN) for h in heads]
    w = [_dot_b(p[h], kb[h] * egc[h], _NN) for h in heads]
    intra = [jnp.where(causal, qk[h] * decay[h], 0.0) for h in heads]
    g_last = [gcs[h][CH - 1:CH, :] for h in heads]
    k_dec = [ks[h] * jnp.exp(g_last[h] - gcs[h]) for h in heads]
    ws = [_dot_b(w[h], ss[h], _NN) for h in heads]
    qs_s = [_dot_b(qs[h] * egc[h], ss[h], _NN) for h in heads]
    v_new = [u[h] - ws[h] for h in heads]
    iv = [_dot_b(intra[h], v_new[h], _NN) for h in heads]
    kv = [_dot_b(k_dec[h], v_new[h], _TN) for h in heads]
    o = tuple(qs_s[h] + iv[h] for h in heads)
    s_new = tuple(ss[h] * jnp.exp(g_last[h]) + kv[h] for h in heads)
    return o, s_new, p


def _head_cols():
    return [slice(h * HD, (h + 1) * HD) for h in range(NH)]


def _head_gates(gt):
    gcs = tuple(jnp.broadcast_to(gt[:, NH + h:NH + h + 1], (CH, HD)) for h in range(NH))
    bbs = tuple(jnp.broadcast_to(gt[:, h:h + 1], (CH, HD)) for h in range(NH))
    return gcs, bbs


def _gdn_scan_fwd(qn, kn, va, gates, *, tt):
    t = qn.shape[0]
    cpb = tt // CH

    def body(q_ref, k_ref, v_ref, gt_ref, o_ref, st_ref, p_ref, s_scr):
        @pl.when(pl.program_id(0) == 0)
        def _():
            s_scr[...] = jnp.zeros_like(s_scr)

        def step(ci, carry):
            rows = pl.ds(pl.multiple_of(ci * CH, CH), CH)
            cols = _head_cols()
            ss = tuple(s_scr[h] for h in range(NH))
            for h in range(NH):
                st_ref[ci, h] = ss[h]
            gcs, bbs = _head_gates(gt_ref[rows, :])
            o, s_new, p = _chunk_fn(*(tuple(ref[rows, cl] for cl in cols) for ref in (q_ref, k_ref, v_ref)), gcs, bbs, ss)
            for h in range(NH):
                o_ref[rows, cols[h]] = o[h]
                s_scr[h] = s_new[h]
                p_ref[ci, h] = p[h].astype(bf16)
            return carry

        lax.fori_loop(0, cpb, step, 0)

    blk = pl.BlockSpec((tt, D), lambda i: (i, 0))
    return pl.pallas_call(
        body, name="gdn_scan_fwd", grid=(t // tt,),
        in_specs=[blk] * 3 + [pl.BlockSpec((tt, HD), lambda i: (i, 0))],
        out_specs=[blk, pl.BlockSpec((cpb, NH, HD, HD), lambda i: (i, 0, 0, 0)),
                   pl.BlockSpec((cpb, NH, CH, CH), lambda i: (i, 0, 0, 0))],
        out_shape=[jax.ShapeDtypeStruct((t, D), f32), jax.ShapeDtypeStruct((t // CH, NH, HD, HD), f32),
                   jax.ShapeDtypeStruct((t // CH, NH, CH, CH), bf16)],
        scratch_shapes=[pltpu.VMEM((NH, HD, HD), f32)],
        compiler_params=_cparams("arbitrary"),
    )(qn, kn, va, gates)


def _gdn_scan_bwd(qn, kn, va, gates, states, inverses, do, *, tt):
    t = qn.shape[0]
    nblk = t // tt
    cpb = tt // CH

    def body(q_ref, k_ref, v_ref, gt_ref, st_ref, p_ref, do_ref, dq_ref, dk_ref, dv_ref, dgt_ref, ds_scr):
        @pl.when(pl.program_id(0) == 0)
        def _():
            ds_scr[...] = jnp.zeros_like(ds_scr)

        def step(j, carry):
            ci = cpb - 1 - j
            rows = pl.ds(pl.multiple_of(ci * CH, CH), CH)
            cols = _head_cols()
            gcs, bbs = _head_gates(gt_ref[rows, :])
            ps = tuple(p_ref[ci, h].astype(f32) for h in range(NH))
            _, vjp = jax.vjp(lambda *a: _chunk_fn(*a, ps=ps)[:2],
                             *(tuple(ref[rows, cl] for cl in cols) for ref in (q_ref, k_ref, v_ref)),
                             gcs, bbs, tuple(st_ref[ci, h] for h in range(NH)))
            grads = vjp((tuple(do_ref[rows, cl] for cl in cols), tuple(ds_scr[h] for h in range(NH))))
            lane = lax.broadcasted_iota(jnp.int32, (CH, HD), 1)
            dgt = jnp.zeros((CH, HD), f32)
            for h in range(NH):
                for ref, g in zip((dq_ref, dk_ref, dv_ref), grads[:3]):
                    ref[rows, cols[h]] = g[h]
                dgt = dgt + jnp.where(lane == NH + h, jnp.sum(grads[3][h], axis=-1, keepdims=True), 0.0)
                dgt = dgt + jnp.where(lane == h, jnp.sum(grads[4][h], axis=-1, keepdims=True), 0.0)
                ds_scr[h] = grads[5][h]
            dgt_ref[rows, :] = dgt
            return carry

        lax.fori_loop(0, cpb, step, 0)

    blk = pl.BlockSpec((tt, D), lambda i: (nblk - 1 - i, 0))
    sblk = pl.BlockSpec((cpb, NH, HD, HD), lambda i: (nblk - 1 - i, 0, 0, 0))
    sds = jax.ShapeDtypeStruct((t, D), f32)
    gblk = pl.BlockSpec((tt, HD), lambda i: (nblk - 1 - i, 0))
    pblk = pl.BlockSpec((cpb, NH, CH, CH), lambda i: (nblk - 1 - i, 0, 0, 0))
    return pl.pallas_call(
        body, name="gdn_scan_bwd", grid=(nblk,),
        in_specs=[blk] * 3 + [gblk, sblk, pblk, blk],
        out_specs=[blk] * 3 + [gblk], out_shape=[sds] * 3 + [jax.ShapeDtypeStruct((t, HD), f32)],
        scratch_shapes=[pltpu.VMEM((NH, HD, HD), f32)],
        compiler_params=_cparams("arbitrary"),
    )(qn, kn, va, gates, states, inverses, do)


def _rms_heads(o):
    ons, rs = [], []
    for h in range(NH):
        a = o[:, h * HD:(h + 1) * HD]
        r = lax.rsqrt(jnp.mean(a * a, axis=-1, keepdims=True) + RMS_EPS)
        ons.append(a * r)
        rs.append(jnp.broadcast_to(r, a.shape))
    return jnp.concatenate(ons, axis=-1), jnp.concatenate(rs, axis=-1)


def _gdn_post_fwd(o, p_gz, ng_b, *, tt):
    t = o.shape[0]

    def body(o_ref, gz_ref, ng_ref, og_ref):
        on, _ = _rms_heads(o_ref[...])
        z, _ = _silu_and_grad(gz_ref[...])
        og_ref[...] = (on * ng_ref[...] * z).astype(bf16)

    blk = pl.BlockSpec((tt, D), lambda i: (i, 0))
    return pl.pallas_call(
        body, name="gdn_post_fwd", grid=(t // tt,),
        in_specs=[blk, blk, pl.BlockSpec((1, D), lambda i: (0, 0))],
        out_specs=blk, out_shape=jax.ShapeDtypeStruct((t, D), bf16),
        compiler_params=_cparams("parallel"),
    )(o, p_gz, ng_b)


def _gdn_post_bwd(o, p_gz, ng_b, dog, *, tt):
    t = o.shape[0]

    def body(o_ref, gz_ref, ng_ref, dog_ref, do_ref, dgz_ref, dng_ref):
        @pl.when(pl.program_id(0) == 0)
        def _():
            dng_ref[...] = jnp.zeros_like(dng_ref)

        on, r = _rms_heads(o_ref[...])
        z, dz = _silu_and_grad(gz_ref[...])
        dog_v = dog_ref[...]
        ng = ng_ref[...]
        dgz_ref[...] = (dog_v * on * ng * dz).astype(bf16)
        dy = dog_v * z
        dng_all = jnp.sum(dy * on, axis=0, keepdims=True)
        dng = dng_all[:, 0:HD]
        for h in range(1, NH):
            dng = dng + dng_all[:, h * HD:(h + 1) * HD]
        dng_ref[0:1, :] += dng
        don = dy * ng
        prod = don * on
        parts = []
        for h in range(NH):
            sl = slice(h * HD, (h + 1) * HD)
            parts.append(don[:, sl] - on[:, sl] * jnp.mean(prod[:, sl], axis=-1, keepdims=True))
        do_ref[...] = r * jnp.concatenate(parts, axis=-1)

    blk = pl.BlockSpec((tt, D), lambda i: (i, 0))
    return pl.pallas_call(
        body, name="gdn_post_bwd", grid=(t // tt,),
        in_specs=[blk, blk, pl.BlockSpec((1, D), lambda i: (0, 0)), blk],
        out_specs=[blk, blk, pl.BlockSpec((8, HD), lambda i: (0, 0))],
        out_shape=[jax.ShapeDtypeStruct((t, D), f32), jax.ShapeDtypeStruct((t, D), bf16),
                   jax.ShapeDtypeStruct((8, HD), f32)],
        compiler_params=_cparams("arbitrary"),
    )(o, p_gz, ng_b, dog)


def _merge(x, y_conf, y_gdn, p_gates, target, w_o, ln_g, ln_b, *, tt):
    t = x.shape[0]

    def body(x_ref, yc_ref, yg_ref, gc_ref, gg_ref, tg_ref, w_ref, g_ref, b_ref,
             loss_ref, dxd_ref, dyc_ref, dyg_ref, dpg_ref, h_ref, dz_ref, dvec_ref):
        @pl.when(pl.program_id(0) == 0)
        def _():
            loss_ref[...] = jnp.zeros_like(loss_ref)
            dvec_ref[...] = jnp.zeros_like(dvec_ref)

        sc = _sigmoid(gc_ref[...])
        sg = _sigmoid(gg_ref[...])
        yc = yc_ref[...]
        yg = yg_ref[...]
        h = (sc * yc + sg * yg).astype(bf16)
        h_ref[...] = h
        z = DN_ALPHA * x_ref[...] + jnp.dot(h, w_ref[...], preferred_element_type=f32)
        mu = jnp.mean(z, axis=-1, keepdims=True)
        cen = z - mu
        rstd = lax.rsqrt(jnp.mean(cen * cen, axis=-1, keepdims=True) + LN_EPS)
        xhat = cen * rstd
        err = xhat * g_ref[...] + b_ref[...] - tg_ref[...]
        loss_ref[...] += 0.5 / D * jnp.sum(err * err)
        dy = err * (1.0 / D)
        dvec_ref[0:1, :] += jnp.sum(dy * xhat, axis=0, keepdims=True)
        dvec_ref[1:2, :] += jnp.sum(dy, axis=0, keepdims=True)
        dxhat = dy * g_ref[...]
        dz = rstd * (dxhat - jnp.mean(dxhat, axis=-1, keepdims=True)
                     - xhat * jnp.mean(dxhat * xhat, axis=-1, keepdims=True))
        dxd_ref[...] = DN_ALPHA * dz
        dz_b = dz.astype(bf16)
        dz_ref[...] = dz_b
        dh = lax.dot_general(dz_b, w_ref[...], (_NT, ((), ())), preferred_element_type=f32)
        dyc_ref[...] = (dh * sc).astype(bf16)
        dyg_ref[...] = (dh * sg).astype(bf16)
        dpg_ref[:, 0:D] = (dh * yc * sc * (1.0 - sc)).astype(bf16)
        dpg_ref[:, D:] = (dh * yg * sg * (1.0 - sg)).astype(bf16)

    blk = pl.BlockSpec((tt, D), lambda i: (i, 0))
    wblk = pl.BlockSpec((D, D), lambda i: (0, 0))
    vec = pl.BlockSpec((1, D), lambda i: (0, 0))
    return pl.pallas_call(
        body, name="merge_norm_loss", grid=(t // tt,),
        in_specs=[blk, blk, blk, pl.BlockSpec((tt, D), lambda i: (i, 0)), pl.BlockSpec((tt, D), lambda i: (i, 1)),
                  blk, wblk, vec, vec],
        out_specs=[pl.BlockSpec((8, HD), lambda i: (0, 0)), blk, blk, blk,
                   pl.BlockSpec((tt, 2 * D), lambda i: (i, 0)), blk, blk, pl.BlockSpec((8, D), lambda i: (0, 0))],
        out_shape=[jax.ShapeDtypeStruct((8, HD), f32), jax.ShapeDtypeStruct((t, D), f32),
                   jax.ShapeDtypeStruct((t, D), bf16), jax.ShapeDtypeStruct((t, D), bf16),
                   jax.ShapeDtypeStruct((t, 2 * D), bf16), jax.ShapeDtypeStruct((t, D), bf16),
                   jax.ShapeDtypeStruct((t, D), bf16), jax.ShapeDtypeStruct((8, D), f32)],
        compiler_params=_cparams("arbitrary"),
    )(x, y_conf, y_gdn, p_gates, p_gates, target, w_o, ln_g, ln_b)


def _place():
    return lax.axis_index("x"), lax.axis_index("y"), lax.axis_index("c")


def _any_specs(n):
    return [pl.BlockSpec(memory_space=pl.ANY)] * n


def _sibling_merge(arrs, name, take_other_half=False):
    k = len(arrs)

    def body(*refs):
        a_refs, o_refs = refs[:k], refs[k:2 * k]
        send_sems, recv_sems = refs[2 * k:]
        x, y, c = _place()
        sends = []
        for i in range(k):
            src = a_refs[i]
            if take_other_half:
                n = a_refs[i].shape[-2] // 2
                lead = (slice(None),) * (len(a_refs[i].shape) - 2)
                src = a_refs[i].at[lead + (pl.ds((1 - c) * n, n), slice(None))]
            cp = pltpu.make_async_remote_copy(src_ref=src, dst_ref=o_refs[i], send_sem=send_sems.at[i],
                                              recv_sem=recv_sems.at[i], device_id=(x, y, 1 - c), device_id_type=MESH)
            cp.start()
            sends.append(cp)
        for cp in sends:
            cp.wait()

    def out_sds(a):
        rows = a.shape[-2] // 2 if take_other_half else a.shape[-2]
        return jax.ShapeDtypeStruct(a.shape[:-2] + (rows, a.shape[-1]), a.dtype)

    return pl.pallas_call(
        body, name=name, in_specs=_any_specs(k), out_specs=_any_specs(k),
        out_shape=[out_sds(a) for a in arrs],
        scratch_shapes=[pltpu.SemaphoreType.DMA((k,)), pltpu.SemaphoreType.DMA((k,))],
    )(*arrs)


def _join_halves(mine, other):
    c = lax.axis_index("c")
    return jnp.concatenate([jnp.where(c == 0, mine, other), jnp.where(c == 0, other, mine)], axis=-2)


def _chip_exchange(arrs, name, scatter):
    k = len(arrs)

    def body(*refs):
        a_refs, o_refs = refs[:k], refs[k:2 * k]
        send_sems, recv_sems, local_sems = refs[2 * k:]
        x, y, c = _place()
        me = 2 * x + y
        peers = [(1 - x, y), (x, 1 - y), (1 - x, 1 - y)]

        def src(i, j):
            return a_refs[i].at[j] if scatter else a_refs[i]

        def copy(i, n, send_j, slot):
            px, py = peers[n]
            return pltpu.make_async_remote_copy(
                src_ref=src(i, send_j), dst_ref=o_refs[i].at[slot], send_sem=send_sems.at[3 * i + n],
                recv_sem=recv_sems.at[3 * i + n], device_id=(px, py, c), device_id_type=MESH)

        owns = [pltpu.make_async_copy(src(i, me), o_refs[i].at[me], local_sems.at[i]) for i in range(k)]
        for own in owns:
            own.start()
        sends = [copy(i, n, 2 * peers[n][0] + peers[n][1], me) for n in range(3) for i in range(k)]
        for cp in sends:
            cp.start()
        for n in range(3):
            for i in range(k):
                copy(i, n, me, 2 * peers[n][0] + peers[n][1]).wait_recv()
        for cp in sends:
            cp.wait_send()
        for own in owns:
            own.wait()

    def out_sds(a):
        return jax.ShapeDtypeStruct((N_CHIPS,) + tuple(a.shape[1:] if scatter else a.shape), a.dtype)

    return pl.pallas_call(
        body, name=name, in_specs=_any_specs(k), out_specs=_any_specs(k),
        out_shape=[out_sds(a) for a in arrs],
        scratch_shapes=[pltpu.SemaphoreType.DMA((3 * k,)), pltpu.SemaphoreType.DMA((3 * k,)),
                        pltpu.SemaphoreType.DMA((k,))],
    )(*arrs)


def _pair_sum(g_all, got, c_arr, name, out_dtype):
    n, w = got.shape[1:]
    tile = n // 4
    n_tiles = n // tile

    def body(c_ref, a_ref, b_ref, o_ref):
        o_ref[...] = (a_ref[...] + b_ref[...]).astype(out_dtype)

    return pl.pallas_call(
        body, name=name,
        grid_spec=pltpu.PrefetchScalarGridSpec(
            num_scalar_prefetch=1, grid=(N_CHIPS, n_tiles),
            in_specs=[pl.BlockSpec((1, tile, w), lambda j, i, c_ref: (j, c_ref[0] * n_tiles + i, 0)),
                      pl.BlockSpec((1, tile, w), lambda j, i, c_ref: (j, i, 0))],
            out_specs=pl.BlockSpec((1, tile, w), lambda j, i, c_ref: (j, i, 0))),
        out_shape=jax.ShapeDtypeStruct(got.shape, out_dtype),
        compiler_params=_cparams("parallel", "parallel"),
    )(c_arr, g_all, got)


def _sum_slots(a, name):
    n, w = a.shape[1:]
    tile = n // 4

    def body(a_ref, o_ref):
        o_ref[...] = ((a_ref[0].astype(f32) + a_ref[1].astype(f32)) + a_ref[2].astype(f32)) + a_ref[3].astype(f32)

    return pl.pallas_call(
        body, name=name, grid=(n // tile,),
        in_specs=[pl.BlockSpec((N_CHIPS, tile, w), lambda i: (0, i, 0))],
        out_specs=pl.BlockSpec((tile, w), lambda i: (i, 0)),
        out_shape=jax.ShapeDtypeStruct((n, w), f32),
        compiler_params=_cparams("parallel"),
    )(a)


def _adamw(w, g, m, v, name):
    rows, width = w.shape
    tile = rows // 8
    c1 = 1.0 / (1.0 - ADAM_B1 ** ADAM_STEP)
    c2 = 1.0 / (1.0 - ADAM_B2 ** ADAM_STEP)

    def body(w_ref, g_ref, m_ref, v_ref, d_ref, mo_ref, vo_ref):
        g_v = g_ref[...]
        m_new = ADAM_B1 * m_ref[...] + (1.0 - ADAM_B1) * g_v
        v_new = ADAM_B2 * v_ref[...] + (1.0 - ADAM_B2) * (g_v * g_v)
        mo_ref[...] = m_new
        vo_ref[...] = v_new
        d_ref[...] = -ADAM_LR * ((m_new * c1) / (jnp.sqrt(v_new * c2) + ADAM_EPS) + ADAM_WD * w_ref[...])

    blk = pl.BlockSpec((tile, width), lambda i: (i, 0))
    sds = jax.ShapeDtypeStruct((rows, width), f32)
    return pl.pallas_call(
        body, name=name, grid=(rows // tile,),
        in_specs=[blk] * 4, out_specs=[blk] * 3, out_shape=[sds] * 3,
        compiler_params=_cparams("parallel"),
    )(w, g, m, v)


R_DW = 3 * SQ_BLK
R_CW = R_DW + 8
R_VEC = R_CW + 8
R_SMALL = R_VEC + 8
REST_ROWS = 896


def _pack_small(conf_dw_w, gdn_conv_w, vecs, a_log, dt_bias, norm_g):
    dw = jnp.pad(conf_dw_w.reshape(-1), (0, 8 * D - KC * SQ_BLK)).reshape(8, D)
    cw = jnp.pad(gdn_conv_w.reshape(-1), (0, 5 * D)).reshape(8, D)
    vec = jnp.pad(jnp.stack(vecs), ((0, 3), (0, 0)))
    small = jnp.pad(jnp.concatenate([a_log, dt_bias, norm_g]), (0, D - 2 * NH - HD)).reshape(1, D)
    return jnp.pad(jnp.concatenate([dw, cw, vec, small], axis=0), ((0, REST_ROWS - R_SMALL - 1), (0, 0)))


def _pack_rest(conf_w_out, gdn_w_out, w_o, small):
    return jnp.concatenate([conf_w_out, gdn_w_out, w_o, small], axis=0)


def _unpack_rest(p):
    conf_dw_w = p[R_DW:R_DW + 8].reshape(-1)[:KC * SQ_BLK].reshape(KC, SQ_BLK)
    gdn_conv_w = p[R_CW:R_CW + 3].reshape(KG, 3 * SQ_BLK)
    small = p[R_SMALL]
    return dict(conf_w_out=p[0:SQ_BLK], gdn_w_out=p[SQ_BLK:2 * SQ_BLK], w_o=p[2 * SQ_BLK:R_DW],
                conf_dw_w=conf_dw_w, gdn_conv_w=gdn_conv_w, conf_dw_b=p[R_VEC], conf_ln_g=p[R_VEC + 1],
                conf_ln_b=p[R_VEC + 2], post_ln_g=p[R_VEC + 3], post_ln_b=p[R_VEC + 4],
                gdn_A_log=small[0:NH], gdn_dt_bias=small[NH:2 * NH], gdn_norm_g=small[2 * NH:2 * NH + HD])


_WEIGHT_ORDER = ("w_in", "conf_dw_w", "conf_dw_b", "conf_ln_g", "conf_ln_b", "conf_w_out", "gdn_conv_w",
                 "gdn_A_log", "gdn_dt_bias", "gdn_norm_g", "gdn_w_out", "w_o", "post_ln_g", "post_ln_b")


def _gather_weights(w_in, conf_w_out, gdn_w_out, w_o, conf_dw_w, gdn_conv_w):
    c = lax.axis_index("c")
    sq = jnp.concatenate([conf_w_out, gdn_w_out, w_o], axis=0).astype(bf16)
    w_half = lax.dynamic_slice_in_dim(w_in.astype(bf16), c * (D // 2), D // 2, axis=0)
    sq_half = lax.dynamic_slice_in_dim(sq, c * (sq.shape[0] // 2), sq.shape[0] // 2, axis=0)
    small = jnp.concatenate([jnp.pad(conf_dw_w.reshape(-1), (0, 8 * D - KC * SQ_BLK)).reshape(8, D),
                             jnp.pad(gdn_conv_w.reshape(-1), (0, 5 * D)).reshape(8, D)], axis=0)
    got_w, got_sq, small_all = _chip_exchange([w_half, sq_half, small], "weight_gather_chips", scatter=False)
    oth_w, oth_sq = _sibling_merge([got_w, got_sq], "weight_gather_sibling")
    w4 = _join_halves(got_w, oth_w)
    sq4 = _join_halves(got_sq, oth_sq)
    sq_full = [sq4[:, n * SQ_BLK:(n + 1) * SQ_BLK].reshape(D, D) for n in range(3)]
    dw_full = small_all[:, 0:8].reshape(N_CHIPS, 8 * D)[:, :KC * SQ_BLK].reshape(N_CHIPS, KC, SQ_BLK)
    dw_full = dw_full.transpose(1, 0, 2).reshape(KC, D)
    cw_full = small_all[:, 8:11].reshape(N_CHIPS, KG, 3 * SQ_BLK).transpose(1, 0, 2).reshape(KG, 3 * D)
    return w4, sq_full[0], sq_full[1], sq_full[2], dw_full, cw_full


def _w_in_cols(w4, lo, hi):
    parts = []
    for j in range(N_CHIPS):
        a, b = max(lo, j * W_IN_BLK), min(hi, (j + 1) * W_IN_BLK)
        if a < b:
            parts.append(w4[j, :, a - j * W_IN_BLK:b - j * W_IN_BLK])
    return parts[0] if len(parts) == 1 else jnp.concatenate(parts, axis=1)


def _w_in_by_chip(pieces):
    chips = []
    for j in range(N_CHIPS):
        lo, hi = j * W_IN_BLK, (j + 1) * W_IN_BLK
        parts = []
        for start, arr in pieces:
            a, b = max(lo, start), min(hi, start + arr.shape[1])
            if a < b:
                parts.append(arr[:, a - start:b - start])
        chips.append(jnp.concatenate(parts, axis=1))
    return jnp.stack(chips)


def _reduce_scatter(g_w, g_rest):
    c_arr = lax.axis_index("c").astype(jnp.int32).reshape(1)
    got_w, got_r = _sibling_merge([g_w, g_rest], "grad_sibling_halves", take_other_half=True)
    pair_w = _pair_sum(g_w, got_w, c_arr, "grad_pair_sum_w_in", bf16)
    pair_r = _pair_sum(g_rest, got_r, c_arr, "grad_pair_sum_rest", f32)
    all_w, all_r = _chip_exchange([pair_w, pair_r], "grad_chip_scatter", scatter=True)
    tot_w, tot_r = _sum_slots(all_w, "grad_chip_sum_w_in"), _sum_slots(all_r, "grad_chip_sum_rest")
    oth_w, oth_r = _sibling_merge([tot_w, tot_r], "grad_sibling_result")
    return _join_halves(tot_w, oth_w), _join_halves(tot_r, oth_r)


def _local_step(x2, tgt, w4, wc_out, wg_out, wo_full, dw_full, cw_full, conf_dw_b, conf_ln_g, conf_ln_b,
                gdn_A_log, gdn_dt_bias, gdn_norm_g, post_ln_g, post_ln_b):
    t = x2.shape[0]
    tt = min(TOKEN_TILE, t)
    tm = min(512, t)

    w_conv, w_qkv, w_gz = _w_in_cols(w4, 0, 3 * D), _w_in_cols(w4, 3 * D, 6 * D), _w_in_cols(w4, 6 * D, 7 * D)
    w_gates = _w_in_cols(w4, 7 * D + 2 * NH, W_IN_COLS)
    dw_pad = jnp.pad(dw_full, ((0, HALO_C - KC), (0, 0)))
    cw_pad = jnp.pad(cw_full, ((0, 8 - KG), (0, 0)))
    row = lambda v: v.reshape(1, D)
    alog_v = jnp.pad(gdn_A_log, (NH, HD - 2 * NH)).reshape(1, HD)
    dt_v = jnp.pad(gdn_dt_bias, (NH, HD - 2 * NH)).reshape(1, HD)
    ng_b = row(jnp.tile(gdn_norm_g, NH))
    w_ba = jnp.pad(_w_in_cols(w4, 7 * D, 7 * D + 2 * NH), ((0, 0), (0, HD - 2 * NH)))

    x_b = x2.astype(bf16)

    p_conv = _mm_multi([x_b], [w_conv], out_dtype=f32, tm=tm, tn=1024, name="proj_conv")
    p_qkv = _mm_multi([x_b], [w_qkv], out_dtype=f32, tm=tm, tn=1024, name="proj_qkv")
    p_gz = _mm_multi([x_b], [w_gz], out_dtype=f32, tm=tm, tn=1024, name="proj_gz")
    p_gates = _mm_multi([x_b], [w_gates], out_dtype=f32, tm=tm, tn=1024, name="proj_gates")
    p_ba = _mm_multi([x_b], [w_ba], out_dtype=f32, tm=tm, tn=HD, name="proj_ba")

    u = _conv_fwd(p_conv, dw_pad, row(conf_dw_b), row(conf_ln_g), row(conf_ln_b), tt=tt)
    y_conf = _mm_multi([u], [wc_out], out_dtype=f32, tm=tm, tn=1024, name="conf_out")

    qn, kn, va, gates = _gdn_pre_fwd(p_qkv, p_ba, cw_pad, alog_v, dt_v, tt=tt)
    o, states, inverses = _gdn_scan_fwd(qn, kn, va, gates, tt=tt)
    og = _gdn_post_fwd(o, p_gz, ng_b, tt=tt)
    y_gdn = _mm_multi([og], [wg_out], out_dtype=f32, tm=tm, tn=1024, name="gdn_out")

    loss_blk, dxd, dyc, dyg, dp_gates, h, dz, dpost = _merge(
        x2, y_conf, y_gdn, p_gates, tgt, wo_full, row(post_ln_g), row(post_ln_b), tt=tt)

    d_wo = _mm_kloop(h, dz, tm=D, tn=1024, tk=min(512, t), name="grad_w_o")
    du = _mm_multi([dyc], [wc_out], out_dtype=f32, tm=tm, tn=1024, name="conf_out_bwd", rhs_t=True)
    d_wc = _mm_kloop(u, dyc, tm=D, tn=1024, tk=min(512, t), name="grad_conf_w_out")
    dog = _mm_multi([dyg], [wg_out], out_dtype=f32, tm=tm, tn=1024, name="gdn_out_bwd", rhs_t=True)
    d_wg = _mm_kloop(og, dyg, tm=D, tn=1024, tk=min(512, t), name="grad_gdn_w_out")

    dp_conv, d_dww, dconv_vec = _conv_bwd(p_conv, du, dw_pad, row(conf_dw_b), row(conf_ln_g), row(conf_ln_b), tt=tt)

    do, dp_gz, dng = _gdn_post_bwd(o, p_gz, ng_b, dog, tt=tt)
    dqn, dkn, dva, dgates = _gdn_scan_bwd(qn, kn, va, gates, states, inverses, do, tt=tt)
    dp_qkv, dp_ba, d_cw, d_ad = _gdn_pre_bwd(p_qkv, p_ba, cw_pad, alog_v, dt_v, dqn, dkn, dva, dgates, tt=tt)
    dp_ba_b = dp_ba.astype(bf16)

    grad_x = _mm_multi(
        [dp_conv, dp_qkv, dp_gz, dp_gates, dp_ba_b],
        [w_conv, w_qkv, w_gz, w_gates, w_ba], dxd,
        out_dtype=f32, tm=min(256, t), tn=512, name="grad_x", rhs_t=True)

    tk = min(512, t)
    d_w_conv = _mm_kloop(x_b, dp_conv, tm=D, tn=1024, tk=tk, name="grad_w_in_conv")
    d_w_qkv = _mm_kloop(x_b, dp_qkv, tm=D, tn=1024, tk=tk, name="grad_w_in_qkv")
    d_w_gz = _mm_kloop(x_b, dp_gz, tm=D, tn=1024, tk=tk, name="grad_w_in_gz")
    d_w_gates = _mm_kloop(x_b, dp_gates, tm=D, tn=1024, tk=tk, name="grad_w_in_gates")
    d_w_ba = _mm_kloop(x_b, dp_ba_b, tm=D, tn=HD, tk=tk, name="grad_w_in_ba")
    d_w_in = _w_in_by_chip([(0, d_w_conv), (3 * D, d_w_qkv), (6 * D, d_w_gz), (7 * D, d_w_ba[:, :2 * NH]),
                            (7 * D + 2 * NH, d_w_gates)])

    return (loss_blk[0, 0], grad_x, d_w_in, d_wc, d_wg, d_wo, d_dww, d_cw, dconv_vec, dpost, d_ad, dng)


def kernel(x, w_in, conf_dw_w, conf_dw_b, conf_ln_g, conf_ln_b, conf_w_out, gdn_conv_w, gdn_A_log, gdn_dt_bias, gdn_norm_g, gdn_w_out, w_o, post_ln_g, post_ln_b, loss_target, m_w_in, m_conf_dw_w, m_conf_dw_b, m_conf_ln_g, m_conf_ln_b, m_conf_w_out, m_gdn_conv_w, m_gdn_A_log, m_gdn_dt_bias, m_gdn_norm_g, m_gdn_w_out, m_w_o, m_post_ln_g, m_post_ln_b, v_w_in, v_conf_dw_w, v_conf_dw_b, v_conf_ln_g, v_conf_ln_b, v_conf_w_out, v_gdn_conv_w, v_gdn_A_log, v_gdn_dt_bias, v_gdn_norm_g, v_gdn_w_out, v_w_o, v_post_ln_g, v_post_ln_b):
    x2 = x.reshape(x.shape[-2], D)
    tgt = loss_target.reshape(x2.shape)
    w4, wc_out, wg_out, wo_full, dw_full, cw_full = _gather_weights(
        w_in, conf_w_out, gdn_w_out, w_o, conf_dw_w, gdn_conv_w)
    (loss_part, grad_x, d_w_in, d_wc, d_wg, d_wo, d_dww, d_cw, dconv_vec, dpost, d_ad, dng) = _local_step(
        x2, tgt, w4, wc_out, wg_out, wo_full, dw_full, cw_full, conf_dw_b, conf_ln_g, conf_ln_b,
        gdn_A_log, gdn_dt_bias, gdn_norm_g, post_ln_g, post_ln_b)
    loss = lax.psum(loss_part, ("x", "y", "c"))

    dww_c = d_dww[:KC].reshape(KC, N_CHIPS, SQ_BLK)
    dcw_c = d_cw[:KG].reshape(KG, N_CHIPS, 3 * SQ_BLK)
    vecs = [dconv_vec[0], dconv_vec[1], dconv_vec[2], dpost[0], dpost[1]]
    g_rest = jnp.stack([
        _pack_rest(d_wc[j * SQ_BLK:(j + 1) * SQ_BLK], d_wg[j * SQ_BLK:(j + 1) * SQ_BLK], d_wo[j * SQ_BLK:(j + 1) * SQ_BLK],
                   _pack_small(dww_c[:, j], dcw_c[:, j], vecs, d_ad[0, NH:2 * NH], d_ad[1, NH:2 * NH], dng[0]))
        for j in range(N_CHIPS)])
    g_w_in, g_rest = _reduce_scatter(d_w_in, g_rest)

    def rest_of(w_c, w_g, w_oo, dw, cw, b1, g1, b2, g2, b3, a_log, dt_bias, norm_g):
        return _pack_rest(w_c, w_g, w_oo, _pack_small(dw, cw, [b1, g1, b2, g2, b3], a_log, dt_bias, norm_g))

    w_r = rest_of(conf_w_out, gdn_w_out, w_o, conf_dw_w, gdn_conv_w, conf_dw_b, conf_ln_g, conf_ln_b,
                  post_ln_g, post_ln_b, gdn_A_log, gdn_dt_bias, gdn_norm_g)
    m_r = rest_of(m_conf_w_out, m_gdn_w_out, m_w_o, m_conf_dw_w, m_gdn_conv_w, m_conf_dw_b, m_conf_ln_g, m_conf_ln_b,
                  m_post_ln_g, m_post_ln_b, m_gdn_A_log, m_gdn_dt_bias, m_gdn_norm_g)
    v_r = rest_of(v_conf_w_out, v_gdn_w_out, v_w_o, v_conf_dw_w, v_gdn_conv_w, v_conf_dw_b, v_conf_ln_g, v_conf_ln_b,
                  v_post_ln_g, v_post_ln_b, v_gdn_A_log, v_gdn_dt_bias, v_gdn_norm_g)
    upd_w_in = _adamw(w_in, g_w_in, m_w_in, v_w_in, "adamw_w_in")
    upd_rest = _adamw(w_r, g_rest, m_r, v_r, "adamw_rest")

    out = [loss, grad_x.reshape(x.shape)]
    for big, rest in zip((g_w_in,) + tuple(upd_w_in), (g_rest,) + tuple(upd_rest)):
        d = dict(_unpack_rest(rest), w_in=big)
        out += [d[n] for n in _WEIGHT_ORDER]
    return tuple(out)
```

```python
import functools

import jax
import jax.numpy as jnp
from jax import lax
from jax.experimental import pallas as pl
from jax.experimental.pallas import tpu as pltpu

f32 = jnp.float32
bf16 = jnp.bfloat16
HI = lax.Precision.HIGHEST
MESH = pl.DeviceIdType.MESH

D = 1024
NH = 8
HD = 128
CH = 64
KC = 31
KG = 4
HALO_C = 32
HALO_G = 8
LANE = 128
STRIP = 32
N_SHIFT = 7
LN_EPS = 1e-5
RMS_EPS = 1e-6
L2_EPS = 1e-6
DN_ALPHA = 2.0 ** 0.25
N_CHIPS = 4
W_IN_COLS = 9232
W_IN_BLK = W_IN_COLS // N_CHIPS
SQ_BLK = D // N_CHIPS
VMEM_LIMIT = 52 * 1024 * 1024
TOKEN_TILE = 256

ADAM_LR = 0.001
ADAM_B1 = 0.9
ADAM_B2 = 0.999
ADAM_EPS = 1e-08
ADAM_WD = 0.01
ADAM_STEP = 10


def _sigmoid(x):
    return 1.0 / (1.0 + jnp.exp(-x))


def _silu_and_grad(x):
    s = _sigmoid(x)
    return x * s, s * (1.0 + x * (1.0 - s))


_NN = ((1,), (0,))
_NT = ((1,), (1,))
_TN = ((0,), (0,))


def _cparams(*sem):
    return pltpu.CompilerParams(dimension_semantics=sem, vmem_limit_bytes=VMEM_LIMIT)


def _mm_multi(a_list, b_list, addend=None, *, out_dtype, tm, tn, name, rhs_t=False, scatter=()):
    n_pairs = len(a_list)
    m = a_list[0].shape[0]
    n = b_list[0].shape[0 if rhs_t else 1]
    has_add = addend is not None
    dims = (_NT if rhs_t else _NN, ((), ()))
    n_in = 2 * n_pairs + has_add
    k = len(scatter)
    grid = (n // tn, m // tm)

    def body(*refs):
        a_refs = refs[:n_pairs]
        b_refs = refs[n_pairs:2 * n_pairs]
        o_ref = refs[n_in + k]
        if k:
            start, finish = _chip_exchange_ops(refs[n_in:n_in + k], refs[n_in + k + 1:n_in + 2 * k + 1],
                                               *refs[n_in + 2 * k + 1:], True)
            step = pl.program_id(0) * grid[1] + pl.program_id(1)
            pl.when(step == 0)(start)
        acc = None
        for a_ref, b_ref in zip(a_refs, b_refs):
            p = lax.dot_general(a_ref[...].astype(bf16), b_ref[...].astype(bf16), dims, preferred_element_type=f32)
            acc = p if acc is None else acc + p
        if has_add:
            acc = acc + refs[2 * n_pairs][...]
        o_ref[...] = acc.astype(out_dtype)
        if k:
            pl.when(step == grid[0] * grid[1] - 1)(finish)

    in_specs = [pl.BlockSpec((tm, a.shape[1]), lambda j, i: (i, 0)) for a in a_list]
    if rhs_t:
        in_specs += [pl.BlockSpec((tn, b.shape[1]), lambda j, i: (j, 0)) for b in b_list]
    else:
        in_specs += [pl.BlockSpec((b.shape[0], tn), lambda j, i: (0, j)) for b in b_list]
    args = list(a_list) + list(b_list)
    if has_add:
        in_specs.append(pl.BlockSpec((tm, tn), lambda j, i: (i, j)))
        args.append(addend)
    out = pl.pallas_call(
        body, name=name, grid=grid,
        in_specs=in_specs + _any_specs(k), out_specs=[pl.BlockSpec((tm, tn), lambda j, i: (i, j))] + _any_specs(k),
        out_shape=[jax.ShapeDtypeStruct((m, n), out_dtype)] + _chip_exchange_shapes(scatter, True),
        scratch_shapes=_chip_exchange_sems(k) if k else [],
        compiler_params=_cparams("arbitrary", "arbitrary") if k else _cparams("parallel", "parallel"),
    )(*args, *scatter)
    return out if k else out[0]


def _mm_kloop(a, b, *, tm, tn, tk, name):
    k, m = a.shape
    n = b.shape[1]
    nk = k // tk

    def body(a_ref, b_ref, o_ref):
        @pl.when(pl.program_id(2) == 0)
        def _():
            o_ref[...] = jnp.zeros_like(o_ref)
        o_ref[...] += lax.dot_general(a_ref[...].astype(bf16), b_ref[...].astype(bf16), (_TN, ((), ())),
                                      preferred_element_type=f32)

    return pl.pallas_call(
        body, name=name, grid=(n // tn, m // tm, nk),
        in_specs=[pl.BlockSpec((tk, tm), lambda j, i, kk: (kk, i)), pl.BlockSpec((tk, tn), lambda j, i, kk: (kk, j))],
        out_specs=pl.BlockSpec((tm, tn), lambda j, i, kk: (i, j)),
        out_shape=jax.ShapeDtypeStruct((m, n), f32),
        compiler_params=_cparams("parallel", "parallel", "arbitrary"),
    )(a, b)


def _shift_copies(src_ref, sh_ref, n, shifts=tuple(range(1, 8))):
    for i, b in enumerate(shifts):
        sh_ref[i, 0:n, :] = src_ref[pl.ds(b, n), :]


def _by_residue(offs):
    groups = {}
    for k, off in enumerate(offs):
        groups.setdefault(off % 8, []).append((k, off // 8))
    return groups


def _slab(src_ref, sh_ref, shifts, b, r0, n, lanes):
    ref = src_ref if b == 0 else sh_ref.at[shifts.index(b)]
    return ref[r0:r0 + n, lanes]


def _tap_conv(out_ref, n_rows, src_ref, sh_ref, w_ref, offs, bias_ref=None, shifts=tuple(range(1, 8))):
    groups = _by_residue(offs)
    for j in range(D // LANE):
        lanes = slice(j * LANE, (j + 1) * LANE)
        wv = [w_ref[k:k + 1, lanes] for k in range(len(offs))]
        for r0 in range(0, n_rows, STRIP):
            n = min(STRIP, n_rows - r0)
            accs = [jnp.zeros((n, LANE), f32) if bias_ref is None else jnp.broadcast_to(bias_ref[0:1, lanes], (n, LANE)),
                    jnp.zeros((n, LANE), f32)]
            m = 0
            for b, taps in groups.items():
                a_lo = min(a for _, a in taps)
                a_hi = max(a for _, a in taps)
                wide = _slab(src_ref, sh_ref, shifts, b, r0 + 8 * a_lo, 8 * (a_hi - a_lo) + n, lanes)
                for k, a in taps:
                    accs[m % 2] = accs[m % 2] + wv[k] * wide[8 * (a - a_lo):8 * (a - a_lo) + n]
                    m += 1
            out_ref[r0:r0 + n, lanes] = accs[0] + accs[1]


def _tap_corr(dw_ref, n_rows, lhs_ref, src_ref, sh_ref, offs, shifts=tuple(range(1, 8))):
    groups = _by_residue(offs)
    for j in range(D // LANE):
        lanes = slice(j * LANE, (j + 1) * LANE)
        accs = [jnp.zeros((8, LANE), f32) for _ in offs]
        for r0 in range(0, n_rows, STRIP):
            n = min(STRIP, n_rows - r0)
            d = lhs_ref[r0:r0 + n, lanes]
            for b, taps in groups.items():
                a_lo = min(a for _, a in taps)
                a_hi = max(a for _, a in taps)
                wide = _slab(src_ref, sh_ref, shifts, b, r0 + 8 * a_lo, 8 * (a_hi - a_lo) + n, lanes)
                for k, a in taps:
                    prod = d * wide[8 * (a - a_lo):8 * (a - a_lo) + n]
                    part = prod[0:8]
                    for q in range(1, n // 8):
                        part = part + prod[8 * q:8 * q + 8]
                    accs[k] = accs[k] + part
        for k in range(len(offs)):
            dw_ref[k:k + 1, lanes] += jnp.sum(accs[k], axis=0, keepdims=True)


_FWD_OFFS = [HALO_C - (KC - 1) + k for k in range(KC)]
_BWD_OFFS = [KC - 1 - k for k in range(KC)]


def _norm_act(a1, cz, g_ref, bb_ref):
    mu = jnp.mean(a1, axis=-1, keepdims=True)
    cen = a1 - mu
    var = jnp.mean(cen * cen, axis=-1, keepdims=True)
    rstd = lax.rsqrt(var + LN_EPS)
    xhat = cen * rstd
    ln = xhat * g_ref[...] + bb_ref[...]
    s, ds = _silu_and_grad(ln)
    zc, dzc = _silu_and_grad(cz)
    return xhat, rstd, s, ds, zc, dzc


def _conv_fwd(p_conv, dw_w, dw_b, ln_g, ln_b, *, tt):
    t = p_conv.shape[0]
    hb = tt // HALO_C

    def body(cv_ref, cg_ref, cz_ref, cvh_ref, cgh_ref, w_ref, b_ref, g_ref, bb_ref, u_ref, ext_ref, sh_ref, a1_ref):
        first = pl.program_id(0) == 0
        halo = cvh_ref[...] * _sigmoid(cgh_ref[...])
        ext_ref[0:HALO_C, :] = jnp.where(first, 0.0, halo)
        ext_ref[HALO_C:, :] = cv_ref[...] * _sigmoid(cg_ref[...])
        _shift_copies(ext_ref, sh_ref, tt + HALO_C - 8)
        _tap_conv(a1_ref, tt, ext_ref, sh_ref, w_ref, _FWD_OFFS, b_ref)
        _, _, s, _, zc, _ = _norm_act(a1_ref[...], cz_ref[...], g_ref, bb_ref)
        u_ref[...] = (s * zc).astype(bf16)

    def main(col):
        return pl.BlockSpec((tt, D), lambda i: (i, col))

    def prev(col):
        return pl.BlockSpec((HALO_C, D), lambda i: (jnp.maximum(i * hb - 1, 0), col))

    vec = pl.BlockSpec((1, D), lambda i: (0, 0))
    return pl.pallas_call(
        body, name="conv_fwd", grid=(t // tt,),
        in_specs=[main(0), main(1), main(2), prev(0), prev(1),
                  pl.BlockSpec((HALO_C, D), lambda i: (0, 0)), vec, vec, vec],
        out_specs=pl.BlockSpec((tt, D), lambda i: (i, 0)),
        out_shape=jax.ShapeDtypeStruct((t, D), bf16),
        scratch_shapes=[pltpu.VMEM((tt + HALO_C, D), f32), pltpu.VMEM((N_SHIFT, tt + HALO_C - 8, D), f32),
                        pltpu.VMEM((tt, D), f32)],
        compiler_params=_cparams("parallel"),
    )(p_conv, p_conv, p_conv, p_conv, p_conv, dw_w, dw_b, ln_g, ln_b)


def _conv_bwd(p_conv, du, dw_w, dw_b, ln_g, ln_b, *, tt):
    t = p_conv.shape[0]
    hb = tt // HALO_C
    n_tiles = t // tt
    last_hb = t // HALO_C - 1
    ne = tt + HALO_C

    def body(cv_ref, cg_ref, cz_ref, du_ref, cvp_ref, cgp_ref, cvn_ref, cgn_ref, czn_ref, dun_ref,
             w_ref, b_ref, g_ref, bb_ref, dp_ref, dww_ref, dvec_ref, ext_ref, sh_ref, a1_ref, da1_ref, da0_ref):
        i = pl.program_id(0)
        first = i == 0
        last = i == n_tiles - 1

        @pl.when(first)
        def _():
            dww_ref[...] = jnp.zeros_like(dww_ref)
            dvec_ref[...] = jnp.zeros_like(dvec_ref)

        sig = _sigmoid(cg_ref[...])
        ext_ref[0:HALO_C, :] = jnp.where(first, 0.0, cvp_ref[...] * _sigmoid(cgp_ref[...]))
        ext_ref[HALO_C:HALO_C + tt, :] = cv_ref[...] * sig
        ext_ref[HALO_C + tt:, :] = cvn_ref[...] * _sigmoid(cgn_ref[...])
        _shift_copies(ext_ref, sh_ref, ne + HALO_C - 8)
        _tap_conv(a1_ref, ne, ext_ref, sh_ref, w_ref, _FWD_OFFS, b_ref)
        cz = jnp.concatenate([cz_ref[...], czn_ref[...]], axis=0)
        du_all = jnp.concatenate([du_ref[...], jnp.where(last, 0.0, dun_ref[...])], axis=0)
        xhat, rstd, s, ds, zc, dzc = _norm_act(a1_ref[...], cz, g_ref, bb_ref)
        dln = du_all * zc * ds
        dxhat = dln * g_ref[...]
        da1 = rstd * (dxhat - jnp.mean(dxhat, axis=-1, keepdims=True)
                      - xhat * jnp.mean(dxhat * xhat, axis=-1, keepdims=True))
        da1_ref[...] = da1
        dcz = (du_all * s * dzc)[:tt]
        dvec_ref[0:1, :] += jnp.sum(da1[:tt], axis=0, keepdims=True)
        dvec_ref[1:2, :] += jnp.sum((dln * xhat)[:tt], axis=0, keepdims=True)
        dvec_ref[2:3, :] += jnp.sum(dln[:tt], axis=0, keepdims=True)
        _tap_corr(dww_ref, tt, da1_ref, ext_ref, sh_ref, _FWD_OFFS)
        _shift_copies(da1_ref, sh_ref, ne - 8)
        _tap_conv(da0_ref, tt, da1_ref, sh_ref, w_ref, _BWD_OFFS)
        da0 = da0_ref[...]
        cv = cv_ref[...]
        dp_ref[:, 0:D] = (da0 * sig).astype(bf16)
        dp_ref[:, D:2 * D] = (da0 * cv * sig * (1.0 - sig)).astype(bf16)
        dp_ref[:, 2 * D:] = dcz.astype(bf16)

    def main(col):
        return pl.BlockSpec((tt, D), lambda i: (i, col))

    def prev(col):
        return pl.BlockSpec((HALO_C, D), lambda i: (jnp.maximum(i * hb - 1, 0), col))

    def nxt(col):
        return pl.BlockSpec((HALO_C, D), lambda i: (jnp.minimum((i + 1) * hb, last_hb), col))

    vec = pl.BlockSpec((1, D), lambda i: (0, 0))
    return pl.pallas_call(
        body, name="conv_bwd", grid=(n_tiles,),
        in_specs=[main(0), main(1), main(2), main(0), prev(0), prev(1), nxt(0), nxt(1), nxt(2), nxt(0),
                  pl.BlockSpec((HALO_C, D), lambda i: (0, 0)), vec, vec, vec],
        out_specs=[pl.BlockSpec((tt, 3 * D), lambda i: (i, 0)),
                   pl.BlockSpec((HALO_C, D), lambda i: (0, 0)),
                   pl.BlockSpec((8, D), lambda i: (0, 0))],
        out_shape=[jax.ShapeDtypeStruct((t, 3 * D), bf16), jax.ShapeDtypeStruct((HALO_C, D), f32),
                   jax.ShapeDtypeStruct((8, D), f32)],
        scratch_shapes=[pltpu.VMEM((ne + HALO_C, D), f32), pltpu.VMEM((N_SHIFT, ne + HALO_C - 8, D), f32),
                        pltpu.VMEM((ne, D), f32), pltpu.VMEM((ne, D), f32), pltpu.VMEM((tt, D), f32)],
        compiler_params=_cparams("arbitrary"),
    )(p_conv, p_conv, p_conv, du, p_conv, p_conv, p_conv, p_conv, p_conv, du, dw_w, dw_b, ln_g, ln_b)


def _dot_hi(a, b):
    return lax.dot_general(a, b, (((1,), (0,)), ((), ())), precision=HI, preferred_element_type=f32)


def _chunk_tri(n, lower):
    r = lax.broadcasted_iota(jnp.int32, (n, n), 0)
    c = lax.broadcasted_iota(jnp.int32, (n, n), 1)
    tri = (r >= c) if lower else (r <= c)
    return jnp.where(tri & (r // CH == c // CH), 1.0, 0.0).astype(f32)


def _softplus_and_sigmoid(x):
    e = jnp.exp(-jnp.abs(x))
    log1p = jnp.where(e < 1e-2, e * (1.0 - e * (0.5 - e * (1.0 / 3.0 - 0.25 * e))), jnp.log(1.0 + e))
    return jnp.maximum(x, 0.0) + log1p, _sigmoid(x)


_G_FWD_OFFS = [HALO_G - (KG - 1) + k for k in range(KG)]
_G_FWD_SHIFTS = (5, 6, 7)
_G_BWD_OFFS = [KG - 1 - k for k in range(KG)]
_G_BWD_SHIFTS = (1, 2, 3)


def _gdn_short_conv(pre_ref, ext_ref, sh_ref, n_rows, w_ref):
    _shift_copies(ext_ref, sh_ref, n_rows, _G_FWD_SHIFTS)
    _tap_conv(pre_ref, n_rows, ext_ref, sh_ref, w_ref, _G_FWD_OFFS, shifts=_G_FWD_SHIFTS)
    return pre_ref[...]


def _l2norm_heads(act, scale):
    outs, rs = [], []
    for h in range(NH):
        a = act[:, h * HD:(h + 1) * HD]
        r = lax.rsqrt(jnp.sum(a * a, axis=-1, keepdims=True) + L2_EPS)
        outs.append(a * (r * scale))
        rs.append(jnp.broadcast_to(r, a.shape))
    return jnp.concatenate(outs, axis=-1), jnp.concatenate(rs, axis=-1)


def _gate_math(ba, al_ref, dt_ref):
    lane = lax.broadcasted_iota(jnp.int32, ba.shape, 1)
    is_b = lane < NH
    is_a = (lane >= NH) & (lane < 2 * NH)
    sp, sg = _softplus_and_sigmoid(ba + dt_ref[...])
    neg_a = -jnp.exp(al_ref[...])
    return is_b, is_a, _sigmoid(ba), neg_a * sp, sg, neg_a


def _gdn_pre_fwd(p_qkv, p_ba, cw, alog_v, dt_v, *, tt):
    t = p_qkv.shape[0]
    hb = tt // HALO_G

    def body(q_ref, k_ref, v_ref, qh_ref, kh_ref, vh_ref, ba_ref, wq_ref, wk_ref, wv_ref, al_ref, dt_ref,
             qn_ref, kn_ref, va_ref, gt_ref, ext_ref, sh_ref, pre_ref):
        first = pl.program_id(0) == 0

        def conv_act(x_ref, xh_ref, w_ref):
            ext_ref[0:HALO_G, :] = jnp.where(first, 0.0, xh_ref[...])
            ext_ref[HALO_G:, :] = x_ref[...]
            pre = _gdn_short_conv(pre_ref, ext_ref, sh_ref, tt, w_ref)
            return pre * _sigmoid(pre)

        qn_ref[...] = _l2norm_heads(conv_act(q_ref, qh_ref, wq_ref), HD ** -0.5)[0]
        kn_ref[...] = _l2norm_heads(conv_act(k_ref, kh_ref, wk_ref), 1.0)[0]
        va_ref[...] = conv_act(v_ref, vh_ref, wv_ref)
        is_b, is_a, beta, g, _, _ = _gate_math(ba_ref[...], al_ref, dt_ref)
        gc = _dot_hi(_chunk_tri(tt, lower=True), jnp.where(is_a, g, 0.0))
        gt_ref[...] = jnp.where(is_b, beta, gc)

    def main(col):
        return pl.BlockSpec((tt, D), lambda i: (i, col))

    def prev(col):
        return pl.BlockSpec((HALO_G, D), lambda i: (jnp.maximum(i * hb - 1, 0), col))

    def wspec(col):
        return pl.BlockSpec((8, D), lambda i: (0, col))

    vec = pl.BlockSpec((1, HD), lambda i: (0, 0))
    gblk = pl.BlockSpec((tt, HD), lambda i: (i, 0))
    sds = jax.ShapeDtypeStruct((t, D), f32)
    return pl.pallas_call(
        body, name="gdn_pre_fwd", grid=(t // tt,),
        in_specs=[main(0), main(1), main(2), prev(0), prev(1), prev(2), gblk, wspec(0), wspec(1), wspec(2), vec, vec],
        out_specs=[pl.BlockSpec((tt, D), lambda i: (i, 0))] * 3 + [gblk],
        out_shape=[sds] * 3 + [jax.ShapeDtypeStruct((t, HD), f32)],
        scratch_shapes=[pltpu.VMEM((tt + HALO_G, D), f32), pltpu.VMEM((KG - 1, tt, D), f32), pltpu.VMEM((tt, D), f32)],
        compiler_params=_cparams("parallel"),
    )(p_qkv, p_qkv, p_qkv, p_qkv, p_qkv, p_qkv, p_ba, cw, cw, cw, alog_v, dt_v)


def _gdn_pre_bwd(p_qkv, p_ba, cw, alog_v, dt_v, dqn, dkn, dva, dgt, *, tt):
    t = p_qkv.shape[0]
    hb = tt // HALO_G
    n_tiles = t // tt
    last_hb = t // HALO_G - 1
    ne = tt + HALO_G

    def body(q_ref, k_ref, v_ref, qp_ref, kp_ref, vp_ref, qx_ref, kx_ref, vx_ref,
             dq_ref, dk_ref, dv_ref, dqx_ref, dkx_ref, dvx_ref, ba_ref, dgt_ref,
             wq_ref, wk_ref, wv_ref, al_ref, dt_ref,
             dp_ref, dba_ref, dcw_ref, dad_ref, ext_ref, sh_ref, pre_ref, dpre_ref, draw_ref):
        i = pl.program_id(0)
        first = i == 0
        last = i == n_tiles - 1

        @pl.when(first)
        def _():
            dcw_ref[...] = jnp.zeros_like(dcw_ref)
            dad_ref[...] = jnp.zeros_like(dad_ref)

        def one(x_ref, xp_ref, xx_ref, d_ref, dx_ref, w_ref, col, scale):
            ext_ref[0:HALO_G, :] = jnp.where(first, 0.0, xp_ref[...])
            ext_ref[HALO_G:HALO_G + tt, :] = x_ref[...]
            ext_ref[HALO_G + tt:, :] = xx_ref[...]
            pre = _gdn_short_conv(pre_ref, ext_ref, sh_ref, ne, w_ref)
            act, dact = _silu_and_grad(pre)
            d_out = jnp.concatenate([d_ref[...], jnp.where(last, 0.0, dx_ref[...])], axis=0)
            if scale is None:
                d_act = d_out
            else:
                parts = []
                for h in range(NH):
                    a = act[:, h * HD:(h + 1) * HD]
                    dn = d_out[:, h * HD:(h + 1) * HD]
                    r = lax.rsqrt(jnp.sum(a * a, axis=-1, keepdims=True) + L2_EPS)
                    parts.append(scale * r * (dn - a * (r * r) * jnp.sum(dn * a, axis=-1, keepdims=True)))
                d_act = jnp.concatenate(parts, axis=-1)
            dpre_ref[...] = d_act * dact
            _tap_corr(dcw_ref.at[:, col * D:(col + 1) * D], tt, dpre_ref, ext_ref, sh_ref, _G_FWD_OFFS, shifts=_G_FWD_SHIFTS)
            _shift_copies(dpre_ref, sh_ref, tt, _G_BWD_SHIFTS)
            _tap_conv(draw_ref, tt, dpre_ref, sh_ref, w_ref, _G_BWD_OFFS, shifts=_G_BWD_SHIFTS)
            dp_ref[:, col * D:(col + 1) * D] = draw_ref[...].astype(bf16)

        one(q_ref, qp_ref, qx_ref, dq_ref, dqx_ref, wq_ref, 0, HD ** -0.5)
        one(k_ref, kp_ref, kx_ref, dk_ref, dkx_ref, wk_ref, 1, 1.0)
        one(v_ref, vp_ref, vx_ref, dv_ref, dvx_ref, wv_ref, 2, None)

        is_b, is_a, beta, g, sg, neg_a = _gate_math(ba_ref[...], al_ref, dt_ref)
        dgt_v = dgt_ref[...]
        dg = _dot_hi(_chunk_tri(tt, lower=False), jnp.where(is_a, dgt_v, 0.0))
        d_al = jnp.where(is_a, dg * neg_a * sg, 0.0)
        dba_ref[...] = jnp.where(is_b, dgt_v * beta * (1.0 - beta), d_al)
        dad_ref[0:1, :] += jnp.sum(jnp.where(is_a, dg * g, 0.0), axis=0, keepdims=True)
        dad_ref[1:2, :] += jnp.sum(d_al, axis=0, keepdims=True)

    def main(col):
        return pl.BlockSpec((tt, D), lambda i: (i, col))

    def prev(col):
        return pl.BlockSpec((HALO_G, D), lambda i: (jnp.maximum(i * hb - 1, 0), col))

    def nxt(col):
        return pl.BlockSpec((HALO_G, D), lambda i: (jnp.minimum((i + 1) * hb, last_hb), col))

    def wspec(col):
        return pl.BlockSpec((8, D), lambda i: (0, col))

    vec = pl.BlockSpec((1, HD), lambda i: (0, 0))
    gblk = pl.BlockSpec((tt, HD), lambda i: (i, 0))
    return pl.pallas_call(
        body, name="gdn_pre_bwd", grid=(n_tiles,),
        in_specs=[main(0), main(1), main(2), prev(0), prev(1), prev(2), nxt(0), nxt(1), nxt(2),
                  main(0), main(0), main(0), nxt(0), nxt(0), nxt(0), gblk, gblk,
                  wspec(0), wspec(1), wspec(2), vec, vec],
        out_specs=[pl.BlockSpec((tt, 3 * D), lambda i: (i, 0)), pl.BlockSpec((tt, HD), lambda i: (i, 0)),
                   pl.BlockSpec((8, 3 * D), lambda i: (0, 0)), pl.BlockSpec((8, HD), lambda i: (0, 0))],
        out_shape=[jax.ShapeDtypeStruct((t, 3 * D), bf16), jax.ShapeDtypeStruct((t, HD), f32),
                   jax.ShapeDtypeStruct((8, 3 * D), f32), jax.ShapeDtypeStruct((8, HD), f32)],
        scratch_shapes=[pltpu.VMEM((HALO_G + tt + HALO_G, D), f32), pltpu.VMEM((KG - 1, ne, D), f32),
                        pltpu.VMEM((ne, D), f32), pltpu.VMEM((ne, D), f32), pltpu.VMEM((tt, D), f32)],
        compiler_params=_cparams("arbitrary"),
    )(p_qkv, p_qkv, p_qkv, p_qkv, p_qkv, p_qkv, p_qkv, p_qkv, p_qkv,
      dqn, dkn, dva, dqn, dkn, dva, p_ba, dgt, cw, cw, cw, alog_v, dt_v)


def _dot_b(a, b, dims):
    return lax.dot_general(a.astype(bf16), b.astype(bf16), (dims, ((), ())), preferred_element_type=f32)


def _inverse_by_doubling(ms):
    heads = range(len(ms))
    r = lax.broadcasted_iota(jnp.int32, (CH, CH), 0)
    c = lax.broadcasted_iota(jnp.int32, (CH, CH), 1)
    eye = jnp.where(r == c, 1.0, 0.0).astype(f32)
    p = [eye + ms[h] for h in heads]
    mp = ms
    for _ in range(5):
        mp = [_dot_b(mp[h], mp[h], _NN) for h in heads]
        pm = [_dot_b(p[h], mp[h], _NN) for h in heads]
        p = [p[h] + pm[h] for h in heads]
    return tuple(p)


@jax.custom_vjp
def _known_inverse(ms, ps):
    return ps


def _known_inverse_fwd(ms, ps):
    return ps, ps


def _known_inverse_bwd(ps, cts):
    heads = range(len(ps))
    left = [_dot_b(ps[h], cts[h], _TN) for h in heads]
    return tuple(_dot_b(left[h], ps[h], _NT) for h in heads), tuple(jnp.zeros_like(p) for p in ps)


_known_inverse.defvjp(_known_inverse_fwd, _known_inverse_bwd)


def _chunk_fn(qs, ks, vs, gcs, bbs, ss, ps=None):
    heads = range(len(qs))
    r = lax.broadcasted_iota(jnp.int32, (CH, CH), 0)
    c = lax.broadcasted_iota(jnp.int32, (CH, CH), 1)
    causal = r >= c
    strict = r > c
    gc_row = [gcs[h].T[:CH, :] for h in heads]
    decay = [jnp.where(causal, jnp.exp(jnp.where(causal, gcs[h][:, :CH] - gc_row[h], 0.0)), 0.0) for h in heads]
    kb = [ks[h] * bbs[h] for h in heads]
    egc = [jnp.exp(gcs[h]) for h in heads]
    kk = [_dot_b(kb[h], ks[h], _NT) for h in heads]
    qk = [_dot_b(qs[h], ks[h], _NT) for h in heads]
    m = tuple(-jnp.where(strict, kk[h] * decay[h], 0.0) for h in heads)
    p = _inverse_by_doubling(m) if ps is None else _known_inverse(m, ps)
    u = [_dot_b(p[h], vs[h] * bbs[h], _NN) for h in heads]
    w = [_dot_b(p[h], kb[h] * egc[h], _NN) for h in heads]
    intra = [jnp.where(causal, qk[h] * decay[h], 0.0) for h in heads]
    g_last = [gcs[h][CH - 1:CH, :] for h in heads]
    k_dec = [ks[h] * jnp.exp(g_last[h] - gcs[h]) for h in heads]
    ws = [_dot_b(w[h], ss[h], _NN) for h in heads]
    qs_s = [_dot_b(qs[h] * egc[h], ss[h], _NN) for h in heads]
    v_new = [u[h] - ws[h] for h in heads]
    iv = [_dot_b(intra[h], v_new[h], _NN) for h in heads]
    kv = [_dot_b(k_dec[h], v_new[h], _TN) for h in heads]
    o = tuple(qs_s[h] + iv[h] for h in heads)
    s_new = tuple(ss[h] * jnp.exp(g_last[h]) + kv[h] for h in heads)
    return o, s_new, p


def _head_cols():
    return [slice(h * HD, (h + 1) * HD) for h in range(NH)]


def _head_gates(gt):
    gcs = tuple(jnp.broadcast_to(gt[:, NH + h:NH + h + 1], (CH, HD)) for h in range(NH))
    bbs = tuple(jnp.broadcast_to(gt[:, h:h + 1], (CH, HD)) for h in range(NH))
    return gcs, bbs


def _gdn_scan_fwd(qn, kn, va, gates, *, tt):
    t = qn.shape[0]
    cpb = tt // CH

    def body(q_ref, k_ref, v_ref, gt_ref, o_ref, st_ref, p_ref, s_scr):
        @pl.when(pl.program_id(0) == 0)
        def _():
            s_scr[...] = jnp.zeros_like(s_scr)

        def step(ci, carry):
            rows = pl.ds(pl.multiple_of(ci * CH, CH), CH)
            cols = _head_cols()
            ss = tuple(s_scr[h] for h in range(NH))
            for h in range(NH):
                st_ref[ci, h] = ss[h]
            gcs, bbs = _head_gates(gt_ref[rows, :])
            o, s_new, p = _chunk_fn(*(tuple(ref[rows, cl] for cl in cols) for ref in (q_ref, k_ref, v_ref)), gcs, bbs, ss)
            for h in range(NH):
                o_ref[rows, cols[h]] = o[h]
                s_scr[h] = s_new[h]
                p_ref[ci, h] = p[h].astype(bf16)
            return carry

        lax.fori_loop(0, cpb, step, 0)

    blk = pl.BlockSpec((tt, D), lambda i: (i, 0))
    return pl.pallas_call(
        body, name="gdn_scan_fwd", grid=(t // tt,),
        in_specs=[blk] * 3 + [pl.BlockSpec((tt, HD), lambda i: (i, 0))],
        out_specs=[blk, pl.BlockSpec((cpb, NH, HD, HD), lambda i: (i, 0, 0, 0)),
                   pl.BlockSpec((cpb, NH, CH, CH), lambda i: (i, 0, 0, 0))],
        out_shape=[jax.ShapeDtypeStruct((t, D), f32), jax.ShapeDtypeStruct((t // CH, NH, HD, HD), f32),
                   jax.ShapeDtypeStruct((t // CH, NH, CH, CH), bf16)],
        scratch_shapes=[pltpu.VMEM((NH, HD, HD), f32)],
        compiler_params=_cparams("arbitrary"),
    )(qn, kn, va, gates)


def _gdn_scan_bwd(qn, kn, va, gates, states, inverses, do, *, tt):
    t = qn.shape[0]
    nblk = t // tt
    cpb = tt // CH

    def body(q_ref, k_ref, v_ref, gt_ref, st_ref, p_ref, do_ref, dq_ref, dk_ref, dv_ref, dgt_ref, ds_scr):
        @pl.when(pl.program_id(0) == 0)
        def _():
            ds_scr[...] = jnp.zeros_like(ds_scr)

        def step(j, carry):
            ci = cpb - 1 - j
            rows = pl.ds(pl.multiple_of(ci * CH, CH), CH)
            cols = _head_cols()
            gcs, bbs = _head_gates(gt_ref[rows, :])
            ps = tuple(p_ref[ci, h].astype(f32) for h in range(NH))
            _, vjp = jax.vjp(lambda *a: _chunk_fn(*a, ps=ps)[:2],
                             *(tuple(ref[rows, cl] for cl in cols) for ref in (q_ref, k_ref, v_ref)),
                             gcs, bbs, tuple(st_ref[ci, h] for h in range(NH)))
            grads = vjp((tuple(do_ref[rows, cl] for cl in cols), tuple(ds_scr[h] for h in range(NH))))
            lane = lax.broadcasted_iota(jnp.int32, (CH, HD), 1)
            dgt = jnp.zeros((CH, HD), f32)
            for h in range(NH):
                for ref, g in zip((dq_ref, dk_ref, dv_ref), grads[:3]):
                    ref[rows, cols[h]] = g[h]
                dgt = dgt + jnp.where(lane == NH + h, jnp.sum(grads[3][h], axis=-1, keepdims=True), 0.0)
                dgt = dgt + jnp.where(lane == h, jnp.sum(grads[4][h], axis=-1, keepdims=True), 0.0)
                ds_scr[h] = grads[5][h]
            dgt_ref[rows, :] = dgt
            return carry

        lax.fori_loop(0, cpb, step, 0)

    blk = pl.BlockSpec((tt, D), lambda i: (nblk - 1 - i, 0))
    sblk = pl.BlockSpec((cpb, NH, HD, HD), lambda i: (nblk - 1 - i, 0, 0, 0))
    sds = jax.ShapeDtypeStruct((t, D), f32)
    gblk = pl.BlockSpec((tt, HD), lambda i: (nblk - 1 - i, 0))
    pblk = pl.BlockSpec((cpb, NH, CH, CH), lambda i: (nblk - 1 - i, 0, 0, 0))
    return pl.pallas_call(
        body, name="gdn_scan_bwd", grid=(nblk,),
        in_specs=[blk] * 3 + [gblk, sblk, pblk, blk],
        out_specs=[blk] * 3 + [gblk], out_shape=[sds] * 3 + [jax.ShapeDtypeStruct((t, HD), f32)],
        scratch_shapes=[pltpu.VMEM((NH, HD, HD), f32)],
        compiler_params=_cparams("arbitrary"),
    )(qn, kn, va, gates, states, inverses, do)


def _rms_heads(o):
    ons, rs = [], []
    for h in range(NH):
        a = o[:, h * HD:(h + 1) * HD]
        r = lax.rsqrt(jnp.mean(a * a, axis=-1, keepdims=True) + RMS_EPS)
        ons.append(a * r)
        rs.append(jnp.broadcast_to(r, a.shape))
    return jnp.concatenate(ons, axis=-1), jnp.concatenate(rs, axis=-1)


def _gdn_post_fwd(o, p_gz, ng_b, *, tt):
    t = o.shape[0]

    def body(o_ref, gz_ref, ng_ref, og_ref):
        on, _ = _rms_heads(o_ref[...])
        z, _ = _silu_and_grad(gz_ref[...])
        og_ref[...] = (on * ng_ref[...] * z).astype(bf16)

    blk = pl.BlockSpec((tt, D), lambda i: (i, 0))
    return pl.pallas_call(
        body, name="gdn_post_fwd", grid=(t // tt,),
        in_specs=[blk, blk, pl.BlockSpec((1, D), lambda i: (0, 0))],
        out_specs=blk, out_shape=jax.ShapeDtypeStruct((t, D), bf16),
        compiler_params=_cparams("parallel"),
    )(o, p_gz, ng_b)


def _gdn_post_bwd(o, p_gz, ng_b, dog, *, tt):
    t = o.shape[0]

    def body(o_ref, gz_ref, ng_ref, dog_ref, do_ref, dgz_ref, dng_ref):
        @pl.when(pl.program_id(0) == 0)
        def _():
            dng_ref[...] = jnp.zeros_like(dng_ref)

        on, r = _rms_heads(o_ref[...])
        z, dz = _silu_and_grad(gz_ref[...])
        dog_v = dog_ref[...]
        ng = ng_ref[...]
        dgz_ref[...] = (dog_v * on * ng * dz).astype(bf16)
        dy = dog_v * z
        dng_all = jnp.sum(dy * on, axis=0, keepdims=True)
        dng = dng_all[:, 0:HD]
        for h in range(1, NH):
            dng = dng + dng_all[:, h * HD:(h + 1) * HD]
        dng_ref[0:1, :] += dng
        don = dy * ng
        prod = don * on
        parts = []
        for h in range(NH):
            sl = slice(h * HD, (h + 1) * HD)
            parts.append(don[:, sl] - on[:, sl] * jnp.mean(prod[:, sl], axis=-1, keepdims=True))
        do_ref[...] = r * jnp.concatenate(parts, axis=-1)

    blk = pl.BlockSpec((tt, D), lambda i: (i, 0))
    return pl.pallas_call(
        body, name="gdn_post_bwd", grid=(t // tt,),
        in_specs=[blk, blk, pl.BlockSpec((1, D), lambda i: (0, 0)), blk],
        out_specs=[blk, blk, pl.BlockSpec((8, HD), lambda i: (0, 0))],
        out_shape=[jax.ShapeDtypeStruct((t, D), f32), jax.ShapeDtypeStruct((t, D), bf16),
                   jax.ShapeDtypeStruct((8, HD), f32)],
        compiler_params=_cparams("arbitrary"),
    )(o, p_gz, ng_b, dog)


def _merge(x, y_conf, y_gdn, p_gates, target, w_o, ln_g, ln_b, *, tt):
    t = x.shape[0]

    def body(x_ref, yc_ref, yg_ref, gc_ref, gg_ref, tg_ref, w_ref, g_ref, b_ref,
             loss_ref, dxd_ref, dyc_ref, dyg_ref, dpg_ref, h_ref, dz_ref, dvec_ref):
        @pl.when(pl.program_id(0) == 0)
        def _():
            loss_ref[...] = jnp.zeros_like(loss_ref)
            dvec_ref[...] = jnp.zeros_like(dvec_ref)

        sc = _sigmoid(gc_ref[...])
        sg = _sigmoid(gg_ref[...])
        yc = yc_ref[...]
        yg = yg_ref[...]
        h = (sc * yc + sg * yg).astype(bf16)
        h_ref[...] = h
        z = DN_ALPHA * x_ref[...] + jnp.dot(h, w_ref[...], preferred_element_type=f32)
        mu = jnp.mean(z, axis=-1, keepdims=True)
        cen = z - mu
        rstd = lax.rsqrt(jnp.mean(cen * cen, axis=-1, keepdims=True) + LN_EPS)
        xhat = cen * rstd
        err = xhat * g_ref[...] + b_ref[...] - tg_ref[...]
        loss_ref[...] += 0.5 / D * jnp.sum(err * err)
        dy = err * (1.0 / D)
        dvec_ref[0:1, :] += jnp.sum(dy * xhat, axis=0, keepdims=True)
        dvec_ref[1:2, :] += jnp.sum(dy, axis=0, keepdims=True)
        dxhat = dy * g_ref[...]
        dz = rstd * (dxhat - jnp.mean(dxhat, axis=-1, keepdims=True)
                     - xhat * jnp.mean(dxhat * xhat, axis=-1, keepdims=True))
        dxd_ref[...] = DN_ALPHA * dz
        dz_b = dz.astype(bf16)
        dz_ref[...] = dz_b
        dh = lax.dot_general(dz_b, w_ref[...], (_NT, ((), ())), preferred_element_type=f32)
        dyc_ref[...] = (dh * sc).astype(bf16)
        dyg_ref[...] = (dh * sg).astype(bf16)
        dpg_ref[:, 0:D] = (dh * yc * sc * (1.0 - sc)).astype(bf16)
        dpg_ref[:, D:] = (dh * yg * sg * (1.0 - sg)).astype(bf16)

    blk = pl.BlockSpec((tt, D), lambda i: (i, 0))
    wblk = pl.BlockSpec((D, D), lambda i: (0, 0))
    vec = pl.BlockSpec((1, D), lambda i: (0, 0))
    return pl.pallas_call(
        body, name="merge_norm_loss", grid=(t // tt,),
        in_specs=[blk, blk, blk, pl.BlockSpec((tt, D), lambda i: (i, 0)), pl.BlockSpec((tt, D), lambda i: (i, 1)),
                  blk, wblk, vec, vec],
        out_specs=[pl.BlockSpec((8, HD), lambda i: (0, 0)), blk, blk, blk,
                   pl.BlockSpec((tt, 2 * D), lambda i: (i, 0)), blk, blk, pl.BlockSpec((8, D), lambda i: (0, 0))],
        out_shape=[jax.ShapeDtypeStruct((8, HD), f32), jax.ShapeDtypeStruct((t, D), f32),
                   jax.ShapeDtypeStruct((t, D), bf16), jax.ShapeDtypeStruct((t, D), bf16),
                   jax.ShapeDtypeStruct((t, 2 * D), bf16), jax.ShapeDtypeStruct((t, D), bf16),
                   jax.ShapeDtypeStruct((t, D), bf16), jax.ShapeDtypeStruct((8, D), f32)],
        compiler_params=_cparams("arbitrary"),
    )(x, y_conf, y_gdn, p_gates, p_gates, target, w_o, ln_g, ln_b)


def _place():
    return lax.axis_index("x"), lax.axis_index("y"), lax.axis_index("c")


def _any_specs(n):
    return [pl.BlockSpec(memory_space=pl.ANY)] * n


def _sibling_merge(arrs, name, take_other_half=False):
    k = len(arrs)

    def body(*refs):
        a_refs, o_refs = refs[:k], refs[k:2 * k]
        send_sems, recv_sems = refs[2 * k:]
        x, y, c = _place()
        sends = []
        for i in range(k):
            src = a_refs[i]
            if take_other_half:
                n = a_refs[i].shape[-2] // 2
                lead = (slice(None),) * (len(a_refs[i].shape) - 2)
                src = a_refs[i].at[lead + (pl.ds((1 - c) * n, n), slice(None))]
            cp = pltpu.make_async_remote_copy(src_ref=src, dst_ref=o_refs[i], send_sem=send_sems.at[i],
                                              recv_sem=recv_sems.at[i], device_id=(x, y, 1 - c), device_id_type=MESH)
            cp.start()
            sends.append(cp)
        for cp in sends:
            cp.wait()

    def out_sds(a):
        rows = a.shape[-2] // 2 if take_other_half else a.shape[-2]
        return jax.ShapeDtypeStruct(a.shape[:-2] + (rows, a.shape[-1]), a.dtype)

    return pl.pallas_call(
        body, name=name, in_specs=_any_specs(k), out_specs=_any_specs(k),
        out_shape=[out_sds(a) for a in arrs],
        scratch_shapes=[pltpu.SemaphoreType.DMA((k,)), pltpu.SemaphoreType.DMA((k,))],
    )(*arrs)


def _join_halves(mine, other):
    c = lax.axis_index("c")
    return jnp.concatenate([jnp.where(c == 0, mine, other), jnp.where(c == 0, other, mine)], axis=-2)


def _chip_exchange_ops(a_refs, o_refs, send_sems, recv_sems, local_sems, scatter):
    k = len(a_refs)
    x, y, c = _place()
    me = 2 * x + y
    peers = [(1 - x, y), (x, 1 - y), (1 - x, 1 - y)]

    def src(i, j):
        return a_refs[i].at[j] if scatter else a_refs[i]

    def copy(i, n, send_j, slot):
        px, py = peers[n]
        return pltpu.make_async_remote_copy(
            src_ref=src(i, send_j), dst_ref=o_refs[i].at[slot], send_sem=send_sems.at[3 * i + n],
            recv_sem=recv_sems.at[3 * i + n], device_id=(px, py, c), device_id_type=MESH)

    def owns():
        return [pltpu.make_async_copy(src(i, me), o_refs[i].at[me], local_sems.at[i]) for i in range(k)]

    def sends():
        return [copy(i, n, 2 * peers[n][0] + peers[n][1], me) for n in range(3) for i in range(k)]

    def start():
        for cp in owns() + sends():
            cp.start()

    def finish():
        for n in range(3):
            for i in range(k):
                copy(i, n, me, 2 * peers[n][0] + peers[n][1]).wait_recv()
        for cp in sends():
            cp.wait_send()
        for cp in owns():
            cp.wait()

    return start, finish


def _chip_exchange_shapes(arrs, scatter):
    return [jax.ShapeDtypeStruct((N_CHIPS,) + tuple(a.shape[1:] if scatter else a.shape), a.dtype) for a in arrs]


def _chip_exchange_sems(k):
    return [pltpu.SemaphoreType.DMA((3 * k,)), pltpu.SemaphoreType.DMA((3 * k,)), pltpu.SemaphoreType.DMA((k,))]


def _chip_exchange(arrs, name, scatter):
    k = len(arrs)

    def body(*refs):
        start, finish = _chip_exchange_ops(refs[:k], refs[k:2 * k], *refs[2 * k:], scatter)
        start()
        finish()

    return pl.pallas_call(
        body, name=name, in_specs=_any_specs(k), out_specs=_any_specs(k),
        out_shape=_chip_exchange_shapes(arrs, scatter), scratch_shapes=_chip_exchange_sems(k),
    )(*arrs)


def _pair_sum(g_all, got, c_arr, name, out_dtype):
    n, w = got.shape[1:]
    tile = n // 4
    n_tiles = n // tile

    def body(c_ref, a_ref, b_ref, o_ref):
        o_ref[...] = (a_ref[...] + b_ref[...]).astype(out_dtype)

    return pl.pallas_call(
        body, name=name,
        grid_spec=pltpu.PrefetchScalarGridSpec(
            num_scalar_prefetch=1, grid=(N_CHIPS, n_tiles),
            in_specs=[pl.BlockSpec((1, tile, w), lambda j, i, c_ref: (j, c_ref[0] * n_tiles + i, 0)),
                      pl.BlockSpec((1, tile, w), lambda j, i, c_ref: (j, i, 0))],
            out_specs=pl.BlockSpec((1, tile, w), lambda j, i, c_ref: (j, i, 0))),
        out_shape=jax.ShapeDtypeStruct(got.shape, out_dtype),
        compiler_params=_cparams("parallel", "parallel"),
    )(c_arr, g_all, got)


def _sum_slots(a, name):
    n, w = a.shape[1:]
    tile = n // 4

    def body(a_ref, o_ref):
        o_ref[...] = ((a_ref[0].astype(f32) + a_ref[1].astype(f32)) + a_ref[2].astype(f32)) + a_ref[3].astype(f32)

    return pl.pallas_call(
        body, name=name, grid=(n // tile,),
        in_specs=[pl.BlockSpec((N_CHIPS, tile, w), lambda i: (0, i, 0))],
        out_specs=pl.BlockSpec((tile, w), lambda i: (i, 0)),
        out_shape=jax.ShapeDtypeStruct((n, w), f32),
        compiler_params=_cparams("parallel"),
    )(a)


def _adamw(w, g, m, v, name):
    rows, width = w.shape
    tile = rows // 8
    c1 = 1.0 / (1.0 - ADAM_B1 ** ADAM_STEP)
    c2 = 1.0 / (1.0 - ADAM_B2 ** ADAM_STEP)

    def body(w_ref, g_ref, m_ref, v_ref, d_ref, mo_ref, vo_ref):
        g_v = g_ref[...]
        m_new = ADAM_B1 * m_ref[...] + (1.0 - ADAM_B1) * g_v
        v_new = ADAM_B2 * v_ref[...] + (1.0 - ADAM_B2) * (g_v * g_v)
        mo_ref[...] = m_new
        vo_ref[...] = v_new
        d_ref[...] = -ADAM_LR * ((m_new * c1) / (jnp.sqrt(v_new * c2) + ADAM_EPS) + ADAM_WD * w_ref[...])

    blk = pl.BlockSpec((tile, width), lambda i: (i, 0))
    sds = jax.ShapeDtypeStruct((rows, width), f32)
    return pl.pallas_call(
        body, name=name, grid=(rows // tile,),
        in_specs=[blk] * 4, out_specs=[blk] * 3, out_shape=[sds] * 3,
        compiler_params=_cparams("parallel"),
    )(w, g, m, v)


R_DW = 3 * SQ_BLK
R_CW = R_DW + 8
R_VEC = R_CW + 8
R_SMALL = R_VEC + 8
REST_ROWS = 896


def _pack_small(conf_dw_w, gdn_conv_w, vecs, a_log, dt_bias, norm_g):
    dw = jnp.pad(conf_dw_w.reshape(-1), (0, 8 * D - KC * SQ_BLK)).reshape(8, D)
    cw = jnp.pad(gdn_conv_w.reshape(-1), (0, 5 * D)).reshape(8, D)
    vec = jnp.pad(jnp.stack(vecs), ((0, 3), (0, 0)))
    small = jnp.pad(jnp.concatenate([a_log, dt_bias, norm_g]), (0, D - 2 * NH - HD)).reshape(1, D)
    return jnp.pad(jnp.concatenate([dw, cw, vec, small], axis=0), ((0, REST_ROWS - R_SMALL - 1), (0, 0)))


def _pack_rest(conf_w_out, gdn_w_out, w_o, small):
    return jnp.concatenate([conf_w_out, gdn_w_out, w_o, small], axis=0)


def _unpack_rest(p):
    conf_dw_w = p[R_DW:R_DW + 8].reshape(-1)[:KC * SQ_BLK].reshape(KC, SQ_BLK)
    gdn_conv_w = p[R_CW:R_CW + 3].reshape(KG, 3 * SQ_BLK)
    small = p[R_SMALL]
    return dict(conf_w_out=p[0:SQ_BLK], gdn_w_out=p[SQ_BLK:2 * SQ_BLK], w_o=p[2 * SQ_BLK:R_DW],
                conf_dw_w=conf_dw_w, gdn_conv_w=gdn_conv_w, conf_dw_b=p[R_VEC], conf_ln_g=p[R_VEC + 1],
                conf_ln_b=p[R_VEC + 2], post_ln_g=p[R_VEC + 3], post_ln_b=p[R_VEC + 4],
                gdn_A_log=small[0:NH], gdn_dt_bias=small[NH:2 * NH], gdn_norm_g=small[2 * NH:2 * NH + HD])


_WEIGHT_ORDER = ("w_in", "conf_dw_w", "conf_dw_b", "conf_ln_g", "conf_ln_b", "conf_w_out", "gdn_conv_w",
                 "gdn_A_log", "gdn_dt_bias", "gdn_norm_g", "gdn_w_out", "w_o", "post_ln_g", "post_ln_b")


def _gather_weights(w_in, conf_w_out, gdn_w_out, w_o, conf_dw_w, gdn_conv_w):
    c = lax.axis_index("c")
    sq = jnp.concatenate([conf_w_out, gdn_w_out, w_o], axis=0).astype(bf16)
    w_half = lax.dynamic_slice_in_dim(w_in.astype(bf16), c * (D // 2), D // 2, axis=0)
    sq_half = lax.dynamic_slice_in_dim(sq, c * (sq.shape[0] // 2), sq.shape[0] // 2, axis=0)
    small = jnp.concatenate([jnp.pad(conf_dw_w.reshape(-1), (0, 8 * D - KC * SQ_BLK)).reshape(8, D),
                             jnp.pad(gdn_conv_w.reshape(-1), (0, 5 * D)).reshape(8, D)], axis=0)
    got_w, got_sq, small_all = _chip_exchange([w_half, sq_half, small], "weight_gather_chips", scatter=False)
    oth_w, oth_sq = _sibling_merge([got_w, got_sq], "weight_gather_sibling")
    w4 = _join_halves(got_w, oth_w)
    sq4 = _join_halves(got_sq, oth_sq)
    sq_full = [sq4[:, n * SQ_BLK:(n + 1) * SQ_BLK].reshape(D, D) for n in range(3)]
    dw_full = small_all[:, 0:8].reshape(N_CHIPS, 8 * D)[:, :KC * SQ_BLK].reshape(N_CHIPS, KC, SQ_BLK)
    dw_full = dw_full.transpose(1, 0, 2).reshape(KC, D)
    cw_full = small_all[:, 8:11].reshape(N_CHIPS, KG, 3 * SQ_BLK).transpose(1, 0, 2).reshape(KG, 3 * D)
    return w4, sq_full[0], sq_full[1], sq_full[2], dw_full, cw_full


def _w_in_cols(w4, lo, hi):
    parts = []
    for j in range(N_CHIPS):
        a, b = max(lo, j * W_IN_BLK), min(hi, (j + 1) * W_IN_BLK)
        if a < b:
            parts.append(w4[j, :, a - j * W_IN_BLK:b - j * W_IN_BLK])
    return parts[0] if len(parts) == 1 else jnp.concatenate(parts, axis=1)


def _w_in_by_chip(pieces):
    chips = []
    for j in range(N_CHIPS):
        lo, hi = j * W_IN_BLK, (j + 1) * W_IN_BLK
        parts = []
        for start, arr in pieces:
            a, b = max(lo, start), min(hi, start + arr.shape[1])
            if a < b:
                parts.append(arr[:, a - start:b - start])
        chips.append(jnp.concatenate(parts, axis=1))
    return jnp.stack(chips)


def _pair_sums(g_w, g_rest):
    c_arr = lax.axis_index("c").astype(jnp.int32).reshape(1)
    got_w, got_r = _sibling_merge([g_w, g_rest], "grad_sibling_halves", take_other_half=True)
    pair_w = _pair_sum(g_w, got_w, c_arr, "grad_pair_sum_w_in", bf16)
    pair_r = _pair_sum(g_rest, got_r, c_arr, "grad_pair_sum_rest", f32)
    return pair_w, pair_r


def _chip_sums(all_w, all_r):
    tot_w, tot_r = _sum_slots(all_w, "grad_chip_sum_w_in"), _sum_slots(all_r, "grad_chip_sum_rest")
    oth_w, oth_r = _sibling_merge([tot_w, tot_r], "grad_sibling_result")
    return _join_halves(tot_w, oth_w), _join_halves(tot_r, oth_r)


def _local_step(x2, tgt, w4, wc_out, wg_out, wo_full, dw_full, cw_full, conf_dw_b, conf_ln_g, conf_ln_b,
                gdn_A_log, gdn_dt_bias, gdn_norm_g, post_ln_g, post_ln_b):
    t = x2.shape[0]
    tt = min(TOKEN_TILE, t)
    tm = min(512, t)

    w_conv, w_qkv, w_gz = _w_in_cols(w4, 0, 3 * D), _w_in_cols(w4, 3 * D, 6 * D), _w_in_cols(w4, 6 * D, 7 * D)
    w_gates = _w_in_cols(w4, 7 * D + 2 * NH, W_IN_COLS)
    dw_pad = jnp.pad(dw_full, ((0, HALO_C - KC), (0, 0)))
    cw_pad = jnp.pad(cw_full, ((0, 8 - KG), (0, 0)))
    row = lambda v: v.reshape(1, D)
    alog_v = jnp.pad(gdn_A_log, (NH, HD - 2 * NH)).reshape(1, HD)
    dt_v = jnp.pad(gdn_dt_bias, (NH, HD - 2 * NH)).reshape(1, HD)
    ng_b = row(jnp.tile(gdn_norm_g, NH))
    w_ba = jnp.pad(_w_in_cols(w4, 7 * D, 7 * D + 2 * NH), ((0, 0), (0, HD - 2 * NH)))

    x_b = x2.astype(bf16)

    p_conv = _mm_multi([x_b], [w_conv], out_dtype=f32, tm=tm, tn=1024, name="proj_conv")
    p_qkv = _mm_multi([x_b], [w_qkv], out_dtype=f32, tm=tm, tn=1024, name="proj_qkv")
    p_gz = _mm_multi([x_b], [w_gz], out_dtype=f32, tm=tm, tn=1024, name="proj_gz")
    p_gates = _mm_multi([x_b], [w_gates], out_dtype=f32, tm=tm, tn=1024, name="proj_gates")
    p_ba = _mm_multi([x_b], [w_ba], out_dtype=f32, tm=tm, tn=HD, name="proj_ba")

    u = _conv_fwd(p_conv, dw_pad, row(conf_dw_b), row(conf_ln_g), row(conf_ln_b), tt=tt)
    y_conf = _mm_multi([u], [wc_out], out_dtype=f32, tm=tm, tn=1024, name="conf_out")

    qn, kn, va, gates = _gdn_pre_fwd(p_qkv, p_ba, cw_pad, alog_v, dt_v, tt=tt)
    o, states, inverses = _gdn_scan_fwd(qn, kn, va, gates, tt=tt)
    og = _gdn_post_fwd(o, p_gz, ng_b, tt=tt)
    y_gdn = _mm_multi([og], [wg_out], out_dtype=f32, tm=tm, tn=1024, name="gdn_out")

    loss_blk, dxd, dyc, dyg, dp_gates, h, dz, dpost = _merge(
        x2, y_conf, y_gdn, p_gates, tgt, wo_full, row(post_ln_g), row(post_ln_b), tt=tt)

    d_wo = _mm_kloop(h, dz, tm=D, tn=1024, tk=min(512, t), name="grad_w_o")
    du = _mm_multi([dyc], [wc_out], out_dtype=f32, tm=tm, tn=1024, name="conf_out_bwd", rhs_t=True)
    d_wc = _mm_kloop(u, dyc, tm=D, tn=1024, tk=min(512, t), name="grad_conf_w_out")
    dog = _mm_multi([dyg], [wg_out], out_dtype=f32, tm=tm, tn=1024, name="gdn_out_bwd", rhs_t=True)
    d_wg = _mm_kloop(og, dyg, tm=D, tn=1024, tk=min(512, t), name="grad_gdn_w_out")

    dp_conv, d_dww, dconv_vec = _conv_bwd(p_conv, du, dw_pad, row(conf_dw_b), row(conf_ln_g), row(conf_ln_b), tt=tt)

    do, dp_gz, dng = _gdn_post_bwd(o, p_gz, ng_b, dog, tt=tt)
    dqn, dkn, dva, dgates = _gdn_scan_bwd(qn, kn, va, gates, states, inverses, do, tt=tt)
    dp_qkv, dp_ba, d_cw, d_ad = _gdn_pre_bwd(p_qkv, p_ba, cw_pad, alog_v, dt_v, dqn, dkn, dva, dgates, tt=tt)
    dp_ba_b = dp_ba.astype(bf16)

    grad_x_factors = ([dp_conv, dp_qkv, dp_gz, dp_gates, dp_ba_b], [w_conv, w_qkv, w_gz, w_gates, w_ba], dxd)

    tk = min(512, t)
    d_w_conv = _mm_kloop(x_b, dp_conv, tm=D, tn=1024, tk=tk, name="grad_w_in_conv")
    d_w_qkv = _mm_kloop(x_b, dp_qkv, tm=D, tn=1024, tk=tk, name="grad_w_in_qkv")
    d_w_gz = _mm_kloop(x_b, dp_gz, tm=D, tn=1024, tk=tk, name="grad_w_in_gz")
    d_w_gates = _mm_kloop(x_b, dp_gates, tm=D, tn=1024, tk=tk, name="grad_w_in_gates")
    d_w_ba = _mm_kloop(x_b, dp_ba_b, tm=D, tn=HD, tk=tk, name="grad_w_in_ba")
    d_w_in = _w_in_by_chip([(0, d_w_conv), (3 * D, d_w_qkv), (6 * D, d_w_gz), (7 * D, d_w_ba[:, :2 * NH]),
                            (7 * D + 2 * NH, d_w_gates)])

    return (loss_blk[0, 0], grad_x_factors, d_w_in, d_wc, d_wg, d_wo, d_dww, d_cw, dconv_vec, dpost, d_ad, dng)


def kernel(x, w_in, conf_dw_w, conf_dw_b, conf_ln_g, conf_ln_b, conf_w_out, gdn_conv_w, gdn_A_log, gdn_dt_bias, gdn_norm_g, gdn_w_out, w_o, post_ln_g, post_ln_b, loss_target, m_w_in, m_conf_dw_w, m_conf_dw_b, m_conf_ln_g, m_conf_ln_b, m_conf_w_out, m_gdn_conv_w, m_gdn_A_log, m_gdn_dt_bias, m_gdn_norm_g, m_gdn_w_out, m_w_o, m_post_ln_g, m_post_ln_b, v_w_in, v_conf_dw_w, v_conf_dw_b, v_conf_ln_g, v_conf_ln_b, v_conf_w_out, v_gdn_conv_w, v_gdn_A_log, v_gdn_dt_bias, v_gdn_norm_g, v_gdn_w_out, v_w_o, v_post_ln_g, v_post_ln_b):
    x2 = x.reshape(x.shape[-2], D)
    tgt = loss_target.reshape(x2.shape)
    w4, wc_out, wg_out, wo_full, dw_full, cw_full = _gather_weights(
        w_in, conf_w_out, gdn_w_out, w_o, conf_dw_w, gdn_conv_w)
    (loss_part, grad_x_factors, d_w_in, d_wc, d_wg, d_wo, d_dww, d_cw, dconv_vec, dpost, d_ad, dng) = _local_step(
        x2, tgt, w4, wc_out, wg_out, wo_full, dw_full, cw_full, conf_dw_b, conf_ln_g, conf_ln_b,
        gdn_A_log, gdn_dt_bias, gdn_norm_g, post_ln_g, post_ln_b)
    loss = lax.psum(loss_part, ("x", "y", "c"))

    dww_c = d_dww[:KC].reshape(KC, N_CHIPS, SQ_BLK)
    dcw_c = d_cw[:KG].reshape(KG, N_CHIPS, 3 * SQ_BLK)
    vecs = [dconv_vec[0], dconv_vec[1], dconv_vec[2], dpost[0], dpost[1]]
    g_rest = jnp.stack([
        _pack_rest(d_wc[j * SQ_BLK:(j + 1) * SQ_BLK], d_wg[j * SQ_BLK:(j + 1) * SQ_BLK], d_wo[j * SQ_BLK:(j + 1) * SQ_BLK],
                   _pack_small(dww_c[:, j], dcw_c[:, j], vecs, d_ad[0, NH:2 * NH], d_ad[1, NH:2 * NH], dng[0]))
        for j in range(N_CHIPS)])
    pair_w, pair_r = _pair_sums(d_w_in, g_rest)
    grad_x, all_w, all_r = _mm_multi(*grad_x_factors, out_dtype=f32, tm=min(256, x2.shape[0]), tn=512,
                                     name="grad_x_and_chip_scatter", rhs_t=True, scatter=[pair_w, pair_r])
    g_w_in, g_rest = _chip_sums(all_w, all_r)

    def rest_of(w_c, w_g, w_oo, dw, cw, b1, g1, b2, g2, b3, a_log, dt_bias, norm_g):
        return _pack_rest(w_c, w_g, w_oo, _pack_small(dw, cw, [b1, g1, b2, g2, b3], a_log, dt_bias, norm_g))

    w_r = rest_of(conf_w_out, gdn_w_out, w_o, conf_dw_w, gdn_conv_w, conf_dw_b, conf_ln_g, conf_ln_b,
                  post_ln_g, post_ln_b, gdn_A_log, gdn_dt_bias, gdn_norm_g)
    m_r = rest_of(m_conf_w_out, m_gdn_w_out, m_w_o, m_conf_dw_w, m_gdn_conv_w, m_conf_dw_b, m_conf_ln_g, m_conf_ln_b,
                  m_post_ln_g, m_post_ln_b, m_gdn_A_log, m_gdn_dt_bias, m_gdn_norm_g)
    v_r = rest_of(v_conf_w_out, v_gdn_w_out, v_w_o, v_conf_dw_w, v_gdn_conv_w, v_conf_dw_b, v_conf_ln_g, v_conf_ln_b,
                  v_post_ln_g, v_post_ln_b, v_gdn_A_log, v_gdn_dt_bias, v_gdn_norm_g)
    upd_w_in = _adamw(w_in, g_w_in, m_w_in, v_w_in, "adamw_w_in")
    upd_rest = _adamw(w_r, g_rest, m_r, v_r, "adamw_rest")

    out = [loss, grad_x.reshape(x.shape)]
    for big, rest in zip((g_w_in,) + tuple(upd_w_in), (g_rest,) + tuple(upd_rest)):
        d = dict(_unpack_rest(rest), w_in=big)
        out += [d[n] for n in _WEIGHT_ORDER]
    return tuple(out)
```

```python
import functools

import jax
import jax.numpy as jnp
from jax import lax
from jax.experimental import pallas as pl
from jax.experimental.pallas import tpu as pltpu

f32 = jnp.float32
bf16 = jnp.bfloat16
HI = lax.Precision.HIGHEST
MESH = pl.DeviceIdType.MESH

D = 1024
NH = 8
HD = 128
CH = 64
KC = 31
KG = 4
HALO_C = 32
HALO_G = 8
LANE = 128
STRIP = 32
N_SHIFT = 7
LN_EPS = 1e-5
RMS_EPS = 1e-6
L2_EPS = 1e-6
DN_ALPHA = 2.0 ** 0.25
N_CHIPS = 4
W_IN_COLS = 9232
W_IN_BLK = W_IN_COLS // N_CHIPS
SQ_BLK = D // N_CHIPS
VMEM_LIMIT = 52 * 1024 * 1024
TOKEN_TILE = 256

ADAM_LR = 0.001
ADAM_B1 = 0.9
ADAM_B2 = 0.999
ADAM_EPS = 1e-08
ADAM_WD = 0.01
ADAM_STEP = 10


def _sigmoid(x):
    return 1.0 / (1.0 + jnp.exp(-x))


def _silu_and_grad(x):
    s = _sigmoid(x)
    return x * s, s * (1.0 + x * (1.0 - s))


_NN = ((1,), (0,))
_NT = ((1,), (1,))
_TN = ((0,), (0,))


def _cparams(*sem):
    return pltpu.CompilerParams(dimension_semantics=sem, vmem_limit_bytes=VMEM_LIMIT)


def _mm_multi(a_list, b_list, addend=None, *, out_dtype, tm, tn, name, rhs_t=False, scatter=()):
    n_pairs = len(a_list)
    m = a_list[0].shape[0]
    n = b_list[0].shape[0 if rhs_t else 1]
    has_add = addend is not None
    dims = (_NT if rhs_t else _NN, ((), ()))
    n_in = 2 * n_pairs + has_add
    k = len(scatter)
    grid = (n // tn, m // tm)

    def body(*refs):
        a_refs = refs[:n_pairs]
        b_refs = refs[n_pairs:2 * n_pairs]
        o_ref = refs[n_in + k]
        if k:
            start, finish = _chip_exchange_ops(refs[n_in:n_in + k], refs[n_in + k + 1:n_in + 2 * k + 1],
                                               *refs[n_in + 2 * k + 1:], True)
            step = pl.program_id(0) * grid[1] + pl.program_id(1)
            pl.when(step == 0)(start)
        acc = None
        for a_ref, b_ref in zip(a_refs, b_refs):
            p = lax.dot_general(a_ref[...].astype(bf16), b_ref[...].astype(bf16), dims, preferred_element_type=f32)
            acc = p if acc is None else acc + p
        if has_add:
            acc = acc + refs[2 * n_pairs][...]
        o_ref[...] = acc.astype(out_dtype)
        if k:
            pl.when(step == grid[0] * grid[1] - 1)(finish)

    in_specs = [pl.BlockSpec((tm, a.shape[1]), lambda j, i: (i, 0)) for a in a_list]
    if rhs_t:
        in_specs += [pl.BlockSpec((tn, b.shape[1]), lambda j, i: (j, 0)) for b in b_list]
    else:
        in_specs += [pl.BlockSpec((b.shape[0], tn), lambda j, i: (0, j)) for b in b_list]
    args = list(a_list) + list(b_list)
    if has_add:
        in_specs.append(pl.BlockSpec((tm, tn), lambda j, i: (i, j)))
        args.append(addend)
    out = pl.pallas_call(
        body, name=name, grid=grid,
        in_specs=in_specs + _any_specs(k), out_specs=[pl.BlockSpec((tm, tn), lambda j, i: (i, j))] + _any_specs(k),
        out_shape=[jax.ShapeDtypeStruct((m, n), out_dtype)] + _chip_exchange_shapes(scatter, True),
        scratch_shapes=_chip_exchange_sems(k) if k else [],
        compiler_params=_cparams("arbitrary", "arbitrary") if k else _cparams("parallel", "parallel"),
    )(*args, *scatter)
    return out if k else out[0]


def _mm_kloop(a, b, *, tm, tn, tk, name):
    k, m = a.shape
    n = b.shape[1]
    nk = k // tk

    def body(a_ref, b_ref, o_ref):
        @pl.when(pl.program_id(2) == 0)
        def _():
            o_ref[...] = jnp.zeros_like(o_ref)
        o_ref[...] += lax.dot_general(a_ref[...].astype(bf16), b_ref[...].astype(bf16), (_TN, ((), ())),
                                      preferred_element_type=f32)

    return pl.pallas_call(
        body, name=name, grid=(n // tn, m // tm, nk),
        in_specs=[pl.BlockSpec((tk, tm), lambda j, i, kk: (kk, i)), pl.BlockSpec((tk, tn), lambda j, i, kk: (kk, j))],
        out_specs=pl.BlockSpec((tm, tn), lambda j, i, kk: (i, j)),
        out_shape=jax.ShapeDtypeStruct((m, n), f32),
        compiler_params=_cparams("parallel", "parallel", "arbitrary"),
    )(a, b)


def _shift_copies(src_ref, sh_ref, n, shifts=tuple(range(1, 8))):
    for i, b in enumerate(shifts):
        sh_ref[i, 0:n, :] = src_ref[pl.ds(b, n), :]


def _by_residue(offs):
    groups = {}
    for k, off in enumerate(offs):
        groups.setdefault(off % 8, []).append((k, off // 8))
    return groups


def _slab(src_ref, sh_ref, shifts, b, r0, n, lanes):
    ref = src_ref if b == 0 else sh_ref.at[shifts.index(b)]
    return ref[r0:r0 + n, lanes]


def _tap_conv(out_ref, n_rows, src_ref, sh_ref, w_ref, offs, bias_ref=None, shifts=tuple(range(1, 8))):
    groups = _by_residue(offs)
    for j in range(D // LANE):
        lanes = slice(j * LANE, (j + 1) * LANE)
        wv = [w_ref[k:k + 1, lanes] for k in range(len(offs))]
        for r0 in range(0, n_rows, STRIP):
            n = min(STRIP, n_rows - r0)
            accs = [jnp.zeros((n, LANE), f32) if bias_ref is None else jnp.broadcast_to(bias_ref[0:1, lanes], (n, LANE)),
                    jnp.zeros((n, LANE), f32)]
            m = 0
            for b, taps in groups.items():
                a_lo = min(a for _, a in taps)
                a_hi = max(a for _, a in taps)
                wide = _slab(src_ref, sh_ref, shifts, b, r0 + 8 * a_lo, 8 * (a_hi - a_lo) + n, lanes)
                for k, a in taps:
                    accs[m % 2] = accs[m % 2] + wv[k] * wide[8 * (a - a_lo):8 * (a - a_lo) + n]
                    m += 1
            out_ref[r0:r0 + n, lanes] = accs[0] + accs[1]


def _tap_corr(dw_ref, n_rows, lhs_ref, src_ref, sh_ref, offs, shifts=tuple(range(1, 8))):
    groups = _by_residue(offs)
    for j in range(D // LANE):
        lanes = slice(j * LANE, (j + 1) * LANE)
        accs = [jnp.zeros((8, LANE), f32) for _ in offs]
        for r0 in range(0, n_rows, STRIP):
            n = min(STRIP, n_rows - r0)
            d = lhs_ref[r0:r0 + n, lanes]
            for b, taps in groups.items():
                a_lo = min(a for _, a in taps)
                a_hi = max(a for _, a in taps)
                wide = _slab(src_ref, sh_ref, shifts, b, r0 + 8 * a_lo, 8 * (a_hi - a_lo) + n, lanes)
                for k, a in taps:
                    prod = d * wide[8 * (a - a_lo):8 * (a - a_lo) + n]
                    part = prod[0:8]
                    for q in range(1, n // 8):
                        part = part + prod[8 * q:8 * q + 8]
                    accs[k] = accs[k] + part
        for k in range(len(offs)):
            dw_ref[k:k + 1, lanes] += jnp.sum(accs[k], axis=0, keepdims=True)


_FWD_OFFS = [HALO_C - (KC - 1) + k for k in range(KC)]
_BWD_OFFS = [KC - 1 - k for k in range(KC)]


def _norm_act(a1, cz, g_ref, bb_ref):
    mu = jnp.mean(a1, axis=-1, keepdims=True)
    cen = a1 - mu
    var = jnp.mean(cen * cen, axis=-1, keepdims=True)
    rstd = lax.rsqrt(var + LN_EPS)
    xhat = cen * rstd
    ln = xhat * g_ref[...] + bb_ref[...]
    s, ds = _silu_and_grad(ln)
    zc, dzc = _silu_and_grad(cz)
    return xhat, rstd, s, ds, zc, dzc


def _conv_fwd(p_conv, dw_w, dw_b, ln_g, ln_b, *, tt):
    t = p_conv.shape[0]
    hb = tt // HALO_C

    def body(cv_ref, cg_ref, cz_ref, cvh_ref, cgh_ref, w_ref, b_ref, g_ref, bb_ref, u_ref, a1_ref, ext_ref, sh_ref):
        first = pl.program_id(0) == 0
        halo = cvh_ref[...] * _sigmoid(cgh_ref[...])
        ext_ref[0:HALO_C, :] = jnp.where(first, 0.0, halo)
        ext_ref[HALO_C:, :] = cv_ref[...] * _sigmoid(cg_ref[...])
        _shift_copies(ext_ref, sh_ref, tt + HALO_C - 8)
        _tap_conv(a1_ref, tt, ext_ref, sh_ref, w_ref, _FWD_OFFS, b_ref)
        _, _, s, _, zc, _ = _norm_act(a1_ref[...], cz_ref[...], g_ref, bb_ref)
        u_ref[...] = (s * zc).astype(bf16)

    def main(col):
        return pl.BlockSpec((tt, D), lambda i: (i, col))

    def prev(col):
        return pl.BlockSpec((HALO_C, D), lambda i: (jnp.maximum(i * hb - 1, 0), col))

    vec = pl.BlockSpec((1, D), lambda i: (0, 0))
    return pl.pallas_call(
        body, name="conv_fwd", grid=(t // tt,),
        in_specs=[main(0), main(1), main(2), prev(0), prev(1),
                  pl.BlockSpec((HALO_C, D), lambda i: (0, 0)), vec, vec, vec],
        out_specs=[pl.BlockSpec((tt, D), lambda i: (i, 0))] * 2,
        out_shape=[jax.ShapeDtypeStruct((t, D), bf16), jax.ShapeDtypeStruct((t, D), f32)],
        scratch_shapes=[pltpu.VMEM((tt + HALO_C, D), f32), pltpu.VMEM((N_SHIFT, tt + HALO_C - 8, D), f32)],
        compiler_params=_cparams("parallel"),
    )(p_conv, p_conv, p_conv, p_conv, p_conv, dw_w, dw_b, ln_g, ln_b)


def _conv_bwd(p_conv, a1, du, dw_w, ln_g, ln_b, *, tt):
    t = p_conv.shape[0]
    hb = tt // HALO_C
    n_tiles = t // tt
    last_hb = t // HALO_C - 1
    ne = tt + HALO_C

    def body(cv_ref, cg_ref, cz_ref, a1_ref, du_ref, cvp_ref, cgp_ref, czn_ref, a1n_ref, dun_ref,
             w_ref, g_ref, bb_ref, dp_ref, dww_ref, dvec_ref, ext_ref, sh_ref, da1_ref, da0_ref):
        i = pl.program_id(0)
        first = i == 0
        last = i == n_tiles - 1

        @pl.when(first)
        def _():
            dww_ref[...] = jnp.zeros_like(dww_ref)
            dvec_ref[...] = jnp.zeros_like(dvec_ref)

        sig = _sigmoid(cg_ref[...])
        ext_ref[0:HALO_C, :] = jnp.where(first, 0.0, cvp_ref[...] * _sigmoid(cgp_ref[...]))
        ext_ref[HALO_C:, :] = cv_ref[...] * sig
        a1_all = jnp.concatenate([a1_ref[...], a1n_ref[...]], axis=0)
        cz = jnp.concatenate([cz_ref[...], czn_ref[...]], axis=0)
        du_all = jnp.concatenate([du_ref[...], jnp.where(last, 0.0, dun_ref[...])], axis=0)
        xhat, rstd, s, ds, zc, dzc = _norm_act(a1_all, cz, g_ref, bb_ref)
        dln = du_all * zc * ds
        dxhat = dln * g_ref[...]
        da1 = rstd * (dxhat - jnp.mean(dxhat, axis=-1, keepdims=True)
                      - xhat * jnp.mean(dxhat * xhat, axis=-1, keepdims=True))
        da1_ref[...] = da1
        dcz = (du_all * s * dzc)[:tt]
        dvec_ref[0:1, :] += jnp.sum(da1[:tt], axis=0, keepdims=True)
        dvec_ref[1:2, :] += jnp.sum((dln * xhat)[:tt], axis=0, keepdims=True)
        dvec_ref[2:3, :] += jnp.sum(dln[:tt], axis=0, keepdims=True)
        _shift_copies(ext_ref, sh_ref, ne - 8)
        _tap_corr(dww_ref, tt, da1_ref, ext_ref, sh_ref, _FWD_OFFS)
        _shift_copies(da1_ref, sh_ref, ne - 8)
        _tap_conv(da0_ref, tt, da1_ref, sh_ref, w_ref, _BWD_OFFS)
        da0 = da0_ref[...]
        cv = cv_ref[...]
        dp_ref[:, 0:D] = (da0 * sig).astype(bf16)
        dp_ref[:, D:2 * D] = (da0 * cv * sig * (1.0 - sig)).astype(bf16)
        dp_ref[:, 2 * D:] = dcz.astype(bf16)

    def main(col):
        return pl.BlockSpec((tt, D), lambda i: (i, col))

    def prev(col):
        return pl.BlockSpec((HALO_C, D), lambda i: (jnp.maximum(i * hb - 1, 0), col))

    def nxt(col):
        return pl.BlockSpec((HALO_C, D), lambda i: (jnp.minimum((i + 1) * hb, last_hb), col))

    vec = pl.BlockSpec((1, D), lambda i: (0, 0))
    return pl.pallas_call(
        body, name="conv_bwd", grid=(n_tiles,),
        in_specs=[main(0), main(1), main(2), main(0), main(0), prev(0), prev(1), nxt(2), nxt(0), nxt(0),
                  pl.BlockSpec((HALO_C, D), lambda i: (0, 0)), vec, vec],
        out_specs=[pl.BlockSpec((tt, 3 * D), lambda i: (i, 0)),
                   pl.BlockSpec((HALO_C, D), lambda i: (0, 0)),
                   pl.BlockSpec((8, D), lambda i: (0, 0))],
        out_shape=[jax.ShapeDtypeStruct((t, 3 * D), bf16), jax.ShapeDtypeStruct((HALO_C, D), f32),
                   jax.ShapeDtypeStruct((8, D), f32)],
        scratch_shapes=[pltpu.VMEM((ne, D), f32), pltpu.VMEM((N_SHIFT, ne - 8, D), f32),
                        pltpu.VMEM((ne, D), f32), pltpu.VMEM((tt, D), f32)],
        compiler_params=_cparams("arbitrary"),
    )(p_conv, p_conv, p_conv, a1, du, p_conv, p_conv, p_conv, a1, du, dw_w, ln_g, ln_b)


def _dot_hi(a, b):
    return lax.dot_general(a, b, (((1,), (0,)), ((), ())), precision=HI, preferred_element_type=f32)


def _chunk_tri(n, lower):
    r = lax.broadcasted_iota(jnp.int32, (n, n), 0)
    c = lax.broadcasted_iota(jnp.int32, (n, n), 1)
    tri = (r >= c) if lower else (r <= c)
    return jnp.where(tri & (r // CH == c // CH), 1.0, 0.0).astype(f32)


def _softplus_and_sigmoid(x):
    e = jnp.exp(-jnp.abs(x))
    log1p = jnp.where(e < 1e-2, e * (1.0 - e * (0.5 - e * (1.0 / 3.0 - 0.25 * e))), jnp.log(1.0 + e))
    return jnp.maximum(x, 0.0) + log1p, _sigmoid(x)


_G_FWD_OFFS = [HALO_G - (KG - 1) + k for k in range(KG)]
_G_FWD_SHIFTS = (5, 6, 7)
_G_BWD_OFFS = [KG - 1 - k for k in range(KG)]
_G_BWD_SHIFTS = (1, 2, 3)


def _gdn_short_conv(pre_ref, ext_ref, sh_ref, n_rows, w_ref):
    _shift_copies(ext_ref, sh_ref, n_rows, _G_FWD_SHIFTS)
    _tap_conv(pre_ref, n_rows, ext_ref, sh_ref, w_ref, _G_FWD_OFFS, shifts=_G_FWD_SHIFTS)
    return pre_ref[...]


def _l2norm_heads(act, scale):
    outs, rs = [], []
    for h in range(NH):
        a = act[:, h * HD:(h + 1) * HD]
        r = lax.rsqrt(jnp.sum(a * a, axis=-1, keepdims=True) + L2_EPS)
        outs.append(a * (r * scale))
        rs.append(jnp.broadcast_to(r, a.shape))
    return jnp.concatenate(outs, axis=-1), jnp.concatenate(rs, axis=-1)


def _gate_math(ba, al_ref, dt_ref):
    lane = lax.broadcasted_iota(jnp.int32, ba.shape, 1)
    is_b = lane < NH
    is_a = (lane >= NH) & (lane < 2 * NH)
    sp, sg = _softplus_and_sigmoid(ba + dt_ref[...])
    neg_a = -jnp.exp(al_ref[...])
    return is_b, is_a, _sigmoid(ba), neg_a * sp, sg, neg_a


def _gdn_pre_fwd(p_qkv, p_ba, cw, alog_v, dt_v, *, tt):
    t = p_qkv.shape[0]
    hb = tt // HALO_G

    def body(q_ref, k_ref, v_ref, qh_ref, kh_ref, vh_ref, ba_ref, wq_ref, wk_ref, wv_ref, al_ref, dt_ref,
             qn_ref, kn_ref, va_ref, gt_ref, ext_ref, sh_ref, pre_ref):
        first = pl.program_id(0) == 0

        def conv_act(x_ref, xh_ref, w_ref):
            ext_ref[0:HALO_G, :] = jnp.where(first, 0.0, xh_ref[...])
            ext_ref[HALO_G:, :] = x_ref[...]
            pre = _gdn_short_conv(pre_ref, ext_ref, sh_ref, tt, w_ref)
            return pre * _sigmoid(pre)

        qn_ref[...] = _l2norm_heads(conv_act(q_ref, qh_ref, wq_ref), HD ** -0.5)[0]
        kn_ref[...] = _l2norm_heads(conv_act(k_ref, kh_ref, wk_ref), 1.0)[0]
        va_ref[...] = conv_act(v_ref, vh_ref, wv_ref)
        is_b, is_a, beta, g, _, _ = _gate_math(ba_ref[...], al_ref, dt_ref)
        gc = _dot_hi(_chunk_tri(tt, lower=True), jnp.where(is_a, g, 0.0))
        gt_ref[...] = jnp.where(is_b, beta, gc)

    def main(col):
        return pl.BlockSpec((tt, D), lambda i: (i, col))

    def prev(col):
        return pl.BlockSpec((HALO_G, D), lambda i: (jnp.maximum(i * hb - 1, 0), col))

    def wspec(col):
        return pl.BlockSpec((8, D), lambda i: (0, col))

    vec = pl.BlockSpec((1, HD), lambda i: (0, 0))
    gblk = pl.BlockSpec((tt, HD), lambda i: (i, 0))
    sds = jax.ShapeDtypeStruct((t, D), f32)
    return pl.pallas_call(
        body, name="gdn_pre_fwd", grid=(t // tt,),
        in_specs=[main(0), main(1), main(2), prev(0), prev(1), prev(2), gblk, wspec(0), wspec(1), wspec(2), vec, vec],
        out_specs=[pl.BlockSpec((tt, D), lambda i: (i, 0))] * 3 + [gblk],
        out_shape=[sds] * 3 + [jax.ShapeDtypeStruct((t, HD), f32)],
        scratch_shapes=[pltpu.VMEM((tt + HALO_G, D), f32), pltpu.VMEM((KG - 1, tt, D), f32), pltpu.VMEM((tt, D), f32)],
        compiler_params=_cparams("parallel"),
    )(p_qkv, p_qkv, p_qkv, p_qkv, p_qkv, p_qkv, p_ba, cw, cw, cw, alog_v, dt_v)


def _gdn_pre_bwd(p_qkv, p_ba, cw, alog_v, dt_v, dqn, dkn, dva, dgt, *, tt):
    t = p_qkv.shape[0]
    hb = tt // HALO_G
    n_tiles = t // tt
    last_hb = t // HALO_G - 1
    ne = tt + HALO_G

    def body(q_ref, k_ref, v_ref, qp_ref, kp_ref, vp_ref, qx_ref, kx_ref, vx_ref,
             dq_ref, dk_ref, dv_ref, dqx_ref, dkx_ref, dvx_ref, ba_ref, dgt_ref,
             wq_ref, wk_ref, wv_ref, al_ref, dt_ref,
             dp_ref, dba_ref, dcw_ref, dad_ref, ext_ref, sh_ref, pre_ref, dpre_ref, draw_ref):
        i = pl.program_id(0)
        first = i == 0
        last = i == n_tiles - 1

        @pl.when(first)
        def _():
            dcw_ref[...] = jnp.zeros_like(dcw_ref)
            dad_ref[...] = jnp.zeros_like(dad_ref)

        def one(x_ref, xp_ref, xx_ref, d_ref, dx_ref, w_ref, col, scale):
            ext_ref[0:HALO_G, :] = jnp.where(first, 0.0, xp_ref[...])
            ext_ref[HALO_G:HALO_G + tt, :] = x_ref[...]
            ext_ref[HALO_G + tt:, :] = xx_ref[...]
            pre = _gdn_short_conv(pre_ref, ext_ref, sh_ref, ne, w_ref)
            act, dact = _silu_and_grad(pre)
            d_out = jnp.concatenate([d_ref[...], jnp.where(last, 0.0, dx_ref[...])], axis=0)
            if scale is None:
                d_act = d_out
            else:
                parts = []
                for h in range(NH):
                    a = act[:, h * HD:(h + 1) * HD]
                    dn = d_out[:, h * HD:(h + 1) * HD]
                    r = lax.rsqrt(jnp.sum(a * a, axis=-1, keepdims=True) + L2_EPS)
                    parts.append(scale * r * (dn - a * (r * r) * jnp.sum(dn * a, axis=-1, keepdims=True)))
                d_act = jnp.concatenate(parts, axis=-1)
            dpre_ref[...] = d_act * dact
            _tap_corr(dcw_ref.at[:, col * D:(col + 1) * D], tt, dpre_ref, ext_ref, sh_ref, _G_FWD_OFFS, shifts=_G_FWD_SHIFTS)
            _shift_copies(dpre_ref, sh_ref, tt, _G_BWD_SHIFTS)
            _tap_conv(draw_ref, tt, dpre_ref, sh_ref, w_ref, _G_BWD_OFFS, shifts=_G_BWD_SHIFTS)
            dp_ref[:, col * D:(col + 1) * D] = draw_ref[...].astype(bf16)

        one(q_ref, qp_ref, qx_ref, dq_ref, dqx_ref, wq_ref, 0, HD ** -0.5)
        one(k_ref, kp_ref, kx_ref, dk_ref, dkx_ref, wk_ref, 1, 1.0)
        one(v_ref, vp_ref, vx_ref, dv_ref, dvx_ref, wv_ref, 2, None)

        is_b, is_a, beta, g, sg, neg_a = _gate_math(ba_ref[...], al_ref, dt_ref)
        dgt_v = dgt_ref[...]
        dg = _dot_hi(_chunk_tri(tt, lower=False), jnp.where(is_a, dgt_v, 0.0))
        d_al = jnp.where(is_a, dg * neg_a * sg, 0.0)
        dba_ref[...] = jnp.where(is_b, dgt_v * beta * (1.0 - beta), d_al)
        dad_ref[0:1, :] += jnp.sum(jnp.where(is_a, dg * g, 0.0), axis=0, keepdims=True)
        dad_ref[1:2, :] += jnp.sum(d_al, axis=0, keepdims=True)

    def main(col):
        return pl.BlockSpec((tt, D), lambda i: (i, col))

    def prev(col):
        return pl.BlockSpec((HALO_G, D), lambda i: (jnp.maximum(i * hb - 1, 0), col))

    def nxt(col):
        return pl.BlockSpec((HALO_G, D), lambda i: (jnp.minimum((i + 1) * hb, last_hb), col))

    def wspec(col):
        return pl.BlockSpec((8, D), lambda i: (0, col))

    vec = pl.BlockSpec((1, HD), lambda i: (0, 0))
    gblk = pl.BlockSpec((tt, HD), lambda i: (i, 0))
    return pl.pallas_call(
        body, name="gdn_pre_bwd", grid=(n_tiles,),
        in_specs=[main(0), main(1), main(2), prev(0), prev(1), prev(2), nxt(0), nxt(1), nxt(2),
                  main(0), main(0), main(0), nxt(0), nxt(0), nxt(0), gblk, gblk,
                  wspec(0), wspec(1), wspec(2), vec, vec],
        out_specs=[pl.BlockSpec((tt, 3 * D), lambda i: (i, 0)), pl.BlockSpec((tt, HD), lambda i: (i, 0)),
                   pl.BlockSpec((8, 3 * D), lambda i: (0, 0)), pl.BlockSpec((8, HD), lambda i: (0, 0))],
        out_shape=[jax.ShapeDtypeStruct((t, 3 * D), bf16), jax.ShapeDtypeStruct((t, HD), f32),
                   jax.ShapeDtypeStruct((8, 3 * D), f32), jax.ShapeDtypeStruct((8, HD), f32)],
        scratch_shapes=[pltpu.VMEM((HALO_G + tt + HALO_G, D), f32), pltpu.VMEM((KG - 1, ne, D), f32),
                        pltpu.VMEM((ne, D), f32), pltpu.VMEM((ne, D), f32), pltpu.VMEM((tt, D), f32)],
        compiler_params=_cparams("arbitrary"),
    )(p_qkv, p_qkv, p_qkv, p_qkv, p_qkv, p_qkv, p_qkv, p_qkv, p_qkv,
      dqn, dkn, dva, dqn, dkn, dva, p_ba, dgt, cw, cw, cw, alog_v, dt_v)


def _dot_b(a, b, dims):
    return lax.dot_general(a.astype(bf16), b.astype(bf16), (dims, ((), ())), preferred_element_type=f32)


def _inverse_by_doubling(ms):
    heads = range(len(ms))
    r = lax.broadcasted_iota(jnp.int32, (CH, CH), 0)
    c = lax.broadcasted_iota(jnp.int32, (CH, CH), 1)
    eye = jnp.where(r == c, 1.0, 0.0).astype(f32)
    p = [eye + ms[h] for h in heads]
    mp = ms
    for _ in range(5):
        mp = [_dot_b(mp[h], mp[h], _NN) for h in heads]
        pm = [_dot_b(p[h], mp[h], _NN) for h in heads]
        p = [p[h] + pm[h] for h in heads]
    return tuple(p)


@jax.custom_vjp
def _known_inverse(ms, ps):
    return ps


def _known_inverse_fwd(ms, ps):
    return ps, ps


def _known_inverse_bwd(ps, cts):
    heads = range(len(ps))
    left = [_dot_b(ps[h], cts[h], _TN) for h in heads]
    return tuple(_dot_b(left[h], ps[h], _NT) for h in heads), tuple(jnp.zeros_like(p) for p in ps)


_known_inverse.defvjp(_known_inverse_fwd, _known_inverse_bwd)


def _chunk_fn(qs, ks, vs, gcs, bbs, ss, ps=None):
    heads = range(len(qs))
    r = lax.broadcasted_iota(jnp.int32, (CH, CH), 0)
    c = lax.broadcasted_iota(jnp.int32, (CH, CH), 1)
    causal = r >= c
    strict = r > c
    gc_row = [gcs[h].T[:CH, :] for h in heads]
    decay = [jnp.where(causal, jnp.exp(jnp.where(causal, gcs[h][:, :CH] - gc_row[h], 0.0)), 0.0) for h in heads]
    kb = [ks[h] * bbs[h] for h in heads]
    egc = [jnp.exp(gcs[h]) for h in heads]
    kk = [_dot_b(kb[h], ks[h], _NT) for h in heads]
    qk = [_dot_b(qs[h], ks[h], _NT) for h in heads]
    m = tuple(-jnp.where(strict, kk[h] * decay[h], 0.0) for h in heads)
    p = _inverse_by_doubling(m) if ps is None else _known_inverse(m, ps)
    u = [_dot_b(p[h], vs[h] * bbs[h], _NN) for h in heads]
    w = [_dot_b(p[h], kb[h] * egc[h], _NN) for h in heads]
    intra = [jnp.where(causal, qk[h] * decay[h], 0.0) for h in heads]
    g_last = [gcs[h][CH - 1:CH, :] for h in heads]
    k_dec = [ks[h] * jnp.exp(g_last[h] - gcs[h]) for h in heads]
    ws = [_dot_b(w[h], ss[h], _NN) for h in heads]
    qs_s = [_dot_b(qs[h] * egc[h], ss[h], _NN) for h in heads]
    v_new = [u[h] - ws[h] for h in heads]
    iv = [_dot_b(intra[h], v_new[h], _NN) for h in heads]
    kv = [_dot_b(k_dec[h], v_new[h], _TN) for h in heads]
    o = tuple(qs_s[h] + iv[h] for h in heads)
    s_new = tuple(ss[h] * jnp.exp(g_last[h]) + kv[h] for h in heads)
    return o, s_new, p


def _head_cols():
    return [slice(h * HD, (h + 1) * HD) for h in range(NH)]


def _head_gates(gt):
    gcs = tuple(jnp.broadcast_to(gt[:, NH + h:NH + h + 1], (CH, HD)) for h in range(NH))
    bbs = tuple(jnp.broadcast_to(gt[:, h:h + 1], (CH, HD)) for h in range(NH))
    return gcs, bbs


def _gdn_scan_fwd(qn, kn, va, gates, *, tt):
    t = qn.shape[0]
    cpb = tt // CH

    def body(q_ref, k_ref, v_ref, gt_ref, o_ref, st_ref, p_ref, s_scr):
        @pl.when(pl.program_id(0) == 0)
        def _():
            s_scr[...] = jnp.zeros_like(s_scr)

        def step(ci, carry):
            rows = pl.ds(pl.multiple_of(ci * CH, CH), CH)
            cols = _head_cols()
            ss = tuple(s_scr[h] for h in range(NH))
            for h in range(NH):
                st_ref[ci, h] = ss[h]
            gcs, bbs = _head_gates(gt_ref[rows, :])
            o, s_new, p = _chunk_fn(*(tuple(ref[rows, cl] for cl in cols) for ref in (q_ref, k_ref, v_ref)), gcs, bbs, ss)
            for h in range(NH):
                o_ref[rows, cols[h]] = o[h]
                s_scr[h] = s_new[h]
                p_ref[ci, h] = p[h].astype(bf16)
            return carry

        lax.fori_loop(0, cpb, step, 0)

    blk = pl.BlockSpec((tt, D), lambda i: (i, 0))
    return pl.pallas_call(
        body, name="gdn_scan_fwd", grid=(t // tt,),
        in_specs=[blk] * 3 + [pl.BlockSpec((tt, HD), lambda i: (i, 0))],
        out_specs=[blk, pl.BlockSpec((cpb, NH, HD, HD), lambda i: (i, 0, 0, 0)),
                   pl.BlockSpec((cpb, NH, CH, CH), lambda i: (i, 0, 0, 0))],
        out_shape=[jax.ShapeDtypeStruct((t, D), f32), jax.ShapeDtypeStruct((t // CH, NH, HD, HD), f32),
                   jax.ShapeDtypeStruct((t // CH, NH, CH, CH), bf16)],
        scratch_shapes=[pltpu.VMEM((NH, HD, HD), f32)],
        compiler_params=_cparams("arbitrary"),
    )(qn, kn, va, gates)


def _gdn_scan_bwd(qn, kn, va, gates, states, inverses, do, *, tt):
    t = qn.shape[0]
    nblk = t // tt
    cpb = tt // CH

    def body(q_ref, k_ref, v_ref, gt_ref, st_ref, p_ref, do_ref, dq_ref, dk_ref, dv_ref, dgt_ref, ds_scr):
        @pl.when(pl.program_id(0) == 0)
        def _():
            ds_scr[...] = jnp.zeros_like(ds_scr)

        def step(j, carry):
            ci = cpb - 1 - j
            rows = pl.ds(pl.multiple_of(ci * CH, CH), CH)
            cols = _head_cols()
            gcs, bbs = _head_gates(gt_ref[rows, :])
            ps = tuple(p_ref[ci, h].astype(f32) for h in range(NH))
            _, vjp = jax.vjp(lambda *a: _chunk_fn(*a, ps=ps)[:2],
                             *(tuple(ref[rows, cl] for cl in cols) for ref in (q_ref, k_ref, v_ref)),
                             gcs, bbs, tuple(st_ref[ci, h] for h in range(NH)))
            grads = vjp((tuple(do_ref[rows, cl] for cl in cols), tuple(ds_scr[h] for h in range(NH))))
            lane = lax.broadcasted_iota(jnp.int32, (CH, HD), 1)
            dgt = jnp.zeros((CH, HD), f32)
            for h in range(NH):
                for ref, g in zip((dq_ref, dk_ref, dv_ref), grads[:3]):
                    ref[rows, cols[h]] = g[h]
                dgt = dgt + jnp.where(lane == NH + h, jnp.sum(grads[3][h], axis=-1, keepdims=True), 0.0)
                dgt = dgt + jnp.where(lane == h, jnp.sum(grads[4][h], axis=-1, keepdims=True), 0.0)
                ds_scr[h] = grads[5][h]
            dgt_ref[rows, :] = dgt
            return carry

        lax.fori_loop(0, cpb, step, 0)

    blk = pl.BlockSpec((tt, D), lambda i: (nblk - 1 - i, 0))
    sblk = pl.BlockSpec((cpb, NH, HD, HD), lambda i: (nblk - 1 - i, 0, 0, 0))
    sds = jax.ShapeDtypeStruct((t, D), f32)
    gblk = pl.BlockSpec((tt, HD), lambda i: (nblk - 1 - i, 0))
    pblk = pl.BlockSpec((cpb, NH, CH, CH), lambda i: (nblk - 1 - i, 0, 0, 0))
    return pl.pallas_call(
        body, name="gdn_scan_bwd", grid=(nblk,),
        in_specs=[blk] * 3 + [gblk, sblk, pblk, blk],
        out_specs=[blk] * 3 + [gblk], out_shape=[sds] * 3 + [jax.ShapeDtypeStruct((t, HD), f32)],
        scratch_shapes=[pltpu.VMEM((NH, HD, HD), f32)],
        compiler_params=_cparams("arbitrary"),
    )(qn, kn, va, gates, states, inverses, do)


def _rms_heads(o):
    ons, rs = [], []
    for h in range(NH):
        a = o[:, h * HD:(h + 1) * HD]
        r = lax.rsqrt(jnp.mean(a * a, axis=-1, keepdims=True) + RMS_EPS)
        ons.append(a * r)
        rs.append(jnp.broadcast_to(r, a.shape))
    return jnp.concatenate(ons, axis=-1), jnp.concatenate(rs, axis=-1)


def _gdn_post_fwd(o, p_gz, ng_b, *, tt):
    t = o.shape[0]

    def body(o_ref, gz_ref, ng_ref, og_ref):
        on, _ = _rms_heads(o_ref[...])
        z, _ = _silu_and_grad(gz_ref[...])
        og_ref[...] = (on * ng_ref[...] * z).astype(bf16)

    blk = pl.BlockSpec((tt, D), lambda i: (i, 0))
    return pl.pallas_call(
        body, name="gdn_post_fwd", grid=(t // tt,),
        in_specs=[blk, blk, pl.BlockSpec((1, D), lambda i: (0, 0))],
        out_specs=blk, out_shape=jax.ShapeDtypeStruct((t, D), bf16),
        compiler_params=_cparams("parallel"),
    )(o, p_gz, ng_b)


def _gdn_post_bwd(o, p_gz, ng_b, dog, *, tt):
    t = o.shape[0]

    def body(o_ref, gz_ref, ng_ref, dog_ref, do_ref, dgz_ref, dng_ref):
        @pl.when(pl.program_id(0) == 0)
        def _():
            dng_ref[...] = jnp.zeros_like(dng_ref)

        on, r = _rms_heads(o_ref[...])
        z, dz = _silu_and_grad(gz_ref[...])
        dog_v = dog_ref[...]
        ng = ng_ref[...]
        dgz_ref[...] = (dog_v * on * ng * dz).astype(bf16)
        dy = dog_v * z
        dng_all = jnp.sum(dy * on, axis=0, keepdims=True)
        dng = dng_all[:, 0:HD]
        for h in range(1, NH):
            dng = dng + dng_all[:, h * HD:(h + 1) * HD]
        dng_ref[0:1, :] += dng
        don = dy * ng
        prod = don * on
        parts = []
        for h in range(NH):
            sl = slice(h * HD, (h + 1) * HD)
            parts.append(don[:, sl] - on[:, sl] * jnp.mean(prod[:, sl], axis=-1, keepdims=True))
        do_ref[...] = r * jnp.concatenate(parts, axis=-1)

    blk = pl.BlockSpec((tt, D), lambda i: (i, 0))
    return pl.pallas_call(
        body, name="gdn_post_bwd", grid=(t // tt,),
        in_specs=[blk, blk, pl.BlockSpec((1, D), lambda i: (0, 0)), blk],
        out_specs=[blk, blk, pl.BlockSpec((8, HD), lambda i: (0, 0))],
        out_shape=[jax.ShapeDtypeStruct((t, D), f32), jax.ShapeDtypeStruct((t, D), bf16),
                   jax.ShapeDtypeStruct((8, HD), f32)],
        compiler_params=_cparams("arbitrary"),
    )(o, p_gz, ng_b, dog)


def _merge(x, y_conf, y_gdn, p_gates, target, w_o, ln_g, ln_b, *, tt):
    t = x.shape[0]

    def body(x_ref, yc_ref, yg_ref, gc_ref, gg_ref, tg_ref, w_ref, g_ref, b_ref,
             loss_ref, dxd_ref, dyc_ref, dyg_ref, dpg_ref, h_ref, dz_ref, dvec_ref):
        @pl.when(pl.program_id(0) == 0)
        def _():
            loss_ref[...] = jnp.zeros_like(loss_ref)
            dvec_ref[...] = jnp.zeros_like(dvec_ref)

        sc = _sigmoid(gc_ref[...])
        sg = _sigmoid(gg_ref[...])
        yc = yc_ref[...]
        yg = yg_ref[...]
        h = (sc * yc + sg * yg).astype(bf16)
        h_ref[...] = h
        z = DN_ALPHA * x_ref[...] + jnp.dot(h, w_ref[...], preferred_element_type=f32)
        mu = jnp.mean(z, axis=-1, keepdims=True)
        cen = z - mu
        rstd = lax.rsqrt(jnp.mean(cen * cen, axis=-1, keepdims=True) + LN_EPS)
        xhat = cen * rstd
        err = xhat * g_ref[...] + b_ref[...] - tg_ref[...]
        loss_ref[...] += 0.5 / D * jnp.sum(err * err)
        dy = err * (1.0 / D)
        dvec_ref[0:1, :] += jnp.sum(dy * xhat, axis=0, keepdims=True)
        dvec_ref[1:2, :] += jnp.sum(dy, axis=0, keepdims=True)
        dxhat = dy * g_ref[...]
        dz = rstd * (dxhat - jnp.mean(dxhat, axis=-1, keepdims=True)
                     - xhat * jnp.mean(dxhat * xhat, axis=-1, keepdims=True))
        dxd_ref[...] = DN_ALPHA * dz
        dz_b = dz.astype(bf16)
        dz_ref[...] = dz_b
        dh = lax.dot_general(dz_b, w_ref[...], (_NT, ((), ())), preferred_element_type=f32)
        dyc_ref[...] = (dh * sc).astype(bf16)
        dyg_ref[...] = (dh * sg).astype(bf16)
        dpg_ref[:, 0:D] = (dh * yc * sc * (1.0 - sc)).astype(bf16)
        dpg_ref[:, D:] = (dh * yg * sg * (1.0 - sg)).astype(bf16)

    blk = pl.BlockSpec((tt, D), lambda i: (i, 0))
    wblk = pl.BlockSpec((D, D), lambda i: (0, 0))
    vec = pl.BlockSpec((1, D), lambda i: (0, 0))
    return pl.pallas_call(
        body, name="merge_norm_loss", grid=(t // tt,),
        in_specs=[blk, blk, blk, pl.BlockSpec((tt, D), lambda i: (i, 0)), pl.BlockSpec((tt, D), lambda i: (i, 1)),
                  blk, wblk, vec, vec],
        out_specs=[pl.BlockSpec((8, HD), lambda i: (0, 0)), blk, blk, blk,
                   pl.BlockSpec((tt, 2 * D), lambda i: (i, 0)), blk, blk, pl.BlockSpec((8, D), lambda i: (0, 0))],
        out_shape=[jax.ShapeDtypeStruct((8, HD), f32), jax.ShapeDtypeStruct((t, D), f32),
                   jax.ShapeDtypeStruct((t, D), bf16), jax.ShapeDtypeStruct((t, D), bf16),
                   jax.ShapeDtypeStruct((t, 2 * D), bf16), jax.ShapeDtypeStruct((t, D), bf16),
                   jax.ShapeDtypeStruct((t, D), bf16), jax.ShapeDtypeStruct((8, D), f32)],
        compiler_params=_cparams("arbitrary"),
    )(x, y_conf, y_gdn, p_gates, p_gates, target, w_o, ln_g, ln_b)


def _place():
    return lax.axis_index("x"), lax.axis_index("y"), lax.axis_index("c")


def _any_specs(n):
    return [pl.BlockSpec(memory_space=pl.ANY)] * n


def _sibling_merge(arrs, name, take_other_half=False):
    k = len(arrs)

    def body(*refs):
        a_refs, o_refs = refs[:k], refs[k:2 * k]
        send_sems, recv_sems = refs[2 * k:]
        x, y, c = _place()
        sends = []
        for i in range(k):
            src = a_refs[i]
            if take_other_half:
                n = a_refs[i].shape[-2] // 2
                lead = (slice(None),) * (len(a_refs[i].shape) - 2)
                src = a_refs[i].at[lead + (pl.ds((1 - c) * n, n), slice(None))]
            cp = pltpu.make_async_remote_copy(src_ref=src, dst_ref=o_refs[i], send_sem=send_sems.at[i],
                                              recv_sem=recv_sems.at[i], device_id=(x, y, 1 - c), device_id_type=MESH)
            cp.start()
            sends.append(cp)
        for cp in sends:
            cp.wait()

    def out_sds(a):
        rows = a.shape[-2] // 2 if take_other_half else a.shape[-2]
        return jax.ShapeDtypeStruct(a.shape[:-2] + (rows, a.shape[-1]), a.dtype)

    return pl.pallas_call(
        body, name=name, in_specs=_any_specs(k), out_specs=_any_specs(k),
        out_shape=[out_sds(a) for a in arrs],
        scratch_shapes=[pltpu.SemaphoreType.DMA((k,)), pltpu.SemaphoreType.DMA((k,))],
    )(*arrs)


def _join_halves(mine, other):
    c = lax.axis_index("c")
    return jnp.concatenate([jnp.where(c == 0, mine, other), jnp.where(c == 0, other, mine)], axis=-2)


def _chip_exchange_ops(a_refs, o_refs, send_sems, recv_sems, local_sems, scatter):
    k = len(a_refs)
    x, y, c = _place()
    me = 2 * x + y
    peers = [(1 - x, y), (x, 1 - y), (1 - x, 1 - y)]

    def src(i, j):
        return a_refs[i].at[j] if scatter else a_refs[i]

    def copy(i, n, send_j, slot):
        px, py = peers[n]
        return pltpu.make_async_remote_copy(
            src_ref=src(i, send_j), dst_ref=o_refs[i].at[slot], send_sem=send_sems.at[3 * i + n],
            recv_sem=recv_sems.at[3 * i + n], device_id=(px, py, c), device_id_type=MESH)

    def owns():
        return [pltpu.make_async_copy(src(i, me), o_refs[i].at[me], local_sems.at[i]) for i in range(k)]

    def sends():
        return [copy(i, n, 2 * peers[n][0] + peers[n][1], me) for n in range(3) for i in range(k)]

    def start():
        for cp in owns() + sends():
            cp.start()

    def finish():
        for n in range(3):
            for i in range(k):
                copy(i, n, me, 2 * peers[n][0] + peers[n][1]).wait_recv()
        for cp in sends():
            cp.wait_send()
        for cp in owns():
            cp.wait()

    return start, finish


def _chip_exchange_shapes(arrs, scatter):
    return [jax.ShapeDtypeStruct((N_CHIPS,) + tuple(a.shape[1:] if scatter else a.shape), a.dtype) for a in arrs]


def _chip_exchange_sems(k):
    return [pltpu.SemaphoreType.DMA((3 * k,)), pltpu.SemaphoreType.DMA((3 * k,)), pltpu.SemaphoreType.DMA((k,))]


def _chip_exchange(arrs, name, scatter):
    k = len(arrs)

    def body(*refs):
        start, finish = _chip_exchange_ops(refs[:k], refs[k:2 * k], *refs[2 * k:], scatter)
        start()
        finish()

    return pl.pallas_call(
        body, name=name, in_specs=_any_specs(k), out_specs=_any_specs(k),
        out_shape=_chip_exchange_shapes(arrs, scatter), scratch_shapes=_chip_exchange_sems(k),
    )(*arrs)


def _pair_sum(g_all, got, c_arr, name, out_dtype):
    n, w = got.shape[1:]
    tile = n // 4
    n_tiles = n // tile

    def body(c_ref, a_ref, b_ref, o_ref):
        o_ref[...] = (a_ref[...] + b_ref[...]).astype(out_dtype)

    return pl.pallas_call(
        body, name=name,
        grid_spec=pltpu.PrefetchScalarGridSpec(
            num_scalar_prefetch=1, grid=(N_CHIPS, n_tiles),
            in_specs=[pl.BlockSpec((1, tile, w), lambda j, i, c_ref: (j, c_ref[0] * n_tiles + i, 0)),
                      pl.BlockSpec((1, tile, w), lambda j, i, c_ref: (j, i, 0))],
            out_specs=pl.BlockSpec((1, tile, w), lambda j, i, c_ref: (j, i, 0))),
        out_shape=jax.ShapeDtypeStruct(got.shape, out_dtype),
        compiler_params=_cparams("parallel", "parallel"),
    )(c_arr, g_all, got)


def _sum_slots(a, name):
    n, w = a.shape[1:]
    tile = n // 4

    def body(a_ref, o_ref):
        o_ref[...] = ((a_ref[0].astype(f32) + a_ref[1].astype(f32)) + a_ref[2].astype(f32)) + a_ref[3].astype(f32)

    return pl.pallas_call(
        body, name=name, grid=(n // tile,),
        in_specs=[pl.BlockSpec((N_CHIPS, tile, w), lambda i: (0, i, 0))],
        out_specs=pl.BlockSpec((tile, w), lambda i: (i, 0)),
        out_shape=jax.ShapeDtypeStruct((n, w), f32),
        compiler_params=_cparams("parallel"),
    )(a)


def _adamw(w, g, m, v, name):
    rows, width = w.shape
    tile = rows // 8
    c1 = 1.0 / (1.0 - ADAM_B1 ** ADAM_STEP)
    c2 = 1.0 / (1.0 - ADAM_B2 ** ADAM_STEP)

    def body(w_ref, g_ref, m_ref, v_ref, d_ref, mo_ref, vo_ref):
        g_v = g_ref[...]
        m_new = ADAM_B1 * m_ref[...] + (1.0 - ADAM_B1) * g_v
        v_new = ADAM_B2 * v_ref[...] + (1.0 - ADAM_B2) * (g_v * g_v)
        mo_ref[...] = m_new
        vo_ref[...] = v_new
        d_ref[...] = -ADAM_LR * ((m_new * c1) / (jnp.sqrt(v_new * c2) + ADAM_EPS) + ADAM_WD * w_ref[...])

    blk = pl.BlockSpec((tile, width), lambda i: (i, 0))
    sds = jax.ShapeDtypeStruct((rows, width), f32)
    return pl.pallas_call(
        body, name=name, grid=(rows // tile,),
        in_specs=[blk] * 4, out_specs=[blk] * 3, out_shape=[sds] * 3,
        compiler_params=_cparams("parallel"),
    )(w, g, m, v)


R_DW = 3 * SQ_BLK
R_CW = R_DW + 8
R_VEC = R_CW + 8
R_SMALL = R_VEC + 8
REST_ROWS = 896


def _pack_small(conf_dw_w, gdn_conv_w, vecs, a_log, dt_bias, norm_g):
    dw = jnp.pad(conf_dw_w.reshape(-1), (0, 8 * D - KC * SQ_BLK)).reshape(8, D)
    cw = jnp.pad(gdn_conv_w.reshape(-1), (0, 5 * D)).reshape(8, D)
    vec = jnp.pad(jnp.stack(vecs), ((0, 3), (0, 0)))
    small = jnp.pad(jnp.concatenate([a_log, dt_bias, norm_g]), (0, D - 2 * NH - HD)).reshape(1, D)
    return jnp.pad(jnp.concatenate([dw, cw, vec, small], axis=0), ((0, REST_ROWS - R_SMALL - 1), (0, 0)))


def _pack_rest(conf_w_out, gdn_w_out, w_o, small):
    return jnp.concatenate([conf_w_out, gdn_w_out, w_o, small], axis=0)


def _unpack_rest(p):
    conf_dw_w = p[R_DW:R_DW + 8].reshape(-1)[:KC * SQ_BLK].reshape(KC, SQ_BLK)
    gdn_conv_w = p[R_CW:R_CW + 3].reshape(KG, 3 * SQ_BLK)
    small = p[R_SMALL]
    return dict(conf_w_out=p[0:SQ_BLK], gdn_w_out=p[SQ_BLK:2 * SQ_BLK], w_o=p[2 * SQ_BLK:R_DW],
                conf_dw_w=conf_dw_w, gdn_conv_w=gdn_conv_w, conf_dw_b=p[R_VEC], conf_ln_g=p[R_VEC + 1],
                conf_ln_b=p[R_VEC + 2], post_ln_g=p[R_VEC + 3], post_ln_b=p[R_VEC + 4],
                gdn_A_log=small[0:NH], gdn_dt_bias=small[NH:2 * NH], gdn_norm_g=small[2 * NH:2 * NH + HD])


_WEIGHT_ORDER = ("w_in", "conf_dw_w", "conf_dw_b", "conf_ln_g", "conf_ln_b", "conf_w_out", "gdn_conv_w",
                 "gdn_A_log", "gdn_dt_bias", "gdn_norm_g", "gdn_w_out", "w_o", "post_ln_g", "post_ln_b")


def _gather_weights(w_in, conf_w_out, gdn_w_out, w_o, conf_dw_w, gdn_conv_w):
    c = lax.axis_index("c")
    sq = jnp.concatenate([conf_w_out, gdn_w_out, w_o], axis=0).astype(bf16)
    w_half = lax.dynamic_slice_in_dim(w_in.astype(bf16), c * (D // 2), D // 2, axis=0)
    sq_half = lax.dynamic_slice_in_dim(sq, c * (sq.shape[0] // 2), sq.shape[0] // 2, axis=0)
    small = jnp.concatenate([jnp.pad(conf_dw_w.reshape(-1), (0, 8 * D - KC * SQ_BLK)).reshape(8, D),
                             jnp.pad(gdn_conv_w.reshape(-1), (0, 5 * D)).reshape(8, D)], axis=0)
    got_w, got_sq, small_all = _chip_exchange([w_half, sq_half, small], "weight_gather_chips", scatter=False)
    oth_w, oth_sq = _sibling_merge([got_w, got_sq], "weight_gather_sibling")
    w4 = _join_halves(got_w, oth_w)
    sq4 = _join_halves(got_sq, oth_sq)
    sq_full = [sq4[:, n * SQ_BLK:(n + 1) * SQ_BLK].reshape(D, D) for n in range(3)]
    dw_full = small_all[:, 0:8].reshape(N_CHIPS, 8 * D)[:, :KC * SQ_BLK].reshape(N_CHIPS, KC, SQ_BLK)
    dw_full = dw_full.transpose(1, 0, 2).reshape(KC, D)
    cw_full = small_all[:, 8:11].reshape(N_CHIPS, KG, 3 * SQ_BLK).transpose(1, 0, 2).reshape(KG, 3 * D)
    return w4, sq_full[0], sq_full[1], sq_full[2], dw_full, cw_full


def _w_in_cols(w4, lo, hi):
    parts = []
    for j in range(N_CHIPS):
        a, b = max(lo, j * W_IN_BLK), min(hi, (j + 1) * W_IN_BLK)
        if a < b:
            parts.append(w4[j, :, a - j * W_IN_BLK:b - j * W_IN_BLK])
    return parts[0] if len(parts) == 1 else jnp.concatenate(parts, axis=1)


def _w_in_by_chip(pieces):
    chips = []
    for j in range(N_CHIPS):
        lo, hi = j * W_IN_BLK, (j + 1) * W_IN_BLK
        parts = []
        for start, arr in pieces:
            a, b = max(lo, start), min(hi, start + arr.shape[1])
            if a < b:
                parts.append(arr[:, a - start:b - start])
        chips.append(jnp.concatenate(parts, axis=1))
    return jnp.stack(chips)


def _pair_sums(g_w, g_rest):
    c_arr = lax.axis_index("c").astype(jnp.int32).reshape(1)
    got_w, got_r = _sibling_merge([g_w, g_rest], "grad_sibling_halves", take_other_half=True)
    pair_w = _pair_sum(g_w, got_w, c_arr, "grad_pair_sum_w_in", bf16)
    pair_r = _pair_sum(g_rest, got_r, c_arr, "grad_pair_sum_rest", f32)
    return pair_w, pair_r


def _chip_sums(all_w, all_r):
    tot_w, tot_r = _sum_slots(all_w, "grad_chip_sum_w_in"), _sum_slots(all_r, "grad_chip_sum_rest")
    oth_w, oth_r = _sibling_merge([tot_w, tot_r], "grad_sibling_result")
    return _join_halves(tot_w, oth_w), _join_halves(tot_r, oth_r)


def _local_step(x2, tgt, w4, wc_out, wg_out, wo_full, dw_full, cw_full, conf_dw_b, conf_ln_g, conf_ln_b,
                gdn_A_log, gdn_dt_bias, gdn_norm_g, post_ln_g, post_ln_b):
    t = x2.shape[0]
    tt = min(TOKEN_TILE, t)
    tm = min(1024, t)

    w_conv, w_qkv, w_gz = _w_in_cols(w4, 0, 3 * D), _w_in_cols(w4, 3 * D, 6 * D), _w_in_cols(w4, 6 * D, 7 * D)
    w_gates = _w_in_cols(w4, 7 * D + 2 * NH, W_IN_COLS)
    dw_pad = jnp.pad(dw_full, ((0, HALO_C - KC), (0, 0)))
    cw_pad = jnp.pad(cw_full, ((0, 8 - KG), (0, 0)))
    row = lambda v: v.reshape(1, D)
    alog_v = jnp.pad(gdn_A_log, (NH, HD - 2 * NH)).reshape(1, HD)
    dt_v = jnp.pad(gdn_dt_bias, (NH, HD - 2 * NH)).reshape(1, HD)
    ng_b = row(jnp.tile(gdn_norm_g, NH))
    w_ba = jnp.pad(_w_in_cols(w4, 7 * D, 7 * D + 2 * NH), ((0, 0), (0, HD - 2 * NH)))

    x_b = x2.astype(bf16)

    p_conv = _mm_multi([x_b], [w_conv], out_dtype=f32, tm=tm, tn=1024, name="proj_conv")
    p_qkv = _mm_multi([x_b], [w_qkv], out_dtype=f32, tm=tm, tn=1024, name="proj_qkv")
    p_gz = _mm_multi([x_b], [w_gz], out_dtype=f32, tm=tm, tn=1024, name="proj_gz")
    p_gates = _mm_multi([x_b], [w_gates], out_dtype=f32, tm=tm, tn=1024, name="proj_gates")
    p_ba = _mm_multi([x_b], [w_ba], out_dtype=f32, tm=tm, tn=HD, name="proj_ba")

    u, a1 = _conv_fwd(p_conv, dw_pad, row(conf_dw_b), row(conf_ln_g), row(conf_ln_b), tt=tt)
    y_conf = _mm_multi([u], [wc_out], out_dtype=f32, tm=tm, tn=1024, name="conf_out")

    qn, kn, va, gates = _gdn_pre_fwd(p_qkv, p_ba, cw_pad, alog_v, dt_v, tt=tt)
    o, states, inverses = _gdn_scan_fwd(qn, kn, va, gates, tt=tt)
    og = _gdn_post_fwd(o, p_gz, ng_b, tt=tt)
    y_gdn = _mm_multi([og], [wg_out], out_dtype=f32, tm=tm, tn=1024, name="gdn_out")

    loss_blk, dxd, dyc, dyg, dp_gates, h, dz, dpost = _merge(
        x2, y_conf, y_gdn, p_gates, tgt, wo_full, row(post_ln_g), row(post_ln_b), tt=tt)

    d_wo = _mm_kloop(h, dz, tm=D, tn=1024, tk=min(512, t), name="grad_w_o")
    du = _mm_multi([dyc], [wc_out], out_dtype=f32, tm=tm, tn=1024, name="conf_out_bwd", rhs_t=True)
    d_wc = _mm_kloop(u, dyc, tm=D, tn=1024, tk=min(512, t), name="grad_conf_w_out")
    dog = _mm_multi([dyg], [wg_out], out_dtype=f32, tm=tm, tn=1024, name="gdn_out_bwd", rhs_t=True)
    d_wg = _mm_kloop(og, dyg, tm=D, tn=1024, tk=min(512, t), name="grad_gdn_w_out")

    dp_conv, d_dww, dconv_vec = _conv_bwd(p_conv, a1, du, dw_pad, row(conf_ln_g), row(conf_ln_b), tt=tt)

    do, dp_gz, dng = _gdn_post_bwd(o, p_gz, ng_b, dog, tt=tt)
    dqn, dkn, dva, dgates = _gdn_scan_bwd(qn, kn, va, gates, states, inverses, do, tt=tt)
    dp_qkv, dp_ba, d_cw, d_ad = _gdn_pre_bwd(p_qkv, p_ba, cw_pad, alog_v, dt_v, dqn, dkn, dva, dgates, tt=tt)
    dp_ba_b = dp_ba.astype(bf16)

    grad_x_factors = ([dp_conv, dp_qkv, dp_gz, dp_gates, dp_ba_b], [w_conv, w_qkv, w_gz, w_gates, w_ba], dxd)

    tk = min(512, t)
    d_w_conv = _mm_kloop(x_b, dp_conv, tm=D, tn=1024, tk=tk, name="grad_w_in_conv")
    d_w_qkv = _mm_kloop(x_b, dp_qkv, tm=D, tn=1024, tk=tk, name="grad_w_in_qkv")
    d_w_gz = _mm_kloop(x_b, dp_gz, tm=D, tn=1024, tk=tk, name="grad_w_in_gz")
    d_w_gates = _mm_kloop(x_b, dp_gates, tm=D, tn=1024, tk=tk, name="grad_w_in_gates")
    d_w_ba = _mm_kloop(x_b, dp_ba_b, tm=D, tn=HD, tk=tk, name="grad_w_in_ba")
    d_w_in = _w_in_by_chip([(0, d_w_conv), (3 * D, d_w_qkv), (6 * D, d_w_gz), (7 * D, d_w_ba[:, :2 * NH]),
                            (7 * D + 2 * NH, d_w_gates)])

    return (loss_blk[0, 0], grad_x_factors, d_w_in, d_wc, d_wg, d_wo, d_dww, d_cw, dconv_vec, dpost, d_ad, dng)


def kernel(x, w_in, conf_dw_w, conf_dw_b, conf_ln_g, conf_ln_b, conf_w_out, gdn_conv_w, gdn_A_log, gdn_dt_bias, gdn_norm_g, gdn_w_out, w_o, post_ln_g, post_ln_b, loss_target, m_w_in, m_conf_dw_w, m_conf_dw_b, m_conf_ln_g, m_conf_ln_b, m_conf_w_out, m_gdn_conv_w, m_gdn_A_log, m_gdn_dt_bias, m_gdn_norm_g, m_gdn_w_out, m_w_o, m_post_ln_g, m_post_ln_b, v_w_in, v_conf_dw_w, v_conf_dw_b, v_conf_ln_g, v_conf_ln_b, v_conf_w_out, v_gdn_conv_w, v_gdn_A_log, v_gdn_dt_bias, v_gdn_norm_g, v_gdn_w_out, v_w_o, v_post_ln_g, v_post_ln_b):
    x2 = x.reshape(x.shape[-2], D)
    tgt = loss_target.reshape(x2.shape)
    w4, wc_out, wg_out, wo_full, dw_full, cw_full = _gather_weights(
        w_in, conf_w_out, gdn_w_out, w_o, conf_dw_w, gdn_conv_w)
    (loss_part, grad_x_factors, d_w_in, d_wc, d_wg, d_wo, d_dww, d_cw, dconv_vec, dpost, d_ad, dng) = _local_step(
        x2, tgt, w4, wc_out, wg_out, wo_full, dw_full, cw_full, conf_dw_b, conf_ln_g, conf_ln_b,
        gdn_A_log, gdn_dt_bias, gdn_norm_g, post_ln_g, post_ln_b)
    loss = lax.psum(loss_part, ("x", "y", "c"))

    dww_c = d_dww[:KC].reshape(KC, N_CHIPS, SQ_BLK)
    dcw_c = d_cw[:KG].reshape(KG, N_CHIPS, 3 * SQ_BLK)
    vecs = [dconv_vec[0], dconv_vec[1], dconv_vec[2], dpost[0], dpost[1]]
    g_rest = jnp.stack([
        _pack_rest(d_wc[j * SQ_BLK:(j + 1) * SQ_BLK], d_wg[j * SQ_BLK:(j + 1) * SQ_BLK], d_wo[j * SQ_BLK:(j + 1) * SQ_BLK],
                   _pack_small(dww_c[:, j], dcw_c[:, j], vecs, d_ad[0, NH:2 * NH], d_ad[1, NH:2 * NH], dng[0]))
        for j in range(N_CHIPS)])
    pair_w, pair_r = _pair_sums(d_w_in, g_rest)
    grad_x, all_w, all_r = _mm_multi(*grad_x_factors, out_dtype=f32, tm=min(256, x2.shape[0]), tn=512,
                                     name="grad_x_and_chip_scatter", rhs_t=True, scatter=[pair_w, pair_r])
    g_w_in, g_rest = _chip_sums(all_w, all_r)

    def rest_of(w_c, w_g, w_oo, dw, cw, b1, g1, b2, g2, b3, a_log, dt_bias, norm_g):
        return _pack_rest(w_c, w_g, w_oo, _pack_small(dw, cw, [b1, g1, b2, g2, b3], a_log, dt_bias, norm_g))

    w_r = rest_of(conf_w_out, gdn_w_out, w_o, conf_dw_w, gdn_conv_w, conf_dw_b, conf_ln_g, conf_ln_b,
                  post_ln_g, post_ln_b, gdn_A_log, gdn_dt_bias, gdn_norm_g)
    m_r = rest_of(m_conf_w_out, m_gdn_w_out, m_w_o, m_conf_dw_w, m_gdn_conv_w, m_conf_dw_b, m_conf_ln_g, m_conf_ln_b,
                  m_post_ln_g, m_post_ln_b, m_gdn_A_log, m_gdn_dt_bias, m_gdn_norm_g)
    v_r = rest_of(v_conf_w_out, v_gdn_w_out, v_w_o, v_conf_dw_w, v_gdn_conv_w, v_conf_dw_b, v_conf_ln_g, v_conf_ln_b,
                  v_post_ln_g, v_post_ln_b, v_gdn_A_log, v_gdn_dt_bias, v_gdn_norm_g)
    upd_w_in = _adamw(w_in, g_w_in, m_w_in, v_w_in, "adamw_w_in")
    upd_rest = _adamw(w_r, g_rest, m_r, v_r, "adamw_rest")

    out = [loss, grad_x.reshape(x.shape)]
    for big, rest in zip((g_w_in,) + tuple(upd_w_in), (g_rest,) + tuple(upd_rest)):
        d = dict(_unpack_rest(rest), w_in=big)
        out += [d[n] for n in _WEIGHT_ORDER]
    return tuple(out)
```

```python
import functools

import jax
import jax.numpy as jnp
from jax import lax
from jax.experimental import pallas as pl
from jax.experimental.pallas import tpu as pltpu

f32 = jnp.float32
bf16 = jnp.bfloat16
HI = lax.Precision.HIGHEST
MESH = pl.DeviceIdType.MESH

D = 1024
NH = 8
HD = 128
CH = 64
KC = 31
KG = 4
HALO_C = 32
HALO_G = 8
LANE = 128
STRIP = 32
N_SHIFT = 7
LN_EPS = 1e-5
RMS_EPS = 1e-6
L2_EPS = 1e-6
DN_ALPHA = 2.0 ** 0.25
N_CHIPS = 4
W_IN_COLS = 9232
W_IN_BLK = W_IN_COLS // N_CHIPS
SQ_BLK = D // N_CHIPS
VMEM_LIMIT = 52 * 1024 * 1024
TOKEN_TILE = 256

ADAM_LR = 0.001
ADAM_B1 = 0.9
ADAM_B2 = 0.999
ADAM_EPS = 1e-08
ADAM_WD = 0.01
ADAM_STEP = 10


def _sigmoid(x):
    return 1.0 / (1.0 + jnp.exp(-x))


def _silu_and_grad(x):
    s = _sigmoid(x)
    return x * s, s * (1.0 + x * (1.0 - s))


_NN = ((1,), (0,))
_NT = ((1,), (1,))
_TN = ((0,), (0,))


def _cparams(*sem):
    return pltpu.CompilerParams(dimension_semantics=sem, vmem_limit_bytes=VMEM_LIMIT)


def _mm_multi(a_list, b_list, addend=None, *, out_dtype, tm, tn, name, rhs_t=False, scatter=()):
    n_pairs = len(a_list)
    m = a_list[0].shape[0]
    n = b_list[0].shape[0 if rhs_t else 1]
    has_add = addend is not None
    dims = (_NT if rhs_t else _NN, ((), ()))
    n_in = 2 * n_pairs + has_add
    k = len(scatter)
    grid = (n // tn, m // tm)

    def body(*refs):
        a_refs = refs[:n_pairs]
        b_refs = refs[n_pairs:2 * n_pairs]
        o_ref = refs[n_in + k]
        if k:
            start, finish = _chip_exchange_ops(refs[n_in:n_in + k], refs[n_in + k + 1:n_in + 2 * k + 1],
                                               *refs[n_in + 2 * k + 1:], True)
            step = pl.program_id(0) * grid[1] + pl.program_id(1)
            pl.when(step == 0)(start)
        acc = None
        for a_ref, b_ref in zip(a_refs, b_refs):
            p = lax.dot_general(a_ref[...].astype(bf16), b_ref[...].astype(bf16), dims, preferred_element_type=f32)
            acc = p if acc is None else acc + p
        if has_add:
            acc = acc + refs[2 * n_pairs][...]
        o_ref[...] = acc.astype(out_dtype)
        if k:
            pl.when(step == grid[0] * grid[1] - 1)(finish)

    in_specs = [pl.BlockSpec((tm, a.shape[1]), lambda j, i: (i, 0)) for a in a_list]
    if rhs_t:
        in_specs += [pl.BlockSpec((tn, b.shape[1]), lambda j, i: (j, 0)) for b in b_list]
    else:
        in_specs += [pl.BlockSpec((b.shape[0], tn), lambda j, i: (0, j)) for b in b_list]
    args = list(a_list) + list(b_list)
    if has_add:
        in_specs.append(pl.BlockSpec((tm, tn), lambda j, i: (i, j)))
        args.append(addend)
    out = pl.pallas_call(
        body, name=name, grid=grid,
        in_specs=in_specs + _any_specs(k), out_specs=[pl.BlockSpec((tm, tn), lambda j, i: (i, j))] + _any_specs(k),
        out_shape=[jax.ShapeDtypeStruct((m, n), out_dtype)] + _chip_exchange_shapes(scatter, True),
        scratch_shapes=_chip_exchange_sems(k) if k else [],
        compiler_params=_cparams("arbitrary", "arbitrary") if k else _cparams("parallel", "parallel"),
    )(*args, *scatter)
    return out if k else out[0]


def _mm_kloop(a, b, *, tm, tn, tk, name):
    k, m = a.shape
    n = b.shape[1]
    nk = k // tk

    def body(a_ref, b_ref, o_ref):
        @pl.when(pl.program_id(2) == 0)
        def _():
            o_ref[...] = jnp.zeros_like(o_ref)
        o_ref[...] += lax.dot_general(a_ref[...].astype(bf16), b_ref[...].astype(bf16), (_TN, ((), ())),
                                      preferred_element_type=f32)

    return pl.pallas_call(
        body, name=name, grid=(n // tn, m // tm, nk),
        in_specs=[pl.BlockSpec((tk, tm), lambda j, i, kk: (kk, i)), pl.BlockSpec((tk, tn), lambda j, i, kk: (kk, j))],
        out_specs=pl.BlockSpec((tm, tn), lambda j, i, kk: (i, j)),
        out_shape=jax.ShapeDtypeStruct((m, n), f32),
        compiler_params=_cparams("parallel", "parallel", "arbitrary"),
    )(a, b)


def _shift_copies(src_ref, sh_ref, n, shifts=tuple(range(1, 8))):
    for i, b in enumerate(shifts):
        sh_ref[i, 0:n, :] = src_ref[pl.ds(b, n), :]


def _by_residue(offs):
    groups = {}
    for k, off in enumerate(offs):
        groups.setdefault(off % 8, []).append((k, off // 8))
    return groups


def _slab(src_ref, sh_ref, shifts, b, r0, n, lanes):
    ref = src_ref if b == 0 else sh_ref.at[shifts.index(b)]
    return ref[r0:r0 + n, lanes]


def _tap_conv(out_ref, n_rows, src_ref, sh_ref, w_ref, offs, bias_ref=None, shifts=tuple(range(1, 8))):
    groups = _by_residue(offs)
    for j in range(D // LANE):
        lanes = slice(j * LANE, (j + 1) * LANE)
        wv = [w_ref[k:k + 1, lanes] for k in range(len(offs))]
        for r0 in range(0, n_rows, STRIP):
            n = min(STRIP, n_rows - r0)
            accs = [jnp.zeros((n, LANE), f32) if bias_ref is None else jnp.broadcast_to(bias_ref[0:1, lanes], (n, LANE)),
                    jnp.zeros((n, LANE), f32)]
            m = 0
            for b, taps in groups.items():
                a_lo = min(a for _, a in taps)
                a_hi = max(a for _, a in taps)
                wide = _slab(src_ref, sh_ref, shifts, b, r0 + 8 * a_lo, 8 * (a_hi - a_lo) + n, lanes)
                for k, a in taps:
                    accs[m % 2] = accs[m % 2] + wv[k] * wide[8 * (a - a_lo):8 * (a - a_lo) + n]
                    m += 1
            out_ref[r0:r0 + n, lanes] = accs[0] + accs[1]


def _tap_corr(dw_ref, n_rows, lhs_ref, src_ref, sh_ref, offs, shifts=tuple(range(1, 8))):
    groups = _by_residue(offs)
    for j in range(D // LANE):
        lanes = slice(j * LANE, (j + 1) * LANE)
        accs = [jnp.zeros((8, LANE), f32) for _ in offs]
        for r0 in range(0, n_rows, STRIP):
            n = min(STRIP, n_rows - r0)
            d = lhs_ref[r0:r0 + n, lanes]
            for b, taps in groups.items():
                a_lo = min(a for _, a in taps)
                a_hi = max(a for _, a in taps)
                wide = _slab(src_ref, sh_ref, shifts, b, r0 + 8 * a_lo, 8 * (a_hi - a_lo) + n, lanes)
                for k, a in taps:
                    prod = d * wide[8 * (a - a_lo):8 * (a - a_lo) + n]
                    part = prod[0:8]
                    for q in range(1, n // 8):
                        part = part + prod[8 * q:8 * q + 8]
                    accs[k] = accs[k] + part
        for k in range(len(offs)):
            dw_ref[k:k + 1, lanes] += jnp.sum(accs[k], axis=0, keepdims=True)


_FWD_OFFS = [HALO_C - (KC - 1) + k for k in range(KC)]
_BWD_OFFS = [KC - 1 - k for k in range(KC)]


def _norm_act(a1, cz, g_ref, bb_ref):
    mu = jnp.mean(a1, axis=-1, keepdims=True)
    cen = a1 - mu
    var = jnp.mean(cen * cen, axis=-1, keepdims=True)
    rstd = lax.rsqrt(var + LN_EPS)
    xhat = cen * rstd
    ln = xhat * g_ref[...] + bb_ref[...]
    s, ds = _silu_and_grad(ln)
    zc, dzc = _silu_and_grad(cz)
    return xhat, rstd, s, ds, zc, dzc


def _conv_fwd(p_conv, dw_w, dw_b, ln_g, ln_b, *, tt):
    t = p_conv.shape[0]
    hb = tt // HALO_C

    def body(cv_ref, cg_ref, cz_ref, cvh_ref, cgh_ref, w_ref, b_ref, g_ref, bb_ref, u_ref, a1_ref, ext_ref, sh_ref):
        first = pl.program_id(0) == 0
        halo = cvh_ref[...] * _sigmoid(cgh_ref[...])
        ext_ref[0:HALO_C, :] = jnp.where(first, 0.0, halo)
        ext_ref[HALO_C:, :] = cv_ref[...] * _sigmoid(cg_ref[...])
        _shift_copies(ext_ref, sh_ref, tt + HALO_C - 8)
        _tap_conv(a1_ref, tt, ext_ref, sh_ref, w_ref, _FWD_OFFS, b_ref)
        _, _, s, _, zc, _ = _norm_act(a1_ref[...], cz_ref[...], g_ref, bb_ref)
        u_ref[...] = (s * zc).astype(bf16)

    def main(col):
        return pl.BlockSpec((tt, D), lambda i: (i, col))

    def prev(col):
        return pl.BlockSpec((HALO_C, D), lambda i: (jnp.maximum(i * hb - 1, 0), col))

    vec = pl.BlockSpec((1, D), lambda i: (0, 0))
    return pl.pallas_call(
        body, name="conv_fwd", grid=(t // tt,),
        in_specs=[main(0), main(1), main(2), prev(0), prev(1),
                  pl.BlockSpec((HALO_C, D), lambda i: (0, 0)), vec, vec, vec],
        out_specs=[pl.BlockSpec((tt, D), lambda i: (i, 0))] * 2,
        out_shape=[jax.ShapeDtypeStruct((t, D), bf16), jax.ShapeDtypeStruct((t, D), f32)],
        scratch_shapes=[pltpu.VMEM((tt + HALO_C, D), f32), pltpu.VMEM((N_SHIFT, tt + HALO_C - 8, D), f32)],
        compiler_params=_cparams("parallel"),
    )(p_conv, p_conv, p_conv, p_conv, p_conv, dw_w, dw_b, ln_g, ln_b)


def _conv_bwd(p_conv, a1, du, dw_w, ln_g, ln_b, *, tt):
    t = p_conv.shape[0]
    hb = tt // HALO_C
    n_tiles = t // tt
    last_hb = t // HALO_C - 1
    ne = tt + HALO_C

    def body(cv_ref, cg_ref, cz_ref, a1_ref, du_ref, cvp_ref, cgp_ref, czn_ref, a1n_ref, dun_ref,
             w_ref, g_ref, bb_ref, dp_ref, dww_ref, dvec_ref, ext_ref, sh_ref, da1_ref, da0_ref):
        i = pl.program_id(0)
        first = i == 0
        last = i == n_tiles - 1

        @pl.when(first)
        def _():
            dww_ref[...] = jnp.zeros_like(dww_ref)
            dvec_ref[...] = jnp.zeros_like(dvec_ref)

        sig = _sigmoid(cg_ref[...])
        ext_ref[0:HALO_C, :] = jnp.where(first, 0.0, cvp_ref[...] * _sigmoid(cgp_ref[...]))
        ext_ref[HALO_C:, :] = cv_ref[...] * sig
        a1_all = jnp.concatenate([a1_ref[...], a1n_ref[...]], axis=0)
        cz = jnp.concatenate([cz_ref[...], czn_ref[...]], axis=0)
        du_all = jnp.concatenate([du_ref[...], jnp.where(last, 0.0, dun_ref[...])], axis=0)
        xhat, rstd, s, ds, zc, dzc = _norm_act(a1_all, cz, g_ref, bb_ref)
        dln = du_all * zc * ds
        dxhat = dln * g_ref[...]
        da1 = rstd * (dxhat - jnp.mean(dxhat, axis=-1, keepdims=True)
                      - xhat * jnp.mean(dxhat * xhat, axis=-1, keepdims=True))
        da1_ref[...] = da1
        dcz = (du_all * s * dzc)[:tt]
        dvec_ref[0:1, :] += jnp.sum(da1[:tt], axis=0, keepdims=True)
        dvec_ref[1:2, :] += jnp.sum((dln * xhat)[:tt], axis=0, keepdims=True)
        dvec_ref[2:3, :] += jnp.sum(dln[:tt], axis=0, keepdims=True)
        _shift_copies(ext_ref, sh_ref, ne - 8)
        _tap_corr(dww_ref, tt, da1_ref, ext_ref, sh_ref, _FWD_OFFS)
        _shift_copies(da1_ref, sh_ref, ne - 8)
        _tap_conv(da0_ref, tt, da1_ref, sh_ref, w_ref, _BWD_OFFS)
        da0 = da0_ref[...]
        cv = cv_ref[...]
        dp_ref[:, 0:D] = (da0 * sig).astype(bf16)
        dp_ref[:, D:2 * D] = (da0 * cv * sig * (1.0 - sig)).astype(bf16)
        dp_ref[:, 2 * D:] = dcz.astype(bf16)

    def main(col):
        return pl.BlockSpec((tt, D), lambda i: (i, col))

    def prev(col):
        return pl.BlockSpec((HALO_C, D), lambda i: (jnp.maximum(i * hb - 1, 0), col))

    def nxt(col):
        return pl.BlockSpec((HALO_C, D), lambda i: (jnp.minimum((i + 1) * hb, last_hb), col))

    vec = pl.BlockSpec((1, D), lambda i: (0, 0))
    return pl.pallas_call(
        body, name="conv_bwd", grid=(n_tiles,),
        in_specs=[main(0), main(1), main(2), main(0), main(0), prev(0), prev(1), nxt(2), nxt(0), nxt(0),
                  pl.BlockSpec((HALO_C, D), lambda i: (0, 0)), vec, vec],
        out_specs=[pl.BlockSpec((tt, 3 * D), lambda i: (i, 0)),
                   pl.BlockSpec((HALO_C, D), lambda i: (0, 0)),
                   pl.BlockSpec((8, D), lambda i: (0, 0))],
        out_shape=[jax.ShapeDtypeStruct((t, 3 * D), bf16), jax.ShapeDtypeStruct((HALO_C, D), f32),
                   jax.ShapeDtypeStruct((8, D), f32)],
        scratch_shapes=[pltpu.VMEM((ne, D), f32), pltpu.VMEM((N_SHIFT, ne - 8, D), f32),
                        pltpu.VMEM((ne, D), f32), pltpu.VMEM((tt, D), f32)],
        compiler_params=_cparams("arbitrary"),
    )(p_conv, p_conv, p_conv, a1, du, p_conv, p_conv, p_conv, a1, du, dw_w, ln_g, ln_b)


def _dot_hi(a, b):
    return lax.dot_general(a, b, (((1,), (0,)), ((), ())), precision=HI, preferred_element_type=f32)


def _chunk_tri(n, lower):
    r = lax.broadcasted_iota(jnp.int32, (n, n), 0)
    c = lax.broadcasted_iota(jnp.int32, (n, n), 1)
    tri = (r >= c) if lower else (r <= c)
    return jnp.where(tri & (r // CH == c // CH), 1.0, 0.0).astype(f32)


def _softplus_and_sigmoid(x):
    e = jnp.exp(-jnp.abs(x))
    log1p = jnp.where(e < 1e-2, e * (1.0 - e * (0.5 - e * (1.0 / 3.0 - 0.25 * e))), jnp.log(1.0 + e))
    return jnp.maximum(x, 0.0) + log1p, _sigmoid(x)


_G_FWD_OFFS = [HALO_G - (KG - 1) + k for k in range(KG)]
_G_FWD_SHIFTS = (5, 6, 7)
_G_BWD_OFFS = [KG - 1 - k for k in range(KG)]
_G_BWD_SHIFTS = (1, 2, 3)


def _gdn_short_conv(pre_ref, ext_ref, sh_ref, n_rows, w_ref):
    _shift_copies(ext_ref, sh_ref, n_rows, _G_FWD_SHIFTS)
    _tap_conv(pre_ref, n_rows, ext_ref, sh_ref, w_ref, _G_FWD_OFFS, shifts=_G_FWD_SHIFTS)
    return pre_ref[...]


def _l2norm_heads(act, scale):
    outs, rs = [], []
    for h in range(NH):
        a = act[:, h * HD:(h + 1) * HD]
        r = lax.rsqrt(jnp.sum(a * a, axis=-1, keepdims=True) + L2_EPS)
        outs.append(a * (r * scale))
        rs.append(jnp.broadcast_to(r, a.shape))
    return jnp.concatenate(outs, axis=-1), jnp.concatenate(rs, axis=-1)


def _gate_math(ba, al_ref, dt_ref):
    lane = lax.broadcasted_iota(jnp.int32, ba.shape, 1)
    is_b = lane < NH
    is_a = (lane >= NH) & (lane < 2 * NH)
    sp, sg = _softplus_and_sigmoid(ba + dt_ref[...])
    neg_a = -jnp.exp(al_ref[...])
    return is_b, is_a, _sigmoid(ba), neg_a * sp, sg, neg_a


def _gdn_pre_fwd(p_qkv, p_ba, cw, alog_v, dt_v, *, tt):
    t = p_qkv.shape[0]
    hb = tt // HALO_G

    def body(q_ref, k_ref, v_ref, qh_ref, kh_ref, vh_ref, ba_ref, wq_ref, wk_ref, wv_ref, al_ref, dt_ref,
             qn_ref, kn_ref, va_ref, gt_ref, ext_ref, sh_ref, pre_ref):
        first = pl.program_id(0) == 0

        def conv_act(x_ref, xh_ref, w_ref):
            ext_ref[0:HALO_G, :] = jnp.where(first, 0.0, xh_ref[...])
            ext_ref[HALO_G:, :] = x_ref[...]
            pre = _gdn_short_conv(pre_ref, ext_ref, sh_ref, tt, w_ref)
            return pre * _sigmoid(pre)

        qn_ref[...] = _l2norm_heads(conv_act(q_ref, qh_ref, wq_ref), HD ** -0.5)[0]
        kn_ref[...] = _l2norm_heads(conv_act(k_ref, kh_ref, wk_ref), 1.0)[0]
        va_ref[...] = conv_act(v_ref, vh_ref, wv_ref)
        is_b, is_a, beta, g, _, _ = _gate_math(ba_ref[...], al_ref, dt_ref)
        gc = _dot_hi(_chunk_tri(tt, lower=True), jnp.where(is_a, g, 0.0))
        gt_ref[...] = jnp.where(is_b, beta, gc)

    def main(col):
        return pl.BlockSpec((tt, D), lambda i: (i, col))

    def prev(col):
        return pl.BlockSpec((HALO_G, D), lambda i: (jnp.maximum(i * hb - 1, 0), col))

    def wspec(col):
        return pl.BlockSpec((8, D), lambda i: (0, col))

    vec = pl.BlockSpec((1, HD), lambda i: (0, 0))
    gblk = pl.BlockSpec((tt, HD), lambda i: (i, 0))
    sds = jax.ShapeDtypeStruct((t, D), f32)
    return pl.pallas_call(
        body, name="gdn_pre_fwd", grid=(t // tt,),
        in_specs=[main(0), main(1), main(2), prev(0), prev(1), prev(2), gblk, wspec(0), wspec(1), wspec(2), vec, vec],
        out_specs=[pl.BlockSpec((tt, D), lambda i: (i, 0))] * 3 + [gblk],
        out_shape=[sds] * 3 + [jax.ShapeDtypeStruct((t, HD), f32)],
        scratch_shapes=[pltpu.VMEM((tt + HALO_G, D), f32), pltpu.VMEM((KG - 1, tt, D), f32), pltpu.VMEM((tt, D), f32)],
        compiler_params=_cparams("parallel"),
    )(p_qkv, p_qkv, p_qkv, p_qkv, p_qkv, p_qkv, p_ba, cw, cw, cw, alog_v, dt_v)


def _gdn_pre_bwd(p_qkv, p_ba, cw, alog_v, dt_v, dqn, dkn, dva, dgt, *, tt):
    t = p_qkv.shape[0]
    hb = tt // HALO_G
    n_tiles = t // tt
    last_hb = t // HALO_G - 1
    ne = tt + HALO_G

    def body(q_ref, k_ref, v_ref, qp_ref, kp_ref, vp_ref, qx_ref, kx_ref, vx_ref,
             dq_ref, dk_ref, dv_ref, dqx_ref, dkx_ref, dvx_ref, ba_ref, dgt_ref,
             wq_ref, wk_ref, wv_ref, al_ref, dt_ref,
             dp_ref, dba_ref, dcw_ref, dad_ref, ext_ref, sh_ref, pre_ref, dpre_ref, draw_ref):
        i = pl.program_id(0)
        first = i == 0
        last = i == n_tiles - 1

        @pl.when(first)
        def _():
            dcw_ref[...] = jnp.zeros_like(dcw_ref)
            dad_ref[...] = jnp.zeros_like(dad_ref)

        def one(x_ref, xp_ref, xx_ref, d_ref, dx_ref, w_ref, col, scale):
            ext_ref[0:HALO_G, :] = jnp.where(first, 0.0, xp_ref[...])
            ext_ref[HALO_G:HALO_G + tt, :] = x_ref[...]
            ext_ref[HALO_G + tt:, :] = xx_ref[...]
            pre = _gdn_short_conv(pre_ref, ext_ref, sh_ref, ne, w_ref)
            act, dact = _silu_and_grad(pre)
            d_out = jnp.concatenate([d_ref[...], jnp.where(last, 0.0, dx_ref[...])], axis=0)
            if scale is None:
                d_act = d_out
            else:
                parts = []
                for h in range(NH):
                    a = act[:, h * HD:(h + 1) * HD]
                    dn = d_out[:, h * HD:(h + 1) * HD]
                    r = lax.rsqrt(jnp.sum(a * a, axis=-1, keepdims=True) + L2_EPS)
                    parts.append(scale * r * (dn - a * (r * r) * jnp.sum(dn * a, axis=-1, keepdims=True)))
                d_act = jnp.concatenate(parts, axis=-1)
            dpre_ref[...] = d_act * dact
            _tap_corr(dcw_ref.at[:, col * D:(col + 1) * D], tt, dpre_ref, ext_ref, sh_ref, _G_FWD_OFFS, shifts=_G_FWD_SHIFTS)
            _shift_copies(dpre_ref, sh_ref, tt, _G_BWD_SHIFTS)
            _tap_conv(draw_ref, tt, dpre_ref, sh_ref, w_ref, _G_BWD_OFFS, shifts=_G_BWD_SHIFTS)
            dp_ref[:, col * D:(col + 1) * D] = draw_ref[...].astype(bf16)

        one(q_ref, qp_ref, qx_ref, dq_ref, dqx_ref, wq_ref, 0, HD ** -0.5)
        one(k_ref, kp_ref, kx_ref, dk_ref, dkx_ref, wk_ref, 1, 1.0)
        one(v_ref, vp_ref, vx_ref, dv_ref, dvx_ref, wv_ref, 2, None)

        is_b, is_a, beta, g, sg, neg_a = _gate_math(ba_ref[...], al_ref, dt_ref)
        dgt_v = dgt_ref[...]
        dg = _dot_hi(_chunk_tri(tt, lower=False), jnp.where(is_a, dgt_v, 0.0))
        d_al = jnp.where(is_a, dg * neg_a * sg, 0.0)
        dba_ref[...] = jnp.where(is_b, dgt_v * beta * (1.0 - beta), d_al)
        dad_ref[0:1, :] += jnp.sum(jnp.where(is_a, dg * g, 0.0), axis=0, keepdims=True)
        dad_ref[1:2, :] += jnp.sum(d_al, axis=0, keepdims=True)

    def main(col):
        return pl.BlockSpec((tt, D), lambda i: (i, col))

    def prev(col):
        return pl.BlockSpec((HALO_G, D), lambda i: (jnp.maximum(i * hb - 1, 0), col))

    def nxt(col):
        return pl.BlockSpec((HALO_G, D), lambda i: (jnp.minimum((i + 1) * hb, last_hb), col))

    def wspec(col):
        return pl.BlockSpec((8, D), lambda i: (0, col))

    vec = pl.BlockSpec((1, HD), lambda i: (0, 0))
    gblk = pl.BlockSpec((tt, HD), lambda i: (i, 0))
    return pl.pallas_call(
        body, name="gdn_pre_bwd", grid=(n_tiles,),
        in_specs=[main(0), main(1), main(2), prev(0), prev(1), prev(2), nxt(0), nxt(1), nxt(2),
                  main(0), main(0), main(0), nxt(0), nxt(0), nxt(0), gblk, gblk,
                  wspec(0), wspec(1), wspec(2), vec, vec],
        out_specs=[pl.BlockSpec((tt, 3 * D), lambda i: (i, 0)), pl.BlockSpec((tt, HD), lambda i: (i, 0)),
                   pl.BlockSpec((8, 3 * D), lambda i: (0, 0)), pl.BlockSpec((8, HD), lambda i: (0, 0))],
        out_shape=[jax.ShapeDtypeStruct((t, 3 * D), bf16), jax.ShapeDtypeStruct((t, HD), f32),
                   jax.ShapeDtypeStruct((8, 3 * D), f32), jax.ShapeDtypeStruct((8, HD), f32)],
        scratch_shapes=[pltpu.VMEM((HALO_G + tt + HALO_G, D), f32), pltpu.VMEM((KG - 1, ne, D), f32),
                        pltpu.VMEM((ne, D), f32), pltpu.VMEM((ne, D), f32), pltpu.VMEM((tt, D), f32)],
        compiler_params=_cparams("arbitrary"),
    )(p_qkv, p_qkv, p_qkv, p_qkv, p_qkv, p_qkv, p_qkv, p_qkv, p_qkv,
      dqn, dkn, dva, dqn, dkn, dva, p_ba, dgt, cw, cw, cw, alog_v, dt_v)


def _dot_b(a, b, dims):
    return lax.dot_general(a.astype(bf16), b.astype(bf16), (dims, ((), ())), preferred_element_type=f32)


def _inverse_by_doubling(ms):
    heads = range(len(ms))
    r = lax.broadcasted_iota(jnp.int32, (CH, CH), 0)
    c = lax.broadcasted_iota(jnp.int32, (CH, CH), 1)
    eye = jnp.where(r == c, 1.0, 0.0).astype(f32)
    p = [eye + ms[h] for h in heads]
    mp = ms
    for _ in range(5):
        mp = [_dot_b(mp[h], mp[h], _NN) for h in heads]
        pm = [_dot_b(p[h], mp[h], _NN) for h in heads]
        p = [p[h] + pm[h] for h in heads]
    return tuple(p)


@jax.custom_vjp
def _known_inverse(ms, ps):
    return ps


def _known_inverse_fwd(ms, ps):
    return ps, ps


def _known_inverse_bwd(ps, cts):
    heads = range(len(ps))
    left = [_dot_b(ps[h], cts[h], _TN) for h in heads]
    return tuple(_dot_b(left[h], ps[h], _NT) for h in heads), tuple(jnp.zeros_like(p) for p in ps)


_known_inverse.defvjp(_known_inverse_fwd, _known_inverse_bwd)


def _chunk_fn(qs, ks, vs, gcs, bbs, ss, ps=None):
    heads = range(len(qs))
    r = lax.broadcasted_iota(jnp.int32, (CH, CH), 0)
    c = lax.broadcasted_iota(jnp.int32, (CH, CH), 1)
    causal = r >= c
    strict = r > c
    gc_row = [gcs[h].T[:CH, :] for h in heads]
    decay = [jnp.where(causal, jnp.exp(jnp.where(causal, gcs[h][:, :CH] - gc_row[h], 0.0)), 0.0) for h in heads]
    kb = [ks[h] * bbs[h] for h in heads]
    egc = [jnp.exp(gcs[h]) for h in heads]
    kk = [_dot_b(kb[h], ks[h], _NT) for h in heads]
    qk = [_dot_b(qs[h], ks[h], _NT) for h in heads]
    m = tuple(-jnp.where(strict, kk[h] * decay[h], 0.0) for h in heads)
    p = _inverse_by_doubling(m) if ps is None else _known_inverse(m, ps)
    u = [_dot_b(p[h], vs[h] * bbs[h], _NN) for h in heads]
    w = [_dot_b(p[h], kb[h] * egc[h], _NN) for h in heads]
    intra = [jnp.where(causal, qk[h] * decay[h], 0.0) for h in heads]
    g_last = [gcs[h][CH - 1:CH, :] for h in heads]
    k_dec = [ks[h] * jnp.exp(g_last[h] - gcs[h]) for h in heads]
    ws = [_dot_b(w[h], ss[h], _NN) for h in heads]
    qs_s = [_dot_b(qs[h] * egc[h], ss[h], _NN) for h in heads]
    v_new = [u[h] - ws[h] for h in heads]
    iv = [_dot_b(intra[h], v_new[h], _NN) for h in heads]
    kv = [_dot_b(k_dec[h], v_new[h], _TN) for h in heads]
    o = tuple(qs_s[h] + iv[h] for h in heads)
    s_new = tuple(ss[h] * jnp.exp(g_last[h]) + kv[h] for h in heads)
    return o, s_new, p


def _head_cols():
    return [slice(h * HD, (h + 1) * HD) for h in range(NH)]


def _head_gates(gt):
    gcs = tuple(jnp.broadcast_to(gt[:, NH + h:NH + h + 1], (CH, HD)) for h in range(NH))
    bbs = tuple(jnp.broadcast_to(gt[:, h:h + 1], (CH, HD)) for h in range(NH))
    return gcs, bbs


def _gdn_scan_fwd(qn, kn, va, gates, *, tt):
    t = qn.shape[0]
    cpb = tt // CH

    def body(q_ref, k_ref, v_ref, gt_ref, o_ref, st_ref, p_ref, s_scr):
        @pl.when(pl.program_id(0) == 0)
        def _():
            s_scr[...] = jnp.zeros_like(s_scr)

        def step(ci, carry):
            rows = pl.ds(pl.multiple_of(ci * CH, CH), CH)
            cols = _head_cols()
            ss = tuple(s_scr[h] for h in range(NH))
            for h in range(NH):
                st_ref[ci, h] = ss[h]
            gcs, bbs = _head_gates(gt_ref[rows, :])
            o, s_new, p = _chunk_fn(*(tuple(ref[rows, cl] for cl in cols) for ref in (q_ref, k_ref, v_ref)), gcs, bbs, ss)
            for h in range(NH):
                o_ref[rows, cols[h]] = o[h]
                s_scr[h] = s_new[h]
                p_ref[ci, h] = p[h].astype(bf16)
            return carry

        lax.fori_loop(0, cpb, step, 0)

    blk = pl.BlockSpec((tt, D), lambda i: (i, 0))
    return pl.pallas_call(
        body, name="gdn_scan_fwd", grid=(t // tt,),
        in_specs=[blk] * 3 + [pl.BlockSpec((tt, HD), lambda i: (i, 0))],
        out_specs=[blk, pl.BlockSpec((cpb, NH, HD, HD), lambda i: (i, 0, 0, 0)),
                   pl.BlockSpec((cpb, NH, CH, CH), lambda i: (i, 0, 0, 0))],
        out_shape=[jax.ShapeDtypeStruct((t, D), f32), jax.ShapeDtypeStruct((t // CH, NH, HD, HD), f32),
                   jax.ShapeDtypeStruct((t // CH, NH, CH, CH), bf16)],
        scratch_shapes=[pltpu.VMEM((NH, HD, HD), f32)],
        compiler_params=_cparams("arbitrary"),
    )(qn, kn, va, gates)


def _gdn_scan_bwd(qn, kn, va, gates, states, inverses, do, *, tt):
    t = qn.shape[0]
    nblk = t // tt
    cpb = tt // CH

    def body(q_ref, k_ref, v_ref, gt_ref, st_ref, p_ref, do_ref, dq_ref, dk_ref, dv_ref, dgt_ref, ds_scr):
        @pl.when(pl.program_id(0) == 0)
        def _():
            ds_scr[...] = jnp.zeros_like(ds_scr)

        def step(j, carry):
            ci = cpb - 1 - j
            rows = pl.ds(pl.multiple_of(ci * CH, CH), CH)
            cols = _head_cols()
            gcs, bbs = _head_gates(gt_ref[rows, :])
            ps = tuple(p_ref[ci, h].astype(f32) for h in range(NH))
            _, vjp = jax.vjp(lambda *a: _chunk_fn(*a, ps=ps)[:2],
                             *(tuple(ref[rows, cl] for cl in cols) for ref in (q_ref, k_ref, v_ref)),
                             gcs, bbs, tuple(st_ref[ci, h] for h in range(NH)))
            grads = vjp((tuple(do_ref[rows, cl] for cl in cols), tuple(ds_scr[h] for h in range(NH))))
            lane = lax.broadcasted_iota(jnp.int32, (CH, HD), 1)
            dgt = jnp.zeros((CH, HD), f32)
            for h in range(NH):
                for ref, g in zip((dq_ref, dk_ref, dv_ref), grads[:3]):
                    ref[rows, cols[h]] = g[h]
                dgt = dgt + jnp.where(lane == NH + h, jnp.sum(grads[3][h], axis=-1, keepdims=True), 0.0)
                dgt = dgt + jnp.where(lane == h, jnp.sum(grads[4][h], axis=-1, keepdims=True), 0.0)
                ds_scr[h] = grads[5][h]
            dgt_ref[rows, :] = dgt
            return carry

        lax.fori_loop(0, cpb, step, 0)

    blk = pl.BlockSpec((tt, D), lambda i: (nblk - 1 - i, 0))
    sblk = pl.BlockSpec((cpb, NH, HD, HD), lambda i: (nblk - 1 - i, 0, 0, 0))
    sds = jax.ShapeDtypeStruct((t, D), f32)
    gblk = pl.BlockSpec((tt, HD), lambda i: (nblk - 1 - i, 0))
    pblk = pl.BlockSpec((cpb, NH, CH, CH), lambda i: (nblk - 1 - i, 0, 0, 0))
    return pl.pallas_call(
        body, name="gdn_scan_bwd", grid=(nblk,),
        in_specs=[blk] * 3 + [gblk, sblk, pblk, blk],
        out_specs=[blk] * 3 + [gblk], out_shape=[sds] * 3 + [jax.ShapeDtypeStruct((t, HD), f32)],
        scratch_shapes=[pltpu.VMEM((NH, HD, HD), f32)],
        compiler_params=_cparams("arbitrary"),
    )(qn, kn, va, gates, states, inverses, do)


def _rms_heads(o):
    ons, rs = [], []
    for h in range(NH):
        a = o[:, h * HD:(h + 1) * HD]
        r = lax.rsqrt(jnp.mean(a * a, axis=-1, keepdims=True) + RMS_EPS)
        ons.append(a * r)
        rs.append(jnp.broadcast_to(r, a.shape))
    return jnp.concatenate(ons, axis=-1), jnp.concatenate(rs, axis=-1)


def _gdn_post_fwd(o, p_gz, ng_b, *, tt):
    t = o.shape[0]

    def body(o_ref, gz_ref, ng_ref, og_ref):
        on, _ = _rms_heads(o_ref[...])
        z, _ = _silu_and_grad(gz_ref[...])
        og_ref[...] = (on * ng_ref[...] * z).astype(bf16)

    blk = pl.BlockSpec((tt, D), lambda i: (i, 0))
    return pl.pallas_call(
        body, name="gdn_post_fwd", grid=(t // tt,),
        in_specs=[blk, blk, pl.BlockSpec((1, D), lambda i: (0, 0))],
        out_specs=blk, out_shape=jax.ShapeDtypeStruct((t, D), bf16),
        compiler_params=_cparams("parallel"),
    )(o, p_gz, ng_b)


def _gdn_post_bwd(o, p_gz, ng_b, dog, *, tt):
    t = o.shape[0]

    def body(o_ref, gz_ref, ng_ref, dog_ref, do_ref, dgz_ref, dng_ref):
        @pl.when(pl.program_id(0) == 0)
        def _():
            dng_ref[...] = jnp.zeros_like(dng_ref)

        on, r = _rms_heads(o_ref[...])
        z, dz = _silu_and_grad(gz_ref[...])
        dog_v = dog_ref[...]
        ng = ng_ref[...]
        dgz_ref[...] = (dog_v * on * ng * dz).astype(bf16)
        dy = dog_v * z
        dng_all = jnp.sum(dy * on, axis=0, keepdims=True)
        dng = dng_all[:, 0:HD]
        for h in range(1, NH):
            dng = dng + dng_all[:, h * HD:(h + 1) * HD]
        dng_ref[0:1, :] += dng
        don = dy * ng
        prod = don * on
        parts = []
        for h in range(NH):
            sl = slice(h * HD, (h + 1) * HD)
            parts.append(don[:, sl] - on[:, sl] * jnp.mean(prod[:, sl], axis=-1, keepdims=True))
        do_ref[...] = r * jnp.concatenate(parts, axis=-1)

    blk = pl.BlockSpec((tt, D), lambda i: (i, 0))
    return pl.pallas_call(
        body, name="gdn_post_bwd", grid=(t // tt,),
        in_specs=[blk, blk, pl.BlockSpec((1, D), lambda i: (0, 0)), blk],
        out_specs=[blk, blk, pl.BlockSpec((8, HD), lambda i: (0, 0))],
        out_shape=[jax.ShapeDtypeStruct((t, D), f32), jax.ShapeDtypeStruct((t, D), bf16),
                   jax.ShapeDtypeStruct((8, HD), f32)],
        compiler_params=_cparams("arbitrary"),
    )(o, p_gz, ng_b, dog)


def _merge(x, y_conf, y_gdn, p_gates, target, w_o, ln_g, ln_b, *, tt):
    t = x.shape[0]

    def body(x_ref, yc_ref, yg_ref, gc_ref, gg_ref, tg_ref, w_ref, g_ref, b_ref,
             loss_ref, dxd_ref, dyc_ref, dyg_ref, dpg_ref, h_ref, dz_ref, dvec_ref):
        @pl.when(pl.program_id(0) == 0)
        def _():
            loss_ref[...] = jnp.zeros_like(loss_ref)
            dvec_ref[...] = jnp.zeros_like(dvec_ref)

        sc = _sigmoid(gc_ref[...])
        sg = _sigmoid(gg_ref[...])
        yc = yc_ref[...]
        yg = yg_ref[...]
        h = (sc * yc + sg * yg).astype(bf16)
        h_ref[...] = h
        z = DN_ALPHA * x_ref[...] + jnp.dot(h, w_ref[...], preferred_element_type=f32)
        mu = jnp.mean(z, axis=-1, keepdims=True)
        cen = z - mu
        rstd = lax.rsqrt(jnp.mean(cen * cen, axis=-1, keepdims=True) + LN_EPS)
        xhat = cen * rstd
        err = xhat * g_ref[...] + b_ref[...] - tg_ref[...]
        loss_ref[...] += 0.5 / D * jnp.sum(err * err)
        dy = err * (1.0 / D)
        dvec_ref[0:1, :] += jnp.sum(dy * xhat, axis=0, keepdims=True)
        dvec_ref[1:2, :] += jnp.sum(dy, axis=0, keepdims=True)
        dxhat = dy * g_ref[...]
        dz = rstd * (dxhat - jnp.mean(dxhat, axis=-1, keepdims=True)
                     - xhat * jnp.mean(dxhat * xhat, axis=-1, keepdims=True))
        dxd_ref[...] = DN_ALPHA * dz
        dz_b = dz.astype(bf16)
        dz_ref[...] = dz_b
        dh = lax.dot_general(dz_b, w_ref[...], (_NT, ((), ())), preferred_element_type=f32)
        dyc_ref[...] = (dh * sc).astype(bf16)
        dyg_ref[...] = (dh * sg).astype(bf16)
        dpg_ref[:, 0:D] = (dh * yc * sc * (1.0 - sc)).astype(bf16)
        dpg_ref[:, D:] = (dh * yg * sg * (1.0 - sg)).astype(bf16)

    blk = pl.BlockSpec((tt, D), lambda i: (i, 0))
    wblk = pl.BlockSpec((D, D), lambda i: (0, 0))
    vec = pl.BlockSpec((1, D), lambda i: (0, 0))
    return pl.pallas_call(
        body, name="merge_norm_loss", grid=(t // tt,),
        in_specs=[blk, blk, blk, pl.BlockSpec((tt, D), lambda i: (i, 0)), pl.BlockSpec((tt, D), lambda i: (i, 1)),
                  blk, wblk, vec, vec],
        out_specs=[pl.BlockSpec((8, HD), lambda i: (0, 0)), blk, blk, blk,
                   pl.BlockSpec((tt, 2 * D), lambda i: (i, 0)), blk, blk, pl.BlockSpec((8, D), lambda i: (0, 0))],
        out_shape=[jax.ShapeDtypeStruct((8, HD), f32), jax.ShapeDtypeStruct((t, D), f32),
                   jax.ShapeDtypeStruct((t, D), bf16), jax.ShapeDtypeStruct((t, D), bf16),
                   jax.ShapeDtypeStruct((t, 2 * D), bf16), jax.ShapeDtypeStruct((t, D), bf16),
                   jax.ShapeDtypeStruct((t, D), bf16), jax.ShapeDtypeStruct((8, D), f32)],
        compiler_params=_cparams("arbitrary"),
    )(x, y_conf, y_gdn, p_gates, p_gates, target, w_o, ln_g, ln_b)


def _place():
    return lax.axis_index("x"), lax.axis_index("y"), lax.axis_index("c")


def _any_specs(n):
    return [pl.BlockSpec(memory_space=pl.ANY)] * n


def _sibling_merge(arrs, name, take_other_half=False):
    k = len(arrs)

    def body(*refs):
        a_refs, o_refs = refs[:k], refs[k:2 * k]
        send_sems, recv_sems = refs[2 * k:]
        x, y, c = _place()
        sends = []
        for i in range(k):
            src = a_refs[i]
            if take_other_half:
                n = a_refs[i].shape[-2] // 2
                lead = (slice(None),) * (len(a_refs[i].shape) - 2)
                src = a_refs[i].at[lead + (pl.ds((1 - c) * n, n), slice(None))]
            cp = pltpu.make_async_remote_copy(src_ref=src, dst_ref=o_refs[i], send_sem=send_sems.at[i],
                                              recv_sem=recv_sems.at[i], device_id=(x, y, 1 - c), device_id_type=MESH)
            cp.start()
            sends.append(cp)
        for cp in sends:
            cp.wait()

    def out_sds(a):
        rows = a.shape[-2] // 2 if take_other_half else a.shape[-2]
        return jax.ShapeDtypeStruct(a.shape[:-2] + (rows, a.shape[-1]), a.dtype)

    return pl.pallas_call(
        body, name=name, in_specs=_any_specs(k), out_specs=_any_specs(k),
        out_shape=[out_sds(a) for a in arrs],
        scratch_shapes=[pltpu.SemaphoreType.DMA((k,)), pltpu.SemaphoreType.DMA((k,))],
    )(*arrs)


def _join_halves(mine, other):
    c = lax.axis_index("c")
    return jnp.concatenate([jnp.where(c == 0, mine, other), jnp.where(c == 0, other, mine)], axis=-2)


def _chip_exchange_ops(a_refs, o_refs, send_sems, recv_sems, local_sems, scatter):
    k = len(a_refs)
    x, y, c = _place()
    me = 2 * x + y
    peers = [(1 - x, y), (x, 1 - y), (1 - x, 1 - y)]

    def src(i, j):
        return a_refs[i].at[j] if scatter else a_refs[i]

    def copy(i, n, send_j, slot):
        px, py = peers[n]
        return pltpu.make_async_remote_copy(
            src_ref=src(i, send_j), dst_ref=o_refs[i].at[slot], send_sem=send_sems.at[3 * i + n],
            recv_sem=recv_sems.at[3 * i + n], device_id=(px, py, c), device_id_type=MESH)

    def owns():
        return [pltpu.make_async_copy(src(i, me), o_refs[i].at[me], local_sems.at[i]) for i in range(k)]

    def sends():
        return [copy(i, n, 2 * peers[n][0] + peers[n][1], me) for n in range(3) for i in range(k)]

    def start():
        for cp in owns() + sends():
            cp.start()

    def finish():
        for n in range(3):
            for i in range(k):
                copy(i, n, me, 2 * peers[n][0] + peers[n][1]).wait_recv()
        for cp in sends():
            cp.wait_send()
        for cp in owns():
            cp.wait()

    return start, finish


def _chip_exchange_shapes(arrs, scatter):
    return [jax.ShapeDtypeStruct((N_CHIPS,) + tuple(a.shape[1:] if scatter else a.shape), a.dtype) for a in arrs]


def _chip_exchange_sems(k):
    return [pltpu.SemaphoreType.DMA((3 * k,)), pltpu.SemaphoreType.DMA((3 * k,)), pltpu.SemaphoreType.DMA((k,))]


def _chip_exchange(arrs, name, scatter):
    k = len(arrs)

    def body(*refs):
        start, finish = _chip_exchange_ops(refs[:k], refs[k:2 * k], *refs[2 * k:], scatter)
        start()
        finish()

    return pl.pallas_call(
        body, name=name, in_specs=_any_specs(k), out_specs=_any_specs(k),
        out_shape=_chip_exchange_shapes(arrs, scatter), scratch_shapes=_chip_exchange_sems(k),
    )(*arrs)


def _pair_sum(g_all, got, c_arr, name, out_dtype):
    n, w = got.shape[1:]
    tile = n // 4
    n_tiles = n // tile

    def body(c_ref, a_ref, b_ref, o_ref):
        o_ref[...] = (a_ref[...] + b_ref[...]).astype(out_dtype)

    return pl.pallas_call(
        body, name=name,
        grid_spec=pltpu.PrefetchScalarGridSpec(
            num_scalar_prefetch=1, grid=(N_CHIPS, n_tiles),
            in_specs=[pl.BlockSpec((1, tile, w), lambda j, i, c_ref: (j, c_ref[0] * n_tiles + i, 0)),
                      pl.BlockSpec((1, tile, w), lambda j, i, c_ref: (j, i, 0))],
            out_specs=pl.BlockSpec((1, tile, w), lambda j, i, c_ref: (j, i, 0))),
        out_shape=jax.ShapeDtypeStruct(got.shape, out_dtype),
        compiler_params=_cparams("parallel", "parallel"),
    )(c_arr, g_all, got)


def _sum_slots(a, name):
    n, w = a.shape[1:]
    tile = n // 4

    def body(a_ref, o_ref):
        o_ref[...] = ((a_ref[0].astype(f32) + a_ref[1].astype(f32)) + a_ref[2].astype(f32)) + a_ref[3].astype(f32)

    return pl.pallas_call(
        body, name=name, grid=(n // tile,),
        in_specs=[pl.BlockSpec((N_CHIPS, tile, w), lambda i: (0, i, 0))],
        out_specs=pl.BlockSpec((tile, w), lambda i: (i, 0)),
        out_shape=jax.ShapeDtypeStruct((n, w), f32),
        compiler_params=_cparams("parallel"),
    )(a)


def _adamw(w, g, m, v, name):
    rows, width = w.shape
    by_rows = rows % 64 == 0
    c1 = 1.0 / (1.0 - ADAM_B1 ** ADAM_STEP)
    c2 = 1.0 / (1.0 - ADAM_B2 ** ADAM_STEP)

    def body(w_ref, g_ref, m_ref, v_ref, d_ref, mo_ref, vo_ref):
        g_v = g_ref[...]
        m_new = ADAM_B1 * m_ref[...] + (1.0 - ADAM_B1) * g_v
        v_new = ADAM_B2 * v_ref[...] + (1.0 - ADAM_B2) * (g_v * g_v)
        mo_ref[...] = m_new
        vo_ref[...] = v_new
        d_ref[...] = -ADAM_LR * ((m_new * c1) / (jnp.sqrt(v_new * c2) + ADAM_EPS) + ADAM_WD * w_ref[...])

    blk = pl.BlockSpec((rows // 8, width), lambda i: (i, 0)) if by_rows else pl.BlockSpec((rows, LANE), lambda i: (0, i))
    sds = jax.ShapeDtypeStruct((rows, width), f32)
    return pl.pallas_call(
        body, name=name, grid=(8 if by_rows else width // LANE,),
        in_specs=[blk] * 4, out_specs=[blk] * 3, out_shape=[sds] * 3,
        compiler_params=_cparams("parallel"),
    )(w, g, m, v)


R_DW = 3 * SQ_BLK
R_CW = R_DW + 8
R_VEC = R_CW + 8
R_SMALL = R_VEC + 8
REST_ROWS = 896


def _pack_small(conf_dw_w, gdn_conv_w, vecs, a_log, dt_bias, norm_g):
    dw = jnp.pad(conf_dw_w.reshape(-1), (0, 8 * D - KC * SQ_BLK)).reshape(8, D)
    cw = jnp.pad(gdn_conv_w.reshape(-1), (0, 5 * D)).reshape(8, D)
    vec = jnp.pad(jnp.stack(vecs), ((0, 3), (0, 0)))
    small = jnp.pad(jnp.concatenate([a_log, dt_bias, norm_g]), (0, D - 2 * NH - HD)).reshape(1, D)
    return jnp.pad(jnp.concatenate([dw, cw, vec, small], axis=0), ((0, REST_ROWS - R_SMALL - 1), (0, 0)))


def _pack_rest(conf_w_out, gdn_w_out, w_o, small):
    return jnp.concatenate([conf_w_out, gdn_w_out, w_o, small], axis=0)


def _unpack_rest(p):
    conf_dw_w = p[R_DW:R_DW + 8].reshape(-1)[:KC * SQ_BLK].reshape(KC, SQ_BLK)
    gdn_conv_w = p[R_CW:R_CW + 3].reshape(KG, 3 * SQ_BLK)
    small = p[R_SMALL]
    return dict(conf_w_out=p[0:SQ_BLK], gdn_w_out=p[SQ_BLK:2 * SQ_BLK], w_o=p[2 * SQ_BLK:R_DW],
                conf_dw_w=conf_dw_w, gdn_conv_w=gdn_conv_w, conf_dw_b=p[R_VEC], conf_ln_g=p[R_VEC + 1],
                conf_ln_b=p[R_VEC + 2], post_ln_g=p[R_VEC + 3], post_ln_b=p[R_VEC + 4],
                gdn_A_log=small[0:NH], gdn_dt_bias=small[NH:2 * NH], gdn_norm_g=small[2 * NH:2 * NH + HD])


_WEIGHT_ORDER = ("w_in", "conf_dw_w", "conf_dw_b", "conf_ln_g", "conf_ln_b", "conf_w_out", "gdn_conv_w",
                 "gdn_A_log", "gdn_dt_bias", "gdn_norm_g", "gdn_w_out", "w_o", "post_ln_g", "post_ln_b")


def _gather_weights(w_in, conf_w_out, gdn_w_out, w_o, conf_dw_w, gdn_conv_w):
    c = lax.axis_index("c")
    sq = jnp.concatenate([conf_w_out, gdn_w_out, w_o], axis=0).astype(bf16)
    w_half = lax.dynamic_slice_in_dim(w_in.astype(bf16), c * (D // 2), D // 2, axis=0)
    sq_half = lax.dynamic_slice_in_dim(sq, c * (sq.shape[0] // 2), sq.shape[0] // 2, axis=0)
    small = jnp.concatenate([jnp.pad(conf_dw_w.reshape(-1), (0, 8 * D - KC * SQ_BLK)).reshape(8, D),
                             jnp.pad(gdn_conv_w.reshape(-1), (0, 5 * D)).reshape(8, D)], axis=0)
    got_w, got_sq, small_all = _chip_exchange([w_half, sq_half, small], "weight_gather_chips", scatter=False)
    oth_w, oth_sq = _sibling_merge([got_w, got_sq], "weight_gather_sibling")
    w4 = _join_halves(got_w, oth_w)
    sq4 = _join_halves(got_sq, oth_sq)
    sq_full = [sq4[:, n * SQ_BLK:(n + 1) * SQ_BLK].reshape(D, D) for n in range(3)]
    dw_full = small_all[:, 0:8].reshape(N_CHIPS, 8 * D)[:, :KC * SQ_BLK].reshape(N_CHIPS, KC, SQ_BLK)
    dw_full = dw_full.transpose(1, 0, 2).reshape(KC, D)
    cw_full = small_all[:, 8:11].reshape(N_CHIPS, KG, 3 * SQ_BLK).transpose(1, 0, 2).reshape(KG, 3 * D)
    return w4, sq_full[0], sq_full[1], sq_full[2], dw_full, cw_full


def _w_in_cols(w4, lo, hi):
    parts = []
    for j in range(N_CHIPS):
        a, b = max(lo, j * W_IN_BLK), min(hi, (j + 1) * W_IN_BLK)
        if a < b:
            parts.append(w4[j, :, a - j * W_IN_BLK:b - j * W_IN_BLK])
    return parts[0] if len(parts) == 1 else jnp.concatenate(parts, axis=1)


def _w_in_by_chip(pieces):
    chips = []
    for j in range(N_CHIPS):
        lo, hi = j * W_IN_BLK, (j + 1) * W_IN_BLK
        parts = []
        for start, arr in pieces:
            a, b = max(lo, start), min(hi, start + arr.shape[1])
            if a < b:
                parts.append(arr[:, a - start:b - start])
        chips.append(jnp.concatenate(parts, axis=1))
    return jnp.stack(chips)


def _pair_sums(g_w, g_rest):
    c_arr = lax.axis_index("c").astype(jnp.int32).reshape(1)
    got_w, got_r = _sibling_merge([g_w, g_rest], "grad_sibling_halves", take_other_half=True)
    pair_w = _pair_sum(g_w, got_w, c_arr, "grad_pair_sum_w_in", bf16)
    pair_r = _pair_sum(g_rest, got_r, c_arr, "grad_pair_sum_rest", f32)
    return pair_w, pair_r


def _chip_sums(all_w, all_r):
    tot_w, tot_r = _sum_slots(all_w, "grad_chip_sum_w_in"), _sum_slots(all_r, "grad_chip_sum_rest")
    oth_w, oth_r = _sibling_merge([tot_w, tot_r], "grad_sibling_result")
    return _join_halves(tot_w, oth_w), _join_halves(tot_r, oth_r)


def _local_step(x2, tgt, w4, wc_out, wg_out, wo_full, dw_full, cw_full, conf_dw_b, conf_ln_g, conf_ln_b,
                gdn_A_log, gdn_dt_bias, gdn_norm_g, post_ln_g, post_ln_b):
    t = x2.shape[0]
    tt = min(TOKEN_TILE, t)
    tm = min(1024, t)

    w_conv, w_qkv, w_gz = _w_in_cols(w4, 0, 3 * D), _w_in_cols(w4, 3 * D, 6 * D), _w_in_cols(w4, 6 * D, 7 * D)
    w_gates = _w_in_cols(w4, 7 * D + 2 * NH, W_IN_COLS)
    dw_pad = jnp.pad(dw_full, ((0, HALO_C - KC), (0, 0)))
    cw_pad = jnp.pad(cw_full, ((0, 8 - KG), (0, 0)))
    row = lambda v: v.reshape(1, D)
    alog_v = jnp.pad(gdn_A_log, (NH, HD - 2 * NH)).reshape(1, HD)
    dt_v = jnp.pad(gdn_dt_bias, (NH, HD - 2 * NH)).reshape(1, HD)
    ng_b = row(jnp.tile(gdn_norm_g, NH))
    w_ba = jnp.pad(_w_in_cols(w4, 7 * D, 7 * D + 2 * NH), ((0, 0), (0, HD - 2 * NH)))

    x_b = x2.astype(bf16)

    p_conv = _mm_multi([x_b], [w_conv], out_dtype=f32, tm=tm, tn=1024, name="proj_conv")
    p_qkv = _mm_multi([x_b], [w_qkv], out_dtype=f32, tm=tm, tn=1024, name="proj_qkv")
    p_gz = _mm_multi([x_b], [w_gz], out_dtype=f32, tm=tm, tn=1024, name="proj_gz")
    p_gates = _mm_multi([x_b], [w_gates], out_dtype=f32, tm=tm, tn=1024, name="proj_gates")
    p_ba = _mm_multi([x_b], [w_ba], out_dtype=f32, tm=tm, tn=HD, name="proj_ba")

    u, a1 = _conv_fwd(p_conv, dw_pad, row(conf_dw_b), row(conf_ln_g), row(conf_ln_b), tt=tt)
    y_conf = _mm_multi([u], [wc_out], out_dtype=f32, tm=tm, tn=1024, name="conf_out")

    qn, kn, va, gates = _gdn_pre_fwd(p_qkv, p_ba, cw_pad, alog_v, dt_v, tt=tt)
    o, states, inverses = _gdn_scan_fwd(qn, kn, va, gates, tt=tt)
    og = _gdn_post_fwd(o, p_gz, ng_b, tt=tt)
    y_gdn = _mm_multi([og], [wg_out], out_dtype=f32, tm=tm, tn=1024, name="gdn_out")

    loss_blk, dxd, dyc, dyg, dp_gates, h, dz, dpost = _merge(
        x2, y_conf, y_gdn, p_gates, tgt, wo_full, row(post_ln_g), row(post_ln_b), tt=tt)

    d_wo = _mm_kloop(h, dz, tm=D, tn=1024, tk=min(512, t), name="grad_w_o")
    du = _mm_multi([dyc], [wc_out], out_dtype=f32, tm=tm, tn=1024, name="conf_out_bwd", rhs_t=True)
    d_wc = _mm_kloop(u, dyc, tm=D, tn=1024, tk=min(512, t), name="grad_conf_w_out")
    dog = _mm_multi([dyg], [wg_out], out_dtype=f32, tm=tm, tn=1024, name="gdn_out_bwd", rhs_t=True)
    d_wg = _mm_kloop(og, dyg, tm=D, tn=1024, tk=min(512, t), name="grad_gdn_w_out")

    dp_conv, d_dww, dconv_vec = _conv_bwd(p_conv, a1, du, dw_pad, row(conf_ln_g), row(conf_ln_b), tt=tt)

    do, dp_gz, dng = _gdn_post_bwd(o, p_gz, ng_b, dog, tt=tt)
    dqn, dkn, dva, dgates = _gdn_scan_bwd(qn, kn, va, gates, states, inverses, do, tt=tt)
    dp_qkv, dp_ba, d_cw, d_ad = _gdn_pre_bwd(p_qkv, p_ba, cw_pad, alog_v, dt_v, dqn, dkn, dva, dgates, tt=tt)
    dp_ba_b = dp_ba.astype(bf16)

    grad_x_factors = ([dp_conv, dp_qkv, dp_gz, dp_gates, dp_ba_b], [w_conv, w_qkv, w_gz, w_gates, w_ba], dxd)

    tk = min(512, t)
    d_w_conv = _mm_kloop(x_b, dp_conv, tm=D, tn=1024, tk=tk, name="grad_w_in_conv")
    d_w_qkv = _mm_kloop(x_b, dp_qkv, tm=D, tn=1024, tk=tk, name="grad_w_in_qkv")
    d_w_gz = _mm_kloop(x_b, dp_gz, tm=D, tn=1024, tk=tk, name="grad_w_in_gz")
    d_w_gates = _mm_kloop(x_b, dp_gates, tm=D, tn=1024, tk=tk, name="grad_w_in_gates")
    d_w_ba = _mm_kloop(x_b, dp_ba_b, tm=D, tn=HD, tk=tk, name="grad_w_in_ba")
    d_w_in = _w_in_by_chip([(0, d_w_conv), (3 * D, d_w_qkv), (6 * D, d_w_gz), (7 * D, d_w_ba[:, :2 * NH]),
                            (7 * D + 2 * NH, d_w_gates)])

    return (loss_blk[0, 0], grad_x_factors, d_w_in, d_wc, d_wg, d_wo, d_dww, d_cw, dconv_vec, dpost, d_ad, dng)


def kernel(x, w_in, conf_dw_w, conf_dw_b, conf_ln_g, conf_ln_b, conf_w_out, gdn_conv_w, gdn_A_log, gdn_dt_bias, gdn_norm_g, gdn_w_out, w_o, post_ln_g, post_ln_b, loss_target, m_w_in, m_conf_dw_w, m_conf_dw_b, m_conf_ln_g, m_conf_ln_b, m_conf_w_out, m_gdn_conv_w, m_gdn_A_log, m_gdn_dt_bias, m_gdn_norm_g, m_gdn_w_out, m_w_o, m_post_ln_g, m_post_ln_b, v_w_in, v_conf_dw_w, v_conf_dw_b, v_conf_ln_g, v_conf_ln_b, v_conf_w_out, v_gdn_conv_w, v_gdn_A_log, v_gdn_dt_bias, v_gdn_norm_g, v_gdn_w_out, v_w_o, v_post_ln_g, v_post_ln_b):
    x2 = x.reshape(x.shape[-2], D)
    tgt = loss_target.reshape(x2.shape)
    w4, wc_out, wg_out, wo_full, dw_full, cw_full = _gather_weights(
        w_in, conf_w_out, gdn_w_out, w_o, conf_dw_w, gdn_conv_w)
    (loss_part, grad_x_factors, d_w_in, d_wc, d_wg, d_wo, d_dww, d_cw, dconv_vec, dpost, d_ad, dng) = _local_step(
        x2, tgt, w4, wc_out, wg_out, wo_full, dw_full, cw_full, conf_dw_b, conf_ln_g, conf_ln_b,
        gdn_A_log, gdn_dt_bias, gdn_norm_g, post_ln_g, post_ln_b)
    loss = lax.psum(loss_part, ("x", "y", "c"))

    dww_c = d_dww[:KC].reshape(KC, N_CHIPS, SQ_BLK)
    dcw_c = d_cw[:KG].reshape(KG, N_CHIPS, 3 * SQ_BLK)
    vecs = [dconv_vec[0], dconv_vec[1], dconv_vec[2], dpost[0], dpost[1]]
    g_rest = jnp.stack([
        _pack_rest(d_wc[j * SQ_BLK:(j + 1) * SQ_BLK], d_wg[j * SQ_BLK:(j + 1) * SQ_BLK], d_wo[j * SQ_BLK:(j + 1) * SQ_BLK],
                   _pack_small(dww_c[:, j], dcw_c[:, j], vecs, d_ad[0, NH:2 * NH], d_ad[1, NH:2 * NH], dng[0]))
        for j in range(N_CHIPS)])
    pair_w, pair_r = _pair_sums(d_w_in, g_rest)
    grad_x, all_w, all_r = _mm_multi(*grad_x_factors, out_dtype=f32, tm=min(256, x2.shape[0]), tn=512,
                                     name="grad_x_and_chip_scatter", rhs_t=True, scatter=[pair_w, pair_r])
    g_w_in, g_rest = _chip_sums(all_w, all_r)

    def rest_of(w_c, w_g, w_oo, dw, cw, b1, g1, b2, g2, b3, a_log, dt_bias, norm_g):
        return _pack_rest(w_c, w_g, w_oo, _pack_small(dw, cw, [b1, g1, b2, g2, b3], a_log, dt_bias, norm_g))

    w_r = rest_of(conf_w_out, gdn_w_out, w_o, conf_dw_w, gdn_conv_w, conf_dw_b, conf_ln_g, conf_ln_b,
                  post_ln_g, post_ln_b, gdn_A_log, gdn_dt_bias, gdn_norm_g)
    m_r = rest_of(m_conf_w_out, m_gdn_w_out, m_w_o, m_conf_dw_w, m_gdn_conv_w, m_conf_dw_b, m_conf_ln_g, m_conf_ln_b,
                  m_post_ln_g, m_post_ln_b, m_gdn_A_log, m_gdn_dt_bias, m_gdn_norm_g)
    v_r = rest_of(v_conf_w_out, v_gdn_w_out, v_w_o, v_conf_dw_w, v_gdn_conv_w, v_conf_dw_b, v_conf_ln_g, v_conf_ln_b,
                  v_post_ln_g, v_post_ln_b, v_gdn_A_log, v_gdn_dt_bias, v_gdn_norm_g)
    upd_w_in = [u.T for u in _adamw(w_in.T, g_w_in.T, m_w_in.T, v_w_in.T, "adamw_w_in")]
    upd_rest = _adamw(w_r, g_rest, m_r, v_r, "adamw_rest")

    out = [loss, grad_x.reshape(x.shape)]
    for big, rest in zip((g_w_in,) + tuple(upd_w_in), (g_rest,) + tuple(upd_rest)):
        d = dict(_unpack_rest(rest), w_in=big)
        out += [d[n] for n in _WEIGHT_ORDER]
    return tuple(out)
```

```python
import functools

import jax
import jax.numpy as jnp
from jax import lax
from jax.experimental import pallas as pl
from jax.experimental.pallas import tpu as pltpu

f32 = jnp.float32
bf16 = jnp.bfloat16
HI = lax.Precision.HIGHEST
MESH = pl.DeviceIdType.MESH

D = 1024
NH = 8
HD = 128
CH = 64
KC = 31
KG = 4
HALO_C = 32
HALO_G = 8
LANE = 128
STRIP = 32
N_SHIFT = 7
LN_EPS = 1e-5
RMS_EPS = 1e-6
L2_EPS = 1e-6
DN_ALPHA = 2.0 ** 0.25
N_CHIPS = 4
W_IN_COLS = 9232
W_IN_BLK = W_IN_COLS // N_CHIPS
SQ_BLK = D // N_CHIPS
VMEM_LIMIT = 52 * 1024 * 1024
TOKEN_TILE = 256
SCAN_GROUP = 4

ADAM_LR = 0.001
ADAM_B1 = 0.9
ADAM_B2 = 0.999
ADAM_EPS = 1e-08
ADAM_WD = 0.01
ADAM_STEP = 10


def _sigmoid(x):
    return 1.0 / (1.0 + jnp.exp(-x))


def _silu_and_grad(x):
    s = _sigmoid(x)
    return x * s, s * (1.0 + x * (1.0 - s))


_NN = ((1,), (0,))
_NT = ((1,), (1,))
_TN = ((0,), (0,))


def _cparams(*sem):
    return pltpu.CompilerParams(dimension_semantics=sem, vmem_limit_bytes=VMEM_LIMIT)


def _mm_multi(a_list, b_list, addend=None, *, out_dtype, tm, tn, name, rhs_t=False, scatter=()):
    n_pairs = len(a_list)
    m = a_list[0].shape[0]
    n = b_list[0].shape[0 if rhs_t else 1]
    has_add = addend is not None
    dims = (_NT if rhs_t else _NN, ((), ()))
    n_in = 2 * n_pairs + has_add
    k = len(scatter)
    grid = (n // tn, m // tm)

    def body(*refs):
        a_refs = refs[:n_pairs]
        b_refs = refs[n_pairs:2 * n_pairs]
        o_ref = refs[n_in + k]
        if k:
            start, finish = _chip_exchange_ops(refs[n_in:n_in + k], refs[n_in + k + 1:n_in + 2 * k + 1],
                                               *refs[n_in + 2 * k + 1:], True)
            step = pl.program_id(0) * grid[1] + pl.program_id(1)
            pl.when(step == 0)(start)
        acc = None
        for a_ref, b_ref in zip(a_refs, b_refs):
            p = lax.dot_general(a_ref[...].astype(bf16), b_ref[...].astype(bf16), dims, preferred_element_type=f32)
            acc = p if acc is None else acc + p
        if has_add:
            acc = acc + refs[2 * n_pairs][...]
        o_ref[...] = acc.astype(out_dtype)
        if k:
            pl.when(step == grid[0] * grid[1] - 1)(finish)

    in_specs = [pl.BlockSpec((tm, a.shape[1]), lambda j, i: (i, 0)) for a in a_list]
    if rhs_t:
        in_specs += [pl.BlockSpec((tn, b.shape[1]), lambda j, i: (j, 0)) for b in b_list]
    else:
        in_specs += [pl.BlockSpec((b.shape[0], tn), lambda j, i: (0, j)) for b in b_list]
    args = list(a_list) + list(b_list)
    if has_add:
        in_specs.append(pl.BlockSpec((tm, tn), lambda j, i: (i, j)))
        args.append(addend)
    out = pl.pallas_call(
        body, name=name, grid=grid,
        in_specs=in_specs + _any_specs(k), out_specs=[pl.BlockSpec((tm, tn), lambda j, i: (i, j))] + _any_specs(k),
        out_shape=[jax.ShapeDtypeStruct((m, n), out_dtype)] + _chip_exchange_shapes(scatter, True),
        scratch_shapes=_chip_exchange_sems(k) if k else [],
        compiler_params=_cparams("arbitrary", "arbitrary") if k else _cparams("parallel", "parallel"),
    )(*args, *scatter)
    return out if k else out[0]


def _mm_kloop(a, b, *, tm, tn, tk, name):
    k, m = a.shape
    n = b.shape[1]
    nk = k // tk

    def body(a_ref, b_ref, o_ref):
        @pl.when(pl.program_id(2) == 0)
        def _():
            o_ref[...] = jnp.zeros_like(o_ref)
        o_ref[...] += lax.dot_general(a_ref[...].astype(bf16), b_ref[...].astype(bf16), (_TN, ((), ())),
                                      preferred_element_type=f32)

    return pl.pallas_call(
        body, name=name, grid=(n // tn, m // tm, nk),
        in_specs=[pl.BlockSpec((tk, tm), lambda j, i, kk: (kk, i)), pl.BlockSpec((tk, tn), lambda j, i, kk: (kk, j))],
        out_specs=pl.BlockSpec((tm, tn), lambda j, i, kk: (i, j)),
        out_shape=jax.ShapeDtypeStruct((m, n), f32),
        compiler_params=_cparams("parallel", "parallel", "arbitrary"),
    )(a, b)


def _shift_copies(src_ref, sh_ref, n, shifts=tuple(range(1, 8))):
    for i, b in enumerate(shifts):
        sh_ref[i, 0:n, :] = src_ref[pl.ds(b, n), :]


def _by_residue(offs):
    groups = {}
    for k, off in enumerate(offs):
        groups.setdefault(off % 8, []).append((k, off // 8))
    return groups


def _slab(src_ref, sh_ref, shifts, b, r0, n, lanes):
    ref = src_ref if b == 0 else sh_ref.at[shifts.index(b)]
    return ref[r0:r0 + n, lanes]


def _tap_conv(out_ref, n_rows, src_ref, sh_ref, w_ref, offs, bias_ref=None, shifts=tuple(range(1, 8))):
    groups = _by_residue(offs)
    for j in range(D // LANE):
        lanes = slice(j * LANE, (j + 1) * LANE)
        wv = [w_ref[k:k + 1, lanes] for k in range(len(offs))]
        for r0 in range(0, n_rows, STRIP):
            n = min(STRIP, n_rows - r0)
            accs = [jnp.zeros((n, LANE), f32) if bias_ref is None else jnp.broadcast_to(bias_ref[0:1, lanes], (n, LANE)),
                    jnp.zeros((n, LANE), f32)]
            m = 0
            for b, taps in groups.items():
                a_lo = min(a for _, a in taps)
                a_hi = max(a for _, a in taps)
                wide = _slab(src_ref, sh_ref, shifts, b, r0 + 8 * a_lo, 8 * (a_hi - a_lo) + n, lanes)
                for k, a in taps:
                    accs[m % 2] = accs[m % 2] + wv[k] * wide[8 * (a - a_lo):8 * (a - a_lo) + n]
                    m += 1
            out_ref[r0:r0 + n, lanes] = accs[0] + accs[1]


def _tap_corr(dw_ref, n_rows, lhs_ref, src_ref, sh_ref, offs, shifts=tuple(range(1, 8))):
    groups = _by_residue(offs)
    for j in range(D // LANE):
        lanes = slice(j * LANE, (j + 1) * LANE)
        accs = [jnp.zeros((8, LANE), f32) for _ in offs]
        for r0 in range(0, n_rows, STRIP):
            n = min(STRIP, n_rows - r0)
            d = lhs_ref[r0:r0 + n, lanes]
            for b, taps in groups.items():
                a_lo = min(a for _, a in taps)
                a_hi = max(a for _, a in taps)
                wide = _slab(src_ref, sh_ref, shifts, b, r0 + 8 * a_lo, 8 * (a_hi - a_lo) + n, lanes)
                for k, a in taps:
                    prod = d * wide[8 * (a - a_lo):8 * (a - a_lo) + n]
                    part = prod[0:8]
                    for q in range(1, n // 8):
                        part = part + prod[8 * q:8 * q + 8]
                    accs[k] = accs[k] + part
        for k in range(len(offs)):
            dw_ref[k:k + 1, lanes] += jnp.sum(accs[k], axis=0, keepdims=True)


_FWD_OFFS = [HALO_C - (KC - 1) + k for k in range(KC)]
_BWD_OFFS = [KC - 1 - k for k in range(KC)]


def _norm_act(a1, cz, g_ref, bb_ref):
    mu = jnp.mean(a1, axis=-1, keepdims=True)
    cen = a1 - mu
    var = jnp.mean(cen * cen, axis=-1, keepdims=True)
    rstd = lax.rsqrt(var + LN_EPS)
    xhat = cen * rstd
    ln = xhat * g_ref[...] + bb_ref[...]
    s, ds = _silu_and_grad(ln)
    zc, dzc = _silu_and_grad(cz)
    return xhat, rstd, s, ds, zc, dzc


def _conv_fwd(p_conv, dw_w, dw_b, ln_g, ln_b, *, tt):
    t = p_conv.shape[0]
    hb = tt // HALO_C

    def body(cv_ref, cg_ref, cz_ref, cvh_ref, cgh_ref, w_ref, b_ref, g_ref, bb_ref, u_ref, a1_ref, ext_ref, sh_ref):
        first = pl.program_id(0) == 0
        halo = cvh_ref[...] * _sigmoid(cgh_ref[...])
        ext_ref[0:HALO_C, :] = jnp.where(first, 0.0, halo)
        ext_ref[HALO_C:, :] = cv_ref[...] * _sigmoid(cg_ref[...])
        _shift_copies(ext_ref, sh_ref, tt + HALO_C - 8)
        _tap_conv(a1_ref, tt, ext_ref, sh_ref, w_ref, _FWD_OFFS, b_ref)
        _, _, s, _, zc, _ = _norm_act(a1_ref[...], cz_ref[...], g_ref, bb_ref)
        u_ref[...] = (s * zc).astype(bf16)

    def main(col):
        return pl.BlockSpec((tt, D), lambda i: (i, col))

    def prev(col):
        return pl.BlockSpec((HALO_C, D), lambda i: (jnp.maximum(i * hb - 1, 0), col))

    vec = pl.BlockSpec((1, D), lambda i: (0, 0))
    return pl.pallas_call(
        body, name="conv_fwd", grid=(t // tt,),
        in_specs=[main(0), main(1), main(2), prev(0), prev(1),
                  pl.BlockSpec((HALO_C, D), lambda i: (0, 0)), vec, vec, vec],
        out_specs=[pl.BlockSpec((tt, D), lambda i: (i, 0))] * 2,
        out_shape=[jax.ShapeDtypeStruct((t, D), bf16), jax.ShapeDtypeStruct((t, D), f32)],
        scratch_shapes=[pltpu.VMEM((tt + HALO_C, D), f32), pltpu.VMEM((N_SHIFT, tt + HALO_C - 8, D), f32)],
        compiler_params=_cparams("parallel"),
    )(p_conv, p_conv, p_conv, p_conv, p_conv, dw_w, dw_b, ln_g, ln_b)


def _conv_bwd(p_conv, a1, du, dw_w, ln_g, ln_b, *, tt):
    t = p_conv.shape[0]
    hb = tt // HALO_C
    n_tiles = t // tt
    last_hb = t // HALO_C - 1
    ne = tt + HALO_C

    def body(cv_ref, cg_ref, cz_ref, a1_ref, du_ref, cvp_ref, cgp_ref, czn_ref, a1n_ref, dun_ref,
             w_ref, g_ref, bb_ref, dp_ref, dww_ref, dvec_ref, ext_ref, sh_ref, da1_ref, da0_ref):
        i = pl.program_id(0)
        first = i == 0
        last = i == n_tiles - 1

        @pl.when(first)
        def _():
            dww_ref[...] = jnp.zeros_like(dww_ref)
            dvec_ref[...] = jnp.zeros_like(dvec_ref)

        sig = _sigmoid(cg_ref[...])
        ext_ref[0:HALO_C, :] = jnp.where(first, 0.0, cvp_ref[...] * _sigmoid(cgp_ref[...]))
        ext_ref[HALO_C:, :] = cv_ref[...] * sig
        a1_all = jnp.concatenate([a1_ref[...], a1n_ref[...]], axis=0)
        cz = jnp.concatenate([cz_ref[...], czn_ref[...]], axis=0)
        du_all = jnp.concatenate([du_ref[...], jnp.where(last, 0.0, dun_ref[...])], axis=0)
        xhat, rstd, s, ds, zc, dzc = _norm_act(a1_all, cz, g_ref, bb_ref)
        dln = du_all * zc * ds
        dxhat = dln * g_ref[...]
        da1 = rstd * (dxhat - jnp.mean(dxhat, axis=-1, keepdims=True)
                      - xhat * jnp.mean(dxhat * xhat, axis=-1, keepdims=True))
        da1_ref[...] = da1
        dcz = (du_all * s * dzc)[:tt]
        dvec_ref[0:1, :] += jnp.sum(da1[:tt], axis=0, keepdims=True)
        dvec_ref[1:2, :] += jnp.sum((dln * xhat)[:tt], axis=0, keepdims=True)
        dvec_ref[2:3, :] += jnp.sum(dln[:tt], axis=0, keepdims=True)
        _shift_copies(ext_ref, sh_ref, ne - 8)
        _tap_corr(dww_ref, tt, da1_ref, ext_ref, sh_ref, _FWD_OFFS)
        _shift_copies(da1_ref, sh_ref, ne - 8)
        _tap_conv(da0_ref, tt, da1_ref, sh_ref, w_ref, _BWD_OFFS)
        da0 = da0_ref[...]
        cv = cv_ref[...]
        dp_ref[:, 0:D] = (da0 * sig).astype(bf16)
        dp_ref[:, D:2 * D] = (da0 * cv * sig * (1.0 - sig)).astype(bf16)
        dp_ref[:, 2 * D:] = dcz.astype(bf16)

    def main(col):
        return pl.BlockSpec((tt, D), lambda i: (i, col))

    def prev(col):
        return pl.BlockSpec((HALO_C, D), lambda i: (jnp.maximum(i * hb - 1, 0), col))

    def nxt(col):
        return pl.BlockSpec((HALO_C, D), lambda i: (jnp.minimum((i + 1) * hb, last_hb), col))

    vec = pl.BlockSpec((1, D), lambda i: (0, 0))
    return pl.pallas_call(
        body, name="conv_bwd", grid=(n_tiles,),
        in_specs=[main(0), main(1), main(2), main(0), main(0), prev(0), prev(1), nxt(2), nxt(0), nxt(0),
                  pl.BlockSpec((HALO_C, D), lambda i: (0, 0)), vec, vec],
        out_specs=[pl.BlockSpec((tt, 3 * D), lambda i: (i, 0)),
                   pl.BlockSpec((HALO_C, D), lambda i: (0, 0)),
                   pl.BlockSpec((8, D), lambda i: (0, 0))],
        out_shape=[jax.ShapeDtypeStruct((t, 3 * D), bf16), jax.ShapeDtypeStruct((HALO_C, D), f32),
                   jax.ShapeDtypeStruct((8, D), f32)],
        scratch_shapes=[pltpu.VMEM((ne, D), f32), pltpu.VMEM((N_SHIFT, ne - 8, D), f32),
                        pltpu.VMEM((ne, D), f32), pltpu.VMEM((tt, D), f32)],
        compiler_params=_cparams("arbitrary"),
    )(p_conv, p_conv, p_conv, a1, du, p_conv, p_conv, p_conv, a1, du, dw_w, ln_g, ln_b)


def _dot_hi(a, b):
    return lax.dot_general(a, b, (((1,), (0,)), ((), ())), precision=HI, preferred_element_type=f32)


def _chunk_tri(n, lower):
    r = lax.broadcasted_iota(jnp.int32, (n, n), 0)
    c = lax.broadcasted_iota(jnp.int32, (n, n), 1)
    tri = (r >= c) if lower else (r <= c)
    return jnp.where(tri & (r // CH == c // CH), 1.0, 0.0).astype(f32)


def _softplus_and_sigmoid(x):
    e = jnp.exp(-jnp.abs(x))
    log1p = jnp.where(e < 1e-2, e * (1.0 - e * (0.5 - e * (1.0 / 3.0 - 0.25 * e))), jnp.log(1.0 + e))
    return jnp.maximum(x, 0.0) + log1p, _sigmoid(x)


_G_FWD_OFFS = [HALO_G - (KG - 1) + k for k in range(KG)]
_G_FWD_SHIFTS = (5, 6, 7)
_G_BWD_OFFS = [KG - 1 - k for k in range(KG)]
_G_BWD_SHIFTS = (1, 2, 3)


def _gdn_short_conv(pre_ref, ext_ref, sh_ref, n_rows, w_ref):
    _shift_copies(ext_ref, sh_ref, n_rows, _G_FWD_SHIFTS)
    _tap_conv(pre_ref, n_rows, ext_ref, sh_ref, w_ref, _G_FWD_OFFS, shifts=_G_FWD_SHIFTS)
    return pre_ref[...]


def _l2norm_heads(act, scale):
    outs, rs = [], []
    for h in range(NH):
        a = act[:, h * HD:(h + 1) * HD]
        r = lax.rsqrt(jnp.sum(a * a, axis=-1, keepdims=True) + L2_EPS)
        outs.append(a * (r * scale))
        rs.append(jnp.broadcast_to(r, a.shape))
    return jnp.concatenate(outs, axis=-1), jnp.concatenate(rs, axis=-1)


def _gate_math(ba, al_ref, dt_ref):
    lane = lax.broadcasted_iota(jnp.int32, ba.shape, 1)
    is_b = lane < NH
    is_a = (lane >= NH) & (lane < 2 * NH)
    sp, sg = _softplus_and_sigmoid(ba + dt_ref[...])
    neg_a = -jnp.exp(al_ref[...])
    return is_b, is_a, _sigmoid(ba), neg_a * sp, sg, neg_a


def _gdn_pre_fwd(p_qkv, p_ba, cw, alog_v, dt_v, *, tt):
    t = p_qkv.shape[0]
    hb = tt // HALO_G

    def body(q_ref, k_ref, v_ref, qh_ref, kh_ref, vh_ref, ba_ref, wq_ref, wk_ref, wv_ref, al_ref, dt_ref,
             qn_ref, kn_ref, va_ref, gt_ref, ext_ref, sh_ref, pre_ref):
        first = pl.program_id(0) == 0

        def conv_act(x_ref, xh_ref, w_ref):
            ext_ref[0:HALO_G, :] = jnp.where(first, 0.0, xh_ref[...])
            ext_ref[HALO_G:, :] = x_ref[...]
            pre = _gdn_short_conv(pre_ref, ext_ref, sh_ref, tt, w_ref)
            return pre * _sigmoid(pre)

        qn_ref[...] = _l2norm_heads(conv_act(q_ref, qh_ref, wq_ref), HD ** -0.5)[0]
        kn_ref[...] = _l2norm_heads(conv_act(k_ref, kh_ref, wk_ref), 1.0)[0]
        va_ref[...] = conv_act(v_ref, vh_ref, wv_ref)
        is_b, is_a, beta, g, _, _ = _gate_math(ba_ref[...], al_ref, dt_ref)
        gc = _dot_hi(_chunk_tri(tt, lower=True), jnp.where(is_a, g, 0.0))
        gt_ref[...] = jnp.where(is_b, beta, gc)

    def main(col):
        return pl.BlockSpec((tt, D), lambda i: (i, col))

    def prev(col):
        return pl.BlockSpec((HALO_G, D), lambda i: (jnp.maximum(i * hb - 1, 0), col))

    def wspec(col):
        return pl.BlockSpec((8, D), lambda i: (0, col))

    vec = pl.BlockSpec((1, HD), lambda i: (0, 0))
    gblk = pl.BlockSpec((tt, HD), lambda i: (i, 0))
    sds = jax.ShapeDtypeStruct((t, D), f32)
    return pl.pallas_call(
        body, name="gdn_pre_fwd", grid=(t // tt,),
        in_specs=[main(0), main(1), main(2), prev(0), prev(1), prev(2), gblk, wspec(0), wspec(1), wspec(2), vec, vec],
        out_specs=[pl.BlockSpec((tt, D), lambda i: (i, 0))] * 3 + [gblk],
        out_shape=[sds] * 3 + [jax.ShapeDtypeStruct((t, HD), f32)],
        scratch_shapes=[pltpu.VMEM((tt + HALO_G, D), f32), pltpu.VMEM((KG - 1, tt, D), f32), pltpu.VMEM((tt, D), f32)],
        compiler_params=_cparams("parallel"),
    )(p_qkv, p_qkv, p_qkv, p_qkv, p_qkv, p_qkv, p_ba, cw, cw, cw, alog_v, dt_v)


def _gdn_pre_bwd(p_qkv, p_ba, cw, alog_v, dt_v, dqn, dkn, dva, dgt, *, tt):
    t = p_qkv.shape[0]
    hb = tt // HALO_G
    n_tiles = t // tt
    last_hb = t // HALO_G - 1
    ne = tt + HALO_G

    def body(q_ref, k_ref, v_ref, qp_ref, kp_ref, vp_ref, qx_ref, kx_ref, vx_ref,
             dq_ref, dk_ref, dv_ref, dqx_ref, dkx_ref, dvx_ref, ba_ref, dgt_ref,
             wq_ref, wk_ref, wv_ref, al_ref, dt_ref,
             dp_ref, dba_ref, dcw_ref, dad_ref, ext_ref, sh_ref, pre_ref, dpre_ref, draw_ref):
        i = pl.program_id(0)
        first = i == 0
        last = i == n_tiles - 1

        @pl.when(first)
        def _():
            dcw_ref[...] = jnp.zeros_like(dcw_ref)
            dad_ref[...] = jnp.zeros_like(dad_ref)

        def one(x_ref, xp_ref, xx_ref, d_ref, dx_ref, w_ref, col, scale):
            ext_ref[0:HALO_G, :] = jnp.where(first, 0.0, xp_ref[...])
            ext_ref[HALO_G:HALO_G + tt, :] = x_ref[...]
            ext_ref[HALO_G + tt:, :] = xx_ref[...]
            pre = _gdn_short_conv(pre_ref, ext_ref, sh_ref, ne, w_ref)
            act, dact = _silu_and_grad(pre)
            d_out = jnp.concatenate([d_ref[...], jnp.where(last, 0.0, dx_ref[...])], axis=0)
            if scale is None:
                d_act = d_out
            else:
                parts = []
                for h in range(NH):
                    a = act[:, h * HD:(h + 1) * HD]
                    dn = d_out[:, h * HD:(h + 1) * HD]
                    r = lax.rsqrt(jnp.sum(a * a, axis=-1, keepdims=True) + L2_EPS)
                    parts.append(scale * r * (dn - a * (r * r) * jnp.sum(dn * a, axis=-1, keepdims=True)))
                d_act = jnp.concatenate(parts, axis=-1)
            dpre_ref[...] = d_act * dact
            _tap_corr(dcw_ref.at[:, col * D:(col + 1) * D], tt, dpre_ref, ext_ref, sh_ref, _G_FWD_OFFS, shifts=_G_FWD_SHIFTS)
            _shift_copies(dpre_ref, sh_ref, tt, _G_BWD_SHIFTS)
            _tap_conv(draw_ref, tt, dpre_ref, sh_ref, w_ref, _G_BWD_OFFS, shifts=_G_BWD_SHIFTS)
            dp_ref[:, col * D:(col + 1) * D] = draw_ref[...].astype(bf16)

        one(q_ref, qp_ref, qx_ref, dq_ref, dqx_ref, wq_ref, 0, HD ** -0.5)
        one(k_ref, kp_ref, kx_ref, dk_ref, dkx_ref, wk_ref, 1, 1.0)
        one(v_ref, vp_ref, vx_ref, dv_ref, dvx_ref, wv_ref, 2, None)

        is_b, is_a, beta, g, sg, neg_a = _gate_math(ba_ref[...], al_ref, dt_ref)
        dgt_v = dgt_ref[...]
        dg = _dot_hi(_chunk_tri(tt, lower=False), jnp.where(is_a, dgt_v, 0.0))
        d_al = jnp.where(is_a, dg * neg_a * sg, 0.0)
        dba_ref[...] = jnp.where(is_b, dgt_v * beta * (1.0 - beta), d_al)
        dad_ref[0:1, :] += jnp.sum(jnp.where(is_a, dg * g, 0.0), axis=0, keepdims=True)
        dad_ref[1:2, :] += jnp.sum(d_al, axis=0, keepdims=True)

    def main(col):
        return pl.BlockSpec((tt, D), lambda i: (i, col))

    def prev(col):
        return pl.BlockSpec((HALO_G, D), lambda i: (jnp.maximum(i * hb - 1, 0), col))

    def nxt(col):
        return pl.BlockSpec((HALO_G, D), lambda i: (jnp.minimum((i + 1) * hb, last_hb), col))

    def wspec(col):
        return pl.BlockSpec((8, D), lambda i: (0, col))

    vec = pl.BlockSpec((1, HD), lambda i: (0, 0))
    gblk = pl.BlockSpec((tt, HD), lambda i: (i, 0))
    return pl.pallas_call(
        body, name="gdn_pre_bwd", grid=(n_tiles,),
        in_specs=[main(0), main(1), main(2), prev(0), prev(1), prev(2), nxt(0), nxt(1), nxt(2),
                  main(0), main(0), main(0), nxt(0), nxt(0), nxt(0), gblk, gblk,
                  wspec(0), wspec(1), wspec(2), vec, vec],
        out_specs=[pl.BlockSpec((tt, 3 * D), lambda i: (i, 0)), pl.BlockSpec((tt, HD), lambda i: (i, 0)),
                   pl.BlockSpec((8, 3 * D), lambda i: (0, 0)), pl.BlockSpec((8, HD), lambda i: (0, 0))],
        out_shape=[jax.ShapeDtypeStruct((t, 3 * D), bf16), jax.ShapeDtypeStruct((t, HD), f32),
                   jax.ShapeDtypeStruct((8, 3 * D), f32), jax.ShapeDtypeStruct((8, HD), f32)],
        scratch_shapes=[pltpu.VMEM((HALO_G + tt + HALO_G, D), f32), pltpu.VMEM((KG - 1, ne, D), f32),
                        pltpu.VMEM((ne, D), f32), pltpu.VMEM((ne, D), f32), pltpu.VMEM((tt, D), f32)],
        compiler_params=_cparams("arbitrary"),
    )(p_qkv, p_qkv, p_qkv, p_qkv, p_qkv, p_qkv, p_qkv, p_qkv, p_qkv,
      dqn, dkn, dva, dqn, dkn, dva, p_ba, dgt, cw, cw, cw, alog_v, dt_v)


def _dot_b(a, b, dims):
    return lax.dot_general(a.astype(bf16), b.astype(bf16), (dims, ((), ())), preferred_element_type=f32)


def _inverse_by_doubling(ms):
    heads = range(len(ms))
    r = lax.broadcasted_iota(jnp.int32, (CH, CH), 0)
    c = lax.broadcasted_iota(jnp.int32, (CH, CH), 1)
    eye = jnp.where(r == c, 1.0, 0.0).astype(f32)
    p = [eye + ms[h] for h in heads]
    mp = ms
    for _ in range(5):
        mp = [_dot_b(mp[h], mp[h], _NN) for h in heads]
        pm = [_dot_b(p[h], mp[h], _NN) for h in heads]
        p = [p[h] + pm[h] for h in heads]
    return tuple(p)


@jax.custom_vjp
def _known_inverse(ms, ps):
    return ps


def _known_inverse_fwd(ms, ps):
    return ps, ps


def _known_inverse_bwd(ps, cts):
    heads = range(len(ps))
    left = [_dot_b(ps[h], cts[h], _TN) for h in heads]
    return tuple(_dot_b(left[h], ps[h], _NT) for h in heads), tuple(jnp.zeros_like(p) for p in ps)


_known_inverse.defvjp(_known_inverse_fwd, _known_inverse_bwd)


def _chunk_prepare(qs, ks, vs, gcs, bbs, ps=None):
    heads = range(len(qs))
    r = lax.broadcasted_iota(jnp.int32, (CH, CH), 0)
    c = lax.broadcasted_iota(jnp.int32, (CH, CH), 1)
    causal = r >= c
    strict = r > c
    gc_row = [gcs[h].T[:CH, :] for h in heads]
    decay = [jnp.where(causal, jnp.exp(jnp.where(causal, gcs[h][:, :CH] - gc_row[h], 0.0)), 0.0) for h in heads]
    kb = [ks[h] * bbs[h] for h in heads]
    egc = [jnp.exp(gcs[h]) for h in heads]
    kk = [_dot_b(kb[h], ks[h], _NT) for h in heads]
    qk = [_dot_b(qs[h], ks[h], _NT) for h in heads]
    m = tuple(-jnp.where(strict, kk[h] * decay[h], 0.0) for h in heads)
    p = _inverse_by_doubling(m) if ps is None else _known_inverse(m, ps)
    u = [_dot_b(p[h], vs[h] * bbs[h], _NN) for h in heads]
    w = [_dot_b(p[h], kb[h] * egc[h], _NN) for h in heads]
    intra = [jnp.where(causal, qk[h] * decay[h], 0.0) for h in heads]
    g_last = [gcs[h][CH - 1:CH, :] for h in heads]
    k_dec = [ks[h] * jnp.exp(g_last[h] - gcs[h]) for h in heads]
    q_dec = [qs[h] * egc[h] for h in heads]
    e_last = [jnp.exp(g_last[h]) for h in heads]
    return u, w, intra, q_dec, k_dec, e_last, p


def _chunk_apply(u, w, intra, q_dec, k_dec, e_last, ss):
    heads = range(len(ss))
    ws = [_dot_b(w[h], ss[h], _NN) for h in heads]
    qs_s = [_dot_b(q_dec[h], ss[h], _NN) for h in heads]
    v_new = [u[h] - ws[h] for h in heads]
    iv = [_dot_b(intra[h], v_new[h], _NN) for h in heads]
    kv = [_dot_b(k_dec[h], v_new[h], _TN) for h in heads]
    o = tuple(qs_s[h] + iv[h] for h in heads)
    s_new = tuple(ss[h] * e_last[h] + kv[h] for h in heads)
    return o, s_new


def _chunk_group_fn(ins, ss, ps=None):
    n = len(ss)
    prep = _chunk_prepare(*(sum((tuple(c[i]) for c in ins), ()) for i in range(5)), ps=ps)
    outs, befores = [], []
    for g in range(len(ins)):
        befores.append(ss)
        o, ss = _chunk_apply(*(x[g * n:(g + 1) * n] for x in prep[:6]), ss)
        outs.append(o)
    return tuple(outs), tuple(befores), ss, prep[6]


def _head_cols():
    return [slice(h * HD, (h + 1) * HD) for h in range(NH)]


def _head_gates(gt):
    gcs = tuple(jnp.broadcast_to(gt[:, NH + h:NH + h + 1], (CH, HD)) for h in range(NH))
    bbs = tuple(jnp.broadcast_to(gt[:, h:h + 1], (CH, HD)) for h in range(NH))
    return gcs, bbs


def _gdn_scan_fwd(qn, kn, va, gates, *, tt):
    t = qn.shape[0]
    cpb = tt // CH
    group = min(SCAN_GROUP, cpb)

    def body(q_ref, k_ref, v_ref, gt_ref, o_ref, st_ref, p_ref, s_scr):
        @pl.when(pl.program_id(0) == 0)
        def _():
            s_scr[...] = jnp.zeros_like(s_scr)

        cols = _head_cols()

        def inputs(ci):
            rows = pl.ds(pl.multiple_of(ci * CH, CH), CH)
            gcs, bbs = _head_gates(gt_ref[rows, :])
            return tuple(tuple(ref[rows, cl] for cl in cols) for ref in (q_ref, k_ref, v_ref)) + (gcs, bbs)

        def step(gi, carry):
            chunks = [group * gi + g for g in range(group)]
            outs, befores, s_end, p = _chunk_group_fn([inputs(ci) for ci in chunks], tuple(s_scr[h] for h in range(NH)))
            for g, ci in enumerate(chunks):
                rows = pl.ds(pl.multiple_of(ci * CH, CH), CH)
                for h in range(NH):
                    st_ref[ci, h] = befores[g][h]
                    o_ref[rows, cols[h]] = outs[g][h]
                    p_ref[ci, h] = p[g * NH + h].astype(bf16)
            for h in range(NH):
                s_scr[h] = s_end[h]
            return carry

        lax.fori_loop(0, cpb // group, step, 0)

    blk = pl.BlockSpec((tt, D), lambda i: (i, 0))
    return pl.pallas_call(
        body, name="gdn_scan_fwd", grid=(t // tt,),
        in_specs=[blk] * 3 + [pl.BlockSpec((tt, HD), lambda i: (i, 0))],
        out_specs=[blk, pl.BlockSpec((cpb, NH, HD, HD), lambda i: (i, 0, 0, 0)),
                   pl.BlockSpec((cpb, NH, CH, CH), lambda i: (i, 0, 0, 0))],
        out_shape=[jax.ShapeDtypeStruct((t, D), f32), jax.ShapeDtypeStruct((t // CH, NH, HD, HD), f32),
                   jax.ShapeDtypeStruct((t // CH, NH, CH, CH), bf16)],
        scratch_shapes=[pltpu.VMEM((NH, HD, HD), f32)],
        compiler_params=_cparams("arbitrary"),
    )(qn, kn, va, gates)


def _gdn_scan_bwd(qn, kn, va, gates, states, inverses, do, *, tt):
    t = qn.shape[0]
    nblk = t // tt
    cpb = tt // CH

    def body(q_ref, k_ref, v_ref, gt_ref, st_ref, p_ref, do_ref, dq_ref, dk_ref, dv_ref, dgt_ref, ds_scr):
        @pl.when(pl.program_id(0) == 0)
        def _():
            ds_scr[...] = jnp.zeros_like(ds_scr)

        cols = _head_cols()

        def rows_of(ci):
            return pl.ds(pl.multiple_of(ci * CH, CH), CH)

        def inputs(ci):
            gcs, bbs = _head_gates(gt_ref[rows_of(ci), :])
            return tuple(tuple(ref[rows_of(ci), cl] for cl in cols) for ref in (q_ref, k_ref, v_ref)) + (gcs, bbs)

        def step(j, carry):
            ci = cpb - 1 - j
            ps = tuple(p_ref[ci, h].astype(f32) for h in range(NH))

            def one(ins, ss):
                outs, _, s_end, _ = _chunk_group_fn([ins], ss, ps=ps)
                return outs[0], s_end

            _, vjp = jax.vjp(one, inputs(ci), tuple(st_ref[ci, h] for h in range(NH)))
            grads, ds = vjp((tuple(do_ref[rows_of(ci), cl] for cl in cols), tuple(ds_scr[h] for h in range(NH))))
            lane = lax.broadcasted_iota(jnp.int32, (CH, HD), 1)
            dgt = jnp.zeros((CH, HD), f32)
            for h in range(NH):
                for ref, g in zip((dq_ref, dk_ref, dv_ref), grads[:3]):
                    ref[rows_of(ci), cols[h]] = g[h]
                dgt = dgt + jnp.where(lane == NH + h, jnp.sum(grads[3][h], axis=-1, keepdims=True), 0.0)
                dgt = dgt + jnp.where(lane == h, jnp.sum(grads[4][h], axis=-1, keepdims=True), 0.0)
                ds_scr[h] = ds[h]
            dgt_ref[rows_of(ci), :] = dgt
            return carry

        lax.fori_loop(0, cpb, step, 0)

    blk = pl.BlockSpec((tt, D), lambda i: (nblk - 1 - i, 0))
    sblk = pl.BlockSpec((cpb, NH, HD, HD), lambda i: (nblk - 1 - i, 0, 0, 0))
    sds = jax.ShapeDtypeStruct((t, D), f32)
    gblk = pl.BlockSpec((tt, HD), lambda i: (nblk - 1 - i, 0))
    pblk = pl.BlockSpec((cpb, NH, CH, CH), lambda i: (nblk - 1 - i, 0, 0, 0))
    return pl.pallas_call(
        body, name="gdn_scan_bwd", grid=(nblk,),
        in_specs=[blk] * 3 + [gblk, sblk, pblk, blk],
        out_specs=[blk] * 3 + [gblk], out_shape=[sds] * 3 + [jax.ShapeDtypeStruct((t, HD), f32)],
        scratch_shapes=[pltpu.VMEM((NH, HD, HD), f32)],
        compiler_params=_cparams("arbitrary"),
    )(qn, kn, va, gates, states, inverses, do)


def _rms_heads(o):
    ons, rs = [], []
    for h in range(NH):
        a = o[:, h * HD:(h + 1) * HD]
        r = lax.rsqrt(jnp.mean(a * a, axis=-1, keepdims=True) + RMS_EPS)
        ons.append(a * r)
        rs.append(jnp.broadcast_to(r, a.shape))
    return jnp.concatenate(ons, axis=-1), jnp.concatenate(rs, axis=-1)


def _gdn_post_fwd(o, p_gz, ng_b, *, tt):
    t = o.shape[0]

    def body(o_ref, gz_ref, ng_ref, og_ref):
        on, _ = _rms_heads(o_ref[...])
        z, _ = _silu_and_grad(gz_ref[...])
        og_ref[...] = (on * ng_ref[...] * z).astype(bf16)

    blk = pl.BlockSpec((tt, D), lambda i: (i, 0))
    return pl.pallas_call(
        body, name="gdn_post_fwd", grid=(t // tt,),
        in_specs=[blk, blk, pl.BlockSpec((1, D), lambda i: (0, 0))],
        out_specs=blk, out_shape=jax.ShapeDtypeStruct((t, D), bf16),
        compiler_params=_cparams("parallel"),
    )(o, p_gz, ng_b)


def _gdn_post_bwd(o, p_gz, ng_b, dog, *, tt):
    t = o.shape[0]

    def body(o_ref, gz_ref, ng_ref, dog_ref, do_ref, dgz_ref, dng_ref):
        @pl.when(pl.program_id(0) == 0)
        def _():
            dng_ref[...] = jnp.zeros_like(dng_ref)

        on, r = _rms_heads(o_ref[...])
        z, dz = _silu_and_grad(gz_ref[...])
        dog_v = dog_ref[...]
        ng = ng_ref[...]
        dgz_ref[...] = (dog_v * on * ng * dz).astype(bf16)
        dy = dog_v * z
        dng_all = jnp.sum(dy * on, axis=0, keepdims=True)
        dng = dng_all[:, 0:HD]
        for h in range(1, NH):
            dng = dng + dng_all[:, h * HD:(h + 1) * HD]
        dng_ref[0:1, :] += dng
        don = dy * ng
        prod = don * on
        parts = []
        for h in range(NH):
            sl = slice(h * HD, (h + 1) * HD)
            parts.append(don[:, sl] - on[:, sl] * jnp.mean(prod[:, sl], axis=-1, keepdims=True))
        do_ref[...] = r * jnp.concatenate(parts, axis=-1)

    blk = pl.BlockSpec((tt, D), lambda i: (i, 0))
    return pl.pallas_call(
        body, name="gdn_post_bwd", grid=(t // tt,),
        in_specs=[blk, blk, pl.BlockSpec((1, D), lambda i: (0, 0)), blk],
        out_specs=[blk, blk, pl.BlockSpec((8, HD), lambda i: (0, 0))],
        out_shape=[jax.ShapeDtypeStruct((t, D), f32), jax.ShapeDtypeStruct((t, D), bf16),
                   jax.ShapeDtypeStruct((8, HD), f32)],
        compiler_params=_cparams("arbitrary"),
    )(o, p_gz, ng_b, dog)


def _merge(x, y_conf, y_gdn, p_gates, target, w_o, ln_g, ln_b, *, tt):
    t = x.shape[0]

    def body(x_ref, yc_ref, yg_ref, gc_ref, gg_ref, tg_ref, w_ref, g_ref, b_ref,
             loss_ref, dxd_ref, dyc_ref, dyg_ref, dpg_ref, h_ref, dz_ref, dvec_ref):
        @pl.when(pl.program_id(0) == 0)
        def _():
            loss_ref[...] = jnp.zeros_like(loss_ref)
            dvec_ref[...] = jnp.zeros_like(dvec_ref)

        sc = _sigmoid(gc_ref[...])
        sg = _sigmoid(gg_ref[...])
        yc = yc_ref[...]
        yg = yg_ref[...]
        h = (sc * yc + sg * yg).astype(bf16)
        h_ref[...] = h
        z = DN_ALPHA * x_ref[...] + jnp.dot(h, w_ref[...], preferred_element_type=f32)
        mu = jnp.mean(z, axis=-1, keepdims=True)
        cen = z - mu
        rstd = lax.rsqrt(jnp.mean(cen * cen, axis=-1, keepdims=True) + LN_EPS)
        xhat = cen * rstd
        err = xhat * g_ref[...] + b_ref[...] - tg_ref[...]
        loss_ref[...] += 0.5 / D * jnp.sum(err * err)
        dy = err * (1.0 / D)
        dvec_ref[0:1, :] += jnp.sum(dy * xhat, axis=0, keepdims=True)
        dvec_ref[1:2, :] += jnp.sum(dy, axis=0, keepdims=True)
        dxhat = dy * g_ref[...]
        dz = rstd * (dxhat - jnp.mean(dxhat, axis=-1, keepdims=True)
                     - xhat * jnp.mean(dxhat * xhat, axis=-1, keepdims=True))
        dxd_ref[...] = DN_ALPHA * dz
        dz_b = dz.astype(bf16)
        dz_ref[...] = dz_b
        dh = lax.dot_general(dz_b, w_ref[...], (_NT, ((), ())), preferred_element_type=f32)
        dyc_ref[...] = (dh * sc).astype(bf16)
        dyg_ref[...] = (dh * sg).astype(bf16)
        dpg_ref[:, 0:D] = (dh * yc * sc * (1.0 - sc)).astype(bf16)
        dpg_ref[:, D:] = (dh * yg * sg * (1.0 - sg)).astype(bf16)

    blk = pl.BlockSpec((tt, D), lambda i: (i, 0))
    wblk = pl.BlockSpec((D, D), lambda i: (0, 0))
    vec = pl.BlockSpec((1, D), lambda i: (0, 0))
    return pl.pallas_call(
        body, name="merge_norm_loss", grid=(t // tt,),
        in_specs=[blk, blk, blk, pl.BlockSpec((tt, D), lambda i: (i, 0)), pl.BlockSpec((tt, D), lambda i: (i, 1)),
                  blk, wblk, vec, vec],
        out_specs=[pl.BlockSpec((8, HD), lambda i: (0, 0)), blk, blk, blk,
                   pl.BlockSpec((tt, 2 * D), lambda i: (i, 0)), blk, blk, pl.BlockSpec((8, D), lambda i: (0, 0))],
        out_shape=[jax.ShapeDtypeStruct((8, HD), f32), jax.ShapeDtypeStruct((t, D), f32),
                   jax.ShapeDtypeStruct((t, D), bf16), jax.ShapeDtypeStruct((t, D), bf16),
                   jax.ShapeDtypeStruct((t, 2 * D), bf16), jax.ShapeDtypeStruct((t, D), bf16),
                   jax.ShapeDtypeStruct((t, D), bf16), jax.ShapeDtypeStruct((8, D), f32)],
        compiler_params=_cparams("arbitrary"),
    )(x, y_conf, y_gdn, p_gates, p_gates, target, w_o, ln_g, ln_b)


def _place():
    return lax.axis_index("x"), lax.axis_index("y"), lax.axis_index("c")


def _any_specs(n):
    return [pl.BlockSpec(memory_space=pl.ANY)] * n


def _sibling_merge(arrs, name, take_other_half=False):
    k = len(arrs)

    def body(*refs):
        a_refs, o_refs = refs[:k], refs[k:2 * k]
        send_sems, recv_sems = refs[2 * k:]
        x, y, c = _place()
        sends = []
        for i in range(k):
            src = a_refs[i]
            if take_other_half:
                n = a_refs[i].shape[-2] // 2
                lead = (slice(None),) * (len(a_refs[i].shape) - 2)
                src = a_refs[i].at[lead + (pl.ds((1 - c) * n, n), slice(None))]
            cp = pltpu.make_async_remote_copy(src_ref=src, dst_ref=o_refs[i], send_sem=send_sems.at[i],
                                              recv_sem=recv_sems.at[i], device_id=(x, y, 1 - c), device_id_type=MESH)
            cp.start()
            sends.append(cp)
        for cp in sends:
            cp.wait()

    def out_sds(a):
        rows = a.shape[-2] // 2 if take_other_half else a.shape[-2]
        return jax.ShapeDtypeStruct(a.shape[:-2] + (rows, a.shape[-1]), a.dtype)

    return pl.pallas_call(
        body, name=name, in_specs=_any_specs(k), out_specs=_any_specs(k),
        out_shape=[out_sds(a) for a in arrs],
        scratch_shapes=[pltpu.SemaphoreType.DMA((k,)), pltpu.SemaphoreType.DMA((k,))],
    )(*arrs)


def _join_halves(mine, other):
    c = lax.axis_index("c")
    return jnp.concatenate([jnp.where(c == 0, mine, other), jnp.where(c == 0, other, mine)], axis=-2)


def _chip_exchange_ops(a_refs, o_refs, send_sems, recv_sems, local_sems, scatter):
    k = len(a_refs)
    x, y, c = _place()
    me = 2 * x + y
    peers = [(1 - x, y), (x, 1 - y), (1 - x, 1 - y)]

    def src(i, j):
        return a_refs[i].at[j] if scatter else a_refs[i]

    def copy(i, n, send_j, slot):
        px, py = peers[n]
        return pltpu.make_async_remote_copy(
            src_ref=src(i, send_j), dst_ref=o_refs[i].at[slot], send_sem=send_sems.at[3 * i + n],
            recv_sem=recv_sems.at[3 * i + n], device_id=(px, py, c), device_id_type=MESH)

    def owns():
        return [pltpu.make_async_copy(src(i, me), o_refs[i].at[me], local_sems.at[i]) for i in range(k)]

    def sends():
        return [copy(i, n, 2 * peers[n][0] + peers[n][1], me) for n in range(3) for i in range(k)]

    def start():
        for cp in owns() + sends():
            cp.start()

    def finish():
        for n in range(3):
            for i in range(k):
                copy(i, n, me, 2 * peers[n][0] + peers[n][1]).wait_recv()
        for cp in sends():
            cp.wait_send()
        for cp in owns():
            cp.wait()

    return start, finish


def _chip_exchange_shapes(arrs, scatter):
    return [jax.ShapeDtypeStruct((N_CHIPS,) + tuple(a.shape[1:] if scatter else a.shape), a.dtype) for a in arrs]


def _chip_exchange_sems(k):
    return [pltpu.SemaphoreType.DMA((3 * k,)), pltpu.SemaphoreType.DMA((3 * k,)), pltpu.SemaphoreType.DMA((k,))]


def _chip_exchange(arrs, name, scatter):
    k = len(arrs)

    def body(*refs):
        start, finish = _chip_exchange_ops(refs[:k], refs[k:2 * k], *refs[2 * k:], scatter)
        start()
        finish()

    return pl.pallas_call(
        body, name=name, in_specs=_any_specs(k), out_specs=_any_specs(k),
        out_shape=_chip_exchange_shapes(arrs, scatter), scratch_shapes=_chip_exchange_sems(k),
    )(*arrs)


def _pair_sum(g_all, got, c_arr, name, out_dtype):
    n, w = got.shape[1:]
    tile = n // 4
    n_tiles = n // tile

    def body(c_ref, a_ref, b_ref, o_ref):
        o_ref[...] = (a_ref[...] + b_ref[...]).astype(out_dtype)

    return pl.pallas_call(
        body, name=name,
        grid_spec=pltpu.PrefetchScalarGridSpec(
            num_scalar_prefetch=1, grid=(N_CHIPS, n_tiles),
            in_specs=[pl.BlockSpec((1, tile, w), lambda j, i, c_ref: (j, c_ref[0] * n_tiles + i, 0)),
                      pl.BlockSpec((1, tile, w), lambda j, i, c_ref: (j, i, 0))],
            out_specs=pl.BlockSpec((1, tile, w), lambda j, i, c_ref: (j, i, 0))),
        out_shape=jax.ShapeDtypeStruct(got.shape, out_dtype),
        compiler_params=_cparams("parallel", "parallel"),
    )(c_arr, g_all, got)


def _sum_slots(a, name):
    n, w = a.shape[1:]
    tile = n // 4

    def body(a_ref, o_ref):
        o_ref[...] = ((a_ref[0].astype(f32) + a_ref[1].astype(f32)) + a_ref[2].astype(f32)) + a_ref[3].astype(f32)

    return pl.pallas_call(
        body, name=name, grid=(n // tile,),
        in_specs=[pl.BlockSpec((N_CHIPS, tile, w), lambda i: (0, i, 0))],
        out_specs=pl.BlockSpec((tile, w), lambda i: (i, 0)),
        out_shape=jax.ShapeDtypeStruct((n, w), f32),
        compiler_params=_cparams("parallel"),
    )(a)


def _adamw(w, g, m, v, name):
    rows, width = w.shape
    by_rows = rows % 64 == 0
    c1 = 1.0 / (1.0 - ADAM_B1 ** ADAM_STEP)
    c2 = 1.0 / (1.0 - ADAM_B2 ** ADAM_STEP)

    def body(w_ref, g_ref, m_ref, v_ref, d_ref, mo_ref, vo_ref):
        g_v = g_ref[...]
        m_new = ADAM_B1 * m_ref[...] + (1.0 - ADAM_B1) * g_v
        v_new = ADAM_B2 * v_ref[...] + (1.0 - ADAM_B2) * (g_v * g_v)
        mo_ref[...] = m_new
        vo_ref[...] = v_new
        d_ref[...] = -ADAM_LR * ((m_new * c1) / (jnp.sqrt(v_new * c2) + ADAM_EPS) + ADAM_WD * w_ref[...])

    blk = pl.BlockSpec((rows // 8, width), lambda i: (i, 0)) if by_rows else pl.BlockSpec((rows, LANE), lambda i: (0, i))
    sds = jax.ShapeDtypeStruct((rows, width), f32)
    return pl.pallas_call(
        body, name=name, grid=(8 if by_rows else width // LANE,),
        in_specs=[blk] * 4, out_specs=[blk] * 3, out_shape=[sds] * 3,
        compiler_params=_cparams("parallel"),
    )(w, g, m, v)


R_DW = 3 * SQ_BLK
R_CW = R_DW + 8
R_VEC = R_CW + 8
R_SMALL = R_VEC + 8
REST_ROWS = 896


def _pack_small(conf_dw_w, gdn_conv_w, vecs, a_log, dt_bias, norm_g):
    dw = jnp.pad(conf_dw_w.reshape(-1), (0, 8 * D - KC * SQ_BLK)).reshape(8, D)
    cw = jnp.pad(gdn_conv_w.reshape(-1), (0, 5 * D)).reshape(8, D)
    vec = jnp.pad(jnp.stack(vecs), ((0, 3), (0, 0)))
    small = jnp.pad(jnp.concatenate([a_log, dt_bias, norm_g]), (0, D - 2 * NH - HD)).reshape(1, D)
    return jnp.pad(jnp.concatenate([dw, cw, vec, small], axis=0), ((0, REST_ROWS - R_SMALL - 1), (0, 0)))


def _pack_rest(conf_w_out, gdn_w_out, w_o, small):
    return jnp.concatenate([conf_w_out, gdn_w_out, w_o, small], axis=0)


def _unpack_rest(p):
    conf_dw_w = p[R_DW:R_DW + 8].reshape(-1)[:KC * SQ_BLK].reshape(KC, SQ_BLK)
    gdn_conv_w = p[R_CW:R_CW + 3].reshape(KG, 3 * SQ_BLK)
    small = p[R_SMALL]
    return dict(conf_w_out=p[0:SQ_BLK], gdn_w_out=p[SQ_BLK:2 * SQ_BLK], w_o=p[2 * SQ_BLK:R_DW],
                conf_dw_w=conf_dw_w, gdn_conv_w=gdn_conv_w, conf_dw_b=p[R_VEC], conf_ln_g=p[R_VEC + 1],
                conf_ln_b=p[R_VEC + 2], post_ln_g=p[R_VEC + 3], post_ln_b=p[R_VEC + 4],
                gdn_A_log=small[0:NH], gdn_dt_bias=small[NH:2 * NH], gdn_norm_g=small[2 * NH:2 * NH + HD])


_WEIGHT_ORDER = ("w_in", "conf_dw_w", "conf_dw_b", "conf_ln_g", "conf_ln_b", "conf_w_out", "gdn_conv_w",
                 "gdn_A_log", "gdn_dt_bias", "gdn_norm_g", "gdn_w_out", "w_o", "post_ln_g", "post_ln_b")


def _gather_weights(w_in, conf_w_out, gdn_w_out, w_o, conf_dw_w, gdn_conv_w):
    c = lax.axis_index("c")
    sq = jnp.concatenate([conf_w_out, gdn_w_out, w_o], axis=0).astype(bf16)
    w_half = lax.dynamic_slice_in_dim(w_in.astype(bf16), c * (D // 2), D // 2, axis=0)
    sq_half = lax.dynamic_slice_in_dim(sq, c * (sq.shape[0] // 2), sq.shape[0] // 2, axis=0)
    small = jnp.concatenate([jnp.pad(conf_dw_w.reshape(-1), (0, 8 * D - KC * SQ_BLK)).reshape(8, D),
                             jnp.pad(gdn_conv_w.reshape(-1), (0, 5 * D)).reshape(8, D)], axis=0)
    got_w, got_sq, small_all = _chip_exchange([w_half, sq_half, small], "weight_gather_chips", scatter=False)
    oth_w, oth_sq = _sibling_merge([got_w, got_sq], "weight_gather_sibling")
    w4 = _join_halves(got_w, oth_w)
    sq4 = _join_halves(got_sq, oth_sq)
    sq_full = [sq4[:, n * SQ_BLK:(n + 1) * SQ_BLK].reshape(D, D) for n in range(3)]
    dw_full = small_all[:, 0:8].reshape(N_CHIPS, 8 * D)[:, :KC * SQ_BLK].reshape(N_CHIPS, KC, SQ_BLK)
    dw_full = dw_full.transpose(1, 0, 2).reshape(KC, D)
    cw_full = small_all[:, 8:11].reshape(N_CHIPS, KG, 3 * SQ_BLK).transpose(1, 0, 2).reshape(KG, 3 * D)
    return w4, sq_full[0], sq_full[1], sq_full[2], dw_full, cw_full


def _w_in_cols(w4, lo, hi):
    parts = []
    for j in range(N_CHIPS):
        a, b = max(lo, j * W_IN_BLK), min(hi, (j + 1) * W_IN_BLK)
        if a < b:
            parts.append(w4[j, :, a - j * W_IN_BLK:b - j * W_IN_BLK])
    return parts[0] if len(parts) == 1 else jnp.concatenate(parts, axis=1)


def _w_in_by_chip(pieces):
    chips = []
    for j in range(N_CHIPS):
        lo, hi = j * W_IN_BLK, (j + 1) * W_IN_BLK
        parts = []
        for start, arr in pieces:
            a, b = max(lo, start), min(hi, start + arr.shape[1])
            if a < b:
                parts.append(arr[:, a - start:b - start])
        chips.append(jnp.concatenate(parts, axis=1))
    return jnp.stack(chips)


def _pair_sums(g_w, g_rest):
    c_arr = lax.axis_index("c").astype(jnp.int32).reshape(1)
    got_w, got_r = _sibling_merge([g_w, g_rest], "grad_sibling_halves", take_other_half=True)
    pair_w = _pair_sum(g_w, got_w, c_arr, "grad_pair_sum_w_in", bf16)
    pair_r = _pair_sum(g_rest, got_r, c_arr, "grad_pair_sum_rest", f32)
    return pair_w, pair_r


def _chip_sums(all_w, all_r):
    tot_w, tot_r = _sum_slots(all_w, "grad_chip_sum_w_in"), _sum_slots(all_r, "grad_chip_sum_rest")
    oth_w, oth_r = _sibling_merge([tot_w, tot_r], "grad_sibling_result")
    return _join_halves(tot_w, oth_w), _join_halves(tot_r, oth_r)


def _local_step(x2, tgt, w4, wc_out, wg_out, wo_full, dw_full, cw_full, conf_dw_b, conf_ln_g, conf_ln_b,
                gdn_A_log, gdn_dt_bias, gdn_norm_g, post_ln_g, post_ln_b):
    t = x2.shape[0]
    tt = min(TOKEN_TILE, t)
    tm = min(1024, t)

    w_conv, w_qkv, w_gz = _w_in_cols(w4, 0, 3 * D), _w_in_cols(w4, 3 * D, 6 * D), _w_in_cols(w4, 6 * D, 7 * D)
    w_gates = _w_in_cols(w4, 7 * D + 2 * NH, W_IN_COLS)
    dw_pad = jnp.pad(dw_full, ((0, HALO_C - KC), (0, 0)))
    cw_pad = jnp.pad(cw_full, ((0, 8 - KG), (0, 0)))
    row = lambda v: v.reshape(1, D)
    alog_v = jnp.pad(gdn_A_log, (NH, HD - 2 * NH)).reshape(1, HD)
    dt_v = jnp.pad(gdn_dt_bias, (NH, HD - 2 * NH)).reshape(1, HD)
    ng_b = row(jnp.tile(gdn_norm_g, NH))
    w_ba = jnp.pad(_w_in_cols(w4, 7 * D, 7 * D + 2 * NH), ((0, 0), (0, HD - 2 * NH)))

    x_b = x2.astype(bf16)

    p_conv = _mm_multi([x_b], [w_conv], out_dtype=f32, tm=tm, tn=1024, name="proj_conv")
    p_qkv = _mm_multi([x_b], [w_qkv], out_dtype=f32, tm=tm, tn=1024, name="proj_qkv")
    p_gz = _mm_multi([x_b], [w_gz], out_dtype=f32, tm=tm, tn=1024, name="proj_gz")
    p_gates = _mm_multi([x_b], [w_gates], out_dtype=f32, tm=tm, tn=1024, name="proj_gates")
    p_ba = _mm_multi([x_b], [w_ba], out_dtype=f32, tm=tm, tn=HD, name="proj_ba")

    u, a1 = _conv_fwd(p_conv, dw_pad, row(conf_dw_b), row(conf_ln_g), row(conf_ln_b), tt=tt)
    y_conf = _mm_multi([u], [wc_out], out_dtype=f32, tm=tm, tn=1024, name="conf_out")

    qn, kn, va, gates = _gdn_pre_fwd(p_qkv, p_ba, cw_pad, alog_v, dt_v, tt=tt)
    o, states, inverses = _gdn_scan_fwd(qn, kn, va, gates, tt=tt)
    og = _gdn_post_fwd(o, p_gz, ng_b, tt=tt)
    y_gdn = _mm_multi([og], [wg_out], out_dtype=f32, tm=tm, tn=1024, name="gdn_out")

    loss_blk, dxd, dyc, dyg, dp_gates, h, dz, dpost = _merge(
        x2, y_conf, y_gdn, p_gates, tgt, wo_full, row(post_ln_g), row(post_ln_b), tt=tt)

    d_wo = _mm_kloop(h, dz, tm=D, tn=1024, tk=min(512, t), name="grad_w_o")
    du = _mm_multi([dyc], [wc_out], out_dtype=f32, tm=tm, tn=1024, name="conf_out_bwd", rhs_t=True)
    d_wc = _mm_kloop(u, dyc, tm=D, tn=1024, tk=min(512, t), name="grad_conf_w_out")
    dog = _mm_multi([dyg], [wg_out], out_dtype=f32, tm=tm, tn=1024, name="gdn_out_bwd", rhs_t=True)
    d_wg = _mm_kloop(og, dyg, tm=D, tn=1024, tk=min(512, t), name="grad_gdn_w_out")

    dp_conv, d_dww, dconv_vec = _conv_bwd(p_conv, a1, du, dw_pad, row(conf_ln_g), row(conf_ln_b), tt=tt)

    do, dp_gz, dng = _gdn_post_bwd(o, p_gz, ng_b, dog, tt=tt)
    dqn, dkn, dva, dgates = _gdn_scan_bwd(qn, kn, va, gates, states, inverses, do, tt=tt)
    dp_qkv, dp_ba, d_cw, d_ad = _gdn_pre_bwd(p_qkv, p_ba, cw_pad, alog_v, dt_v, dqn, dkn, dva, dgates, tt=tt)
    dp_ba_b = dp_ba.astype(bf16)

    grad_x_factors = ([dp_conv, dp_qkv, dp_gz, dp_gates, dp_ba_b], [w_conv, w_qkv, w_gz, w_gates, w_ba], dxd)

    tk = min(512, t)
    d_w_conv = _mm_kloop(x_b, dp_conv, tm=D, tn=1024, tk=tk, name="grad_w_in_conv")
    d_w_qkv = _mm_kloop(x_b, dp_qkv, tm=D, tn=1024, tk=tk, name="grad_w_in_qkv")
    d_w_gz = _mm_kloop(x_b, dp_gz, tm=D, tn=1024, tk=tk, name="grad_w_in_gz")
    d_w_gates = _mm_kloop(x_b, dp_gates, tm=D, tn=1024, tk=tk, name="grad_w_in_gates")
    d_w_ba = _mm_kloop(x_b, dp_ba_b, tm=D, tn=HD, tk=tk, name="grad_w_in_ba")
    d_w_in = _w_in_by_chip([(0, d_w_conv), (3 * D, d_w_qkv), (6 * D, d_w_gz), (7 * D, d_w_ba[:, :2 * NH]),
                            (7 * D + 2 * NH, d_w_gates)])

    return (loss_blk[0, 0], grad_x_factors, d_w_in, d_wc, d_wg, d_wo, d_dww, d_cw, dconv_vec, dpost, d_ad, dng)


def kernel(x, w_in, conf_dw_w, conf_dw_b, conf_ln_g, conf_ln_b, conf_w_out, gdn_conv_w, gdn_A_log, gdn_dt_bias, gdn_norm_g, gdn_w_out, w_o, post_ln_g, post_ln_b, loss_target, m_w_in, m_conf_dw_w, m_conf_dw_b, m_conf_ln_g, m_conf_ln_b, m_conf_w_out, m_gdn_conv_w, m_gdn_A_log, m_gdn_dt_bias, m_gdn_norm_g, m_gdn_w_out, m_w_o, m_post_ln_g, m_post_ln_b, v_w_in, v_conf_dw_w, v_conf_dw_b, v_conf_ln_g, v_conf_ln_b, v_conf_w_out, v_gdn_conv_w, v_gdn_A_log, v_gdn_dt_bias, v_gdn_norm_g, v_gdn_w_out, v_w_o, v_post_ln_g, v_post_ln_b):
    x2 = x.reshape(x.shape[-2], D)
    tgt = loss_target.reshape(x2.shape)
    w4, wc_out, wg_out, wo_full, dw_full, cw_full = _gather_weights(
        w_in, conf_w_out, gdn_w_out, w_o, conf_dw_w, gdn_conv_w)
    (loss_part, grad_x_factors, d_w_in, d_wc, d_wg, d_wo, d_dww, d_cw, dconv_vec, dpost, d_ad, dng) = _local_step(
        x2, tgt, w4, wc_out, wg_out, wo_full, dw_full, cw_full, conf_dw_b, conf_ln_g, conf_ln_b,
        gdn_A_log, gdn_dt_bias, gdn_norm_g, post_ln_g, post_ln_b)
    loss = lax.psum(loss_part, ("x", "y", "c"))

    dww_c = d_dww[:KC].reshape(KC, N_CHIPS, SQ_BLK)
    dcw_c = d_cw[:KG].reshape(KG, N_CHIPS, 3 * SQ_BLK)
    vecs = [dconv_vec[0], dconv_vec[1], dconv_vec[2], dpost[0], dpost[1]]
    g_rest = jnp.stack([
        _pack_rest(d_wc[j * SQ_BLK:(j + 1) * SQ_BLK], d_wg[j * SQ_BLK:(j + 1) * SQ_BLK], d_wo[j * SQ_BLK:(j + 1) * SQ_BLK],
                   _pack_small(dww_c[:, j], dcw_c[:, j], vecs, d_ad[0, NH:2 * NH], d_ad[1, NH:2 * NH], dng[0]))
        for j in range(N_CHIPS)])
    pair_w, pair_r = _pair_sums(d_w_in, g_rest)
    grad_x, all_w, all_r = _mm_multi(*grad_x_factors, out_dtype=f32, tm=min(256, x2.shape[0]), tn=512,
                                     name="grad_x_and_chip_scatter", rhs_t=True, scatter=[pair_w, pair_r])
    g_w_in, g_rest = _chip_sums(all_w, all_r)

    def rest_of(w_c, w_g, w_oo, dw, cw, b1, g1, b2, g2, b3, a_log, dt_bias, norm_g):
        return _pack_rest(w_c, w_g, w_oo, _pack_small(dw, cw, [b1, g1, b2, g2, b3], a_log, dt_bias, norm_g))

    w_r = rest_of(conf_w_out, gdn_w_out, w_o, conf_dw_w, gdn_conv_w, conf_dw_b, conf_ln_g, conf_ln_b,
                  post_ln_g, post_ln_b, gdn_A_log, gdn_dt_bias, gdn_norm_g)
    m_r = rest_of(m_conf_w_out, m_gdn_w_out, m_w_o, m_conf_dw_w, m_gdn_conv_w, m_conf_dw_b, m_conf_ln_g, m_conf_ln_b,
                  m_post_ln_g, m_post_ln_b, m_gdn_A_log, m_gdn_dt_bias, m_gdn_norm_g)
    v_r = rest_of(v_conf_w_out, v_gdn_w_out, v_w_o, v_conf_dw_w, v_gdn_conv_w, v_conf_dw_b, v_conf_ln_g, v_conf_ln_b,
                  v_post_ln_g, v_post_ln_b, v_gdn_A_log, v_gdn_dt_bias, v_gdn_norm_g)
    upd_w_in = [u.T for u in _adamw(w_in.T, g_w_in.T, m_w_in.T, v_w_in.T, "adamw_w_in")]
    upd_rest = _adamw(w_r, g_rest, m_r, v_r, "adamw_rest")

    out = [loss, grad_x.reshape(x.shape)]
    for big, rest in zip((g_w_in,) + tuple(upd_w_in), (g_rest,) + tuple(upd_rest)):
        d = dict(_unpack_rest(rest), w_in=big)
        out += [d[n] for n in _WEIGHT_ORDER]
    return tuple(out)
```

```python
import jax
import jax.numpy as jnp
from jax import lax
from jax.experimental import pallas as pl
from jax.experimental.pallas import tpu as pltpu

f32 = jnp.float32
bf16 = jnp.bfloat16
HI = lax.Precision.HIGHEST
MESH = pl.DeviceIdType.MESH

D = 1024
NH = 8
HD = 128
CH = 64
KC = 31
KG = 4
HALO_C = 32
HALO_G = 8
LANE = 128
STRIP = 32
N_SHIFT = 7
LN_EPS = 1e-5
RMS_EPS = 1e-6
L2_EPS = 1e-6
DN_ALPHA = 2.0 ** 0.25
N_CHIPS = 4
W_IN_COLS = 9232
W_IN_BLK = W_IN_COLS // N_CHIPS
SQ_BLK = D // N_CHIPS
VMEM_LIMIT = 52 * 1024 * 1024
TOKEN_TILE = 256
SCAN_GROUP = 4

ADAM_LR = 0.001
ADAM_B1 = 0.9
ADAM_B2 = 0.999
ADAM_EPS = 1e-08
ADAM_WD = 0.01
ADAM_STEP = 10


def _sigmoid(x):
    return 1.0 / (1.0 + jnp.exp(-x))


def _silu_and_grad(x):
    s = _sigmoid(x)
    return x * s, s * (1.0 + x * (1.0 - s))


_NN = ((1,), (0,))
_NT = ((1,), (1,))
_TN = ((0,), (0,))


def _cparams(*sem):
    return pltpu.CompilerParams(dimension_semantics=sem, vmem_limit_bytes=VMEM_LIMIT)


def _mm_multi(a_list, b_list, addend=None, *, out_dtype, tm, tn, name, rhs_t=False, scatter=()):
    n_pairs = len(a_list)
    m = a_list[0].shape[0]
    n = b_list[0].shape[0 if rhs_t else 1]
    has_add = addend is not None
    dims = (_NT if rhs_t else _NN, ((), ()))
    n_in = 2 * n_pairs + has_add
    k = len(scatter)
    grid = (n // tn, m // tm)

    def body(*refs):
        a_refs = refs[:n_pairs]
        b_refs = refs[n_pairs:2 * n_pairs]
        o_ref = refs[n_in + k]
        if k:
            start, finish = _chip_exchange_ops(refs[n_in:n_in + k], refs[n_in + k + 1:n_in + 2 * k + 1],
                                               *refs[n_in + 2 * k + 1:], True)
            step = pl.program_id(0) * grid[1] + pl.program_id(1)
            pl.when(step == 0)(start)
        acc = None
        for a_ref, b_ref in zip(a_refs, b_refs):
            p = lax.dot_general(a_ref[...].astype(bf16), b_ref[...].astype(bf16), dims, preferred_element_type=f32)
            acc = p if acc is None else acc + p
        if has_add:
            acc = acc + refs[2 * n_pairs][...]
        o_ref[...] = acc.astype(out_dtype)
        if k:
            pl.when(step == grid[0] * grid[1] - 1)(finish)

    in_specs = [pl.BlockSpec((tm, a.shape[1]), lambda j, i: (i, 0)) for a in a_list]
    if rhs_t:
        in_specs += [pl.BlockSpec((tn, b.shape[1]), lambda j, i: (j, 0)) for b in b_list]
    else:
        in_specs += [pl.BlockSpec((b.shape[0], tn), lambda j, i: (0, j)) for b in b_list]
    args = list(a_list) + list(b_list)
    if has_add:
        in_specs.append(pl.BlockSpec((tm, tn), lambda j, i: (i, j)))
        args.append(addend)
    out = pl.pallas_call(
        body, name=name, grid=grid,
        in_specs=in_specs + _any_specs(k), out_specs=[pl.BlockSpec((tm, tn), lambda j, i: (i, j))] + _any_specs(k),
        out_shape=[jax.ShapeDtypeStruct((m, n), out_dtype)] + _chip_exchange_shapes(scatter, True),
        scratch_shapes=_chip_exchange_sems(k) if k else [],
        compiler_params=_cparams("arbitrary", "arbitrary") if k else _cparams("parallel", "parallel"),
    )(*args, *scatter)
    return out if k else out[0]


def _mm_kloop(a, b, *, tm, tn, tk, name):
    k, m = a.shape
    n = b.shape[1]
    nk = k // tk

    def body(a_ref, b_ref, o_ref):
        @pl.when(pl.program_id(2) == 0)
        def _():
            o_ref[...] = jnp.zeros_like(o_ref)
        o_ref[...] += lax.dot_general(a_ref[...].astype(bf16), b_ref[...].astype(bf16), (_TN, ((), ())),
                                      preferred_element_type=f32)

    return pl.pallas_call(
        body, name=name, grid=(n // tn, m // tm, nk),
        in_specs=[pl.BlockSpec((tk, tm), lambda j, i, kk: (kk, i)), pl.BlockSpec((tk, tn), lambda j, i, kk: (kk, j))],
        out_specs=pl.BlockSpec((tm, tn), lambda j, i, kk: (i, j)),
        out_shape=jax.ShapeDtypeStruct((m, n), f32),
        compiler_params=_cparams("parallel", "parallel", "arbitrary"),
    )(a, b)


def _shift_copies(src_ref, sh_ref, n, shifts=tuple(range(1, 8))):
    for i, b in enumerate(shifts):
        sh_ref[i, 0:n, :] = src_ref[pl.ds(b, n), :]


def _by_residue(offs):
    groups = {}
    for k, off in enumerate(offs):
        groups.setdefault(off % 8, []).append((k, off // 8))
    return groups


def _slab(src_ref, sh_ref, shifts, b, r0, n, lanes):
    ref = src_ref if b == 0 else sh_ref.at[shifts.index(b)]
    return ref[r0:r0 + n, lanes]


def _tap_conv(out_ref, n_rows, src_ref, sh_ref, w_ref, offs, bias_ref=None, shifts=tuple(range(1, 8))):
    groups = _by_residue(offs)
    for j in range(D // LANE):
        lanes = slice(j * LANE, (j + 1) * LANE)
        wv = [w_ref[k:k + 1, lanes] for k in range(len(offs))]
        for r0 in range(0, n_rows, STRIP):
            n = min(STRIP, n_rows - r0)
            accs = [jnp.zeros((n, LANE), f32) if bias_ref is None else jnp.broadcast_to(bias_ref[0:1, lanes], (n, LANE)),
                    jnp.zeros((n, LANE), f32)]
            m = 0
            for b, taps in groups.items():
                a_lo = min(a for _, a in taps)
                a_hi = max(a for _, a in taps)
                wide = _slab(src_ref, sh_ref, shifts, b, r0 + 8 * a_lo, 8 * (a_hi - a_lo) + n, lanes)
                for k, a in taps:
                    accs[m % 2] = accs[m % 2] + wv[k] * wide[8 * (a - a_lo):8 * (a - a_lo) + n]
                    m += 1
            out_ref[r0:r0 + n, lanes] = accs[0] + accs[1]


def _tap_corr(dw_ref, n_rows, lhs_ref, src_ref, sh_ref, offs, shifts=tuple(range(1, 8))):
    groups = _by_residue(offs)
    for j in range(D // LANE):
        lanes = slice(j * LANE, (j + 1) * LANE)
        accs = [jnp.zeros((8, LANE), f32) for _ in offs]
        for r0 in range(0, n_rows, STRIP):
            n = min(STRIP, n_rows - r0)
            d = lhs_ref[r0:r0 + n, lanes]
            for b, taps in groups.items():
                a_lo = min(a for _, a in taps)
                a_hi = max(a for _, a in taps)
                wide = _slab(src_ref, sh_ref, shifts, b, r0 + 8 * a_lo, 8 * (a_hi - a_lo) + n, lanes)
                for k, a in taps:
                    prod = d * wide[8 * (a - a_lo):8 * (a - a_lo) + n]
                    part = prod[0:8]
                    for q in range(1, n // 8):
                        part = part + prod[8 * q:8 * q + 8]
                    accs[k] = accs[k] + part
        for k in range(len(offs)):
            dw_ref[k:k + 1, lanes] += jnp.sum(accs[k], axis=0, keepdims=True)


_FWD_OFFS = [HALO_C - (KC - 1) + k for k in range(KC)]
_BWD_OFFS = [KC - 1 - k for k in range(KC)]


def _norm_act(a1, cz, g_ref, bb_ref):
    mu = jnp.mean(a1, axis=-1, keepdims=True)
    cen = a1 - mu
    var = jnp.mean(cen * cen, axis=-1, keepdims=True)
    rstd = lax.rsqrt(var + LN_EPS)
    xhat = cen * rstd
    ln = xhat * g_ref[...] + bb_ref[...]
    s, ds = _silu_and_grad(ln)
    zc, dzc = _silu_and_grad(cz)
    return xhat, rstd, s, ds, zc, dzc


def _conv_fwd(p_conv, dw_w, dw_b, ln_g, ln_b, w_out, *, tt):
    t = p_conv.shape[0]
    hb = tt // HALO_C

    def body(cv_ref, cg_ref, cz_ref, cvh_ref, cgh_ref, w_ref, b_ref, g_ref, bb_ref, wo_ref,
             u_ref, a1_ref, y_ref, ext_ref, sh_ref):
        first = pl.program_id(0) == 0
        halo = cvh_ref[...] * _sigmoid(cgh_ref[...])
        ext_ref[0:HALO_C, :] = jnp.where(first, 0.0, halo)
        ext_ref[HALO_C:, :] = cv_ref[...] * _sigmoid(cg_ref[...])
        _shift_copies(ext_ref, sh_ref, tt + HALO_C - 8)
        _tap_conv(a1_ref, tt, ext_ref, sh_ref, w_ref, _FWD_OFFS, b_ref)
        _, _, s, _, zc, _ = _norm_act(a1_ref[...], cz_ref[...], g_ref, bb_ref)
        u = (s * zc).astype(bf16)
        u_ref[...] = u
        y_ref[...] = jnp.dot(u, wo_ref[...], preferred_element_type=f32)

    def main(col):
        return pl.BlockSpec((tt, D), lambda i: (i, col))

    def prev(col):
        return pl.BlockSpec((HALO_C, D), lambda i: (jnp.maximum(i * hb - 1, 0), col))

    vec = pl.BlockSpec((1, D), lambda i: (0, 0))
    return pl.pallas_call(
        body, name="conv_fwd", grid=(t // tt,),
        in_specs=[main(0), main(1), main(2), prev(0), prev(1),
                  pl.BlockSpec((HALO_C, D), lambda i: (0, 0)), vec, vec, vec, pl.BlockSpec((D, D), lambda i: (0, 0))],
        out_specs=[pl.BlockSpec((tt, D), lambda i: (i, 0))] * 3,
        out_shape=[jax.ShapeDtypeStruct((t, D), bf16), jax.ShapeDtypeStruct((t, D), f32),
                   jax.ShapeDtypeStruct((t, D), f32)],
        scratch_shapes=[pltpu.VMEM((tt + HALO_C, D), f32), pltpu.VMEM((N_SHIFT, tt + HALO_C - 8, D), f32)],
        compiler_params=_cparams("parallel"),
    )(p_conv, p_conv, p_conv, p_conv, p_conv, dw_w, dw_b, ln_g, ln_b, w_out)


def _conv_bwd(p_conv, a1, dyc, w_out, dw_w, ln_g, ln_b, *, tt):
    t = p_conv.shape[0]
    hb = tt // HALO_C
    n_tiles = t // tt
    last_hb = t // HALO_C - 1
    ne = tt + HALO_C

    def body(cv_ref, cg_ref, cz_ref, a1_ref, dy_ref, cvp_ref, cgp_ref, czn_ref, a1n_ref, dyn_ref,
             w_ref, g_ref, bb_ref, wo_ref, dp_ref, dww_ref, dvec_ref, ext_ref, sh_ref, da1_ref, da0_ref):
        i = pl.program_id(0)
        first = i == 0
        last = i == n_tiles - 1

        @pl.when(first)
        def _():
            dww_ref[...] = jnp.zeros_like(dww_ref)
            dvec_ref[...] = jnp.zeros_like(dvec_ref)

        sig = _sigmoid(cg_ref[...])
        ext_ref[0:HALO_C, :] = jnp.where(first, 0.0, cvp_ref[...] * _sigmoid(cgp_ref[...]))
        ext_ref[HALO_C:, :] = cv_ref[...] * sig
        a1_all = jnp.concatenate([a1_ref[...], a1n_ref[...]], axis=0)
        cz = jnp.concatenate([cz_ref[...], czn_ref[...]], axis=0)
        dy_all = jnp.concatenate([dy_ref[...], jnp.where(last, 0.0, dyn_ref[...]).astype(bf16)], axis=0)
        du_all = lax.dot_general(dy_all, wo_ref[...], (_NT, ((), ())), preferred_element_type=f32)
        xhat, rstd, s, ds, zc, dzc = _norm_act(a1_all, cz, g_ref, bb_ref)
        dln = du_all * zc * ds
        dxhat = dln * g_ref[...]
        da1 = rstd * (dxhat - jnp.mean(dxhat, axis=-1, keepdims=True)
                      - xhat * jnp.mean(dxhat * xhat, axis=-1, keepdims=True))
        da1_ref[...] = da1
        dcz = (du_all * s * dzc)[:tt]
        dvec_ref[0:1, :] += jnp.sum(da1[:tt], axis=0, keepdims=True)
        dvec_ref[1:2, :] += jnp.sum((dln * xhat)[:tt], axis=0, keepdims=True)
        dvec_ref[2:3, :] += jnp.sum(dln[:tt], axis=0, keepdims=True)
        _shift_copies(ext_ref, sh_ref, ne - 8)
        _tap_corr(dww_ref, tt, da1_ref, ext_ref, sh_ref, _FWD_OFFS)
        _shift_copies(da1_ref, sh_ref, ne - 8)
        _tap_conv(da0_ref, tt, da1_ref, sh_ref, w_ref, _BWD_OFFS)
        da0 = da0_ref[...]
        cv = cv_ref[...]
        dp_ref[:, 0:D] = (da0 * sig).astype(bf16)
        dp_ref[:, D:2 * D] = (da0 * cv * sig * (1.0 - sig)).astype(bf16)
        dp_ref[:, 2 * D:] = dcz.astype(bf16)

    def main(col):
        return pl.BlockSpec((tt, D), lambda i: (i, col))

    def prev(col):
        return pl.BlockSpec((HALO_C, D), lambda i: (jnp.maximum(i * hb - 1, 0), col))

    def nxt(col):
        return pl.BlockSpec((HALO_C, D), lambda i: (jnp.minimum((i + 1) * hb, last_hb), col))

    vec = pl.BlockSpec((1, D), lambda i: (0, 0))
    return pl.pallas_call(
        body, name="conv_bwd", grid=(n_tiles,),
        in_specs=[main(0), main(1), main(2), main(0), main(0), prev(0), prev(1), nxt(2), nxt(0), nxt(0),
                  pl.BlockSpec((HALO_C, D), lambda i: (0, 0)), vec, vec, pl.BlockSpec((D, D), lambda i: (0, 0))],
        out_specs=[pl.BlockSpec((tt, 3 * D), lambda i: (i, 0)),
                   pl.BlockSpec((HALO_C, D), lambda i: (0, 0)),
                   pl.BlockSpec((8, D), lambda i: (0, 0))],
        out_shape=[jax.ShapeDtypeStruct((t, 3 * D), bf16), jax.ShapeDtypeStruct((HALO_C, D), f32),
                   jax.ShapeDtypeStruct((8, D), f32)],
        scratch_shapes=[pltpu.VMEM((ne, D), f32), pltpu.VMEM((N_SHIFT, ne - 8, D), f32),
                        pltpu.VMEM((ne, D), f32), pltpu.VMEM((tt, D), f32)],
        compiler_params=_cparams("arbitrary"),
    )(p_conv, p_conv, p_conv, a1, dyc, p_conv, p_conv, p_conv, a1, dyc, dw_w, ln_g, ln_b, w_out)


def _dot_hi(a, b):
    return lax.dot_general(a, b, (((1,), (0,)), ((), ())), precision=HI, preferred_element_type=f32)


def _chunk_tri(n, lower):
    r = lax.broadcasted_iota(jnp.int32, (n, n), 0)
    c = lax.broadcasted_iota(jnp.int32, (n, n), 1)
    tri = (r >= c) if lower else (r <= c)
    return jnp.where(tri & (r // CH == c // CH), 1.0, 0.0).astype(f32)


def _softplus_and_sigmoid(x):
    e = jnp.exp(-jnp.abs(x))
    log1p = jnp.where(e < 1e-2, e * (1.0 - e * (0.5 - e * (1.0 / 3.0 - 0.25 * e))), jnp.log(1.0 + e))
    return jnp.maximum(x, 0.0) + log1p, _sigmoid(x)


_G_FWD_OFFS = [HALO_G - (KG - 1) + k for k in range(KG)]
_G_FWD_SHIFTS = (5, 6, 7)
_G_BWD_OFFS = [KG - 1 - k for k in range(KG)]
_G_BWD_SHIFTS = (1, 2, 3)


def _gdn_short_conv(pre_ref, ext_ref, sh_ref, n_rows, w_ref):
    _shift_copies(ext_ref, sh_ref, n_rows, _G_FWD_SHIFTS)
    _tap_conv(pre_ref, n_rows, ext_ref, sh_ref, w_ref, _G_FWD_OFFS, shifts=_G_FWD_SHIFTS)
    return pre_ref[...]


def _l2norm_heads(act, scale):
    outs, rs = [], []
    for h in range(NH):
        a = act[:, h * HD:(h + 1) * HD]
        r = lax.rsqrt(jnp.sum(a * a, axis=-1, keepdims=True) + L2_EPS)
        outs.append(a * (r * scale))
        rs.append(jnp.broadcast_to(r, a.shape))
    return jnp.concatenate(outs, axis=-1), jnp.concatenate(rs, axis=-1)


def _gate_math(ba, al_ref, dt_ref):
    lane = lax.broadcasted_iota(jnp.int32, ba.shape, 1)
    is_b = lane < NH
    is_a = (lane >= NH) & (lane < 2 * NH)
    sp, sg = _softplus_and_sigmoid(ba + dt_ref[...])
    neg_a = -jnp.exp(al_ref[...])
    return is_b, is_a, _sigmoid(ba), neg_a * sp, sg, neg_a


def _gdn_pre_fwd(p_qkv, p_ba, cw, alog_v, dt_v, *, tt):
    t = p_qkv.shape[0]
    hb = tt // HALO_G

    def body(q_ref, k_ref, v_ref, qh_ref, kh_ref, vh_ref, ba_ref, wq_ref, wk_ref, wv_ref, al_ref, dt_ref,
             qn_ref, kn_ref, va_ref, gt_ref, ext_ref, sh_ref, pre_ref):
        first = pl.program_id(0) == 0

        def conv_act(x_ref, xh_ref, w_ref):
            ext_ref[0:HALO_G, :] = jnp.where(first, 0.0, xh_ref[...])
            ext_ref[HALO_G:, :] = x_ref[...]
            pre = _gdn_short_conv(pre_ref, ext_ref, sh_ref, tt, w_ref)
            return pre * _sigmoid(pre)

        qn_ref[...] = _l2norm_heads(conv_act(q_ref, qh_ref, wq_ref), HD ** -0.5)[0]
        kn_ref[...] = _l2norm_heads(conv_act(k_ref, kh_ref, wk_ref), 1.0)[0]
        va_ref[...] = conv_act(v_ref, vh_ref, wv_ref)
        is_b, is_a, beta, g, _, _ = _gate_math(ba_ref[...], al_ref, dt_ref)
        gc = _dot_hi(_chunk_tri(tt, lower=True), jnp.where(is_a, g, 0.0))
        gt_ref[...] = jnp.where(is_b, beta, gc)

    def main(col):
        return pl.BlockSpec((tt, D), lambda i: (i, col))

    def prev(col):
        return pl.BlockSpec((HALO_G, D), lambda i: (jnp.maximum(i * hb - 1, 0), col))

    def wspec(col):
        return pl.BlockSpec((8, D), lambda i: (0, col))

    vec = pl.BlockSpec((1, HD), lambda i: (0, 0))
    gblk = pl.BlockSpec((tt, HD), lambda i: (i, 0))
    sds = jax.ShapeDtypeStruct((t, D), f32)
    return pl.pallas_call(
        body, name="gdn_pre_fwd", grid=(t // tt,),
        in_specs=[main(0), main(1), main(2), prev(0), prev(1), prev(2), gblk, wspec(0), wspec(1), wspec(2), vec, vec],
        out_specs=[pl.BlockSpec((tt, D), lambda i: (i, 0))] * 3 + [gblk],
        out_shape=[sds] * 3 + [jax.ShapeDtypeStruct((t, HD), f32)],
        scratch_shapes=[pltpu.VMEM((tt + HALO_G, D), f32), pltpu.VMEM((KG - 1, tt, D), f32), pltpu.VMEM((tt, D), f32)],
        compiler_params=_cparams("parallel"),
    )(p_qkv, p_qkv, p_qkv, p_qkv, p_qkv, p_qkv, p_ba, cw, cw, cw, alog_v, dt_v)


def _gdn_pre_bwd(p_qkv, p_ba, cw, alog_v, dt_v, dqn, dkn, dva, dgt, *, tt):
    t = p_qkv.shape[0]
    hb = tt // HALO_G
    n_tiles = t // tt
    last_hb = t // HALO_G - 1
    ne = tt + HALO_G

    def body(q_ref, k_ref, v_ref, qp_ref, kp_ref, vp_ref, qx_ref, kx_ref, vx_ref,
             dq_ref, dk_ref, dv_ref, dqx_ref, dkx_ref, dvx_ref, ba_ref, dgt_ref,
             wq_ref, wk_ref, wv_ref, al_ref, dt_ref,
             dp_ref, dba_ref, dcw_ref, dad_ref, ext_ref, sh_ref, pre_ref, dpre_ref, draw_ref):
        i = pl.program_id(0)
        first = i == 0
        last = i == n_tiles - 1

        @pl.when(first)
        def _():
            dcw_ref[...] = jnp.zeros_like(dcw_ref)
            dad_ref[...] = jnp.zeros_like(dad_ref)

        def one(x_ref, xp_ref, xx_ref, d_ref, dx_ref, w_ref, col, scale):
            ext_ref[0:HALO_G, :] = jnp.where(first, 0.0, xp_ref[...])
            ext_ref[HALO_G:HALO_G + tt, :] = x_ref[...]
            ext_ref[HALO_G + tt:, :] = xx_ref[...]
            pre = _gdn_short_conv(pre_ref, ext_ref, sh_ref, ne, w_ref)
            act, dact = _silu_and_grad(pre)
            d_out = jnp.concatenate([d_ref[...], jnp.where(last, 0.0, dx_ref[...])], axis=0)
            if scale is None:
                d_act = d_out
            else:
                parts = []
                for h in range(NH):
                    a = act[:, h * HD:(h + 1) * HD]
                    dn = d_out[:, h * HD:(h + 1) * HD]
                    r = lax.rsqrt(jnp.sum(a * a, axis=-1, keepdims=True) + L2_EPS)
                    parts.append(scale * r * (dn - a * (r * r) * jnp.sum(dn * a, axis=-1, keepdims=True)))
                d_act = jnp.concatenate(parts, axis=-1)
            dpre_ref[...] = d_act * dact
            _tap_corr(dcw_ref.at[:, col * D:(col + 1) * D], tt, dpre_ref, ext_ref, sh_ref, _G_FWD_OFFS, shifts=_G_FWD_SHIFTS)
            _shift_copies(dpre_ref, sh_ref, tt, _G_BWD_SHIFTS)
            _tap_conv(draw_ref, tt, dpre_ref, sh_ref, w_ref, _G_BWD_OFFS, shifts=_G_BWD_SHIFTS)
            dp_ref[:, col * D:(col + 1) * D] = draw_ref[...].astype(bf16)

        one(q_ref, qp_ref, qx_ref, dq_ref, dqx_ref, wq_ref, 0, HD ** -0.5)
        one(k_ref, kp_ref, kx_ref, dk_ref, dkx_ref, wk_ref, 1, 1.0)
        one(v_ref, vp_ref, vx_ref, dv_ref, dvx_ref, wv_ref, 2, None)

        is_b, is_a, beta, g, sg, neg_a = _gate_math(ba_ref[...], al_ref, dt_ref)
        dgt_v = dgt_ref[...]
        dg = _dot_hi(_chunk_tri(tt, lower=False), jnp.where(is_a, dgt_v, 0.0))
        d_al = jnp.where(is_a, dg * neg_a * sg, 0.0)
        dba_ref[...] = jnp.where(is_b, dgt_v * beta * (1.0 - beta), d_al)
        dad_ref[0:1, :] += jnp.sum(jnp.where(is_a, dg * g, 0.0), axis=0, keepdims=True)
        dad_ref[1:2, :] += jnp.sum(d_al, axis=0, keepdims=True)

    def main(col):
        return pl.BlockSpec((tt, D), lambda i: (i, col))

    def prev(col):
        return pl.BlockSpec((HALO_G, D), lambda i: (jnp.maximum(i * hb - 1, 0), col))

    def nxt(col):
        return pl.BlockSpec((HALO_G, D), lambda i: (jnp.minimum((i + 1) * hb, last_hb), col))

    def wspec(col):
        return pl.BlockSpec((8, D), lambda i: (0, col))

    vec = pl.BlockSpec((1, HD), lambda i: (0, 0))
    gblk = pl.BlockSpec((tt, HD), lambda i: (i, 0))
    return pl.pallas_call(
        body, name="gdn_pre_bwd", grid=(n_tiles,),
        in_specs=[main(0), main(1), main(2), prev(0), prev(1), prev(2), nxt(0), nxt(1), nxt(2),
                  main(0), main(0), main(0), nxt(0), nxt(0), nxt(0), gblk, gblk,
                  wspec(0), wspec(1), wspec(2), vec, vec],
        out_specs=[pl.BlockSpec((tt, 3 * D), lambda i: (i, 0)), pl.BlockSpec((tt, HD), lambda i: (i, 0)),
                   pl.BlockSpec((8, 3 * D), lambda i: (0, 0)), pl.BlockSpec((8, HD), lambda i: (0, 0))],
        out_shape=[jax.ShapeDtypeStruct((t, 3 * D), bf16), jax.ShapeDtypeStruct((t, HD), f32),
                   jax.ShapeDtypeStruct((8, 3 * D), f32), jax.ShapeDtypeStruct((8, HD), f32)],
        scratch_shapes=[pltpu.VMEM((HALO_G + tt + HALO_G, D), f32), pltpu.VMEM((KG - 1, ne, D), f32),
                        pltpu.VMEM((ne, D), f32), pltpu.VMEM((ne, D), f32), pltpu.VMEM((tt, D), f32)],
        compiler_params=_cparams("arbitrary"),
    )(p_qkv, p_qkv, p_qkv, p_qkv, p_qkv, p_qkv, p_qkv, p_qkv, p_qkv,
      dqn, dkn, dva, dqn, dkn, dva, p_ba, dgt, cw, cw, cw, alog_v, dt_v)


def _dot_b(a, b, dims):
    return lax.dot_general(a.astype(bf16), b.astype(bf16), (dims, ((), ())), preferred_element_type=f32)


def _inverse_by_doubling(ms):
    heads = range(len(ms))
    r = lax.broadcasted_iota(jnp.int32, (CH, CH), 0)
    c = lax.broadcasted_iota(jnp.int32, (CH, CH), 1)
    eye = jnp.where(r == c, 1.0, 0.0).astype(f32)
    p = [eye + ms[h] for h in heads]
    mp = ms
    for _ in range(5):
        mp = [_dot_b(mp[h], mp[h], _NN) for h in heads]
        pm = [_dot_b(p[h], mp[h], _NN) for h in heads]
        p = [p[h] + pm[h] for h in heads]
    return tuple(p)


@jax.custom_vjp
def _known_inverse(ms, ps):
    return ps


def _known_inverse_fwd(ms, ps):
    return ps, ps


def _known_inverse_bwd(ps, cts):
    heads = range(len(ps))
    left = [_dot_b(ps[h], cts[h], _TN) for h in heads]
    return tuple(_dot_b(left[h], ps[h], _NT) for h in heads), tuple(jnp.zeros_like(p) for p in ps)


_known_inverse.defvjp(_known_inverse_fwd, _known_inverse_bwd)


def _chunk_prepare(qs, ks, vs, gcs, bbs, ps=None):
    heads = range(len(qs))
    r = lax.broadcasted_iota(jnp.int32, (CH, CH), 0)
    c = lax.broadcasted_iota(jnp.int32, (CH, CH), 1)
    causal = r >= c
    strict = r > c
    gc_row = [gcs[h].T[:CH, :] for h in heads]
    decay = [jnp.where(causal, jnp.exp(jnp.where(causal, gcs[h][:, :CH] - gc_row[h], 0.0)), 0.0) for h in heads]
    kb = [ks[h] * bbs[h] for h in heads]
    egc = [jnp.exp(gcs[h]) for h in heads]
    kk = [_dot_b(kb[h], ks[h], _NT) for h in heads]
    qk = [_dot_b(qs[h], ks[h], _NT) for h in heads]
    m = tuple(-jnp.where(strict, kk[h] * decay[h], 0.0) for h in heads)
    p = _inverse_by_doubling(m) if ps is None else _known_inverse(m, ps)
    u = [_dot_b(p[h], vs[h] * bbs[h], _NN) for h in heads]
    w = [_dot_b(p[h], kb[h] * egc[h], _NN) for h in heads]
    intra = [jnp.where(causal, qk[h] * decay[h], 0.0) for h in heads]
    g_last = [gcs[h][CH - 1:CH, :] for h in heads]
    k_dec = [ks[h] * jnp.exp(g_last[h] - gcs[h]) for h in heads]
    q_dec = [qs[h] * egc[h] for h in heads]
    e_last = [jnp.exp(g_last[h]) for h in heads]
    return u, w, intra, q_dec, k_dec, e_last, p


def _chunk_apply(u, w, intra, q_dec, k_dec, e_last, ss):
    heads = range(len(ss))
    ws = [_dot_b(w[h], ss[h], _NN) for h in heads]
    qs_s = [_dot_b(q_dec[h], ss[h], _NN) for h in heads]
    v_new = [u[h] - ws[h] for h in heads]
    iv = [_dot_b(intra[h], v_new[h], _NN) for h in heads]
    kv = [_dot_b(k_dec[h], v_new[h], _TN) for h in heads]
    o = tuple(qs_s[h] + iv[h] for h in heads)
    s_new = tuple(ss[h] * e_last[h] + kv[h] for h in heads)
    return o, s_new


def _chunk_group_fn(ins, ss, ps=None):
    n = len(ss)
    prep = _chunk_prepare(*(sum((tuple(c[i]) for c in ins), ()) for i in range(5)), ps=ps)
    outs, befores = [], []
    for g in range(len(ins)):
        befores.append(ss)
        o, ss = _chunk_apply(*(x[g * n:(g + 1) * n] for x in prep[:6]), ss)
        outs.append(o)
    return tuple(outs), tuple(befores), ss, prep[6]


def _head_cols():
    return [slice(h * HD, (h + 1) * HD) for h in range(NH)]


def _head_gates(gt):
    gcs = tuple(jnp.broadcast_to(gt[:, NH + h:NH + h + 1], (CH, HD)) for h in range(NH))
    bbs = tuple(jnp.broadcast_to(gt[:, h:h + 1], (CH, HD)) for h in range(NH))
    return gcs, bbs


def _gdn_scan_fwd(qn, kn, va, gates, *, tt):
    t = qn.shape[0]
    cpb = tt // CH
    group = min(SCAN_GROUP, cpb)

    def body(q_ref, k_ref, v_ref, gt_ref, o_ref, st_ref, p_ref, s_scr):
        @pl.when(pl.program_id(0) == 0)
        def _():
            s_scr[...] = jnp.zeros_like(s_scr)

        cols = _head_cols()

        def inputs(ci):
            rows = pl.ds(pl.multiple_of(ci * CH, CH), CH)
            gcs, bbs = _head_gates(gt_ref[rows, :])
            return tuple(tuple(ref[rows, cl] for cl in cols) for ref in (q_ref, k_ref, v_ref)) + (gcs, bbs)

        def step(gi, carry):
            chunks = [group * gi + g for g in range(group)]
            outs, befores, s_end, p = _chunk_group_fn([inputs(ci) for ci in chunks], tuple(s_scr[h] for h in range(NH)))
            for g, ci in enumerate(chunks):
                rows = pl.ds(pl.multiple_of(ci * CH, CH), CH)
                for h in range(NH):
                    st_ref[ci, h] = befores[g][h]
                    o_ref[rows, cols[h]] = outs[g][h]
                    p_ref[ci, h] = p[g * NH + h].astype(bf16)
            for h in range(NH):
                s_scr[h] = s_end[h]
            return carry

        lax.fori_loop(0, cpb // group, step, 0)

    blk = pl.BlockSpec((tt, D), lambda i: (i, 0))
    return pl.pallas_call(
        body, name="gdn_scan_fwd", grid=(t // tt,),
        in_specs=[blk] * 3 + [pl.BlockSpec((tt, HD), lambda i: (i, 0))],
        out_specs=[blk, pl.BlockSpec((cpb, NH, HD, HD), lambda i: (i, 0, 0, 0)),
                   pl.BlockSpec((cpb, NH, CH, CH), lambda i: (i, 0, 0, 0))],
        out_shape=[jax.ShapeDtypeStruct((t, D), f32), jax.ShapeDtypeStruct((t // CH, NH, HD, HD), f32),
                   jax.ShapeDtypeStruct((t // CH, NH, CH, CH), bf16)],
        scratch_shapes=[pltpu.VMEM((NH, HD, HD), f32)],
        compiler_params=_cparams("arbitrary"),
    )(qn, kn, va, gates)


def _gdn_scan_bwd(qn, kn, va, gates, states, inverses, do, *, tt):
    t = qn.shape[0]
    nblk = t // tt
    cpb = tt // CH

    def body(q_ref, k_ref, v_ref, gt_ref, st_ref, p_ref, do_ref, dq_ref, dk_ref, dv_ref, dgt_ref, ds_scr):
        @pl.when(pl.program_id(0) == 0)
        def _():
            ds_scr[...] = jnp.zeros_like(ds_scr)

        cols = _head_cols()

        def rows_of(ci):
            return pl.ds(pl.multiple_of(ci * CH, CH), CH)

        def inputs(ci):
            gcs, bbs = _head_gates(gt_ref[rows_of(ci), :])
            return tuple(tuple(ref[rows_of(ci), cl] for cl in cols) for ref in (q_ref, k_ref, v_ref)) + (gcs, bbs)

        def step(j, carry):
            ci = cpb - 1 - j
            ps = tuple(p_ref[ci, h].astype(f32) for h in range(NH))

            def one(ins, ss):
                outs, _, s_end, _ = _chunk_group_fn([ins], ss, ps=ps)
                return outs[0], s_end

            _, vjp = jax.vjp(one, inputs(ci), tuple(st_ref[ci, h] for h in range(NH)))
            grads, ds = vjp((tuple(do_ref[rows_of(ci), cl] for cl in cols), tuple(ds_scr[h] for h in range(NH))))
            lane = lax.broadcasted_iota(jnp.int32, (CH, HD), 1)
            dgt = jnp.zeros((CH, HD), f32)
            for h in range(NH):
                for ref, g in zip((dq_ref, dk_ref, dv_ref), grads[:3]):
                    ref[rows_of(ci), cols[h]] = g[h]
                dgt = dgt + jnp.where(lane == NH + h, jnp.sum(grads[3][h], axis=-1, keepdims=True), 0.0)
                dgt = dgt + jnp.where(lane == h, jnp.sum(grads[4][h], axis=-1, keepdims=True), 0.0)
                ds_scr[h] = ds[h]
            dgt_ref[rows_of(ci), :] = dgt
            return carry

        lax.fori_loop(0, cpb, step, 0)

    blk = pl.BlockSpec((tt, D), lambda i: (nblk - 1 - i, 0))
    sblk = pl.BlockSpec((cpb, NH, HD, HD), lambda i: (nblk - 1 - i, 0, 0, 0))
    sds = jax.ShapeDtypeStruct((t, D), f32)
    gblk = pl.BlockSpec((tt, HD), lambda i: (nblk - 1 - i, 0))
    pblk = pl.BlockSpec((cpb, NH, CH, CH), lambda i: (nblk - 1 - i, 0, 0, 0))
    return pl.pallas_call(
        body, name="gdn_scan_bwd", grid=(nblk,),
        in_specs=[blk] * 3 + [gblk, sblk, pblk, blk],
        out_specs=[blk] * 3 + [gblk], out_shape=[sds] * 3 + [jax.ShapeDtypeStruct((t, HD), f32)],
        scratch_shapes=[pltpu.VMEM((NH, HD, HD), f32)],
        compiler_params=_cparams("arbitrary"),
    )(qn, kn, va, gates, states, inverses, do)


def _rms_heads(o):
    ons, rs = [], []
    for h in range(NH):
        a = o[:, h * HD:(h + 1) * HD]
        r = lax.rsqrt(jnp.mean(a * a, axis=-1, keepdims=True) + RMS_EPS)
        ons.append(a * r)
        rs.append(jnp.broadcast_to(r, a.shape))
    return jnp.concatenate(ons, axis=-1), jnp.concatenate(rs, axis=-1)


def _gdn_post_fwd(o, p_gz, ng_b, w_out, *, tt):
    t = o.shape[0]

    def body(o_ref, gz_ref, ng_ref, w_ref, og_ref, y_ref):
        on, _ = _rms_heads(o_ref[...])
        z, _ = _silu_and_grad(gz_ref[...])
        og = (on * ng_ref[...] * z).astype(bf16)
        og_ref[...] = og
        y_ref[...] = jnp.dot(og, w_ref[...], preferred_element_type=f32)

    blk = pl.BlockSpec((tt, D), lambda i: (i, 0))
    return pl.pallas_call(
        body, name="gdn_post_fwd", grid=(t // tt,),
        in_specs=[blk, blk, pl.BlockSpec((1, D), lambda i: (0, 0)), pl.BlockSpec((D, D), lambda i: (0, 0))],
        out_specs=[blk, blk], out_shape=[jax.ShapeDtypeStruct((t, D), bf16), jax.ShapeDtypeStruct((t, D), f32)],
        compiler_params=_cparams("parallel"),
    )(o, p_gz, ng_b, w_out)


def _gdn_post_bwd(o, p_gz, ng_b, w_out, dyg, *, tt):
    t = o.shape[0]

    def body(o_ref, gz_ref, ng_ref, w_ref, dyg_ref, do_ref, dgz_ref, dng_ref):
        @pl.when(pl.program_id(0) == 0)
        def _():
            dng_ref[...] = jnp.zeros_like(dng_ref)

        on, r = _rms_heads(o_ref[...])
        z, dz = _silu_and_grad(gz_ref[...])
        dog_v = lax.dot_general(dyg_ref[...], w_ref[...], (_NT, ((), ())), preferred_element_type=f32)
        ng = ng_ref[...]
        dgz_ref[...] = (dog_v * on * ng * dz).astype(bf16)
        dy = dog_v * z
        dng_all = jnp.sum(dy * on, axis=0, keepdims=True)
        dng = dng_all[:, 0:HD]
        for h in range(1, NH):
            dng = dng + dng_all[:, h * HD:(h + 1) * HD]
        dng_ref[0:1, :] += dng
        don = dy * ng
        prod = don * on
        parts = []
        for h in range(NH):
            sl = slice(h * HD, (h + 1) * HD)
            parts.append(don[:, sl] - on[:, sl] * jnp.mean(prod[:, sl], axis=-1, keepdims=True))
        do_ref[...] = r * jnp.concatenate(parts, axis=-1)

    blk = pl.BlockSpec((tt, D), lambda i: (i, 0))
    return pl.pallas_call(
        body, name="gdn_post_bwd", grid=(t // tt,),
        in_specs=[blk, blk, pl.BlockSpec((1, D), lambda i: (0, 0)), pl.BlockSpec((D, D), lambda i: (0, 0)), blk],
        out_specs=[blk, blk, pl.BlockSpec((8, HD), lambda i: (0, 0))],
        out_shape=[jax.ShapeDtypeStruct((t, D), f32), jax.ShapeDtypeStruct((t, D), bf16),
                   jax.ShapeDtypeStruct((8, HD), f32)],
        compiler_params=_cparams("arbitrary"),
    )(o, p_gz, ng_b, w_out, dyg)


def _merge(x, y_conf, y_gdn, p_gates, target, w_o, ln_g, ln_b, *, tt):
    t = x.shape[0]

    def body(x_ref, yc_ref, yg_ref, gc_ref, gg_ref, tg_ref, w_ref, g_ref, b_ref,
             loss_ref, dxd_ref, dyc_ref, dyg_ref, dpg_ref, h_ref, dz_ref, dvec_ref):
        @pl.when(pl.program_id(0) == 0)
        def _():
            loss_ref[...] = jnp.zeros_like(loss_ref)
            dvec_ref[...] = jnp.zeros_like(dvec_ref)

        sc = _sigmoid(gc_ref[...])
        sg = _sigmoid(gg_ref[...])
        yc = yc_ref[...]
        yg = yg_ref[...]
        h = (sc * yc + sg * yg).astype(bf16)
        h_ref[...] = h
        z = DN_ALPHA * x_ref[...] + jnp.dot(h, w_ref[...], preferred_element_type=f32)
        mu = jnp.mean(z, axis=-1, keepdims=True)
        cen = z - mu
        rstd = lax.rsqrt(jnp.mean(cen * cen, axis=-1, keepdims=True) + LN_EPS)
        xhat = cen * rstd
        err = xhat * g_ref[...] + b_ref[...] - tg_ref[...]
        loss_ref[...] += 0.5 / D * jnp.sum(err * err)
        dy = err * (1.0 / D)
        dvec_ref[0:1, :] += jnp.sum(dy * xhat, axis=0, keepdims=True)
        dvec_ref[1:2, :] += jnp.sum(dy, axis=0, keepdims=True)
        dxhat = dy * g_ref[...]
        dz = rstd * (dxhat - jnp.mean(dxhat, axis=-1, keepdims=True)
                     - xhat * jnp.mean(dxhat * xhat, axis=-1, keepdims=True))
        dxd_ref[...] = DN_ALPHA * dz
        dz_b = dz.astype(bf16)
        dz_ref[...] = dz_b
        dh = lax.dot_general(dz_b, w_ref[...], (_NT, ((), ())), preferred_element_type=f32)
        dyc_ref[...] = (dh * sc).astype(bf16)
        dyg_ref[...] = (dh * sg).astype(bf16)
        dpg_ref[:, 0:D] = (dh * yc * sc * (1.0 - sc)).astype(bf16)
        dpg_ref[:, D:] = (dh * yg * sg * (1.0 - sg)).astype(bf16)

    blk = pl.BlockSpec((tt, D), lambda i: (i, 0))
    wblk = pl.BlockSpec((D, D), lambda i: (0, 0))
    vec = pl.BlockSpec((1, D), lambda i: (0, 0))
    return pl.pallas_call(
        body, name="merge_norm_loss", grid=(t // tt,),
        in_specs=[blk, blk, blk, pl.BlockSpec((tt, D), lambda i: (i, 0)), pl.BlockSpec((tt, D), lambda i: (i, 1)),
                  blk, wblk, vec, vec],
        out_specs=[pl.BlockSpec((8, HD), lambda i: (0, 0)), blk, blk, blk,
                   pl.BlockSpec((tt, 2 * D), lambda i: (i, 0)), blk, blk, pl.BlockSpec((8, D), lambda i: (0, 0))],
        out_shape=[jax.ShapeDtypeStruct((8, HD), f32), jax.ShapeDtypeStruct((t, D), f32),
                   jax.ShapeDtypeStruct((t, D), bf16), jax.ShapeDtypeStruct((t, D), bf16),
                   jax.ShapeDtypeStruct((t, 2 * D), bf16), jax.ShapeDtypeStruct((t, D), bf16),
                   jax.ShapeDtypeStruct((t, D), bf16), jax.ShapeDtypeStruct((8, D), f32)],
        compiler_params=_cparams("arbitrary"),
    )(x, y_conf, y_gdn, p_gates, p_gates, target, w_o, ln_g, ln_b)


def _place():
    return lax.axis_index("x"), lax.axis_index("y"), lax.axis_index("c")


def _any_specs(n):
    return [pl.BlockSpec(memory_space=pl.ANY)] * n


def _sibling_merge(arrs, name, take_other_half=False):
    k = len(arrs)

    def body(*refs):
        a_refs, o_refs = refs[:k], refs[k:2 * k]
        send_sems, recv_sems = refs[2 * k:]
        x, y, c = _place()
        sends = []
        for i in range(k):
            src = a_refs[i]
            if take_other_half:
                n = a_refs[i].shape[-2] // 2
                lead = (slice(None),) * (len(a_refs[i].shape) - 2)
                src = a_refs[i].at[lead + (pl.ds((1 - c) * n, n), slice(None))]
            cp = pltpu.make_async_remote_copy(src_ref=src, dst_ref=o_refs[i], send_sem=send_sems.at[i],
                                              recv_sem=recv_sems.at[i], device_id=(x, y, 1 - c), device_id_type=MESH)
            cp.start()
            sends.append(cp)
        for cp in sends:
            cp.wait()

    def out_sds(a):
        rows = a.shape[-2] // 2 if take_other_half else a.shape[-2]
        return jax.ShapeDtypeStruct(a.shape[:-2] + (rows, a.shape[-1]), a.dtype)

    return pl.pallas_call(
        body, name=name, in_specs=_any_specs(k), out_specs=_any_specs(k),
        out_shape=[out_sds(a) for a in arrs],
        scratch_shapes=[pltpu.SemaphoreType.DMA((k,)), pltpu.SemaphoreType.DMA((k,))],
    )(*arrs)


def _join_halves(mine, other):
    c = lax.axis_index("c")
    return jnp.concatenate([jnp.where(c == 0, mine, other), jnp.where(c == 0, other, mine)], axis=-2)


def _chip_exchange_ops(a_refs, o_refs, send_sems, recv_sems, local_sems, scatter):
    k = len(a_refs)
    x, y, c = _place()
    me = 2 * x + y
    peers = [(1 - x, y), (x, 1 - y), (1 - x, 1 - y)]

    def src(i, j):
        return a_refs[i].at[j] if scatter else a_refs[i]

    def copy(i, n, send_j, slot):
        px, py = peers[n]
        return pltpu.make_async_remote_copy(
            src_ref=src(i, send_j), dst_ref=o_refs[i].at[slot], send_sem=send_sems.at[3 * i + n],
            recv_sem=recv_sems.at[3 * i + n], device_id=(px, py, c), device_id_type=MESH)

    def owns():
        return [pltpu.make_async_copy(src(i, me), o_refs[i].at[me], local_sems.at[i]) for i in range(k)]

    def sends():
        return [copy(i, n, 2 * peers[n][0] + peers[n][1], me) for n in range(3) for i in range(k)]

    def start():
        for cp in owns() + sends():
            cp.start()

    def finish():
        for n in range(3):
            for i in range(k):
                copy(i, n, me, 2 * peers[n][0] + peers[n][1]).wait_recv()
        for cp in sends():
            cp.wait_send()
        for cp in owns():
            cp.wait()

    return start, finish


def _chip_exchange_shapes(arrs, scatter):
    return [jax.ShapeDtypeStruct((N_CHIPS,) + tuple(a.shape[1:] if scatter else a.shape), a.dtype) for a in arrs]


def _chip_exchange_sems(k):
    return [pltpu.SemaphoreType.DMA((3 * k,)), pltpu.SemaphoreType.DMA((3 * k,)), pltpu.SemaphoreType.DMA((k,))]


def _chip_exchange(arrs, name, scatter):
    k = len(arrs)

    def body(*refs):
        start, finish = _chip_exchange_ops(refs[:k], refs[k:2 * k], *refs[2 * k:], scatter)
        start()
        finish()

    return pl.pallas_call(
        body, name=name, in_specs=_any_specs(k), out_specs=_any_specs(k),
        out_shape=_chip_exchange_shapes(arrs, scatter), scratch_shapes=_chip_exchange_sems(k),
    )(*arrs)


def _pair_sum(g_all, got, c_arr, name, out_dtype):
    n, w = got.shape[1:]
    tile = n // 4
    n_tiles = n // tile

    def body(c_ref, a_ref, b_ref, o_ref):
        o_ref[...] = (a_ref[...] + b_ref[...]).astype(out_dtype)

    return pl.pallas_call(
        body, name=name,
        grid_spec=pltpu.PrefetchScalarGridSpec(
            num_scalar_prefetch=1, grid=(N_CHIPS, n_tiles),
            in_specs=[pl.BlockSpec((1, tile, w), lambda j, i, c_ref: (j, c_ref[0] * n_tiles + i, 0)),
                      pl.BlockSpec((1, tile, w), lambda j, i, c_ref: (j, i, 0))],
            out_specs=pl.BlockSpec((1, tile, w), lambda j, i, c_ref: (j, i, 0))),
        out_shape=jax.ShapeDtypeStruct(got.shape, out_dtype),
        compiler_params=_cparams("parallel", "parallel"),
    )(c_arr, g_all, got)


def _sum_slots(a, name):
    n, w = a.shape[1:]
    tile = n // 4

    def body(a_ref, o_ref):
        o_ref[...] = ((a_ref[0].astype(f32) + a_ref[1].astype(f32)) + a_ref[2].astype(f32)) + a_ref[3].astype(f32)

    return pl.pallas_call(
        body, name=name, grid=(n // tile,),
        in_specs=[pl.BlockSpec((N_CHIPS, tile, w), lambda i: (0, i, 0))],
        out_specs=pl.BlockSpec((tile, w), lambda i: (i, 0)),
        out_shape=jax.ShapeDtypeStruct((n, w), f32),
        compiler_params=_cparams("parallel"),
    )(a)


def _adamw(w, g, m, v, name):
    rows, width = w.shape
    by_rows = rows % 64 == 0
    c1 = 1.0 / (1.0 - ADAM_B1 ** ADAM_STEP)
    c2 = 1.0 / (1.0 - ADAM_B2 ** ADAM_STEP)

    def body(w_ref, g_ref, m_ref, v_ref, d_ref, mo_ref, vo_ref):
        g_v = g_ref[...]
        m_new = ADAM_B1 * m_ref[...] + (1.0 - ADAM_B1) * g_v
        v_new = ADAM_B2 * v_ref[...] + (1.0 - ADAM_B2) * (g_v * g_v)
        mo_ref[...] = m_new
        vo_ref[...] = v_new
        d_ref[...] = -ADAM_LR * ((m_new * c1) / (jnp.sqrt(v_new * c2) + ADAM_EPS) + ADAM_WD * w_ref[...])

    blk = pl.BlockSpec((rows // 8, width), lambda i: (i, 0)) if by_rows else pl.BlockSpec((rows, LANE), lambda i: (0, i))
    sds = jax.ShapeDtypeStruct((rows, width), f32)
    return pl.pallas_call(
        body, name=name, grid=(8 if by_rows else width // LANE,),
        in_specs=[blk] * 4, out_specs=[blk] * 3, out_shape=[sds] * 3,
        compiler_params=_cparams("parallel"),
    )(w, g, m, v)


R_DW = 3 * SQ_BLK
R_CW = R_DW + 8
R_VEC = R_CW + 8
R_SMALL = R_VEC + 8
REST_ROWS = 896


def _pack_small(conf_dw_w, gdn_conv_w, vecs, a_log, dt_bias, norm_g):
    dw = jnp.pad(conf_dw_w.reshape(-1), (0, 8 * D - KC * SQ_BLK)).reshape(8, D)
    cw = jnp.pad(gdn_conv_w.reshape(-1), (0, 5 * D)).reshape(8, D)
    vec = jnp.pad(jnp.stack(vecs), ((0, 3), (0, 0)))
    small = jnp.pad(jnp.concatenate([a_log, dt_bias, norm_g]), (0, D - 2 * NH - HD)).reshape(1, D)
    return jnp.pad(jnp.concatenate([dw, cw, vec, small], axis=0), ((0, REST_ROWS - R_SMALL - 1), (0, 0)))


def _pack_rest(conf_w_out, gdn_w_out, w_o, small):
    return jnp.concatenate([conf_w_out, gdn_w_out, w_o, small], axis=0)


def _unpack_rest(p):
    conf_dw_w = p[R_DW:R_DW + 8].reshape(-1)[:KC * SQ_BLK].reshape(KC, SQ_BLK)
    gdn_conv_w = p[R_CW:R_CW + 3].reshape(KG, 3 * SQ_BLK)
    small = p[R_SMALL]
    return dict(conf_w_out=p[0:SQ_BLK], gdn_w_out=p[SQ_BLK:2 * SQ_BLK], w_o=p[2 * SQ_BLK:R_DW],
                conf_dw_w=conf_dw_w, gdn_conv_w=gdn_conv_w, conf_dw_b=p[R_VEC], conf_ln_g=p[R_VEC + 1],
                conf_ln_b=p[R_VEC + 2], post_ln_g=p[R_VEC + 3], post_ln_b=p[R_VEC + 4],
                gdn_A_log=small[0:NH], gdn_dt_bias=small[NH:2 * NH], gdn_norm_g=small[2 * NH:2 * NH + HD])


_WEIGHT_ORDER = ("w_in", "conf_dw_w", "conf_dw_b", "conf_ln_g", "conf_ln_b", "conf_w_out", "gdn_conv_w",
                 "gdn_A_log", "gdn_dt_bias", "gdn_norm_g", "gdn_w_out", "w_o", "post_ln_g", "post_ln_b")


def _gather_weights(w_in, conf_w_out, gdn_w_out, w_o, conf_dw_w, gdn_conv_w):
    c = lax.axis_index("c")
    sq = jnp.concatenate([conf_w_out, gdn_w_out, w_o], axis=0).astype(bf16)
    w_half = lax.dynamic_slice_in_dim(w_in.astype(bf16), c * (D // 2), D // 2, axis=0)
    sq_half = lax.dynamic_slice_in_dim(sq, c * (sq.shape[0] // 2), sq.shape[0] // 2, axis=0)
    small = jnp.concatenate([jnp.pad(conf_dw_w.reshape(-1), (0, 8 * D - KC * SQ_BLK)).reshape(8, D),
                             jnp.pad(gdn_conv_w.reshape(-1), (0, 5 * D)).reshape(8, D)], axis=0)
    got_w, got_sq, small_all = _chip_exchange([w_half, sq_half, small], "weight_gather_chips", scatter=False)
    oth_w, oth_sq = _sibling_merge([got_w, got_sq], "weight_gather_sibling")
    w4 = _join_halves(got_w, oth_w)
    sq4 = _join_halves(got_sq, oth_sq)
    sq_full = [sq4[:, n * SQ_BLK:(n + 1) * SQ_BLK].reshape(D, D) for n in range(3)]
    dw_full = small_all[:, 0:8].reshape(N_CHIPS, 8 * D)[:, :KC * SQ_BLK].reshape(N_CHIPS, KC, SQ_BLK)
    dw_full = dw_full.transpose(1, 0, 2).reshape(KC, D)
    cw_full = small_all[:, 8:11].reshape(N_CHIPS, KG, 3 * SQ_BLK).transpose(1, 0, 2).reshape(KG, 3 * D)
    return w4, sq_full[0], sq_full[1], sq_full[2], dw_full, cw_full


def _w_in_cols(w4, lo, hi):
    parts = []
    for j in range(N_CHIPS):
        a, b = max(lo, j * W_IN_BLK), min(hi, (j + 1) * W_IN_BLK)
        if a < b:
            parts.append(w4[j, :, a - j * W_IN_BLK:b - j * W_IN_BLK])
    return parts[0] if len(parts) == 1 else jnp.concatenate(parts, axis=1)


def _w_in_by_chip(pieces):
    chips = []
    for j in range(N_CHIPS):
        lo, hi = j * W_IN_BLK, (j + 1) * W_IN_BLK
        parts = []
        for start, arr in pieces:
            a, b = max(lo, start), min(hi, start + arr.shape[1])
            if a < b:
                parts.append(arr[:, a - start:b - start])
        chips.append(jnp.concatenate(parts, axis=1))
    return jnp.stack(chips)


def _pair_sums(g_w, g_rest):
    c_arr = lax.axis_index("c").astype(jnp.int32).reshape(1)
    got_w, got_r = _sibling_merge([g_w, g_rest], "grad_sibling_halves", take_other_half=True)
    pair_w = _pair_sum(g_w, got_w, c_arr, "grad_pair_sum_w_in", bf16)
    pair_r = _pair_sum(g_rest, got_r, c_arr, "grad_pair_sum_rest", f32)
    return pair_w, pair_r


def _chip_sums(all_w, all_r):
    tot_w, tot_r = _sum_slots(all_w, "grad_chip_sum_w_in"), _sum_slots(all_r, "grad_chip_sum_rest")
    oth_w, oth_r = _sibling_merge([tot_w, tot_r], "grad_sibling_result")
    return _join_halves(tot_w, oth_w), _join_halves(tot_r, oth_r)


def _local_step(x2, tgt, w4, wc_out, wg_out, wo_full, dw_full, cw_full, conf_dw_b, conf_ln_g, conf_ln_b,
                gdn_A_log, gdn_dt_bias, gdn_norm_g, post_ln_g, post_ln_b):
    t = x2.shape[0]
    tt = min(TOKEN_TILE, t)
    tm = min(1024, t)

    w_conv, w_qkv, w_gz = _w_in_cols(w4, 0, 3 * D), _w_in_cols(w4, 3 * D, 6 * D), _w_in_cols(w4, 6 * D, 7 * D)
    w_gates = _w_in_cols(w4, 7 * D + 2 * NH, W_IN_COLS)
    dw_pad = jnp.pad(dw_full, ((0, HALO_C - KC), (0, 0)))
    cw_pad = jnp.pad(cw_full, ((0, 8 - KG), (0, 0)))
    row = lambda v: v.reshape(1, D)
    alog_v = jnp.pad(gdn_A_log, (NH, HD - 2 * NH)).reshape(1, HD)
    dt_v = jnp.pad(gdn_dt_bias, (NH, HD - 2 * NH)).reshape(1, HD)
    ng_b = row(jnp.tile(gdn_norm_g, NH))
    w_ba = jnp.pad(_w_in_cols(w4, 7 * D, 7 * D + 2 * NH), ((0, 0), (0, HD - 2 * NH)))

    x_b = x2.astype(bf16)

    p_conv = _mm_multi([x_b], [w_conv], out_dtype=f32, tm=tm, tn=1024, name="proj_conv")
    p_qkv = _mm_multi([x_b], [w_qkv], out_dtype=f32, tm=tm, tn=1024, name="proj_qkv")
    p_gz = _mm_multi([x_b], [w_gz], out_dtype=f32, tm=tm, tn=1024, name="proj_gz")
    p_gates = _mm_multi([x_b], [w_gates], out_dtype=f32, tm=tm, tn=1024, name="proj_gates")
    p_ba = _mm_multi([x_b], [w_ba], out_dtype=f32, tm=tm, tn=HD, name="proj_ba")

    u, a1, y_conf = _conv_fwd(p_conv, dw_pad, row(conf_dw_b), row(conf_ln_g), row(conf_ln_b), wc_out, tt=tt)

    qn, kn, va, gates = _gdn_pre_fwd(p_qkv, p_ba, cw_pad, alog_v, dt_v, tt=tt)
    o, states, inverses = _gdn_scan_fwd(qn, kn, va, gates, tt=tt)
    og, y_gdn = _gdn_post_fwd(o, p_gz, ng_b, wg_out, tt=tt)

    loss_blk, dxd, dyc, dyg, dp_gates, h, dz, dpost = _merge(
        x2, y_conf, y_gdn, p_gates, tgt, wo_full, row(post_ln_g), row(post_ln_b), tt=tt)

    d_wo = _mm_kloop(h, dz, tm=D, tn=1024, tk=min(512, t), name="grad_w_o")
    d_wc = _mm_kloop(u, dyc, tm=D, tn=1024, tk=min(512, t), name="grad_conf_w_out")
    d_wg = _mm_kloop(og, dyg, tm=D, tn=1024, tk=min(512, t), name="grad_gdn_w_out")

    dp_conv, d_dww, dconv_vec = _conv_bwd(p_conv, a1, dyc, wc_out, dw_pad, row(conf_ln_g), row(conf_ln_b), tt=tt)

    do, dp_gz, dng = _gdn_post_bwd(o, p_gz, ng_b, wg_out, dyg, tt=tt)
    dqn, dkn, dva, dgates = _gdn_scan_bwd(qn, kn, va, gates, states, inverses, do, tt=tt)
    dp_qkv, dp_ba, d_cw, d_ad = _gdn_pre_bwd(p_qkv, p_ba, cw_pad, alog_v, dt_v, dqn, dkn, dva, dgates, tt=tt)
    dp_ba_b = dp_ba.astype(bf16)

    grad_x_factors = ([dp_conv, dp_qkv, dp_gz, dp_gates, dp_ba_b], [w_conv, w_qkv, w_gz, w_gates, w_ba], dxd)

    tk = min(512, t)
    d_w_conv = _mm_kloop(x_b, dp_conv, tm=D, tn=1024, tk=tk, name="grad_w_in_conv")
    d_w_qkv = _mm_kloop(x_b, dp_qkv, tm=D, tn=1024, tk=tk, name="grad_w_in_qkv")
    d_w_gz = _mm_kloop(x_b, dp_gz, tm=D, tn=1024, tk=tk, name="grad_w_in_gz")
    d_w_gates = _mm_kloop(x_b, dp_gates, tm=D, tn=1024, tk=tk, name="grad_w_in_gates")
    d_w_ba = _mm_kloop(x_b, dp_ba_b, tm=D, tn=HD, tk=tk, name="grad_w_in_ba")
    d_w_in = _w_in_by_chip([(0, d_w_conv), (3 * D, d_w_qkv), (6 * D, d_w_gz), (7 * D, d_w_ba[:, :2 * NH]),
                            (7 * D + 2 * NH, d_w_gates)])

    return (loss_blk[0, 0], grad_x_factors, d_w_in, d_wc, d_wg, d_wo, d_dww, d_cw, dconv_vec, dpost, d_ad, dng)


def kernel(x, w_in, conf_dw_w, conf_dw_b, conf_ln_g, conf_ln_b, conf_w_out, gdn_conv_w, gdn_A_log, gdn_dt_bias, gdn_norm_g, gdn_w_out, w_o, post_ln_g, post_ln_b, loss_target, m_w_in, m_conf_dw_w, m_conf_dw_b, m_conf_ln_g, m_conf_ln_b, m_conf_w_out, m_gdn_conv_w, m_gdn_A_log, m_gdn_dt_bias, m_gdn_norm_g, m_gdn_w_out, m_w_o, m_post_ln_g, m_post_ln_b, v_w_in, v_conf_dw_w, v_conf_dw_b, v_conf_ln_g, v_conf_ln_b, v_conf_w_out, v_gdn_conv_w, v_gdn_A_log, v_gdn_dt_bias, v_gdn_norm_g, v_gdn_w_out, v_w_o, v_post_ln_g, v_post_ln_b):
    x2 = x.reshape(x.shape[-2], D)
    tgt = loss_target.reshape(x2.shape)
    w4, wc_out, wg_out, wo_full, dw_full, cw_full = _gather_weights(
        w_in, conf_w_out, gdn_w_out, w_o, conf_dw_w, gdn_conv_w)
    (loss_part, grad_x_factors, d_w_in, d_wc, d_wg, d_wo, d_dww, d_cw, dconv_vec, dpost, d_ad, dng) = _local_step(
        x2, tgt, w4, wc_out, wg_out, wo_full, dw_full, cw_full, conf_dw_b, conf_ln_g, conf_ln_b,
        gdn_A_log, gdn_dt_bias, gdn_norm_g, post_ln_g, post_ln_b)
    loss = lax.psum(loss_part, ("x", "y", "c"))

    dww_c = d_dww[:KC].reshape(KC, N_CHIPS, SQ_BLK)
    dcw_c = d_cw[:KG].reshape(KG, N_CHIPS, 3 * SQ_BLK)
    vecs = [dconv_vec[0], dconv_vec[1], dconv_vec[2], dpost[0], dpost[1]]
    g_rest = jnp.stack([
        _pack_rest(d_wc[j * SQ_BLK:(j + 1) * SQ_BLK], d_wg[j * SQ_BLK:(j + 1) * SQ_BLK], d_wo[j * SQ_BLK:(j + 1) * SQ_BLK],
                   _pack_small(dww_c[:, j], dcw_c[:, j], vecs, d_ad[0, NH:2 * NH], d_ad[1, NH:2 * NH], dng[0]))
        for j in range(N_CHIPS)])
    pair_w, pair_r = _pair_sums(d_w_in, g_rest)
    grad_x, all_w, all_r = _mm_multi(*grad_x_factors, out_dtype=f32, tm=min(256, x2.shape[0]), tn=512,
                                     name="grad_x_and_chip_scatter", rhs_t=True, scatter=[pair_w, pair_r])
    g_w_in, g_rest = _chip_sums(all_w, all_r)

    def rest_of(w_c, w_g, w_oo, dw, cw, b1, g1, b2, g2, b3, a_log, dt_bias, norm_g):
        return _pack_rest(w_c, w_g, w_oo, _pack_small(dw, cw, [b1, g1, b2, g2, b3], a_log, dt_bias, norm_g))

    w_r = rest_of(conf_w_out, gdn_w_out, w_o, conf_dw_w, gdn_conv_w, conf_dw_b, conf_ln_g, conf_ln_b,
                  post_ln_g, post_ln_b, gdn_A_log, gdn_dt_bias, gdn_norm_g)
    m_r = rest_of(m_conf_w_out, m_gdn_w_out, m_w_o, m_conf_dw_w, m_gdn_conv_w, m_conf_dw_b, m_conf_ln_g, m_conf_ln_b,
                  m_post_ln_g, m_post_ln_b, m_gdn_A_log, m_gdn_dt_bias, m_gdn_norm_g)
    v_r = rest_of(v_conf_w_out, v_gdn_w_out, v_w_o, v_conf_dw_w, v_gdn_conv_w, v_conf_dw_b, v_conf_ln_g, v_conf_ln_b,
                  v_post_ln_g, v_post_ln_b, v_gdn_A_log, v_gdn_dt_bias, v_gdn_norm_g)
    upd_w_in = [u.T for u in _adamw(w_in.T, g_w_in.T, m_w_in.T, v_w_in.T, "adamw_w_in")]
    upd_rest = _adamw(w_r, g_rest, m_r, v_r, "adamw_rest")

    out = [loss, grad_x.reshape(x.shape)]
    for big, rest in zip((g_w_in,) + tuple(upd_w_in), (g_rest,) + tuple(upd_rest)):
        d = dict(_unpack_rest(rest), w_in=big)
        out += [d[n] for n in _WEIGHT_ORDER]
    return tuple(out)
```

```python
import jax
import jax.numpy as jnp
from jax import lax
from jax.experimental import pallas as pl
from jax.experimental.pallas import tpu as pltpu

f32 = jnp.float32
bf16 = jnp.bfloat16
HI = lax.Precision.HIGHEST
MESH = pl.DeviceIdType.MESH

D = 1024
NH = 8
HD = 128
CH = 64
KC = 31
KG = 4
HALO_C = 32
HALO_G = 8
LANE = 128
STRIP = 32
N_SHIFT = 7
LN_EPS = 1e-5
RMS_EPS = 1e-6
L2_EPS = 1e-6
DN_ALPHA = 2.0 ** 0.25
N_CHIPS = 4
W_IN_COLS = 9232
W_IN_BLK = W_IN_COLS // N_CHIPS
SQ_BLK = D // N_CHIPS
VMEM_LIMIT = 52 * 1024 * 1024
MM_TILE = 1024
MM_K_TILE = 512
GRAD_X_TILE = (256, 512)
TOKEN_TILE = 256
SCAN_GROUP = 4

ADAM_LR = 0.001
ADAM_B1 = 0.9
ADAM_B2 = 0.999
ADAM_EPS = 1e-08
ADAM_WD = 0.01
ADAM_STEP = 10


def _sigmoid(x):
    return 1.0 / (1.0 + jnp.exp(-x))


def _silu_and_grad(x):
    s = _sigmoid(x)
    return x * s, s * (1.0 + x * (1.0 - s))


_NN = ((1,), (0,))
_NT = ((1,), (1,))
_TN = ((0,), (0,))


def _cparams(*sem):
    return pltpu.CompilerParams(dimension_semantics=sem, vmem_limit_bytes=VMEM_LIMIT)


def _mm_multi(a_list, b_list, addend=None, *, out_dtype, tm, tn, name, rhs_t=False, scatter=()):
    n_pairs = len(a_list)
    m = a_list[0].shape[0]
    n = b_list[0].shape[0 if rhs_t else 1]
    has_add = addend is not None
    dims = (_NT if rhs_t else _NN, ((), ()))
    n_in = 2 * n_pairs + has_add
    k = len(scatter)
    grid = (n // tn, m // tm)

    def body(*refs):
        a_refs = refs[:n_pairs]
        b_refs = refs[n_pairs:2 * n_pairs]
        o_ref = refs[n_in + k]
        if k:
            start, finish = _chip_exchange_ops(refs[n_in:n_in + k], refs[n_in + k + 1:n_in + 2 * k + 1],
                                               *refs[n_in + 2 * k + 1:], True)
            step = pl.program_id(0) * grid[1] + pl.program_id(1)
            pl.when(step == 0)(start)
        acc = None
        for a_ref, b_ref in zip(a_refs, b_refs):
            p = lax.dot_general(a_ref[...].astype(bf16), b_ref[...].astype(bf16), dims, preferred_element_type=f32)
            acc = p if acc is None else acc + p
        if has_add:
            acc = acc + refs[2 * n_pairs][...]
        o_ref[...] = acc.astype(out_dtype)
        if k:
            pl.when(step == grid[0] * grid[1] - 1)(finish)

    in_specs = [pl.BlockSpec((tm, a.shape[1]), lambda j, i: (i, 0)) for a in a_list]
    if rhs_t:
        in_specs += [pl.BlockSpec((tn, b.shape[1]), lambda j, i: (j, 0)) for b in b_list]
    else:
        in_specs += [pl.BlockSpec((b.shape[0], tn), lambda j, i: (0, j)) for b in b_list]
    args = list(a_list) + list(b_list)
    if has_add:
        in_specs.append(pl.BlockSpec((tm, tn), lambda j, i: (i, j)))
        args.append(addend)
    out = pl.pallas_call(
        body, name=name, grid=grid,
        in_specs=in_specs + _any_specs(k), out_specs=[pl.BlockSpec((tm, tn), lambda j, i: (i, j))] + _any_specs(k),
        out_shape=[jax.ShapeDtypeStruct((m, n), out_dtype)] + _chip_exchange_shapes(scatter, True),
        scratch_shapes=_chip_exchange_sems(k) if k else [],
        compiler_params=_cparams("arbitrary", "arbitrary") if k else _cparams("parallel", "parallel"),
    )(*args, *scatter)
    return out if k else out[0]


def _mm_kloop(a, b, *, tm, tn, tk, name):
    k, m = a.shape
    n = b.shape[1]
    nk = k // tk

    def body(a_ref, b_ref, o_ref):
        @pl.when(pl.program_id(2) == 0)
        def _():
            o_ref[...] = jnp.zeros_like(o_ref)
        o_ref[...] += lax.dot_general(a_ref[...].astype(bf16), b_ref[...].astype(bf16), (_TN, ((), ())),
                                      preferred_element_type=f32)

    return pl.pallas_call(
        body, name=name, grid=(n // tn, m // tm, nk),
        in_specs=[pl.BlockSpec((tk, tm), lambda j, i, kk: (kk, i)), pl.BlockSpec((tk, tn), lambda j, i, kk: (kk, j))],
        out_specs=pl.BlockSpec((tm, tn), lambda j, i, kk: (i, j)),
        out_shape=jax.ShapeDtypeStruct((m, n), f32),
        compiler_params=_cparams("parallel", "parallel", "arbitrary"),
    )(a, b)


def _shift_copies(src_ref, sh_ref, n, shifts=tuple(range(1, 8))):
    for i, b in enumerate(shifts):
        sh_ref[i, 0:n, :] = src_ref[pl.ds(b, n), :]


def _by_residue(offs):
    groups = {}
    for k, off in enumerate(offs):
        groups.setdefault(off % 8, []).append((k, off // 8))
    return groups


def _slab(src_ref, sh_ref, shifts, b, r0, n, lanes):
    ref = src_ref if b == 0 else sh_ref.at[shifts.index(b)]
    return ref[r0:r0 + n, lanes]


def _tap_conv(out_ref, n_rows, src_ref, sh_ref, w_ref, offs, bias_ref=None, shifts=tuple(range(1, 8))):
    groups = _by_residue(offs)
    for j in range(D // LANE):
        lanes = slice(j * LANE, (j + 1) * LANE)
        wv = [w_ref[k:k + 1, lanes] for k in range(len(offs))]
        for r0 in range(0, n_rows, STRIP):
            n = min(STRIP, n_rows - r0)
            accs = [jnp.zeros((n, LANE), f32) if bias_ref is None else jnp.broadcast_to(bias_ref[0:1, lanes], (n, LANE)),
                    jnp.zeros((n, LANE), f32)]
            m = 0
            for b, taps in groups.items():
                a_lo = min(a for _, a in taps)
                a_hi = max(a for _, a in taps)
                wide = _slab(src_ref, sh_ref, shifts, b, r0 + 8 * a_lo, 8 * (a_hi - a_lo) + n, lanes)
                for k, a in taps:
                    accs[m % 2] = accs[m % 2] + wv[k] * wide[8 * (a - a_lo):8 * (a - a_lo) + n]
                    m += 1
            out_ref[r0:r0 + n, lanes] = accs[0] + accs[1]


def _tap_corr(dw_ref, n_rows, lhs_ref, src_ref, sh_ref, offs, shifts=tuple(range(1, 8))):
    groups = _by_residue(offs)
    for j in range(D // LANE):
        lanes = slice(j * LANE, (j + 1) * LANE)
        accs = [jnp.zeros((8, LANE), f32) for _ in offs]
        for r0 in range(0, n_rows, STRIP):
            n = min(STRIP, n_rows - r0)
            d = lhs_ref[r0:r0 + n, lanes]
            for b, taps in groups.items():
                a_lo = min(a for _, a in taps)
                a_hi = max(a for _, a in taps)
                wide = _slab(src_ref, sh_ref, shifts, b, r0 + 8 * a_lo, 8 * (a_hi - a_lo) + n, lanes)
                for k, a in taps:
                    prod = d * wide[8 * (a - a_lo):8 * (a - a_lo) + n]
                    part = prod[0:8]
                    for q in range(1, n // 8):
                        part = part + prod[8 * q:8 * q + 8]
                    accs[k] = accs[k] + part
        for k in range(len(offs)):
            dw_ref[k:k + 1, lanes] += jnp.sum(accs[k], axis=0, keepdims=True)


_FWD_OFFS = [HALO_C - (KC - 1) + k for k in range(KC)]
_BWD_OFFS = [KC - 1 - k for k in range(KC)]


def _norm_act(a1, cz, g_ref, bb_ref):
    mu = jnp.mean(a1, axis=-1, keepdims=True)
    cen = a1 - mu
    var = jnp.mean(cen * cen, axis=-1, keepdims=True)
    rstd = lax.rsqrt(var + LN_EPS)
    xhat = cen * rstd
    ln = xhat * g_ref[...] + bb_ref[...]
    s, ds = _silu_and_grad(ln)
    zc, dzc = _silu_and_grad(cz)
    return xhat, rstd, s, ds, zc, dzc


def _conv_fwd(p_conv, dw_w, dw_b, ln_g, ln_b, w_out, *, tt):
    t = p_conv.shape[0]
    hb = tt // HALO_C

    def body(cv_ref, cg_ref, cz_ref, cvh_ref, cgh_ref, w_ref, b_ref, g_ref, bb_ref, wo_ref,
             u_ref, a1_ref, y_ref, ext_ref, sh_ref):
        first = pl.program_id(0) == 0
        halo = cvh_ref[...] * _sigmoid(cgh_ref[...])
        ext_ref[0:HALO_C, :] = jnp.where(first, 0.0, halo)
        ext_ref[HALO_C:, :] = cv_ref[...] * _sigmoid(cg_ref[...])
        _shift_copies(ext_ref, sh_ref, tt + HALO_C - 8)
        _tap_conv(a1_ref, tt, ext_ref, sh_ref, w_ref, _FWD_OFFS, b_ref)
        _, _, s, _, zc, _ = _norm_act(a1_ref[...], cz_ref[...], g_ref, bb_ref)
        u = (s * zc).astype(bf16)
        u_ref[...] = u
        y_ref[...] = jnp.dot(u, wo_ref[...], preferred_element_type=f32)

    def main(col):
        return pl.BlockSpec((tt, D), lambda i: (i, col))

    def prev(col):
        return pl.BlockSpec((HALO_C, D), lambda i: (jnp.maximum(i * hb - 1, 0), col))

    vec = pl.BlockSpec((1, D), lambda i: (0, 0))
    return pl.pallas_call(
        body, name="conv_fwd", grid=(t // tt,),
        in_specs=[main(0), main(1), main(2), prev(0), prev(1),
                  pl.BlockSpec((HALO_C, D), lambda i: (0, 0)), vec, vec, vec, pl.BlockSpec((D, D), lambda i: (0, 0))],
        out_specs=[pl.BlockSpec((tt, D), lambda i: (i, 0))] * 3,
        out_shape=[jax.ShapeDtypeStruct((t, D), bf16), jax.ShapeDtypeStruct((t, D), f32),
                   jax.ShapeDtypeStruct((t, D), f32)],
        scratch_shapes=[pltpu.VMEM((tt + HALO_C, D), f32), pltpu.VMEM((N_SHIFT, tt + HALO_C - 8, D), f32)],
        compiler_params=_cparams("parallel"),
    )(p_conv, p_conv, p_conv, p_conv, p_conv, dw_w, dw_b, ln_g, ln_b, w_out)


def _conv_bwd(p_conv, a1, du, dw_w, ln_g, ln_b, *, tt):
    t = p_conv.shape[0]
    hb = tt // HALO_C
    n_tiles = t // tt
    last_hb = t // HALO_C - 1
    ne = tt + HALO_C

    def body(cv_ref, cg_ref, cz_ref, a1_ref, du_ref, cvp_ref, cgp_ref, czn_ref, a1n_ref, dun_ref,
             w_ref, g_ref, bb_ref, dp_ref, dww_ref, dvec_ref, ext_ref, sh_ref, da1_ref, da0_ref):
        i = pl.program_id(0)
        first = i == 0
        last = i == n_tiles - 1

        @pl.when(first)
        def _():
            dww_ref[...] = jnp.zeros_like(dww_ref)
            dvec_ref[...] = jnp.zeros_like(dvec_ref)

        sig = _sigmoid(cg_ref[...])
        ext_ref[0:HALO_C, :] = jnp.where(first, 0.0, cvp_ref[...] * _sigmoid(cgp_ref[...]))
        ext_ref[HALO_C:, :] = cv_ref[...] * sig
        a1_all = jnp.concatenate([a1_ref[...], a1n_ref[...]], axis=0)
        cz = jnp.concatenate([cz_ref[...], czn_ref[...]], axis=0)
        du_all = jnp.concatenate([du_ref[...], jnp.where(last, 0.0, dun_ref[...])], axis=0)
        xhat, rstd, s, ds, zc, dzc = _norm_act(a1_all, cz, g_ref, bb_ref)
        dln = du_all * zc * ds
        dxhat = dln * g_ref[...]
        da1 = rstd * (dxhat - jnp.mean(dxhat, axis=-1, keepdims=True)
                      - xhat * jnp.mean(dxhat * xhat, axis=-1, keepdims=True))
        da1_ref[...] = da1
        dcz = (du_all * s * dzc)[:tt]
        dvec_ref[0:1, :] += jnp.sum(da1[:tt], axis=0, keepdims=True)
        dvec_ref[1:2, :] += jnp.sum((dln * xhat)[:tt], axis=0, keepdims=True)
        dvec_ref[2:3, :] += jnp.sum(dln[:tt], axis=0, keepdims=True)
        _shift_copies(ext_ref, sh_ref, ne - 8)
        _tap_corr(dww_ref, tt, da1_ref, ext_ref, sh_ref, _FWD_OFFS)
        _shift_copies(da1_ref, sh_ref, ne - 8)
        _tap_conv(da0_ref, tt, da1_ref, sh_ref, w_ref, _BWD_OFFS)
        da0 = da0_ref[...]
        cv = cv_ref[...]
        dp_ref[:, 0:D] = (da0 * sig).astype(bf16)
        dp_ref[:, D:2 * D] = (da0 * cv * sig * (1.0 - sig)).astype(bf16)
        dp_ref[:, 2 * D:] = dcz.astype(bf16)

    def main(col):
        return pl.BlockSpec((tt, D), lambda i: (i, col))

    def prev(col):
        return pl.BlockSpec((HALO_C, D), lambda i: (jnp.maximum(i * hb - 1, 0), col))

    def nxt(col):
        return pl.BlockSpec((HALO_C, D), lambda i: (jnp.minimum((i + 1) * hb, last_hb), col))

    vec = pl.BlockSpec((1, D), lambda i: (0, 0))
    return pl.pallas_call(
        body, name="conv_bwd", grid=(n_tiles,),
        in_specs=[main(0), main(1), main(2), main(0), main(0), prev(0), prev(1), nxt(2), nxt(0), nxt(0),
                  pl.BlockSpec((HALO_C, D), lambda i: (0, 0)), vec, vec],
        out_specs=[pl.BlockSpec((tt, 3 * D), lambda i: (i, 0)),
                   pl.BlockSpec((HALO_C, D), lambda i: (0, 0)),
                   pl.BlockSpec((8, D), lambda i: (0, 0))],
        out_shape=[jax.ShapeDtypeStruct((t, 3 * D), bf16), jax.ShapeDtypeStruct((HALO_C, D), f32),
                   jax.ShapeDtypeStruct((8, D), f32)],
        scratch_shapes=[pltpu.VMEM((ne, D), f32), pltpu.VMEM((N_SHIFT, ne - 8, D), f32),
                        pltpu.VMEM((ne, D), f32), pltpu.VMEM((tt, D), f32)],
        compiler_params=_cparams("arbitrary"),
    )(p_conv, p_conv, p_conv, a1, du, p_conv, p_conv, p_conv, a1, du, dw_w, ln_g, ln_b)


def _dot_hi(a, b):
    return lax.dot_general(a, b, (((1,), (0,)), ((), ())), precision=HI, preferred_element_type=f32)


def _chunk_tri(n, lower):
    r = lax.broadcasted_iota(jnp.int32, (n, n), 0)
    c = lax.broadcasted_iota(jnp.int32, (n, n), 1)
    tri = (r >= c) if lower else (r <= c)
    return jnp.where(tri & (r // CH == c // CH), 1.0, 0.0).astype(f32)


def _softplus_and_sigmoid(x):
    e = jnp.exp(-jnp.abs(x))
    log1p = jnp.where(e < 1e-2, e * (1.0 - e * (0.5 - e * (1.0 / 3.0 - 0.25 * e))), jnp.log(1.0 + e))
    return jnp.maximum(x, 0.0) + log1p, _sigmoid(x)


_G_FWD_OFFS = [HALO_G - (KG - 1) + k for k in range(KG)]
_G_FWD_SHIFTS = (5, 6, 7)
_G_BWD_OFFS = [KG - 1 - k for k in range(KG)]
_G_BWD_SHIFTS = (1, 2, 3)


def _gdn_short_conv(pre_ref, ext_ref, sh_ref, n_rows, w_ref):
    _shift_copies(ext_ref, sh_ref, n_rows, _G_FWD_SHIFTS)
    _tap_conv(pre_ref, n_rows, ext_ref, sh_ref, w_ref, _G_FWD_OFFS, shifts=_G_FWD_SHIFTS)
    return pre_ref[...]


def _l2norm_heads(act, scale):
    outs, rs = [], []
    for h in range(NH):
        a = act[:, h * HD:(h + 1) * HD]
        r = lax.rsqrt(jnp.sum(a * a, axis=-1, keepdims=True) + L2_EPS)
        outs.append(a * (r * scale))
        rs.append(jnp.broadcast_to(r, a.shape))
    return jnp.concatenate(outs, axis=-1), jnp.concatenate(rs, axis=-1)


def _gate_math(ba, al_ref, dt_ref):
    lane = lax.broadcasted_iota(jnp.int32, ba.shape, 1)
    is_b = lane < NH
    is_a = (lane >= NH) & (lane < 2 * NH)
    sp, sg = _softplus_and_sigmoid(ba + dt_ref[...])
    neg_a = -jnp.exp(al_ref[...])
    return is_b, is_a, _sigmoid(ba), neg_a * sp, sg, neg_a


def _gdn_pre_fwd(p_qkv, p_ba, cw, alog_v, dt_v, *, tt):
    t = p_qkv.shape[0]
    hb = tt // HALO_G

    def body(q_ref, k_ref, v_ref, qh_ref, kh_ref, vh_ref, ba_ref, wq_ref, wk_ref, wv_ref, al_ref, dt_ref,
             qn_ref, kn_ref, va_ref, gt_ref, ext_ref, sh_ref, pre_ref):
        first = pl.program_id(0) == 0

        def conv_act(x_ref, xh_ref, w_ref):
            ext_ref[0:HALO_G, :] = jnp.where(first, 0.0, xh_ref[...])
            ext_ref[HALO_G:, :] = x_ref[...]
            pre = _gdn_short_conv(pre_ref, ext_ref, sh_ref, tt, w_ref)
            return pre * _sigmoid(pre)

        qn_ref[...] = _l2norm_heads(conv_act(q_ref, qh_ref, wq_ref), HD ** -0.5)[0]
        kn_ref[...] = _l2norm_heads(conv_act(k_ref, kh_ref, wk_ref), 1.0)[0]
        va_ref[...] = conv_act(v_ref, vh_ref, wv_ref)
        is_b, is_a, beta, g, _, _ = _gate_math(ba_ref[...], al_ref, dt_ref)
        gc = _dot_hi(_chunk_tri(tt, lower=True), jnp.where(is_a, g, 0.0))
        gt_ref[...] = jnp.where(is_b, beta, gc)

    def main(col):
        return pl.BlockSpec((tt, D), lambda i: (i, col))

    def prev(col):
        return pl.BlockSpec((HALO_G, D), lambda i: (jnp.maximum(i * hb - 1, 0), col))

    def wspec(col):
        return pl.BlockSpec((8, D), lambda i: (0, col))

    vec = pl.BlockSpec((1, HD), lambda i: (0, 0))
    gblk = pl.BlockSpec((tt, HD), lambda i: (i, 0))
    sds = jax.ShapeDtypeStruct((t, D), f32)
    return pl.pallas_call(
        body, name="gdn_pre_fwd", grid=(t // tt,),
        in_specs=[main(0), main(1), main(2), prev(0), prev(1), prev(2), gblk, wspec(0), wspec(1), wspec(2), vec, vec],
        out_specs=[pl.BlockSpec((tt, D), lambda i: (i, 0))] * 3 + [gblk],
        out_shape=[sds] * 3 + [jax.ShapeDtypeStruct((t, HD), f32)],
        scratch_shapes=[pltpu.VMEM((tt + HALO_G, D), f32), pltpu.VMEM((KG - 1, tt, D), f32), pltpu.VMEM((tt, D), f32)],
        compiler_params=_cparams("parallel"),
    )(p_qkv, p_qkv, p_qkv, p_qkv, p_qkv, p_qkv, p_ba, cw, cw, cw, alog_v, dt_v)


def _gdn_pre_bwd(p_qkv, p_ba, cw, alog_v, dt_v, dqn, dkn, dva, dgt, *, tt):
    t = p_qkv.shape[0]
    hb = tt // HALO_G
    n_tiles = t // tt
    last_hb = t // HALO_G - 1
    ne = tt + HALO_G

    def body(q_ref, k_ref, v_ref, qp_ref, kp_ref, vp_ref, qx_ref, kx_ref, vx_ref,
             dq_ref, dk_ref, dv_ref, dqx_ref, dkx_ref, dvx_ref, ba_ref, dgt_ref,
             wq_ref, wk_ref, wv_ref, al_ref, dt_ref,
             dp_ref, dba_ref, dcw_ref, dad_ref, ext_ref, sh_ref, pre_ref, dpre_ref, draw_ref):
        i = pl.program_id(0)
        first = i == 0
        last = i == n_tiles - 1

        @pl.when(first)
        def _():
            dcw_ref[...] = jnp.zeros_like(dcw_ref)
            dad_ref[...] = jnp.zeros_like(dad_ref)

        def one(x_ref, xp_ref, xx_ref, d_ref, dx_ref, w_ref, col, scale):
            ext_ref[0:HALO_G, :] = jnp.where(first, 0.0, xp_ref[...])
            ext_ref[HALO_G:HALO_G + tt, :] = x_ref[...]
            ext_ref[HALO_G + tt:, :] = xx_ref[...]
            pre = _gdn_short_conv(pre_ref, ext_ref, sh_ref, ne, w_ref)
            act, dact = _silu_and_grad(pre)
            d_out = jnp.concatenate([d_ref[...], jnp.where(last, 0.0, dx_ref[...])], axis=0)
            if scale is None:
                d_act = d_out
            else:
                parts = []
                for h in range(NH):
                    a = act[:, h * HD:(h + 1) * HD]
                    dn = d_out[:, h * HD:(h + 1) * HD]
                    r = lax.rsqrt(jnp.sum(a * a, axis=-1, keepdims=True) + L2_EPS)
                    parts.append(scale * r * (dn - a * (r * r) * jnp.sum(dn * a, axis=-1, keepdims=True)))
                d_act = jnp.concatenate(parts, axis=-1)
            dpre_ref[...] = d_act * dact
            _tap_corr(dcw_ref.at[:, col * D:(col + 1) * D], tt, dpre_ref, ext_ref, sh_ref, _G_FWD_OFFS, shifts=_G_FWD_SHIFTS)
            _shift_copies(dpre_ref, sh_ref, tt, _G_BWD_SHIFTS)
            _tap_conv(draw_ref, tt, dpre_ref, sh_ref, w_ref, _G_BWD_OFFS, shifts=_G_BWD_SHIFTS)
            dp_ref[:, col * D:(col + 1) * D] = draw_ref[...].astype(bf16)

        one(q_ref, qp_ref, qx_ref, dq_ref, dqx_ref, wq_ref, 0, HD ** -0.5)
        one(k_ref, kp_ref, kx_ref, dk_ref, dkx_ref, wk_ref, 1, 1.0)
        one(v_ref, vp_ref, vx_ref, dv_ref, dvx_ref, wv_ref, 2, None)

        is_b, is_a, beta, g, sg, neg_a = _gate_math(ba_ref[...], al_ref, dt_ref)
        dgt_v = dgt_ref[...]
        dg = _dot_hi(_chunk_tri(tt, lower=False), jnp.where(is_a, dgt_v, 0.0))
        d_al = jnp.where(is_a, dg * neg_a * sg, 0.0)
        dba_ref[...] = jnp.where(is_b, dgt_v * beta * (1.0 - beta), d_al)
        dad_ref[0:1, :] += jnp.sum(jnp.where(is_a, dg * g, 0.0), axis=0, keepdims=True)
        dad_ref[1:2, :] += jnp.sum(d_al, axis=0, keepdims=True)

    def main(col):
        return pl.BlockSpec((tt, D), lambda i: (i, col))

    def prev(col):
        return pl.BlockSpec((HALO_G, D), lambda i: (jnp.maximum(i * hb - 1, 0), col))

    def nxt(col):
        return pl.BlockSpec((HALO_G, D), lambda i: (jnp.minimum((i + 1) * hb, last_hb), col))

    def wspec(col):
        return pl.BlockSpec((8, D), lambda i: (0, col))

    vec = pl.BlockSpec((1, HD), lambda i: (0, 0))
    gblk = pl.BlockSpec((tt, HD), lambda i: (i, 0))
    return pl.pallas_call(
        body, name="gdn_pre_bwd", grid=(n_tiles,),
        in_specs=[main(0), main(1), main(2), prev(0), prev(1), prev(2), nxt(0), nxt(1), nxt(2),
                  main(0), main(0), main(0), nxt(0), nxt(0), nxt(0), gblk, gblk,
                  wspec(0), wspec(1), wspec(2), vec, vec],
        out_specs=[pl.BlockSpec((tt, 3 * D), lambda i: (i, 0)), pl.BlockSpec((tt, HD), lambda i: (i, 0)),
                   pl.BlockSpec((8, 3 * D), lambda i: (0, 0)), pl.BlockSpec((8, HD), lambda i: (0, 0))],
        out_shape=[jax.ShapeDtypeStruct((t, 3 * D), bf16), jax.ShapeDtypeStruct((t, HD), f32),
                   jax.ShapeDtypeStruct((8, 3 * D), f32), jax.ShapeDtypeStruct((8, HD), f32)],
        scratch_shapes=[pltpu.VMEM((HALO_G + tt + HALO_G, D), f32), pltpu.VMEM((KG - 1, ne, D), f32),
                        pltpu.VMEM((ne, D), f32), pltpu.VMEM((ne, D), f32), pltpu.VMEM((tt, D), f32)],
        compiler_params=_cparams("arbitrary"),
    )(p_qkv, p_qkv, p_qkv, p_qkv, p_qkv, p_qkv, p_qkv, p_qkv, p_qkv,
      dqn, dkn, dva, dqn, dkn, dva, p_ba, dgt, cw, cw, cw, alog_v, dt_v)


def _dot_b(a, b, dims):
    return lax.dot_general(a.astype(bf16), b.astype(bf16), (dims, ((), ())), preferred_element_type=f32)


def _inverse_by_doubling(ms):
    heads = range(len(ms))
    r = lax.broadcasted_iota(jnp.int32, (CH, CH), 0)
    c = lax.broadcasted_iota(jnp.int32, (CH, CH), 1)
    eye = jnp.where(r == c, 1.0, 0.0).astype(f32)
    p = [eye + ms[h] for h in heads]
    mp = ms
    for _ in range(5):
        mp = [_dot_b(mp[h], mp[h], _NN) for h in heads]
        pm = [_dot_b(p[h], mp[h], _NN) for h in heads]
        p = [p[h] + pm[h] for h in heads]
    return tuple(p)


@jax.custom_vjp
def _known_inverse(ms, ps):
    return ps


def _known_inverse_fwd(ms, ps):
    return ps, ps


def _known_inverse_bwd(ps, cts):
    heads = range(len(ps))
    left = [_dot_b(ps[h], cts[h], _TN) for h in heads]
    return tuple(_dot_b(left[h], ps[h], _NT) for h in heads), tuple(jnp.zeros_like(p) for p in ps)


_known_inverse.defvjp(_known_inverse_fwd, _known_inverse_bwd)


def _chunk_prepare(qs, ks, vs, gcs, bbs, ps=None):
    heads = range(len(qs))
    r = lax.broadcasted_iota(jnp.int32, (CH, CH), 0)
    c = lax.broadcasted_iota(jnp.int32, (CH, CH), 1)
    causal = r >= c
    strict = r > c
    gc_row = [gcs[h].T[:CH, :] for h in heads]
    decay = [jnp.where(causal, jnp.exp(jnp.where(causal, gcs[h][:, :CH] - gc_row[h], 0.0)), 0.0) for h in heads]
    kb = [ks[h] * bbs[h] for h in heads]
    egc = [jnp.exp(gcs[h]) for h in heads]
    kk = [_dot_b(kb[h], ks[h], _NT) for h in heads]
    qk = [_dot_b(qs[h], ks[h], _NT) for h in heads]
    m = tuple(-jnp.where(strict, kk[h] * decay[h], 0.0) for h in heads)
    p = _inverse_by_doubling(m) if ps is None else _known_inverse(m, ps)
    u = [_dot_b(p[h], vs[h] * bbs[h], _NN) for h in heads]
    w = [_dot_b(p[h], kb[h] * egc[h], _NN) for h in heads]
    intra = [jnp.where(causal, qk[h] * decay[h], 0.0) for h in heads]
    g_last = [gcs[h][CH - 1:CH, :] for h in heads]
    k_dec = [ks[h] * jnp.exp(g_last[h] - gcs[h]) for h in heads]
    q_dec = [qs[h] * egc[h] for h in heads]
    e_last = [jnp.exp(g_last[h]) for h in heads]
    return u, w, intra, q_dec, k_dec, e_last, p


def _chunk_apply(u, w, intra, q_dec, k_dec, e_last, ss):
    heads = range(len(ss))
    ws = [_dot_b(w[h], ss[h], _NN) for h in heads]
    qs_s = [_dot_b(q_dec[h], ss[h], _NN) for h in heads]
    v_new = [u[h] - ws[h] for h in heads]
    iv = [_dot_b(intra[h], v_new[h], _NN) for h in heads]
    kv = [_dot_b(k_dec[h], v_new[h], _TN) for h in heads]
    o = tuple(qs_s[h] + iv[h] for h in heads)
    s_new = tuple(ss[h] * e_last[h] + kv[h] for h in heads)
    return o, s_new


def _chunk_group_fn(ins, ss, ps=None):
    n = len(ss)
    prep = _chunk_prepare(*(sum((tuple(c[i]) for c in ins), ()) for i in range(5)), ps=ps)
    outs, befores = [], []
    for g in range(len(ins)):
        befores.append(ss)
        o, ss = _chunk_apply(*(x[g * n:(g + 1) * n] for x in prep[:6]), ss)
        outs.append(o)
    return tuple(outs), tuple(befores), ss, prep[6]


def _head_cols():
    return [slice(h * HD, (h + 1) * HD) for h in range(NH)]


def _head_gates(gt):
    gcs = tuple(jnp.broadcast_to(gt[:, NH + h:NH + h + 1], (CH, HD)) for h in range(NH))
    bbs = tuple(jnp.broadcast_to(gt[:, h:h + 1], (CH, HD)) for h in range(NH))
    return gcs, bbs


def _gdn_scan_fwd(qn, kn, va, gates, *, tt):
    t = qn.shape[0]
    cpb = tt // CH
    group = min(SCAN_GROUP, cpb)

    def body(q_ref, k_ref, v_ref, gt_ref, o_ref, st_ref, p_ref, s_scr):
        @pl.when(pl.program_id(0) == 0)
        def _():
            s_scr[...] = jnp.zeros_like(s_scr)

        cols = _head_cols()

        def inputs(ci):
            rows = pl.ds(pl.multiple_of(ci * CH, CH), CH)
            gcs, bbs = _head_gates(gt_ref[rows, :])
            return tuple(tuple(ref[rows, cl] for cl in cols) for ref in (q_ref, k_ref, v_ref)) + (gcs, bbs)

        def step(gi, carry):
            chunks = [group * gi + g for g in range(group)]
            outs, befores, s_end, p = _chunk_group_fn([inputs(ci) for ci in chunks], tuple(s_scr[h] for h in range(NH)))
            for g, ci in enumerate(chunks):
                rows = pl.ds(pl.multiple_of(ci * CH, CH), CH)
                for h in range(NH):
                    st_ref[ci, h] = befores[g][h]
                    o_ref[rows, cols[h]] = outs[g][h]
                    p_ref[ci, h] = p[g * NH + h].astype(bf16)
            for h in range(NH):
                s_scr[h] = s_end[h]
            return carry

        lax.fori_loop(0, cpb // group, step, 0)

    blk = pl.BlockSpec((tt, D), lambda i: (i, 0))
    return pl.pallas_call(
        body, name="gdn_scan_fwd", grid=(t // tt,),
        in_specs=[blk] * 3 + [pl.BlockSpec((tt, HD), lambda i: (i, 0))],
        out_specs=[blk, pl.BlockSpec((cpb, NH, HD, HD), lambda i: (i, 0, 0, 0)),
                   pl.BlockSpec((cpb, NH, CH, CH), lambda i: (i, 0, 0, 0))],
        out_shape=[jax.ShapeDtypeStruct((t, D), f32), jax.ShapeDtypeStruct((t // CH, NH, HD, HD), f32),
                   jax.ShapeDtypeStruct((t // CH, NH, CH, CH), bf16)],
        scratch_shapes=[pltpu.VMEM((NH, HD, HD), f32)],
        compiler_params=_cparams("arbitrary"),
    )(qn, kn, va, gates)


def _gdn_scan_bwd(qn, kn, va, gates, states, inverses, do, *, tt):
    t = qn.shape[0]
    nblk = t // tt
    cpb = tt // CH

    def body(q_ref, k_ref, v_ref, gt_ref, st_ref, p_ref, do_ref, dq_ref, dk_ref, dv_ref, dgt_ref, ds_scr):
        @pl.when(pl.program_id(0) == 0)
        def _():
            ds_scr[...] = jnp.zeros_like(ds_scr)

        cols = _head_cols()

        def rows_of(ci):
            return pl.ds(pl.multiple_of(ci * CH, CH), CH)

        def inputs(ci):
            gcs, bbs = _head_gates(gt_ref[rows_of(ci), :])
            return tuple(tuple(ref[rows_of(ci), cl] for cl in cols) for ref in (q_ref, k_ref, v_ref)) + (gcs, bbs)

        def step(j, carry):
            ci = cpb - 1 - j
            ps = tuple(p_ref[ci, h].astype(f32) for h in range(NH))

            def one(ins, ss):
                outs, _, s_end, _ = _chunk_group_fn([ins], ss, ps=ps)
                return outs[0], s_end

            _, vjp = jax.vjp(one, inputs(ci), tuple(st_ref[ci, h] for h in range(NH)))
            grads, ds = vjp((tuple(do_ref[rows_of(ci), cl] for cl in cols), tuple(ds_scr[h] for h in range(NH))))
            lane = lax.broadcasted_iota(jnp.int32, (CH, HD), 1)
            dgt = jnp.zeros((CH, HD), f32)
            for h in range(NH):
                for ref, g in zip((dq_ref, dk_ref, dv_ref), grads[:3]):
                    ref[rows_of(ci), cols[h]] = g[h]
                dgt = dgt + jnp.where(lane == NH + h, jnp.sum(grads[3][h], axis=-1, keepdims=True), 0.0)
                dgt = dgt + jnp.where(lane == h, jnp.sum(grads[4][h], axis=-1, keepdims=True), 0.0)
                ds_scr[h] = ds[h]
            dgt_ref[rows_of(ci), :] = dgt
            return carry

        lax.fori_loop(0, cpb, step, 0)

    blk = pl.BlockSpec((tt, D), lambda i: (nblk - 1 - i, 0))
    sblk = pl.BlockSpec((cpb, NH, HD, HD), lambda i: (nblk - 1 - i, 0, 0, 0))
    sds = jax.ShapeDtypeStruct((t, D), f32)
    gblk = pl.BlockSpec((tt, HD), lambda i: (nblk - 1 - i, 0))
    pblk = pl.BlockSpec((cpb, NH, CH, CH), lambda i: (nblk - 1 - i, 0, 0, 0))
    return pl.pallas_call(
        body, name="gdn_scan_bwd", grid=(nblk,),
        in_specs=[blk] * 3 + [gblk, sblk, pblk, blk],
        out_specs=[blk] * 3 + [gblk], out_shape=[sds] * 3 + [jax.ShapeDtypeStruct((t, HD), f32)],
        scratch_shapes=[pltpu.VMEM((NH, HD, HD), f32)],
        compiler_params=_cparams("arbitrary"),
    )(qn, kn, va, gates, states, inverses, do)


def _rms_heads(o):
    ons, rs = [], []
    for h in range(NH):
        a = o[:, h * HD:(h + 1) * HD]
        r = lax.rsqrt(jnp.mean(a * a, axis=-1, keepdims=True) + RMS_EPS)
        ons.append(a * r)
        rs.append(jnp.broadcast_to(r, a.shape))
    return jnp.concatenate(ons, axis=-1), jnp.concatenate(rs, axis=-1)


def _gdn_post_fwd(o, p_gz, ng_b, w_out, *, tt):
    t = o.shape[0]

    def body(o_ref, gz_ref, ng_ref, w_ref, og_ref, y_ref):
        on, _ = _rms_heads(o_ref[...])
        z, _ = _silu_and_grad(gz_ref[...])
        og = (on * ng_ref[...] * z).astype(bf16)
        og_ref[...] = og
        y_ref[...] = jnp.dot(og, w_ref[...], preferred_element_type=f32)

    blk = pl.BlockSpec((tt, D), lambda i: (i, 0))
    return pl.pallas_call(
        body, name="gdn_post_fwd", grid=(t // tt,),
        in_specs=[blk, blk, pl.BlockSpec((1, D), lambda i: (0, 0)), pl.BlockSpec((D, D), lambda i: (0, 0))],
        out_specs=[blk, blk], out_shape=[jax.ShapeDtypeStruct((t, D), bf16), jax.ShapeDtypeStruct((t, D), f32)],
        compiler_params=_cparams("parallel"),
    )(o, p_gz, ng_b, w_out)


def _gdn_post_bwd(o, p_gz, ng_b, w_out, dyg, *, tt):
    t = o.shape[0]

    def body(o_ref, gz_ref, ng_ref, w_ref, dyg_ref, do_ref, dgz_ref, dng_ref):
        @pl.when(pl.program_id(0) == 0)
        def _():
            dng_ref[...] = jnp.zeros_like(dng_ref)

        on, r = _rms_heads(o_ref[...])
        z, dz = _silu_and_grad(gz_ref[...])
        dog_v = lax.dot_general(dyg_ref[...], w_ref[...], (_NT, ((), ())), preferred_element_type=f32)
        ng = ng_ref[...]
        dgz_ref[...] = (dog_v * on * ng * dz).astype(bf16)
        dy = dog_v * z
        dng_all = jnp.sum(dy * on, axis=0, keepdims=True)
        dng = dng_all[:, 0:HD]
        for h in range(1, NH):
            dng = dng + dng_all[:, h * HD:(h + 1) * HD]
        dng_ref[0:1, :] += dng
        don = dy * ng
        prod = don * on
        parts = []
        for h in range(NH):
            sl = slice(h * HD, (h + 1) * HD)
            parts.append(don[:, sl] - on[:, sl] * jnp.mean(prod[:, sl], axis=-1, keepdims=True))
        do_ref[...] = r * jnp.concatenate(parts, axis=-1)

    blk = pl.BlockSpec((tt, D), lambda i: (i, 0))
    return pl.pallas_call(
        body, name="gdn_post_bwd", grid=(t // tt,),
        in_specs=[blk, blk, pl.BlockSpec((1, D), lambda i: (0, 0)), pl.BlockSpec((D, D), lambda i: (0, 0)), blk],
        out_specs=[blk, blk, pl.BlockSpec((8, HD), lambda i: (0, 0))],
        out_shape=[jax.ShapeDtypeStruct((t, D), f32), jax.ShapeDtypeStruct((t, D), bf16),
                   jax.ShapeDtypeStruct((8, HD), f32)],
        compiler_params=_cparams("arbitrary"),
    )(o, p_gz, ng_b, w_out, dyg)


def _merge(x, y_conf, y_gdn, p_gates, target, w_o, ln_g, ln_b, *, tt):
    t = x.shape[0]

    def body(x_ref, yc_ref, yg_ref, gc_ref, gg_ref, tg_ref, w_ref, g_ref, b_ref,
             loss_ref, dxd_ref, dyc_ref, dyg_ref, dpg_ref, h_ref, dz_ref, dvec_ref):
        @pl.when(pl.program_id(0) == 0)
        def _():
            loss_ref[...] = jnp.zeros_like(loss_ref)
            dvec_ref[...] = jnp.zeros_like(dvec_ref)

        sc = _sigmoid(gc_ref[...])
        sg = _sigmoid(gg_ref[...])
        yc = yc_ref[...]
        yg = yg_ref[...]
        h = (sc * yc + sg * yg).astype(bf16)
        h_ref[...] = h
        z = DN_ALPHA * x_ref[...] + jnp.dot(h, w_ref[...], preferred_element_type=f32)
        mu = jnp.mean(z, axis=-1, keepdims=True)
        cen = z - mu
        rstd = lax.rsqrt(jnp.mean(cen * cen, axis=-1, keepdims=True) + LN_EPS)
        xhat = cen * rstd
        err = xhat * g_ref[...] + b_ref[...] - tg_ref[...]
        loss_ref[...] += 0.5 / D * jnp.sum(err * err)
        dy = err * (1.0 / D)
        dvec_ref[0:1, :] += jnp.sum(dy * xhat, axis=0, keepdims=True)
        dvec_ref[1:2, :] += jnp.sum(dy, axis=0, keepdims=True)
        dxhat = dy * g_ref[...]
        dz = rstd * (dxhat - jnp.mean(dxhat, axis=-1, keepdims=True)
                     - xhat * jnp.mean(dxhat * xhat, axis=-1, keepdims=True))
        dxd_ref[...] = DN_ALPHA * dz
        dz_b = dz.astype(bf16)
        dz_ref[...] = dz_b
        dh = lax.dot_general(dz_b, w_ref[...], (_NT, ((), ())), preferred_element_type=f32)
        dyc_ref[...] = (dh * sc).astype(bf16)
        dyg_ref[...] = (dh * sg).astype(bf16)
        dpg_ref[:, 0:D] = (dh * yc * sc * (1.0 - sc)).astype(bf16)
        dpg_ref[:, D:] = (dh * yg * sg * (1.0 - sg)).astype(bf16)

    blk = pl.BlockSpec((tt, D), lambda i: (i, 0))
    wblk = pl.BlockSpec((D, D), lambda i: (0, 0))
    vec = pl.BlockSpec((1, D), lambda i: (0, 0))
    return pl.pallas_call(
        body, name="merge_norm_loss", grid=(t // tt,),
        in_specs=[blk, blk, blk, pl.BlockSpec((tt, D), lambda i: (i, 0)), pl.BlockSpec((tt, D), lambda i: (i, 1)),
                  blk, wblk, vec, vec],
        out_specs=[pl.BlockSpec((8, HD), lambda i: (0, 0)), blk, blk, blk,
                   pl.BlockSpec((tt, 2 * D), lambda i: (i, 0)), blk, blk, pl.BlockSpec((8, D), lambda i: (0, 0))],
        out_shape=[jax.ShapeDtypeStruct((8, HD), f32), jax.ShapeDtypeStruct((t, D), f32),
                   jax.ShapeDtypeStruct((t, D), bf16), jax.ShapeDtypeStruct((t, D), bf16),
                   jax.ShapeDtypeStruct((t, 2 * D), bf16), jax.ShapeDtypeStruct((t, D), bf16),
                   jax.ShapeDtypeStruct((t, D), bf16), jax.ShapeDtypeStruct((8, D), f32)],
        compiler_params=_cparams("arbitrary"),
    )(x, y_conf, y_gdn, p_gates, p_gates, target, w_o, ln_g, ln_b)


def _place():
    return lax.axis_index("x"), lax.axis_index("y"), lax.axis_index("c")


def _any_specs(n):
    return [pl.BlockSpec(memory_space=pl.ANY)] * n


def _sibling_merge(arrs, name, take_other_half=False):
    k = len(arrs)

    def body(*refs):
        a_refs, o_refs = refs[:k], refs[k:2 * k]
        send_sems, recv_sems = refs[2 * k:]
        x, y, c = _place()
        sends = []
        for i in range(k):
            src = a_refs[i]
            if take_other_half:
                n = a_refs[i].shape[-2] // 2
                lead = (slice(None),) * (len(a_refs[i].shape) - 2)
                src = a_refs[i].at[lead + (pl.ds((1 - c) * n, n), slice(None))]
            cp = pltpu.make_async_remote_copy(src_ref=src, dst_ref=o_refs[i], send_sem=send_sems.at[i],
                                              recv_sem=recv_sems.at[i], device_id=(x, y, 1 - c), device_id_type=MESH)
            cp.start()
            sends.append(cp)
        for cp in sends:
            cp.wait()

    def out_sds(a):
        rows = a.shape[-2] // 2 if take_other_half else a.shape[-2]
        return jax.ShapeDtypeStruct(a.shape[:-2] + (rows, a.shape[-1]), a.dtype)

    return pl.pallas_call(
        body, name=name, in_specs=_any_specs(k), out_specs=_any_specs(k),
        out_shape=[out_sds(a) for a in arrs],
        scratch_shapes=[pltpu.SemaphoreType.DMA((k,)), pltpu.SemaphoreType.DMA((k,))],
    )(*arrs)


def _join_halves(mine, other):
    c = lax.axis_index("c")
    return jnp.concatenate([jnp.where(c == 0, mine, other), jnp.where(c == 0, other, mine)], axis=-2)


def _chip_exchange_ops(a_refs, o_refs, send_sems, recv_sems, local_sems, scatter):
    k = len(a_refs)
    x, y, c = _place()
    me = 2 * x + y
    peers = [(1 - x, y), (x, 1 - y), (1 - x, 1 - y)]

    def src(i, j):
        return a_refs[i].at[j] if scatter else a_refs[i]

    def copy(i, n, send_j, slot):
        px, py = peers[n]
        return pltpu.make_async_remote_copy(
            src_ref=src(i, send_j), dst_ref=o_refs[i].at[slot], send_sem=send_sems.at[3 * i + n],
            recv_sem=recv_sems.at[3 * i + n], device_id=(px, py, c), device_id_type=MESH)

    def owns():
        return [pltpu.make_async_copy(src(i, me), o_refs[i].at[me], local_sems.at[i]) for i in range(k)]

    def sends():
        return [copy(i, n, 2 * peers[n][0] + peers[n][1], me) for n in range(3) for i in range(k)]

    def start():
        for cp in owns() + sends():
            cp.start()

    def finish():
        for n in range(3):
            for i in range(k):
                copy(i, n, me, 2 * peers[n][0] + peers[n][1]).wait_recv()
        for cp in sends():
            cp.wait_send()
        for cp in owns():
            cp.wait()

    return start, finish


def _chip_exchange_shapes(arrs, scatter):
    return [jax.ShapeDtypeStruct((N_CHIPS,) + tuple(a.shape[1:] if scatter else a.shape), a.dtype) for a in arrs]


def _chip_exchange_sems(k):
    return [pltpu.SemaphoreType.DMA((3 * k,)), pltpu.SemaphoreType.DMA((3 * k,)), pltpu.SemaphoreType.DMA((k,))]


def _gather_halves(halves, wholes, name):
    kh, kw = len(halves), len(wholes)
    k = kh + kw

    def body(*refs):
        a_refs, got_refs, oth_refs = refs[:k], refs[k:2 * k], refs[2 * k:2 * k + kh]
        send_sems, recv_sems, local_sems, fwd_send_sems, fwd_recv_sems = refs[2 * k + kh:]
        x, y, c = _place()
        me = 2 * x + y
        start, _ = _chip_exchange_ops(a_refs, got_refs, send_sems, recv_sems, local_sems, False)
        slots = [me] + [2 * px + py for px, py in [(1 - x, y), (x, 1 - y), (1 - x, 1 - y)]]

        def forward(i, r):
            src = a_refs[i] if r == 0 else got_refs[i].at[slots[r]]
            return pltpu.make_async_remote_copy(
                src_ref=src, dst_ref=oth_refs[i].at[slots[r]], send_sem=fwd_send_sems.at[4 * i + r],
                recv_sem=fwd_recv_sems.at[4 * i + r], device_id=(x, y, 1 - c), device_id_type=MESH)

        def arrival(i, n):
            px, py = [(1 - x, y), (x, 1 - y), (1 - x, 1 - y)][n]
            return pltpu.make_async_remote_copy(
                src_ref=a_refs[i], dst_ref=got_refs[i].at[slots[n + 1]], send_sem=send_sems.at[3 * i + n],
                recv_sem=recv_sems.at[3 * i + n], device_id=(px, py, c), device_id_type=MESH)

        start()
        for i in range(kh):
            forward(i, 0).start()
        for n in range(3):
            for i in range(k):
                arrival(i, n).wait_recv()
                if i < kh:
                    forward(i, n + 1).start()
        for i in range(kh):
            for r in range(4):
                forward(i, r).wait()
        for n in range(3):
            for i in range(k):
                arrival(i, n).wait_send()
        for i in range(k):
            pltpu.make_async_copy(a_refs[i], got_refs[i].at[me], local_sems.at[i]).wait()

    arrs = list(halves) + list(wholes)
    shapes = _chip_exchange_shapes(arrs, False)
    out = pl.pallas_call(
        body, name=name, in_specs=_any_specs(k), out_specs=_any_specs(k + kh),
        out_shape=shapes + shapes[:kh],
        scratch_shapes=_chip_exchange_sems(k) + [pltpu.SemaphoreType.DMA((4 * kh,)), pltpu.SemaphoreType.DMA((4 * kh,))],
    )(*arrs)
    return [(out[i], out[k + i]) for i in range(kh)], out[kh:k]


def _pair_sum(g_all, got, c_arr, name, out_dtype):
    n, w = got.shape[1:]
    tile = n // 4
    n_tiles = n // tile

    def body(c_ref, a_ref, b_ref, o_ref):
        o_ref[...] = (a_ref[...] + b_ref[...]).astype(out_dtype)

    return pl.pallas_call(
        body, name=name,
        grid_spec=pltpu.PrefetchScalarGridSpec(
            num_scalar_prefetch=1, grid=(N_CHIPS, n_tiles),
            in_specs=[pl.BlockSpec((1, tile, w), lambda j, i, c_ref: (j, c_ref[0] * n_tiles + i, 0)),
                      pl.BlockSpec((1, tile, w), lambda j, i, c_ref: (j, i, 0))],
            out_specs=pl.BlockSpec((1, tile, w), lambda j, i, c_ref: (j, i, 0))),
        out_shape=jax.ShapeDtypeStruct(got.shape, out_dtype),
        compiler_params=_cparams("parallel", "parallel"),
    )(c_arr, g_all, got)


def _sum_slots(a, name):
    n, w = a.shape[1:]
    tile = n // 4

    def body(a_ref, o_ref):
        o_ref[...] = ((a_ref[0].astype(f32) + a_ref[1].astype(f32)) + a_ref[2].astype(f32)) + a_ref[3].astype(f32)

    return pl.pallas_call(
        body, name=name, grid=(n // tile,),
        in_specs=[pl.BlockSpec((N_CHIPS, tile, w), lambda i: (0, i, 0))],
        out_specs=pl.BlockSpec((tile, w), lambda i: (i, 0)),
        out_shape=jax.ShapeDtypeStruct((n, w), f32),
        compiler_params=_cparams("parallel"),
    )(a)


def _adamw(w, g, m, v, name):
    rows, width = w.shape
    by_rows = rows % 64 == 0
    c1 = 1.0 / (1.0 - ADAM_B1 ** ADAM_STEP)
    c2 = 1.0 / (1.0 - ADAM_B2 ** ADAM_STEP)

    def body(w_ref, g_ref, m_ref, v_ref, d_ref, mo_ref, vo_ref):
        g_v = g_ref[...]
        m_new = ADAM_B1 * m_ref[...] + (1.0 - ADAM_B1) * g_v
        v_new = ADAM_B2 * v_ref[...] + (1.0 - ADAM_B2) * (g_v * g_v)
        mo_ref[...] = m_new
        vo_ref[...] = v_new
        d_ref[...] = -ADAM_LR * ((m_new * c1) / (jnp.sqrt(v_new * c2) + ADAM_EPS) + ADAM_WD * w_ref[...])

    blk = pl.BlockSpec((rows // 8, width), lambda i: (i, 0)) if by_rows else pl.BlockSpec((rows, LANE), lambda i: (0, i))
    sds = jax.ShapeDtypeStruct((rows, width), f32)
    return pl.pallas_call(
        body, name=name, grid=(8 if by_rows else width // LANE,),
        in_specs=[blk] * 4, out_specs=[blk] * 3, out_shape=[sds] * 3,
        compiler_params=_cparams("parallel"),
    )(w, g, m, v)


R_DW = 3 * SQ_BLK
R_CW = R_DW + 8
R_VEC = R_CW + 8
R_SMALL = R_VEC + 8
REST_ROWS = 896


def _pack_small(conf_dw_w, gdn_conv_w, vecs, a_log, dt_bias, norm_g):
    dw = jnp.pad(conf_dw_w.reshape(-1), (0, 8 * D - KC * SQ_BLK)).reshape(8, D)
    cw = jnp.pad(gdn_conv_w.reshape(-1), (0, 5 * D)).reshape(8, D)
    vec = jnp.pad(jnp.stack(vecs), ((0, 3), (0, 0)))
    small = jnp.pad(jnp.concatenate([a_log, dt_bias, norm_g]), (0, D - 2 * NH - HD)).reshape(1, D)
    return jnp.pad(jnp.concatenate([dw, cw, vec, small], axis=0), ((0, REST_ROWS - R_SMALL - 1), (0, 0)))


def _pack_rest(conf_w_out, gdn_w_out, w_o, small):
    return jnp.concatenate([conf_w_out, gdn_w_out, w_o, small], axis=0)


def _unpack_rest(p):
    conf_dw_w = p[R_DW:R_DW + 8].reshape(-1)[:KC * SQ_BLK].reshape(KC, SQ_BLK)
    gdn_conv_w = p[R_CW:R_CW + 3].reshape(KG, 3 * SQ_BLK)
    small = p[R_SMALL]
    return dict(conf_w_out=p[0:SQ_BLK], gdn_w_out=p[SQ_BLK:2 * SQ_BLK], w_o=p[2 * SQ_BLK:R_DW],
                conf_dw_w=conf_dw_w, gdn_conv_w=gdn_conv_w, conf_dw_b=p[R_VEC], conf_ln_g=p[R_VEC + 1],
                conf_ln_b=p[R_VEC + 2], post_ln_g=p[R_VEC + 3], post_ln_b=p[R_VEC + 4],
                gdn_A_log=small[0:NH], gdn_dt_bias=small[NH:2 * NH], gdn_norm_g=small[2 * NH:2 * NH + HD])


_WEIGHT_ORDER = ("w_in", "conf_dw_w", "conf_dw_b", "conf_ln_g", "conf_ln_b", "conf_w_out", "gdn_conv_w",
                 "gdn_A_log", "gdn_dt_bias", "gdn_norm_g", "gdn_w_out", "w_o", "post_ln_g", "post_ln_b")


def _gather_weights(w_in, conf_w_out, gdn_w_out, w_o, conf_dw_w, gdn_conv_w):
    c = lax.axis_index("c")
    sq = jnp.concatenate([conf_w_out, gdn_w_out, w_o], axis=0).astype(bf16)
    w_half = lax.dynamic_slice_in_dim(w_in.astype(bf16), c * (D // 2), D // 2, axis=0)
    sq_half = lax.dynamic_slice_in_dim(sq, c * (sq.shape[0] // 2), sq.shape[0] // 2, axis=0)
    small = jnp.concatenate([jnp.pad(conf_dw_w.reshape(-1), (0, 8 * D - KC * SQ_BLK)).reshape(8, D),
                             jnp.pad(gdn_conv_w.reshape(-1), (0, 5 * D)).reshape(8, D)], axis=0)
    ((w_mine, w_other), (sq_mine, sq_other)), (small_all,) = _gather_halves([w_half, sq_half], [small], "weight_gather")
    w4 = _join_halves(w_mine, w_other)
    sq4 = _join_halves(sq_mine, sq_other)
    sq_full = [sq4[:, n * SQ_BLK:(n + 1) * SQ_BLK].reshape(D, D) for n in range(3)]
    dw_full = small_all[:, 0:8].reshape(N_CHIPS, 8 * D)[:, :KC * SQ_BLK].reshape(N_CHIPS, KC, SQ_BLK)
    dw_full = dw_full.transpose(1, 0, 2).reshape(KC, D)
    cw_full = small_all[:, 8:11].reshape(N_CHIPS, KG, 3 * SQ_BLK).transpose(1, 0, 2).reshape(KG, 3 * D)
    return w4, sq_full[0], sq_full[1], sq_full[2], dw_full, cw_full


def _w_in_cols(w4, lo, hi):
    parts = []
    for j in range(N_CHIPS):
        a, b = max(lo, j * W_IN_BLK), min(hi, (j + 1) * W_IN_BLK)
        if a < b:
            parts.append(w4[j, :, a - j * W_IN_BLK:b - j * W_IN_BLK])
    return parts[0] if len(parts) == 1 else jnp.concatenate(parts, axis=1)


def _w_in_by_chip(pieces):
    chips = []
    for j in range(N_CHIPS):
        lo, hi = j * W_IN_BLK, (j + 1) * W_IN_BLK
        parts = []
        for start, arr in pieces:
            a, b = max(lo, start), min(hi, start + arr.shape[1])
            if a < b:
                parts.append(arr[:, a - start:b - start])
        chips.append(jnp.concatenate(parts, axis=1))
    return jnp.stack(chips)


def _pair_sums(g_w, g_rest):
    c_arr = lax.axis_index("c").astype(jnp.int32).reshape(1)
    got_w, got_r = _sibling_merge([g_w, g_rest], "grad_sibling_halves", take_other_half=True)
    pair_w = _pair_sum(g_w, got_w, c_arr, "grad_pair_sum_w_in", bf16)
    pair_r = _pair_sum(g_rest, got_r, c_arr, "grad_pair_sum_rest", f32)
    return pair_w, pair_r


def _chip_sums(all_w, all_r):
    tot_w, tot_r = _sum_slots(all_w, "grad_chip_sum_w_in"), _sum_slots(all_r, "grad_chip_sum_rest")
    oth_w, oth_r = _sibling_merge([tot_w, tot_r], "grad_sibling_result")
    return _join_halves(tot_w, oth_w), _join_halves(tot_r, oth_r)


def _local_step(x2, tgt, w4, wc_out, wg_out, wo_full, dw_full, cw_full, conf_dw_b, conf_ln_g, conf_ln_b,
                gdn_A_log, gdn_dt_bias, gdn_norm_g, post_ln_g, post_ln_b):
    t = x2.shape[0]
    tt = min(TOKEN_TILE, t)
    tm = min(MM_TILE, t)

    w_conv, w_qkv, w_gz = _w_in_cols(w4, 0, 3 * D), _w_in_cols(w4, 3 * D, 6 * D), _w_in_cols(w4, 6 * D, 7 * D)
    w_gates = _w_in_cols(w4, 7 * D + 2 * NH, W_IN_COLS)
    dw_pad = jnp.pad(dw_full, ((0, HALO_C - KC), (0, 0)))
    cw_pad = jnp.pad(cw_full, ((0, 8 - KG), (0, 0)))
    row = lambda v: v.reshape(1, D)
    alog_v = jnp.pad(gdn_A_log, (NH, HD - 2 * NH)).reshape(1, HD)
    dt_v = jnp.pad(gdn_dt_bias, (NH, HD - 2 * NH)).reshape(1, HD)
    ng_b = row(jnp.tile(gdn_norm_g, NH))
    w_ba = jnp.pad(_w_in_cols(w4, 7 * D, 7 * D + 2 * NH), ((0, 0), (0, HD - 2 * NH)))

    x_b = x2.astype(bf16)

    p_conv = _mm_multi([x_b], [w_conv], out_dtype=f32, tm=tm, tn=MM_TILE, name="proj_conv")
    p_qkv = _mm_multi([x_b], [w_qkv], out_dtype=f32, tm=tm, tn=MM_TILE, name="proj_qkv")
    p_gz = _mm_multi([x_b], [w_gz], out_dtype=f32, tm=tm, tn=MM_TILE, name="proj_gz")
    p_gates = _mm_multi([x_b], [w_gates], out_dtype=f32, tm=tm, tn=MM_TILE, name="proj_gates")
    p_ba = _mm_multi([x_b], [w_ba], out_dtype=f32, tm=tm, tn=HD, name="proj_ba")

    u, a1, y_conf = _conv_fwd(p_conv, dw_pad, row(conf_dw_b), row(conf_ln_g), row(conf_ln_b), wc_out, tt=tt)

    qn, kn, va, gates = _gdn_pre_fwd(p_qkv, p_ba, cw_pad, alog_v, dt_v, tt=tt)
    o, states, inverses = _gdn_scan_fwd(qn, kn, va, gates, tt=tt)
    og, y_gdn = _gdn_post_fwd(o, p_gz, ng_b, wg_out, tt=tt)

    loss_blk, dxd, dyc, dyg, dp_gates, h, dz, dpost = _merge(
        x2, y_conf, y_gdn, p_gates, tgt, wo_full, row(post_ln_g), row(post_ln_b), tt=tt)

    d_wo = _mm_kloop(h, dz, tm=D, tn=MM_TILE, tk=min(MM_K_TILE, t), name="grad_w_o")
    du = _mm_multi([dyc], [wc_out], out_dtype=f32, tm=tm, tn=MM_TILE, name="conf_out_bwd", rhs_t=True)
    d_wc = _mm_kloop(u, dyc, tm=D, tn=MM_TILE, tk=min(MM_K_TILE, t), name="grad_conf_w_out")
    d_wg = _mm_kloop(og, dyg, tm=D, tn=MM_TILE, tk=min(MM_K_TILE, t), name="grad_gdn_w_out")

    dp_conv, d_dww, dconv_vec = _conv_bwd(p_conv, a1, du, dw_pad, row(conf_ln_g), row(conf_ln_b), tt=tt)

    do, dp_gz, dng = _gdn_post_bwd(o, p_gz, ng_b, wg_out, dyg, tt=tt)
    dqn, dkn, dva, dgates = _gdn_scan_bwd(qn, kn, va, gates, states, inverses, do, tt=tt)
    dp_qkv, dp_ba, d_cw, d_ad = _gdn_pre_bwd(p_qkv, p_ba, cw_pad, alog_v, dt_v, dqn, dkn, dva, dgates, tt=tt)
    dp_ba_b = dp_ba.astype(bf16)

    grad_x_factors = ([dp_conv, dp_qkv, dp_gz, dp_gates, dp_ba_b], [w_conv, w_qkv, w_gz, w_gates, w_ba], dxd)

    tk = min(MM_K_TILE, t)
    d_w_conv = _mm_kloop(x_b, dp_conv, tm=D, tn=MM_TILE, tk=tk, name="grad_w_in_conv")
    d_w_qkv = _mm_kloop(x_b, dp_qkv, tm=D, tn=MM_TILE, tk=tk, name="grad_w_in_qkv")
    d_w_gz = _mm_kloop(x_b, dp_gz, tm=D, tn=MM_TILE, tk=tk, name="grad_w_in_gz")
    d_w_gates = _mm_kloop(x_b, dp_gates, tm=D, tn=MM_TILE, tk=tk, name="grad_w_in_gates")
    d_w_ba = _mm_kloop(x_b, dp_ba_b, tm=D, tn=HD, tk=tk, name="grad_w_in_ba")
    d_w_in = _w_in_by_chip([(0, d_w_conv), (3 * D, d_w_qkv), (6 * D, d_w_gz), (7 * D, d_w_ba[:, :2 * NH]),
                            (7 * D + 2 * NH, d_w_gates)])

    return (loss_blk[0, 0], grad_x_factors, d_w_in, d_wc, d_wg, d_wo, d_dww, d_cw, dconv_vec, dpost, d_ad, dng)


def kernel(x, w_in, conf_dw_w, conf_dw_b, conf_ln_g, conf_ln_b, conf_w_out, gdn_conv_w, gdn_A_log, gdn_dt_bias, gdn_norm_g, gdn_w_out, w_o, post_ln_g, post_ln_b, loss_target, m_w_in, m_conf_dw_w, m_conf_dw_b, m_conf_ln_g, m_conf_ln_b, m_conf_w_out, m_gdn_conv_w, m_gdn_A_log, m_gdn_dt_bias, m_gdn_norm_g, m_gdn_w_out, m_w_o, m_post_ln_g, m_post_ln_b, v_w_in, v_conf_dw_w, v_conf_dw_b, v_conf_ln_g, v_conf_ln_b, v_conf_w_out, v_gdn_conv_w, v_gdn_A_log, v_gdn_dt_bias, v_gdn_norm_g, v_gdn_w_out, v_w_o, v_post_ln_g, v_post_ln_b):
    x2 = x.reshape(x.shape[-2], D)
    tgt = loss_target.reshape(x2.shape)
    w4, wc_out, wg_out, wo_full, dw_full, cw_full = _gather_weights(
        w_in, conf_w_out, gdn_w_out, w_o, conf_dw_w, gdn_conv_w)
    (loss_part, grad_x_factors, d_w_in, d_wc, d_wg, d_wo, d_dww, d_cw, dconv_vec, dpost, d_ad, dng) = _local_step(
        x2, tgt, w4, wc_out, wg_out, wo_full, dw_full, cw_full, conf_dw_b, conf_ln_g, conf_ln_b,
        gdn_A_log, gdn_dt_bias, gdn_norm_g, post_ln_g, post_ln_b)
    loss = lax.psum(loss_part, ("x", "y", "c"))

    dww_c = d_dww[:KC].reshape(KC, N_CHIPS, SQ_BLK)
    dcw_c = d_cw[:KG].reshape(KG, N_CHIPS, 3 * SQ_BLK)
    vecs = [dconv_vec[0], dconv_vec[1], dconv_vec[2], dpost[0], dpost[1]]
    g_rest = jnp.stack([
        _pack_rest(d_wc[j * SQ_BLK:(j + 1) * SQ_BLK], d_wg[j * SQ_BLK:(j + 1) * SQ_BLK], d_wo[j * SQ_BLK:(j + 1) * SQ_BLK],
                   _pack_small(dww_c[:, j], dcw_c[:, j], vecs, d_ad[0, NH:2 * NH], d_ad[1, NH:2 * NH], dng[0]))
        for j in range(N_CHIPS)])
    pair_w, pair_r = _pair_sums(d_w_in, g_rest)
    grad_x, all_w, all_r = _mm_multi(*grad_x_factors, out_dtype=f32, tm=min(GRAD_X_TILE[0], x2.shape[0]), tn=GRAD_X_TILE[1],
                                     name="grad_x_and_chip_scatter", rhs_t=True, scatter=[pair_w, pair_r])
    g_w_in, g_rest = _chip_sums(all_w, all_r)

    def rest_of(w_c, w_g, w_oo, dw, cw, b1, g1, b2, g2, b3, a_log, dt_bias, norm_g):
        return _pack_rest(w_c, w_g, w_oo, _pack_small(dw, cw, [b1, g1, b2, g2, b3], a_log, dt_bias, norm_g))

    w_r = rest_of(conf_w_out, gdn_w_out, w_o, conf_dw_w, gdn_conv_w, conf_dw_b, conf_ln_g, conf_ln_b,
                  post_ln_g, post_ln_b, gdn_A_log, gdn_dt_bias, gdn_norm_g)
    m_r = rest_of(m_conf_w_out, m_gdn_w_out, m_w_o, m_conf_dw_w, m_gdn_conv_w, m_conf_dw_b, m_conf_ln_g, m_conf_ln_b,
                  m_post_ln_g, m_post_ln_b, m_gdn_A_log, m_gdn_dt_bias, m_gdn_norm_g)
    v_r = rest_of(v_conf_w_out, v_gdn_w_out, v_w_o, v_conf_dw_w, v_gdn_conv_w, v_conf_dw_b, v_conf_ln_g, v_conf_ln_b,
                  v_post_ln_g, v_post_ln_b, v_gdn_A_log, v_gdn_dt_bias, v_gdn_norm_g)
    upd_w_in = [u.T for u in _adamw(w_in.T, g_w_in.T, m_w_in.T, v_w_in.T, "adamw_w_in")]
    upd_rest = _adamw(w_r, g_rest, m_r, v_r, "adamw_rest")

    out = [loss, grad_x.reshape(x.shape)]
    for big, rest in zip((g_w_in,) + tuple(upd_w_in), (g_rest,) + tuple(upd_rest)):
        d = dict(_unpack_rest(rest), w_in=big)
        out += [d[n] for n in _WEIGHT_ORDER]
    return tuple(out)
```

```python
import jax
import jax.numpy as jnp
from jax import lax
from jax.experimental import pallas as pl
from jax.experimental.pallas import tpu as pltpu

f32 = jnp.float32
bf16 = jnp.bfloat16
HI = lax.Precision.HIGHEST
MESH = pl.DeviceIdType.MESH

D = 1024
NH = 8
HD = 128
CH = 64
KC = 31
KG = 4
HALO_C = 32
HALO_G = 8
LANE = 128
STRIP = 32
N_SHIFT = 7
LN_EPS = 1e-5
RMS_EPS = 1e-6
L2_EPS = 1e-6
DN_ALPHA = 2.0 ** 0.25
N_CHIPS = 4
W_IN_COLS = 9232
W_IN_BLK = W_IN_COLS // N_CHIPS
SQ_BLK = D // N_CHIPS
VMEM_LIMIT = 52 * 1024 * 1024
MM_TILE = 1024
MM_K_TILE = 512
GRAD_X_TILE = (256, 512)
TOKEN_TILE = 256
SCAN_GROUP = 4

ADAM_LR = 0.001
ADAM_B1 = 0.9
ADAM_B2 = 0.999
ADAM_EPS = 1e-08
ADAM_WD = 0.01
ADAM_STEP = 10


def _sigmoid(x):
    return 1.0 / (1.0 + jnp.exp(-x))


def _silu_and_grad(x):
    s = _sigmoid(x)
    return x * s, s * (1.0 + x * (1.0 - s))


_NN = ((1,), (0,))
_NT = ((1,), (1,))
_TN = ((0,), (0,))


def _cparams(*sem):
    return pltpu.CompilerParams(dimension_semantics=sem, vmem_limit_bytes=VMEM_LIMIT)


def _mm_multi(a_list, b_list, addend=None, *, out_dtype, tm, tn, name, rhs_t=False, scatter=()):
    n_pairs = len(a_list)
    m = a_list[0].shape[0]
    n = b_list[0].shape[0 if rhs_t else 1]
    has_add = addend is not None
    dims = (_NT if rhs_t else _NN, ((), ()))
    n_in = 2 * n_pairs + has_add
    k = len(scatter)
    grid = (n // tn, m // tm)

    def body(*refs):
        a_refs = refs[:n_pairs]
        b_refs = refs[n_pairs:2 * n_pairs]
        o_ref = refs[n_in + k]
        if k:
            start, finish = _chip_exchange_ops(refs[n_in:n_in + k], refs[n_in + k + 1:n_in + 2 * k + 1],
                                               *refs[n_in + 2 * k + 1:], True)
            step = pl.program_id(0) * grid[1] + pl.program_id(1)
            pl.when(step == 0)(start)
        acc = None
        for a_ref, b_ref in zip(a_refs, b_refs):
            p = lax.dot_general(a_ref[...].astype(bf16), b_ref[...].astype(bf16), dims, preferred_element_type=f32)
            acc = p if acc is None else acc + p
        if has_add:
            acc = acc + refs[2 * n_pairs][...]
        o_ref[...] = acc.astype(out_dtype)
        if k:
            pl.when(step == grid[0] * grid[1] - 1)(finish)

    in_specs = [pl.BlockSpec((tm, a.shape[1]), lambda j, i: (i, 0)) for a in a_list]
    if rhs_t:
        in_specs += [pl.BlockSpec((tn, b.shape[1]), lambda j, i: (j, 0)) for b in b_list]
    else:
        in_specs += [pl.BlockSpec((b.shape[0], tn), lambda j, i: (0, j)) for b in b_list]
    args = list(a_list) + list(b_list)
    if has_add:
        in_specs.append(pl.BlockSpec((tm, tn), lambda j, i: (i, j)))
        args.append(addend)
    out = pl.pallas_call(
        body, name=name, grid=grid,
        in_specs=in_specs + _any_specs(k), out_specs=[pl.BlockSpec((tm, tn), lambda j, i: (i, j))] + _any_specs(k),
        out_shape=[jax.ShapeDtypeStruct((m, n), out_dtype)] + _chip_exchange_shapes(scatter, True),
        scratch_shapes=_chip_exchange_sems(k) if k else [],
        compiler_params=_cparams("arbitrary", "arbitrary") if k else _cparams("parallel", "parallel"),
    )(*args, *scatter)
    return out if k else out[0]


def _mm_kloop(a, b, *, tm, tn, tk, name):
    k, m = a.shape
    n = b.shape[1]
    nk = k // tk

    def body(a_ref, b_ref, o_ref):
        @pl.when(pl.program_id(2) == 0)
        def _():
            o_ref[...] = jnp.zeros_like(o_ref)
        o_ref[...] += lax.dot_general(a_ref[...].astype(bf16), b_ref[...].astype(bf16), (_TN, ((), ())),
                                      preferred_element_type=f32)

    return pl.pallas_call(
        body, name=name, grid=(n // tn, m // tm, nk),
        in_specs=[pl.BlockSpec((tk, tm), lambda j, i, kk: (kk, i)), pl.BlockSpec((tk, tn), lambda j, i, kk: (kk, j))],
        out_specs=pl.BlockSpec((tm, tn), lambda j, i, kk: (i, j)),
        out_shape=jax.ShapeDtypeStruct((m, n), f32),
        compiler_params=_cparams("parallel", "parallel", "arbitrary"),
    )(a, b)


def _shift_copies(src_ref, sh_ref, n, shifts=tuple(range(1, 8))):
    for i, b in enumerate(shifts):
        sh_ref[i, 0:n, :] = src_ref[pl.ds(b, n), :]


def _by_residue(offs):
    groups = {}
    for k, off in enumerate(offs):
        groups.setdefault(off % 8, []).append((k, off // 8))
    return groups


def _slab(src_ref, sh_ref, shifts, b, r0, n, lanes):
    ref = src_ref if b == 0 else sh_ref.at[shifts.index(b)]
    return ref[r0:r0 + n, lanes]


def _tap_conv(out_ref, n_rows, src_ref, sh_ref, w_ref, offs, bias_ref=None, shifts=tuple(range(1, 8))):
    groups = _by_residue(offs)
    for j in range(D // LANE):
        lanes = slice(j * LANE, (j + 1) * LANE)
        wv = [w_ref[k:k + 1, lanes] for k in range(len(offs))]
        for r0 in range(0, n_rows, STRIP):
            n = min(STRIP, n_rows - r0)
            accs = [jnp.zeros((n, LANE), f32) if bias_ref is None else jnp.broadcast_to(bias_ref[0:1, lanes], (n, LANE)),
                    jnp.zeros((n, LANE), f32)]
            m = 0
            for b, taps in groups.items():
                a_lo = min(a for _, a in taps)
                a_hi = max(a for _, a in taps)
                wide = _slab(src_ref, sh_ref, shifts, b, r0 + 8 * a_lo, 8 * (a_hi - a_lo) + n, lanes)
                for k, a in taps:
                    accs[m % 2] = accs[m % 2] + wv[k] * wide[8 * (a - a_lo):8 * (a - a_lo) + n]
                    m += 1
            out_ref[r0:r0 + n, lanes] = accs[0] + accs[1]


def _tap_corr(dw_ref, n_rows, lhs_ref, src_ref, sh_ref, offs, shifts=tuple(range(1, 8))):
    groups = _by_residue(offs)
    for j in range(D // LANE):
        lanes = slice(j * LANE, (j + 1) * LANE)
        accs = [jnp.zeros((8, LANE), f32) for _ in offs]
        for r0 in range(0, n_rows, STRIP):
            n = min(STRIP, n_rows - r0)
            d = lhs_ref[r0:r0 + n, lanes]
            for b, taps in groups.items():
                a_lo = min(a for _, a in taps)
                a_hi = max(a for _, a in taps)
                wide = _slab(src_ref, sh_ref, shifts, b, r0 + 8 * a_lo, 8 * (a_hi - a_lo) + n, lanes)
                for k, a in taps:
                    prod = d * wide[8 * (a - a_lo):8 * (a - a_lo) + n]
                    part = prod[0:8]
                    for q in range(1, n // 8):
                        part = part + prod[8 * q:8 * q + 8]
                    accs[k] = accs[k] + part
        for k in range(len(offs)):
            dw_ref[k:k + 1, lanes] += jnp.sum(accs[k], axis=0, keepdims=True)


_FWD_OFFS = [HALO_C - (KC - 1) + k for k in range(KC)]
_BWD_OFFS = [KC - 1 - k for k in range(KC)]


def _norm_act(a1, cz, g_ref, bb_ref):
    mu = jnp.mean(a1, axis=-1, keepdims=True)
    cen = a1 - mu
    var = jnp.mean(cen * cen, axis=-1, keepdims=True)
    rstd = lax.rsqrt(var + LN_EPS)
    xhat = cen * rstd
    ln = xhat * g_ref[...] + bb_ref[...]
    s, ds = _silu_and_grad(ln)
    zc, dzc = _silu_and_grad(cz)
    return xhat, rstd, s, ds, zc, dzc


def _conv_fwd(p_conv, dw_w, dw_b, ln_g, ln_b, w_out, *, tt):
    t = p_conv.shape[0]
    hb = tt // HALO_C

    def body(cv_ref, cg_ref, cz_ref, cvh_ref, cgh_ref, w_ref, b_ref, g_ref, bb_ref, wo_ref,
             u_ref, a1_ref, y_ref, ext_ref, sh_ref):
        first = pl.program_id(0) == 0
        halo = cvh_ref[...] * _sigmoid(cgh_ref[...])
        ext_ref[0:HALO_C, :] = jnp.where(first, 0.0, halo)
        ext_ref[HALO_C:, :] = cv_ref[...] * _sigmoid(cg_ref[...])
        _shift_copies(ext_ref, sh_ref, tt + HALO_C - 8)
        _tap_conv(a1_ref, tt, ext_ref, sh_ref, w_ref, _FWD_OFFS, b_ref)
        _, _, s, _, zc, _ = _norm_act(a1_ref[...], cz_ref[...], g_ref, bb_ref)
        u = (s * zc).astype(bf16)
        u_ref[...] = u
        y_ref[...] = jnp.dot(u, wo_ref[...], preferred_element_type=f32)

    def main(col):
        return pl.BlockSpec((tt, D), lambda i: (i, col))

    def prev(col):
        return pl.BlockSpec((HALO_C, D), lambda i: (jnp.maximum(i * hb - 1, 0), col))

    vec = pl.BlockSpec((1, D), lambda i: (0, 0))
    return pl.pallas_call(
        body, name="conv_fwd", grid=(t // tt,),
        in_specs=[main(0), main(1), main(2), prev(0), prev(1),
                  pl.BlockSpec((HALO_C, D), lambda i: (0, 0)), vec, vec, vec, pl.BlockSpec((D, D), lambda i: (0, 0))],
        out_specs=[pl.BlockSpec((tt, D), lambda i: (i, 0))] * 3,
        out_shape=[jax.ShapeDtypeStruct((t, D), bf16), jax.ShapeDtypeStruct((t, D), f32),
                   jax.ShapeDtypeStruct((t, D), f32)],
        scratch_shapes=[pltpu.VMEM((tt + HALO_C, D), f32), pltpu.VMEM((N_SHIFT, tt + HALO_C - 8, D), f32)],
        compiler_params=_cparams("parallel"),
    )(p_conv, p_conv, p_conv, p_conv, p_conv, dw_w, dw_b, ln_g, ln_b, w_out)


def _conv_bwd(p_conv, a1, du, dw_w, ln_g, ln_b, *, tt):
    t = p_conv.shape[0]
    hb = tt // HALO_C
    n_tiles = t // tt
    last_hb = t // HALO_C - 1
    ne = tt + HALO_C

    def body(cv_ref, cg_ref, cz_ref, a1_ref, du_ref, cvp_ref, cgp_ref, czn_ref, a1n_ref, dun_ref,
             w_ref, g_ref, bb_ref, dp_ref, dww_ref, dvec_ref, ext_ref, sh_ref, da1_ref, da0_ref):
        i = pl.program_id(0)
        first = i == 0
        last = i == n_tiles - 1

        @pl.when(first)
        def _():
            dww_ref[...] = jnp.zeros_like(dww_ref)
            dvec_ref[...] = jnp.zeros_like(dvec_ref)

        sig = _sigmoid(cg_ref[...])
        ext_ref[0:HALO_C, :] = jnp.where(first, 0.0, cvp_ref[...] * _sigmoid(cgp_ref[...]))
        ext_ref[HALO_C:, :] = cv_ref[...] * sig
        a1_all = jnp.concatenate([a1_ref[...], a1n_ref[...]], axis=0)
        cz = jnp.concatenate([cz_ref[...], czn_ref[...]], axis=0)
        du_all = jnp.concatenate([du_ref[...], jnp.where(last, 0.0, dun_ref[...])], axis=0)
        xhat, rstd, s, ds, zc, dzc = _norm_act(a1_all, cz, g_ref, bb_ref)
        dln = du_all * zc * ds
        dxhat = dln * g_ref[...]
        da1 = rstd * (dxhat - jnp.mean(dxhat, axis=-1, keepdims=True)
                      - xhat * jnp.mean(dxhat * xhat, axis=-1, keepdims=True))
        da1_ref[...] = da1
        dcz = (du_all * s * dzc)[:tt]
        dvec_ref[0:1, :] += jnp.sum(da1[:tt], axis=0, keepdims=True)
        dvec_ref[1:2, :] += jnp.sum((dln * xhat)[:tt], axis=0, keepdims=True)
        dvec_ref[2:3, :] += jnp.sum(dln[:tt], axis=0, keepdims=True)
        _shift_copies(ext_ref, sh_ref, ne - 8)
        _tap_corr(dww_ref, tt, da1_ref, ext_ref, sh_ref, _FWD_OFFS)
        _shift_copies(da1_ref, sh_ref, ne - 8)
        _tap_conv(da0_ref, tt, da1_ref, sh_ref, w_ref, _BWD_OFFS)
        da0 = da0_ref[...]
        cv = cv_ref[...]
        dp_ref[:, 0:D] = (da0 * sig).astype(bf16)
        dp_ref[:, D:2 * D] = (da0 * cv * sig * (1.0 - sig)).astype(bf16)
        dp_ref[:, 2 * D:] = dcz.astype(bf16)

    def main(col):
        return pl.BlockSpec((tt, D), lambda i: (i, col))

    def prev(col):
        return pl.BlockSpec((HALO_C, D), lambda i: (jnp.maximum(i * hb - 1, 0), col))

    def nxt(col):
        return pl.BlockSpec((HALO_C, D), lambda i: (jnp.minimum((i + 1) * hb, last_hb), col))

    vec = pl.BlockSpec((1, D), lambda i: (0, 0))
    return pl.pallas_call(
        body, name="conv_bwd", grid=(n_tiles,),
        in_specs=[main(0), main(1), main(2), main(0), main(0), prev(0), prev(1), nxt(2), nxt(0), nxt(0),
                  pl.BlockSpec((HALO_C, D), lambda i: (0, 0)), vec, vec],
        out_specs=[pl.BlockSpec((tt, 3 * D), lambda i: (i, 0)),
                   pl.BlockSpec((HALO_C, D), lambda i: (0, 0)),
                   pl.BlockSpec((8, D), lambda i: (0, 0))],
        out_shape=[jax.ShapeDtypeStruct((t, 3 * D), bf16), jax.ShapeDtypeStruct((HALO_C, D), f32),
                   jax.ShapeDtypeStruct((8, D), f32)],
        scratch_shapes=[pltpu.VMEM((ne, D), f32), pltpu.VMEM((N_SHIFT, ne - 8, D), f32),
                        pltpu.VMEM((ne, D), f32), pltpu.VMEM((tt, D), f32)],
        compiler_params=_cparams("arbitrary"),
    )(p_conv, p_conv, p_conv, a1, du, p_conv, p_conv, p_conv, a1, du, dw_w, ln_g, ln_b)


def _dot_hi(a, b):
    return lax.dot_general(a, b, (((1,), (0,)), ((), ())), precision=HI, preferred_element_type=f32)


def _chunk_tri(n, lower):
    r = lax.broadcasted_iota(jnp.int32, (n, n), 0)
    c = lax.broadcasted_iota(jnp.int32, (n, n), 1)
    tri = (r >= c) if lower else (r <= c)
    return jnp.where(tri & (r // CH == c // CH), 1.0, 0.0).astype(f32)


def _softplus_and_sigmoid(x):
    e = jnp.exp(-jnp.abs(x))
    log1p = jnp.where(e < 1e-2, e * (1.0 - e * (0.5 - e * (1.0 / 3.0 - 0.25 * e))), jnp.log(1.0 + e))
    return jnp.maximum(x, 0.0) + log1p, _sigmoid(x)


_G_FWD_OFFS = [HALO_G - (KG - 1) + k for k in range(KG)]
_G_FWD_SHIFTS = (5, 6, 7)
_G_BWD_OFFS = [KG - 1 - k for k in range(KG)]
_G_BWD_SHIFTS = (1, 2, 3)


def _gdn_short_conv(pre_ref, ext_ref, sh_ref, n_rows, w_ref):
    _shift_copies(ext_ref, sh_ref, n_rows, _G_FWD_SHIFTS)
    _tap_conv(pre_ref, n_rows, ext_ref, sh_ref, w_ref, _G_FWD_OFFS, shifts=_G_FWD_SHIFTS)
    return pre_ref[...]


def _l2norm_heads(act, scale):
    outs, rs = [], []
    for h in range(NH):
        a = act[:, h * HD:(h + 1) * HD]
        r = lax.rsqrt(jnp.sum(a * a, axis=-1, keepdims=True) + L2_EPS)
        outs.append(a * (r * scale))
        rs.append(jnp.broadcast_to(r, a.shape))
    return jnp.concatenate(outs, axis=-1), jnp.concatenate(rs, axis=-1)


def _gate_math(ba, al_ref, dt_ref):
    lane = lax.broadcasted_iota(jnp.int32, ba.shape, 1)
    is_b = lane < NH
    is_a = (lane >= NH) & (lane < 2 * NH)
    sp, sg = _softplus_and_sigmoid(ba + dt_ref[...])
    neg_a = -jnp.exp(al_ref[...])
    return is_b, is_a, _sigmoid(ba), neg_a * sp, sg, neg_a


def _gdn_pre_fwd(p_qkv, p_ba, cw, alog_v, dt_v, *, tt):
    t = p_qkv.shape[0]
    hb = tt // HALO_G

    def body(q_ref, k_ref, v_ref, qh_ref, kh_ref, vh_ref, ba_ref, wq_ref, wk_ref, wv_ref, al_ref, dt_ref,
             qn_ref, kn_ref, va_ref, gt_ref, ext_ref, sh_ref, pre_ref):
        first = pl.program_id(0) == 0

        def conv_act(x_ref, xh_ref, w_ref):
            ext_ref[0:HALO_G, :] = jnp.where(first, 0.0, xh_ref[...])
            ext_ref[HALO_G:, :] = x_ref[...]
            pre = _gdn_short_conv(pre_ref, ext_ref, sh_ref, tt, w_ref)
            return pre * _sigmoid(pre)

        qn_ref[...] = _l2norm_heads(conv_act(q_ref, qh_ref, wq_ref), HD ** -0.5)[0]
        kn_ref[...] = _l2norm_heads(conv_act(k_ref, kh_ref, wk_ref), 1.0)[0]
        va_ref[...] = conv_act(v_ref, vh_ref, wv_ref)
        is_b, is_a, beta, g, _, _ = _gate_math(ba_ref[...], al_ref, dt_ref)
        gc = _dot_hi(_chunk_tri(tt, lower=True), jnp.where(is_a, g, 0.0))
        gt_ref[...] = jnp.where(is_b, beta, gc)

    def main(col):
        return pl.BlockSpec((tt, D), lambda i: (i, col))

    def prev(col):
        return pl.BlockSpec((HALO_G, D), lambda i: (jnp.maximum(i * hb - 1, 0), col))

    def wspec(col):
        return pl.BlockSpec((8, D), lambda i: (0, col))

    vec = pl.BlockSpec((1, HD), lambda i: (0, 0))
    gblk = pl.BlockSpec((tt, HD), lambda i: (i, 0))
    sds = jax.ShapeDtypeStruct((t, D), f32)
    return pl.pallas_call(
        body, name="gdn_pre_fwd", grid=(t // tt,),
        in_specs=[main(0), main(1), main(2), prev(0), prev(1), prev(2), gblk, wspec(0), wspec(1), wspec(2), vec, vec],
        out_specs=[pl.BlockSpec((tt, D), lambda i: (i, 0))] * 3 + [gblk],
        out_shape=[sds] * 3 + [jax.ShapeDtypeStruct((t, HD), f32)],
        scratch_shapes=[pltpu.VMEM((tt + HALO_G, D), f32), pltpu.VMEM((KG - 1, tt, D), f32), pltpu.VMEM((tt, D), f32)],
        compiler_params=_cparams("parallel"),
    )(p_qkv, p_qkv, p_qkv, p_qkv, p_qkv, p_qkv, p_ba, cw, cw, cw, alog_v, dt_v)


def _gdn_pre_bwd(p_qkv, p_ba, cw, alog_v, dt_v, dqn, dkn, dva, dgt, *, tt):
    t = p_qkv.shape[0]
    hb = tt // HALO_G
    n_tiles = t // tt
    last_hb = t // HALO_G - 1
    ne = tt + HALO_G

    def body(q_ref, k_ref, v_ref, qp_ref, kp_ref, vp_ref, qx_ref, kx_ref, vx_ref,
             dq_ref, dk_ref, dv_ref, dqx_ref, dkx_ref, dvx_ref, ba_ref, dgt_ref,
             wq_ref, wk_ref, wv_ref, al_ref, dt_ref,
             dp_ref, dba_ref, dcw_ref, dad_ref, ext_ref, sh_ref, pre_ref, dpre_ref, draw_ref):
        i = pl.program_id(0)
        first = i == 0
        last = i == n_tiles - 1

        @pl.when(first)
        def _():
            dcw_ref[...] = jnp.zeros_like(dcw_ref)
            dad_ref[...] = jnp.zeros_like(dad_ref)

        def one(x_ref, xp_ref, xx_ref, d_ref, dx_ref, w_ref, col, scale):
            ext_ref[0:HALO_G, :] = jnp.where(first, 0.0, xp_ref[...])
            ext_ref[HALO_G:HALO_G + tt, :] = x_ref[...]
            ext_ref[HALO_G + tt:, :] = xx_ref[...]
            pre = _gdn_short_conv(pre_ref, ext_ref, sh_ref, ne, w_ref)
            act, dact = _silu_and_grad(pre)
            d_out = jnp.concatenate([d_ref[...], jnp.where(last, 0.0, dx_ref[...])], axis=0)
            if scale is None:
                d_act = d_out
            else:
                parts = []
                for h in range(NH):
                    a = act[:, h * HD:(h + 1) * HD]
                    dn = d_out[:, h * HD:(h + 1) * HD]
                    r = lax.rsqrt(jnp.sum(a * a, axis=-1, keepdims=True) + L2_EPS)
                    parts.append(scale * r * (dn - a * (r * r) * jnp.sum(dn * a, axis=-1, keepdims=True)))
                d_act = jnp.concatenate(parts, axis=-1)
            dpre_ref[...] = d_act * dact
            _tap_corr(dcw_ref.at[:, col * D:(col + 1) * D], tt, dpre_ref, ext_ref, sh_ref, _G_FWD_OFFS, shifts=_G_FWD_SHIFTS)
            _shift_copies(dpre_ref, sh_ref, tt, _G_BWD_SHIFTS)
            _tap_conv(draw_ref, tt, dpre_ref, sh_ref, w_ref, _G_BWD_OFFS, shifts=_G_BWD_SHIFTS)
            dp_ref[:, col * D:(col + 1) * D] = draw_ref[...].astype(bf16)

        one(q_ref, qp_ref, qx_ref, dq_ref, dqx_ref, wq_ref, 0, HD ** -0.5)
        one(k_ref, kp_ref, kx_ref, dk_ref, dkx_ref, wk_ref, 1, 1.0)
        one(v_ref, vp_ref, vx_ref, dv_ref, dvx_ref, wv_ref, 2, None)

        is_b, is_a, beta, g, sg, neg_a = _gate_math(ba_ref[...], al_ref, dt_ref)
        dgt_v = dgt_ref[...]
        dg = _dot_hi(_chunk_tri(tt, lower=False), jnp.where(is_a, dgt_v, 0.0))
        d_al = jnp.where(is_a, dg * neg_a * sg, 0.0)
        dba_ref[...] = jnp.where(is_b, dgt_v * beta * (1.0 - beta), d_al)
        dad_ref[0:1, :] += jnp.sum(jnp.where(is_a, dg * g, 0.0), axis=0, keepdims=True)
        dad_ref[1:2, :] += jnp.sum(d_al, axis=0, keepdims=True)

    def main(col):
        return pl.BlockSpec((tt, D), lambda i: (i, col))

    def prev(col):
        return pl.BlockSpec((HALO_G, D), lambda i: (jnp.maximum(i * hb - 1, 0), col))

    def nxt(col):
        return pl.BlockSpec((HALO_G, D), lambda i: (jnp.minimum((i + 1) * hb, last_hb), col))

    def wspec(col):
        return pl.BlockSpec((8, D), lambda i: (0, col))

    vec = pl.BlockSpec((1, HD), lambda i: (0, 0))
    gblk = pl.BlockSpec((tt, HD), lambda i: (i, 0))
    return pl.pallas_call(
        body, name="gdn_pre_bwd", grid=(n_tiles,),
        in_specs=[main(0), main(1), main(2), prev(0), prev(1), prev(2), nxt(0), nxt(1), nxt(2),
                  main(0), main(0), main(0), nxt(0), nxt(0), nxt(0), gblk, gblk,
                  wspec(0), wspec(1), wspec(2), vec, vec],
        out_specs=[pl.BlockSpec((tt, 3 * D), lambda i: (i, 0)), pl.BlockSpec((tt, HD), lambda i: (i, 0)),
                   pl.BlockSpec((8, 3 * D), lambda i: (0, 0)), pl.BlockSpec((8, HD), lambda i: (0, 0))],
        out_shape=[jax.ShapeDtypeStruct((t, 3 * D), bf16), jax.ShapeDtypeStruct((t, HD), f32),
                   jax.ShapeDtypeStruct((8, 3 * D), f32), jax.ShapeDtypeStruct((8, HD), f32)],
        scratch_shapes=[pltpu.VMEM((HALO_G + tt + HALO_G, D), f32), pltpu.VMEM((KG - 1, ne, D), f32),
                        pltpu.VMEM((ne, D), f32), pltpu.VMEM((ne, D), f32), pltpu.VMEM((tt, D), f32)],
        compiler_params=_cparams("arbitrary"),
    )(p_qkv, p_qkv, p_qkv, p_qkv, p_qkv, p_qkv, p_qkv, p_qkv, p_qkv,
      dqn, dkn, dva, dqn, dkn, dva, p_ba, dgt, cw, cw, cw, alog_v, dt_v)


def _dot_b(a, b, dims):
    return lax.dot_general(a.astype(bf16), b.astype(bf16), (dims, ((), ())), preferred_element_type=f32)


def _inverse_by_doubling(ms):
    heads = range(len(ms))
    r = lax.broadcasted_iota(jnp.int32, (CH, CH), 0)
    c = lax.broadcasted_iota(jnp.int32, (CH, CH), 1)
    eye = jnp.where(r == c, 1.0, 0.0).astype(f32)
    p = [eye + ms[h] for h in heads]
    mp = ms
    for _ in range(5):
        mp = [_dot_b(mp[h], mp[h], _NN) for h in heads]
        pm = [_dot_b(p[h], mp[h], _NN) for h in heads]
        p = [p[h] + pm[h] for h in heads]
    return tuple(p)


@jax.custom_vjp
def _known_inverse(ms, ps):
    return ps


def _known_inverse_fwd(ms, ps):
    return ps, ps


def _known_inverse_bwd(ps, cts):
    heads = range(len(ps))
    left = [_dot_b(ps[h], cts[h], _TN) for h in heads]
    return tuple(_dot_b(left[h], ps[h], _NT) for h in heads), tuple(jnp.zeros_like(p) for p in ps)


_known_inverse.defvjp(_known_inverse_fwd, _known_inverse_bwd)


def _chunk_prepare(qs, ks, vs, gcs, bbs, ps=None):
    heads = range(len(qs))
    r = lax.broadcasted_iota(jnp.int32, (CH, CH), 0)
    c = lax.broadcasted_iota(jnp.int32, (CH, CH), 1)
    causal = r >= c
    strict = r > c
    gc_row = [gcs[h].T[:CH, :] for h in heads]
    decay = [jnp.where(causal, jnp.exp(jnp.where(causal, gcs[h][:, :CH] - gc_row[h], 0.0)), 0.0) for h in heads]
    kb = [ks[h] * bbs[h] for h in heads]
    egc = [jnp.exp(gcs[h]) for h in heads]
    kk = [_dot_b(kb[h], ks[h], _NT) for h in heads]
    qk = [_dot_b(qs[h], ks[h], _NT) for h in heads]
    m = tuple(-jnp.where(strict, kk[h] * decay[h], 0.0) for h in heads)
    p = _inverse_by_doubling(m) if ps is None else _known_inverse(m, ps)
    u = [_dot_b(p[h], vs[h] * bbs[h], _NN) for h in heads]
    w = [_dot_b(p[h], kb[h] * egc[h], _NN) for h in heads]
    intra = [jnp.where(causal, qk[h] * decay[h], 0.0) for h in heads]
    g_last = [gcs[h][CH - 1:CH, :] for h in heads]
    k_dec = [ks[h] * jnp.exp(g_last[h] - gcs[h]) for h in heads]
    q_dec = [qs[h] * egc[h] for h in heads]
    e_last = [jnp.exp(g_last[h]) for h in heads]
    return u, w, intra, q_dec, k_dec, e_last, p


def _chunk_apply(u, w, intra, q_dec, k_dec, e_last, ss):
    heads = range(len(ss))
    ws = [_dot_b(w[h], ss[h], _NN) for h in heads]
    qs_s = [_dot_b(q_dec[h], ss[h], _NN) for h in heads]
    v_new = [u[h] - ws[h] for h in heads]
    iv = [_dot_b(intra[h], v_new[h], _NN) for h in heads]
    kv = [_dot_b(k_dec[h], v_new[h], _TN) for h in heads]
    o = tuple(qs_s[h] + iv[h] for h in heads)
    s_new = tuple(ss[h] * e_last[h] + kv[h] for h in heads)
    return o, s_new


def _chunk_group_fn(ins, ss, ps=None):
    n = len(ss)
    prep = _chunk_prepare(*(sum((tuple(c[i]) for c in ins), ()) for i in range(5)), ps=ps)
    outs, befores = [], []
    for g in range(len(ins)):
        befores.append(ss)
        o, ss = _chunk_apply(*(x[g * n:(g + 1) * n] for x in prep[:6]), ss)
        outs.append(o)
    return tuple(outs), tuple(befores), ss, prep[6]


def _head_cols():
    return [slice(h * HD, (h + 1) * HD) for h in range(NH)]


def _head_gates(gt):
    gcs = tuple(jnp.broadcast_to(gt[:, NH + h:NH + h + 1], (CH, HD)) for h in range(NH))
    bbs = tuple(jnp.broadcast_to(gt[:, h:h + 1], (CH, HD)) for h in range(NH))
    return gcs, bbs


def _gdn_scan_fwd(qn, kn, va, gates, *, tt):
    t = qn.shape[0]
    cpb = tt // CH
    group = min(SCAN_GROUP, cpb)

    def body(q_ref, k_ref, v_ref, gt_ref, o_ref, st_ref, p_ref, s_scr):
        @pl.when(pl.program_id(0) == 0)
        def _():
            s_scr[...] = jnp.zeros_like(s_scr)

        cols = _head_cols()

        def inputs(ci):
            rows = pl.ds(pl.multiple_of(ci * CH, CH), CH)
            gcs, bbs = _head_gates(gt_ref[rows, :])
            return tuple(tuple(ref[rows, cl] for cl in cols) for ref in (q_ref, k_ref, v_ref)) + (gcs, bbs)

        def step(gi, carry):
            chunks = [group * gi + g for g in range(group)]
            outs, befores, s_end, p = _chunk_group_fn([inputs(ci) for ci in chunks], tuple(s_scr[h] for h in range(NH)))
            for g, ci in enumerate(chunks):
                rows = pl.ds(pl.multiple_of(ci * CH, CH), CH)
                for h in range(NH):
                    st_ref[ci, h] = befores[g][h]
                    o_ref[rows, cols[h]] = outs[g][h]
                    p_ref[ci, h] = p[g * NH + h].astype(bf16)
            for h in range(NH):
                s_scr[h] = s_end[h]
            return carry

        lax.fori_loop(0, cpb // group, step, 0)

    blk = pl.BlockSpec((tt, D), lambda i: (i, 0))
    return pl.pallas_call(
        body, name="gdn_scan_fwd", grid=(t // tt,),
        in_specs=[blk] * 3 + [pl.BlockSpec((tt, HD), lambda i: (i, 0))],
        out_specs=[blk, pl.BlockSpec((cpb, NH, HD, HD), lambda i: (i, 0, 0, 0)),
                   pl.BlockSpec((cpb, NH, CH, CH), lambda i: (i, 0, 0, 0))],
        out_shape=[jax.ShapeDtypeStruct((t, D), f32), jax.ShapeDtypeStruct((t // CH, NH, HD, HD), f32),
                   jax.ShapeDtypeStruct((t // CH, NH, CH, CH), bf16)],
        scratch_shapes=[pltpu.VMEM((NH, HD, HD), f32)],
        compiler_params=_cparams("arbitrary"),
    )(qn, kn, va, gates)


def _gdn_scan_bwd(qn, kn, va, gates, states, inverses, do, *, tt):
    t = qn.shape[0]
    nblk = t // tt
    cpb = tt // CH

    def body(q_ref, k_ref, v_ref, gt_ref, st_ref, p_ref, do_ref, dq_ref, dk_ref, dv_ref, dgt_ref, ds_scr):
        @pl.when(pl.program_id(0) == 0)
        def _():
            ds_scr[...] = jnp.zeros_like(ds_scr)

        cols = _head_cols()

        def rows_of(ci):
            return pl.ds(pl.multiple_of(ci * CH, CH), CH)

        def inputs(ci):
            gcs, bbs = _head_gates(gt_ref[rows_of(ci), :])
            return tuple(tuple(ref[rows_of(ci), cl] for cl in cols) for ref in (q_ref, k_ref, v_ref)) + (gcs, bbs)

        def step(j, carry):
            ci = cpb - 1 - j
            ps = tuple(p_ref[ci, h].astype(f32) for h in range(NH))

            def one(ins, ss):
                outs, _, s_end, _ = _chunk_group_fn([ins], ss, ps=ps)
                return outs[0], s_end

            _, vjp = jax.vjp(one, inputs(ci), tuple(st_ref[ci, h] for h in range(NH)))
            grads, ds = vjp((tuple(do_ref[rows_of(ci), cl] for cl in cols), tuple(ds_scr[h] for h in range(NH))))
            lane = lax.broadcasted_iota(jnp.int32, (CH, HD), 1)
            dgt = jnp.zeros((CH, HD), f32)
            for h in range(NH):
                for ref, g in zip((dq_ref, dk_ref, dv_ref), grads[:3]):
                    ref[rows_of(ci), cols[h]] = g[h]
                dgt = dgt + jnp.where(lane == NH + h, jnp.sum(grads[3][h], axis=-1, keepdims=True), 0.0)
                dgt = dgt + jnp.where(lane == h, jnp.sum(grads[4][h], axis=-1, keepdims=True), 0.0)
                ds_scr[h] = ds[h]
            dgt_ref[rows_of(ci), :] = dgt
            return carry

        lax.fori_loop(0, cpb, step, 0)

    blk = pl.BlockSpec((tt, D), lambda i: (nblk - 1 - i, 0))
    sblk = pl.BlockSpec((cpb, NH, HD, HD), lambda i: (nblk - 1 - i, 0, 0, 0))
    sds = jax.ShapeDtypeStruct((t, D), f32)
    gblk = pl.BlockSpec((tt, HD), lambda i: (nblk - 1 - i, 0))
    pblk = pl.BlockSpec((cpb, NH, CH, CH), lambda i: (nblk - 1 - i, 0, 0, 0))
    return pl.pallas_call(
        body, name="gdn_scan_bwd", grid=(nblk,),
        in_specs=[blk] * 3 + [gblk, sblk, pblk, blk],
        out_specs=[blk] * 3 + [gblk], out_shape=[sds] * 3 + [jax.ShapeDtypeStruct((t, HD), f32)],
        scratch_shapes=[pltpu.VMEM((NH, HD, HD), f32)],
        compiler_params=_cparams("arbitrary"),
    )(qn, kn, va, gates, states, inverses, do)


def _rms_heads(o):
    ons, rs = [], []
    for h in range(NH):
        a = o[:, h * HD:(h + 1) * HD]
        r = lax.rsqrt(jnp.mean(a * a, axis=-1, keepdims=True) + RMS_EPS)
        ons.append(a * r)
        rs.append(jnp.broadcast_to(r, a.shape))
    return jnp.concatenate(ons, axis=-1), jnp.concatenate(rs, axis=-1)


def _gdn_post_fwd(o, p_gz, ng_b, w_out, *, tt):
    t = o.shape[0]

    def body(o_ref, gz_ref, ng_ref, w_ref, og_ref, y_ref):
        on, _ = _rms_heads(o_ref[...])
        z, _ = _silu_and_grad(gz_ref[...])
        og = (on * ng_ref[...] * z).astype(bf16)
        og_ref[...] = og
        y_ref[...] = jnp.dot(og, w_ref[...], preferred_element_type=f32)

    blk = pl.BlockSpec((tt, D), lambda i: (i, 0))
    return pl.pallas_call(
        body, name="gdn_post_fwd", grid=(t // tt,),
        in_specs=[blk, blk, pl.BlockSpec((1, D), lambda i: (0, 0)), pl.BlockSpec((D, D), lambda i: (0, 0))],
        out_specs=[blk, blk], out_shape=[jax.ShapeDtypeStruct((t, D), bf16), jax.ShapeDtypeStruct((t, D), f32)],
        compiler_params=_cparams("parallel"),
    )(o, p_gz, ng_b, w_out)


def _gdn_post_bwd(o, p_gz, ng_b, w_out, dyg, *, tt):
    t = o.shape[0]

    def body(o_ref, gz_ref, ng_ref, w_ref, dyg_ref, do_ref, dgz_ref, dng_ref):
        @pl.when(pl.program_id(0) == 0)
        def _():
            dng_ref[...] = jnp.zeros_like(dng_ref)

        on, r = _rms_heads(o_ref[...])
        z, dz = _silu_and_grad(gz_ref[...])
        dog_v = lax.dot_general(dyg_ref[...], w_ref[...], (_NT, ((), ())), preferred_element_type=f32)
        ng = ng_ref[...]
        dgz_ref[...] = (dog_v * on * ng * dz).astype(bf16)
        dy = dog_v * z
        dng_all = jnp.sum(dy * on, axis=0, keepdims=True)
        dng = dng_all[:, 0:HD]
        for h in range(1, NH):
            dng = dng + dng_all[:, h * HD:(h + 1) * HD]
        dng_ref[0:1, :] += dng
        don = dy * ng
        prod = don * on
        parts = []
        for h in range(NH):
            sl = slice(h * HD, (h + 1) * HD)
            parts.append(don[:, sl] - on[:, sl] * jnp.mean(prod[:, sl], axis=-1, keepdims=True))
        do_ref[...] = r * jnp.concatenate(parts, axis=-1)

    blk = pl.BlockSpec((tt, D), lambda i: (i, 0))
    return pl.pallas_call(
        body, name="gdn_post_bwd", grid=(t // tt,),
        in_specs=[blk, blk, pl.BlockSpec((1, D), lambda i: (0, 0)), pl.BlockSpec((D, D), lambda i: (0, 0)), blk],
        out_specs=[blk, blk, pl.BlockSpec((8, HD), lambda i: (0, 0))],
        out_shape=[jax.ShapeDtypeStruct((t, D), f32), jax.ShapeDtypeStruct((t, D), bf16),
                   jax.ShapeDtypeStruct((8, HD), f32)],
        compiler_params=_cparams("arbitrary"),
    )(o, p_gz, ng_b, w_out, dyg)


def _merge(x, y_conf, y_gdn, p_gates, target, w_o, ln_g, ln_b, *, tt):
    t = x.shape[0]

    def body(x_ref, yc_ref, yg_ref, gc_ref, gg_ref, tg_ref, w_ref, g_ref, b_ref,
             loss_ref, dxd_ref, dyc_ref, dyg_ref, dpg_ref, h_ref, dz_ref, dvec_ref):
        @pl.when(pl.program_id(0) == 0)
        def _():
            loss_ref[...] = jnp.zeros_like(loss_ref)
            dvec_ref[...] = jnp.zeros_like(dvec_ref)

        sc = _sigmoid(gc_ref[...])
        sg = _sigmoid(gg_ref[...])
        yc = yc_ref[...]
        yg = yg_ref[...]
        h = (sc * yc + sg * yg).astype(bf16)
        h_ref[...] = h
        z = DN_ALPHA * x_ref[...] + jnp.dot(h, w_ref[...], preferred_element_type=f32)
        mu = jnp.mean(z, axis=-1, keepdims=True)
        cen = z - mu
        rstd = lax.rsqrt(jnp.mean(cen * cen, axis=-1, keepdims=True) + LN_EPS)
        xhat = cen * rstd
        err = xhat * g_ref[...] + b_ref[...] - tg_ref[...]
        loss_ref[...] += 0.5 / D * jnp.sum(err * err)
        dy = err * (1.0 / D)
        dvec_ref[0:1, :] += jnp.sum(dy * xhat, axis=0, keepdims=True)
        dvec_ref[1:2, :] += jnp.sum(dy, axis=0, keepdims=True)
        dxhat = dy * g_ref[...]
        dz = rstd * (dxhat - jnp.mean(dxhat, axis=-1, keepdims=True)
                     - xhat * jnp.mean(dxhat * xhat, axis=-1, keepdims=True))
        dxd_ref[...] = DN_ALPHA * dz
        dz_b = dz.astype(bf16)
        dz_ref[...] = dz_b
        dh = lax.dot_general(dz_b, w_ref[...], (_NT, ((), ())), preferred_element_type=f32)
        dyc_ref[...] = (dh * sc).astype(bf16)
        dyg_ref[...] = (dh * sg).astype(bf16)
        dpg_ref[:, 0:D] = (dh * yc * sc * (1.0 - sc)).astype(bf16)
        dpg_ref[:, D:] = (dh * yg * sg * (1.0 - sg)).astype(bf16)

    blk = pl.BlockSpec((tt, D), lambda i: (i, 0))
    wblk = pl.BlockSpec((D, D), lambda i: (0, 0))
    vec = pl.BlockSpec((1, D), lambda i: (0, 0))
    return pl.pallas_call(
        body, name="merge_norm_loss", grid=(t // tt,),
        in_specs=[blk, blk, blk, pl.BlockSpec((tt, D), lambda i: (i, 0)), pl.BlockSpec((tt, D), lambda i: (i, 1)),
                  blk, wblk, vec, vec],
        out_specs=[pl.BlockSpec((8, HD), lambda i: (0, 0)), blk, blk, blk,
                   pl.BlockSpec((tt, 2 * D), lambda i: (i, 0)), blk, blk, pl.BlockSpec((8, D), lambda i: (0, 0))],
        out_shape=[jax.ShapeDtypeStruct((8, HD), f32), jax.ShapeDtypeStruct((t, D), f32),
                   jax.ShapeDtypeStruct((t, D), bf16), jax.ShapeDtypeStruct((t, D), bf16),
                   jax.ShapeDtypeStruct((t, 2 * D), bf16), jax.ShapeDtypeStruct((t, D), bf16),
                   jax.ShapeDtypeStruct((t, D), bf16), jax.ShapeDtypeStruct((8, D), f32)],
        compiler_params=_cparams("arbitrary"),
    )(x, y_conf, y_gdn, p_gates, p_gates, target, w_o, ln_g, ln_b)


def _place():
    return lax.axis_index("x"), lax.axis_index("y"), lax.axis_index("c")


def _any_specs(n):
    return [pl.BlockSpec(memory_space=pl.ANY)] * n


def _sibling_merge(arrs, name, half_axes=None):
    k = len(arrs)

    def half_shape(a, ax):
        shape = list(a.shape)
        shape[ax] //= 2
        return tuple(shape)

    def body(*refs):
        a_refs, o_refs = refs[:k], refs[k:2 * k]
        send_sems, recv_sems = refs[2 * k:]
        x, y, c = _place()
        sends = []
        for i in range(k):
            src = a_refs[i]
            if half_axes is not None:
                n = a_refs[i].shape[half_axes[i]] // 2
                idx = [slice(None)] * len(a_refs[i].shape)
                idx[half_axes[i]] = pl.ds((1 - c) * n, n)
                src = a_refs[i].at[tuple(idx)]
            cp = pltpu.make_async_remote_copy(src_ref=src, dst_ref=o_refs[i], send_sem=send_sems.at[i],
                                              recv_sem=recv_sems.at[i], device_id=(x, y, 1 - c), device_id_type=MESH)
            cp.start()
            sends.append(cp)
        for cp in sends:
            cp.wait()

    shapes = [a.shape if half_axes is None else half_shape(a, ax) for a, ax in zip(arrs, half_axes or arrs)]
    return pl.pallas_call(
        body, name=name, in_specs=_any_specs(k), out_specs=_any_specs(k),
        out_shape=[jax.ShapeDtypeStruct(sh, a.dtype) for sh, a in zip(shapes, arrs)],
        scratch_shapes=[pltpu.SemaphoreType.DMA((k,)), pltpu.SemaphoreType.DMA((k,))],
    )(*arrs)


def _join_halves(mine, other, axis=-2):
    c = lax.axis_index("c")
    return jnp.concatenate([jnp.where(c == 0, mine, other), jnp.where(c == 0, other, mine)], axis=axis)


def _chip_exchange_ops(a_refs, o_refs, send_sems, recv_sems, local_sems, scatter):
    k = len(a_refs)
    x, y, c = _place()
    me = 2 * x + y
    peers = [(1 - x, y), (x, 1 - y), (1 - x, 1 - y)]

    def src(i, j):
        return a_refs[i].at[j] if scatter else a_refs[i]

    def copy(i, n, send_j, slot):
        px, py = peers[n]
        return pltpu.make_async_remote_copy(
            src_ref=src(i, send_j), dst_ref=o_refs[i].at[slot], send_sem=send_sems.at[3 * i + n],
            recv_sem=recv_sems.at[3 * i + n], device_id=(px, py, c), device_id_type=MESH)

    def owns():
        return [pltpu.make_async_copy(src(i, me), o_refs[i].at[me], local_sems.at[i]) for i in range(k)]

    def sends():
        return [copy(i, n, 2 * peers[n][0] + peers[n][1], me) for n in range(3) for i in range(k)]

    def start():
        for cp in owns() + sends():
            cp.start()

    def finish():
        for n in range(3):
            for i in range(k):
                copy(i, n, me, 2 * peers[n][0] + peers[n][1]).wait_recv()
        for cp in sends():
            cp.wait_send()
        for cp in owns():
            cp.wait()

    return start, finish


def _chip_exchange_shapes(arrs, scatter):
    return [jax.ShapeDtypeStruct((N_CHIPS,) + tuple(a.shape[1:] if scatter else a.shape), a.dtype) for a in arrs]


def _chip_exchange_sems(k):
    return [pltpu.SemaphoreType.DMA((3 * k,)), pltpu.SemaphoreType.DMA((3 * k,)), pltpu.SemaphoreType.DMA((k,))]


def _gather_halves(halves, wholes, name):
    kh, kw = len(halves), len(wholes)
    k = kh + kw

    def body(*refs):
        a_refs, got_refs, oth_refs = refs[:k], refs[k:2 * k], refs[2 * k:2 * k + kh]
        send_sems, recv_sems, local_sems, fwd_send_sems, fwd_recv_sems = refs[2 * k + kh:]
        x, y, c = _place()
        me = 2 * x + y
        start, _ = _chip_exchange_ops(a_refs, got_refs, send_sems, recv_sems, local_sems, False)
        slots = [me] + [2 * px + py for px, py in [(1 - x, y), (x, 1 - y), (1 - x, 1 - y)]]

        def forward(i, r):
            src = a_refs[i] if r == 0 else got_refs[i].at[slots[r]]
            return pltpu.make_async_remote_copy(
                src_ref=src, dst_ref=oth_refs[i].at[slots[r]], send_sem=fwd_send_sems.at[4 * i + r],
                recv_sem=fwd_recv_sems.at[4 * i + r], device_id=(x, y, 1 - c), device_id_type=MESH)

        def arrival(i, n):
            px, py = [(1 - x, y), (x, 1 - y), (1 - x, 1 - y)][n]
            return pltpu.make_async_remote_copy(
                src_ref=a_refs[i], dst_ref=got_refs[i].at[slots[n + 1]], send_sem=send_sems.at[3 * i + n],
                recv_sem=recv_sems.at[3 * i + n], device_id=(px, py, c), device_id_type=MESH)

        start()
        for i in range(kh):
            forward(i, 0).start()
        for n in range(3):
            for i in range(k):
                arrival(i, n).wait_recv()
                if i < kh:
                    forward(i, n + 1).start()
        for i in range(kh):
            for r in range(4):
                forward(i, r).wait()
        for n in range(3):
            for i in range(k):
                arrival(i, n).wait_send()
        for i in range(k):
            pltpu.make_async_copy(a_refs[i], got_refs[i].at[me], local_sems.at[i]).wait()

    arrs = list(halves) + list(wholes)
    shapes = _chip_exchange_shapes(arrs, False)
    out = pl.pallas_call(
        body, name=name, in_specs=_any_specs(k), out_specs=_any_specs(k + kh),
        out_shape=shapes + shapes[:kh],
        scratch_shapes=_chip_exchange_sems(k) + [pltpu.SemaphoreType.DMA((4 * kh,)), pltpu.SemaphoreType.DMA((4 * kh,))],
    )(*arrs)
    return [(out[i], out[k + i]) for i in range(kh)], out[kh:k]


def _pair_sum(g_all, got, c_arr, name, out_dtype, axis):
    rows, w = got.shape[1:]
    if axis == -2:
        blk, steps = (1, rows // 4, w), 4
        pick = lambda j, i, c_ref: (j, c_ref[0] * steps + i, 0)
        mine = lambda j, i, c_ref: (j, i, 0)
    else:
        blk, steps = (1, rows, LANE), w // LANE
        pick = lambda j, i, c_ref: (j, 0, c_ref[0] * steps + i)
        mine = lambda j, i, c_ref: (j, 0, i)

    def body(c_ref, a_ref, b_ref, o_ref):
        o_ref[...] = (a_ref[...] + b_ref[...]).astype(out_dtype)

    return pl.pallas_call(
        body, name=name,
        grid_spec=pltpu.PrefetchScalarGridSpec(
            num_scalar_prefetch=1, grid=(N_CHIPS, steps),
            in_specs=[pl.BlockSpec(blk, pick), pl.BlockSpec(blk, mine)], out_specs=pl.BlockSpec(blk, mine)),
        out_shape=jax.ShapeDtypeStruct(got.shape, out_dtype),
        compiler_params=_cparams("parallel", "parallel"),
    )(c_arr, g_all, got)


def _sum_slots(a, name):
    n, w = a.shape[1:]
    by_rows = n % 64 == 0
    in_blk = pl.BlockSpec((N_CHIPS, n // 4, w), lambda i: (0, i, 0)) if by_rows else pl.BlockSpec((N_CHIPS, n, LANE), lambda i: (0, 0, i))
    out_blk = pl.BlockSpec((n // 4, w), lambda i: (i, 0)) if by_rows else pl.BlockSpec((n, LANE), lambda i: (0, i))

    def body(a_ref, o_ref):
        o_ref[...] = ((a_ref[0].astype(f32) + a_ref[1].astype(f32)) + a_ref[2].astype(f32)) + a_ref[3].astype(f32)

    return pl.pallas_call(
        body, name=name, grid=(4 if by_rows else w // LANE,),
        in_specs=[in_blk], out_specs=out_blk,
        out_shape=jax.ShapeDtypeStruct((n, w), f32),
        compiler_params=_cparams("parallel"),
    )(a)


def _adamw(w, g, m, v, name):
    rows, width = w.shape
    by_rows = rows % 64 == 0
    c1 = 1.0 / (1.0 - ADAM_B1 ** ADAM_STEP)
    c2 = 1.0 / (1.0 - ADAM_B2 ** ADAM_STEP)

    def body(w_ref, g_ref, m_ref, v_ref, d_ref, mo_ref, vo_ref):
        g_v = g_ref[...]
        m_new = ADAM_B1 * m_ref[...] + (1.0 - ADAM_B1) * g_v
        v_new = ADAM_B2 * v_ref[...] + (1.0 - ADAM_B2) * (g_v * g_v)
        mo_ref[...] = m_new
        vo_ref[...] = v_new
        d_ref[...] = -ADAM_LR * ((m_new * c1) / (jnp.sqrt(v_new * c2) + ADAM_EPS) + ADAM_WD * w_ref[...])

    blk = pl.BlockSpec((rows // 8, width), lambda i: (i, 0)) if by_rows else pl.BlockSpec((rows, LANE), lambda i: (0, i))
    sds = jax.ShapeDtypeStruct((rows, width), f32)
    return pl.pallas_call(
        body, name=name, grid=(8 if by_rows else width // LANE,),
        in_specs=[blk] * 4, out_specs=[blk] * 3, out_shape=[sds] * 3,
        compiler_params=_cparams("parallel"),
    )(w, g, m, v)


R_DW = 3 * SQ_BLK
R_CW = R_DW + 8
R_VEC = R_CW + 8
R_SMALL = R_VEC + 8
REST_ROWS = 896


def _pack_small(conf_dw_w, gdn_conv_w, vecs, a_log, dt_bias, norm_g):
    dw = jnp.pad(conf_dw_w.reshape(-1), (0, 8 * D - KC * SQ_BLK)).reshape(8, D)
    cw = jnp.pad(gdn_conv_w.reshape(-1), (0, 5 * D)).reshape(8, D)
    vec = jnp.pad(jnp.stack(vecs), ((0, 3), (0, 0)))
    small = jnp.pad(jnp.concatenate([a_log, dt_bias, norm_g]), (0, D - 2 * NH - HD)).reshape(1, D)
    return jnp.pad(jnp.concatenate([dw, cw, vec, small], axis=0), ((0, REST_ROWS - R_SMALL - 1), (0, 0)))


def _pack_rest(conf_w_out, gdn_w_out, w_o, small):
    return jnp.concatenate([conf_w_out, gdn_w_out, w_o, small], axis=0)


def _unpack_rest(p):
    conf_dw_w = p[R_DW:R_DW + 8].reshape(-1)[:KC * SQ_BLK].reshape(KC, SQ_BLK)
    gdn_conv_w = p[R_CW:R_CW + 3].reshape(KG, 3 * SQ_BLK)
    small = p[R_SMALL]
    return dict(conf_w_out=p[0:SQ_BLK], gdn_w_out=p[SQ_BLK:2 * SQ_BLK], w_o=p[2 * SQ_BLK:R_DW],
                conf_dw_w=conf_dw_w, gdn_conv_w=gdn_conv_w, conf_dw_b=p[R_VEC], conf_ln_g=p[R_VEC + 1],
                conf_ln_b=p[R_VEC + 2], post_ln_g=p[R_VEC + 3], post_ln_b=p[R_VEC + 4],
                gdn_A_log=small[0:NH], gdn_dt_bias=small[NH:2 * NH], gdn_norm_g=small[2 * NH:2 * NH + HD])


_WEIGHT_ORDER = ("w_in", "conf_dw_w", "conf_dw_b", "conf_ln_g", "conf_ln_b", "conf_w_out", "gdn_conv_w",
                 "gdn_A_log", "gdn_dt_bias", "gdn_norm_g", "gdn_w_out", "w_o", "post_ln_g", "post_ln_b")


def _gather_weights(w_in, conf_w_out, gdn_w_out, w_o, conf_dw_w, gdn_conv_w):
    c = lax.axis_index("c")
    sq = jnp.concatenate([conf_w_out, gdn_w_out, w_o], axis=0).astype(bf16)
    w_half = lax.dynamic_slice_in_dim(w_in.T.astype(bf16), c * (D // 2), D // 2, axis=1)
    sq_half = lax.dynamic_slice_in_dim(sq, c * (sq.shape[0] // 2), sq.shape[0] // 2, axis=0)
    small = jnp.concatenate([jnp.pad(conf_dw_w.reshape(-1), (0, 8 * D - KC * SQ_BLK)).reshape(8, D),
                             jnp.pad(gdn_conv_w.reshape(-1), (0, 5 * D)).reshape(8, D)], axis=0)
    ((w_mine, w_other), (sq_mine, sq_other)), (small_all,) = _gather_halves([w_half, sq_half], [small], "weight_gather")
    w_t = _join_halves(w_mine, w_other, axis=-1).reshape(W_IN_COLS, D)
    sq4 = _join_halves(sq_mine, sq_other)
    sq_full = [sq4[:, n * SQ_BLK:(n + 1) * SQ_BLK].reshape(D, D) for n in range(3)]
    dw_full = small_all[:, 0:8].reshape(N_CHIPS, 8 * D)[:, :KC * SQ_BLK].reshape(N_CHIPS, KC, SQ_BLK)
    dw_full = dw_full.transpose(1, 0, 2).reshape(KC, D)
    cw_full = small_all[:, 8:11].reshape(N_CHIPS, KG, 3 * SQ_BLK).transpose(1, 0, 2).reshape(KG, 3 * D)
    return w_t, sq_full[0], sq_full[1], sq_full[2], dw_full, cw_full


def _pair_sums(g_w, g_rest):
    c_arr = lax.axis_index("c").astype(jnp.int32).reshape(1)
    got_w, got_r = _sibling_merge([g_w, g_rest], "grad_sibling_halves", half_axes=[-1, -2])
    pair_w = _pair_sum(g_w, got_w, c_arr, "grad_pair_sum_w_in", bf16, -1)
    pair_r = _pair_sum(g_rest, got_r, c_arr, "grad_pair_sum_rest", f32, -2)
    return pair_w, pair_r


def _chip_sums(all_w, all_r):
    tot_w, tot_r = _sum_slots(all_w, "grad_chip_sum_w_in"), _sum_slots(all_r, "grad_chip_sum_rest")
    oth_w, oth_r = _sibling_merge([tot_w, tot_r], "grad_sibling_result")
    return _join_halves(tot_w, oth_w, axis=-1), _join_halves(tot_r, oth_r)


def _local_step(x2, tgt, w_t, wc_out, wg_out, wo_full, dw_full, cw_full, conf_dw_b, conf_ln_g, conf_ln_b,
                gdn_A_log, gdn_dt_bias, gdn_norm_g, post_ln_g, post_ln_b):
    t = x2.shape[0]
    tt = min(TOKEN_TILE, t)
    tm = min(MM_TILE, t)

    w_conv, w_qkv, w_gz = w_t[0:3 * D], w_t[3 * D:6 * D], w_t[6 * D:7 * D]
    w_gates = w_t[7 * D + 2 * NH:]
    dw_pad = jnp.pad(dw_full, ((0, HALO_C - KC), (0, 0)))
    cw_pad = jnp.pad(cw_full, ((0, 8 - KG), (0, 0)))
    row = lambda v: v.reshape(1, D)
    alog_v = jnp.pad(gdn_A_log, (NH, HD - 2 * NH)).reshape(1, HD)
    dt_v = jnp.pad(gdn_dt_bias, (NH, HD - 2 * NH)).reshape(1, HD)
    ng_b = row(jnp.tile(gdn_norm_g, NH))
    w_ba = jnp.pad(w_t[7 * D:7 * D + 2 * NH], ((0, HD - 2 * NH), (0, 0)))

    x_b = x2.astype(bf16)

    p_conv = _mm_multi([x_b], [w_conv], out_dtype=f32, tm=tm, tn=MM_TILE, name="proj_conv", rhs_t=True)
    p_qkv = _mm_multi([x_b], [w_qkv], out_dtype=f32, tm=tm, tn=MM_TILE, name="proj_qkv", rhs_t=True)
    p_gz = _mm_multi([x_b], [w_gz], out_dtype=f32, tm=tm, tn=MM_TILE, name="proj_gz", rhs_t=True)
    p_gates = _mm_multi([x_b], [w_gates], out_dtype=f32, tm=tm, tn=MM_TILE, name="proj_gates", rhs_t=True)
    p_ba = _mm_multi([x_b], [w_ba], out_dtype=f32, tm=tm, tn=HD, name="proj_ba", rhs_t=True)

    u, a1, y_conf = _conv_fwd(p_conv, dw_pad, row(conf_dw_b), row(conf_ln_g), row(conf_ln_b), wc_out, tt=tt)

    qn, kn, va, gates = _gdn_pre_fwd(p_qkv, p_ba, cw_pad, alog_v, dt_v, tt=tt)
    o, states, inverses = _gdn_scan_fwd(qn, kn, va, gates, tt=tt)
    og, y_gdn = _gdn_post_fwd(o, p_gz, ng_b, wg_out, tt=tt)

    loss_blk, dxd, dyc, dyg, dp_gates, h, dz, dpost = _merge(
        x2, y_conf, y_gdn, p_gates, tgt, wo_full, row(post_ln_g), row(post_ln_b), tt=tt)

    d_wo = _mm_kloop(h, dz, tm=D, tn=MM_TILE, tk=min(MM_K_TILE, t), name="grad_w_o")
    du = _mm_multi([dyc], [wc_out], out_dtype=f32, tm=tm, tn=MM_TILE, name="conf_out_bwd", rhs_t=True)
    d_wc = _mm_kloop(u, dyc, tm=D, tn=MM_TILE, tk=min(MM_K_TILE, t), name="grad_conf_w_out")
    d_wg = _mm_kloop(og, dyg, tm=D, tn=MM_TILE, tk=min(MM_K_TILE, t), name="grad_gdn_w_out")

    dp_conv, d_dww, dconv_vec = _conv_bwd(p_conv, a1, du, dw_pad, row(conf_ln_g), row(conf_ln_b), tt=tt)

    do, dp_gz, dng = _gdn_post_bwd(o, p_gz, ng_b, wg_out, dyg, tt=tt)
    dqn, dkn, dva, dgates = _gdn_scan_bwd(qn, kn, va, gates, states, inverses, do, tt=tt)
    dp_qkv, dp_ba, d_cw, d_ad = _gdn_pre_bwd(p_qkv, p_ba, cw_pad, alog_v, dt_v, dqn, dkn, dva, dgates, tt=tt)
    dp_ba_b = dp_ba.astype(bf16)

    grad_x_factors = ([dp_conv, dp_qkv, dp_gz, dp_gates, dp_ba_b], [w_conv, w_qkv, w_gz, w_gates, w_ba], dxd)

    tk = min(MM_K_TILE, t)
    d_w_conv = _mm_kloop(dp_conv, x_b, tm=MM_TILE, tn=D, tk=tk, name="grad_w_in_conv")
    d_w_qkv = _mm_kloop(dp_qkv, x_b, tm=MM_TILE, tn=D, tk=tk, name="grad_w_in_qkv")
    d_w_gz = _mm_kloop(dp_gz, x_b, tm=MM_TILE, tn=D, tk=tk, name="grad_w_in_gz")
    d_w_gates = _mm_kloop(dp_gates, x_b, tm=MM_TILE, tn=D, tk=tk, name="grad_w_in_gates")
    d_w_ba = _mm_kloop(dp_ba_b, x_b, tm=HD, tn=D, tk=tk, name="grad_w_in_ba")
    d_w_in = jnp.concatenate([d_w_conv, d_w_qkv, d_w_gz, d_w_ba[:2 * NH], d_w_gates], axis=0).reshape(
        N_CHIPS, W_IN_BLK, D)

    return (loss_blk[0, 0], grad_x_factors, d_w_in, d_wc, d_wg, d_wo, d_dww, d_cw, dconv_vec, dpost, d_ad, dng)


def kernel(x, w_in, conf_dw_w, conf_dw_b, conf_ln_g, conf_ln_b, conf_w_out, gdn_conv_w, gdn_A_log, gdn_dt_bias, gdn_norm_g, gdn_w_out, w_o, post_ln_g, post_ln_b, loss_target, m_w_in, m_conf_dw_w, m_conf_dw_b, m_conf_ln_g, m_conf_ln_b, m_conf_w_out, m_gdn_conv_w, m_gdn_A_log, m_gdn_dt_bias, m_gdn_norm_g, m_gdn_w_out, m_w_o, m_post_ln_g, m_post_ln_b, v_w_in, v_conf_dw_w, v_conf_dw_b, v_conf_ln_g, v_conf_ln_b, v_conf_w_out, v_gdn_conv_w, v_gdn_A_log, v_gdn_dt_bias, v_gdn_norm_g, v_gdn_w_out, v_w_o, v_post_ln_g, v_post_ln_b):
    x2 = x.reshape(x.shape[-2], D)
    tgt = loss_target.reshape(x2.shape)
    w_t, wc_out, wg_out, wo_full, dw_full, cw_full = _gather_weights(
        w_in, conf_w_out, gdn_w_out, w_o, conf_dw_w, gdn_conv_w)
    (loss_part, grad_x_factors, d_w_in, d_wc, d_wg, d_wo, d_dww, d_cw, dconv_vec, dpost, d_ad, dng) = _local_step(
        x2, tgt, w_t, wc_out, wg_out, wo_full, dw_full, cw_full, conf_dw_b, conf_ln_g, conf_ln_b,
        gdn_A_log, gdn_dt_bias, gdn_norm_g, post_ln_g, post_ln_b)
    loss = lax.psum(loss_part, ("x", "y", "c"))

    dww_c = d_dww[:KC].reshape(KC, N_CHIPS, SQ_BLK)
    dcw_c = d_cw[:KG].reshape(KG, N_CHIPS, 3 * SQ_BLK)
    vecs = [dconv_vec[0], dconv_vec[1], dconv_vec[2], dpost[0], dpost[1]]
    g_rest = jnp.stack([
        _pack_rest(d_wc[j * SQ_BLK:(j + 1) * SQ_BLK], d_wg[j * SQ_BLK:(j + 1) * SQ_BLK], d_wo[j * SQ_BLK:(j + 1) * SQ_BLK],
                   _pack_small(dww_c[:, j], dcw_c[:, j], vecs, d_ad[0, NH:2 * NH], d_ad[1, NH:2 * NH], dng[0]))
        for j in range(N_CHIPS)])
    pair_w, pair_r = _pair_sums(d_w_in, g_rest)
    grad_x, all_w, all_r = _mm_multi(*grad_x_factors, out_dtype=f32, tm=min(GRAD_X_TILE[0], x2.shape[0]), tn=GRAD_X_TILE[1],
                                     name="grad_x_and_chip_scatter", scatter=[pair_w, pair_r])
    g_w_in, g_rest = _chip_sums(all_w, all_r)

    def rest_of(w_c, w_g, w_oo, dw, cw, b1, g1, b2, g2, b3, a_log, dt_bias, norm_g):
        return _pack_rest(w_c, w_g, w_oo, _pack_small(dw, cw, [b1, g1, b2, g2, b3], a_log, dt_bias, norm_g))

    w_r = rest_of(conf_w_out, gdn_w_out, w_o, conf_dw_w, gdn_conv_w, conf_dw_b, conf_ln_g, conf_ln_b,
                  post_ln_g, post_ln_b, gdn_A_log, gdn_dt_bias, gdn_norm_g)
    m_r = rest_of(m_conf_w_out, m_gdn_w_out, m_w_o, m_conf_dw_w, m_gdn_conv_w, m_conf_dw_b, m_conf_ln_g, m_conf_ln_b,
                  m_post_ln_g, m_post_ln_b, m_gdn_A_log, m_gdn_dt_bias, m_gdn_norm_g)
    v_r = rest_of(v_conf_w_out, v_gdn_w_out, v_w_o, v_conf_dw_w, v_gdn_conv_w, v_conf_dw_b, v_conf_ln_g, v_conf_ln_b,
                  v_post_ln_g, v_post_ln_b, v_gdn_A_log, v_gdn_dt_bias, v_gdn_norm_g)
    upd_w_in = _adamw(w_in.T, g_w_in, m_w_in.T, v_w_in.T, "adamw_w_in")
    upd_rest = _adamw(w_r, g_rest, m_r, v_r, "adamw_rest")

    out = [loss, grad_x.reshape(x.shape)]
    for big, rest in zip((g_w_in,) + tuple(upd_w_in), (g_rest,) + tuple(upd_rest)):
        d = dict(_unpack_rest(rest), w_in=big.T)
        out += [d[n] for n in _WEIGHT_ORDER]
    return tuple(out)
```

```python
import jax
import jax.numpy as jnp
from jax import lax
from jax.experimental import pallas as pl
from jax.experimental.pallas import tpu as pltpu

f32 = jnp.float32
bf16 = jnp.bfloat16
HI = lax.Precision.HIGHEST
MESH = pl.DeviceIdType.MESH

D = 1024
NH = 8
HD = 128
CH = 64
KC = 31
KG = 4
HALO_C = 32
HALO_G = 8
LANE = 128
STRIP = 32
N_SHIFT = 7
LN_EPS = 1e-5
RMS_EPS = 1e-6
L2_EPS = 1e-6
DN_ALPHA = 2.0 ** 0.25
N_CHIPS = 4
W_IN_COLS = 9232
W_IN_BLK = W_IN_COLS // N_CHIPS
SQ_BLK = D // N_CHIPS
VMEM_LIMIT = 52 * 1024 * 1024
MM_TILE = 1024
MM_K_TILE = 512
GRAD_X_TILE = (256, 512)
TOKEN_TILE = 256
SCAN_GROUP = 4

ADAM_LR = 0.001
ADAM_B1 = 0.9
ADAM_B2 = 0.999
ADAM_EPS = 1e-08
ADAM_WD = 0.01
ADAM_STEP = 10


def _sigmoid(x):
    return 1.0 / (1.0 + jnp.exp(-x))


def _silu_and_grad(x):
    s = _sigmoid(x)
    return x * s, s * (1.0 + x * (1.0 - s))


_NN = ((1,), (0,))
_NT = ((1,), (1,))
_TN = ((0,), (0,))


def _cparams(*sem):
    return pltpu.CompilerParams(dimension_semantics=sem, vmem_limit_bytes=VMEM_LIMIT)


def _mm_multi(a_list, b_list, addend=None, *, out_dtype, tm, tn, name, rhs_t=False, scatter=()):
    n_pairs = len(a_list)
    m = a_list[0].shape[0]
    n = b_list[0].shape[0 if rhs_t else 1]
    has_add = addend is not None
    dims = (_NT if rhs_t else _NN, ((), ()))
    n_in = 2 * n_pairs + has_add
    k = len(scatter)
    grid = (n // tn, m // tm)

    def body(*refs):
        a_refs = refs[:n_pairs]
        b_refs = refs[n_pairs:2 * n_pairs]
        o_ref = refs[n_in + k]
        if k:
            start, finish = _chip_exchange_ops(refs[n_in:n_in + k], refs[n_in + k + 1:n_in + 2 * k + 1],
                                               *refs[n_in + 2 * k + 1:], True)
            step = pl.program_id(0) * grid[1] + pl.program_id(1)
            pl.when(step == 0)(start)
        acc = None
        for a_ref, b_ref in zip(a_refs, b_refs):
            p = lax.dot_general(a_ref[...].astype(bf16), b_ref[...].astype(bf16), dims, preferred_element_type=f32)
            acc = p if acc is None else acc + p
        if has_add:
            acc = acc + refs[2 * n_pairs][...]
        o_ref[...] = acc.astype(out_dtype)
        if k:
            pl.when(step == grid[0] * grid[1] - 1)(finish)

    in_specs = [pl.BlockSpec((tm, a.shape[1]), lambda j, i: (i, 0)) for a in a_list]
    if rhs_t:
        in_specs += [pl.BlockSpec((tn, b.shape[1]), lambda j, i: (j, 0)) for b in b_list]
    else:
        in_specs += [pl.BlockSpec((b.shape[0], tn), lambda j, i: (0, j)) for b in b_list]
    args = list(a_list) + list(b_list)
    if has_add:
        in_specs.append(pl.BlockSpec((tm, tn), lambda j, i: (i, j)))
        args.append(addend)
    out = pl.pallas_call(
        body, name=name, grid=grid,
        in_specs=in_specs + _any_specs(k), out_specs=[pl.BlockSpec((tm, tn), lambda j, i: (i, j))] + _any_specs(k),
        out_shape=[jax.ShapeDtypeStruct((m, n), out_dtype)] + _chip_exchange_shapes(scatter, True),
        scratch_shapes=_chip_exchange_sems(k) if k else [],
        compiler_params=_cparams("arbitrary", "arbitrary") if k else _cparams("parallel", "parallel"),
    )(*args, *scatter)
    return out if k else out[0]


def _mm_kloop(a, b, *, tm, tn, tk, name):
    k, m = a.shape
    n = b.shape[1]
    nk = k // tk

    def body(a_ref, b_ref, o_ref):
        @pl.when(pl.program_id(2) == 0)
        def _():
            o_ref[...] = jnp.zeros_like(o_ref)
        o_ref[...] += lax.dot_general(a_ref[...].astype(bf16), b_ref[...].astype(bf16), (_TN, ((), ())),
                                      preferred_element_type=f32)

    return pl.pallas_call(
        body, name=name, grid=(n // tn, m // tm, nk),
        in_specs=[pl.BlockSpec((tk, tm), lambda j, i, kk: (kk, i)), pl.BlockSpec((tk, tn), lambda j, i, kk: (kk, j))],
        out_specs=pl.BlockSpec((tm, tn), lambda j, i, kk: (i, j)),
        out_shape=jax.ShapeDtypeStruct((m, n), f32),
        compiler_params=_cparams("parallel", "parallel", "arbitrary"),
    )(a, b)


def _shift_copies(src_ref, sh_ref, n, shifts=tuple(range(1, 8))):
    for i, b in enumerate(shifts):
        sh_ref[i, 0:n, :] = src_ref[pl.ds(b, n), :]


def _by_residue(offs):
    groups = {}
    for k, off in enumerate(offs):
        groups.setdefault(off % 8, []).append((k, off // 8))
    return groups


def _slab(src_ref, sh_ref, shifts, b, r0, n, lanes):
    ref = src_ref if b == 0 else sh_ref.at[shifts.index(b)]
    return ref[r0:r0 + n, lanes]


def _tap_conv(out_ref, n_rows, src_ref, sh_ref, w_ref, offs, bias_ref=None, shifts=tuple(range(1, 8))):
    groups = _by_residue(offs)
    for j in range(D // LANE):
        lanes = slice(j * LANE, (j + 1) * LANE)
        wv = [w_ref[k:k + 1, lanes] for k in range(len(offs))]
        for r0 in range(0, n_rows, STRIP):
            n = min(STRIP, n_rows - r0)
            accs = [jnp.zeros((n, LANE), f32) if bias_ref is None else jnp.broadcast_to(bias_ref[0:1, lanes], (n, LANE)),
                    jnp.zeros((n, LANE), f32)]
            m = 0
            for b, taps in groups.items():
                a_lo = min(a for _, a in taps)
                a_hi = max(a for _, a in taps)
                wide = _slab(src_ref, sh_ref, shifts, b, r0 + 8 * a_lo, 8 * (a_hi - a_lo) + n, lanes)
                for k, a in taps:
                    accs[m % 2] = accs[m % 2] + wv[k] * wide[8 * (a - a_lo):8 * (a - a_lo) + n]
                    m += 1
            out_ref[r0:r0 + n, lanes] = accs[0] + accs[1]


def _tap_corr(dw_ref, n_rows, lhs_ref, src_ref, sh_ref, offs, shifts=tuple(range(1, 8))):
    groups = _by_residue(offs)
    for j in range(D // LANE):
        lanes = slice(j * LANE, (j + 1) * LANE)
        accs = [jnp.zeros((8, LANE), f32) for _ in offs]
        for r0 in range(0, n_rows, STRIP):
            n = min(STRIP, n_rows - r0)
            d = lhs_ref[r0:r0 + n, lanes]
            for b, taps in groups.items():
                a_lo = min(a for _, a in taps)
                a_hi = max(a for _, a in taps)
                wide = _slab(src_ref, sh_ref, shifts, b, r0 + 8 * a_lo, 8 * (a_hi - a_lo) + n, lanes)
                for k, a in taps:
                    prod = d * wide[8 * (a - a_lo):8 * (a - a_lo) + n]
                    part = prod[0:8]
                    for q in range(1, n // 8):
                        part = part + prod[8 * q:8 * q + 8]
                    accs[k] = accs[k] + part
        for k in range(len(offs)):
            dw_ref[k:k + 1, lanes] += jnp.sum(accs[k], axis=0, keepdims=True)


_FWD_OFFS = [HALO_C - (KC - 1) + k for k in range(KC)]
_BWD_OFFS = [KC - 1 - k for k in range(KC)]


def _norm_act(a1, cz, g_ref, bb_ref):
    mu = jnp.mean(a1, axis=-1, keepdims=True)
    cen = a1 - mu
    var = jnp.mean(cen * cen, axis=-1, keepdims=True)
    rstd = lax.rsqrt(var + LN_EPS)
    xhat = cen * rstd
    ln = xhat * g_ref[...] + bb_ref[...]
    s, ds = _silu_and_grad(ln)
    zc, dzc = _silu_and_grad(cz)
    return xhat, rstd, s, ds, zc, dzc


def _conv_fwd(p_conv, dw_w, dw_b, ln_g, ln_b, w_out, *, tt):
    t = p_conv.shape[0]
    hb = tt // HALO_C

    def body(cv_ref, cg_ref, cz_ref, cvh_ref, cgh_ref, w_ref, b_ref, g_ref, bb_ref, wo_ref,
             u_ref, a1_ref, y_ref, ext_ref, sh_ref):
        first = pl.program_id(0) == 0
        halo = cvh_ref[...] * _sigmoid(cgh_ref[...])
        ext_ref[0:HALO_C, :] = jnp.where(first, 0.0, halo)
        ext_ref[HALO_C:, :] = cv_ref[...] * _sigmoid(cg_ref[...])
        _shift_copies(ext_ref, sh_ref, tt + HALO_C - 8)
        _tap_conv(a1_ref, tt, ext_ref, sh_ref, w_ref, _FWD_OFFS, b_ref)
        _, _, s, _, zc, _ = _norm_act(a1_ref[...], cz_ref[...], g_ref, bb_ref)
        u = (s * zc).astype(bf16)
        u_ref[...] = u
        y_ref[...] = jnp.dot(u, wo_ref[...], preferred_element_type=f32)

    def main(col):
        return pl.BlockSpec((tt, D), lambda i: (i, col))

    def prev(col):
        return pl.BlockSpec((HALO_C, D), lambda i: (jnp.maximum(i * hb - 1, 0), col))

    vec = pl.BlockSpec((1, D), lambda i: (0, 0))
    return pl.pallas_call(
        body, name="conv_fwd", grid=(t // tt,),
        in_specs=[main(0), main(1), main(2), prev(0), prev(1),
                  pl.BlockSpec((HALO_C, D), lambda i: (0, 0)), vec, vec, vec, pl.BlockSpec((D, D), lambda i: (0, 0))],
        out_specs=[pl.BlockSpec((tt, D), lambda i: (i, 0))] * 3,
        out_shape=[jax.ShapeDtypeStruct((t, D), bf16), jax.ShapeDtypeStruct((t, D), f32),
                   jax.ShapeDtypeStruct((t, D), f32)],
        scratch_shapes=[pltpu.VMEM((tt + HALO_C, D), f32), pltpu.VMEM((N_SHIFT, tt + HALO_C - 8, D), f32)],
        compiler_params=_cparams("parallel"),
    )(p_conv, p_conv, p_conv, p_conv, p_conv, dw_w, dw_b, ln_g, ln_b, w_out)


def _conv_bwd(p_conv, a1, du, dw_w, ln_g, ln_b, *, tt):
    t = p_conv.shape[0]
    hb = tt // HALO_C
    n_tiles = t // tt
    last_hb = t // HALO_C - 1
    ne = tt + HALO_C

    def body(cv_ref, cg_ref, cz_ref, a1_ref, du_ref, cvp_ref, cgp_ref, czn_ref, a1n_ref, dun_ref,
             w_ref, g_ref, bb_ref, dp_ref, dww_ref, dvec_ref, ext_ref, sh_ref, da1_ref, da0_ref):
        i = pl.program_id(0)
        first = i == 0
        last = i == n_tiles - 1

        @pl.when(first)
        def _():
            dww_ref[...] = jnp.zeros_like(dww_ref)
            dvec_ref[...] = jnp.zeros_like(dvec_ref)

        sig = _sigmoid(cg_ref[...])
        ext_ref[0:HALO_C, :] = jnp.where(first, 0.0, cvp_ref[...] * _sigmoid(cgp_ref[...]))
        ext_ref[HALO_C:, :] = cv_ref[...] * sig
        a1_all = jnp.concatenate([a1_ref[...], a1n_ref[...]], axis=0)
        cz = jnp.concatenate([cz_ref[...], czn_ref[...]], axis=0)
        du_all = jnp.concatenate([du_ref[...], jnp.where(last, 0.0, dun_ref[...])], axis=0)
        xhat, rstd, s, ds, zc, dzc = _norm_act(a1_all, cz, g_ref, bb_ref)
        dln = du_all * zc * ds
        dxhat = dln * g_ref[...]
        da1 = rstd * (dxhat - jnp.mean(dxhat, axis=-1, keepdims=True)
                      - xhat * jnp.mean(dxhat * xhat, axis=-1, keepdims=True))
        da1_ref[...] = da1
        dcz = (du_all * s * dzc)[:tt]
        dvec_ref[0:1, :] += jnp.sum(da1[:tt], axis=0, keepdims=True)
        dvec_ref[1:2, :] += jnp.sum((dln * xhat)[:tt], axis=0, keepdims=True)
        dvec_ref[2:3, :] += jnp.sum(dln[:tt], axis=0, keepdims=True)
        _shift_copies(ext_ref, sh_ref, ne - 8)
        _tap_corr(dww_ref, tt, da1_ref, ext_ref, sh_ref, _FWD_OFFS)
        _shift_copies(da1_ref, sh_ref, ne - 8)
        _tap_conv(da0_ref, tt, da1_ref, sh_ref, w_ref, _BWD_OFFS)
        da0 = da0_ref[...]
        cv = cv_ref[...]
        dp_ref[:, 0:D] = (da0 * sig).astype(bf16)
        dp_ref[:, D:2 * D] = (da0 * cv * sig * (1.0 - sig)).astype(bf16)
        dp_ref[:, 2 * D:] = dcz.astype(bf16)

    def main(col):
        return pl.BlockSpec((tt, D), lambda i: (i, col))

    def prev(col):
        return pl.BlockSpec((HALO_C, D), lambda i: (jnp.maximum(i * hb - 1, 0), col))

    def nxt(col):
        return pl.BlockSpec((HALO_C, D), lambda i: (jnp.minimum((i + 1) * hb, last_hb), col))

    vec = pl.BlockSpec((1, D), lambda i: (0, 0))
    return pl.pallas_call(
        body, name="conv_bwd", grid=(n_tiles,),
        in_specs=[main(0), main(1), main(2), main(0), main(0), prev(0), prev(1), nxt(2), nxt(0), nxt(0),
                  pl.BlockSpec((HALO_C, D), lambda i: (0, 0)), vec, vec],
        out_specs=[pl.BlockSpec((tt, 3 * D), lambda i: (i, 0)),
                   pl.BlockSpec((HALO_C, D), lambda i: (0, 0)),
                   pl.BlockSpec((8, D), lambda i: (0, 0))],
        out_shape=[jax.ShapeDtypeStruct((t, 3 * D), bf16), jax.ShapeDtypeStruct((HALO_C, D), f32),
                   jax.ShapeDtypeStruct((8, D), f32)],
        scratch_shapes=[pltpu.VMEM((ne, D), f32), pltpu.VMEM((N_SHIFT, ne - 8, D), f32),
                        pltpu.VMEM((ne, D), f32), pltpu.VMEM((tt, D), f32)],
        compiler_params=_cparams("arbitrary"),
    )(p_conv, p_conv, p_conv, a1, du, p_conv, p_conv, p_conv, a1, du, dw_w, ln_g, ln_b)


def _dot_hi(a, b):
    return lax.dot_general(a, b, (((1,), (0,)), ((), ())), precision=HI, preferred_element_type=f32)


def _chunk_tri(n, lower):
    r = lax.broadcasted_iota(jnp.int32, (n, n), 0)
    c = lax.broadcasted_iota(jnp.int32, (n, n), 1)
    tri = (r >= c) if lower else (r <= c)
    return jnp.where(tri & (r // CH == c // CH), 1.0, 0.0).astype(f32)


def _softplus_and_sigmoid(x):
    e = jnp.exp(-jnp.abs(x))
    log1p = jnp.where(e < 1e-2, e * (1.0 - e * (0.5 - e * (1.0 / 3.0 - 0.25 * e))), jnp.log(1.0 + e))
    return jnp.maximum(x, 0.0) + log1p, _sigmoid(x)


_G_FWD_OFFS = [HALO_G - (KG - 1) + k for k in range(KG)]
_G_FWD_SHIFTS = (5, 6, 7)
_G_BWD_OFFS = [KG - 1 - k for k in range(KG)]
_G_BWD_SHIFTS = (1, 2, 3)


def _gdn_short_conv(pre_ref, ext_ref, sh_ref, n_rows, w_ref):
    _shift_copies(ext_ref, sh_ref, n_rows, _G_FWD_SHIFTS)
    _tap_conv(pre_ref, n_rows, ext_ref, sh_ref, w_ref, _G_FWD_OFFS, shifts=_G_FWD_SHIFTS)
    return pre_ref[...]


def _l2norm_heads(act, scale):
    outs, rs = [], []
    for h in range(NH):
        a = act[:, h * HD:(h + 1) * HD]
        r = lax.rsqrt(jnp.sum(a * a, axis=-1, keepdims=True) + L2_EPS)
        outs.append(a * (r * scale))
        rs.append(jnp.broadcast_to(r, a.shape))
    return jnp.concatenate(outs, axis=-1), jnp.concatenate(rs, axis=-1)


def _gate_math(ba, al_ref, dt_ref):
    lane = lax.broadcasted_iota(jnp.int32, ba.shape, 1)
    is_b = lane < NH
    is_a = (lane >= NH) & (lane < 2 * NH)
    sp, sg = _softplus_and_sigmoid(ba + dt_ref[...])
    neg_a = -jnp.exp(al_ref[...])
    return is_b, is_a, _sigmoid(ba), neg_a * sp, sg, neg_a


def _gdn_pre_fwd(p_qkv, p_ba, cw, alog_v, dt_v, *, tt):
    t = p_qkv.shape[0]
    hb = tt // HALO_G

    def body(q_ref, k_ref, v_ref, qh_ref, kh_ref, vh_ref, ba_ref, wq_ref, wk_ref, wv_ref, al_ref, dt_ref,
             qn_ref, kn_ref, va_ref, gt_ref, ext_ref, sh_ref, pre_ref):
        first = pl.program_id(0) == 0

        def conv_act(x_ref, xh_ref, w_ref):
            ext_ref[0:HALO_G, :] = jnp.where(first, 0.0, xh_ref[...])
            ext_ref[HALO_G:, :] = x_ref[...]
            pre = _gdn_short_conv(pre_ref, ext_ref, sh_ref, tt, w_ref)
            return pre * _sigmoid(pre)

        qn_ref[...] = _l2norm_heads(conv_act(q_ref, qh_ref, wq_ref), HD ** -0.5)[0]
        kn_ref[...] = _l2norm_heads(conv_act(k_ref, kh_ref, wk_ref), 1.0)[0]
        va_ref[...] = conv_act(v_ref, vh_ref, wv_ref)
        is_b, is_a, beta, g, _, _ = _gate_math(ba_ref[...], al_ref, dt_ref)
        gc = _dot_hi(_chunk_tri(tt, lower=True), jnp.where(is_a, g, 0.0))
        gt_ref[...] = jnp.where(is_b, beta, gc)

    def main(col):
        return pl.BlockSpec((tt, D), lambda i: (i, col))

    def prev(col):
        return pl.BlockSpec((HALO_G, D), lambda i: (jnp.maximum(i * hb - 1, 0), col))

    def wspec(col):
        return pl.BlockSpec((8, D), lambda i: (0, col))

    vec = pl.BlockSpec((1, HD), lambda i: (0, 0))
    gblk = pl.BlockSpec((tt, HD), lambda i: (i, 0))
    sds = jax.ShapeDtypeStruct((t, D), f32)
    return pl.pallas_call(
        body, name="gdn_pre_fwd", grid=(t // tt,),
        in_specs=[main(0), main(1), main(2), prev(0), prev(1), prev(2), gblk, wspec(0), wspec(1), wspec(2), vec, vec],
        out_specs=[pl.BlockSpec((tt, D), lambda i: (i, 0))] * 3 + [gblk],
        out_shape=[sds] * 3 + [jax.ShapeDtypeStruct((t, HD), f32)],
        scratch_shapes=[pltpu.VMEM((tt + HALO_G, D), f32), pltpu.VMEM((KG - 1, tt, D), f32), pltpu.VMEM((tt, D), f32)],
        compiler_params=_cparams("parallel"),
    )(p_qkv, p_qkv, p_qkv, p_qkv, p_qkv, p_qkv, p_ba, cw, cw, cw, alog_v, dt_v)


def _gdn_pre_bwd(p_qkv, p_ba, cw, alog_v, dt_v, dqn, dkn, dva, dgt, *, tt):
    t = p_qkv.shape[0]
    hb = tt // HALO_G
    n_tiles = t // tt
    last_hb = t // HALO_G - 1
    ne = tt + HALO_G

    def body(q_ref, k_ref, v_ref, qp_ref, kp_ref, vp_ref, qx_ref, kx_ref, vx_ref,
             dq_ref, dk_ref, dv_ref, dqx_ref, dkx_ref, dvx_ref, ba_ref, dgt_ref,
             wq_ref, wk_ref, wv_ref, al_ref, dt_ref,
             dp_ref, dba_ref, dcw_ref, dad_ref, ext_ref, sh_ref, pre_ref, dpre_ref, draw_ref):
        i = pl.program_id(0)
        first = i == 0
        last = i == n_tiles - 1

        @pl.when(first)
        def _():
            dcw_ref[...] = jnp.zeros_like(dcw_ref)
            dad_ref[...] = jnp.zeros_like(dad_ref)

        def one(x_ref, xp_ref, xx_ref, d_ref, dx_ref, w_ref, col, scale):
            ext_ref[0:HALO_G, :] = jnp.where(first, 0.0, xp_ref[...])
            ext_ref[HALO_G:HALO_G + tt, :] = x_ref[...]
            ext_ref[HALO_G + tt:, :] = xx_ref[...]
            pre = _gdn_short_conv(pre_ref, ext_ref, sh_ref, ne, w_ref)
            act, dact = _silu_and_grad(pre)
            d_out = jnp.concatenate([d_ref[...], jnp.where(last, 0.0, dx_ref[...])], axis=0)
            if scale is None:
                d_act = d_out
            else:
                parts = []
                for h in range(NH):
                    a = act[:, h * HD:(h + 1) * HD]
                    dn = d_out[:, h * HD:(h + 1) * HD]
                    r = lax.rsqrt(jnp.sum(a * a, axis=-1, keepdims=True) + L2_EPS)
                    parts.append(scale * r * (dn - a * (r * r) * jnp.sum(dn * a, axis=-1, keepdims=True)))
                d_act = jnp.concatenate(parts, axis=-1)
            dpre_ref[...] = d_act * dact
            _tap_corr(dcw_ref.at[:, col * D:(col + 1) * D], tt, dpre_ref, ext_ref, sh_ref, _G_FWD_OFFS, shifts=_G_FWD_SHIFTS)
            _shift_copies(dpre_ref, sh_ref, tt, _G_BWD_SHIFTS)
            _tap_conv(draw_ref, tt, dpre_ref, sh_ref, w_ref, _G_BWD_OFFS, shifts=_G_BWD_SHIFTS)
            dp_ref[:, col * D:(col + 1) * D] = draw_ref[...].astype(bf16)

        one(q_ref, qp_ref, qx_ref, dq_ref, dqx_ref, wq_ref, 0, HD ** -0.5)
        one(k_ref, kp_ref, kx_ref, dk_ref, dkx_ref, wk_ref, 1, 1.0)
        one(v_ref, vp_ref, vx_ref, dv_ref, dvx_ref, wv_ref, 2, None)

        is_b, is_a, beta, g, sg, neg_a = _gate_math(ba_ref[...], al_ref, dt_ref)
        dgt_v = dgt_ref[...]
        dg = _dot_hi(_chunk_tri(tt, lower=False), jnp.where(is_a, dgt_v, 0.0))
        d_al = jnp.where(is_a, dg * neg_a * sg, 0.0)
        dba_ref[...] = jnp.where(is_b, dgt_v * beta * (1.0 - beta), d_al)
        dad_ref[0:1, :] += jnp.sum(jnp.where(is_a, dg * g, 0.0), axis=0, keepdims=True)
        dad_ref[1:2, :] += jnp.sum(d_al, axis=0, keepdims=True)

    def main(col):
        return pl.BlockSpec((tt, D), lambda i: (i, col))

    def prev(col):
        return pl.BlockSpec((HALO_G, D), lambda i: (jnp.maximum(i * hb - 1, 0), col))

    def nxt(col):
        return pl.BlockSpec((HALO_G, D), lambda i: (jnp.minimum((i + 1) * hb, last_hb), col))

    def wspec(col):
        return pl.BlockSpec((8, D), lambda i: (0, col))

    vec = pl.BlockSpec((1, HD), lambda i: (0, 0))
    gblk = pl.BlockSpec((tt, HD), lambda i: (i, 0))
    return pl.pallas_call(
        body, name="gdn_pre_bwd", grid=(n_tiles,),
        in_specs=[main(0), main(1), main(2), prev(0), prev(1), prev(2), nxt(0), nxt(1), nxt(2),
                  main(0), main(0), main(0), nxt(0), nxt(0), nxt(0), gblk, gblk,
                  wspec(0), wspec(1), wspec(2), vec, vec],
        out_specs=[pl.BlockSpec((tt, 3 * D), lambda i: (i, 0)), pl.BlockSpec((tt, HD), lambda i: (i, 0)),
                   pl.BlockSpec((8, 3 * D), lambda i: (0, 0)), pl.BlockSpec((8, HD), lambda i: (0, 0))],
        out_shape=[jax.ShapeDtypeStruct((t, 3 * D), bf16), jax.ShapeDtypeStruct((t, HD), f32),
                   jax.ShapeDtypeStruct((8, 3 * D), f32), jax.ShapeDtypeStruct((8, HD), f32)],
        scratch_shapes=[pltpu.VMEM((HALO_G + tt + HALO_G, D), f32), pltpu.VMEM((KG - 1, ne, D), f32),
                        pltpu.VMEM((ne, D), f32), pltpu.VMEM((ne, D), f32), pltpu.VMEM((tt, D), f32)],
        compiler_params=_cparams("arbitrary"),
    )(p_qkv, p_qkv, p_qkv, p_qkv, p_qkv, p_qkv, p_qkv, p_qkv, p_qkv,
      dqn, dkn, dva, dqn, dkn, dva, p_ba, dgt, cw, cw, cw, alog_v, dt_v)


def _dot_b(a, b, dims):
    return lax.dot_general(a.astype(bf16), b.astype(bf16), (dims, ((), ())), preferred_element_type=f32)


def _inverse_by_doubling(ms):
    heads = range(len(ms))
    r = lax.broadcasted_iota(jnp.int32, (CH, CH), 0)
    c = lax.broadcasted_iota(jnp.int32, (CH, CH), 1)
    eye = jnp.where(r == c, 1.0, 0.0).astype(f32)
    p = [eye + ms[h] for h in heads]
    mp = ms
    for _ in range(5):
        mp = [_dot_b(mp[h], mp[h], _NN) for h in heads]
        pm = [_dot_b(p[h], mp[h], _NN) for h in heads]
        p = [p[h] + pm[h] for h in heads]
    return tuple(p)


@jax.custom_vjp
def _known_inverse(ms, ps):
    return ps


def _known_inverse_fwd(ms, ps):
    return ps, ps


def _known_inverse_bwd(ps, cts):
    heads = range(len(ps))
    left = [_dot_b(ps[h], cts[h], _TN) for h in heads]
    return tuple(_dot_b(left[h], ps[h], _NT) for h in heads), tuple(jnp.zeros_like(p) for p in ps)


_known_inverse.defvjp(_known_inverse_fwd, _known_inverse_bwd)


def _chunk_prepare(qs, ks, vs, gcs, bbs, ps=None):
    heads = range(len(qs))
    r = lax.broadcasted_iota(jnp.int32, (CH, CH), 0)
    c = lax.broadcasted_iota(jnp.int32, (CH, CH), 1)
    causal = r >= c
    strict = r > c
    gc_row = [gcs[h].T[:CH, :] for h in heads]
    decay = [jnp.where(causal, jnp.exp(jnp.where(causal, gcs[h][:, :CH] - gc_row[h], 0.0)), 0.0) for h in heads]
    kb = [ks[h] * bbs[h] for h in heads]
    egc = [jnp.exp(gcs[h]) for h in heads]
    kk = [_dot_b(kb[h], ks[h], _NT) for h in heads]
    qk = [_dot_b(qs[h], ks[h], _NT) for h in heads]
    m = tuple(-jnp.where(strict, kk[h] * decay[h], 0.0) for h in heads)
    p = _inverse_by_doubling(m) if ps is None else _known_inverse(m, ps)
    u = [_dot_b(p[h], vs[h] * bbs[h], _NN) for h in heads]
    w = [_dot_b(p[h], kb[h] * egc[h], _NN) for h in heads]
    intra = [jnp.where(causal, qk[h] * decay[h], 0.0) for h in heads]
    g_last = [gcs[h][CH - 1:CH, :] for h in heads]
    k_dec = [ks[h] * jnp.exp(g_last[h] - gcs[h]) for h in heads]
    q_dec = [qs[h] * egc[h] for h in heads]
    e_last = [jnp.exp(g_last[h]) for h in heads]
    return u, w, intra, q_dec, k_dec, e_last, p


def _chunk_apply(u, w, intra, q_dec, k_dec, e_last, ss):
    heads = range(len(ss))
    ws = [_dot_b(w[h], ss[h], _NN) for h in heads]
    qs_s = [_dot_b(q_dec[h], ss[h], _NN) for h in heads]
    v_new = [u[h] - ws[h] for h in heads]
    iv = [_dot_b(intra[h], v_new[h], _NN) for h in heads]
    kv = [_dot_b(k_dec[h], v_new[h], _TN) for h in heads]
    o = tuple(qs_s[h] + iv[h] for h in heads)
    s_new = tuple(ss[h] * e_last[h] + kv[h] for h in heads)
    return o, s_new


def _chunk_group_fn(ins, ss, ps=None):
    n = len(ss)
    prep = _chunk_prepare(*(sum((tuple(c[i]) for c in ins), ()) for i in range(5)), ps=ps)
    outs, befores = [], []
    for g in range(len(ins)):
        befores.append(ss)
        o, ss = _chunk_apply(*(x[g * n:(g + 1) * n] for x in prep[:6]), ss)
        outs.append(o)
    return tuple(outs), tuple(befores), ss, prep[6]


def _head_cols():
    return [slice(h * HD, (h + 1) * HD) for h in range(NH)]


def _head_gates(gt):
    gcs = tuple(jnp.broadcast_to(gt[:, NH + h:NH + h + 1], (CH, HD)) for h in range(NH))
    bbs = tuple(jnp.broadcast_to(gt[:, h:h + 1], (CH, HD)) for h in range(NH))
    return gcs, bbs


def _gdn_scan_fwd(qn, kn, va, gates, *, tt):
    t = qn.shape[0]
    cpb = tt // CH
    group = min(SCAN_GROUP, cpb)

    def body(q_ref, k_ref, v_ref, gt_ref, o_ref, st_ref, p_ref, s_scr):
        @pl.when(pl.program_id(0) == 0)
        def _():
            s_scr[...] = jnp.zeros_like(s_scr)

        cols = _head_cols()

        def inputs(ci):
            rows = pl.ds(pl.multiple_of(ci * CH, CH), CH)
            gcs, bbs = _head_gates(gt_ref[rows, :])
            return tuple(tuple(ref[rows, cl] for cl in cols) for ref in (q_ref, k_ref, v_ref)) + (gcs, bbs)

        def step(gi, carry):
            chunks = [group * gi + g for g in range(group)]
            outs, befores, s_end, p = _chunk_group_fn([inputs(ci) for ci in chunks], tuple(s_scr[h] for h in range(NH)))
            for g, ci in enumerate(chunks):
                rows = pl.ds(pl.multiple_of(ci * CH, CH), CH)
                for h in range(NH):
                    st_ref[ci, h] = befores[g][h]
                    o_ref[rows, cols[h]] = outs[g][h]
                    p_ref[ci, h] = p[g * NH + h].astype(bf16)
            for h in range(NH):
                s_scr[h] = s_end[h]
            return carry

        lax.fori_loop(0, cpb // group, step, 0)

    blk = pl.BlockSpec((tt, D), lambda i: (i, 0))
    return pl.pallas_call(
        body, name="gdn_scan_fwd", grid=(t // tt,),
        in_specs=[blk] * 3 + [pl.BlockSpec((tt, HD), lambda i: (i, 0))],
        out_specs=[blk, pl.BlockSpec((cpb, NH, HD, HD), lambda i: (i, 0, 0, 0)),
                   pl.BlockSpec((cpb, NH, CH, CH), lambda i: (i, 0, 0, 0))],
        out_shape=[jax.ShapeDtypeStruct((t, D), f32), jax.ShapeDtypeStruct((t // CH, NH, HD, HD), f32),
                   jax.ShapeDtypeStruct((t // CH, NH, CH, CH), bf16)],
        scratch_shapes=[pltpu.VMEM((NH, HD, HD), f32)],
        compiler_params=_cparams("arbitrary"),
    )(qn, kn, va, gates)


def _gdn_scan_bwd(qn, kn, va, gates, states, inverses, do, *, tt):
    t = qn.shape[0]
    nblk = t // tt
    cpb = tt // CH

    def body(q_ref, k_ref, v_ref, gt_ref, st_ref, p_ref, do_ref, dq_ref, dk_ref, dv_ref, dgt_ref, ds_scr):
        @pl.when(pl.program_id(0) == 0)
        def _():
            ds_scr[...] = jnp.zeros_like(ds_scr)

        cols = _head_cols()

        def rows_of(ci):
            return pl.ds(pl.multiple_of(ci * CH, CH), CH)

        def inputs(ci):
            gcs, bbs = _head_gates(gt_ref[rows_of(ci), :])
            return tuple(tuple(ref[rows_of(ci), cl] for cl in cols) for ref in (q_ref, k_ref, v_ref)) + (gcs, bbs)

        def step(j, carry):
            ci = cpb - 1 - j
            ps = tuple(p_ref[ci, h].astype(f32) for h in range(NH))

            def one(ins, ss):
                outs, _, s_end, _ = _chunk_group_fn([ins], ss, ps=ps)
                return outs[0], s_end

            _, vjp = jax.vjp(one, inputs(ci), tuple(st_ref[ci, h] for h in range(NH)))
            grads, ds = vjp((tuple(do_ref[rows_of(ci), cl] for cl in cols), tuple(ds_scr[h] for h in range(NH))))
            lane = lax.broadcasted_iota(jnp.int32, (CH, HD), 1)
            dgt = jnp.zeros((CH, HD), f32)
            for h in range(NH):
                for ref, g in zip((dq_ref, dk_ref, dv_ref), grads[:3]):
                    ref[rows_of(ci), cols[h]] = g[h]
                dgt = dgt + jnp.where(lane == NH + h, jnp.sum(grads[3][h], axis=-1, keepdims=True), 0.0)
                dgt = dgt + jnp.where(lane == h, jnp.sum(grads[4][h], axis=-1, keepdims=True), 0.0)
                ds_scr[h] = ds[h]
            dgt_ref[rows_of(ci), :] = dgt
            return carry

        lax.fori_loop(0, cpb, step, 0)

    blk = pl.BlockSpec((tt, D), lambda i: (nblk - 1 - i, 0))
    sblk = pl.BlockSpec((cpb, NH, HD, HD), lambda i: (nblk - 1 - i, 0, 0, 0))
    sds = jax.ShapeDtypeStruct((t, D), f32)
    gblk = pl.BlockSpec((tt, HD), lambda i: (nblk - 1 - i, 0))
    pblk = pl.BlockSpec((cpb, NH, CH, CH), lambda i: (nblk - 1 - i, 0, 0, 0))
    return pl.pallas_call(
        body, name="gdn_scan_bwd", grid=(nblk,),
        in_specs=[blk] * 3 + [gblk, sblk, pblk, blk],
        out_specs=[blk] * 3 + [gblk], out_shape=[sds] * 3 + [jax.ShapeDtypeStruct((t, HD), f32)],
        scratch_shapes=[pltpu.VMEM((NH, HD, HD), f32)],
        compiler_params=_cparams("arbitrary"),
    )(qn, kn, va, gates, states, inverses, do)


def _rms_heads(o):
    ons, rs = [], []
    for h in range(NH):
        a = o[:, h * HD:(h + 1) * HD]
        r = lax.rsqrt(jnp.mean(a * a, axis=-1, keepdims=True) + RMS_EPS)
        ons.append(a * r)
        rs.append(jnp.broadcast_to(r, a.shape))
    return jnp.concatenate(ons, axis=-1), jnp.concatenate(rs, axis=-1)


def _gdn_post_fwd(o, p_gz, ng_b, w_out, *, tt):
    t = o.shape[0]

    def body(o_ref, gz_ref, ng_ref, w_ref, og_ref, y_ref):
        on, _ = _rms_heads(o_ref[...])
        z, _ = _silu_and_grad(gz_ref[...])
        og = (on * ng_ref[...] * z).astype(bf16)
        og_ref[...] = og
        y_ref[...] = jnp.dot(og, w_ref[...], preferred_element_type=f32)

    blk = pl.BlockSpec((tt, D), lambda i: (i, 0))
    return pl.pallas_call(
        body, name="gdn_post_fwd", grid=(t // tt,),
        in_specs=[blk, blk, pl.BlockSpec((1, D), lambda i: (0, 0)), pl.BlockSpec((D, D), lambda i: (0, 0))],
        out_specs=[blk, blk], out_shape=[jax.ShapeDtypeStruct((t, D), bf16), jax.ShapeDtypeStruct((t, D), f32)],
        compiler_params=_cparams("parallel"),
    )(o, p_gz, ng_b, w_out)


def _gdn_post_bwd(o, p_gz, ng_b, w_out, dyg, *, tt):
    t = o.shape[0]

    def body(o_ref, gz_ref, ng_ref, w_ref, dyg_ref, do_ref, dgz_ref, dng_ref):
        @pl.when(pl.program_id(0) == 0)
        def _():
            dng_ref[...] = jnp.zeros_like(dng_ref)

        on, r = _rms_heads(o_ref[...])
        z, dz = _silu_and_grad(gz_ref[...])
        dog_v = lax.dot_general(dyg_ref[...], w_ref[...], (_NT, ((), ())), preferred_element_type=f32)
        ng = ng_ref[...]
        dgz_ref[...] = (dog_v * on * ng * dz).astype(bf16)
        dy = dog_v * z
        dng_all = jnp.sum(dy * on, axis=0, keepdims=True)
        dng = dng_all[:, 0:HD]
        for h in range(1, NH):
            dng = dng + dng_all[:, h * HD:(h + 1) * HD]
        dng_ref[0:1, :] += dng
        don = dy * ng
        prod = don * on
        parts = []
        for h in range(NH):
            sl = slice(h * HD, (h + 1) * HD)
            parts.append(don[:, sl] - on[:, sl] * jnp.mean(prod[:, sl], axis=-1, keepdims=True))
        do_ref[...] = r * jnp.concatenate(parts, axis=-1)

    blk = pl.BlockSpec((tt, D), lambda i: (i, 0))
    return pl.pallas_call(
        body, name="gdn_post_bwd", grid=(t // tt,),
        in_specs=[blk, blk, pl.BlockSpec((1, D), lambda i: (0, 0)), pl.BlockSpec((D, D), lambda i: (0, 0)), blk],
        out_specs=[blk, blk, pl.BlockSpec((8, HD), lambda i: (0, 0))],
        out_shape=[jax.ShapeDtypeStruct((t, D), f32), jax.ShapeDtypeStruct((t, D), bf16),
                   jax.ShapeDtypeStruct((8, HD), f32)],
        compiler_params=_cparams("arbitrary"),
    )(o, p_gz, ng_b, w_out, dyg)


def _merge(x, y_conf, y_gdn, p_gates, target, w_o, ln_g, ln_b, *, tt):
    t = x.shape[0]

    def body(x_ref, yc_ref, yg_ref, gc_ref, gg_ref, tg_ref, w_ref, g_ref, b_ref,
             loss_ref, dxd_ref, dyc_ref, dyg_ref, dpg_ref, h_ref, dz_ref, dvec_ref):
        @pl.when(pl.program_id(0) == 0)
        def _():
            loss_ref[...] = jnp.zeros_like(loss_ref)
            dvec_ref[...] = jnp.zeros_like(dvec_ref)

        sc = _sigmoid(gc_ref[...])
        sg = _sigmoid(gg_ref[...])
        yc = yc_ref[...]
        yg = yg_ref[...]
        h = (sc * yc + sg * yg).astype(bf16)
        h_ref[...] = h
        z = DN_ALPHA * x_ref[...] + jnp.dot(h, w_ref[...], preferred_element_type=f32)
        mu = jnp.mean(z, axis=-1, keepdims=True)
        cen = z - mu
        rstd = lax.rsqrt(jnp.mean(cen * cen, axis=-1, keepdims=True) + LN_EPS)
        xhat = cen * rstd
        err = xhat * g_ref[...] + b_ref[...] - tg_ref[...]
        loss_ref[...] += 0.5 / D * jnp.sum(err * err)
        dy = err * (1.0 / D)
        dvec_ref[0:1, :] += jnp.sum(dy * xhat, axis=0, keepdims=True)
        dvec_ref[1:2, :] += jnp.sum(dy, axis=0, keepdims=True)
        dxhat = dy * g_ref[...]
        dz = rstd * (dxhat - jnp.mean(dxhat, axis=-1, keepdims=True)
                     - xhat * jnp.mean(dxhat * xhat, axis=-1, keepdims=True))
        dxd_ref[...] = DN_ALPHA * dz
        dz_b = dz.astype(bf16)
        dz_ref[...] = dz_b
        dh = lax.dot_general(dz_b, w_ref[...], (_NT, ((), ())), preferred_element_type=f32)
        dyc_ref[...] = (dh * sc).astype(bf16)
        dyg_ref[...] = (dh * sg).astype(bf16)
        dpg_ref[:, 0:D] = (dh * yc * sc * (1.0 - sc)).astype(bf16)
        dpg_ref[:, D:] = (dh * yg * sg * (1.0 - sg)).astype(bf16)

    blk = pl.BlockSpec((tt, D), lambda i: (i, 0))
    wblk = pl.BlockSpec((D, D), lambda i: (0, 0))
    vec = pl.BlockSpec((1, D), lambda i: (0, 0))
    return pl.pallas_call(
        body, name="merge_norm_loss", grid=(t // tt,),
        in_specs=[blk, blk, blk, pl.BlockSpec((tt, D), lambda i: (i, 0)), pl.BlockSpec((tt, D), lambda i: (i, 1)),
                  blk, wblk, vec, vec],
        out_specs=[pl.BlockSpec((8, HD), lambda i: (0, 0)), blk, blk, blk,
                   pl.BlockSpec((tt, 2 * D), lambda i: (i, 0)), blk, blk, pl.BlockSpec((8, D), lambda i: (0, 0))],
        out_shape=[jax.ShapeDtypeStruct((8, HD), f32), jax.ShapeDtypeStruct((t, D), f32),
                   jax.ShapeDtypeStruct((t, D), bf16), jax.ShapeDtypeStruct((t, D), bf16),
                   jax.ShapeDtypeStruct((t, 2 * D), bf16), jax.ShapeDtypeStruct((t, D), bf16),
                   jax.ShapeDtypeStruct((t, D), bf16), jax.ShapeDtypeStruct((8, D), f32)],
        compiler_params=_cparams("arbitrary"),
    )(x, y_conf, y_gdn, p_gates, p_gates, target, w_o, ln_g, ln_b)


def _place():
    return lax.axis_index("x"), lax.axis_index("y"), lax.axis_index("c")


def _any_specs(n):
    return [pl.BlockSpec(memory_space=pl.ANY)] * n


def _sibling_merge(arrs, name, half_axes=None):
    k = len(arrs)

    def half_shape(a, ax):
        shape = list(a.shape)
        shape[ax] //= 2
        return tuple(shape)

    def body(*refs):
        a_refs, o_refs = refs[:k], refs[k:2 * k]
        send_sems, recv_sems = refs[2 * k:]
        x, y, c = _place()
        sends = []
        for i in range(k):
            src = a_refs[i]
            if half_axes is not None:
                n = a_refs[i].shape[half_axes[i]] // 2
                idx = [slice(None)] * len(a_refs[i].shape)
                idx[half_axes[i]] = pl.ds((1 - c) * n, n)
                src = a_refs[i].at[tuple(idx)]
            cp = pltpu.make_async_remote_copy(src_ref=src, dst_ref=o_refs[i], send_sem=send_sems.at[i],
                                              recv_sem=recv_sems.at[i], device_id=(x, y, 1 - c), device_id_type=MESH)
            cp.start()
            sends.append(cp)
        for cp in sends:
            cp.wait()

    shapes = [a.shape if half_axes is None else half_shape(a, ax) for a, ax in zip(arrs, half_axes or arrs)]
    return pl.pallas_call(
        body, name=name, in_specs=_any_specs(k), out_specs=_any_specs(k),
        out_shape=[jax.ShapeDtypeStruct(sh, a.dtype) for sh, a in zip(shapes, arrs)],
        scratch_shapes=[pltpu.SemaphoreType.DMA((k,)), pltpu.SemaphoreType.DMA((k,))],
    )(*arrs)


def _join_halves(mine, other, axis=-2):
    c = lax.axis_index("c")
    return jnp.concatenate([jnp.where(c == 0, mine, other), jnp.where(c == 0, other, mine)], axis=axis)


def _chip_exchange_ops(a_refs, o_refs, send_sems, recv_sems, local_sems, scatter):
    k = len(a_refs)
    x, y, c = _place()
    me = 2 * x + y
    peers = [(1 - x, y), (x, 1 - y), (1 - x, 1 - y)]

    def src(i, j):
        return a_refs[i].at[j] if scatter else a_refs[i]

    def copy(i, n, send_j, slot):
        px, py = peers[n]
        return pltpu.make_async_remote_copy(
            src_ref=src(i, send_j), dst_ref=o_refs[i].at[slot], send_sem=send_sems.at[3 * i + n],
            recv_sem=recv_sems.at[3 * i + n], device_id=(px, py, c), device_id_type=MESH)

    def owns():
        return [pltpu.make_async_copy(src(i, me), o_refs[i].at[me], local_sems.at[i]) for i in range(k)]

    def sends():
        return [copy(i, n, 2 * peers[n][0] + peers[n][1], me) for n in range(3) for i in range(k)]

    def start():
        for cp in owns() + sends():
            cp.start()

    def finish():
        for n in range(3):
            for i in range(k):
                copy(i, n, me, 2 * peers[n][0] + peers[n][1]).wait_recv()
        for cp in sends():
            cp.wait_send()
        for cp in owns():
            cp.wait()

    return start, finish


def _chip_exchange_shapes(arrs, scatter):
    return [jax.ShapeDtypeStruct((N_CHIPS,) + tuple(a.shape[1:] if scatter else a.shape), a.dtype) for a in arrs]


def _chip_exchange_sems(k):
    return [pltpu.SemaphoreType.DMA((3 * k,)), pltpu.SemaphoreType.DMA((3 * k,)), pltpu.SemaphoreType.DMA((k,))]


def _gather_halves(halves, wholes, name):
    kh, kw = len(halves), len(wholes)
    k = kh + kw

    def body(*refs):
        a_refs, got_refs, oth_refs = refs[:k], refs[k:2 * k], refs[2 * k:2 * k + kh]
        send_sems, recv_sems, local_sems, fwd_send_sems, fwd_recv_sems = refs[2 * k + kh:]
        x, y, c = _place()
        me = 2 * x + y
        start, _ = _chip_exchange_ops(a_refs, got_refs, send_sems, recv_sems, local_sems, False)
        slots = [me] + [2 * px + py for px, py in [(1 - x, y), (x, 1 - y), (1 - x, 1 - y)]]

        def forward(i, r):
            src = a_refs[i] if r == 0 else got_refs[i].at[slots[r]]
            return pltpu.make_async_remote_copy(
                src_ref=src, dst_ref=oth_refs[i].at[slots[r]], send_sem=fwd_send_sems.at[4 * i + r],
                recv_sem=fwd_recv_sems.at[4 * i + r], device_id=(x, y, 1 - c), device_id_type=MESH)

        def arrival(i, n):
            px, py = [(1 - x, y), (x, 1 - y), (1 - x, 1 - y)][n]
            return pltpu.make_async_remote_copy(
                src_ref=a_refs[i], dst_ref=got_refs[i].at[slots[n + 1]], send_sem=send_sems.at[3 * i + n],
                recv_sem=recv_sems.at[3 * i + n], device_id=(px, py, c), device_id_type=MESH)

        start()
        for i in range(kh):
            forward(i, 0).start()
        for n in range(3):
            for i in range(k):
                arrival(i, n).wait_recv()
                if i < kh:
                    forward(i, n + 1).start()
        for i in range(kh):
            for r in range(4):
                forward(i, r).wait()
        for n in range(3):
            for i in range(k):
                arrival(i, n).wait_send()
        for i in range(k):
            pltpu.make_async_copy(a_refs[i], got_refs[i].at[me], local_sems.at[i]).wait()

    arrs = list(halves) + list(wholes)
    shapes = _chip_exchange_shapes(arrs, False)
    out = pl.pallas_call(
        body, name=name, in_specs=_any_specs(k), out_specs=_any_specs(k + kh),
        out_shape=shapes + shapes[:kh],
        scratch_shapes=_chip_exchange_sems(k) + [pltpu.SemaphoreType.DMA((4 * kh,)), pltpu.SemaphoreType.DMA((4 * kh,))],
    )(*arrs)
    return [(out[i], out[k + i]) for i in range(kh)], out[kh:k]


def _pair_sum(g_all, got, c_arr, name, out_dtype, axis):
    rows, w = got.shape[1:]
    if axis == -2:
        blk, steps = (1, rows // 4, w), 4
        pick = lambda j, i, c_ref: (j, c_ref[0] * steps + i, 0)
        mine = lambda j, i, c_ref: (j, i, 0)
    else:
        blk, steps = (1, rows, LANE), w // LANE
        pick = lambda j, i, c_ref: (j, 0, c_ref[0] * steps + i)
        mine = lambda j, i, c_ref: (j, 0, i)

    def body(c_ref, a_ref, b_ref, o_ref):
        o_ref[...] = (a_ref[...] + b_ref[...]).astype(out_dtype)

    return pl.pallas_call(
        body, name=name,
        grid_spec=pltpu.PrefetchScalarGridSpec(
            num_scalar_prefetch=1, grid=(N_CHIPS, steps),
            in_specs=[pl.BlockSpec(blk, pick), pl.BlockSpec(blk, mine)], out_specs=pl.BlockSpec(blk, mine)),
        out_shape=jax.ShapeDtypeStruct(got.shape, out_dtype),
        compiler_params=_cparams("parallel", "parallel"),
    )(c_arr, g_all, got)


def _sum_slots(a, name):
    n, w = a.shape[1:]
    by_rows = n % 64 == 0
    in_blk = pl.BlockSpec((N_CHIPS, n // 4, w), lambda i: (0, i, 0)) if by_rows else pl.BlockSpec((N_CHIPS, n, LANE), lambda i: (0, 0, i))
    out_blk = pl.BlockSpec((n // 4, w), lambda i: (i, 0)) if by_rows else pl.BlockSpec((n, LANE), lambda i: (0, i))

    def body(a_ref, o_ref):
        o_ref[...] = ((a_ref[0].astype(f32) + a_ref[1].astype(f32)) + a_ref[2].astype(f32)) + a_ref[3].astype(f32)

    return pl.pallas_call(
        body, name=name, grid=(4 if by_rows else w // LANE,),
        in_specs=[in_blk], out_specs=out_blk,
        out_shape=jax.ShapeDtypeStruct((n, w), f32),
        compiler_params=_cparams("parallel"),
    )(a)


def _adamw(w, g, m, v, name):
    rows, width = w.shape
    by_rows = rows % 64 == 0
    c1 = 1.0 / (1.0 - ADAM_B1 ** ADAM_STEP)
    c2 = 1.0 / (1.0 - ADAM_B2 ** ADAM_STEP)

    def body(w_ref, g_ref, m_ref, v_ref, d_ref, mo_ref, vo_ref):
        g_v = g_ref[...]
        m_new = ADAM_B1 * m_ref[...] + (1.0 - ADAM_B1) * g_v
        v_new = ADAM_B2 * v_ref[...] + (1.0 - ADAM_B2) * (g_v * g_v)
        mo_ref[...] = m_new
        vo_ref[...] = v_new
        d_ref[...] = -ADAM_LR * ((m_new * c1) / (jnp.sqrt(v_new * c2) + ADAM_EPS) + ADAM_WD * w_ref[...])

    blk = pl.BlockSpec((rows // 8, width), lambda i: (i, 0)) if by_rows else pl.BlockSpec((rows, LANE), lambda i: (0, i))
    sds = jax.ShapeDtypeStruct((rows, width), f32)
    return pl.pallas_call(
        body, name=name, grid=(8 if by_rows else width // LANE,),
        in_specs=[blk] * 4, out_specs=[blk] * 3, out_shape=[sds] * 3,
        compiler_params=_cparams("parallel"),
    )(w, g, m, v)


R_DW = 3 * SQ_BLK
R_CW = R_DW + 8
R_VEC = R_CW + 8
R_SMALL = R_VEC + 8
REST_ROWS = 896


def _pack_small(conf_dw_w, gdn_conv_w, vecs, a_log, dt_bias, norm_g):
    dw = jnp.pad(conf_dw_w.reshape(-1), (0, 8 * D - KC * SQ_BLK)).reshape(8, D)
    cw = jnp.pad(gdn_conv_w.reshape(-1), (0, 5 * D)).reshape(8, D)
    vec = jnp.pad(jnp.stack(vecs), ((0, 3), (0, 0)))
    small = jnp.pad(jnp.concatenate([a_log, dt_bias, norm_g]), (0, D - 2 * NH - HD)).reshape(1, D)
    return jnp.pad(jnp.concatenate([dw, cw, vec, small], axis=0), ((0, REST_ROWS - R_SMALL - 1), (0, 0)))


def _pack_rest(conf_w_out, gdn_w_out, w_o, small):
    return jnp.concatenate([conf_w_out, gdn_w_out, w_o, small], axis=0)


def _unpack_rest(p):
    conf_dw_w = p[R_DW:R_DW + 8].reshape(-1)[:KC * SQ_BLK].reshape(KC, SQ_BLK)
    gdn_conv_w = p[R_CW:R_CW + 3].reshape(KG, 3 * SQ_BLK)
    small = p[R_SMALL]
    return dict(conf_w_out=p[0:SQ_BLK], gdn_w_out=p[SQ_BLK:2 * SQ_BLK], w_o=p[2 * SQ_BLK:R_DW],
                conf_dw_w=conf_dw_w, gdn_conv_w=gdn_conv_w, conf_dw_b=p[R_VEC], conf_ln_g=p[R_VEC + 1],
                conf_ln_b=p[R_VEC + 2], post_ln_g=p[R_VEC + 3], post_ln_b=p[R_VEC + 4],
                gdn_A_log=small[0:NH], gdn_dt_bias=small[NH:2 * NH], gdn_norm_g=small[2 * NH:2 * NH + HD])


_WEIGHT_ORDER = ("w_in", "conf_dw_w", "conf_dw_b", "conf_ln_g", "conf_ln_b", "conf_w_out", "gdn_conv_w",
                 "gdn_A_log", "gdn_dt_bias", "gdn_norm_g", "gdn_w_out", "w_o", "post_ln_g", "post_ln_b")


def _gather_weights(w_in, conf_w_out, gdn_w_out, w_o, conf_dw_w, gdn_conv_w):
    c = lax.axis_index("c")
    sq = jnp.concatenate([conf_w_out, gdn_w_out, w_o], axis=0).astype(bf16)
    w_half = lax.dynamic_slice_in_dim(w_in.T.astype(bf16), c * (D // 2), D // 2, axis=1)
    sq_half = lax.dynamic_slice_in_dim(sq, c * (sq.shape[0] // 2), sq.shape[0] // 2, axis=0)
    small = jnp.concatenate([jnp.pad(conf_dw_w.reshape(-1), (0, 8 * D - KC * SQ_BLK)).reshape(8, D),
                             jnp.pad(gdn_conv_w.reshape(-1), (0, 5 * D)).reshape(8, D)], axis=0)
    ((w_mine, w_other), (sq_mine, sq_other)), (small_all,) = _gather_halves([w_half, sq_half], [small], "weight_gather")
    w_t = _join_halves(w_mine, w_other, axis=-1)
    sq4 = _join_halves(sq_mine, sq_other)
    sq_full = [sq4[:, n * SQ_BLK:(n + 1) * SQ_BLK].reshape(D, D) for n in range(3)]
    dw_full = small_all[:, 0:8].reshape(N_CHIPS, 8 * D)[:, :KC * SQ_BLK].reshape(N_CHIPS, KC, SQ_BLK)
    dw_full = dw_full.transpose(1, 0, 2).reshape(KC, D)
    cw_full = small_all[:, 8:11].reshape(N_CHIPS, KG, 3 * SQ_BLK).transpose(1, 0, 2).reshape(KG, 3 * D)
    return w_t, sq_full[0], sq_full[1], sq_full[2], dw_full, cw_full


def _w_in_rows(w_t, lo, hi):
    parts = []
    for j in range(N_CHIPS):
        a, b = max(lo, j * W_IN_BLK), min(hi, (j + 1) * W_IN_BLK)
        if a < b:
            parts.append(w_t[j, a - j * W_IN_BLK:b - j * W_IN_BLK])
    return parts[0] if len(parts) == 1 else jnp.concatenate(parts, axis=0)


def _w_in_by_chip(pieces):
    chips = []
    for j in range(N_CHIPS):
        lo, hi = j * W_IN_BLK, (j + 1) * W_IN_BLK
        parts = []
        for start, arr in pieces:
            a, b = max(lo, start), min(hi, start + arr.shape[0])
            if a < b:
                parts.append(arr[a - start:b - start])
        chips.append(jnp.concatenate(parts, axis=0))
    return jnp.stack(chips)


def _pair_sums(g_w, g_rest):
    c_arr = lax.axis_index("c").astype(jnp.int32).reshape(1)
    got_w, got_r = _sibling_merge([g_w, g_rest], "grad_sibling_halves", half_axes=[-1, -2])
    pair_w = _pair_sum(g_w, got_w, c_arr, "grad_pair_sum_w_in", bf16, -1)
    pair_r = _pair_sum(g_rest, got_r, c_arr, "grad_pair_sum_rest", f32, -2)
    return pair_w, pair_r


def _chip_sums(all_w, all_r):
    tot_w, tot_r = _sum_slots(all_w, "grad_chip_sum_w_in"), _sum_slots(all_r, "grad_chip_sum_rest")
    oth_w, oth_r = _sibling_merge([tot_w, tot_r], "grad_sibling_result")
    return _join_halves(tot_w, oth_w, axis=-1), _join_halves(tot_r, oth_r)


def _local_step(x2, tgt, w_t, wc_out, wg_out, wo_full, dw_full, cw_full, conf_dw_b, conf_ln_g, conf_ln_b,
                gdn_A_log, gdn_dt_bias, gdn_norm_g, post_ln_g, post_ln_b):
    t = x2.shape[0]
    tt = min(TOKEN_TILE, t)
    tm = min(MM_TILE, t)

    w_conv, w_qkv, w_gz = _w_in_rows(w_t, 0, 3 * D), _w_in_rows(w_t, 3 * D, 6 * D), _w_in_rows(w_t, 6 * D, 7 * D)
    w_gates = _w_in_rows(w_t, 7 * D + 2 * NH, W_IN_COLS)
    dw_pad = jnp.pad(dw_full, ((0, HALO_C - KC), (0, 0)))
    cw_pad = jnp.pad(cw_full, ((0, 8 - KG), (0, 0)))
    row = lambda v: v.reshape(1, D)
    alog_v = jnp.pad(gdn_A_log, (NH, HD - 2 * NH)).reshape(1, HD)
    dt_v = jnp.pad(gdn_dt_bias, (NH, HD - 2 * NH)).reshape(1, HD)
    ng_b = row(jnp.tile(gdn_norm_g, NH))
    w_ba = jnp.pad(_w_in_rows(w_t, 7 * D, 7 * D + 2 * NH), ((0, HD - 2 * NH), (0, 0)))

    x_b = x2.astype(bf16)

    p_conv = _mm_multi([x_b], [w_conv], out_dtype=f32, tm=tm, tn=MM_TILE, name="proj_conv", rhs_t=True)
    p_qkv = _mm_multi([x_b], [w_qkv], out_dtype=f32, tm=tm, tn=MM_TILE, name="proj_qkv", rhs_t=True)
    p_gz = _mm_multi([x_b], [w_gz], out_dtype=f32, tm=tm, tn=MM_TILE, name="proj_gz", rhs_t=True)
    p_gates = _mm_multi([x_b], [w_gates], out_dtype=f32, tm=tm, tn=MM_TILE, name="proj_gates", rhs_t=True)
    p_ba = _mm_multi([x_b], [w_ba], out_dtype=f32, tm=tm, tn=HD, name="proj_ba", rhs_t=True)

    u, a1, y_conf = _conv_fwd(p_conv, dw_pad, row(conf_dw_b), row(conf_ln_g), row(conf_ln_b), wc_out, tt=tt)

    qn, kn, va, gates = _gdn_pre_fwd(p_qkv, p_ba, cw_pad, alog_v, dt_v, tt=tt)
    o, states, inverses = _gdn_scan_fwd(qn, kn, va, gates, tt=tt)
    og, y_gdn = _gdn_post_fwd(o, p_gz, ng_b, wg_out, tt=tt)

    loss_blk, dxd, dyc, dyg, dp_gates, h, dz, dpost = _merge(
        x2, y_conf, y_gdn, p_gates, tgt, wo_full, row(post_ln_g), row(post_ln_b), tt=tt)

    d_wo = _mm_kloop(h, dz, tm=D, tn=MM_TILE, tk=min(MM_K_TILE, t), name="grad_w_o")
    du = _mm_multi([dyc], [wc_out], out_dtype=f32, tm=tm, tn=MM_TILE, name="conf_out_bwd", rhs_t=True)
    d_wc = _mm_kloop(u, dyc, tm=D, tn=MM_TILE, tk=min(MM_K_TILE, t), name="grad_conf_w_out")
    d_wg = _mm_kloop(og, dyg, tm=D, tn=MM_TILE, tk=min(MM_K_TILE, t), name="grad_gdn_w_out")

    dp_conv, d_dww, dconv_vec = _conv_bwd(p_conv, a1, du, dw_pad, row(conf_ln_g), row(conf_ln_b), tt=tt)

    do, dp_gz, dng = _gdn_post_bwd(o, p_gz, ng_b, wg_out, dyg, tt=tt)
    dqn, dkn, dva, dgates = _gdn_scan_bwd(qn, kn, va, gates, states, inverses, do, tt=tt)
    dp_qkv, dp_ba, d_cw, d_ad = _gdn_pre_bwd(p_qkv, p_ba, cw_pad, alog_v, dt_v, dqn, dkn, dva, dgates, tt=tt)
    dp_ba_b = dp_ba.astype(bf16)

    grad_x_factors = ([dp_conv, dp_qkv, dp_gz, dp_gates, dp_ba_b], [w_conv, w_qkv, w_gz, w_gates, w_ba], dxd)

    tk = min(MM_K_TILE, t)
    d_w_conv = _mm_kloop(dp_conv, x_b, tm=MM_TILE, tn=D, tk=tk, name="grad_w_in_conv")
    d_w_qkv = _mm_kloop(dp_qkv, x_b, tm=MM_TILE, tn=D, tk=tk, name="grad_w_in_qkv")
    d_w_gz = _mm_kloop(dp_gz, x_b, tm=MM_TILE, tn=D, tk=tk, name="grad_w_in_gz")
    d_w_gates = _mm_kloop(dp_gates, x_b, tm=MM_TILE, tn=D, tk=tk, name="grad_w_in_gates")
    d_w_ba = _mm_kloop(dp_ba_b, x_b, tm=HD, tn=D, tk=tk, name="grad_w_in_ba")
    d_w_in = _w_in_by_chip([(0, d_w_conv), (3 * D, d_w_qkv), (6 * D, d_w_gz), (7 * D, d_w_ba[:2 * NH]),
                            (7 * D + 2 * NH, d_w_gates)])

    return (loss_blk[0, 0], grad_x_factors, d_w_in, d_wc, d_wg, d_wo, d_dww, d_cw, dconv_vec, dpost, d_ad, dng)


def kernel(x, w_in, conf_dw_w, conf_dw_b, conf_ln_g, conf_ln_b, conf_w_out, gdn_conv_w, gdn_A_log, gdn_dt_bias, gdn_norm_g, gdn_w_out, w_o, post_ln_g, post_ln_b, loss_target, m_w_in, m_conf_dw_w, m_conf_dw_b, m_conf_ln_g, m_conf_ln_b, m_conf_w_out, m_gdn_conv_w, m_gdn_A_log, m_gdn_dt_bias, m_gdn_norm_g, m_gdn_w_out, m_w_o, m_post_ln_g, m_post_ln_b, v_w_in, v_conf_dw_w, v_conf_dw_b, v_conf_ln_g, v_conf_ln_b, v_conf_w_out, v_gdn_conv_w, v_gdn_A_log, v_gdn_dt_bias, v_gdn_norm_g, v_gdn_w_out, v_w_o, v_post_ln_g, v_post_ln_b):
    x2 = x.reshape(x.shape[-2], D)
    tgt = loss_target.reshape(x2.shape)
    w_t, wc_out, wg_out, wo_full, dw_full, cw_full = _gather_weights(
        w_in, conf_w_out, gdn_w_out, w_o, conf_dw_w, gdn_conv_w)
    (loss_part, grad_x_factors, d_w_in, d_wc, d_wg, d_wo, d_dww, d_cw, dconv_vec, dpost, d_ad, dng) = _local_step(
        x2, tgt, w_t, wc_out, wg_out, wo_full, dw_full, cw_full, conf_dw_b, conf_ln_g, conf_ln_b,
        gdn_A_log, gdn_dt_bias, gdn_norm_g, post_ln_g, post_ln_b)
    loss = lax.psum(loss_part, ("x", "y", "c"))

    dww_c = d_dww[:KC].reshape(KC, N_CHIPS, SQ_BLK)
    dcw_c = d_cw[:KG].reshape(KG, N_CHIPS, 3 * SQ_BLK)
    vecs = [dconv_vec[0], dconv_vec[1], dconv_vec[2], dpost[0], dpost[1]]
    g_rest = jnp.stack([
        _pack_rest(d_wc[j * SQ_BLK:(j + 1) * SQ_BLK], d_wg[j * SQ_BLK:(j + 1) * SQ_BLK], d_wo[j * SQ_BLK:(j + 1) * SQ_BLK],
                   _pack_small(dww_c[:, j], dcw_c[:, j], vecs, d_ad[0, NH:2 * NH], d_ad[1, NH:2 * NH], dng[0]))
        for j in range(N_CHIPS)])
    pair_w, pair_r = _pair_sums(d_w_in, g_rest)
    grad_x, all_w, all_r = _mm_multi(*grad_x_factors, out_dtype=f32, tm=min(GRAD_X_TILE[0], x2.shape[0]), tn=GRAD_X_TILE[1],
                                     name="grad_x_and_chip_scatter", scatter=[pair_w, pair_r])
    g_w_in, g_rest = _chip_sums(all_w, all_r)

    def rest_of(w_c, w_g, w_oo, dw, cw, b1, g1, b2, g2, b3, a_log, dt_bias, norm_g):
        return _pack_rest(w_c, w_g, w_oo, _pack_small(dw, cw, [b1, g1, b2, g2, b3], a_log, dt_bias, norm_g))

    w_r = rest_of(conf_w_out, gdn_w_out, w_o, conf_dw_w, gdn_conv_w, conf_dw_b, conf_ln_g, conf_ln_b,
                  post_ln_g, post_ln_b, gdn_A_log, gdn_dt_bias, gdn_norm_g)
    m_r = rest_of(m_conf_w_out, m_gdn_w_out, m_w_o, m_conf_dw_w, m_gdn_conv_w, m_conf_dw_b, m_conf_ln_g, m_conf_ln_b,
                  m_post_ln_g, m_post_ln_b, m_gdn_A_log, m_gdn_dt_bias, m_gdn_norm_g)
    v_r = rest_of(v_conf_w_out, v_gdn_w_out, v_w_o, v_conf_dw_w, v_gdn_conv_w, v_conf_dw_b, v_conf_ln_g, v_conf_ln_b,
                  v_post_ln_g, v_post_ln_b, v_gdn_A_log, v_gdn_dt_bias, v_gdn_norm_g)
    upd_w_in = _adamw(w_in.T, g_w_in, m_w_in.T, v_w_in.T, "adamw_w_in")
    upd_rest = _adamw(w_r, g_rest, m_r, v_r, "adamw_rest")

    out = [loss, grad_x.reshape(x.shape)]
    for big, rest in zip((g_w_in,) + tuple(upd_w_in), (g_rest,) + tuple(upd_rest)):
        d = dict(_unpack_rest(rest), w_in=big.T)
        out += [d[n] for n in _WEIGHT_ORDER]
    return tuple(out)
```

```python
import jax
import jax.numpy as jnp
from jax import lax
from jax.experimental import pallas as pl
from jax.experimental.pallas import tpu as pltpu

f32 = jnp.float32
bf16 = jnp.bfloat16
HI = lax.Precision.HIGHEST
MESH = pl.DeviceIdType.MESH

D = 1024
NH = 8
HD = 128
CH = 64
KC = 31
KG = 4
HALO_C = 32
HALO_G = 8
LANE = 128
STRIP = 32
N_SHIFT = 7
LN_EPS = 1e-5
RMS_EPS = 1e-6
L2_EPS = 1e-6
DN_ALPHA = 2.0 ** 0.25
N_CHIPS = 4
W_IN_COLS = 9232
W_IN_BLK = W_IN_COLS // N_CHIPS
SQ_BLK = D // N_CHIPS
VMEM_LIMIT = 52 * 1024 * 1024
MM_TILE = 1024
MM_K_TILE = 512
GRAD_X_TILE = (256, 512)
TOKEN_TILE = 256
SCAN_GROUP = 4

ADAM_LR = 0.001
ADAM_B1 = 0.9
ADAM_B2 = 0.999
ADAM_EPS = 1e-08
ADAM_WD = 0.01
ADAM_STEP = 10


def _sigmoid(x):
    return 1.0 / (1.0 + jnp.exp(-x))


def _silu_and_grad(x):
    s = _sigmoid(x)
    return x * s, s * (1.0 + x * (1.0 - s))


_NN = ((1,), (0,))
_NT = ((1,), (1,))
_TN = ((0,), (0,))


def _cparams(*sem):
    return pltpu.CompilerParams(dimension_semantics=sem, vmem_limit_bytes=VMEM_LIMIT)


def _mm_multi(a_list, b_list, addend=None, *, out_dtype, tm, tn, name, rhs_t=False, scatter=()):
    n_pairs = len(a_list)
    m = a_list[0].shape[0]
    n = b_list[0].shape[0 if rhs_t else 1]
    has_add = addend is not None
    dims = (_NT if rhs_t else _NN, ((), ()))
    n_in = 2 * n_pairs + has_add
    k = len(scatter)
    grid = (n // tn, m // tm)

    def body(*refs):
        a_refs = refs[:n_pairs]
        b_refs = refs[n_pairs:2 * n_pairs]
        o_ref = refs[n_in + k]
        if k:
            start, finish = _chip_exchange_ops(refs[n_in:n_in + k], refs[n_in + k + 1:n_in + 2 * k + 1],
                                               *refs[n_in + 2 * k + 1:], True)
            step = pl.program_id(0) * grid[1] + pl.program_id(1)
            pl.when(step == 0)(start)
        acc = None
        for a_ref, b_ref in zip(a_refs, b_refs):
            p = lax.dot_general(a_ref[...].astype(bf16), b_ref[...].astype(bf16), dims, preferred_element_type=f32)
            acc = p if acc is None else acc + p
        if has_add:
            acc = acc + refs[2 * n_pairs][...]
        o_ref[...] = acc.astype(out_dtype)
        if k:
            pl.when(step == grid[0] * grid[1] - 1)(finish)

    in_specs = [pl.BlockSpec((tm, a.shape[1]), lambda j, i: (i, 0)) for a in a_list]
    if rhs_t:
        in_specs += [pl.BlockSpec((tn, b.shape[1]), lambda j, i: (j, 0)) for b in b_list]
    else:
        in_specs += [pl.BlockSpec((b.shape[0], tn), lambda j, i: (0, j)) for b in b_list]
    args = list(a_list) + list(b_list)
    if has_add:
        in_specs.append(pl.BlockSpec((tm, tn), lambda j, i: (i, j)))
        args.append(addend)
    out = pl.pallas_call(
        body, name=name, grid=grid,
        in_specs=in_specs + _any_specs(k), out_specs=[pl.BlockSpec((tm, tn), lambda j, i: (i, j))] + _any_specs(k),
        out_shape=[jax.ShapeDtypeStruct((m, n), out_dtype)] + _chip_exchange_shapes(scatter, True),
        scratch_shapes=_chip_exchange_sems(k) if k else [],
        compiler_params=_cparams("arbitrary", "arbitrary") if k else _cparams("parallel", "parallel"),
    )(*args, *scatter)
    return out if k else out[0]


def _mm_kloop(a, b, *, tm, tn, tk, name):
    k, m = a.shape
    n = b.shape[1]
    nk = k // tk

    def body(a_ref, b_ref, o_ref):
        @pl.when(pl.program_id(2) == 0)
        def _():
            o_ref[...] = jnp.zeros_like(o_ref)
        o_ref[...] += lax.dot_general(a_ref[...].astype(bf16), b_ref[...].astype(bf16), (_TN, ((), ())),
                                      preferred_element_type=f32)

    return pl.pallas_call(
        body, name=name, grid=(n // tn, m // tm, nk),
        in_specs=[pl.BlockSpec((tk, tm), lambda j, i, kk: (kk, i)), pl.BlockSpec((tk, tn), lambda j, i, kk: (kk, j))],
        out_specs=pl.BlockSpec((tm, tn), lambda j, i, kk: (i, j)),
        out_shape=jax.ShapeDtypeStruct((m, n), f32),
        compiler_params=_cparams("parallel", "parallel", "arbitrary"),
    )(a, b)


def _shift_copies(src_ref, sh_ref, n, shifts=tuple(range(1, 8))):
    for i, b in enumerate(shifts):
        sh_ref[i, 0:n, :] = src_ref[pl.ds(b, n), :]


def _by_residue(offs):
    groups = {}
    for k, off in enumerate(offs):
        groups.setdefault(off % 8, []).append((k, off // 8))
    return groups


def _slab(src_ref, sh_ref, shifts, b, r0, n, lanes):
    ref = src_ref if b == 0 else sh_ref.at[shifts.index(b)]
    return ref[r0:r0 + n, lanes]


def _tap_conv(out_ref, n_rows, src_ref, sh_ref, w_ref, offs, bias_ref=None, shifts=tuple(range(1, 8))):
    groups = _by_residue(offs)
    for j in range(D // LANE):
        lanes = slice(j * LANE, (j + 1) * LANE)
        wv = [w_ref[k:k + 1, lanes] for k in range(len(offs))]
        for r0 in range(0, n_rows, STRIP):
            n = min(STRIP, n_rows - r0)
            accs = [jnp.zeros((n, LANE), f32) if bias_ref is None else jnp.broadcast_to(bias_ref[0:1, lanes], (n, LANE)),
                    jnp.zeros((n, LANE), f32)]
            m = 0
            for b, taps in groups.items():
                a_lo = min(a for _, a in taps)
                a_hi = max(a for _, a in taps)
                wide = _slab(src_ref, sh_ref, shifts, b, r0 + 8 * a_lo, 8 * (a_hi - a_lo) + n, lanes)
                for k, a in taps:
                    accs[m % 2] = accs[m % 2] + wv[k] * wide[8 * (a - a_lo):8 * (a - a_lo) + n]
                    m += 1
            out_ref[r0:r0 + n, lanes] = accs[0] + accs[1]


def _tap_corr(dw_ref, n_rows, lhs_ref, src_ref, sh_ref, offs, shifts=tuple(range(1, 8))):
    groups = _by_residue(offs)
    for j in range(D // LANE):
        lanes = slice(j * LANE, (j + 1) * LANE)
        accs = [jnp.zeros((8, LANE), f32) for _ in offs]
        for r0 in range(0, n_rows, STRIP):
            n = min(STRIP, n_rows - r0)
            d = lhs_ref[r0:r0 + n, lanes]
            for b, taps in groups.items():
                a_lo = min(a for _, a in taps)
                a_hi = max(a for _, a in taps)
                wide = _slab(src_ref, sh_ref, shifts, b, r0 + 8 * a_lo, 8 * (a_hi - a_lo) + n, lanes)
                for k, a in taps:
                    prod = d * wide[8 * (a - a_lo):8 * (a - a_lo) + n]
                    part = prod[0:8]
                    for q in range(1, n // 8):
                        part = part + prod[8 * q:8 * q + 8]
                    accs[k] = accs[k] + part
        for k in range(len(offs)):
            dw_ref[k:k + 1, lanes] += jnp.sum(accs[k], axis=0, keepdims=True)


_FWD_OFFS = [HALO_C - (KC - 1) + k for k in range(KC)]
_BWD_OFFS = [KC - 1 - k for k in range(KC)]


def _norm_act(a1, cz, g_ref, bb_ref):
    mu = jnp.mean(a1, axis=-1, keepdims=True)
    cen = a1 - mu
    var = jnp.mean(cen * cen, axis=-1, keepdims=True)
    rstd = lax.rsqrt(var + LN_EPS)
    xhat = cen * rstd
    ln = xhat * g_ref[...] + bb_ref[...]
    s, ds = _silu_and_grad(ln)
    zc, dzc = _silu_and_grad(cz)
    return xhat, rstd, s, ds, zc, dzc


def _conv_fwd(p_conv, dw_w, dw_b, ln_g, ln_b, w_out, *, tt):
    t = p_conv.shape[0]
    hb = tt // HALO_C

    def body(cv_ref, cg_ref, cz_ref, cvh_ref, cgh_ref, w_ref, b_ref, g_ref, bb_ref, wo_ref,
             u_ref, a1_ref, y_ref, ext_ref, sh_ref):
        first = pl.program_id(0) == 0
        halo = cvh_ref[...] * _sigmoid(cgh_ref[...])
        ext_ref[0:HALO_C, :] = jnp.where(first, 0.0, halo)
        ext_ref[HALO_C:, :] = cv_ref[...] * _sigmoid(cg_ref[...])
        _shift_copies(ext_ref, sh_ref, tt + HALO_C - 8)
        _tap_conv(a1_ref, tt, ext_ref, sh_ref, w_ref, _FWD_OFFS, b_ref)
        _, _, s, _, zc, _ = _norm_act(a1_ref[...], cz_ref[...], g_ref, bb_ref)
        u = (s * zc).astype(bf16)
        u_ref[...] = u
        y_ref[...] = jnp.dot(u, wo_ref[...], preferred_element_type=f32)

    def main(col):
        return pl.BlockSpec((tt, D), lambda i: (i, col))

    def prev(col):
        return pl.BlockSpec((HALO_C, D), lambda i: (jnp.maximum(i * hb - 1, 0), col))

    vec = pl.BlockSpec((1, D), lambda i: (0, 0))
    return pl.pallas_call(
        body, name="conv_fwd", grid=(t // tt,),
        in_specs=[main(0), main(1), main(2), prev(0), prev(1),
                  pl.BlockSpec((HALO_C, D), lambda i: (0, 0)), vec, vec, vec, pl.BlockSpec((D, D), lambda i: (0, 0))],
        out_specs=[pl.BlockSpec((tt, D), lambda i: (i, 0))] * 3,
        out_shape=[jax.ShapeDtypeStruct((t, D), bf16), jax.ShapeDtypeStruct((t, D), f32),
                   jax.ShapeDtypeStruct((t, D), f32)],
        scratch_shapes=[pltpu.VMEM((tt + HALO_C, D), f32), pltpu.VMEM((N_SHIFT, tt + HALO_C - 8, D), f32)],
        compiler_params=_cparams("parallel"),
    )(p_conv, p_conv, p_conv, p_conv, p_conv, dw_w, dw_b, ln_g, ln_b, w_out)


def _conv_bwd(p_conv, a1, du, dw_w, ln_g, ln_b, *, tt):
    t = p_conv.shape[0]
    hb = tt // HALO_C
    n_tiles = t // tt
    last_hb = t // HALO_C - 1
    ne = tt + HALO_C

    def body(cv_ref, cg_ref, cz_ref, a1_ref, du_ref, cvp_ref, cgp_ref, czn_ref, a1n_ref, dun_ref,
             w_ref, g_ref, bb_ref, dp_ref, dww_ref, dvec_ref, ext_ref, sh_ref, da1_ref, da0_ref):
        i = pl.program_id(0)
        first = i == 0
        last = i == n_tiles - 1

        @pl.when(first)
        def _():
            dww_ref[...] = jnp.zeros_like(dww_ref)
            dvec_ref[...] = jnp.zeros_like(dvec_ref)

        sig = _sigmoid(cg_ref[...])
        ext_ref[0:HALO_C, :] = jnp.where(first, 0.0, cvp_ref[...] * _sigmoid(cgp_ref[...]))
        ext_ref[HALO_C:, :] = cv_ref[...] * sig
        a1_all = jnp.concatenate([a1_ref[...], a1n_ref[...]], axis=0)
        cz = jnp.concatenate([cz_ref[...], czn_ref[...]], axis=0)
        du_all = jnp.concatenate([du_ref[...], jnp.where(last, 0.0, dun_ref[...])], axis=0)
        xhat, rstd, s, ds, zc, dzc = _norm_act(a1_all, cz, g_ref, bb_ref)
        dln = du_all * zc * ds
        dxhat = dln * g_ref[...]
        da1 = rstd * (dxhat - jnp.mean(dxhat, axis=-1, keepdims=True)
                      - xhat * jnp.mean(dxhat * xhat, axis=-1, keepdims=True))
        da1_ref[...] = da1
        dcz = (du_all * s * dzc)[:tt]
        dvec_ref[0:1, :] += jnp.sum(da1[:tt], axis=0, keepdims=True)
        dvec_ref[1:2, :] += jnp.sum((dln * xhat)[:tt], axis=0, keepdims=True)
        dvec_ref[2:3, :] += jnp.sum(dln[:tt], axis=0, keepdims=True)
        _shift_copies(ext_ref, sh_ref, ne - 8)
        _tap_corr(dww_ref, tt, da1_ref, ext_ref, sh_ref, _FWD_OFFS)
        _shift_copies(da1_ref, sh_ref, ne - 8)
        _tap_conv(da0_ref, tt, da1_ref, sh_ref, w_ref, _BWD_OFFS)
        da0 = da0_ref[...]
        cv = cv_ref[...]
        dp_ref[:, 0:D] = (da0 * sig).astype(bf16)
        dp_ref[:, D:2 * D] = (da0 * cv * sig * (1.0 - sig)).astype(bf16)
        dp_ref[:, 2 * D:] = dcz.astype(bf16)

    def main(col):
        return pl.BlockSpec((tt, D), lambda i: (i, col))

    def prev(col):
        return pl.BlockSpec((HALO_C, D), lambda i: (jnp.maximum(i * hb - 1, 0), col))

    def nxt(col):
        return pl.BlockSpec((HALO_C, D), lambda i: (jnp.minimum((i + 1) * hb, last_hb), col))

    vec = pl.BlockSpec((1, D), lambda i: (0, 0))
    return pl.pallas_call(
        body, name="conv_bwd", grid=(n_tiles,),
        in_specs=[main(0), main(1), main(2), main(0), main(0), prev(0), prev(1), nxt(2), nxt(0), nxt(0),
                  pl.BlockSpec((HALO_C, D), lambda i: (0, 0)), vec, vec],
        out_specs=[pl.BlockSpec((tt, 3 * D), lambda i: (i, 0)),
                   pl.BlockSpec((HALO_C, D), lambda i: (0, 0)),
                   pl.BlockSpec((8, D), lambda i: (0, 0))],
        out_shape=[jax.ShapeDtypeStruct((t, 3 * D), bf16), jax.ShapeDtypeStruct((HALO_C, D), f32),
                   jax.ShapeDtypeStruct((8, D), f32)],
        scratch_shapes=[pltpu.VMEM((ne, D), f32), pltpu.VMEM((N_SHIFT, ne - 8, D), f32),
                        pltpu.VMEM((ne, D), f32), pltpu.VMEM((tt, D), f32)],
        compiler_params=_cparams("arbitrary"),
    )(p_conv, p_conv, p_conv, a1, du, p_conv, p_conv, p_conv, a1, du, dw_w, ln_g, ln_b)


def _dot_hi(a, b):
    return lax.dot_general(a, b, (((1,), (0,)), ((), ())), precision=HI, preferred_element_type=f32)


def _chunk_tri(n, lower):
    r = lax.broadcasted_iota(jnp.int32, (n, n), 0)
    c = lax.broadcasted_iota(jnp.int32, (n, n), 1)
    tri = (r >= c) if lower else (r <= c)
    return jnp.where(tri & (r // CH == c // CH), 1.0, 0.0).astype(f32)


def _softplus_and_sigmoid(x):
    e = jnp.exp(-jnp.abs(x))
    log1p = jnp.where(e < 1e-2, e * (1.0 - e * (0.5 - e * (1.0 / 3.0 - 0.25 * e))), jnp.log(1.0 + e))
    return jnp.maximum(x, 0.0) + log1p, _sigmoid(x)


_G_FWD_OFFS = [HALO_G - (KG - 1) + k for k in range(KG)]
_G_FWD_SHIFTS = (5, 6, 7)
_G_BWD_OFFS = [KG - 1 - k for k in range(KG)]
_G_BWD_SHIFTS = (1, 2, 3)


def _gdn_short_conv(pre_ref, ext_ref, sh_ref, n_rows, w_ref):
    _shift_copies(ext_ref, sh_ref, n_rows, _G_FWD_SHIFTS)
    _tap_conv(pre_ref, n_rows, ext_ref, sh_ref, w_ref, _G_FWD_OFFS, shifts=_G_FWD_SHIFTS)
    return pre_ref[...]


def _l2norm_heads(act, scale):
    outs, rs = [], []
    for h in range(NH):
        a = act[:, h * HD:(h + 1) * HD]
        r = lax.rsqrt(jnp.sum(a * a, axis=-1, keepdims=True) + L2_EPS)
        outs.append(a * (r * scale))
        rs.append(jnp.broadcast_to(r, a.shape))
    return jnp.concatenate(outs, axis=-1), jnp.concatenate(rs, axis=-1)


def _gate_math(ba, al_ref, dt_ref):
    lane = lax.broadcasted_iota(jnp.int32, ba.shape, 1)
    is_b = lane < NH
    is_a = (lane >= NH) & (lane < 2 * NH)
    sp, sg = _softplus_and_sigmoid(ba + dt_ref[...])
    neg_a = -jnp.exp(al_ref[...])
    return is_b, is_a, _sigmoid(ba), neg_a * sp, sg, neg_a


def _gdn_pre_fwd(p_qkv, p_ba, cw, alog_v, dt_v, *, tt):
    t = p_qkv.shape[0]
    hb = tt // HALO_G

    def body(q_ref, k_ref, v_ref, qh_ref, kh_ref, vh_ref, ba_ref, wq_ref, wk_ref, wv_ref, al_ref, dt_ref,
             qn_ref, kn_ref, va_ref, gt_ref, pre_ref, ext_ref, sh_ref):
        first = pl.program_id(0) == 0

        def conv_act(x_ref, xh_ref, w_ref, col):
            ext_ref[0:HALO_G, :] = jnp.where(first, 0.0, xh_ref[...])
            ext_ref[HALO_G:, :] = x_ref[...]
            pre = _gdn_short_conv(pre_ref.at[:, col * D:(col + 1) * D], ext_ref, sh_ref, tt, w_ref)
            return pre * _sigmoid(pre)

        qn_ref[...] = _l2norm_heads(conv_act(q_ref, qh_ref, wq_ref, 0), HD ** -0.5)[0]
        kn_ref[...] = _l2norm_heads(conv_act(k_ref, kh_ref, wk_ref, 1), 1.0)[0]
        va_ref[...] = conv_act(v_ref, vh_ref, wv_ref, 2)
        is_b, is_a, beta, g, _, _ = _gate_math(ba_ref[...], al_ref, dt_ref)
        gc = _dot_hi(_chunk_tri(tt, lower=True), jnp.where(is_a, g, 0.0))
        gt_ref[...] = jnp.where(is_b, beta, gc)

    def main(col):
        return pl.BlockSpec((tt, D), lambda i: (i, col))

    def prev(col):
        return pl.BlockSpec((HALO_G, D), lambda i: (jnp.maximum(i * hb - 1, 0), col))

    def wspec(col):
        return pl.BlockSpec((8, D), lambda i: (0, col))

    vec = pl.BlockSpec((1, HD), lambda i: (0, 0))
    gblk = pl.BlockSpec((tt, HD), lambda i: (i, 0))
    sds = jax.ShapeDtypeStruct((t, D), f32)
    return pl.pallas_call(
        body, name="gdn_pre_fwd", grid=(t // tt,),
        in_specs=[main(0), main(1), main(2), prev(0), prev(1), prev(2), gblk, wspec(0), wspec(1), wspec(2), vec, vec],
        out_specs=[pl.BlockSpec((tt, D), lambda i: (i, 0))] * 3 + [gblk, pl.BlockSpec((tt, 3 * D), lambda i: (i, 0))],
        out_shape=[sds] * 3 + [jax.ShapeDtypeStruct((t, HD), f32), jax.ShapeDtypeStruct((t, 3 * D), f32)],
        scratch_shapes=[pltpu.VMEM((tt + HALO_G, D), f32), pltpu.VMEM((KG - 1, tt, D), f32)],
        compiler_params=_cparams("parallel"),
    )(p_qkv, p_qkv, p_qkv, p_qkv, p_qkv, p_qkv, p_ba, cw, cw, cw, alog_v, dt_v)


def _gdn_pre_bwd(p_qkv, p_ba, pre, cw, alog_v, dt_v, dqn, dkn, dva, dgt, *, tt):
    t = p_qkv.shape[0]
    hb = tt // HALO_G
    n_tiles = t // tt
    last_hb = t // HALO_G - 1
    ne = tt + HALO_G

    def body(q_ref, k_ref, v_ref, qp_ref, kp_ref, vp_ref, pq_ref, pk_ref, pv_ref, pqx_ref, pkx_ref, pvx_ref,
             dq_ref, dk_ref, dv_ref, dqx_ref, dkx_ref, dvx_ref, ba_ref, dgt_ref,
             wq_ref, wk_ref, wv_ref, al_ref, dt_ref,
             dp_ref, dba_ref, dcw_ref, dad_ref, ext_ref, sh_ref, dpre_ref, draw_ref):
        i = pl.program_id(0)
        first = i == 0
        last = i == n_tiles - 1

        @pl.when(first)
        def _():
            dcw_ref[...] = jnp.zeros_like(dcw_ref)
            dad_ref[...] = jnp.zeros_like(dad_ref)

        def one(x_ref, xp_ref, pre_ref, prex_ref, d_ref, dx_ref, w_ref, col, scale):
            ext_ref[0:HALO_G, :] = jnp.where(first, 0.0, xp_ref[...])
            ext_ref[HALO_G:, :] = x_ref[...]
            _shift_copies(ext_ref, sh_ref, tt, _G_FWD_SHIFTS)
            act, dact = _silu_and_grad(jnp.concatenate([pre_ref[...], prex_ref[...]], axis=0))
            d_out = jnp.concatenate([d_ref[...], jnp.where(last, 0.0, dx_ref[...])], axis=0)
            if scale is None:
                d_act = d_out
            else:
                parts = []
                for h in range(NH):
                    a = act[:, h * HD:(h + 1) * HD]
                    dn = d_out[:, h * HD:(h + 1) * HD]
                    r = lax.rsqrt(jnp.sum(a * a, axis=-1, keepdims=True) + L2_EPS)
                    parts.append(scale * r * (dn - a * (r * r) * jnp.sum(dn * a, axis=-1, keepdims=True)))
                d_act = jnp.concatenate(parts, axis=-1)
            dpre_ref[...] = d_act * dact
            _tap_corr(dcw_ref.at[:, col * D:(col + 1) * D], tt, dpre_ref, ext_ref, sh_ref, _G_FWD_OFFS, shifts=_G_FWD_SHIFTS)
            _shift_copies(dpre_ref, sh_ref, tt, _G_BWD_SHIFTS)
            _tap_conv(draw_ref, tt, dpre_ref, sh_ref, w_ref, _G_BWD_OFFS, shifts=_G_BWD_SHIFTS)
            dp_ref[:, col * D:(col + 1) * D] = draw_ref[...].astype(bf16)

        one(q_ref, qp_ref, pq_ref, pqx_ref, dq_ref, dqx_ref, wq_ref, 0, HD ** -0.5)
        one(k_ref, kp_ref, pk_ref, pkx_ref, dk_ref, dkx_ref, wk_ref, 1, 1.0)
        one(v_ref, vp_ref, pv_ref, pvx_ref, dv_ref, dvx_ref, wv_ref, 2, None)

        is_b, is_a, beta, g, sg, neg_a = _gate_math(ba_ref[...], al_ref, dt_ref)
        dgt_v = dgt_ref[...]
        dg = _dot_hi(_chunk_tri(tt, lower=False), jnp.where(is_a, dgt_v, 0.0))
        d_al = jnp.where(is_a, dg * neg_a * sg, 0.0)
        dba_ref[...] = jnp.where(is_b, dgt_v * beta * (1.0 - beta), d_al)
        dad_ref[0:1, :] += jnp.sum(jnp.where(is_a, dg * g, 0.0), axis=0, keepdims=True)
        dad_ref[1:2, :] += jnp.sum(d_al, axis=0, keepdims=True)

    def main(col):
        return pl.BlockSpec((tt, D), lambda i: (i, col))

    def prev(col):
        return pl.BlockSpec((HALO_G, D), lambda i: (jnp.maximum(i * hb - 1, 0), col))

    def nxt(col):
        return pl.BlockSpec((HALO_G, D), lambda i: (jnp.minimum((i + 1) * hb, last_hb), col))

    def wspec(col):
        return pl.BlockSpec((8, D), lambda i: (0, col))

    vec = pl.BlockSpec((1, HD), lambda i: (0, 0))
    gblk = pl.BlockSpec((tt, HD), lambda i: (i, 0))
    return pl.pallas_call(
        body, name="gdn_pre_bwd", grid=(n_tiles,),
        in_specs=[main(0), main(1), main(2), prev(0), prev(1), prev(2), main(0), main(1), main(2), nxt(0), nxt(1), nxt(2),
                  main(0), main(0), main(0), nxt(0), nxt(0), nxt(0), gblk, gblk,
                  wspec(0), wspec(1), wspec(2), vec, vec],
        out_specs=[pl.BlockSpec((tt, 3 * D), lambda i: (i, 0)), pl.BlockSpec((tt, HD), lambda i: (i, 0)),
                   pl.BlockSpec((8, 3 * D), lambda i: (0, 0)), pl.BlockSpec((8, HD), lambda i: (0, 0))],
        out_shape=[jax.ShapeDtypeStruct((t, 3 * D), bf16), jax.ShapeDtypeStruct((t, HD), f32),
                   jax.ShapeDtypeStruct((8, 3 * D), f32), jax.ShapeDtypeStruct((8, HD), f32)],
        scratch_shapes=[pltpu.VMEM((HALO_G + tt, D), f32), pltpu.VMEM((KG - 1, tt, D), f32),
                        pltpu.VMEM((ne, D), f32), pltpu.VMEM((tt, D), f32)],
        compiler_params=_cparams("arbitrary"),
    )(p_qkv, p_qkv, p_qkv, p_qkv, p_qkv, p_qkv, pre, pre, pre, pre, pre, pre,
      dqn, dkn, dva, dqn, dkn, dva, p_ba, dgt, cw, cw, cw, alog_v, dt_v)


def _dot_b(a, b, dims):
    return lax.dot_general(a.astype(bf16), b.astype(bf16), (dims, ((), ())), preferred_element_type=f32)


def _inverse_by_doubling(ms):
    heads = range(len(ms))
    r = lax.broadcasted_iota(jnp.int32, (CH, CH), 0)
    c = lax.broadcasted_iota(jnp.int32, (CH, CH), 1)
    eye = jnp.where(r == c, 1.0, 0.0).astype(f32)
    p = [eye + ms[h] for h in heads]
    mp = ms
    for _ in range(5):
        mp = [_dot_b(mp[h], mp[h], _NN) for h in heads]
        pm = [_dot_b(p[h], mp[h], _NN) for h in heads]
        p = [p[h] + pm[h] for h in heads]
    return tuple(p)


@jax.custom_vjp
def _known_inverse(ms, ps):
    return ps


def _known_inverse_fwd(ms, ps):
    return ps, ps


def _known_inverse_bwd(ps, cts):
    heads = range(len(ps))
    left = [_dot_b(ps[h], cts[h], _TN) for h in heads]
    return tuple(_dot_b(left[h], ps[h], _NT) for h in heads), tuple(jnp.zeros_like(p) for p in ps)


_known_inverse.defvjp(_known_inverse_fwd, _known_inverse_bwd)


def _chunk_prepare(qs, ks, vs, gcs, bbs, ps=None):
    heads = range(len(qs))
    r = lax.broadcasted_iota(jnp.int32, (CH, CH), 0)
    c = lax.broadcasted_iota(jnp.int32, (CH, CH), 1)
    causal = r >= c
    strict = r > c
    gc_row = [gcs[h].T[:CH, :] for h in heads]
    decay = [jnp.where(causal, jnp.exp(jnp.where(causal, gcs[h][:, :CH] - gc_row[h], 0.0)), 0.0) for h in heads]
    kb = [ks[h] * bbs[h] for h in heads]
    egc = [jnp.exp(gcs[h]) for h in heads]
    kk = [_dot_b(kb[h], ks[h], _NT) for h in heads]
    qk = [_dot_b(qs[h], ks[h], _NT) for h in heads]
    m = tuple(-jnp.where(strict, kk[h] * decay[h], 0.0) for h in heads)
    p = _inverse_by_doubling(m) if ps is None else _known_inverse(m, ps)
    u = [_dot_b(p[h], vs[h] * bbs[h], _NN) for h in heads]
    w = [_dot_b(p[h], kb[h] * egc[h], _NN) for h in heads]
    intra = [jnp.where(causal, qk[h] * decay[h], 0.0) for h in heads]
    g_last = [gcs[h][CH - 1:CH, :] for h in heads]
    k_dec = [ks[h] * jnp.exp(g_last[h] - gcs[h]) for h in heads]
    q_dec = [qs[h] * egc[h] for h in heads]
    e_last = [jnp.exp(g_last[h]) for h in heads]
    return u, w, intra, q_dec, k_dec, e_last, p


def _chunk_apply(u, w, intra, q_dec, k_dec, e_last, ss):
    heads = range(len(ss))
    ws = [_dot_b(w[h], ss[h], _NN) for h in heads]
    qs_s = [_dot_b(q_dec[h], ss[h], _NN) for h in heads]
    v_new = [u[h] - ws[h] for h in heads]
    iv = [_dot_b(intra[h], v_new[h], _NN) for h in heads]
    kv = [_dot_b(k_dec[h], v_new[h], _TN) for h in heads]
    o = tuple(qs_s[h] + iv[h] for h in heads)
    s_new = tuple(ss[h] * e_last[h] + kv[h] for h in heads)
    return o, s_new


def _chunk_group_fn(ins, ss, ps=None):
    n = len(ss)
    prep = _chunk_prepare(*(sum((tuple(c[i]) for c in ins), ()) for i in range(5)), ps=ps)
    outs, befores = [], []
    for g in range(len(ins)):
        befores.append(ss)
        o, ss = _chunk_apply(*(x[g * n:(g + 1) * n] for x in prep[:6]), ss)
        outs.append(o)
    return tuple(outs), tuple(befores), ss, prep[6]


def _head_cols():
    return [slice(h * HD, (h + 1) * HD) for h in range(NH)]


def _head_gates(gt):
    gcs = tuple(jnp.broadcast_to(gt[:, NH + h:NH + h + 1], (CH, HD)) for h in range(NH))
    bbs = tuple(jnp.broadcast_to(gt[:, h:h + 1], (CH, HD)) for h in range(NH))
    return gcs, bbs


def _gdn_scan_fwd(qn, kn, va, gates, *, tt):
    t = qn.shape[0]
    cpb = tt // CH
    group = min(SCAN_GROUP, cpb)

    def body(q_ref, k_ref, v_ref, gt_ref, o_ref, st_ref, p_ref, s_scr):
        @pl.when(pl.program_id(0) == 0)
        def _():
            s_scr[...] = jnp.zeros_like(s_scr)

        cols = _head_cols()

        def inputs(ci):
            rows = pl.ds(pl.multiple_of(ci * CH, CH), CH)
            gcs, bbs = _head_gates(gt_ref[rows, :])
            return tuple(tuple(ref[rows, cl] for cl in cols) for ref in (q_ref, k_ref, v_ref)) + (gcs, bbs)

        def step(gi, carry):
            chunks = [group * gi + g for g in range(group)]
            outs, befores, s_end, p = _chunk_group_fn([inputs(ci) for ci in chunks], tuple(s_scr[h] for h in range(NH)))
            for g, ci in enumerate(chunks):
                rows = pl.ds(pl.multiple_of(ci * CH, CH), CH)
                for h in range(NH):
                    st_ref[ci, h] = befores[g][h]
                    o_ref[rows, cols[h]] = outs[g][h]
                    p_ref[ci, h] = p[g * NH + h].astype(bf16)
            for h in range(NH):
                s_scr[h] = s_end[h]
            return carry

        lax.fori_loop(0, cpb // group, step, 0)

    blk = pl.BlockSpec((tt, D), lambda i: (i, 0))
    return pl.pallas_call(
        body, name="gdn_scan_fwd", grid=(t // tt,),
        in_specs=[blk] * 3 + [pl.BlockSpec((tt, HD), lambda i: (i, 0))],
        out_specs=[blk, pl.BlockSpec((cpb, NH, HD, HD), lambda i: (i, 0, 0, 0)),
                   pl.BlockSpec((cpb, NH, CH, CH), lambda i: (i, 0, 0, 0))],
        out_shape=[jax.ShapeDtypeStruct((t, D), f32), jax.ShapeDtypeStruct((t // CH, NH, HD, HD), f32),
                   jax.ShapeDtypeStruct((t // CH, NH, CH, CH), bf16)],
        scratch_shapes=[pltpu.VMEM((NH, HD, HD), f32)],
        compiler_params=_cparams("arbitrary"),
    )(qn, kn, va, gates)


def _gdn_scan_bwd(qn, kn, va, gates, states, inverses, do, *, tt):
    t = qn.shape[0]
    nblk = t // tt
    cpb = tt // CH

    def body(q_ref, k_ref, v_ref, gt_ref, st_ref, p_ref, do_ref, dq_ref, dk_ref, dv_ref, dgt_ref, ds_scr):
        @pl.when(pl.program_id(0) == 0)
        def _():
            ds_scr[...] = jnp.zeros_like(ds_scr)

        cols = _head_cols()

        def rows_of(ci):
            return pl.ds(pl.multiple_of(ci * CH, CH), CH)

        def inputs(ci):
            gcs, bbs = _head_gates(gt_ref[rows_of(ci), :])
            return tuple(tuple(ref[rows_of(ci), cl] for cl in cols) for ref in (q_ref, k_ref, v_ref)) + (gcs, bbs)

        def step(j, carry):
            ci = cpb - 1 - j
            ps = tuple(p_ref[ci, h].astype(f32) for h in range(NH))

            def one(ins, ss):
                outs, _, s_end, _ = _chunk_group_fn([ins], ss, ps=ps)
                return outs[0], s_end

            _, vjp = jax.vjp(one, inputs(ci), tuple(st_ref[ci, h] for h in range(NH)))
            grads, ds = vjp((tuple(do_ref[rows_of(ci), cl] for cl in cols), tuple(ds_scr[h] for h in range(NH))))
            lane = lax.broadcasted_iota(jnp.int32, (CH, HD), 1)
            dgt = jnp.zeros((CH, HD), f32)
            for h in range(NH):
                for ref, g in zip((dq_ref, dk_ref, dv_ref), grads[:3]):
                    ref[rows_of(ci), cols[h]] = g[h]
                dgt = dgt + jnp.where(lane == NH + h, jnp.sum(grads[3][h], axis=-1, keepdims=True), 0.0)
                dgt = dgt + jnp.where(lane == h, jnp.sum(grads[4][h], axis=-1, keepdims=True), 0.0)
                ds_scr[h] = ds[h]
            dgt_ref[rows_of(ci), :] = dgt
            return carry

        lax.fori_loop(0, cpb, step, 0)

    blk = pl.BlockSpec((tt, D), lambda i: (nblk - 1 - i, 0))
    sblk = pl.BlockSpec((cpb, NH, HD, HD), lambda i: (nblk - 1 - i, 0, 0, 0))
    sds = jax.ShapeDtypeStruct((t, D), f32)
    gblk = pl.BlockSpec((tt, HD), lambda i: (nblk - 1 - i, 0))
    pblk = pl.BlockSpec((cpb, NH, CH, CH), lambda i: (nblk - 1 - i, 0, 0, 0))
    return pl.pallas_call(
        body, name="gdn_scan_bwd", grid=(nblk,),
        in_specs=[blk] * 3 + [gblk, sblk, pblk, blk],
        out_specs=[blk] * 3 + [gblk], out_shape=[sds] * 3 + [jax.ShapeDtypeStruct((t, HD), f32)],
        scratch_shapes=[pltpu.VMEM((NH, HD, HD), f32)],
        compiler_params=_cparams("arbitrary"),
    )(qn, kn, va, gates, states, inverses, do)


def _rms_heads(o):
    ons, rs = [], []
    for h in range(NH):
        a = o[:, h * HD:(h + 1) * HD]
        r = lax.rsqrt(jnp.mean(a * a, axis=-1, keepdims=True) + RMS_EPS)
        ons.append(a * r)
        rs.append(jnp.broadcast_to(r, a.shape))
    return jnp.concatenate(ons, axis=-1), jnp.concatenate(rs, axis=-1)


def _gdn_post_fwd(o, p_gz, ng_b, w_out, *, tt):
    t = o.shape[0]

    def body(o_ref, gz_ref, ng_ref, w_ref, og_ref, y_ref):
        on, _ = _rms_heads(o_ref[...])
        z, _ = _silu_and_grad(gz_ref[...])
        og = (on * ng_ref[...] * z).astype(bf16)
        og_ref[...] = og
        y_ref[...] = jnp.dot(og, w_ref[...], preferred_element_type=f32)

    blk = pl.BlockSpec((tt, D), lambda i: (i, 0))
    return pl.pallas_call(
        body, name="gdn_post_fwd", grid=(t // tt,),
        in_specs=[blk, blk, pl.BlockSpec((1, D), lambda i: (0, 0)), pl.BlockSpec((D, D), lambda i: (0, 0))],
        out_specs=[blk, blk], out_shape=[jax.ShapeDtypeStruct((t, D), bf16), jax.ShapeDtypeStruct((t, D), f32)],
        compiler_params=_cparams("parallel"),
    )(o, p_gz, ng_b, w_out)


def _gdn_post_bwd(o, p_gz, ng_b, w_out, dyg, *, tt):
    t = o.shape[0]

    def body(o_ref, gz_ref, ng_ref, w_ref, dyg_ref, do_ref, dgz_ref, dng_ref):
        @pl.when(pl.program_id(0) == 0)
        def _():
            dng_ref[...] = jnp.zeros_like(dng_ref)

        on, r = _rms_heads(o_ref[...])
        z, dz = _silu_and_grad(gz_ref[...])
        dog_v = lax.dot_general(dyg_ref[...], w_ref[...], (_NT, ((), ())), preferred_element_type=f32)
        ng = ng_ref[...]
        dgz_ref[...] = (dog_v * on * ng * dz).astype(bf16)
        dy = dog_v * z
        dng_all = jnp.sum(dy * on, axis=0, keepdims=True)
        dng = dng_all[:, 0:HD]
        for h in range(1, NH):
            dng = dng + dng_all[:, h * HD:(h + 1) * HD]
        dng_ref[0:1, :] += dng
        don = dy * ng
        prod = don * on
        parts = []
        for h in range(NH):
            sl = slice(h * HD, (h + 1) * HD)
            parts.append(don[:, sl] - on[:, sl] * jnp.mean(prod[:, sl], axis=-1, keepdims=True))
        do_ref[...] = r * jnp.concatenate(parts, axis=-1)

    blk = pl.BlockSpec((tt, D), lambda i: (i, 0))
    return pl.pallas_call(
        body, name="gdn_post_bwd", grid=(t // tt,),
        in_specs=[blk, blk, pl.BlockSpec((1, D), lambda i: (0, 0)), pl.BlockSpec((D, D), lambda i: (0, 0)), blk],
        out_specs=[blk, blk, pl.BlockSpec((8, HD), lambda i: (0, 0))],
        out_shape=[jax.ShapeDtypeStruct((t, D), f32), jax.ShapeDtypeStruct((t, D), bf16),
                   jax.ShapeDtypeStruct((8, HD), f32)],
        compiler_params=_cparams("arbitrary"),
    )(o, p_gz, ng_b, w_out, dyg)


def _merge(x, y_conf, y_gdn, p_gates, target, w_o, ln_g, ln_b, *, tt):
    t = x.shape[0]

    def body(x_ref, yc_ref, yg_ref, gc_ref, gg_ref, tg_ref, w_ref, g_ref, b_ref,
             loss_ref, dxd_ref, dyc_ref, dyg_ref, dpg_ref, h_ref, dz_ref, dvec_ref):
        @pl.when(pl.program_id(0) == 0)
        def _():
            loss_ref[...] = jnp.zeros_like(loss_ref)
            dvec_ref[...] = jnp.zeros_like(dvec_ref)

        sc = _sigmoid(gc_ref[...])
        sg = _sigmoid(gg_ref[...])
        yc = yc_ref[...]
        yg = yg_ref[...]
        h = (sc * yc + sg * yg).astype(bf16)
        h_ref[...] = h
        z = DN_ALPHA * x_ref[...] + jnp.dot(h, w_ref[...], preferred_element_type=f32)
        mu = jnp.mean(z, axis=-1, keepdims=True)
        cen = z - mu
        rstd = lax.rsqrt(jnp.mean(cen * cen, axis=-1, keepdims=True) + LN_EPS)
        xhat = cen * rstd
        err = xhat * g_ref[...] + b_ref[...] - tg_ref[...]
        loss_ref[...] += 0.5 / D * jnp.sum(err * err)
        dy = err * (1.0 / D)
        dvec_ref[0:1, :] += jnp.sum(dy * xhat, axis=0, keepdims=True)
        dvec_ref[1:2, :] += jnp.sum(dy, axis=0, keepdims=True)
        dxhat = dy * g_ref[...]
        dz = rstd * (dxhat - jnp.mean(dxhat, axis=-1, keepdims=True)
                     - xhat * jnp.mean(dxhat * xhat, axis=-1, keepdims=True))
        dxd_ref[...] = DN_ALPHA * dz
        dz_b = dz.astype(bf16)
        dz_ref[...] = dz_b
        dh = lax.dot_general(dz_b, w_ref[...], (_NT, ((), ())), preferred_element_type=f32)
        dyc_ref[...] = (dh * sc).astype(bf16)
        dyg_ref[...] = (dh * sg).astype(bf16)
        dpg_ref[:, 0:D] = (dh * yc * sc * (1.0 - sc)).astype(bf16)
        dpg_ref[:, D:] = (dh * yg * sg * (1.0 - sg)).astype(bf16)

    blk = pl.BlockSpec((tt, D), lambda i: (i, 0))
    wblk = pl.BlockSpec((D, D), lambda i: (0, 0))
    vec = pl.BlockSpec((1, D), lambda i: (0, 0))
    return pl.pallas_call(
        body, name="merge_norm_loss", grid=(t // tt,),
        in_specs=[blk, blk, blk, pl.BlockSpec((tt, D), lambda i: (i, 0)), pl.BlockSpec((tt, D), lambda i: (i, 1)),
                  blk, wblk, vec, vec],
        out_specs=[pl.BlockSpec((8, HD), lambda i: (0, 0)), blk, blk, blk,
                   pl.BlockSpec((tt, 2 * D), lambda i: (i, 0)), blk, blk, pl.BlockSpec((8, D), lambda i: (0, 0))],
        out_shape=[jax.ShapeDtypeStruct((8, HD), f32), jax.ShapeDtypeStruct((t, D), f32),
                   jax.ShapeDtypeStruct((t, D), bf16), jax.ShapeDtypeStruct((t, D), bf16),
                   jax.ShapeDtypeStruct((t, 2 * D), bf16), jax.ShapeDtypeStruct((t, D), bf16),
                   jax.ShapeDtypeStruct((t, D), bf16), jax.ShapeDtypeStruct((8, D), f32)],
        compiler_params=_cparams("arbitrary"),
    )(x, y_conf, y_gdn, p_gates, p_gates, target, w_o, ln_g, ln_b)


def _place():
    return lax.axis_index("x"), lax.axis_index("y"), lax.axis_index("c")


def _any_specs(n):
    return [pl.BlockSpec(memory_space=pl.ANY)] * n


def _sibling_merge(arrs, name, half_axes=None):
    k = len(arrs)

    def half_shape(a, ax):
        shape = list(a.shape)
        shape[ax] //= 2
        return tuple(shape)

    def body(*refs):
        a_refs, o_refs = refs[:k], refs[k:2 * k]
        send_sems, recv_sems = refs[2 * k:]
        x, y, c = _place()
        sends = []
        for i in range(k):
            src = a_refs[i]
            if half_axes is not None:
                n = a_refs[i].shape[half_axes[i]] // 2
                idx = [slice(None)] * len(a_refs[i].shape)
                idx[half_axes[i]] = pl.ds((1 - c) * n, n)
                src = a_refs[i].at[tuple(idx)]
            cp = pltpu.make_async_remote_copy(src_ref=src, dst_ref=o_refs[i], send_sem=send_sems.at[i],
                                              recv_sem=recv_sems.at[i], device_id=(x, y, 1 - c), device_id_type=MESH)
            cp.start()
            sends.append(cp)
        for cp in sends:
            cp.wait()

    shapes = [a.shape if half_axes is None else half_shape(a, ax) for a, ax in zip(arrs, half_axes or arrs)]
    return pl.pallas_call(
        body, name=name, in_specs=_any_specs(k), out_specs=_any_specs(k),
        out_shape=[jax.ShapeDtypeStruct(sh, a.dtype) for sh, a in zip(shapes, arrs)],
        scratch_shapes=[pltpu.SemaphoreType.DMA((k,)), pltpu.SemaphoreType.DMA((k,))],
    )(*arrs)


def _join_halves(mine, other, axis=-2):
    c = lax.axis_index("c")
    return jnp.concatenate([jnp.where(c == 0, mine, other), jnp.where(c == 0, other, mine)], axis=axis)


def _chip_exchange_ops(a_refs, o_refs, send_sems, recv_sems, local_sems, scatter):
    k = len(a_refs)
    x, y, c = _place()
    me = 2 * x + y
    peers = [(1 - x, y), (x, 1 - y), (1 - x, 1 - y)]

    def src(i, j):
        return a_refs[i].at[j] if scatter else a_refs[i]

    def copy(i, n, send_j, slot):
        px, py = peers[n]
        return pltpu.make_async_remote_copy(
            src_ref=src(i, send_j), dst_ref=o_refs[i].at[slot], send_sem=send_sems.at[3 * i + n],
            recv_sem=recv_sems.at[3 * i + n], device_id=(px, py, c), device_id_type=MESH)

    def owns():
        return [pltpu.make_async_copy(src(i, me), o_refs[i].at[me], local_sems.at[i]) for i in range(k)]

    def sends():
        return [copy(i, n, 2 * peers[n][0] + peers[n][1], me) for n in range(3) for i in range(k)]

    def start():
        for cp in owns() + sends():
            cp.start()

    def finish():
        for n in range(3):
            for i in range(k):
                copy(i, n, me, 2 * peers[n][0] + peers[n][1]).wait_recv()
        for cp in sends():
            cp.wait_send()
        for cp in owns():
            cp.wait()

    return start, finish


def _chip_exchange_shapes(arrs, scatter):
    return [jax.ShapeDtypeStruct((N_CHIPS,) + tuple(a.shape[1:] if scatter else a.shape), a.dtype) for a in arrs]


def _chip_exchange_sems(k):
    return [pltpu.SemaphoreType.DMA((3 * k,)), pltpu.SemaphoreType.DMA((3 * k,)), pltpu.SemaphoreType.DMA((k,))]


def _gather_halves(halves, wholes, name):
    kh, kw = len(halves), len(wholes)
    k = kh + kw

    def body(*refs):
        a_refs, got_refs, oth_refs = refs[:k], refs[k:2 * k], refs[2 * k:2 * k + kh]
        send_sems, recv_sems, local_sems, fwd_send_sems, fwd_recv_sems = refs[2 * k + kh:]
        x, y, c = _place()
        me = 2 * x + y
        start, _ = _chip_exchange_ops(a_refs, got_refs, send_sems, recv_sems, local_sems, False)
        slots = [me] + [2 * px + py for px, py in [(1 - x, y), (x, 1 - y), (1 - x, 1 - y)]]

        def forward(i, r):
            src = a_refs[i] if r == 0 else got_refs[i].at[slots[r]]
            return pltpu.make_async_remote_copy(
                src_ref=src, dst_ref=oth_refs[i].at[slots[r]], send_sem=fwd_send_sems.at[4 * i + r],
                recv_sem=fwd_recv_sems.at[4 * i + r], device_id=(x, y, 1 - c), device_id_type=MESH)

        def arrival(i, n):
            px, py = [(1 - x, y), (x, 1 - y), (1 - x, 1 - y)][n]
            return pltpu.make_async_remote_copy(
                src_ref=a_refs[i], dst_ref=got_refs[i].at[slots[n + 1]], send_sem=send_sems.at[3 * i + n],
                recv_sem=recv_sems.at[3 * i + n], device_id=(px, py, c), device_id_type=MESH)

        start()
        for i in range(kh):
            forward(i, 0).start()
        for n in range(3):
            for i in range(k):
                arrival(i, n).wait_recv()
                if i < kh:
                    forward(i, n + 1).start()
        for i in range(kh):
            for r in range(4):
                forward(i, r).wait()
        for n in range(3):
            for i in range(k):
                arrival(i, n).wait_send()
        for i in range(k):
            pltpu.make_async_copy(a_refs[i], got_refs[i].at[me], local_sems.at[i]).wait()

    arrs = list(halves) + list(wholes)
    shapes = _chip_exchange_shapes(arrs, False)
    out = pl.pallas_call(
        body, name=name, in_specs=_any_specs(k), out_specs=_any_specs(k + kh),
        out_shape=shapes + shapes[:kh],
        scratch_shapes=_chip_exchange_sems(k) + [pltpu.SemaphoreType.DMA((4 * kh,)), pltpu.SemaphoreType.DMA((4 * kh,))],
    )(*arrs)
    return [(out[i], out[k + i]) for i in range(kh)], out[kh:k]


def _pair_sum(g_all, got, c_arr, name, out_dtype, axis):
    rows, w = got.shape[1:]
    if axis == -2:
        blk, steps = (1, rows // 4, w), 4
        pick = lambda j, i, c_ref: (j, c_ref[0] * steps + i, 0)
        mine = lambda j, i, c_ref: (j, i, 0)
    else:
        blk, steps = (1, rows, LANE), w // LANE
        pick = lambda j, i, c_ref: (j, 0, c_ref[0] * steps + i)
        mine = lambda j, i, c_ref: (j, 0, i)

    def body(c_ref, a_ref, b_ref, o_ref):
        o_ref[...] = (a_ref[...] + b_ref[...]).astype(out_dtype)

    return pl.pallas_call(
        body, name=name,
        grid_spec=pltpu.PrefetchScalarGridSpec(
            num_scalar_prefetch=1, grid=(N_CHIPS, steps),
            in_specs=[pl.BlockSpec(blk, pick), pl.BlockSpec(blk, mine)], out_specs=pl.BlockSpec(blk, mine)),
        out_shape=jax.ShapeDtypeStruct(got.shape, out_dtype),
        compiler_params=_cparams("parallel", "parallel"),
    )(c_arr, g_all, got)


def _sum_slots(a, name):
    n, w = a.shape[1:]
    by_rows = n % 64 == 0
    in_blk = pl.BlockSpec((N_CHIPS, n // 4, w), lambda i: (0, i, 0)) if by_rows else pl.BlockSpec((N_CHIPS, n, LANE), lambda i: (0, 0, i))
    out_blk = pl.BlockSpec((n // 4, w), lambda i: (i, 0)) if by_rows else pl.BlockSpec((n, LANE), lambda i: (0, i))

    def body(a_ref, o_ref):
        o_ref[...] = ((a_ref[0].astype(f32) + a_ref[1].astype(f32)) + a_ref[2].astype(f32)) + a_ref[3].astype(f32)

    return pl.pallas_call(
        body, name=name, grid=(4 if by_rows else w // LANE,),
        in_specs=[in_blk], out_specs=out_blk,
        out_shape=jax.ShapeDtypeStruct((n, w), f32),
        compiler_params=_cparams("parallel"),
    )(a)


def _adamw(w, g, m, v, name):
    rows, width = w.shape
    by_rows = rows % 64 == 0
    c1 = 1.0 / (1.0 - ADAM_B1 ** ADAM_STEP)
    c2 = 1.0 / (1.0 - ADAM_B2 ** ADAM_STEP)

    def body(w_ref, g_ref, m_ref, v_ref, d_ref, mo_ref, vo_ref):
        g_v = g_ref[...]
        m_new = ADAM_B1 * m_ref[...] + (1.0 - ADAM_B1) * g_v
        v_new = ADAM_B2 * v_ref[...] + (1.0 - ADAM_B2) * (g_v * g_v)
        mo_ref[...] = m_new
        vo_ref[...] = v_new
        d_ref[...] = -ADAM_LR * ((m_new * c1) / (jnp.sqrt(v_new * c2) + ADAM_EPS) + ADAM_WD * w_ref[...])

    blk = pl.BlockSpec((rows // 8, width), lambda i: (i, 0)) if by_rows else pl.BlockSpec((rows, LANE), lambda i: (0, i))
    sds = jax.ShapeDtypeStruct((rows, width), f32)
    return pl.pallas_call(
        body, name=name, grid=(8 if by_rows else width // LANE,),
        in_specs=[blk] * 4, out_specs=[blk] * 3, out_shape=[sds] * 3,
        compiler_params=_cparams("parallel"),
    )(w, g, m, v)


R_DW = 3 * SQ_BLK
R_CW = R_DW + 8
R_VEC = R_CW + 8
R_SMALL = R_VEC + 8
REST_ROWS = 896


def _pack_small(conf_dw_w, gdn_conv_w, vecs, a_log, dt_bias, norm_g):
    dw = jnp.pad(conf_dw_w.reshape(-1), (0, 8 * D - KC * SQ_BLK)).reshape(8, D)
    cw = jnp.pad(gdn_conv_w.reshape(-1), (0, 5 * D)).reshape(8, D)
    vec = jnp.pad(jnp.stack(vecs), ((0, 3), (0, 0)))
    small = jnp.pad(jnp.concatenate([a_log, dt_bias, norm_g]), (0, D - 2 * NH - HD)).reshape(1, D)
    return jnp.pad(jnp.concatenate([dw, cw, vec, small], axis=0), ((0, REST_ROWS - R_SMALL - 1), (0, 0)))


def _pack_rest(conf_w_out, gdn_w_out, w_o, small):
    return jnp.concatenate([conf_w_out, gdn_w_out, w_o, small], axis=0)


def _unpack_rest(p):
    conf_dw_w = p[R_DW:R_DW + 8].reshape(-1)[:KC * SQ_BLK].reshape(KC, SQ_BLK)
    gdn_conv_w = p[R_CW:R_CW + 3].reshape(KG, 3 * SQ_BLK)
    small = p[R_SMALL]
    return dict(conf_w_out=p[0:SQ_BLK], gdn_w_out=p[SQ_BLK:2 * SQ_BLK], w_o=p[2 * SQ_BLK:R_DW],
                conf_dw_w=conf_dw_w, gdn_conv_w=gdn_conv_w, conf_dw_b=p[R_VEC], conf_ln_g=p[R_VEC + 1],
                conf_ln_b=p[R_VEC + 2], post_ln_g=p[R_VEC + 3], post_ln_b=p[R_VEC + 4],
                gdn_A_log=small[0:NH], gdn_dt_bias=small[NH:2 * NH], gdn_norm_g=small[2 * NH:2 * NH + HD])


_WEIGHT_ORDER = ("w_in", "conf_dw_w", "conf_dw_b", "conf_ln_g", "conf_ln_b", "conf_w_out", "gdn_conv_w",
                 "gdn_A_log", "gdn_dt_bias", "gdn_norm_g", "gdn_w_out", "w_o", "post_ln_g", "post_ln_b")


def _gather_weights(w_in, conf_w_out, gdn_w_out, w_o, conf_dw_w, gdn_conv_w):
    c = lax.axis_index("c")
    sq = jnp.concatenate([conf_w_out, gdn_w_out, w_o], axis=0).astype(bf16)
    w_half = lax.dynamic_slice_in_dim(w_in.T.astype(bf16), c * (D // 2), D // 2, axis=1)
    sq_half = lax.dynamic_slice_in_dim(sq, c * (sq.shape[0] // 2), sq.shape[0] // 2, axis=0)
    small = jnp.concatenate([jnp.pad(conf_dw_w.reshape(-1), (0, 8 * D - KC * SQ_BLK)).reshape(8, D),
                             jnp.pad(gdn_conv_w.reshape(-1), (0, 5 * D)).reshape(8, D)], axis=0)
    ((w_mine, w_other), (sq_mine, sq_other)), (small_all,) = _gather_halves([w_half, sq_half], [small], "weight_gather")
    w_t = _join_halves(w_mine, w_other, axis=-1).reshape(W_IN_COLS, D)
    sq4 = _join_halves(sq_mine, sq_other)
    sq_full = [sq4[:, n * SQ_BLK:(n + 1) * SQ_BLK].reshape(D, D) for n in range(3)]
    dw_full = small_all[:, 0:8].reshape(N_CHIPS, 8 * D)[:, :KC * SQ_BLK].reshape(N_CHIPS, KC, SQ_BLK)
    dw_full = dw_full.transpose(1, 0, 2).reshape(KC, D)
    cw_full = small_all[:, 8:11].reshape(N_CHIPS, KG, 3 * SQ_BLK).transpose(1, 0, 2).reshape(KG, 3 * D)
    return w_t, sq_full[0], sq_full[1], sq_full[2], dw_full, cw_full


def _pair_sums(g_w, g_rest):
    c_arr = lax.axis_index("c").astype(jnp.int32).reshape(1)
    got_w, got_r = _sibling_merge([g_w, g_rest], "grad_sibling_halves", half_axes=[-1, -2])
    pair_w = _pair_sum(g_w, got_w, c_arr, "grad_pair_sum_w_in", bf16, -1)
    pair_r = _pair_sum(g_rest, got_r, c_arr, "grad_pair_sum_rest", f32, -2)
    return pair_w, pair_r


def _chip_sums(all_w, all_r):
    tot_w, tot_r = _sum_slots(all_w, "grad_chip_sum_w_in"), _sum_slots(all_r, "grad_chip_sum_rest")
    oth_w, oth_r = _sibling_merge([tot_w, tot_r], "grad_sibling_result")
    return _join_halves(tot_w, oth_w, axis=-1), _join_halves(tot_r, oth_r)


def _local_step(x2, tgt, w_t, wc_out, wg_out, wo_full, dw_full, cw_full, conf_dw_b, conf_ln_g, conf_ln_b,
                gdn_A_log, gdn_dt_bias, gdn_norm_g, post_ln_g, post_ln_b):
    t = x2.shape[0]
    tt = min(TOKEN_TILE, t)
    tm = min(MM_TILE, t)

    w_conv, w_qkv, w_gz = w_t[0:3 * D], w_t[3 * D:6 * D], w_t[6 * D:7 * D]
    w_gates = w_t[7 * D + 2 * NH:]
    dw_pad = jnp.pad(dw_full, ((0, HALO_C - KC), (0, 0)))
    cw_pad = jnp.pad(cw_full, ((0, 8 - KG), (0, 0)))
    row = lambda v: v.reshape(1, D)
    alog_v = jnp.pad(gdn_A_log, (NH, HD - 2 * NH)).reshape(1, HD)
    dt_v = jnp.pad(gdn_dt_bias, (NH, HD - 2 * NH)).reshape(1, HD)
    ng_b = row(jnp.tile(gdn_norm_g, NH))
    w_ba = jnp.pad(w_t[7 * D:7 * D + 2 * NH], ((0, HD - 2 * NH), (0, 0)))

    x_b = x2.astype(bf16)

    p_conv = _mm_multi([x_b], [w_conv], out_dtype=f32, tm=tm, tn=MM_TILE, name="proj_conv", rhs_t=True)
    p_qkv = _mm_multi([x_b], [w_qkv], out_dtype=f32, tm=tm, tn=MM_TILE, name="proj_qkv", rhs_t=True)
    p_gz = _mm_multi([x_b], [w_gz], out_dtype=f32, tm=tm, tn=MM_TILE, name="proj_gz", rhs_t=True)
    p_gates = _mm_multi([x_b], [w_gates], out_dtype=f32, tm=tm, tn=MM_TILE, name="proj_gates", rhs_t=True)
    p_ba = _mm_multi([x_b], [w_ba], out_dtype=f32, tm=tm, tn=HD, name="proj_ba", rhs_t=True)

    u, a1, y_conf = _conv_fwd(p_conv, dw_pad, row(conf_dw_b), row(conf_ln_g), row(conf_ln_b), wc_out, tt=tt)

    qn, kn, va, gates, pre_qkv = _gdn_pre_fwd(p_qkv, p_ba, cw_pad, alog_v, dt_v, tt=tt)
    o, states, inverses = _gdn_scan_fwd(qn, kn, va, gates, tt=tt)
    og, y_gdn = _gdn_post_fwd(o, p_gz, ng_b, wg_out, tt=min(2 * TOKEN_TILE, t))

    loss_blk, dxd, dyc, dyg, dp_gates, h, dz, dpost = _merge(
        x2, y_conf, y_gdn, p_gates, tgt, wo_full, row(post_ln_g), row(post_ln_b), tt=tt)

    d_wo = _mm_kloop(h, dz, tm=D, tn=MM_TILE, tk=min(MM_K_TILE, t), name="grad_w_o")
    du = _mm_multi([dyc], [wc_out], out_dtype=f32, tm=tm, tn=MM_TILE, name="conf_out_bwd", rhs_t=True)
    d_wc = _mm_kloop(u, dyc, tm=D, tn=MM_TILE, tk=min(MM_K_TILE, t), name="grad_conf_w_out")
    d_wg = _mm_kloop(og, dyg, tm=D, tn=MM_TILE, tk=min(MM_K_TILE, t), name="grad_gdn_w_out")

    dp_conv, d_dww, dconv_vec = _conv_bwd(p_conv, a1, du, dw_pad, row(conf_ln_g), row(conf_ln_b), tt=tt)

    do, dp_gz, dng = _gdn_post_bwd(o, p_gz, ng_b, wg_out, dyg, tt=min(2 * TOKEN_TILE, t))
    dqn, dkn, dva, dgates = _gdn_scan_bwd(qn, kn, va, gates, states, inverses, do, tt=tt)
    dp_qkv, dp_ba, d_cw, d_ad = _gdn_pre_bwd(p_qkv, p_ba, pre_qkv, cw_pad, alog_v, dt_v, dqn, dkn, dva, dgates, tt=tt)
    dp_ba_b = dp_ba.astype(bf16)

    grad_x_factors = ([dp_conv, dp_qkv, dp_gz, dp_gates, dp_ba_b], [w_conv, w_qkv, w_gz, w_gates, w_ba], dxd)

    tk = min(MM_K_TILE, t)
    d_w_conv = _mm_kloop(dp_conv, x_b, tm=MM_TILE, tn=D, tk=tk, name="grad_w_in_conv")
    d_w_qkv = _mm_kloop(dp_qkv, x_b, tm=MM_TILE, tn=D, tk=tk, name="grad_w_in_qkv")
    d_w_gz = _mm_kloop(dp_gz, x_b, tm=MM_TILE, tn=D, tk=tk, name="grad_w_in_gz")
    d_w_gates = _mm_kloop(dp_gates, x_b, tm=MM_TILE, tn=D, tk=tk, name="grad_w_in_gates")
    d_w_ba = _mm_kloop(dp_ba_b, x_b, tm=HD, tn=D, tk=tk, name="grad_w_in_ba")
    d_w_in = jnp.concatenate([d_w_conv, d_w_qkv, d_w_gz, d_w_ba[:2 * NH], d_w_gates], axis=0).reshape(
        N_CHIPS, W_IN_BLK, D)

    return (loss_blk[0, 0], grad_x_factors, d_w_in, d_wc, d_wg, d_wo, d_dww, d_cw, dconv_vec, dpost, d_ad, dng)


def kernel(x, w_in, conf_dw_w, conf_dw_b, conf_ln_g, conf_ln_b, conf_w_out, gdn_conv_w, gdn_A_log, gdn_dt_bias, gdn_norm_g, gdn_w_out, w_o, post_ln_g, post_ln_b, loss_target, m_w_in, m_conf_dw_w, m_conf_dw_b, m_conf_ln_g, m_conf_ln_b, m_conf_w_out, m_gdn_conv_w, m_gdn_A_log, m_gdn_dt_bias, m_gdn_norm_g, m_gdn_w_out, m_w_o, m_post_ln_g, m_post_ln_b, v_w_in, v_conf_dw_w, v_conf_dw_b, v_conf_ln_g, v_conf_ln_b, v_conf_w_out, v_gdn_conv_w, v_gdn_A_log, v_gdn_dt_bias, v_gdn_norm_g, v_gdn_w_out, v_w_o, v_post_ln_g, v_post_ln_b):
    x2 = x.reshape(x.shape[-2], D)
    tgt = loss_target.reshape(x2.shape)
    w_t, wc_out, wg_out, wo_full, dw_full, cw_full = _gather_weights(
        w_in, conf_w_out, gdn_w_out, w_o, conf_dw_w, gdn_conv_w)
    (loss_part, grad_x_factors, d_w_in, d_wc, d_wg, d_wo, d_dww, d_cw, dconv_vec, dpost, d_ad, dng) = _local_step(
        x2, tgt, w_t, wc_out, wg_out, wo_full, dw_full, cw_full, conf_dw_b, conf_ln_g, conf_ln_b,
        gdn_A_log, gdn_dt_bias, gdn_norm_g, post_ln_g, post_ln_b)
    loss = lax.psum(loss_part, ("x", "y", "c"))

    dww_c = d_dww[:KC].reshape(KC, N_CHIPS, SQ_BLK)
    dcw_c = d_cw[:KG].reshape(KG, N_CHIPS, 3 * SQ_BLK)
    vecs = [dconv_vec[0], dconv_vec[1], dconv_vec[2], dpost[0], dpost[1]]
    g_rest = jnp.stack([
        _pack_rest(d_wc[j * SQ_BLK:(j + 1) * SQ_BLK], d_wg[j * SQ_BLK:(j + 1) * SQ_BLK], d_wo[j * SQ_BLK:(j + 1) * SQ_BLK],
                   _pack_small(dww_c[:, j], dcw_c[:, j], vecs, d_ad[0, NH:2 * NH], d_ad[1, NH:2 * NH], dng[0]))
        for j in range(N_CHIPS)])
    pair_w, pair_r = _pair_sums(d_w_in, g_rest)
    grad_x, all_w, all_r = _mm_multi(*grad_x_factors, out_dtype=f32, tm=min(GRAD_X_TILE[0], x2.shape[0]), tn=GRAD_X_TILE[1],
                                     name="grad_x_and_chip_scatter", scatter=[pair_w, pair_r])
    g_w_in, g_rest = _chip_sums(all_w, all_r)

    def rest_of(w_c, w_g, w_oo, dw, cw, b1, g1, b2, g2, b3, a_log, dt_bias, norm_g):
        return _pack_rest(w_c, w_g, w_oo, _pack_small(dw, cw, [b1, g1, b2, g2, b3], a_log, dt_bias, norm_g))

    w_r = rest_of(conf_w_out, gdn_w_out, w_o, conf_dw_w, gdn_conv_w, conf_dw_b, conf_ln_g, conf_ln_b,
                  post_ln_g, post_ln_b, gdn_A_log, gdn_dt_bias, gdn_norm_g)
    m_r = rest_of(m_conf_w_out, m_gdn_w_out, m_w_o, m_conf_dw_w, m_gdn_conv_w, m_conf_dw_b, m_conf_ln_g, m_conf_ln_b,
                  m_post_ln_g, m_post_ln_b, m_gdn_A_log, m_gdn_dt_bias, m_gdn_norm_g)
    v_r = rest_of(v_conf_w_out, v_gdn_w_out, v_w_o, v_conf_dw_w, v_gdn_conv_w, v_conf_dw_b, v_conf_ln_g, v_conf_ln_b,
                  v_post_ln_g, v_post_ln_b, v_gdn_A_log, v_gdn_dt_bias, v_gdn_norm_g)
    upd_w_in = _adamw(w_in.T, g_w_in, m_w_in.T, v_w_in.T, "adamw_w_in")
    upd_rest = _adamw(w_r, g_rest, m_r, v_r, "adamw_rest")

    out = [loss, grad_x.reshape(x.shape)]
    for big, rest in zip((g_w_in,) + tuple(upd_w_in), (g_rest,) + tuple(upd_rest)):
        d = dict(_unpack_rest(rest), w_in=big.T)
        out += [d[n] for n in _WEIGHT_ORDER]
    return tuple(out)
```

```python
import jax
import jax.numpy as jnp
from jax import lax
from jax.experimental import pallas as pl
from jax.experimental.pallas import tpu as pltpu

f32 = jnp.float32
bf16 = jnp.bfloat16
HI = lax.Precision.HIGHEST
MESH = pl.DeviceIdType.MESH

D = 1024
NH = 8
HD = 128
CH = 64
KC = 31
KG = 4
HALO_C = 32
HALO_G = 8
LANE = 128
STRIP = 32
N_SHIFT = 7
LN_EPS = 1e-5
RMS_EPS = 1e-6
L2_EPS = 1e-6
DN_ALPHA = 2.0 ** 0.25
N_CHIPS = 4
W_IN_COLS = 9232
W_IN_BLK = W_IN_COLS // N_CHIPS
SQ_BLK = D // N_CHIPS
VMEM_LIMIT = 52 * 1024 * 1024
MM_TILE = 1024
MM_K_TILE = 1024
GRAD_X_TILE = (256, 512)
TOKEN_TILE = 256
SCAN_GROUP = 4

ADAM_LR = 0.001
ADAM_B1 = 0.9
ADAM_B2 = 0.999
ADAM_EPS = 1e-08
ADAM_WD = 0.01
ADAM_STEP = 10


def _sigmoid(x):
    return 1.0 / (1.0 + jnp.exp(-x))


def _silu_and_grad(x):
    s = _sigmoid(x)
    return x * s, s * (1.0 + x * (1.0 - s))


_NN = ((1,), (0,))
_NT = ((1,), (1,))
_TN = ((0,), (0,))


def _cparams(*sem):
    return pltpu.CompilerParams(dimension_semantics=sem, vmem_limit_bytes=VMEM_LIMIT)


def _mm_multi(a_list, b_list, addend=None, *, out_dtype, tm, tn, name, rhs_t=False, scatter=()):
    n_pairs = len(a_list)
    m = a_list[0].shape[0]
    n = b_list[0].shape[0 if rhs_t else 1]
    has_add = addend is not None
    dims = (_NT if rhs_t else _NN, ((), ()))
    n_in = 2 * n_pairs + has_add
    k = len(scatter)
    grid = (n // tn, m // tm)

    def body(*refs):
        a_refs = refs[:n_pairs]
        b_refs = refs[n_pairs:2 * n_pairs]
        o_ref = refs[n_in + k]
        if k:
            start, finish = _chip_exchange_ops(refs[n_in:n_in + k], refs[n_in + k + 1:n_in + 2 * k + 1],
                                               *refs[n_in + 2 * k + 1:], True)
            step = pl.program_id(0) * grid[1] + pl.program_id(1)
            pl.when(step == 0)(start)
        acc = None
        for a_ref, b_ref in zip(a_refs, b_refs):
            p = lax.dot_general(a_ref[...].astype(bf16), b_ref[...].astype(bf16), dims, preferred_element_type=f32)
            acc = p if acc is None else acc + p
        if has_add:
            acc = acc + refs[2 * n_pairs][...]
        o_ref[...] = acc.astype(out_dtype)
        if k:
            pl.when(step == grid[0] * grid[1] - 1)(finish)

    in_specs = [pl.BlockSpec((tm, a.shape[1]), lambda j, i: (i, 0)) for a in a_list]
    if rhs_t:
        in_specs += [pl.BlockSpec((tn, b.shape[1]), lambda j, i: (j, 0)) for b in b_list]
    else:
        in_specs += [pl.BlockSpec((b.shape[0], tn), lambda j, i: (0, j)) for b in b_list]
    args = list(a_list) + list(b_list)
    if has_add:
        in_specs.append(pl.BlockSpec((tm, tn), lambda j, i: (i, j)))
        args.append(addend)
    out = pl.pallas_call(
        body, name=name, grid=grid,
        in_specs=in_specs + _any_specs(k), out_specs=[pl.BlockSpec((tm, tn), lambda j, i: (i, j))] + _any_specs(k),
        out_shape=[jax.ShapeDtypeStruct((m, n), out_dtype)] + _chip_exchange_shapes(scatter, True),
        scratch_shapes=_chip_exchange_sems(k) if k else [],
        compiler_params=_cparams("arbitrary", "arbitrary") if k else _cparams("parallel", "parallel"),
    )(*args, *scatter)
    return out if k else out[0]


def _mm_kloop(a, b, *, tm, tn, tk, name):
    k, m = a.shape
    n = b.shape[1]
    nk = k // tk

    def body(a_ref, b_ref, o_ref):
        @pl.when(pl.program_id(2) == 0)
        def _():
            o_ref[...] = jnp.zeros_like(o_ref)
        o_ref[...] += lax.dot_general(a_ref[...].astype(bf16), b_ref[...].astype(bf16), (_TN, ((), ())),
                                      preferred_element_type=f32)

    return pl.pallas_call(
        body, name=name, grid=(n // tn, m // tm, nk),
        in_specs=[pl.BlockSpec((tk, tm), lambda j, i, kk: (kk, i)), pl.BlockSpec((tk, tn), lambda j, i, kk: (kk, j))],
        out_specs=pl.BlockSpec((tm, tn), lambda j, i, kk: (i, j)),
        out_shape=jax.ShapeDtypeStruct((m, n), f32),
        compiler_params=_cparams("parallel", "parallel", "arbitrary"),
    )(a, b)


def _shift_copies(src_ref, sh_ref, n, shifts=tuple(range(1, 8))):
    for i, b in enumerate(shifts):
        sh_ref[i, 0:n, :] = src_ref[pl.ds(b, n), :]


def _by_residue(offs):
    groups = {}
    for k, off in enumerate(offs):
        groups.setdefault(off % 8, []).append((k, off // 8))
    return groups


def _slab(src_ref, sh_ref, shifts, b, r0, n, lanes):
    ref = src_ref if b == 0 else sh_ref.at[shifts.index(b)]
    return ref[r0:r0 + n, lanes]


def _tap_conv(out_ref, n_rows, src_ref, sh_ref, w_ref, offs, bias_ref=None, shifts=tuple(range(1, 8))):
    groups = _by_residue(offs)
    for j in range(D // LANE):
        lanes = slice(j * LANE, (j + 1) * LANE)
        wv = [w_ref[k:k + 1, lanes] for k in range(len(offs))]
        for r0 in range(0, n_rows, STRIP):
            n = min(STRIP, n_rows - r0)
            accs = [jnp.zeros((n, LANE), f32) if bias_ref is None else jnp.broadcast_to(bias_ref[0:1, lanes], (n, LANE)),
                    jnp.zeros((n, LANE), f32)]
            m = 0
            for b, taps in groups.items():
                a_lo = min(a for _, a in taps)
                a_hi = max(a for _, a in taps)
                wide = _slab(src_ref, sh_ref, shifts, b, r0 + 8 * a_lo, 8 * (a_hi - a_lo) + n, lanes)
                for k, a in taps:
                    accs[m % 2] = accs[m % 2] + wv[k] * wide[8 * (a - a_lo):8 * (a - a_lo) + n]
                    m += 1
            out_ref[r0:r0 + n, lanes] = accs[0] + accs[1]


def _tap_corr(dw_ref, n_rows, lhs_ref, src_ref, sh_ref, offs, shifts=tuple(range(1, 8))):
    groups = _by_residue(offs)
    for j in range(D // LANE):
        lanes = slice(j * LANE, (j + 1) * LANE)
        accs = [jnp.zeros((8, LANE), f32) for _ in offs]
        for r0 in range(0, n_rows, STRIP):
            n = min(STRIP, n_rows - r0)
            d = lhs_ref[r0:r0 + n, lanes]
            for b, taps in groups.items():
                a_lo = min(a for _, a in taps)
                a_hi = max(a for _, a in taps)
                wide = _slab(src_ref, sh_ref, shifts, b, r0 + 8 * a_lo, 8 * (a_hi - a_lo) + n, lanes)
                for k, a in taps:
                    prod = d * wide[8 * (a - a_lo):8 * (a - a_lo) + n]
                    part = prod[0:8]
                    for q in range(1, n // 8):
                        part = part + prod[8 * q:8 * q + 8]
                    accs[k] = accs[k] + part
        for k in range(len(offs)):
            dw_ref[k:k + 1, lanes] += jnp.sum(accs[k], axis=0, keepdims=True)


_FWD_OFFS = [HALO_C - (KC - 1) + k for k in range(KC)]
_BWD_OFFS = [KC - 1 - k for k in range(KC)]


def _norm_act(a1, cz, g_ref, bb_ref):
    mu = jnp.mean(a1, axis=-1, keepdims=True)
    cen = a1 - mu
    var = jnp.mean(cen * cen, axis=-1, keepdims=True)
    rstd = lax.rsqrt(var + LN_EPS)
    xhat = cen * rstd
    ln = xhat * g_ref[...] + bb_ref[...]
    s, ds = _silu_and_grad(ln)
    zc, dzc = _silu_and_grad(cz)
    return xhat, rstd, s, ds, zc, dzc


def _conv_fwd(p_conv, dw_w, dw_b, ln_g, ln_b, w_out, *, tt):
    t = p_conv.shape[0]
    hb = tt // HALO_C

    def body(cv_ref, cg_ref, cz_ref, cvh_ref, cgh_ref, w_ref, b_ref, g_ref, bb_ref, wo_ref,
             u_ref, a1_ref, y_ref, ext_ref, sh_ref):
        first = pl.program_id(0) == 0
        halo = cvh_ref[...] * _sigmoid(cgh_ref[...])
        ext_ref[0:HALO_C, :] = jnp.where(first, 0.0, halo)
        ext_ref[HALO_C:, :] = cv_ref[...] * _sigmoid(cg_ref[...])
        _shift_copies(ext_ref, sh_ref, tt + HALO_C - 8)
        _tap_conv(a1_ref, tt, ext_ref, sh_ref, w_ref, _FWD_OFFS, b_ref)
        _, _, s, _, zc, _ = _norm_act(a1_ref[...], cz_ref[...], g_ref, bb_ref)
        u = (s * zc).astype(bf16)
        u_ref[...] = u
        y_ref[...] = jnp.dot(u, wo_ref[...], preferred_element_type=f32)

    def main(col):
        return pl.BlockSpec((tt, D), lambda i: (i, col))

    def prev(col):
        return pl.BlockSpec((HALO_C, D), lambda i: (jnp.maximum(i * hb - 1, 0), col))

    vec = pl.BlockSpec((1, D), lambda i: (0, 0))
    return pl.pallas_call(
        body, name="conv_fwd", grid=(t // tt,),
        in_specs=[main(0), main(1), main(2), prev(0), prev(1),
                  pl.BlockSpec((HALO_C, D), lambda i: (0, 0)), vec, vec, vec, pl.BlockSpec((D, D), lambda i: (0, 0))],
        out_specs=[pl.BlockSpec((tt, D), lambda i: (i, 0))] * 3,
        out_shape=[jax.ShapeDtypeStruct((t, D), bf16), jax.ShapeDtypeStruct((t, D), f32),
                   jax.ShapeDtypeStruct((t, D), f32)],
        scratch_shapes=[pltpu.VMEM((tt + HALO_C, D), f32), pltpu.VMEM((N_SHIFT, tt + HALO_C - 8, D), f32)],
        compiler_params=_cparams("parallel"),
    )(p_conv, p_conv, p_conv, p_conv, p_conv, dw_w, dw_b, ln_g, ln_b, w_out)


def _conv_bwd(p_conv, a1, du, dw_w, ln_g, ln_b, *, tt):
    t = p_conv.shape[0]
    hb = tt // HALO_C
    n_tiles = t // tt
    last_hb = t // HALO_C - 1
    ne = tt + HALO_C

    def body(cv_ref, cg_ref, cz_ref, a1_ref, du_ref, cvp_ref, cgp_ref, czn_ref, a1n_ref, dun_ref,
             w_ref, g_ref, bb_ref, dp_ref, dww_ref, dvec_ref, ext_ref, sh_ref, da1_ref, da0_ref):
        i = pl.program_id(0)
        first = i == 0
        last = i == n_tiles - 1

        @pl.when(first)
        def _():
            dww_ref[...] = jnp.zeros_like(dww_ref)
            dvec_ref[...] = jnp.zeros_like(dvec_ref)

        sig = _sigmoid(cg_ref[...])
        ext_ref[0:HALO_C, :] = jnp.where(first, 0.0, cvp_ref[...] * _sigmoid(cgp_ref[...]))
        ext_ref[HALO_C:, :] = cv_ref[...] * sig
        a1_all = jnp.concatenate([a1_ref[...], a1n_ref[...]], axis=0)
        cz = jnp.concatenate([cz_ref[...], czn_ref[...]], axis=0)
        du_all = jnp.concatenate([du_ref[...], jnp.where(last, 0.0, dun_ref[...])], axis=0)
        xhat, rstd, s, ds, zc, dzc = _norm_act(a1_all, cz, g_ref, bb_ref)
        dln = du_all * zc * ds
        dxhat = dln * g_ref[...]
        da1 = rstd * (dxhat - jnp.mean(dxhat, axis=-1, keepdims=True)
                      - xhat * jnp.mean(dxhat * xhat, axis=-1, keepdims=True))
        da1_ref[...] = da1
        dcz = (du_all * s * dzc)[:tt]
        dvec_ref[0:1, :] += jnp.sum(da1[:tt], axis=0, keepdims=True)
        dvec_ref[1:2, :] += jnp.sum((dln * xhat)[:tt], axis=0, keepdims=True)
        dvec_ref[2:3, :] += jnp.sum(dln[:tt], axis=0, keepdims=True)
        _shift_copies(ext_ref, sh_ref, ne - 8)
        _tap_corr(dww_ref, tt, da1_ref, ext_ref, sh_ref, _FWD_OFFS)
        _shift_copies(da1_ref, sh_ref, ne - 8)
        _tap_conv(da0_ref, tt, da1_ref, sh_ref, w_ref, _BWD_OFFS)
        da0 = da0_ref[...]
        cv = cv_ref[...]
        dp_ref[:, 0:D] = (da0 * sig).astype(bf16)
        dp_ref[:, D:2 * D] = (da0 * cv * sig * (1.0 - sig)).astype(bf16)
        dp_ref[:, 2 * D:] = dcz.astype(bf16)

    def main(col):
        return pl.BlockSpec((tt, D), lambda i: (i, col))

    def prev(col):
        return pl.BlockSpec((HALO_C, D), lambda i: (jnp.maximum(i * hb - 1, 0), col))

    def nxt(col):
        return pl.BlockSpec((HALO_C, D), lambda i: (jnp.minimum((i + 1) * hb, last_hb), col))

    vec = pl.BlockSpec((1, D), lambda i: (0, 0))
    return pl.pallas_call(
        body, name="conv_bwd", grid=(n_tiles,),
        in_specs=[main(0), main(1), main(2), main(0), main(0), prev(0), prev(1), nxt(2), nxt(0), nxt(0),
                  pl.BlockSpec((HALO_C, D), lambda i: (0, 0)), vec, vec],
        out_specs=[pl.BlockSpec((tt, 3 * D), lambda i: (i, 0)),
                   pl.BlockSpec((HALO_C, D), lambda i: (0, 0)),
                   pl.BlockSpec((8, D), lambda i: (0, 0))],
        out_shape=[jax.ShapeDtypeStruct((t, 3 * D), bf16), jax.ShapeDtypeStruct((HALO_C, D), f32),
                   jax.ShapeDtypeStruct((8, D), f32)],
        scratch_shapes=[pltpu.VMEM((ne, D), f32), pltpu.VMEM((N_SHIFT, ne - 8, D), f32),
                        pltpu.VMEM((ne, D), f32), pltpu.VMEM((tt, D), f32)],
        compiler_params=_cparams("arbitrary"),
    )(p_conv, p_conv, p_conv, a1, du, p_conv, p_conv, p_conv, a1, du, dw_w, ln_g, ln_b)


def _dot_hi(a, b):
    return lax.dot_general(a, b, (((1,), (0,)), ((), ())), precision=HI, preferred_element_type=f32)


def _chunk_tri(n, lower):
    r = lax.broadcasted_iota(jnp.int32, (n, n), 0)
    c = lax.broadcasted_iota(jnp.int32, (n, n), 1)
    tri = (r >= c) if lower else (r <= c)
    return jnp.where(tri & (r // CH == c // CH), 1.0, 0.0).astype(f32)


def _softplus_and_sigmoid(x):
    e = jnp.exp(-jnp.abs(x))
    log1p = jnp.where(e < 1e-2, e * (1.0 - e * (0.5 - e * (1.0 / 3.0 - 0.25 * e))), jnp.log(1.0 + e))
    return jnp.maximum(x, 0.0) + log1p, _sigmoid(x)


_G_FWD_OFFS = [HALO_G - (KG - 1) + k for k in range(KG)]
_G_FWD_SHIFTS = (5, 6, 7)
_G_BWD_OFFS = [KG - 1 - k for k in range(KG)]
_G_BWD_SHIFTS = (1, 2, 3)


def _gdn_short_conv(pre_ref, ext_ref, sh_ref, n_rows, w_ref):
    _shift_copies(ext_ref, sh_ref, n_rows, _G_FWD_SHIFTS)
    _tap_conv(pre_ref, n_rows, ext_ref, sh_ref, w_ref, _G_FWD_OFFS, shifts=_G_FWD_SHIFTS)
    return pre_ref[...]


def _l2norm_heads(act, scale):
    outs, rs = [], []
    for h in range(NH):
        a = act[:, h * HD:(h + 1) * HD]
        r = lax.rsqrt(jnp.sum(a * a, axis=-1, keepdims=True) + L2_EPS)
        outs.append(a * (r * scale))
        rs.append(jnp.broadcast_to(r, a.shape))
    return jnp.concatenate(outs, axis=-1), jnp.concatenate(rs, axis=-1)


def _gate_math(ba, al_ref, dt_ref):
    lane = lax.broadcasted_iota(jnp.int32, ba.shape, 1)
    is_b = lane < NH
    is_a = (lane >= NH) & (lane < 2 * NH)
    sp, sg = _softplus_and_sigmoid(ba + dt_ref[...])
    neg_a = -jnp.exp(al_ref[...])
    return is_b, is_a, _sigmoid(ba), neg_a * sp, sg, neg_a


def _gdn_pre_fwd(p_qkv, p_ba, cw, alog_v, dt_v, *, tt):
    t = p_qkv.shape[0]
    hb = tt // HALO_G

    def body(q_ref, k_ref, v_ref, qh_ref, kh_ref, vh_ref, ba_ref, wq_ref, wk_ref, wv_ref, al_ref, dt_ref,
             qn_ref, kn_ref, va_ref, gt_ref, pre_ref, ext_ref, sh_ref):
        first = pl.program_id(0) == 0

        def conv_act(x_ref, xh_ref, w_ref, col):
            ext_ref[0:HALO_G, :] = jnp.where(first, 0.0, xh_ref[...])
            ext_ref[HALO_G:, :] = x_ref[...]
            pre = _gdn_short_conv(pre_ref.at[:, col * D:(col + 1) * D], ext_ref, sh_ref, tt, w_ref)
            return pre * _sigmoid(pre)

        qn_ref[...] = _l2norm_heads(conv_act(q_ref, qh_ref, wq_ref, 0), HD ** -0.5)[0]
        kn_ref[...] = _l2norm_heads(conv_act(k_ref, kh_ref, wk_ref, 1), 1.0)[0]
        va_ref[...] = conv_act(v_ref, vh_ref, wv_ref, 2)
        is_b, is_a, beta, g, _, _ = _gate_math(ba_ref[...], al_ref, dt_ref)
        gc = _dot_hi(_chunk_tri(tt, lower=True), jnp.where(is_a, g, 0.0))
        gt_ref[...] = jnp.where(is_b, beta, gc)

    def main(col):
        return pl.BlockSpec((tt, D), lambda i: (i, col))

    def prev(col):
        return pl.BlockSpec((HALO_G, D), lambda i: (jnp.maximum(i * hb - 1, 0), col))

    def wspec(col):
        return pl.BlockSpec((8, D), lambda i: (0, col))

    vec = pl.BlockSpec((1, HD), lambda i: (0, 0))
    gblk = pl.BlockSpec((tt, HD), lambda i: (i, 0))
    sds = jax.ShapeDtypeStruct((t, D), f32)
    return pl.pallas_call(
        body, name="gdn_pre_fwd", grid=(t // tt,),
        in_specs=[main(0), main(1), main(2), prev(0), prev(1), prev(2), gblk, wspec(0), wspec(1), wspec(2), vec, vec],
        out_specs=[pl.BlockSpec((tt, D), lambda i: (i, 0))] * 3 + [gblk, pl.BlockSpec((tt, 3 * D), lambda i: (i, 0))],
        out_shape=[sds] * 3 + [jax.ShapeDtypeStruct((t, HD), f32), jax.ShapeDtypeStruct((t, 3 * D), f32)],
        scratch_shapes=[pltpu.VMEM((tt + HALO_G, D), f32), pltpu.VMEM((KG - 1, tt, D), f32)],
        compiler_params=_cparams("parallel"),
    )(p_qkv, p_qkv, p_qkv, p_qkv, p_qkv, p_qkv, p_ba, cw, cw, cw, alog_v, dt_v)


def _gdn_pre_bwd(p_qkv, p_ba, pre, cw, alog_v, dt_v, dqn, dkn, dva, dgt, *, tt):
    t = p_qkv.shape[0]
    hb = tt // HALO_G
    n_tiles = t // tt
    last_hb = t // HALO_G - 1
    ne = tt + HALO_G

    def body(q_ref, k_ref, v_ref, qp_ref, kp_ref, vp_ref, pq_ref, pk_ref, pv_ref, pqx_ref, pkx_ref, pvx_ref,
             dq_ref, dk_ref, dv_ref, dqx_ref, dkx_ref, dvx_ref, ba_ref, dgt_ref,
             wq_ref, wk_ref, wv_ref, al_ref, dt_ref,
             dp_ref, dba_ref, dcw_ref, dad_ref, ext_ref, sh_ref, dpre_ref, draw_ref):
        i = pl.program_id(0)
        first = i == 0
        last = i == n_tiles - 1

        @pl.when(first)
        def _():
            dcw_ref[...] = jnp.zeros_like(dcw_ref)
            dad_ref[...] = jnp.zeros_like(dad_ref)

        def one(x_ref, xp_ref, pre_ref, prex_ref, d_ref, dx_ref, w_ref, col, scale):
            ext_ref[0:HALO_G, :] = jnp.where(first, 0.0, xp_ref[...])
            ext_ref[HALO_G:, :] = x_ref[...]
            _shift_copies(ext_ref, sh_ref, tt, _G_FWD_SHIFTS)
            act, dact = _silu_and_grad(jnp.concatenate([pre_ref[...], prex_ref[...]], axis=0))
            d_out = jnp.concatenate([d_ref[...], jnp.where(last, 0.0, dx_ref[...])], axis=0)
            if scale is None:
                d_act = d_out
            else:
                parts = []
                for h in range(NH):
                    a = act[:, h * HD:(h + 1) * HD]
                    dn = d_out[:, h * HD:(h + 1) * HD]
                    r = lax.rsqrt(jnp.sum(a * a, axis=-1, keepdims=True) + L2_EPS)
                    parts.append(scale * r * (dn - a * (r * r) * jnp.sum(dn * a, axis=-1, keepdims=True)))
                d_act = jnp.concatenate(parts, axis=-1)
            dpre_ref[...] = d_act * dact
            _tap_corr(dcw_ref.at[:, col * D:(col + 1) * D], tt, dpre_ref, ext_ref, sh_ref, _G_FWD_OFFS, shifts=_G_FWD_SHIFTS)
            _shift_copies(dpre_ref, sh_ref, tt, _G_BWD_SHIFTS)
            _tap_conv(draw_ref, tt, dpre_ref, sh_ref, w_ref, _G_BWD_OFFS, shifts=_G_BWD_SHIFTS)
            dp_ref[:, col * D:(col + 1) * D] = draw_ref[...].astype(bf16)

        one(q_ref, qp_ref, pq_ref, pqx_ref, dq_ref, dqx_ref, wq_ref, 0, HD ** -0.5)
        one(k_ref, kp_ref, pk_ref, pkx_ref, dk_ref, dkx_ref, wk_ref, 1, 1.0)
        one(v_ref, vp_ref, pv_ref, pvx_ref, dv_ref, dvx_ref, wv_ref, 2, None)

        is_b, is_a, beta, g, sg, neg_a = _gate_math(ba_ref[...], al_ref, dt_ref)
        dgt_v = dgt_ref[...]
        dg = _dot_hi(_chunk_tri(tt, lower=False), jnp.where(is_a, dgt_v, 0.0))
        d_al = jnp.where(is_a, dg * neg_a * sg, 0.0)
        dba_ref[...] = jnp.where(is_b, dgt_v * beta * (1.0 - beta), d_al)
        dad_ref[0:1, :] += jnp.sum(jnp.where(is_a, dg * g, 0.0), axis=0, keepdims=True)
        dad_ref[1:2, :] += jnp.sum(d_al, axis=0, keepdims=True)

    def main(col):
        return pl.BlockSpec((tt, D), lambda i: (i, col))

    def prev(col):
        return pl.BlockSpec((HALO_G, D), lambda i: (jnp.maximum(i * hb - 1, 0), col))

    def nxt(col):
        return pl.BlockSpec((HALO_G, D), lambda i: (jnp.minimum((i + 1) * hb, last_hb), col))

    def wspec(col):
        return pl.BlockSpec((8, D), lambda i: (0, col))

    vec = pl.BlockSpec((1, HD), lambda i: (0, 0))
    gblk = pl.BlockSpec((tt, HD), lambda i: (i, 0))
    return pl.pallas_call(
        body, name="gdn_pre_bwd", grid=(n_tiles,),
        in_specs=[main(0), main(1), main(2), prev(0), prev(1), prev(2), main(0), main(1), main(2), nxt(0), nxt(1), nxt(2),
                  main(0), main(0), main(0), nxt(0), nxt(0), nxt(0), gblk, gblk,
                  wspec(0), wspec(1), wspec(2), vec, vec],
        out_specs=[pl.BlockSpec((tt, 3 * D), lambda i: (i, 0)), pl.BlockSpec((tt, HD), lambda i: (i, 0)),
                   pl.BlockSpec((8, 3 * D), lambda i: (0, 0)), pl.BlockSpec((8, HD), lambda i: (0, 0))],
        out_shape=[jax.ShapeDtypeStruct((t, 3 * D), bf16), jax.ShapeDtypeStruct((t, HD), f32),
                   jax.ShapeDtypeStruct((8, 3 * D), f32), jax.ShapeDtypeStruct((8, HD), f32)],
        scratch_shapes=[pltpu.VMEM((HALO_G + tt, D), f32), pltpu.VMEM((KG - 1, tt, D), f32),
                        pltpu.VMEM((ne, D), f32), pltpu.VMEM((tt, D), f32)],
        compiler_params=_cparams("arbitrary"),
    )(p_qkv, p_qkv, p_qkv, p_qkv, p_qkv, p_qkv, pre, pre, pre, pre, pre, pre,
      dqn, dkn, dva, dqn, dkn, dva, p_ba, dgt, cw, cw, cw, alog_v, dt_v)


def _dot_b(a, b, dims):
    return lax.dot_general(a.astype(bf16), b.astype(bf16), (dims, ((), ())), preferred_element_type=f32)


def _inverse_by_doubling(ms):
    heads = range(len(ms))
    r = lax.broadcasted_iota(jnp.int32, (CH, CH), 0)
    c = lax.broadcasted_iota(jnp.int32, (CH, CH), 1)
    eye = jnp.where(r == c, 1.0, 0.0).astype(f32)
    p = [eye + ms[h] for h in heads]
    mp = ms
    for _ in range(5):
        mp = [_dot_b(mp[h], mp[h], _NN) for h in heads]
        pm = [_dot_b(p[h], mp[h], _NN) for h in heads]
        p = [p[h] + pm[h] for h in heads]
    return tuple(p)


@jax.custom_vjp
def _known_inverse(ms, ps):
    return ps


def _known_inverse_fwd(ms, ps):
    return ps, ps


def _known_inverse_bwd(ps, cts):
    heads = range(len(ps))
    left = [_dot_b(ps[h], cts[h], _TN) for h in heads]
    return tuple(_dot_b(left[h], ps[h], _NT) for h in heads), tuple(jnp.zeros_like(p) for p in ps)


_known_inverse.defvjp(_known_inverse_fwd, _known_inverse_bwd)


def _chunk_prepare(qs, ks, vs, gcs, bbs, ps=None):
    heads = range(len(qs))
    r = lax.broadcasted_iota(jnp.int32, (CH, CH), 0)
    c = lax.broadcasted_iota(jnp.int32, (CH, CH), 1)
    causal = r >= c
    strict = r > c
    gc_row = [gcs[h].T[:CH, :] for h in heads]
    decay = [jnp.where(causal, jnp.exp(jnp.where(causal, gcs[h][:, :CH] - gc_row[h], 0.0)), 0.0) for h in heads]
    kb = [ks[h] * bbs[h] for h in heads]
    egc = [jnp.exp(gcs[h]) for h in heads]
    kk = [_dot_b(kb[h], ks[h], _NT) for h in heads]
    qk = [_dot_b(qs[h], ks[h], _NT) for h in heads]
    m = tuple(-jnp.where(strict, kk[h] * decay[h], 0.0) for h in heads)
    p = _inverse_by_doubling(m) if ps is None else _known_inverse(m, ps)
    u = [_dot_b(p[h], vs[h] * bbs[h], _NN) for h in heads]
    w = [_dot_b(p[h], kb[h] * egc[h], _NN) for h in heads]
    intra = [jnp.where(causal, qk[h] * decay[h], 0.0) for h in heads]
    g_last = [gcs[h][CH - 1:CH, :] for h in heads]
    k_dec = [ks[h] * jnp.exp(g_last[h] - gcs[h]) for h in heads]
    q_dec = [qs[h] * egc[h] for h in heads]
    e_last = [jnp.exp(g_last[h]) for h in heads]
    return u, w, intra, q_dec, k_dec, e_last, p


def _chunk_apply(u, w, intra, q_dec, k_dec, e_last, ss):
    heads = range(len(ss))
    ws = [_dot_b(w[h], ss[h], _NN) for h in heads]
    qs_s = [_dot_b(q_dec[h], ss[h], _NN) for h in heads]
    v_new = [u[h] - ws[h] for h in heads]
    iv = [_dot_b(intra[h], v_new[h], _NN) for h in heads]
    kv = [_dot_b(k_dec[h], v_new[h], _TN) for h in heads]
    o = tuple(qs_s[h] + iv[h] for h in heads)
    s_new = tuple(ss[h] * e_last[h] + kv[h] for h in heads)
    return o, s_new


def _chunk_group_fn(ins, ss, ps=None):
    n = len(ss)
    prep = _chunk_prepare(*(sum((tuple(c[i]) for c in ins), ()) for i in range(5)), ps=ps)
    outs, befores = [], []
    for g in range(len(ins)):
        befores.append(ss)
        o, ss = _chunk_apply(*(x[g * n:(g + 1) * n] for x in prep[:6]), ss)
        outs.append(o)
    return tuple(outs), tuple(befores), ss, prep[6]


def _head_cols():
    return [slice(h * HD, (h + 1) * HD) for h in range(NH)]


def _head_gates(gt):
    gcs = tuple(jnp.broadcast_to(gt[:, NH + h:NH + h + 1], (CH, HD)) for h in range(NH))
    bbs = tuple(jnp.broadcast_to(gt[:, h:h + 1], (CH, HD)) for h in range(NH))
    return gcs, bbs


def _gdn_scan_fwd(qn, kn, va, gates, *, tt):
    t = qn.shape[0]
    cpb = tt // CH
    group = min(SCAN_GROUP, cpb)

    def body(q_ref, k_ref, v_ref, gt_ref, o_ref, st_ref, p_ref, s_scr):
        @pl.when(pl.program_id(0) == 0)
        def _():
            s_scr[...] = jnp.zeros_like(s_scr)

        cols = _head_cols()

        def inputs(ci):
            rows = pl.ds(pl.multiple_of(ci * CH, CH), CH)
            gcs, bbs = _head_gates(gt_ref[rows, :])
            return tuple(tuple(ref[rows, cl] for cl in cols) for ref in (q_ref, k_ref, v_ref)) + (gcs, bbs)

        def step(gi, carry):
            chunks = [group * gi + g for g in range(group)]
            outs, befores, s_end, p = _chunk_group_fn([inputs(ci) for ci in chunks], tuple(s_scr[h] for h in range(NH)))
            for g, ci in enumerate(chunks):
                rows = pl.ds(pl.multiple_of(ci * CH, CH), CH)
                for h in range(NH):
                    st_ref[ci, h] = befores[g][h]
                    o_ref[rows, cols[h]] = outs[g][h]
                    p_ref[ci, h] = p[g * NH + h].astype(bf16)
            for h in range(NH):
                s_scr[h] = s_end[h]
            return carry

        lax.fori_loop(0, cpb // group, step, 0)

    blk = pl.BlockSpec((tt, D), lambda i: (i, 0))
    return pl.pallas_call(
        body, name="gdn_scan_fwd", grid=(t // tt,),
        in_specs=[blk] * 3 + [pl.BlockSpec((tt, HD), lambda i: (i, 0))],
        out_specs=[blk, pl.BlockSpec((cpb, NH, HD, HD), lambda i: (i, 0, 0, 0)),
                   pl.BlockSpec((cpb, NH, CH, CH), lambda i: (i, 0, 0, 0))],
        out_shape=[jax.ShapeDtypeStruct((t, D), f32), jax.ShapeDtypeStruct((t // CH, NH, HD, HD), f32),
                   jax.ShapeDtypeStruct((t // CH, NH, CH, CH), bf16)],
        scratch_shapes=[pltpu.VMEM((NH, HD, HD), f32)],
        compiler_params=_cparams("arbitrary"),
    )(qn, kn, va, gates)


def _gdn_scan_bwd(qn, kn, va, gates, states, inverses, do, *, tt):
    t = qn.shape[0]
    nblk = t // tt
    cpb = tt // CH

    def body(q_ref, k_ref, v_ref, gt_ref, st_ref, p_ref, do_ref, dq_ref, dk_ref, dv_ref, dgt_ref, ds_scr):
        @pl.when(pl.program_id(0) == 0)
        def _():
            ds_scr[...] = jnp.zeros_like(ds_scr)

        cols = _head_cols()

        def rows_of(ci):
            return pl.ds(pl.multiple_of(ci * CH, CH), CH)

        def inputs(ci):
            gcs, bbs = _head_gates(gt_ref[rows_of(ci), :])
            return tuple(tuple(ref[rows_of(ci), cl] for cl in cols) for ref in (q_ref, k_ref, v_ref)) + (gcs, bbs)

        def step(j, carry):
            ci = cpb - 1 - j
            ps = tuple(p_ref[ci, h].astype(f32) for h in range(NH))

            def one(ins, ss):
                outs, _, s_end, _ = _chunk_group_fn([ins], ss, ps=ps)
                return outs[0], s_end

            _, vjp = jax.vjp(one, inputs(ci), tuple(st_ref[ci, h] for h in range(NH)))
            grads, ds = vjp((tuple(do_ref[rows_of(ci), cl] for cl in cols), tuple(ds_scr[h] for h in range(NH))))
            lane = lax.broadcasted_iota(jnp.int32, (CH, HD), 1)
            dgt = jnp.zeros((CH, HD), f32)
            for h in range(NH):
                for ref, g in zip((dq_ref, dk_ref, dv_ref), grads[:3]):
                    ref[rows_of(ci), cols[h]] = g[h]
                dgt = dgt + jnp.where(lane == NH + h, jnp.sum(grads[3][h], axis=-1, keepdims=True), 0.0)
                dgt = dgt + jnp.where(lane == h, jnp.sum(grads[4][h], axis=-1, keepdims=True), 0.0)
                ds_scr[h] = ds[h]
            dgt_ref[rows_of(ci), :] = dgt
            return carry

        lax.fori_loop(0, cpb, step, 0)

    blk = pl.BlockSpec((tt, D), lambda i: (nblk - 1 - i, 0))
    sblk = pl.BlockSpec((cpb, NH, HD, HD), lambda i: (nblk - 1 - i, 0, 0, 0))
    sds = jax.ShapeDtypeStruct((t, D), f32)
    gblk = pl.BlockSpec((tt, HD), lambda i: (nblk - 1 - i, 0))
    pblk = pl.BlockSpec((cpb, NH, CH, CH), lambda i: (nblk - 1 - i, 0, 0, 0))
    return pl.pallas_call(
        body, name="gdn_scan_bwd", grid=(nblk,),
        in_specs=[blk] * 3 + [gblk, sblk, pblk, blk],
        out_specs=[blk] * 3 + [gblk], out_shape=[sds] * 3 + [jax.ShapeDtypeStruct((t, HD), f32)],
        scratch_shapes=[pltpu.VMEM((NH, HD, HD), f32)],
        compiler_params=_cparams("arbitrary"),
    )(qn, kn, va, gates, states, inverses, do)


def _rms_heads(o):
    ons, rs = [], []
    for h in range(NH):
        a = o[:, h * HD:(h + 1) * HD]
        r = lax.rsqrt(jnp.mean(a * a, axis=-1, keepdims=True) + RMS_EPS)
        ons.append(a * r)
        rs.append(jnp.broadcast_to(r, a.shape))
    return jnp.concatenate(ons, axis=-1), jnp.concatenate(rs, axis=-1)


def _gdn_post_fwd(o, p_gz, ng_b, w_out, *, tt):
    t = o.shape[0]

    def body(o_ref, gz_ref, ng_ref, w_ref, og_ref, y_ref):
        on, _ = _rms_heads(o_ref[...])
        z, _ = _silu_and_grad(gz_ref[...])
        og = (on * ng_ref[...] * z).astype(bf16)
        og_ref[...] = og
        y_ref[...] = jnp.dot(og, w_ref[...], preferred_element_type=f32)

    blk = pl.BlockSpec((tt, D), lambda i: (i, 0))
    return pl.pallas_call(
        body, name="gdn_post_fwd", grid=(t // tt,),
        in_specs=[blk, blk, pl.BlockSpec((1, D), lambda i: (0, 0)), pl.BlockSpec((D, D), lambda i: (0, 0))],
        out_specs=[blk, blk], out_shape=[jax.ShapeDtypeStruct((t, D), bf16), jax.ShapeDtypeStruct((t, D), f32)],
        compiler_params=_cparams("parallel"),
    )(o, p_gz, ng_b, w_out)


def _gdn_post_bwd(o, p_gz, ng_b, w_out, dyg, *, tt):
    t = o.shape[0]

    def body(o_ref, gz_ref, ng_ref, w_ref, dyg_ref, do_ref, dgz_ref, dng_ref):
        @pl.when(pl.program_id(0) == 0)
        def _():
            dng_ref[...] = jnp.zeros_like(dng_ref)

        on, r = _rms_heads(o_ref[...])
        z, dz = _silu_and_grad(gz_ref[...])
        dog_v = lax.dot_general(dyg_ref[...], w_ref[...], (_NT, ((), ())), preferred_element_type=f32)
        ng = ng_ref[...]
        dgz_ref[...] = (dog_v * on * ng * dz).astype(bf16)
        dy = dog_v * z
        dng_all = jnp.sum(dy * on, axis=0, keepdims=True)
        dng = dng_all[:, 0:HD]
        for h in range(1, NH):
            dng = dng + dng_all[:, h * HD:(h + 1) * HD]
        dng_ref[0:1, :] += dng
        don = dy * ng
        prod = don * on
        parts = []
        for h in range(NH):
            sl = slice(h * HD, (h + 1) * HD)
            parts.append(don[:, sl] - on[:, sl] * jnp.mean(prod[:, sl], axis=-1, keepdims=True))
        do_ref[...] = r * jnp.concatenate(parts, axis=-1)

    blk = pl.BlockSpec((tt, D), lambda i: (i, 0))
    return pl.pallas_call(
        body, name="gdn_post_bwd", grid=(t // tt,),
        in_specs=[blk, blk, pl.BlockSpec((1, D), lambda i: (0, 0)), pl.BlockSpec((D, D), lambda i: (0, 0)), blk],
        out_specs=[blk, blk, pl.BlockSpec((8, HD), lambda i: (0, 0))],
        out_shape=[jax.ShapeDtypeStruct((t, D), f32), jax.ShapeDtypeStruct((t, D), bf16),
                   jax.ShapeDtypeStruct((8, HD), f32)],
        compiler_params=_cparams("arbitrary"),
    )(o, p_gz, ng_b, w_out, dyg)


def _merge(x, y_conf, y_gdn, p_gates, target, w_o, ln_g, ln_b, *, tt):
    t = x.shape[0]

    def body(x_ref, yc_ref, yg_ref, gc_ref, gg_ref, tg_ref, w_ref, g_ref, b_ref,
             loss_ref, dxd_ref, dyc_ref, dyg_ref, dpg_ref, h_ref, dz_ref, dvec_ref):
        @pl.when(pl.program_id(0) == 0)
        def _():
            loss_ref[...] = jnp.zeros_like(loss_ref)
            dvec_ref[...] = jnp.zeros_like(dvec_ref)

        sc = _sigmoid(gc_ref[...])
        sg = _sigmoid(gg_ref[...])
        yc = yc_ref[...]
        yg = yg_ref[...]
        h = (sc * yc + sg * yg).astype(bf16)
        h_ref[...] = h
        z = DN_ALPHA * x_ref[...] + jnp.dot(h, w_ref[...], preferred_element_type=f32)
        mu = jnp.mean(z, axis=-1, keepdims=True)
        cen = z - mu
        rstd = lax.rsqrt(jnp.mean(cen * cen, axis=-1, keepdims=True) + LN_EPS)
        xhat = cen * rstd
        err = xhat * g_ref[...] + b_ref[...] - tg_ref[...]
        loss_ref[...] += 0.5 / D * jnp.sum(err * err)
        dy = err * (1.0 / D)
        dvec_ref[0:1, :] += jnp.sum(dy * xhat, axis=0, keepdims=True)
        dvec_ref[1:2, :] += jnp.sum(dy, axis=0, keepdims=True)
        dxhat = dy * g_ref[...]
        dz = rstd * (dxhat - jnp.mean(dxhat, axis=-1, keepdims=True)
                     - xhat * jnp.mean(dxhat * xhat, axis=-1, keepdims=True))
        dxd_ref[...] = DN_ALPHA * dz
        dz_b = dz.astype(bf16)
        dz_ref[...] = dz_b
        dh = lax.dot_general(dz_b, w_ref[...], (_NT, ((), ())), preferred_element_type=f32)
        dyc_ref[...] = (dh * sc).astype(bf16)
        dyg_ref[...] = (dh * sg).astype(bf16)
        dpg_ref[:, 0:D] = (dh * yc * sc * (1.0 - sc)).astype(bf16)
        dpg_ref[:, D:] = (dh * yg * sg * (1.0 - sg)).astype(bf16)

    blk = pl.BlockSpec((tt, D), lambda i: (i, 0))
    wblk = pl.BlockSpec((D, D), lambda i: (0, 0))
    vec = pl.BlockSpec((1, D), lambda i: (0, 0))
    return pl.pallas_call(
        body, name="merge_norm_loss", grid=(t // tt,),
        in_specs=[blk, blk, blk, pl.BlockSpec((tt, D), lambda i: (i, 0)), pl.BlockSpec((tt, D), lambda i: (i, 1)),
                  blk, wblk, vec, vec],
        out_specs=[pl.BlockSpec((8, HD), lambda i: (0, 0)), blk, blk, blk,
                   pl.BlockSpec((tt, 2 * D), lambda i: (i, 0)), blk, blk, pl.BlockSpec((8, D), lambda i: (0, 0))],
        out_shape=[jax.ShapeDtypeStruct((8, HD), f32), jax.ShapeDtypeStruct((t, D), f32),
                   jax.ShapeDtypeStruct((t, D), bf16), jax.ShapeDtypeStruct((t, D), bf16),
                   jax.ShapeDtypeStruct((t, 2 * D), bf16), jax.ShapeDtypeStruct((t, D), bf16),
                   jax.ShapeDtypeStruct((t, D), bf16), jax.ShapeDtypeStruct((8, D), f32)],
        compiler_params=_cparams("arbitrary"),
    )(x, y_conf, y_gdn, p_gates, p_gates, target, w_o, ln_g, ln_b)


def _place():
    return lax.axis_index("x"), lax.axis_index("y"), lax.axis_index("c")


def _any_specs(n):
    return [pl.BlockSpec(memory_space=pl.ANY)] * n


def _sibling_merge(arrs, name, half_axes=None):
    k = len(arrs)

    def half_shape(a, ax):
        shape = list(a.shape)
        shape[ax] //= 2
        return tuple(shape)

    def body(*refs):
        a_refs, o_refs = refs[:k], refs[k:2 * k]
        send_sems, recv_sems = refs[2 * k:]
        x, y, c = _place()
        sends = []
        for i in range(k):
            src = a_refs[i]
            if half_axes is not None:
                n = a_refs[i].shape[half_axes[i]] // 2
                idx = [slice(None)] * len(a_refs[i].shape)
                idx[half_axes[i]] = pl.ds((1 - c) * n, n)
                src = a_refs[i].at[tuple(idx)]
            cp = pltpu.make_async_remote_copy(src_ref=src, dst_ref=o_refs[i], send_sem=send_sems.at[i],
                                              recv_sem=recv_sems.at[i], device_id=(x, y, 1 - c), device_id_type=MESH)
            cp.start()
            sends.append(cp)
        for cp in sends:
            cp.wait()

    shapes = [a.shape if half_axes is None else half_shape(a, ax) for a, ax in zip(arrs, half_axes or arrs)]
    return pl.pallas_call(
        body, name=name, in_specs=_any_specs(k), out_specs=_any_specs(k),
        out_shape=[jax.ShapeDtypeStruct(sh, a.dtype) for sh, a in zip(shapes, arrs)],
        scratch_shapes=[pltpu.SemaphoreType.DMA((k,)), pltpu.SemaphoreType.DMA((k,))],
    )(*arrs)


def _join_halves(mine, other, axis=-2):
    c = lax.axis_index("c")
    return jnp.concatenate([jnp.where(c == 0, mine, other), jnp.where(c == 0, other, mine)], axis=axis)


def _chip_exchange_ops(a_refs, o_refs, send_sems, recv_sems, local_sems, scatter):
    k = len(a_refs)
    x, y, c = _place()
    me = 2 * x + y
    peers = [(1 - x, y), (x, 1 - y), (1 - x, 1 - y)]

    def src(i, j):
        return a_refs[i].at[j] if scatter else a_refs[i]

    def copy(i, n, send_j, slot):
        px, py = peers[n]
        return pltpu.make_async_remote_copy(
            src_ref=src(i, send_j), dst_ref=o_refs[i].at[slot], send_sem=send_sems.at[3 * i + n],
            recv_sem=recv_sems.at[3 * i + n], device_id=(px, py, c), device_id_type=MESH)

    def owns():
        return [pltpu.make_async_copy(src(i, me), o_refs[i].at[me], local_sems.at[i]) for i in range(k)]

    def sends():
        return [copy(i, n, 2 * peers[n][0] + peers[n][1], me) for n in range(3) for i in range(k)]

    def start():
        for cp in owns() + sends():
            cp.start()

    def finish():
        for n in range(3):
            for i in range(k):
                copy(i, n, me, 2 * peers[n][0] + peers[n][1]).wait_recv()
        for cp in sends():
            cp.wait_send()
        for cp in owns():
            cp.wait()

    return start, finish


def _chip_exchange_shapes(arrs, scatter):
    return [jax.ShapeDtypeStruct((N_CHIPS,) + tuple(a.shape[1:] if scatter else a.shape), a.dtype) for a in arrs]


def _chip_exchange_sems(k):
    return [pltpu.SemaphoreType.DMA((3 * k,)), pltpu.SemaphoreType.DMA((3 * k,)), pltpu.SemaphoreType.DMA((k,))]


def _gather_halves(halves, wholes, name):
    kh, kw = len(halves), len(wholes)
    k = kh + kw

    def body(*refs):
        a_refs, got_refs, oth_refs = refs[:k], refs[k:2 * k], refs[2 * k:2 * k + kh]
        send_sems, recv_sems, local_sems, fwd_send_sems, fwd_recv_sems = refs[2 * k + kh:]
        x, y, c = _place()
        me = 2 * x + y
        start, _ = _chip_exchange_ops(a_refs, got_refs, send_sems, recv_sems, local_sems, False)
        slots = [me] + [2 * px + py for px, py in [(1 - x, y), (x, 1 - y), (1 - x, 1 - y)]]

        def forward(i, r):
            src = a_refs[i] if r == 0 else got_refs[i].at[slots[r]]
            return pltpu.make_async_remote_copy(
                src_ref=src, dst_ref=oth_refs[i].at[slots[r]], send_sem=fwd_send_sems.at[4 * i + r],
                recv_sem=fwd_recv_sems.at[4 * i + r], device_id=(x, y, 1 - c), device_id_type=MESH)

        def arrival(i, n):
            px, py = [(1 - x, y), (x, 1 - y), (1 - x, 1 - y)][n]
            return pltpu.make_async_remote_copy(
                src_ref=a_refs[i], dst_ref=got_refs[i].at[slots[n + 1]], send_sem=send_sems.at[3 * i + n],
                recv_sem=recv_sems.at[3 * i + n], device_id=(px, py, c), device_id_type=MESH)

        start()
        for i in range(kh):
            forward(i, 0).start()
        for n in range(3):
            for i in range(k):
                arrival(i, n).wait_recv()
                if i < kh:
                    forward(i, n + 1).start()
        for i in range(kh):
            for r in range(4):
                forward(i, r).wait()
        for n in range(3):
            for i in range(k):
                arrival(i, n).wait_send()
        for i in range(k):
            pltpu.make_async_copy(a_refs[i], got_refs[i].at[me], local_sems.at[i]).wait()

    arrs = list(halves) + list(wholes)
    shapes = _chip_exchange_shapes(arrs, False)
    out = pl.pallas_call(
        body, name=name, in_specs=_any_specs(k), out_specs=_any_specs(k + kh),
        out_shape=shapes + shapes[:kh],
        scratch_shapes=_chip_exchange_sems(k) + [pltpu.SemaphoreType.DMA((4 * kh,)), pltpu.SemaphoreType.DMA((4 * kh,))],
    )(*arrs)
    return [(out[i], out[k + i]) for i in range(kh)], out[kh:k]


def _pair_sum(g_all, got, c_arr, name, out_dtype, axis):
    rows, w = got.shape[1:]
    if axis == -2:
        blk, steps = (1, rows // 4, w), 4
        pick = lambda j, i, c_ref: (j, c_ref[0] * steps + i, 0)
        mine = lambda j, i, c_ref: (j, i, 0)
    else:
        blk, steps = (1, rows, LANE), w // LANE
        pick = lambda j, i, c_ref: (j, 0, c_ref[0] * steps + i)
        mine = lambda j, i, c_ref: (j, 0, i)

    def body(c_ref, a_ref, b_ref, o_ref):
        o_ref[...] = (a_ref[...] + b_ref[...]).astype(out_dtype)

    return pl.pallas_call(
        body, name=name,
        grid_spec=pltpu.PrefetchScalarGridSpec(
            num_scalar_prefetch=1, grid=(N_CHIPS, steps),
            in_specs=[pl.BlockSpec(blk, pick), pl.BlockSpec(blk, mine)], out_specs=pl.BlockSpec(blk, mine)),
        out_shape=jax.ShapeDtypeStruct(got.shape, out_dtype),
        compiler_params=_cparams("parallel", "parallel"),
    )(c_arr, g_all, got)


def _sum_slots(a, name):
    n, w = a.shape[1:]
    by_rows = n % 64 == 0
    in_blk = pl.BlockSpec((N_CHIPS, n // 4, w), lambda i: (0, i, 0)) if by_rows else pl.BlockSpec((N_CHIPS, n, LANE), lambda i: (0, 0, i))
    out_blk = pl.BlockSpec((n // 4, w), lambda i: (i, 0)) if by_rows else pl.BlockSpec((n, LANE), lambda i: (0, i))

    def body(a_ref, o_ref):
        o_ref[...] = ((a_ref[0].astype(f32) + a_ref[1].astype(f32)) + a_ref[2].astype(f32)) + a_ref[3].astype(f32)

    return pl.pallas_call(
        body, name=name, grid=(4 if by_rows else w // LANE,),
        in_specs=[in_blk], out_specs=out_blk,
        out_shape=jax.ShapeDtypeStruct((n, w), f32),
        compiler_params=_cparams("parallel"),
    )(a)


def _adamw(w, g, m, v, name):
    rows, width = w.shape
    by_rows = rows % 64 == 0
    c1 = 1.0 / (1.0 - ADAM_B1 ** ADAM_STEP)
    c2 = 1.0 / (1.0 - ADAM_B2 ** ADAM_STEP)

    def body(w_ref, g_ref, m_ref, v_ref, d_ref, mo_ref, vo_ref):
        g_v = g_ref[...]
        m_new = ADAM_B1 * m_ref[...] + (1.0 - ADAM_B1) * g_v
        v_new = ADAM_B2 * v_ref[...] + (1.0 - ADAM_B2) * (g_v * g_v)
        mo_ref[...] = m_new
        vo_ref[...] = v_new
        d_ref[...] = -ADAM_LR * ((m_new * c1) / (jnp.sqrt(v_new * c2) + ADAM_EPS) + ADAM_WD * w_ref[...])

    blk = pl.BlockSpec((rows // 8, width), lambda i: (i, 0)) if by_rows else pl.BlockSpec((rows, LANE), lambda i: (0, i))
    sds = jax.ShapeDtypeStruct((rows, width), f32)
    return pl.pallas_call(
        body, name=name, grid=(8 if by_rows else width // LANE,),
        in_specs=[blk] * 4, out_specs=[blk] * 3, out_shape=[sds] * 3,
        compiler_params=_cparams("parallel"),
    )(w, g, m, v)


R_DW = 3 * SQ_BLK
R_CW = R_DW + 8
R_VEC = R_CW + 8
R_SMALL = R_VEC + 8
REST_ROWS = 896


def _pack_small(conf_dw_w, gdn_conv_w, vecs, a_log, dt_bias, norm_g):
    dw = jnp.pad(conf_dw_w.reshape(-1), (0, 8 * D - KC * SQ_BLK)).reshape(8, D)
    cw = jnp.pad(gdn_conv_w.reshape(-1), (0, 5 * D)).reshape(8, D)
    vec = jnp.pad(jnp.stack(vecs), ((0, 3), (0, 0)))
    small = jnp.pad(jnp.concatenate([a_log, dt_bias, norm_g]), (0, D - 2 * NH - HD)).reshape(1, D)
    return jnp.pad(jnp.concatenate([dw, cw, vec, small], axis=0), ((0, REST_ROWS - R_SMALL - 1), (0, 0)))


def _pack_rest(conf_w_out, gdn_w_out, w_o, small):
    return jnp.concatenate([conf_w_out, gdn_w_out, w_o, small], axis=0)


def _unpack_rest(p):
    conf_dw_w = p[R_DW:R_DW + 8].reshape(-1)[:KC * SQ_BLK].reshape(KC, SQ_BLK)
    gdn_conv_w = p[R_CW:R_CW + 3].reshape(KG, 3 * SQ_BLK)
    small = p[R_SMALL]
    return dict(conf_w_out=p[0:SQ_BLK], gdn_w_out=p[SQ_BLK:2 * SQ_BLK], w_o=p[2 * SQ_BLK:R_DW],
                conf_dw_w=conf_dw_w, gdn_conv_w=gdn_conv_w, conf_dw_b=p[R_VEC], conf_ln_g=p[R_VEC + 1],
                conf_ln_b=p[R_VEC + 2], post_ln_g=p[R_VEC + 3], post_ln_b=p[R_VEC + 4],
                gdn_A_log=small[0:NH], gdn_dt_bias=small[NH:2 * NH], gdn_norm_g=small[2 * NH:2 * NH + HD])


_WEIGHT_ORDER = ("w_in", "conf_dw_w", "conf_dw_b", "conf_ln_g", "conf_ln_b", "conf_w_out", "gdn_conv_w",
                 "gdn_A_log", "gdn_dt_bias", "gdn_norm_g", "gdn_w_out", "w_o", "post_ln_g", "post_ln_b")


def _gather_weights(w_in, conf_w_out, gdn_w_out, w_o, conf_dw_w, gdn_conv_w):
    c = lax.axis_index("c")
    sq = jnp.concatenate([conf_w_out, gdn_w_out, w_o], axis=0).astype(bf16)
    w_half = lax.dynamic_slice_in_dim(w_in.T.astype(bf16), c * (D // 2), D // 2, axis=1)
    sq_half = lax.dynamic_slice_in_dim(sq, c * (sq.shape[0] // 2), sq.shape[0] // 2, axis=0)
    small = jnp.concatenate([jnp.pad(conf_dw_w.reshape(-1), (0, 8 * D - KC * SQ_BLK)).reshape(8, D),
                             jnp.pad(gdn_conv_w.reshape(-1), (0, 5 * D)).reshape(8, D)], axis=0)
    ((w_mine, w_other), (sq_mine, sq_other)), (small_all,) = _gather_halves([w_half, sq_half], [small], "weight_gather")
    w_t = _join_halves(w_mine, w_other, axis=-1).reshape(W_IN_COLS, D)
    sq4 = _join_halves(sq_mine, sq_other)
    sq_full = [sq4[:, n * SQ_BLK:(n + 1) * SQ_BLK].reshape(D, D) for n in range(3)]
    dw_full = small_all[:, 0:8].reshape(N_CHIPS, 8 * D)[:, :KC * SQ_BLK].reshape(N_CHIPS, KC, SQ_BLK)
    dw_full = dw_full.transpose(1, 0, 2).reshape(KC, D)
    cw_full = small_all[:, 8:11].reshape(N_CHIPS, KG, 3 * SQ_BLK).transpose(1, 0, 2).reshape(KG, 3 * D)
    return w_t, sq_full[0], sq_full[1], sq_full[2], dw_full, cw_full


def _pair_sums(g_w, g_rest):
    c_arr = lax.axis_index("c").astype(jnp.int32).reshape(1)
    got_w, got_r = _sibling_merge([g_w, g_rest], "grad_sibling_halves", half_axes=[-1, -2])
    pair_w = _pair_sum(g_w, got_w, c_arr, "grad_pair_sum_w_in", bf16, -1)
    pair_r = _pair_sum(g_rest, got_r, c_arr, "grad_pair_sum_rest", f32, -2)
    return pair_w, pair_r


def _chip_sums(all_w, all_r):
    tot_w, tot_r = _sum_slots(all_w, "grad_chip_sum_w_in"), _sum_slots(all_r, "grad_chip_sum_rest")
    oth_w, oth_r = _sibling_merge([tot_w, tot_r], "grad_sibling_result")
    return _join_halves(tot_w, oth_w, axis=-1), _join_halves(tot_r, oth_r)


def _local_step(x2, tgt, w_t, wc_out, wg_out, wo_full, dw_full, cw_full, conf_dw_b, conf_ln_g, conf_ln_b,
                gdn_A_log, gdn_dt_bias, gdn_norm_g, post_ln_g, post_ln_b):
    t = x2.shape[0]
    tt = min(TOKEN_TILE, t)
    tm = min(MM_TILE, t)

    w_conv, w_qkv, w_gz = w_t[0:3 * D], w_t[3 * D:6 * D], w_t[6 * D:7 * D]
    w_gates = w_t[7 * D + 2 * NH:]
    dw_pad = jnp.pad(dw_full, ((0, HALO_C - KC), (0, 0)))
    cw_pad = jnp.pad(cw_full, ((0, 8 - KG), (0, 0)))
    row = lambda v: v.reshape(1, D)
    alog_v = jnp.pad(gdn_A_log, (NH, HD - 2 * NH)).reshape(1, HD)
    dt_v = jnp.pad(gdn_dt_bias, (NH, HD - 2 * NH)).reshape(1, HD)
    ng_b = row(jnp.tile(gdn_norm_g, NH))
    w_ba = jnp.pad(w_t[7 * D:7 * D + 2 * NH], ((0, HD - 2 * NH), (0, 0)))

    x_b = x2.astype(bf16)

    p_conv = _mm_multi([x_b], [w_conv], out_dtype=f32, tm=tm, tn=MM_TILE, name="proj_conv", rhs_t=True)
    p_qkv = _mm_multi([x_b], [w_qkv], out_dtype=f32, tm=tm, tn=MM_TILE, name="proj_qkv", rhs_t=True)
    p_gz = _mm_multi([x_b], [w_gz], out_dtype=f32, tm=tm, tn=MM_TILE, name="proj_gz", rhs_t=True)
    p_gates = _mm_multi([x_b], [w_gates], out_dtype=f32, tm=tm, tn=MM_TILE, name="proj_gates", rhs_t=True)
    p_ba = _mm_multi([x_b], [w_ba], out_dtype=f32, tm=tm, tn=HD, name="proj_ba", rhs_t=True)

    u, a1, y_conf = _conv_fwd(p_conv, dw_pad, row(conf_dw_b), row(conf_ln_g), row(conf_ln_b), wc_out, tt=tt)

    qn, kn, va, gates, pre_qkv = _gdn_pre_fwd(p_qkv, p_ba, cw_pad, alog_v, dt_v, tt=tt)
    o, states, inverses = _gdn_scan_fwd(qn, kn, va, gates, tt=tt)
    og, y_gdn = _gdn_post_fwd(o, p_gz, ng_b, wg_out, tt=min(2 * TOKEN_TILE, t))

    loss_blk, dxd, dyc, dyg, dp_gates, h, dz, dpost = _merge(
        x2, y_conf, y_gdn, p_gates, tgt, wo_full, row(post_ln_g), row(post_ln_b), tt=tt)

    d_wo = _mm_kloop(h, dz, tm=D, tn=MM_TILE, tk=min(MM_K_TILE, t), name="grad_w_o")
    du = _mm_multi([dyc], [wc_out], out_dtype=f32, tm=tm, tn=MM_TILE, name="conf_out_bwd", rhs_t=True)
    d_wc = _mm_kloop(u, dyc, tm=D, tn=MM_TILE, tk=min(MM_K_TILE, t), name="grad_conf_w_out")
    d_wg = _mm_kloop(og, dyg, tm=D, tn=MM_TILE, tk=min(MM_K_TILE, t), name="grad_gdn_w_out")

    dp_conv, d_dww, dconv_vec = _conv_bwd(p_conv, a1, du, dw_pad, row(conf_ln_g), row(conf_ln_b), tt=tt)

    do, dp_gz, dng = _gdn_post_bwd(o, p_gz, ng_b, wg_out, dyg, tt=min(2 * TOKEN_TILE, t))
    dqn, dkn, dva, dgates = _gdn_scan_bwd(qn, kn, va, gates, states, inverses, do, tt=tt)
    dp_qkv, dp_ba, d_cw, d_ad = _gdn_pre_bwd(p_qkv, p_ba, pre_qkv, cw_pad, alog_v, dt_v, dqn, dkn, dva, dgates, tt=tt)
    dp_ba_b = dp_ba.astype(bf16)

    grad_x_factors = ([dp_conv, dp_qkv, dp_gz, dp_gates, dp_ba_b], [w_conv, w_qkv, w_gz, w_gates, w_ba], dxd)

    tk = min(MM_K_TILE, t)
    d_w_conv = _mm_kloop(dp_conv, x_b, tm=MM_TILE, tn=D, tk=tk, name="grad_w_in_conv")
    d_w_qkv = _mm_kloop(dp_qkv, x_b, tm=MM_TILE, tn=D, tk=tk, name="grad_w_in_qkv")
    d_w_gz = _mm_kloop(dp_gz, x_b, tm=MM_TILE, tn=D, tk=tk, name="grad_w_in_gz")
    d_w_gates = _mm_kloop(dp_gates, x_b, tm=MM_TILE, tn=D, tk=tk, name="grad_w_in_gates")
    d_w_ba = _mm_kloop(dp_ba_b, x_b, tm=HD, tn=D, tk=tk, name="grad_w_in_ba")
    d_w_in = jnp.concatenate([d_w_conv, d_w_qkv, d_w_gz, d_w_ba[:2 * NH], d_w_gates], axis=0).reshape(
        N_CHIPS, W_IN_BLK, D)

    return (loss_blk[0, 0], grad_x_factors, d_w_in, d_wc, d_wg, d_wo, d_dww, d_cw, dconv_vec, dpost, d_ad, dng)


def kernel(x, w_in, conf_dw_w, conf_dw_b, conf_ln_g, conf_ln_b, conf_w_out, gdn_conv_w, gdn_A_log, gdn_dt_bias, gdn_norm_g, gdn_w_out, w_o, post_ln_g, post_ln_b, loss_target, m_w_in, m_conf_dw_w, m_conf_dw_b, m_conf_ln_g, m_conf_ln_b, m_conf_w_out, m_gdn_conv_w, m_gdn_A_log, m_gdn_dt_bias, m_gdn_norm_g, m_gdn_w_out, m_w_o, m_post_ln_g, m_post_ln_b, v_w_in, v_conf_dw_w, v_conf_dw_b, v_conf_ln_g, v_conf_ln_b, v_conf_w_out, v_gdn_conv_w, v_gdn_A_log, v_gdn_dt_bias, v_gdn_norm_g, v_gdn_w_out, v_w_o, v_post_ln_g, v_post_ln_b):
    x2 = x.reshape(x.shape[-2], D)
    tgt = loss_target.reshape(x2.shape)
    w_t, wc_out, wg_out, wo_full, dw_full, cw_full = _gather_weights(
        w_in, conf_w_out, gdn_w_out, w_o, conf_dw_w, gdn_conv_w)
    (loss_part, grad_x_factors, d_w_in, d_wc, d_wg, d_wo, d_dww, d_cw, dconv_vec, dpost, d_ad, dng) = _local_step(
        x2, tgt, w_t, wc_out, wg_out, wo_full, dw_full, cw_full, conf_dw_b, conf_ln_g, conf_ln_b,
        gdn_A_log, gdn_dt_bias, gdn_norm_g, post_ln_g, post_ln_b)
    loss = lax.psum(loss_part, ("x", "y", "c"))

    dww_c = d_dww[:KC].reshape(KC, N_CHIPS, SQ_BLK)
    dcw_c = d_cw[:KG].reshape(KG, N_CHIPS, 3 * SQ_BLK)
    vecs = [dconv_vec[0], dconv_vec[1], dconv_vec[2], dpost[0], dpost[1]]
    g_rest = jnp.stack([
        _pack_rest(d_wc[j * SQ_BLK:(j + 1) * SQ_BLK], d_wg[j * SQ_BLK:(j + 1) * SQ_BLK], d_wo[j * SQ_BLK:(j + 1) * SQ_BLK],
                   _pack_small(dww_c[:, j], dcw_c[:, j], vecs, d_ad[0, NH:2 * NH], d_ad[1, NH:2 * NH], dng[0]))
        for j in range(N_CHIPS)])
    pair_w, pair_r = _pair_sums(d_w_in, g_rest)
    grad_x, all_w, all_r = _mm_multi(*grad_x_factors, out_dtype=f32, tm=min(GRAD_X_TILE[0], x2.shape[0]), tn=GRAD_X_TILE[1],
                                     name="grad_x_and_chip_scatter", scatter=[pair_w, pair_r])
    g_w_in, g_rest = _chip_sums(all_w, all_r)

    def rest_of(w_c, w_g, w_oo, dw, cw, b1, g1, b2, g2, b3, a_log, dt_bias, norm_g):
        return _pack_rest(w_c, w_g, w_oo, _pack_small(dw, cw, [b1, g1, b2, g2, b3], a_log, dt_bias, norm_g))

    w_r = rest_of(conf_w_out, gdn_w_out, w_o, conf_dw_w, gdn_conv_w, conf_dw_b, conf_ln_g, conf_ln_b,
                  post_ln_g, post_ln_b, gdn_A_log, gdn_dt_bias, gdn_norm_g)
    m_r = rest_of(m_conf_w_out, m_gdn_w_out, m_w_o, m_conf_dw_w, m_gdn_conv_w, m_conf_dw_b, m_conf_ln_g, m_conf_ln_b,
                  m_post_ln_g, m_post_ln_b, m_gdn_A_log, m_gdn_dt_bias, m_gdn_norm_g)
    v_r = rest_of(v_conf_w_out, v_gdn_w_out, v_w_o, v_conf_dw_w, v_gdn_conv_w, v_conf_dw_b, v_conf_ln_g, v_conf_ln_b,
                  v_post_ln_g, v_post_ln_b, v_gdn_A_log, v_gdn_dt_bias, v_gdn_norm_g)
    upd_w_in = _adamw(w_in.T, g_w_in, m_w_in.T, v_w_in.T, "adamw_w_in")
    upd_rest = _adamw(w_r, g_rest, m_r, v_r, "adamw_rest")

    out = [loss, grad_x.reshape(x.shape)]
    for big, rest in zip((g_w_in,) + tuple(upd_w_in), (g_rest,) + tuple(upd_rest)):
        d = dict(_unpack_rest(rest), w_in=big.T)
        out += [d[n] for n in _WEIGHT_ORDER]
    return tuple(out)
```

```python
import jax
import jax.numpy as jnp
from jax import lax
from jax.experimental import pallas as pl
from jax.experimental.pallas import tpu as pltpu

f32 = jnp.float32
bf16 = jnp.bfloat16
HI = lax.Precision.HIGHEST
MESH = pl.DeviceIdType.MESH

D = 1024
NH = 8
HD = 128
CH = 64
KC = 31
KG = 4
HALO_C = 32
HALO_G = 8
LANE = 128
STRIP = 32
N_SHIFT = 7
LN_EPS = 1e-5
RMS_EPS = 1e-6
L2_EPS = 1e-6
DN_ALPHA = 2.0 ** 0.25
N_CHIPS = 4
W_IN_COLS = 9232
W_IN_BLK = W_IN_COLS // N_CHIPS
SQ_BLK = D // N_CHIPS
VMEM_LIMIT = 52 * 1024 * 1024
MM_TILE = 1024
MM_K_TILE = 2048
GRAD_X_TILE = (256, 512)
TOKEN_TILE = 256
SCAN_GROUP = 4

ADAM_LR = 0.001
ADAM_B1 = 0.9
ADAM_B2 = 0.999
ADAM_EPS = 1e-08
ADAM_WD = 0.01
ADAM_STEP = 10


def _sigmoid(x):
    return 1.0 / (1.0 + jnp.exp(-x))


def _silu_and_grad(x):
    s = _sigmoid(x)
    return x * s, s * (1.0 + x * (1.0 - s))


_NN = ((1,), (0,))
_NT = ((1,), (1,))
_TN = ((0,), (0,))


def _cparams(*sem):
    return pltpu.CompilerParams(dimension_semantics=sem, vmem_limit_bytes=VMEM_LIMIT)


def _mm_multi(a_list, b_list, addend=None, *, out_dtype, tm, tn, name, rhs_t=False, scatter=()):
    n_pairs = len(a_list)
    m = a_list[0].shape[0]
    n = b_list[0].shape[0 if rhs_t else 1]
    has_add = addend is not None
    dims = (_NT if rhs_t else _NN, ((), ()))
    n_in = 2 * n_pairs + has_add
    k = len(scatter)
    grid = (n // tn, m // tm)

    def body(*refs):
        a_refs = refs[:n_pairs]
        b_refs = refs[n_pairs:2 * n_pairs]
        o_ref = refs[n_in + k]
        if k:
            start, finish = _chip_exchange_ops(refs[n_in:n_in + k], refs[n_in + k + 1:n_in + 2 * k + 1],
                                               *refs[n_in + 2 * k + 1:], True)
            step = pl.program_id(0) * grid[1] + pl.program_id(1)
            pl.when(step == 0)(start)
        acc = None
        for a_ref, b_ref in zip(a_refs, b_refs):
            p = lax.dot_general(a_ref[...].astype(bf16), b_ref[...].astype(bf16), dims, preferred_element_type=f32)
            acc = p if acc is None else acc + p
        if has_add:
            acc = acc + refs[2 * n_pairs][...]
        o_ref[...] = acc.astype(out_dtype)
        if k:
            pl.when(step == grid[0] * grid[1] - 1)(finish)

    in_specs = [pl.BlockSpec((tm, a.shape[1]), lambda j, i: (i, 0)) for a in a_list]
    if rhs_t:
        in_specs += [pl.BlockSpec((tn, b.shape[1]), lambda j, i: (j, 0)) for b in b_list]
    else:
        in_specs += [pl.BlockSpec((b.shape[0], tn), lambda j, i: (0, j)) for b in b_list]
    args = list(a_list) + list(b_list)
    if has_add:
        in_specs.append(pl.BlockSpec((tm, tn), lambda j, i: (i, j)))
        args.append(addend)
    out = pl.pallas_call(
        body, name=name, grid=grid,
        in_specs=in_specs + _any_specs(k), out_specs=[pl.BlockSpec((tm, tn), lambda j, i: (i, j))] + _any_specs(k),
        out_shape=[jax.ShapeDtypeStruct((m, n), out_dtype)] + _chip_exchange_shapes(scatter, True),
        scratch_shapes=_chip_exchange_sems(k) if k else [],
        compiler_params=_cparams("arbitrary", "arbitrary") if k else _cparams("parallel", "parallel"),
    )(*args, *scatter)
    return out if k else out[0]


def _mm_kloop(a, b, *, tm, tn, tk, name):
    k, m = a.shape
    n = b.shape[1]
    nk = k // tk

    def body(a_ref, b_ref, o_ref):
        @pl.when(pl.program_id(2) == 0)
        def _():
            o_ref[...] = jnp.zeros_like(o_ref)
        o_ref[...] += lax.dot_general(a_ref[...].astype(bf16), b_ref[...].astype(bf16), (_TN, ((), ())),
                                      preferred_element_type=f32)

    return pl.pallas_call(
        body, name=name, grid=(n // tn, m // tm, nk),
        in_specs=[pl.BlockSpec((tk, tm), lambda j, i, kk: (kk, i)), pl.BlockSpec((tk, tn), lambda j, i, kk: (kk, j))],
        out_specs=pl.BlockSpec((tm, tn), lambda j, i, kk: (i, j)),
        out_shape=jax.ShapeDtypeStruct((m, n), f32),
        compiler_params=_cparams("parallel", "parallel", "arbitrary"),
    )(a, b)


def _shift_copies(src_ref, sh_ref, n, shifts=tuple(range(1, 8))):
    for i, b in enumerate(shifts):
        sh_ref[i, 0:n, :] = src_ref[pl.ds(b, n), :]


def _by_residue(offs):
    groups = {}
    for k, off in enumerate(offs):
        groups.setdefault(off % 8, []).append((k, off // 8))
    return groups


def _slab(src_ref, sh_ref, shifts, b, r0, n, lanes):
    ref = src_ref if b == 0 else sh_ref.at[shifts.index(b)]
    return ref[r0:r0 + n, lanes]


def _tap_conv(out_ref, n_rows, src_ref, sh_ref, w_ref, offs, bias_ref=None, shifts=tuple(range(1, 8))):
    groups = _by_residue(offs)
    for j in range(D // LANE):
        lanes = slice(j * LANE, (j + 1) * LANE)
        wv = [w_ref[k:k + 1, lanes] for k in range(len(offs))]
        for r0 in range(0, n_rows, STRIP):
            n = min(STRIP, n_rows - r0)
            accs = [jnp.zeros((n, LANE), f32) if bias_ref is None else jnp.broadcast_to(bias_ref[0:1, lanes], (n, LANE)),
                    jnp.zeros((n, LANE), f32)]
            m = 0
            for b, taps in groups.items():
                a_lo = min(a for _, a in taps)
                a_hi = max(a for _, a in taps)
                wide = _slab(src_ref, sh_ref, shifts, b, r0 + 8 * a_lo, 8 * (a_hi - a_lo) + n, lanes)
                for k, a in taps:
                    accs[m % 2] = accs[m % 2] + wv[k] * wide[8 * (a - a_lo):8 * (a - a_lo) + n]
                    m += 1
            out_ref[r0:r0 + n, lanes] = accs[0] + accs[1]


def _tap_corr(dw_ref, n_rows, lhs_ref, src_ref, sh_ref, offs, shifts=tuple(range(1, 8))):
    groups = _by_residue(offs)
    for j in range(D // LANE):
        lanes = slice(j * LANE, (j + 1) * LANE)
        accs = [jnp.zeros((8, LANE), f32) for _ in offs]
        for r0 in range(0, n_rows, STRIP):
            n = min(STRIP, n_rows - r0)
            d = lhs_ref[r0:r0 + n, lanes]
            for b, taps in groups.items():
                a_lo = min(a for _, a in taps)
                a_hi = max(a for _, a in taps)
                wide = _slab(src_ref, sh_ref, shifts, b, r0 + 8 * a_lo, 8 * (a_hi - a_lo) + n, lanes)
                for k, a in taps:
                    prod = d * wide[8 * (a - a_lo):8 * (a - a_lo) + n]
                    part = prod[0:8]
                    for q in range(1, n // 8):
                        part = part + prod[8 * q:8 * q + 8]
                    accs[k] = accs[k] + part
        for k in range(len(offs)):
            dw_ref[k:k + 1, lanes] += jnp.sum(accs[k], axis=0, keepdims=True)


_FWD_OFFS = [HALO_C - (KC - 1) + k for k in range(KC)]
_BWD_OFFS = [KC - 1 - k for k in range(KC)]


def _norm_act(a1, cz, g_ref, bb_ref):
    mu = jnp.mean(a1, axis=-1, keepdims=True)
    cen = a1 - mu
    var = jnp.mean(cen * cen, axis=-1, keepdims=True)
    rstd = lax.rsqrt(var + LN_EPS)
    xhat = cen * rstd
    ln = xhat * g_ref[...] + bb_ref[...]
    s, ds = _silu_and_grad(ln)
    zc, dzc = _silu_and_grad(cz)
    return xhat, rstd, s, ds, zc, dzc


def _conv_fwd(p_conv, dw_w, dw_b, ln_g, ln_b, w_out, *, tt):
    t = p_conv.shape[0]
    hb = tt // HALO_C

    def body(cv_ref, cg_ref, cz_ref, cvh_ref, cgh_ref, w_ref, b_ref, g_ref, bb_ref, wo_ref,
             u_ref, a1_ref, y_ref, ext_ref, sh_ref):
        first = pl.program_id(0) == 0
        halo = cvh_ref[...] * _sigmoid(cgh_ref[...])
        ext_ref[0:HALO_C, :] = jnp.where(first, 0.0, halo)
        ext_ref[HALO_C:, :] = cv_ref[...] * _sigmoid(cg_ref[...])
        _shift_copies(ext_ref, sh_ref, tt + HALO_C - 8)
        _tap_conv(a1_ref, tt, ext_ref, sh_ref, w_ref, _FWD_OFFS, b_ref)
        _, _, s, _, zc, _ = _norm_act(a1_ref[...], cz_ref[...], g_ref, bb_ref)
        u = (s * zc).astype(bf16)
        u_ref[...] = u
        y_ref[...] = jnp.dot(u, wo_ref[...], preferred_element_type=f32)

    def main(col):
        return pl.BlockSpec((tt, D), lambda i: (i, col))

    def prev(col):
        return pl.BlockSpec((HALO_C, D), lambda i: (jnp.maximum(i * hb - 1, 0), col))

    vec = pl.BlockSpec((1, D), lambda i: (0, 0))
    return pl.pallas_call(
        body, name="conv_fwd", grid=(t // tt,),
        in_specs=[main(0), main(1), main(2), prev(0), prev(1),
                  pl.BlockSpec((HALO_C, D), lambda i: (0, 0)), vec, vec, vec, pl.BlockSpec((D, D), lambda i: (0, 0))],
        out_specs=[pl.BlockSpec((tt, D), lambda i: (i, 0))] * 3,
        out_shape=[jax.ShapeDtypeStruct((t, D), bf16), jax.ShapeDtypeStruct((t, D), f32),
                   jax.ShapeDtypeStruct((t, D), f32)],
        scratch_shapes=[pltpu.VMEM((tt + HALO_C, D), f32), pltpu.VMEM((N_SHIFT, tt + HALO_C - 8, D), f32)],
        compiler_params=_cparams("parallel"),
    )(p_conv, p_conv, p_conv, p_conv, p_conv, dw_w, dw_b, ln_g, ln_b, w_out)


def _conv_bwd(p_conv, a1, du, dw_w, ln_g, ln_b, *, tt):
    t = p_conv.shape[0]
    hb = tt // HALO_C
    n_tiles = t // tt
    last_hb = t // HALO_C - 1
    ne = tt + HALO_C

    def body(cv_ref, cg_ref, cz_ref, a1_ref, du_ref, cvp_ref, cgp_ref, czn_ref, a1n_ref, dun_ref,
             w_ref, g_ref, bb_ref, dp_ref, dww_ref, dvec_ref, ext_ref, sh_ref, da1_ref, da0_ref):
        i = pl.program_id(0)
        first = i == 0
        last = i == n_tiles - 1

        @pl.when(first)
        def _():
            dww_ref[...] = jnp.zeros_like(dww_ref)
            dvec_ref[...] = jnp.zeros_like(dvec_ref)

        sig = _sigmoid(cg_ref[...])
        ext_ref[0:HALO_C, :] = jnp.where(first, 0.0, cvp_ref[...] * _sigmoid(cgp_ref[...]))
        ext_ref[HALO_C:, :] = cv_ref[...] * sig
        a1_all = jnp.concatenate([a1_ref[...], a1n_ref[...]], axis=0)
        cz = jnp.concatenate([cz_ref[...], czn_ref[...]], axis=0)
        du_all = jnp.concatenate([du_ref[...], jnp.where(last, 0.0, dun_ref[...])], axis=0)
        xhat, rstd, s, ds, zc, dzc = _norm_act(a1_all, cz, g_ref, bb_ref)
        dln = du_all * zc * ds
        dxhat = dln * g_ref[...]
        da1 = rstd * (dxhat - jnp.mean(dxhat, axis=-1, keepdims=True)
                      - xhat * jnp.mean(dxhat * xhat, axis=-1, keepdims=True))
        da1_ref[...] = da1
        dcz = (du_all * s * dzc)[:tt]
        dvec_ref[0:1, :] += jnp.sum(da1[:tt], axis=0, keepdims=True)
        dvec_ref[1:2, :] += jnp.sum((dln * xhat)[:tt], axis=0, keepdims=True)
        dvec_ref[2:3, :] += jnp.sum(dln[:tt], axis=0, keepdims=True)
        _shift_copies(ext_ref, sh_ref, ne - 8)
        _tap_corr(dww_ref, tt, da1_ref, ext_ref, sh_ref, _FWD_OFFS)
        _shift_copies(da1_ref, sh_ref, ne - 8)
        _tap_conv(da0_ref, tt, da1_ref, sh_ref, w_ref, _BWD_OFFS)
        da0 = da0_ref[...]
        cv = cv_ref[...]
        dp_ref[:, 0:D] = (da0 * sig).astype(bf16)
        dp_ref[:, D:2 * D] = (da0 * cv * sig * (1.0 - sig)).astype(bf16)
        dp_ref[:, 2 * D:] = dcz.astype(bf16)

    def main(col):
        return pl.BlockSpec((tt, D), lambda i: (i, col))

    def prev(col):
        return pl.BlockSpec((HALO_C, D), lambda i: (jnp.maximum(i * hb - 1, 0), col))

    def nxt(col):
        return pl.BlockSpec((HALO_C, D), lambda i: (jnp.minimum((i + 1) * hb, last_hb), col))

    vec = pl.BlockSpec((1, D), lambda i: (0, 0))
    return pl.pallas_call(
        body, name="conv_bwd", grid=(n_tiles,),
        in_specs=[main(0), main(1), main(2), main(0), main(0), prev(0), prev(1), nxt(2), nxt(0), nxt(0),
                  pl.BlockSpec((HALO_C, D), lambda i: (0, 0)), vec, vec],
        out_specs=[pl.BlockSpec((tt, 3 * D), lambda i: (i, 0)),
                   pl.BlockSpec((HALO_C, D), lambda i: (0, 0)),
                   pl.BlockSpec((8, D), lambda i: (0, 0))],
        out_shape=[jax.ShapeDtypeStruct((t, 3 * D), bf16), jax.ShapeDtypeStruct((HALO_C, D), f32),
                   jax.ShapeDtypeStruct((8, D), f32)],
        scratch_shapes=[pltpu.VMEM((ne, D), f32), pltpu.VMEM((N_SHIFT, ne - 8, D), f32),
                        pltpu.VMEM((ne, D), f32), pltpu.VMEM((tt, D), f32)],
        compiler_params=_cparams("arbitrary"),
    )(p_conv, p_conv, p_conv, a1, du, p_conv, p_conv, p_conv, a1, du, dw_w, ln_g, ln_b)


def _dot_hi(a, b):
    return lax.dot_general(a, b, (((1,), (0,)), ((), ())), precision=HI, preferred_element_type=f32)


def _chunk_tri(n, lower):
    r = lax.broadcasted_iota(jnp.int32, (n, n), 0)
    c = lax.broadcasted_iota(jnp.int32, (n, n), 1)
    tri = (r >= c) if lower else (r <= c)
    return jnp.where(tri & (r // CH == c // CH), 1.0, 0.0).astype(f32)


def _softplus_and_sigmoid(x):
    e = jnp.exp(-jnp.abs(x))
    log1p = jnp.where(e < 1e-2, e * (1.0 - e * (0.5 - e * (1.0 / 3.0 - 0.25 * e))), jnp.log(1.0 + e))
    return jnp.maximum(x, 0.0) + log1p, _sigmoid(x)


_G_FWD_OFFS = [HALO_G - (KG - 1) + k for k in range(KG)]
_G_FWD_SHIFTS = (5, 6, 7)
_G_BWD_OFFS = [KG - 1 - k for k in range(KG)]
_G_BWD_SHIFTS = (1, 2, 3)


def _gdn_short_conv(pre_ref, ext_ref, sh_ref, n_rows, w_ref):
    _shift_copies(ext_ref, sh_ref, n_rows, _G_FWD_SHIFTS)
    _tap_conv(pre_ref, n_rows, ext_ref, sh_ref, w_ref, _G_FWD_OFFS, shifts=_G_FWD_SHIFTS)
    return pre_ref[...]


def _l2norm_heads(act, scale):
    outs, rs = [], []
    for h in range(NH):
        a = act[:, h * HD:(h + 1) * HD]
        r = lax.rsqrt(jnp.sum(a * a, axis=-1, keepdims=True) + L2_EPS)
        outs.append(a * (r * scale))
        rs.append(jnp.broadcast_to(r, a.shape))
    return jnp.concatenate(outs, axis=-1), jnp.concatenate(rs, axis=-1)


def _gate_math(ba, al_ref, dt_ref):
    lane = lax.broadcasted_iota(jnp.int32, ba.shape, 1)
    is_b = lane < NH
    is_a = (lane >= NH) & (lane < 2 * NH)
    sp, sg = _softplus_and_sigmoid(ba + dt_ref[...])
    neg_a = -jnp.exp(al_ref[...])
    return is_b, is_a, _sigmoid(ba), neg_a * sp, sg, neg_a


def _gdn_pre_fwd(p_qkv, p_ba, cw, alog_v, dt_v, *, tt):
    t = p_qkv.shape[0]
    hb = tt // HALO_G

    def body(q_ref, k_ref, v_ref, qh_ref, kh_ref, vh_ref, ba_ref, wq_ref, wk_ref, wv_ref, al_ref, dt_ref,
             qn_ref, kn_ref, va_ref, gt_ref, pre_ref, ext_ref, sh_ref):
        first = pl.program_id(0) == 0

        def conv_act(x_ref, xh_ref, w_ref, col):
            ext_ref[0:HALO_G, :] = jnp.where(first, 0.0, xh_ref[...])
            ext_ref[HALO_G:, :] = x_ref[...]
            pre = _gdn_short_conv(pre_ref.at[:, col * D:(col + 1) * D], ext_ref, sh_ref, tt, w_ref)
            return pre * _sigmoid(pre)

        qn_ref[...] = _l2norm_heads(conv_act(q_ref, qh_ref, wq_ref, 0), HD ** -0.5)[0]
        kn_ref[...] = _l2norm_heads(conv_act(k_ref, kh_ref, wk_ref, 1), 1.0)[0]
        va_ref[...] = conv_act(v_ref, vh_ref, wv_ref, 2)
        is_b, is_a, beta, g, _, _ = _gate_math(ba_ref[...], al_ref, dt_ref)
        gc = _dot_hi(_chunk_tri(tt, lower=True), jnp.where(is_a, g, 0.0))
        gt_ref[...] = jnp.where(is_b, beta, gc)

    def main(col):
        return pl.BlockSpec((tt, D), lambda i: (i, col))

    def prev(col):
        return pl.BlockSpec((HALO_G, D), lambda i: (jnp.maximum(i * hb - 1, 0), col))

    def wspec(col):
        return pl.BlockSpec((8, D), lambda i: (0, col))

    vec = pl.BlockSpec((1, HD), lambda i: (0, 0))
    gblk = pl.BlockSpec((tt, HD), lambda i: (i, 0))
    sds = jax.ShapeDtypeStruct((t, D), f32)
    return pl.pallas_call(
        body, name="gdn_pre_fwd", grid=(t // tt,),
        in_specs=[main(0), main(1), main(2), prev(0), prev(1), prev(2), gblk, wspec(0), wspec(1), wspec(2), vec, vec],
        out_specs=[pl.BlockSpec((tt, D), lambda i: (i, 0))] * 3 + [gblk, pl.BlockSpec((tt, 3 * D), lambda i: (i, 0))],
        out_shape=[sds] * 3 + [jax.ShapeDtypeStruct((t, HD), f32), jax.ShapeDtypeStruct((t, 3 * D), f32)],
        scratch_shapes=[pltpu.VMEM((tt + HALO_G, D), f32), pltpu.VMEM((KG - 1, tt, D), f32)],
        compiler_params=_cparams("parallel"),
    )(p_qkv, p_qkv, p_qkv, p_qkv, p_qkv, p_qkv, p_ba, cw, cw, cw, alog_v, dt_v)


def _gdn_pre_bwd(p_qkv, p_ba, pre, cw, alog_v, dt_v, dqn, dkn, dva, dgt, *, tt):
    t = p_qkv.shape[0]
    hb = tt // HALO_G
    n_tiles = t // tt
    last_hb = t // HALO_G - 1
    ne = tt + HALO_G

    def body(q_ref, k_ref, v_ref, qp_ref, kp_ref, vp_ref, pq_ref, pk_ref, pv_ref, pqx_ref, pkx_ref, pvx_ref,
             dq_ref, dk_ref, dv_ref, dqx_ref, dkx_ref, dvx_ref, ba_ref, dgt_ref,
             wq_ref, wk_ref, wv_ref, al_ref, dt_ref,
             dp_ref, dba_ref, dcw_ref, dad_ref, ext_ref, sh_ref, dpre_ref, draw_ref):
        i = pl.program_id(0)
        first = i == 0
        last = i == n_tiles - 1

        @pl.when(first)
        def _():
            dcw_ref[...] = jnp.zeros_like(dcw_ref)
            dad_ref[...] = jnp.zeros_like(dad_ref)

        def one(x_ref, xp_ref, pre_ref, prex_ref, d_ref, dx_ref, w_ref, col, scale):
            ext_ref[0:HALO_G, :] = jnp.where(first, 0.0, xp_ref[...])
            ext_ref[HALO_G:, :] = x_ref[...]
            _shift_copies(ext_ref, sh_ref, tt, _G_FWD_SHIFTS)
            act, dact = _silu_and_grad(jnp.concatenate([pre_ref[...], prex_ref[...]], axis=0))
            d_out = jnp.concatenate([d_ref[...], jnp.where(last, 0.0, dx_ref[...])], axis=0)
            if scale is None:
                d_act = d_out
            else:
                parts = []
                for h in range(NH):
                    a = act[:, h * HD:(h + 1) * HD]
                    dn = d_out[:, h * HD:(h + 1) * HD]
                    r = lax.rsqrt(jnp.sum(a * a, axis=-1, keepdims=True) + L2_EPS)
                    parts.append(scale * r * (dn - a * (r * r) * jnp.sum(dn * a, axis=-1, keepdims=True)))
                d_act = jnp.concatenate(parts, axis=-1)
            dpre_ref[...] = d_act * dact
            _tap_corr(dcw_ref.at[:, col * D:(col + 1) * D], tt, dpre_ref, ext_ref, sh_ref, _G_FWD_OFFS, shifts=_G_FWD_SHIFTS)
            _shift_copies(dpre_ref, sh_ref, tt, _G_BWD_SHIFTS)
            _tap_conv(draw_ref, tt, dpre_ref, sh_ref, w_ref, _G_BWD_OFFS, shifts=_G_BWD_SHIFTS)
            dp_ref[:, col * D:(col + 1) * D] = draw_ref[...].astype(bf16)

        one(q_ref, qp_ref, pq_ref, pqx_ref, dq_ref, dqx_ref, wq_ref, 0, HD ** -0.5)
        one(k_ref, kp_ref, pk_ref, pkx_ref, dk_ref, dkx_ref, wk_ref, 1, 1.0)
        one(v_ref, vp_ref, pv_ref, pvx_ref, dv_ref, dvx_ref, wv_ref, 2, None)

        is_b, is_a, beta, g, sg, neg_a = _gate_math(ba_ref[...], al_ref, dt_ref)
        dgt_v = dgt_ref[...]
        dg = _dot_hi(_chunk_tri(tt, lower=False), jnp.where(is_a, dgt_v, 0.0))
        d_al = jnp.where(is_a, dg * neg_a * sg, 0.0)
        dba_ref[...] = jnp.where(is_b, dgt_v * beta * (1.0 - beta), d_al)
        dad_ref[0:1, :] += jnp.sum(jnp.where(is_a, dg * g, 0.0), axis=0, keepdims=True)
        dad_ref[1:2, :] += jnp.sum(d_al, axis=0, keepdims=True)

    def main(col):
        return pl.BlockSpec((tt, D), lambda i: (i, col))

    def prev(col):
        return pl.BlockSpec((HALO_G, D), lambda i: (jnp.maximum(i * hb - 1, 0), col))

    def nxt(col):
        return pl.BlockSpec((HALO_G, D), lambda i: (jnp.minimum((i + 1) * hb, last_hb), col))

    def wspec(col):
        return pl.BlockSpec((8, D), lambda i: (0, col))

    vec = pl.BlockSpec((1, HD), lambda i: (0, 0))
    gblk = pl.BlockSpec((tt, HD), lambda i: (i, 0))
    return pl.pallas_call(
        body, name="gdn_pre_bwd", grid=(n_tiles,),
        in_specs=[main(0), main(1), main(2), prev(0), prev(1), prev(2), main(0), main(1), main(2), nxt(0), nxt(1), nxt(2),
                  main(0), main(0), main(0), nxt(0), nxt(0), nxt(0), gblk, gblk,
                  wspec(0), wspec(1), wspec(2), vec, vec],
        out_specs=[pl.BlockSpec((tt, 3 * D), lambda i: (i, 0)), pl.BlockSpec((tt, HD), lambda i: (i, 0)),
                   pl.BlockSpec((8, 3 * D), lambda i: (0, 0)), pl.BlockSpec((8, HD), lambda i: (0, 0))],
        out_shape=[jax.ShapeDtypeStruct((t, 3 * D), bf16), jax.ShapeDtypeStruct((t, HD), f32),
                   jax.ShapeDtypeStruct((8, 3 * D), f32), jax.ShapeDtypeStruct((8, HD), f32)],
        scratch_shapes=[pltpu.VMEM((HALO_G + tt, D), f32), pltpu.VMEM((KG - 1, tt, D), f32),
                        pltpu.VMEM((ne, D), f32), pltpu.VMEM((tt, D), f32)],
        compiler_params=_cparams("arbitrary"),
    )(p_qkv, p_qkv, p_qkv, p_qkv, p_qkv, p_qkv, pre, pre, pre, pre, pre, pre,
      dqn, dkn, dva, dqn, dkn, dva, p_ba, dgt, cw, cw, cw, alog_v, dt_v)


def _dot_b(a, b, dims):
    return lax.dot_general(a.astype(bf16), b.astype(bf16), (dims, ((), ())), preferred_element_type=f32)


def _inverse_by_doubling(ms):
    heads = range(len(ms))
    r = lax.broadcasted_iota(jnp.int32, (CH, CH), 0)
    c = lax.broadcasted_iota(jnp.int32, (CH, CH), 1)
    eye = jnp.where(r == c, 1.0, 0.0).astype(f32)
    p = [eye + ms[h] for h in heads]
    mp = ms
    for _ in range(5):
        mp = [_dot_b(mp[h], mp[h], _NN) for h in heads]
        pm = [_dot_b(p[h], mp[h], _NN) for h in heads]
        p = [p[h] + pm[h] for h in heads]
    return tuple(p)


@jax.custom_vjp
def _known_inverse(ms, ps):
    return ps


def _known_inverse_fwd(ms, ps):
    return ps, ps


def _known_inverse_bwd(ps, cts):
    heads = range(len(ps))
    left = [_dot_b(ps[h], cts[h], _TN) for h in heads]
    return tuple(_dot_b(left[h], ps[h], _NT) for h in heads), tuple(jnp.zeros_like(p) for p in ps)


_known_inverse.defvjp(_known_inverse_fwd, _known_inverse_bwd)


def _chunk_prepare(qs, ks, vs, gcs, bbs, ps=None):
    heads = range(len(qs))
    r = lax.broadcasted_iota(jnp.int32, (CH, CH), 0)
    c = lax.broadcasted_iota(jnp.int32, (CH, CH), 1)
    causal = r >= c
    strict = r > c
    gc_row = [gcs[h].T[:CH, :] for h in heads]
    decay = [jnp.where(causal, jnp.exp(jnp.where(causal, gcs[h][:, :CH] - gc_row[h], 0.0)), 0.0) for h in heads]
    kb = [ks[h] * bbs[h] for h in heads]
    egc = [jnp.exp(gcs[h]) for h in heads]
    kk = [_dot_b(kb[h], ks[h], _NT) for h in heads]
    qk = [_dot_b(qs[h], ks[h], _NT) for h in heads]
    m = tuple(-jnp.where(strict, kk[h] * decay[h], 0.0) for h in heads)
    p = _inverse_by_doubling(m) if ps is None else _known_inverse(m, ps)
    u = [_dot_b(p[h], vs[h] * bbs[h], _NN) for h in heads]
    w = [_dot_b(p[h], kb[h] * egc[h], _NN) for h in heads]
    intra = [jnp.where(causal, qk[h] * decay[h], 0.0) for h in heads]
    g_last = [gcs[h][CH - 1:CH, :] for h in heads]
    k_dec = [ks[h] * jnp.exp(g_last[h] - gcs[h]) for h in heads]
    q_dec = [qs[h] * egc[h] for h in heads]
    e_last = [jnp.exp(g_last[h]) for h in heads]
    return u, w, intra, q_dec, k_dec, e_last, p


def _chunk_apply(u, w, intra, q_dec, k_dec, e_last, ss):
    heads = range(len(ss))
    ws = [_dot_b(w[h], ss[h], _NN) for h in heads]
    qs_s = [_dot_b(q_dec[h], ss[h], _NN) for h in heads]
    v_new = [u[h] - ws[h] for h in heads]
    iv = [_dot_b(intra[h], v_new[h], _NN) for h in heads]
    kv = [_dot_b(k_dec[h], v_new[h], _TN) for h in heads]
    o = tuple(qs_s[h] + iv[h] for h in heads)
    s_new = tuple(ss[h] * e_last[h] + kv[h] for h in heads)
    return o, s_new


def _chunk_group_fn(ins, ss, ps=None):
    n = len(ss)
    prep = _chunk_prepare(*(sum((tuple(c[i]) for c in ins), ()) for i in range(5)), ps=ps)
    outs, befores = [], []
    for g in range(len(ins)):
        befores.append(ss)
        o, ss = _chunk_apply(*(x[g * n:(g + 1) * n] for x in prep[:6]), ss)
        outs.append(o)
    return tuple(outs), tuple(befores), ss, prep[6]


def _head_cols():
    return [slice(h * HD, (h + 1) * HD) for h in range(NH)]


def _head_gates(gt):
    gcs = tuple(jnp.broadcast_to(gt[:, NH + h:NH + h + 1], (CH, HD)) for h in range(NH))
    bbs = tuple(jnp.broadcast_to(gt[:, h:h + 1], (CH, HD)) for h in range(NH))
    return gcs, bbs


def _gdn_scan_fwd(qn, kn, va, gates, *, tt):
    t = qn.shape[0]
    cpb = tt // CH
    group = min(SCAN_GROUP, cpb)

    def body(q_ref, k_ref, v_ref, gt_ref, o_ref, st_ref, p_ref, s_scr):
        @pl.when(pl.program_id(0) == 0)
        def _():
            s_scr[...] = jnp.zeros_like(s_scr)

        cols = _head_cols()

        def inputs(ci):
            rows = pl.ds(pl.multiple_of(ci * CH, CH), CH)
            gcs, bbs = _head_gates(gt_ref[rows, :])
            return tuple(tuple(ref[rows, cl] for cl in cols) for ref in (q_ref, k_ref, v_ref)) + (gcs, bbs)

        def step(gi, carry):
            chunks = [group * gi + g for g in range(group)]
            outs, befores, s_end, p = _chunk_group_fn([inputs(ci) for ci in chunks], tuple(s_scr[h] for h in range(NH)))
            for g, ci in enumerate(chunks):
                rows = pl.ds(pl.multiple_of(ci * CH, CH), CH)
                for h in range(NH):
                    st_ref[ci, h] = befores[g][h]
                    o_ref[rows, cols[h]] = outs[g][h]
                    p_ref[ci, h] = p[g * NH + h].astype(bf16)
            for h in range(NH):
                s_scr[h] = s_end[h]
            return carry

        lax.fori_loop(0, cpb // group, step, 0)

    blk = pl.BlockSpec((tt, D), lambda i: (i, 0))
    return pl.pallas_call(
        body, name="gdn_scan_fwd", grid=(t // tt,),
        in_specs=[blk] * 3 + [pl.BlockSpec((tt, HD), lambda i: (i, 0))],
        out_specs=[blk, pl.BlockSpec((cpb, NH, HD, HD), lambda i: (i, 0, 0, 0)),
                   pl.BlockSpec((cpb, NH, CH, CH), lambda i: (i, 0, 0, 0))],
        out_shape=[jax.ShapeDtypeStruct((t, D), f32), jax.ShapeDtypeStruct((t // CH, NH, HD, HD), f32),
                   jax.ShapeDtypeStruct((t // CH, NH, CH, CH), bf16)],
        scratch_shapes=[pltpu.VMEM((NH, HD, HD), f32)],
        compiler_params=_cparams("arbitrary"),
    )(qn, kn, va, gates)


def _gdn_scan_bwd(qn, kn, va, gates, states, inverses, do, *, tt):
    t = qn.shape[0]
    nblk = t // tt
    cpb = tt // CH

    def body(q_ref, k_ref, v_ref, gt_ref, st_ref, p_ref, do_ref, dq_ref, dk_ref, dv_ref, dgt_ref, ds_scr):
        @pl.when(pl.program_id(0) == 0)
        def _():
            ds_scr[...] = jnp.zeros_like(ds_scr)

        cols = _head_cols()

        def rows_of(ci):
            return pl.ds(pl.multiple_of(ci * CH, CH), CH)

        def inputs(ci):
            gcs, bbs = _head_gates(gt_ref[rows_of(ci), :])
            return tuple(tuple(ref[rows_of(ci), cl] for cl in cols) for ref in (q_ref, k_ref, v_ref)) + (gcs, bbs)

        def step(j, carry):
            ci = cpb - 1 - j
            ps = tuple(p_ref[ci, h].astype(f32) for h in range(NH))

            def one(ins, ss):
                outs, _, s_end, _ = _chunk_group_fn([ins], ss, ps=ps)
                return outs[0], s_end

            _, vjp = jax.vjp(one, inputs(ci), tuple(st_ref[ci, h] for h in range(NH)))
            grads, ds = vjp((tuple(do_ref[rows_of(ci), cl] for cl in cols), tuple(ds_scr[h] for h in range(NH))))
            lane = lax.broadcasted_iota(jnp.int32, (CH, HD), 1)
            dgt = jnp.zeros((CH, HD), f32)
            for h in range(NH):
                for ref, g in zip((dq_ref, dk_ref, dv_ref), grads[:3]):
                    ref[rows_of(ci), cols[h]] = g[h]
                dgt = dgt + jnp.where(lane == NH + h, jnp.sum(grads[3][h], axis=-1, keepdims=True), 0.0)
                dgt = dgt + jnp.where(lane == h, jnp.sum(grads[4][h], axis=-1, keepdims=True), 0.0)
                ds_scr[h] = ds[h]
            dgt_ref[rows_of(ci), :] = dgt
            return carry

        lax.fori_loop(0, cpb, step, 0)

    blk = pl.BlockSpec((tt, D), lambda i: (nblk - 1 - i, 0))
    sblk = pl.BlockSpec((cpb, NH, HD, HD), lambda i: (nblk - 1 - i, 0, 0, 0))
    sds = jax.ShapeDtypeStruct((t, D), f32)
    gblk = pl.BlockSpec((tt, HD), lambda i: (nblk - 1 - i, 0))
    pblk = pl.BlockSpec((cpb, NH, CH, CH), lambda i: (nblk - 1 - i, 0, 0, 0))
    return pl.pallas_call(
        body, name="gdn_scan_bwd", grid=(nblk,),
        in_specs=[blk] * 3 + [gblk, sblk, pblk, blk],
        out_specs=[blk] * 3 + [gblk], out_shape=[sds] * 3 + [jax.ShapeDtypeStruct((t, HD), f32)],
        scratch_shapes=[pltpu.VMEM((NH, HD, HD), f32)],
        compiler_params=_cparams("arbitrary"),
    )(qn, kn, va, gates, states, inverses, do)


def _rms_heads(o):
    ons, rs = [], []
    for h in range(NH):
        a = o[:, h * HD:(h + 1) * HD]
        r = lax.rsqrt(jnp.mean(a * a, axis=-1, keepdims=True) + RMS_EPS)
        ons.append(a * r)
        rs.append(jnp.broadcast_to(r, a.shape))
    return jnp.concatenate(ons, axis=-1), jnp.concatenate(rs, axis=-1)


def _gdn_post_fwd(o, p_gz, ng_b, w_out, *, tt):
    t = o.shape[0]

    def body(o_ref, gz_ref, ng_ref, w_ref, og_ref, y_ref):
        on, _ = _rms_heads(o_ref[...])
        z, _ = _silu_and_grad(gz_ref[...])
        og = (on * ng_ref[...] * z).astype(bf16)
        og_ref[...] = og
        y_ref[...] = jnp.dot(og, w_ref[...], preferred_element_type=f32)

    blk = pl.BlockSpec((tt, D), lambda i: (i, 0))
    return pl.pallas_call(
        body, name="gdn_post_fwd", grid=(t // tt,),
        in_specs=[blk, blk, pl.BlockSpec((1, D), lambda i: (0, 0)), pl.BlockSpec((D, D), lambda i: (0, 0))],
        out_specs=[blk, blk], out_shape=[jax.ShapeDtypeStruct((t, D), bf16), jax.ShapeDtypeStruct((t, D), f32)],
        compiler_params=_cparams("parallel"),
    )(o, p_gz, ng_b, w_out)


def _gdn_post_bwd(o, p_gz, ng_b, w_out, dyg, *, tt):
    t = o.shape[0]

    def body(o_ref, gz_ref, ng_ref, w_ref, dyg_ref, do_ref, dgz_ref, dng_ref):
        @pl.when(pl.program_id(0) == 0)
        def _():
            dng_ref[...] = jnp.zeros_like(dng_ref)

        on, r = _rms_heads(o_ref[...])
        z, dz = _silu_and_grad(gz_ref[...])
        dog_v = lax.dot_general(dyg_ref[...], w_ref[...], (_NT, ((), ())), preferred_element_type=f32)
        ng = ng_ref[...]
        dgz_ref[...] = (dog_v * on * ng * dz).astype(bf16)
        dy = dog_v * z
        dng_all = jnp.sum(dy * on, axis=0, keepdims=True)
        dng = dng_all[:, 0:HD]
        for h in range(1, NH):
            dng = dng + dng_all[:, h * HD:(h + 1) * HD]
        dng_ref[0:1, :] += dng
        don = dy * ng
        prod = don * on
        parts = []
        for h in range(NH):
            sl = slice(h * HD, (h + 1) * HD)
            parts.append(don[:, sl] - on[:, sl] * jnp.mean(prod[:, sl], axis=-1, keepdims=True))
        do_ref[...] = r * jnp.concatenate(parts, axis=-1)

    blk = pl.BlockSpec((tt, D), lambda i: (i, 0))
    return pl.pallas_call(
        body, name="gdn_post_bwd", grid=(t // tt,),
        in_specs=[blk, blk, pl.BlockSpec((1, D), lambda i: (0, 0)), pl.BlockSpec((D, D), lambda i: (0, 0)), blk],
        out_specs=[blk, blk, pl.BlockSpec((8, HD), lambda i: (0, 0))],
        out_shape=[jax.ShapeDtypeStruct((t, D), f32), jax.ShapeDtypeStruct((t, D), bf16),
                   jax.ShapeDtypeStruct((8, HD), f32)],
        compiler_params=_cparams("arbitrary"),
    )(o, p_gz, ng_b, w_out, dyg)


def _merge(x, y_conf, y_gdn, p_gates, target, w_o, ln_g, ln_b, *, tt):
    t = x.shape[0]

    def body(x_ref, yc_ref, yg_ref, gc_ref, gg_ref, tg_ref, w_ref, g_ref, b_ref,
             loss_ref, dxd_ref, dyc_ref, dyg_ref, dpg_ref, h_ref, dz_ref, dvec_ref):
        @pl.when(pl.program_id(0) == 0)
        def _():
            loss_ref[...] = jnp.zeros_like(loss_ref)
            dvec_ref[...] = jnp.zeros_like(dvec_ref)

        sc = _sigmoid(gc_ref[...])
        sg = _sigmoid(gg_ref[...])
        yc = yc_ref[...]
        yg = yg_ref[...]
        h = (sc * yc + sg * yg).astype(bf16)
        h_ref[...] = h
        z = DN_ALPHA * x_ref[...] + jnp.dot(h, w_ref[...], preferred_element_type=f32)
        mu = jnp.mean(z, axis=-1, keepdims=True)
        cen = z - mu
        rstd = lax.rsqrt(jnp.mean(cen * cen, axis=-1, keepdims=True) + LN_EPS)
        xhat = cen * rstd
        err = xhat * g_ref[...] + b_ref[...] - tg_ref[...]
        loss_ref[...] += 0.5 / D * jnp.sum(err * err)
        dy = err * (1.0 / D)
        dvec_ref[0:1, :] += jnp.sum(dy * xhat, axis=0, keepdims=True)
        dvec_ref[1:2, :] += jnp.sum(dy, axis=0, keepdims=True)
        dxhat = dy * g_ref[...]
        dz = rstd * (dxhat - jnp.mean(dxhat, axis=-1, keepdims=True)
                     - xhat * jnp.mean(dxhat * xhat, axis=-1, keepdims=True))
        dxd_ref[...] = DN_ALPHA * dz
        dz_b = dz.astype(bf16)
        dz_ref[...] = dz_b
        dh = lax.dot_general(dz_b, w_ref[...], (_NT, ((), ())), preferred_element_type=f32)
        dyc_ref[...] = (dh * sc).astype(bf16)
        dyg_ref[...] = (dh * sg).astype(bf16)
        dpg_ref[:, 0:D] = (dh * yc * sc * (1.0 - sc)).astype(bf16)
        dpg_ref[:, D:] = (dh * yg * sg * (1.0 - sg)).astype(bf16)

    blk = pl.BlockSpec((tt, D), lambda i: (i, 0))
    wblk = pl.BlockSpec((D, D), lambda i: (0, 0))
    vec = pl.BlockSpec((1, D), lambda i: (0, 0))
    return pl.pallas_call(
        body, name="merge_norm_loss", grid=(t // tt,),
        in_specs=[blk, blk, blk, pl.BlockSpec((tt, D), lambda i: (i, 0)), pl.BlockSpec((tt, D), lambda i: (i, 1)),
                  blk, wblk, vec, vec],
        out_specs=[pl.BlockSpec((8, HD), lambda i: (0, 0)), blk, blk, blk,
                   pl.BlockSpec((tt, 2 * D), lambda i: (i, 0)), blk, blk, pl.BlockSpec((8, D), lambda i: (0, 0))],
        out_shape=[jax.ShapeDtypeStruct((8, HD), f32), jax.ShapeDtypeStruct((t, D), f32),
                   jax.ShapeDtypeStruct((t, D), bf16), jax.ShapeDtypeStruct((t, D), bf16),
                   jax.ShapeDtypeStruct((t, 2 * D), bf16), jax.ShapeDtypeStruct((t, D), bf16),
                   jax.ShapeDtypeStruct((t, D), bf16), jax.ShapeDtypeStruct((8, D), f32)],
        compiler_params=_cparams("arbitrary"),
    )(x, y_conf, y_gdn, p_gates, p_gates, target, w_o, ln_g, ln_b)


def _place():
    return lax.axis_index("x"), lax.axis_index("y"), lax.axis_index("c")


def _any_specs(n):
    return [pl.BlockSpec(memory_space=pl.ANY)] * n


def _sibling_merge(arrs, name, half_axes=None):
    k = len(arrs)

    def half_shape(a, ax):
        shape = list(a.shape)
        shape[ax] //= 2
        return tuple(shape)

    def body(*refs):
        a_refs, o_refs = refs[:k], refs[k:2 * k]
        send_sems, recv_sems = refs[2 * k:]
        x, y, c = _place()
        sends = []
        for i in range(k):
            src = a_refs[i]
            if half_axes is not None:
                n = a_refs[i].shape[half_axes[i]] // 2
                idx = [slice(None)] * len(a_refs[i].shape)
                idx[half_axes[i]] = pl.ds((1 - c) * n, n)
                src = a_refs[i].at[tuple(idx)]
            cp = pltpu.make_async_remote_copy(src_ref=src, dst_ref=o_refs[i], send_sem=send_sems.at[i],
                                              recv_sem=recv_sems.at[i], device_id=(x, y, 1 - c), device_id_type=MESH)
            cp.start()
            sends.append(cp)
        for cp in sends:
            cp.wait()

    shapes = [a.shape if half_axes is None else half_shape(a, ax) for a, ax in zip(arrs, half_axes or arrs)]
    return pl.pallas_call(
        body, name=name, in_specs=_any_specs(k), out_specs=_any_specs(k),
        out_shape=[jax.ShapeDtypeStruct(sh, a.dtype) for sh, a in zip(shapes, arrs)],
        scratch_shapes=[pltpu.SemaphoreType.DMA((k,)), pltpu.SemaphoreType.DMA((k,))],
    )(*arrs)


def _join_halves(mine, other, axis=-2):
    c = lax.axis_index("c")
    return jnp.concatenate([jnp.where(c == 0, mine, other), jnp.where(c == 0, other, mine)], axis=axis)


def _chip_exchange_ops(a_refs, o_refs, send_sems, recv_sems, local_sems, scatter):
    k = len(a_refs)
    x, y, c = _place()
    me = 2 * x + y
    peers = [(1 - x, y), (x, 1 - y), (1 - x, 1 - y)]

    def src(i, j):
        return a_refs[i].at[j] if scatter else a_refs[i]

    def copy(i, n, send_j, slot):
        px, py = peers[n]
        return pltpu.make_async_remote_copy(
            src_ref=src(i, send_j), dst_ref=o_refs[i].at[slot], send_sem=send_sems.at[3 * i + n],
            recv_sem=recv_sems.at[3 * i + n], device_id=(px, py, c), device_id_type=MESH)

    def owns():
        return [pltpu.make_async_copy(src(i, me), o_refs[i].at[me], local_sems.at[i]) for i in range(k)]

    def sends():
        return [copy(i, n, 2 * peers[n][0] + peers[n][1], me) for n in range(3) for i in range(k)]

    def start():
        for cp in owns() + sends():
            cp.start()

    def finish():
        for n in range(3):
            for i in range(k):
                copy(i, n, me, 2 * peers[n][0] + peers[n][1]).wait_recv()
        for cp in sends():
            cp.wait_send()
        for cp in owns():
            cp.wait()

    return start, finish


def _chip_exchange_shapes(arrs, scatter):
    return [jax.ShapeDtypeStruct((N_CHIPS,) + tuple(a.shape[1:] if scatter else a.shape), a.dtype) for a in arrs]


def _chip_exchange_sems(k):
    return [pltpu.SemaphoreType.DMA((3 * k,)), pltpu.SemaphoreType.DMA((3 * k,)), pltpu.SemaphoreType.DMA((k,))]


def _gather_halves(halves, wholes, name):
    kh, kw = len(halves), len(wholes)
    k = kh + kw

    def body(*refs):
        a_refs, got_refs, oth_refs = refs[:k], refs[k:2 * k], refs[2 * k:2 * k + kh]
        send_sems, recv_sems, local_sems, fwd_send_sems, fwd_recv_sems = refs[2 * k + kh:]
        x, y, c = _place()
        me = 2 * x + y
        start, _ = _chip_exchange_ops(a_refs, got_refs, send_sems, recv_sems, local_sems, False)
        slots = [me] + [2 * px + py for px, py in [(1 - x, y), (x, 1 - y), (1 - x, 1 - y)]]

        def forward(i, r):
            src = a_refs[i] if r == 0 else got_refs[i].at[slots[r]]
            return pltpu.make_async_remote_copy(
                src_ref=src, dst_ref=oth_refs[i].at[slots[r]], send_sem=fwd_send_sems.at[4 * i + r],
                recv_sem=fwd_recv_sems.at[4 * i + r], device_id=(x, y, 1 - c), device_id_type=MESH)

        def arrival(i, n):
            px, py = [(1 - x, y), (x, 1 - y), (1 - x, 1 - y)][n]
            return pltpu.make_async_remote_copy(
                src_ref=a_refs[i], dst_ref=got_refs[i].at[slots[n + 1]], send_sem=send_sems.at[3 * i + n],
                recv_sem=recv_sems.at[3 * i + n], device_id=(px, py, c), device_id_type=MESH)

        start()
        for i in range(kh):
            forward(i, 0).start()
        for n in range(3):
            for i in range(k):
                arrival(i, n).wait_recv()
                if i < kh:
                    forward(i, n + 1).start()
        for i in range(kh):
            for r in range(4):
                forward(i, r).wait()
        for n in range(3):
            for i in range(k):
                arrival(i, n).wait_send()
        for i in range(k):
            pltpu.make_async_copy(a_refs[i], got_refs[i].at[me], local_sems.at[i]).wait()

    arrs = list(halves) + list(wholes)
    shapes = _chip_exchange_shapes(arrs, False)
    out = pl.pallas_call(
        body, name=name, in_specs=_any_specs(k), out_specs=_any_specs(k + kh),
        out_shape=shapes + shapes[:kh],
        scratch_shapes=_chip_exchange_sems(k) + [pltpu.SemaphoreType.DMA((4 * kh,)), pltpu.SemaphoreType.DMA((4 * kh,))],
    )(*arrs)
    return [(out[i], out[k + i]) for i in range(kh)], out[kh:k]


def _pair_sum(g_all, got, c_arr, name, out_dtype, axis):
    rows, w = got.shape[1:]
    if axis == -2:
        blk, steps = (1, rows // 4, w), 4
        pick = lambda j, i, c_ref: (j, c_ref[0] * steps + i, 0)
        mine = lambda j, i, c_ref: (j, i, 0)
    else:
        blk, steps = (1, rows, LANE), w // LANE
        pick = lambda j, i, c_ref: (j, 0, c_ref[0] * steps + i)
        mine = lambda j, i, c_ref: (j, 0, i)

    def body(c_ref, a_ref, b_ref, o_ref):
        o_ref[...] = (a_ref[...] + b_ref[...]).astype(out_dtype)

    return pl.pallas_call(
        body, name=name,
        grid_spec=pltpu.PrefetchScalarGridSpec(
            num_scalar_prefetch=1, grid=(N_CHIPS, steps),
            in_specs=[pl.BlockSpec(blk, pick), pl.BlockSpec(blk, mine)], out_specs=pl.BlockSpec(blk, mine)),
        out_shape=jax.ShapeDtypeStruct(got.shape, out_dtype),
        compiler_params=_cparams("parallel", "parallel"),
    )(c_arr, g_all, got)


def _sum_slots(a, name):
    n, w = a.shape[1:]
    by_rows = n % 64 == 0
    in_blk = pl.BlockSpec((N_CHIPS, n // 4, w), lambda i: (0, i, 0)) if by_rows else pl.BlockSpec((N_CHIPS, n, LANE), lambda i: (0, 0, i))
    out_blk = pl.BlockSpec((n // 4, w), lambda i: (i, 0)) if by_rows else pl.BlockSpec((n, LANE), lambda i: (0, i))

    def body(a_ref, o_ref):
        o_ref[...] = ((a_ref[0].astype(f32) + a_ref[1].astype(f32)) + a_ref[2].astype(f32)) + a_ref[3].astype(f32)

    return pl.pallas_call(
        body, name=name, grid=(4 if by_rows else w // LANE,),
        in_specs=[in_blk], out_specs=out_blk,
        out_shape=jax.ShapeDtypeStruct((n, w), f32),
        compiler_params=_cparams("parallel"),
    )(a)


def _adamw(w, g, m, v, name):
    rows, width = w.shape
    by_rows = rows % 64 == 0
    c1 = 1.0 / (1.0 - ADAM_B1 ** ADAM_STEP)
    c2 = 1.0 / (1.0 - ADAM_B2 ** ADAM_STEP)

    def body(w_ref, g_ref, m_ref, v_ref, d_ref, mo_ref, vo_ref):
        g_v = g_ref[...]
        m_new = ADAM_B1 * m_ref[...] + (1.0 - ADAM_B1) * g_v
        v_new = ADAM_B2 * v_ref[...] + (1.0 - ADAM_B2) * (g_v * g_v)
        mo_ref[...] = m_new
        vo_ref[...] = v_new
        d_ref[...] = -ADAM_LR * ((m_new * c1) / (jnp.sqrt(v_new * c2) + ADAM_EPS) + ADAM_WD * w_ref[...])

    blk = pl.BlockSpec((rows // 8, width), lambda i: (i, 0)) if by_rows else pl.BlockSpec((rows, LANE), lambda i: (0, i))
    sds = jax.ShapeDtypeStruct((rows, width), f32)
    return pl.pallas_call(
        body, name=name, grid=(8 if by_rows else width // LANE,),
        in_specs=[blk] * 4, out_specs=[blk] * 3, out_shape=[sds] * 3,
        compiler_params=_cparams("parallel"),
    )(w, g, m, v)


R_DW = 3 * SQ_BLK
R_CW = R_DW + 8
R_VEC = R_CW + 8
R_SMALL = R_VEC + 8
REST_ROWS = 896


def _pack_small(conf_dw_w, gdn_conv_w, vecs, a_log, dt_bias, norm_g):
    dw = jnp.pad(conf_dw_w.reshape(-1), (0, 8 * D - KC * SQ_BLK)).reshape(8, D)
    cw = jnp.pad(gdn_conv_w.reshape(-1), (0, 5 * D)).reshape(8, D)
    vec = jnp.pad(jnp.stack(vecs), ((0, 3), (0, 0)))
    small = jnp.pad(jnp.concatenate([a_log, dt_bias, norm_g]), (0, D - 2 * NH - HD)).reshape(1, D)
    return jnp.pad(jnp.concatenate([dw, cw, vec, small], axis=0), ((0, REST_ROWS - R_SMALL - 1), (0, 0)))


def _pack_rest(conf_w_out, gdn_w_out, w_o, small):
    return jnp.concatenate([conf_w_out, gdn_w_out, w_o, small], axis=0)


def _unpack_rest(p):
    conf_dw_w = p[R_DW:R_DW + 8].reshape(-1)[:KC * SQ_BLK].reshape(KC, SQ_BLK)
    gdn_conv_w = p[R_CW:R_CW + 3].reshape(KG, 3 * SQ_BLK)
    small = p[R_SMALL]
    return dict(conf_w_out=p[0:SQ_BLK], gdn_w_out=p[SQ_BLK:2 * SQ_BLK], w_o=p[2 * SQ_BLK:R_DW],
                conf_dw_w=conf_dw_w, gdn_conv_w=gdn_conv_w, conf_dw_b=p[R_VEC], conf_ln_g=p[R_VEC + 1],
                conf_ln_b=p[R_VEC + 2], post_ln_g=p[R_VEC + 3], post_ln_b=p[R_VEC + 4],
                gdn_A_log=small[0:NH], gdn_dt_bias=small[NH:2 * NH], gdn_norm_g=small[2 * NH:2 * NH + HD])


_WEIGHT_ORDER = ("w_in", "conf_dw_w", "conf_dw_b", "conf_ln_g", "conf_ln_b", "conf_w_out", "gdn_conv_w",
                 "gdn_A_log", "gdn_dt_bias", "gdn_norm_g", "gdn_w_out", "w_o", "post_ln_g", "post_ln_b")


def _gather_weights(w_in, conf_w_out, gdn_w_out, w_o, conf_dw_w, gdn_conv_w):
    c = lax.axis_index("c")
    sq = jnp.concatenate([conf_w_out, gdn_w_out, w_o], axis=0).astype(bf16)
    w_half = lax.dynamic_slice_in_dim(w_in.T.astype(bf16), c * (D // 2), D // 2, axis=1)
    sq_half = lax.dynamic_slice_in_dim(sq, c * (sq.shape[0] // 2), sq.shape[0] // 2, axis=0)
    small = jnp.concatenate([jnp.pad(conf_dw_w.reshape(-1), (0, 8 * D - KC * SQ_BLK)).reshape(8, D),
                             jnp.pad(gdn_conv_w.reshape(-1), (0, 5 * D)).reshape(8, D)], axis=0)
    ((w_mine, w_other), (sq_mine, sq_other)), (small_all,) = _gather_halves([w_half, sq_half], [small], "weight_gather")
    w_t = _join_halves(w_mine, w_other, axis=-1).reshape(W_IN_COLS, D)
    sq4 = _join_halves(sq_mine, sq_other)
    sq_full = [sq4[:, n * SQ_BLK:(n + 1) * SQ_BLK].reshape(D, D) for n in range(3)]
    dw_full = small_all[:, 0:8].reshape(N_CHIPS, 8 * D)[:, :KC * SQ_BLK].reshape(N_CHIPS, KC, SQ_BLK)
    dw_full = dw_full.transpose(1, 0, 2).reshape(KC, D)
    cw_full = small_all[:, 8:11].reshape(N_CHIPS, KG, 3 * SQ_BLK).transpose(1, 0, 2).reshape(KG, 3 * D)
    return w_t, sq_full[0], sq_full[1], sq_full[2], dw_full, cw_full


def _pair_sums(g_w, g_rest):
    c_arr = lax.axis_index("c").astype(jnp.int32).reshape(1)
    got_w, got_r = _sibling_merge([g_w, g_rest], "grad_sibling_halves", half_axes=[-1, -2])
    pair_w = _pair_sum(g_w, got_w, c_arr, "grad_pair_sum_w_in", bf16, -1)
    pair_r = _pair_sum(g_rest, got_r, c_arr, "grad_pair_sum_rest", f32, -2)
    return pair_w, pair_r


def _chip_sums(all_w, all_r):
    tot_w, tot_r = _sum_slots(all_w, "grad_chip_sum_w_in"), _sum_slots(all_r, "grad_chip_sum_rest")
    oth_w, oth_r = _sibling_merge([tot_w, tot_r], "grad_sibling_result")
    return _join_halves(tot_w, oth_w, axis=-1), _join_halves(tot_r, oth_r)


def _local_step(x2, tgt, w_t, wc_out, wg_out, wo_full, dw_full, cw_full, conf_dw_b, conf_ln_g, conf_ln_b,
                gdn_A_log, gdn_dt_bias, gdn_norm_g, post_ln_g, post_ln_b):
    t = x2.shape[0]
    tt = min(TOKEN_TILE, t)
    tm = min(MM_TILE, t)

    w_conv, w_qkv, w_gz = w_t[0:3 * D], w_t[3 * D:6 * D], w_t[6 * D:7 * D]
    w_gates = w_t[7 * D + 2 * NH:]
    dw_pad = jnp.pad(dw_full, ((0, HALO_C - KC), (0, 0)))
    cw_pad = jnp.pad(cw_full, ((0, 8 - KG), (0, 0)))
    row = lambda v: v.reshape(1, D)
    alog_v = jnp.pad(gdn_A_log, (NH, HD - 2 * NH)).reshape(1, HD)
    dt_v = jnp.pad(gdn_dt_bias, (NH, HD - 2 * NH)).reshape(1, HD)
    ng_b = row(jnp.tile(gdn_norm_g, NH))
    w_ba = jnp.pad(w_t[7 * D:7 * D + 2 * NH], ((0, HD - 2 * NH), (0, 0)))

    x_b = x2.astype(bf16)

    p_conv = _mm_multi([x_b], [w_conv], out_dtype=f32, tm=tm, tn=MM_TILE, name="proj_conv", rhs_t=True)
    p_qkv = _mm_multi([x_b], [w_qkv], out_dtype=f32, tm=tm, tn=MM_TILE, name="proj_qkv", rhs_t=True)
    p_gz = _mm_multi([x_b], [w_gz], out_dtype=f32, tm=tm, tn=MM_TILE, name="proj_gz", rhs_t=True)
    p_gates = _mm_multi([x_b], [w_gates], out_dtype=f32, tm=tm, tn=MM_TILE, name="proj_gates", rhs_t=True)
    p_ba = _mm_multi([x_b], [w_ba], out_dtype=f32, tm=tm, tn=HD, name="proj_ba", rhs_t=True)

    u, a1, y_conf = _conv_fwd(p_conv, dw_pad, row(conf_dw_b), row(conf_ln_g), row(conf_ln_b), wc_out, tt=tt)

    qn, kn, va, gates, pre_qkv = _gdn_pre_fwd(p_qkv, p_ba, cw_pad, alog_v, dt_v, tt=tt)
    o, states, inverses = _gdn_scan_fwd(qn, kn, va, gates, tt=tt)
    og, y_gdn = _gdn_post_fwd(o, p_gz, ng_b, wg_out, tt=min(2 * TOKEN_TILE, t))

    loss_blk, dxd, dyc, dyg, dp_gates, h, dz, dpost = _merge(
        x2, y_conf, y_gdn, p_gates, tgt, wo_full, row(post_ln_g), row(post_ln_b), tt=tt)

    d_wo = _mm_kloop(h, dz, tm=D, tn=MM_TILE, tk=min(MM_K_TILE, t), name="grad_w_o")
    du = _mm_multi([dyc], [wc_out], out_dtype=f32, tm=tm, tn=MM_TILE, name="conf_out_bwd", rhs_t=True)
    d_wc = _mm_kloop(u, dyc, tm=D, tn=MM_TILE, tk=min(MM_K_TILE, t), name="grad_conf_w_out")
    d_wg = _mm_kloop(og, dyg, tm=D, tn=MM_TILE, tk=min(MM_K_TILE, t), name="grad_gdn_w_out")

    dp_conv, d_dww, dconv_vec = _conv_bwd(p_conv, a1, du, dw_pad, row(conf_ln_g), row(conf_ln_b), tt=tt)

    do, dp_gz, dng = _gdn_post_bwd(o, p_gz, ng_b, wg_out, dyg, tt=min(2 * TOKEN_TILE, t))
    dqn, dkn, dva, dgates = _gdn_scan_bwd(qn, kn, va, gates, states, inverses, do, tt=tt)
    dp_qkv, dp_ba, d_cw, d_ad = _gdn_pre_bwd(p_qkv, p_ba, pre_qkv, cw_pad, alog_v, dt_v, dqn, dkn, dva, dgates, tt=tt)
    dp_ba_b = dp_ba.astype(bf16)

    grad_x_factors = ([dp_conv, dp_qkv, dp_gz, dp_gates, dp_ba_b], [w_conv, w_qkv, w_gz, w_gates, w_ba], dxd)

    tk = min(MM_K_TILE, t)
    d_w_conv = _mm_kloop(dp_conv, x_b, tm=MM_TILE, tn=D, tk=tk, name="grad_w_in_conv")
    d_w_qkv = _mm_kloop(dp_qkv, x_b, tm=MM_TILE, tn=D, tk=tk, name="grad_w_in_qkv")
    d_w_gz = _mm_kloop(dp_gz, x_b, tm=MM_TILE, tn=D, tk=tk, name="grad_w_in_gz")
    d_w_gates = _mm_kloop(dp_gates, x_b, tm=MM_TILE, tn=D, tk=tk, name="grad_w_in_gates")
    d_w_ba = _mm_kloop(dp_ba_b, x_b, tm=HD, tn=D, tk=tk, name="grad_w_in_ba")
    d_w_in = jnp.concatenate([d_w_conv, d_w_qkv, d_w_gz, d_w_ba[:2 * NH], d_w_gates], axis=0).reshape(
        N_CHIPS, W_IN_BLK, D)

    return (loss_blk[0, 0], grad_x_factors, d_w_in, d_wc, d_wg, d_wo, d_dww, d_cw, dconv_vec, dpost, d_ad, dng)


def kernel(x, w_in, conf_dw_w, conf_dw_b, conf_ln_g, conf_ln_b, conf_w_out, gdn_conv_w, gdn_A_log, gdn_dt_bias, gdn_norm_g, gdn_w_out, w_o, post_ln_g, post_ln_b, loss_target, m_w_in, m_conf_dw_w, m_conf_dw_b, m_conf_ln_g, m_conf_ln_b, m_conf_w_out, m_gdn_conv_w, m_gdn_A_log, m_gdn_dt_bias, m_gdn_norm_g, m_gdn_w_out, m_w_o, m_post_ln_g, m_post_ln_b, v_w_in, v_conf_dw_w, v_conf_dw_b, v_conf_ln_g, v_conf_ln_b, v_conf_w_out, v_gdn_conv_w, v_gdn_A_log, v_gdn_dt_bias, v_gdn_norm_g, v_gdn_w_out, v_w_o, v_post_ln_g, v_post_ln_b):
    x2 = x.reshape(x.shape[-2], D)
    tgt = loss_target.reshape(x2.shape)
    w_t, wc_out, wg_out, wo_full, dw_full, cw_full = _gather_weights(
        w_in, conf_w_out, gdn_w_out, w_o, conf_dw_w, gdn_conv_w)
    (loss_part, grad_x_factors, d_w_in, d_wc, d_wg, d_wo, d_dww, d_cw, dconv_vec, dpost, d_ad, dng) = _local_step(
        x2, tgt, w_t, wc_out, wg_out, wo_full, dw_full, cw_full, conf_dw_b, conf_ln_g, conf_ln_b,
        gdn_A_log, gdn_dt_bias, gdn_norm_g, post_ln_g, post_ln_b)
    loss = lax.psum(loss_part, ("x", "y", "c"))

    dww_c = d_dww[:KC].reshape(KC, N_CHIPS, SQ_BLK)
    dcw_c = d_cw[:KG].reshape(KG, N_CHIPS, 3 * SQ_BLK)
    vecs = [dconv_vec[0], dconv_vec[1], dconv_vec[2], dpost[0], dpost[1]]
    g_rest = jnp.stack([
        _pack_rest(d_wc[j * SQ_BLK:(j + 1) * SQ_BLK], d_wg[j * SQ_BLK:(j + 1) * SQ_BLK], d_wo[j * SQ_BLK:(j + 1) * SQ_BLK],
                   _pack_small(dww_c[:, j], dcw_c[:, j], vecs, d_ad[0, NH:2 * NH], d_ad[1, NH:2 * NH], dng[0]))
        for j in range(N_CHIPS)])
    pair_w, pair_r = _pair_sums(d_w_in, g_rest)
    grad_x, all_w, all_r = _mm_multi(*grad_x_factors, out_dtype=f32, tm=min(GRAD_X_TILE[0], x2.shape[0]), tn=GRAD_X_TILE[1],
                                     name="grad_x_and_chip_scatter", scatter=[pair_w, pair_r])
    g_w_in, g_rest = _chip_sums(all_w, all_r)

    def rest_of(w_c, w_g, w_oo, dw, cw, b1, g1, b2, g2, b3, a_log, dt_bias, norm_g):
        return _pack_rest(w_c, w_g, w_oo, _pack_small(dw, cw, [b1, g1, b2, g2, b3], a_log, dt_bias, norm_g))

    w_r = rest_of(conf_w_out, gdn_w_out, w_o, conf_dw_w, gdn_conv_w, conf_dw_b, conf_ln_g, conf_ln_b,
                  post_ln_g, post_ln_b, gdn_A_log, gdn_dt_bias, gdn_norm_g)
    m_r = rest_of(m_conf_w_out, m_gdn_w_out, m_w_o, m_conf_dw_w, m_gdn_conv_w, m_conf_dw_b, m_conf_ln_g, m_conf_ln_b,
                  m_post_ln_g, m_post_ln_b, m_gdn_A_log, m_gdn_dt_bias, m_gdn_norm_g)
    v_r = rest_of(v_conf_w_out, v_gdn_w_out, v_w_o, v_conf_dw_w, v_gdn_conv_w, v_conf_dw_b, v_conf_ln_g, v_conf_ln_b,
                  v_post_ln_g, v_post_ln_b, v_gdn_A_log, v_gdn_dt_bias, v_gdn_norm_g)
    upd_w_in = _adamw(w_in.T, g_w_in, m_w_in.T, v_w_in.T, "adamw_w_in")
    upd_rest = _adamw(w_r, g_rest, m_r, v_r, "adamw_rest")

    out = [loss, grad_x.reshape(x.shape)]
    for big, rest in zip((g_w_in,) + tuple(upd_w_in), (g_rest,) + tuple(upd_rest)):
        d = dict(_unpack_rest(rest), w_in=big.T)
        out += [d[n] for n in _WEIGHT_ORDER]
    return tuple(out)
```

```python
import jax
import jax.numpy as jnp
from jax import lax
from jax.experimental import pallas as pl
from jax.experimental.pallas import tpu as pltpu

f32 = jnp.float32
bf16 = jnp.bfloat16
HI = lax.Precision.HIGHEST
MESH = pl.DeviceIdType.MESH

D = 1024
NH = 8
HD = 128
CH = 64
KC = 31
KG = 4
HALO_C = 32
HALO_G = 8
LANE = 128
STRIP = 32
N_SHIFT = 7
LN_EPS = 1e-5
RMS_EPS = 1e-6
L2_EPS = 1e-6
DN_ALPHA = 2.0 ** 0.25
N_CHIPS = 4
W_IN_COLS = 9232
W_IN_BLK = W_IN_COLS // N_CHIPS
SQ_BLK = D // N_CHIPS
VMEM_LIMIT = 52 * 1024 * 1024
MM_TILE = 1024
MM_K_TILE = 2048
GRAD_X_TILE = (256, 1024)
TOKEN_TILE = 256
SCAN_GROUP = 4

ADAM_LR = 0.001
ADAM_B1 = 0.9
ADAM_B2 = 0.999
ADAM_EPS = 1e-08
ADAM_WD = 0.01
ADAM_STEP = 10


def _sigmoid(x):
    return 1.0 / (1.0 + jnp.exp(-x))


def _silu_and_grad(x):
    s = _sigmoid(x)
    return x * s, s * (1.0 + x * (1.0 - s))


_NN = ((1,), (0,))
_NT = ((1,), (1,))
_TN = ((0,), (0,))


def _cparams(*sem):
    return pltpu.CompilerParams(dimension_semantics=sem, vmem_limit_bytes=VMEM_LIMIT)


def _mm_multi(a_list, b_list, addend=None, *, out_dtype, tm, tn, name, rhs_t=False, scatter=()):
    n_pairs = len(a_list)
    m = a_list[0].shape[0]
    n = b_list[0].shape[0 if rhs_t else 1]
    has_add = addend is not None
    dims = (_NT if rhs_t else _NN, ((), ()))
    n_in = 2 * n_pairs + has_add
    k = len(scatter)
    grid = (n // tn, m // tm)

    def body(*refs):
        a_refs = refs[:n_pairs]
        b_refs = refs[n_pairs:2 * n_pairs]
        o_ref = refs[n_in + k]
        if k:
            start, finish = _chip_exchange_ops(refs[n_in:n_in + k], refs[n_in + k + 1:n_in + 2 * k + 1],
                                               *refs[n_in + 2 * k + 1:], True)
            step = pl.program_id(0) * grid[1] + pl.program_id(1)
            pl.when(step == 0)(start)
        acc = None
        for a_ref, b_ref in zip(a_refs, b_refs):
            p = lax.dot_general(a_ref[...].astype(bf16), b_ref[...].astype(bf16), dims, preferred_element_type=f32)
            acc = p if acc is None else acc + p
        if has_add:
            acc = acc + refs[2 * n_pairs][...]
        o_ref[...] = acc.astype(out_dtype)
        if k:
            pl.when(step == grid[0] * grid[1] - 1)(finish)

    in_specs = [pl.BlockSpec((tm, a.shape[1]), lambda j, i: (i, 0)) for a in a_list]
    once = dict(pipeline_mode=pl.Buffered(1)) if tn == n else {}
    if rhs_t:
        in_specs += [pl.BlockSpec((tn, b.shape[1]), lambda j, i: (j, 0), **once) for b in b_list]
    else:
        in_specs += [pl.BlockSpec((b.shape[0], tn), lambda j, i: (0, j), **once) for b in b_list]
    args = list(a_list) + list(b_list)
    if has_add:
        in_specs.append(pl.BlockSpec((tm, tn), lambda j, i: (i, j)))
        args.append(addend)
    out = pl.pallas_call(
        body, name=name, grid=grid,
        in_specs=in_specs + _any_specs(k), out_specs=[pl.BlockSpec((tm, tn), lambda j, i: (i, j))] + _any_specs(k),
        out_shape=[jax.ShapeDtypeStruct((m, n), out_dtype)] + _chip_exchange_shapes(scatter, True),
        scratch_shapes=_chip_exchange_sems(k) if k else [],
        compiler_params=_cparams("arbitrary", "arbitrary") if k else _cparams("parallel", "parallel"),
    )(*args, *scatter)
    return out if k else out[0]


def _mm_kloop(a, b, *, tm, tn, tk, name):
    k, m = a.shape
    n = b.shape[1]
    nk = k // tk

    def body(a_ref, b_ref, o_ref):
        @pl.when(pl.program_id(2) == 0)
        def _():
            o_ref[...] = jnp.zeros_like(o_ref)
        o_ref[...] += lax.dot_general(a_ref[...].astype(bf16), b_ref[...].astype(bf16), (_TN, ((), ())),
                                      preferred_element_type=f32)

    return pl.pallas_call(
        body, name=name, grid=(n // tn, m // tm, nk),
        in_specs=[pl.BlockSpec((tk, tm), lambda j, i, kk: (kk, i)), pl.BlockSpec((tk, tn), lambda j, i, kk: (kk, j))],
        out_specs=pl.BlockSpec((tm, tn), lambda j, i, kk: (i, j)),
        out_shape=jax.ShapeDtypeStruct((m, n), f32),
        compiler_params=_cparams("parallel", "parallel", "arbitrary"),
    )(a, b)


def _shift_copies(src_ref, sh_ref, n, shifts=tuple(range(1, 8))):
    for i, b in enumerate(shifts):
        sh_ref[i, 0:n, :] = src_ref[pl.ds(b, n), :]


def _by_residue(offs):
    groups = {}
    for k, off in enumerate(offs):
        groups.setdefault(off % 8, []).append((k, off // 8))
    return groups


def _slab(src_ref, sh_ref, shifts, b, r0, n, lanes):
    ref = src_ref if b == 0 else sh_ref.at[shifts.index(b)]
    return ref[r0:r0 + n, lanes]


def _tap_conv(out_ref, n_rows, src_ref, sh_ref, w_ref, offs, bias_ref=None, shifts=tuple(range(1, 8))):
    groups = _by_residue(offs)
    for j in range(D // LANE):
        lanes = slice(j * LANE, (j + 1) * LANE)
        wv = [w_ref[k:k + 1, lanes] for k in range(len(offs))]
        for r0 in range(0, n_rows, STRIP):
            n = min(STRIP, n_rows - r0)
            accs = [jnp.zeros((n, LANE), f32) if bias_ref is None else jnp.broadcast_to(bias_ref[0:1, lanes], (n, LANE)),
                    jnp.zeros((n, LANE), f32)]
            m = 0
            for b, taps in groups.items():
                a_lo = min(a for _, a in taps)
                a_hi = max(a for _, a in taps)
                wide = _slab(src_ref, sh_ref, shifts, b, r0 + 8 * a_lo, 8 * (a_hi - a_lo) + n, lanes)
                for k, a in taps:
                    accs[m % 2] = accs[m % 2] + wv[k] * wide[8 * (a - a_lo):8 * (a - a_lo) + n]
                    m += 1
            out_ref[r0:r0 + n, lanes] = accs[0] + accs[1]


def _tap_corr(dw_ref, n_rows, lhs_ref, src_ref, sh_ref, offs, shifts=tuple(range(1, 8))):
    groups = _by_residue(offs)
    for j in range(D // LANE):
        lanes = slice(j * LANE, (j + 1) * LANE)
        accs = [jnp.zeros((8, LANE), f32) for _ in offs]
        for r0 in range(0, n_rows, STRIP):
            n = min(STRIP, n_rows - r0)
            d = lhs_ref[r0:r0 + n, lanes]
            for b, taps in groups.items():
                a_lo = min(a for _, a in taps)
                a_hi = max(a for _, a in taps)
                wide = _slab(src_ref, sh_ref, shifts, b, r0 + 8 * a_lo, 8 * (a_hi - a_lo) + n, lanes)
                for k, a in taps:
                    prod = d * wide[8 * (a - a_lo):8 * (a - a_lo) + n]
                    part = prod[0:8]
                    for q in range(1, n // 8):
                        part = part + prod[8 * q:8 * q + 8]
                    accs[k] = accs[k] + part
        for k in range(len(offs)):
            dw_ref[k:k + 1, lanes] += jnp.sum(accs[k], axis=0, keepdims=True)


_FWD_OFFS = [HALO_C - (KC - 1) + k for k in range(KC)]
_BWD_OFFS = [KC - 1 - k for k in range(KC)]


def _norm_act(a1, cz, g_ref, bb_ref):
    mu = jnp.mean(a1, axis=-1, keepdims=True)
    cen = a1 - mu
    var = jnp.mean(cen * cen, axis=-1, keepdims=True)
    rstd = lax.rsqrt(var + LN_EPS)
    xhat = cen * rstd
    ln = xhat * g_ref[...] + bb_ref[...]
    s, ds = _silu_and_grad(ln)
    zc, dzc = _silu_and_grad(cz)
    return xhat, rstd, s, ds, zc, dzc


def _conv_fwd(p_conv, dw_w, dw_b, ln_g, ln_b, w_out, *, tt):
    t = p_conv.shape[0]
    hb = tt // HALO_C

    def body(cv_ref, cg_ref, cz_ref, cvh_ref, cgh_ref, w_ref, b_ref, g_ref, bb_ref, wo_ref,
             u_ref, a1_ref, y_ref, ext_ref, sh_ref):
        first = pl.program_id(0) == 0
        halo = cvh_ref[...] * _sigmoid(cgh_ref[...])
        ext_ref[0:HALO_C, :] = jnp.where(first, 0.0, halo)
        ext_ref[HALO_C:, :] = cv_ref[...] * _sigmoid(cg_ref[...])
        _shift_copies(ext_ref, sh_ref, tt + HALO_C - 8)
        _tap_conv(a1_ref, tt, ext_ref, sh_ref, w_ref, _FWD_OFFS, b_ref)
        _, _, s, _, zc, _ = _norm_act(a1_ref[...], cz_ref[...], g_ref, bb_ref)
        u = (s * zc).astype(bf16)
        u_ref[...] = u
        y_ref[...] = jnp.dot(u, wo_ref[...], preferred_element_type=f32)

    def main(col):
        return pl.BlockSpec((tt, D), lambda i: (i, col))

    def prev(col):
        return pl.BlockSpec((HALO_C, D), lambda i: (jnp.maximum(i * hb - 1, 0), col))

    vec = pl.BlockSpec((1, D), lambda i: (0, 0))
    return pl.pallas_call(
        body, name="conv_fwd", grid=(t // tt,),
        in_specs=[main(0), main(1), main(2), prev(0), prev(1),
                  pl.BlockSpec((HALO_C, D), lambda i: (0, 0)), vec, vec, vec, pl.BlockSpec((D, D), lambda i: (0, 0))],
        out_specs=[pl.BlockSpec((tt, D), lambda i: (i, 0))] * 3,
        out_shape=[jax.ShapeDtypeStruct((t, D), bf16), jax.ShapeDtypeStruct((t, D), f32),
                   jax.ShapeDtypeStruct((t, D), f32)],
        scratch_shapes=[pltpu.VMEM((tt + HALO_C, D), f32), pltpu.VMEM((N_SHIFT, tt + HALO_C - 8, D), f32)],
        compiler_params=_cparams("parallel"),
    )(p_conv, p_conv, p_conv, p_conv, p_conv, dw_w, dw_b, ln_g, ln_b, w_out)


def _conv_bwd(p_conv, a1, du, dw_w, ln_g, ln_b, *, tt):
    t = p_conv.shape[0]
    hb = tt // HALO_C
    n_tiles = t // tt
    last_hb = t // HALO_C - 1
    ne = tt + HALO_C

    def body(cv_ref, cg_ref, cz_ref, a1_ref, du_ref, cvp_ref, cgp_ref, czn_ref, a1n_ref, dun_ref,
             w_ref, g_ref, bb_ref, dp_ref, dww_ref, dvec_ref, ext_ref, sh_ref, da1_ref, da0_ref):
        i = pl.program_id(0)
        first = i == 0
        last = i == n_tiles - 1

        @pl.when(first)
        def _():
            dww_ref[...] = jnp.zeros_like(dww_ref)
            dvec_ref[...] = jnp.zeros_like(dvec_ref)

        sig = _sigmoid(cg_ref[...])
        ext_ref[0:HALO_C, :] = jnp.where(first, 0.0, cvp_ref[...] * _sigmoid(cgp_ref[...]))
        ext_ref[HALO_C:, :] = cv_ref[...] * sig
        a1_all = jnp.concatenate([a1_ref[...], a1n_ref[...]], axis=0)
        cz = jnp.concatenate([cz_ref[...], czn_ref[...]], axis=0)
        du_all = jnp.concatenate([du_ref[...], jnp.where(last, 0.0, dun_ref[...])], axis=0)
        xhat, rstd, s, ds, zc, dzc = _norm_act(a1_all, cz, g_ref, bb_ref)
        dln = du_all * zc * ds
        dxhat = dln * g_ref[...]
        da1 = rstd * (dxhat - jnp.mean(dxhat, axis=-1, keepdims=True)
                      - xhat * jnp.mean(dxhat * xhat, axis=-1, keepdims=True))
        da1_ref[...] = da1
        dcz = (du_all * s * dzc)[:tt]
        dvec_ref[0:1, :] += jnp.sum(da1[:tt], axis=0, keepdims=True)
        dvec_ref[1:2, :] += jnp.sum((dln * xhat)[:tt], axis=0, keepdims=True)
        dvec_ref[2:3, :] += jnp.sum(dln[:tt], axis=0, keepdims=True)
        _shift_copies(ext_ref, sh_ref, ne - 8)
        _tap_corr(dww_ref, tt, da1_ref, ext_ref, sh_ref, _FWD_OFFS)
        _shift_copies(da1_ref, sh_ref, ne - 8)
        _tap_conv(da0_ref, tt, da1_ref, sh_ref, w_ref, _BWD_OFFS)
        da0 = da0_ref[...]
        cv = cv_ref[...]
        dp_ref[:, 0:D] = (da0 * sig).astype(bf16)
        dp_ref[:, D:2 * D] = (da0 * cv * sig * (1.0 - sig)).astype(bf16)
        dp_ref[:, 2 * D:] = dcz.astype(bf16)

    def main(col):
        return pl.BlockSpec((tt, D), lambda i: (i, col))

    def prev(col):
        return pl.BlockSpec((HALO_C, D), lambda i: (jnp.maximum(i * hb - 1, 0), col))

    def nxt(col):
        return pl.BlockSpec((HALO_C, D), lambda i: (jnp.minimum((i + 1) * hb, last_hb), col))

    vec = pl.BlockSpec((1, D), lambda i: (0, 0))
    return pl.pallas_call(
        body, name="conv_bwd", grid=(n_tiles,),
        in_specs=[main(0), main(1), main(2), main(0), main(0), prev(0), prev(1), nxt(2), nxt(0), nxt(0),
                  pl.BlockSpec((HALO_C, D), lambda i: (0, 0)), vec, vec],
        out_specs=[pl.BlockSpec((tt, 3 * D), lambda i: (i, 0)),
                   pl.BlockSpec((HALO_C, D), lambda i: (0, 0)),
                   pl.BlockSpec((8, D), lambda i: (0, 0))],
        out_shape=[jax.ShapeDtypeStruct((t, 3 * D), bf16), jax.ShapeDtypeStruct((HALO_C, D), f32),
                   jax.ShapeDtypeStruct((8, D), f32)],
        scratch_shapes=[pltpu.VMEM((ne, D), f32), pltpu.VMEM((N_SHIFT, ne - 8, D), f32),
                        pltpu.VMEM((ne, D), f32), pltpu.VMEM((tt, D), f32)],
        compiler_params=_cparams("arbitrary"),
    )(p_conv, p_conv, p_conv, a1, du, p_conv, p_conv, p_conv, a1, du, dw_w, ln_g, ln_b)


def _dot_hi(a, b):
    return lax.dot_general(a, b, (((1,), (0,)), ((), ())), precision=HI, preferred_element_type=f32)


def _chunk_tri(n, lower):
    r = lax.broadcasted_iota(jnp.int32, (n, n), 0)
    c = lax.broadcasted_iota(jnp.int32, (n, n), 1)
    tri = (r >= c) if lower else (r <= c)
    return jnp.where(tri & (r // CH == c // CH), 1.0, 0.0).astype(f32)


def _softplus_and_sigmoid(x):
    e = jnp.exp(-jnp.abs(x))
    log1p = jnp.where(e < 1e-2, e * (1.0 - e * (0.5 - e * (1.0 / 3.0 - 0.25 * e))), jnp.log(1.0 + e))
    return jnp.maximum(x, 0.0) + log1p, _sigmoid(x)


_G_FWD_OFFS = [HALO_G - (KG - 1) + k for k in range(KG)]
_G_FWD_SHIFTS = (5, 6, 7)
_G_BWD_OFFS = [KG - 1 - k for k in range(KG)]
_G_BWD_SHIFTS = (1, 2, 3)


def _gdn_short_conv(pre_ref, ext_ref, sh_ref, n_rows, w_ref):
    _shift_copies(ext_ref, sh_ref, n_rows, _G_FWD_SHIFTS)
    _tap_conv(pre_ref, n_rows, ext_ref, sh_ref, w_ref, _G_FWD_OFFS, shifts=_G_FWD_SHIFTS)
    return pre_ref[...]


def _l2norm_heads(act, scale):
    outs, rs = [], []
    for h in range(NH):
        a = act[:, h * HD:(h + 1) * HD]
        r = lax.rsqrt(jnp.sum(a * a, axis=-1, keepdims=True) + L2_EPS)
        outs.append(a * (r * scale))
        rs.append(jnp.broadcast_to(r, a.shape))
    return jnp.concatenate(outs, axis=-1), jnp.concatenate(rs, axis=-1)


def _gate_math(ba, al_ref, dt_ref):
    lane = lax.broadcasted_iota(jnp.int32, ba.shape, 1)
    is_b = lane < NH
    is_a = (lane >= NH) & (lane < 2 * NH)
    sp, sg = _softplus_and_sigmoid(ba + dt_ref[...])
    neg_a = -jnp.exp(al_ref[...])
    return is_b, is_a, _sigmoid(ba), neg_a * sp, sg, neg_a


def _gdn_pre_fwd(p_qkv, p_ba, cw, alog_v, dt_v, *, tt):
    t = p_qkv.shape[0]
    hb = tt // HALO_G

    def body(q_ref, k_ref, v_ref, qh_ref, kh_ref, vh_ref, ba_ref, wq_ref, wk_ref, wv_ref, al_ref, dt_ref,
             qn_ref, kn_ref, va_ref, gt_ref, pre_ref, ext_ref, sh_ref):
        first = pl.program_id(0) == 0

        def conv_act(x_ref, xh_ref, w_ref, col):
            ext_ref[0:HALO_G, :] = jnp.where(first, 0.0, xh_ref[...])
            ext_ref[HALO_G:, :] = x_ref[...]
            pre = _gdn_short_conv(pre_ref.at[:, col * D:(col + 1) * D], ext_ref, sh_ref, tt, w_ref)
            return pre * _sigmoid(pre)

        qn_ref[...] = _l2norm_heads(conv_act(q_ref, qh_ref, wq_ref, 0), HD ** -0.5)[0]
        kn_ref[...] = _l2norm_heads(conv_act(k_ref, kh_ref, wk_ref, 1), 1.0)[0]
        va_ref[...] = conv_act(v_ref, vh_ref, wv_ref, 2)
        is_b, is_a, beta, g, _, _ = _gate_math(ba_ref[...], al_ref, dt_ref)
        gc = _dot_hi(_chunk_tri(tt, lower=True), jnp.where(is_a, g, 0.0))
        gt_ref[...] = jnp.where(is_b, beta, gc)

    def main(col):
        return pl.BlockSpec((tt, D), lambda i: (i, col))

    def prev(col):
        return pl.BlockSpec((HALO_G, D), lambda i: (jnp.maximum(i * hb - 1, 0), col))

    def wspec(col):
        return pl.BlockSpec((8, D), lambda i: (0, col))

    vec = pl.BlockSpec((1, HD), lambda i: (0, 0))
    gblk = pl.BlockSpec((tt, HD), lambda i: (i, 0))
    sds = jax.ShapeDtypeStruct((t, D), f32)
    return pl.pallas_call(
        body, name="gdn_pre_fwd", grid=(t // tt,),
        in_specs=[main(0), main(1), main(2), prev(0), prev(1), prev(2), gblk, wspec(0), wspec(1), wspec(2), vec, vec],
        out_specs=[pl.BlockSpec((tt, D), lambda i: (i, 0))] * 3 + [gblk, pl.BlockSpec((tt, 3 * D), lambda i: (i, 0))],
        out_shape=[sds] * 3 + [jax.ShapeDtypeStruct((t, HD), f32), jax.ShapeDtypeStruct((t, 3 * D), f32)],
        scratch_shapes=[pltpu.VMEM((tt + HALO_G, D), f32), pltpu.VMEM((KG - 1, tt, D), f32)],
        compiler_params=_cparams("parallel"),
    )(p_qkv, p_qkv, p_qkv, p_qkv, p_qkv, p_qkv, p_ba, cw, cw, cw, alog_v, dt_v)


def _gdn_pre_bwd(p_qkv, p_ba, pre, cw, alog_v, dt_v, dqn, dkn, dva, dgt, *, tt):
    t = p_qkv.shape[0]
    hb = tt // HALO_G
    n_tiles = t // tt
    last_hb = t // HALO_G - 1
    ne = tt + HALO_G

    def body(q_ref, k_ref, v_ref, qp_ref, kp_ref, vp_ref, pq_ref, pk_ref, pv_ref, pqx_ref, pkx_ref, pvx_ref,
             dq_ref, dk_ref, dv_ref, dqx_ref, dkx_ref, dvx_ref, ba_ref, dgt_ref,
             wq_ref, wk_ref, wv_ref, al_ref, dt_ref,
             dp_ref, dba_ref, dcw_ref, dad_ref, ext_ref, sh_ref, dpre_ref, draw_ref):
        i = pl.program_id(0)
        first = i == 0
        last = i == n_tiles - 1

        @pl.when(first)
        def _():
            dcw_ref[...] = jnp.zeros_like(dcw_ref)
            dad_ref[...] = jnp.zeros_like(dad_ref)

        def one(x_ref, xp_ref, pre_ref, prex_ref, d_ref, dx_ref, w_ref, col, scale):
            ext_ref[0:HALO_G, :] = jnp.where(first, 0.0, xp_ref[...])
            ext_ref[HALO_G:, :] = x_ref[...]
            _shift_copies(ext_ref, sh_ref, tt, _G_FWD_SHIFTS)
            act, dact = _silu_and_grad(jnp.concatenate([pre_ref[...], prex_ref[...]], axis=0))
            d_out = jnp.concatenate([d_ref[...], jnp.where(last, 0.0, dx_ref[...])], axis=0)
            if scale is None:
                d_act = d_out
            else:
                parts = []
                for h in range(NH):
                    a = act[:, h * HD:(h + 1) * HD]
                    dn = d_out[:, h * HD:(h + 1) * HD]
                    r = lax.rsqrt(jnp.sum(a * a, axis=-1, keepdims=True) + L2_EPS)
                    parts.append(scale * r * (dn - a * (r * r) * jnp.sum(dn * a, axis=-1, keepdims=True)))
                d_act = jnp.concatenate(parts, axis=-1)
            dpre_ref[...] = d_act * dact
            _tap_corr(dcw_ref.at[:, col * D:(col + 1) * D], tt, dpre_ref, ext_ref, sh_ref, _G_FWD_OFFS, shifts=_G_FWD_SHIFTS)
            _shift_copies(dpre_ref, sh_ref, tt, _G_BWD_SHIFTS)
            _tap_conv(draw_ref, tt, dpre_ref, sh_ref, w_ref, _G_BWD_OFFS, shifts=_G_BWD_SHIFTS)
            dp_ref[:, col * D:(col + 1) * D] = draw_ref[...].astype(bf16)

        one(q_ref, qp_ref, pq_ref, pqx_ref, dq_ref, dqx_ref, wq_ref, 0, HD ** -0.5)
        one(k_ref, kp_ref, pk_ref, pkx_ref, dk_ref, dkx_ref, wk_ref, 1, 1.0)
        one(v_ref, vp_ref, pv_ref, pvx_ref, dv_ref, dvx_ref, wv_ref, 2, None)

        is_b, is_a, beta, g, sg, neg_a = _gate_math(ba_ref[...], al_ref, dt_ref)
        dgt_v = dgt_ref[...]
        dg = _dot_hi(_chunk_tri(tt, lower=False), jnp.where(is_a, dgt_v, 0.0))
        d_al = jnp.where(is_a, dg * neg_a * sg, 0.0)
        dba_ref[...] = jnp.where(is_b, dgt_v * beta * (1.0 - beta), d_al)
        dad_ref[0:1, :] += jnp.sum(jnp.where(is_a, dg * g, 0.0), axis=0, keepdims=True)
        dad_ref[1:2, :] += jnp.sum(d_al, axis=0, keepdims=True)

    def main(col):
        return pl.BlockSpec((tt, D), lambda i: (i, col))

    def prev(col):
        return pl.BlockSpec((HALO_G, D), lambda i: (jnp.maximum(i * hb - 1, 0), col))

    def nxt(col):
        return pl.BlockSpec((HALO_G, D), lambda i: (jnp.minimum((i + 1) * hb, last_hb), col))

    def wspec(col):
        return pl.BlockSpec((8, D), lambda i: (0, col))

    vec = pl.BlockSpec((1, HD), lambda i: (0, 0))
    gblk = pl.BlockSpec((tt, HD), lambda i: (i, 0))
    return pl.pallas_call(
        body, name="gdn_pre_bwd", grid=(n_tiles,),
        in_specs=[main(0), main(1), main(2), prev(0), prev(1), prev(2), main(0), main(1), main(2), nxt(0), nxt(1), nxt(2),
                  main(0), main(0), main(0), nxt(0), nxt(0), nxt(0), gblk, gblk,
                  wspec(0), wspec(1), wspec(2), vec, vec],
        out_specs=[pl.BlockSpec((tt, 3 * D), lambda i: (i, 0)), pl.BlockSpec((tt, HD), lambda i: (i, 0)),
                   pl.BlockSpec((8, 3 * D), lambda i: (0, 0)), pl.BlockSpec((8, HD), lambda i: (0, 0))],
        out_shape=[jax.ShapeDtypeStruct((t, 3 * D), bf16), jax.ShapeDtypeStruct((t, HD), f32),
                   jax.ShapeDtypeStruct((8, 3 * D), f32), jax.ShapeDtypeStruct((8, HD), f32)],
        scratch_shapes=[pltpu.VMEM((HALO_G + tt, D), f32), pltpu.VMEM((KG - 1, tt, D), f32),
                        pltpu.VMEM((ne, D), f32), pltpu.VMEM((tt, D), f32)],
        compiler_params=_cparams("arbitrary"),
    )(p_qkv, p_qkv, p_qkv, p_qkv, p_qkv, p_qkv, pre, pre, pre, pre, pre, pre,
      dqn, dkn, dva, dqn, dkn, dva, p_ba, dgt, cw, cw, cw, alog_v, dt_v)


def _dot_b(a, b, dims):
    return lax.dot_general(a.astype(bf16), b.astype(bf16), (dims, ((), ())), preferred_element_type=f32)


def _inverse_by_doubling(ms):
    heads = range(len(ms))
    r = lax.broadcasted_iota(jnp.int32, (CH, CH), 0)
    c = lax.broadcasted_iota(jnp.int32, (CH, CH), 1)
    eye = jnp.where(r == c, 1.0, 0.0).astype(f32)
    p = [eye + ms[h] for h in heads]
    mp = ms
    for _ in range(5):
        mp = [_dot_b(mp[h], mp[h], _NN) for h in heads]
        pm = [_dot_b(p[h], mp[h], _NN) for h in heads]
        p = [p[h] + pm[h] for h in heads]
    return tuple(p)


@jax.custom_vjp
def _known_inverse(ms, ps):
    return ps


def _known_inverse_fwd(ms, ps):
    return ps, ps


def _known_inverse_bwd(ps, cts):
    heads = range(len(ps))
    left = [_dot_b(ps[h], cts[h], _TN) for h in heads]
    return tuple(_dot_b(left[h], ps[h], _NT) for h in heads), tuple(jnp.zeros_like(p) for p in ps)


_known_inverse.defvjp(_known_inverse_fwd, _known_inverse_bwd)


def _chunk_prepare(qs, ks, vs, gcs, bbs, ps=None):
    heads = range(len(qs))
    r = lax.broadcasted_iota(jnp.int32, (CH, CH), 0)
    c = lax.broadcasted_iota(jnp.int32, (CH, CH), 1)
    causal = r >= c
    strict = r > c
    gc_row = [gcs[h].T[:CH, :] for h in heads]
    decay = [jnp.where(causal, jnp.exp(jnp.where(causal, gcs[h][:, :CH] - gc_row[h], 0.0)), 0.0) for h in heads]
    kb = [ks[h] * bbs[h] for h in heads]
    egc = [jnp.exp(gcs[h]) for h in heads]
    kk = [_dot_b(kb[h], ks[h], _NT) for h in heads]
    qk = [_dot_b(qs[h], ks[h], _NT) for h in heads]
    m = tuple(-jnp.where(strict, kk[h] * decay[h], 0.0) for h in heads)
    p = _inverse_by_doubling(m) if ps is None else _known_inverse(m, ps)
    u = [_dot_b(p[h], vs[h] * bbs[h], _NN) for h in heads]
    w = [_dot_b(p[h], kb[h] * egc[h], _NN) for h in heads]
    intra = [jnp.where(causal, qk[h] * decay[h], 0.0) for h in heads]
    g_last = [gcs[h][CH - 1:CH, :] for h in heads]
    k_dec = [ks[h] * jnp.exp(g_last[h] - gcs[h]) for h in heads]
    q_dec = [qs[h] * egc[h] for h in heads]
    e_last = [jnp.exp(g_last[h]) for h in heads]
    return u, w, intra, q_dec, k_dec, e_last, p


def _chunk_apply(u, w, intra, q_dec, k_dec, e_last, ss):
    heads = range(len(ss))
    ws = [_dot_b(w[h], ss[h], _NN) for h in heads]
    qs_s = [_dot_b(q_dec[h], ss[h], _NN) for h in heads]
    v_new = [u[h] - ws[h] for h in heads]
    iv = [_dot_b(intra[h], v_new[h], _NN) for h in heads]
    kv = [_dot_b(k_dec[h], v_new[h], _TN) for h in heads]
    o = tuple(qs_s[h] + iv[h] for h in heads)
    s_new = tuple(ss[h] * e_last[h] + kv[h] for h in heads)
    return o, s_new


def _chunk_group_fn(ins, ss, ps=None):
    n = len(ss)
    prep = _chunk_prepare(*(sum((tuple(c[i]) for c in ins), ()) for i in range(5)), ps=ps)
    outs, befores = [], []
    for g in range(len(ins)):
        befores.append(ss)
        o, ss = _chunk_apply(*(x[g * n:(g + 1) * n] for x in prep[:6]), ss)
        outs.append(o)
    return tuple(outs), tuple(befores), ss, prep[6]


def _head_cols():
    return [slice(h * HD, (h + 1) * HD) for h in range(NH)]


def _head_gates(gt):
    gcs = tuple(jnp.broadcast_to(gt[:, NH + h:NH + h + 1], (CH, HD)) for h in range(NH))
    bbs = tuple(jnp.broadcast_to(gt[:, h:h + 1], (CH, HD)) for h in range(NH))
    return gcs, bbs


def _gdn_scan_fwd(qn, kn, va, gates, *, tt):
    t = qn.shape[0]
    cpb = tt // CH
    group = min(SCAN_GROUP, cpb)

    def body(q_ref, k_ref, v_ref, gt_ref, o_ref, st_ref, p_ref, s_scr):
        @pl.when(pl.program_id(0) == 0)
        def _():
            s_scr[...] = jnp.zeros_like(s_scr)

        cols = _head_cols()

        def inputs(ci):
            rows = pl.ds(pl.multiple_of(ci * CH, CH), CH)
            gcs, bbs = _head_gates(gt_ref[rows, :])
            return tuple(tuple(ref[rows, cl] for cl in cols) for ref in (q_ref, k_ref, v_ref)) + (gcs, bbs)

        def step(gi, carry):
            chunks = [group * gi + g for g in range(group)]
            outs, befores, s_end, p = _chunk_group_fn([inputs(ci) for ci in chunks], tuple(s_scr[h] for h in range(NH)))
            for g, ci in enumerate(chunks):
                rows = pl.ds(pl.multiple_of(ci * CH, CH), CH)
                for h in range(NH):
                    st_ref[ci, h] = befores[g][h]
                    o_ref[rows, cols[h]] = outs[g][h]
                    p_ref[ci, h] = p[g * NH + h].astype(bf16)
            for h in range(NH):
                s_scr[h] = s_end[h]
            return carry

        lax.fori_loop(0, cpb // group, step, 0)

    blk = pl.BlockSpec((tt, D), lambda i: (i, 0))
    return pl.pallas_call(
        body, name="gdn_scan_fwd", grid=(t // tt,),
        in_specs=[blk] * 3 + [pl.BlockSpec((tt, HD), lambda i: (i, 0))],
        out_specs=[blk, pl.BlockSpec((cpb, NH, HD, HD), lambda i: (i, 0, 0, 0)),
                   pl.BlockSpec((cpb, NH, CH, CH), lambda i: (i, 0, 0, 0))],
        out_shape=[jax.ShapeDtypeStruct((t, D), f32), jax.ShapeDtypeStruct((t // CH, NH, HD, HD), f32),
                   jax.ShapeDtypeStruct((t // CH, NH, CH, CH), bf16)],
        scratch_shapes=[pltpu.VMEM((NH, HD, HD), f32)],
        compiler_params=_cparams("arbitrary"),
    )(qn, kn, va, gates)


def _gdn_scan_bwd(qn, kn, va, gates, states, inverses, do, *, tt):
    t = qn.shape[0]
    nblk = t // tt
    cpb = tt // CH

    def body(q_ref, k_ref, v_ref, gt_ref, st_ref, p_ref, do_ref, dq_ref, dk_ref, dv_ref, dgt_ref, ds_scr):
        @pl.when(pl.program_id(0) == 0)
        def _():
            ds_scr[...] = jnp.zeros_like(ds_scr)

        cols = _head_cols()

        def rows_of(ci):
            return pl.ds(pl.multiple_of(ci * CH, CH), CH)

        def inputs(ci):
            gcs, bbs = _head_gates(gt_ref[rows_of(ci), :])
            return tuple(tuple(ref[rows_of(ci), cl] for cl in cols) for ref in (q_ref, k_ref, v_ref)) + (gcs, bbs)

        def step(j, carry):
            ci = cpb - 1 - j
            ps = tuple(p_ref[ci, h].astype(f32) for h in range(NH))

            def one(ins, ss):
                outs, _, s_end, _ = _chunk_group_fn([ins], ss, ps=ps)
                return outs[0], s_end

            _, vjp = jax.vjp(one, inputs(ci), tuple(st_ref[ci, h] for h in range(NH)))
            grads, ds = vjp((tuple(do_ref[rows_of(ci), cl] for cl in cols), tuple(ds_scr[h] for h in range(NH))))
            lane = lax.broadcasted_iota(jnp.int32, (CH, HD), 1)
            dgt = jnp.zeros((CH, HD), f32)
            for h in range(NH):
                for ref, g in zip((dq_ref, dk_ref, dv_ref), grads[:3]):
                    ref[rows_of(ci), cols[h]] = g[h]
                dgt = dgt + jnp.where(lane == NH + h, jnp.sum(grads[3][h], axis=-1, keepdims=True), 0.0)
                dgt = dgt + jnp.where(lane == h, jnp.sum(grads[4][h], axis=-1, keepdims=True), 0.0)
                ds_scr[h] = ds[h]
            dgt_ref[rows_of(ci), :] = dgt
            return carry

        lax.fori_loop(0, cpb, step, 0)

    blk = pl.BlockSpec((tt, D), lambda i: (nblk - 1 - i, 0))
    sblk = pl.BlockSpec((cpb, NH, HD, HD), lambda i: (nblk - 1 - i, 0, 0, 0))
    sds = jax.ShapeDtypeStruct((t, D), f32)
    gblk = pl.BlockSpec((tt, HD), lambda i: (nblk - 1 - i, 0))
    pblk = pl.BlockSpec((cpb, NH, CH, CH), lambda i: (nblk - 1 - i, 0, 0, 0))
    return pl.pallas_call(
        body, name="gdn_scan_bwd", grid=(nblk,),
        in_specs=[blk] * 3 + [gblk, sblk, pblk, blk],
        out_specs=[blk] * 3 + [gblk], out_shape=[sds] * 3 + [jax.ShapeDtypeStruct((t, HD), f32)],
        scratch_shapes=[pltpu.VMEM((NH, HD, HD), f32)],
        compiler_params=_cparams("arbitrary"),
    )(qn, kn, va, gates, states, inverses, do)


def _rms_heads(o):
    ons, rs = [], []
    for h in range(NH):
        a = o[:, h * HD:(h + 1) * HD]
        r = lax.rsqrt(jnp.mean(a * a, axis=-1, keepdims=True) + RMS_EPS)
        ons.append(a * r)
        rs.append(jnp.broadcast_to(r, a.shape))
    return jnp.concatenate(ons, axis=-1), jnp.concatenate(rs, axis=-1)


def _gdn_post_fwd(o, p_gz, ng_b, w_out, *, tt):
    t = o.shape[0]

    def body(o_ref, gz_ref, ng_ref, w_ref, og_ref, y_ref):
        on, _ = _rms_heads(o_ref[...])
        z, _ = _silu_and_grad(gz_ref[...])
        og = (on * ng_ref[...] * z).astype(bf16)
        og_ref[...] = og
        y_ref[...] = jnp.dot(og, w_ref[...], preferred_element_type=f32)

    blk = pl.BlockSpec((tt, D), lambda i: (i, 0))
    return pl.pallas_call(
        body, name="gdn_post_fwd", grid=(t // tt,),
        in_specs=[blk, blk, pl.BlockSpec((1, D), lambda i: (0, 0)), pl.BlockSpec((D, D), lambda i: (0, 0))],
        out_specs=[blk, blk], out_shape=[jax.ShapeDtypeStruct((t, D), bf16), jax.ShapeDtypeStruct((t, D), f32)],
        compiler_params=_cparams("parallel"),
    )(o, p_gz, ng_b, w_out)


def _gdn_post_bwd(o, p_gz, ng_b, w_out, dyg, *, tt):
    t = o.shape[0]

    def body(o_ref, gz_ref, ng_ref, w_ref, dyg_ref, do_ref, dgz_ref, dng_ref):
        @pl.when(pl.program_id(0) == 0)
        def _():
            dng_ref[...] = jnp.zeros_like(dng_ref)

        on, r = _rms_heads(o_ref[...])
        z, dz = _silu_and_grad(gz_ref[...])
        dog_v = lax.dot_general(dyg_ref[...], w_ref[...], (_NT, ((), ())), preferred_element_type=f32)
        ng = ng_ref[...]
        dgz_ref[...] = (dog_v * on * ng * dz).astype(bf16)
        dy = dog_v * z
        dng_all = jnp.sum(dy * on, axis=0, keepdims=True)
        dng = dng_all[:, 0:HD]
        for h in range(1, NH):
            dng = dng + dng_all[:, h * HD:(h + 1) * HD]
        dng_ref[0:1, :] += dng
        don = dy * ng
        prod = don * on
        parts = []
        for h in range(NH):
            sl = slice(h * HD, (h + 1) * HD)
            parts.append(don[:, sl] - on[:, sl] * jnp.mean(prod[:, sl], axis=-1, keepdims=True))
        do_ref[...] = r * jnp.concatenate(parts, axis=-1)

    blk = pl.BlockSpec((tt, D), lambda i: (i, 0))
    return pl.pallas_call(
        body, name="gdn_post_bwd", grid=(t // tt,),
        in_specs=[blk, blk, pl.BlockSpec((1, D), lambda i: (0, 0)), pl.BlockSpec((D, D), lambda i: (0, 0)), blk],
        out_specs=[blk, blk, pl.BlockSpec((8, HD), lambda i: (0, 0))],
        out_shape=[jax.ShapeDtypeStruct((t, D), f32), jax.ShapeDtypeStruct((t, D), bf16),
                   jax.ShapeDtypeStruct((8, HD), f32)],
        compiler_params=_cparams("arbitrary"),
    )(o, p_gz, ng_b, w_out, dyg)


def _merge(x, y_conf, y_gdn, p_gates, target, w_o, ln_g, ln_b, *, tt):
    t = x.shape[0]

    def body(x_ref, yc_ref, yg_ref, gc_ref, gg_ref, tg_ref, w_ref, g_ref, b_ref,
             loss_ref, dxd_ref, dyc_ref, dyg_ref, dpg_ref, h_ref, dz_ref, dvec_ref):
        @pl.when(pl.program_id(0) == 0)
        def _():
            loss_ref[...] = jnp.zeros_like(loss_ref)
            dvec_ref[...] = jnp.zeros_like(dvec_ref)

        sc = _sigmoid(gc_ref[...])
        sg = _sigmoid(gg_ref[...])
        yc = yc_ref[...]
        yg = yg_ref[...]
        h = (sc * yc + sg * yg).astype(bf16)
        h_ref[...] = h
        z = DN_ALPHA * x_ref[...] + jnp.dot(h, w_ref[...], preferred_element_type=f32)
        mu = jnp.mean(z, axis=-1, keepdims=True)
        cen = z - mu
        rstd = lax.rsqrt(jnp.mean(cen * cen, axis=-1, keepdims=True) + LN_EPS)
        xhat = cen * rstd
        err = xhat * g_ref[...] + b_ref[...] - tg_ref[...]
        loss_ref[...] += 0.5 / D * jnp.sum(err * err)
        dy = err * (1.0 / D)
        dvec_ref[0:1, :] += jnp.sum(dy * xhat, axis=0, keepdims=True)
        dvec_ref[1:2, :] += jnp.sum(dy, axis=0, keepdims=True)
        dxhat = dy * g_ref[...]
        dz = rstd * (dxhat - jnp.mean(dxhat, axis=-1, keepdims=True)
                     - xhat * jnp.mean(dxhat * xhat, axis=-1, keepdims=True))
        dxd_ref[...] = DN_ALPHA * dz
        dz_b = dz.astype(bf16)
        dz_ref[...] = dz_b
        dh = lax.dot_general(dz_b, w_ref[...], (_NT, ((), ())), preferred_element_type=f32)
        dyc_ref[...] = (dh * sc).astype(bf16)
        dyg_ref[...] = (dh * sg).astype(bf16)
        dpg_ref[:, 0:D] = (dh * yc * sc * (1.0 - sc)).astype(bf16)
        dpg_ref[:, D:] = (dh * yg * sg * (1.0 - sg)).astype(bf16)

    blk = pl.BlockSpec((tt, D), lambda i: (i, 0))
    wblk = pl.BlockSpec((D, D), lambda i: (0, 0))
    vec = pl.BlockSpec((1, D), lambda i: (0, 0))
    return pl.pallas_call(
        body, name="merge_norm_loss", grid=(t // tt,),
        in_specs=[blk, blk, blk, pl.BlockSpec((tt, D), lambda i: (i, 0)), pl.BlockSpec((tt, D), lambda i: (i, 1)),
                  blk, wblk, vec, vec],
        out_specs=[pl.BlockSpec((8, HD), lambda i: (0, 0)), blk, blk, blk,
                   pl.BlockSpec((tt, 2 * D), lambda i: (i, 0)), blk, blk, pl.BlockSpec((8, D), lambda i: (0, 0))],
        out_shape=[jax.ShapeDtypeStruct((8, HD), f32), jax.ShapeDtypeStruct((t, D), f32),
                   jax.ShapeDtypeStruct((t, D), bf16), jax.ShapeDtypeStruct((t, D), bf16),
                   jax.ShapeDtypeStruct((t, 2 * D), bf16), jax.ShapeDtypeStruct((t, D), bf16),
                   jax.ShapeDtypeStruct((t, D), bf16), jax.ShapeDtypeStruct((8, D), f32)],
        compiler_params=_cparams("arbitrary"),
    )(x, y_conf, y_gdn, p_gates, p_gates, target, w_o, ln_g, ln_b)


def _place():
    return lax.axis_index("x"), lax.axis_index("y"), lax.axis_index("c")


def _any_specs(n):
    return [pl.BlockSpec(memory_space=pl.ANY)] * n


def _sibling_merge(arrs, name, half_axes=None):
    k = len(arrs)

    def half_shape(a, ax):
        shape = list(a.shape)
        shape[ax] //= 2
        return tuple(shape)

    def body(*refs):
        a_refs, o_refs = refs[:k], refs[k:2 * k]
        send_sems, recv_sems = refs[2 * k:]
        x, y, c = _place()
        sends = []
        for i in range(k):
            src = a_refs[i]
            if half_axes is not None:
                n = a_refs[i].shape[half_axes[i]] // 2
                idx = [slice(None)] * len(a_refs[i].shape)
                idx[half_axes[i]] = pl.ds((1 - c) * n, n)
                src = a_refs[i].at[tuple(idx)]
            cp = pltpu.make_async_remote_copy(src_ref=src, dst_ref=o_refs[i], send_sem=send_sems.at[i],
                                              recv_sem=recv_sems.at[i], device_id=(x, y, 1 - c), device_id_type=MESH)
            cp.start()
            sends.append(cp)
        for cp in sends:
            cp.wait()

    shapes = [a.shape if half_axes is None else half_shape(a, ax) for a, ax in zip(arrs, half_axes or arrs)]
    return pl.pallas_call(
        body, name=name, in_specs=_any_specs(k), out_specs=_any_specs(k),
        out_shape=[jax.ShapeDtypeStruct(sh, a.dtype) for sh, a in zip(shapes, arrs)],
        scratch_shapes=[pltpu.SemaphoreType.DMA((k,)), pltpu.SemaphoreType.DMA((k,))],
    )(*arrs)


def _join_halves(mine, other, axis=-2):
    c = lax.axis_index("c")
    return jnp.concatenate([jnp.where(c == 0, mine, other), jnp.where(c == 0, other, mine)], axis=axis)


def _chip_exchange_ops(a_refs, o_refs, send_sems, recv_sems, local_sems, scatter):
    k = len(a_refs)
    x, y, c = _place()
    me = 2 * x + y
    peers = [(1 - x, y), (x, 1 - y), (1 - x, 1 - y)]

    def src(i, j):
        return a_refs[i].at[j] if scatter else a_refs[i]

    def copy(i, n, send_j, slot):
        px, py = peers[n]
        return pltpu.make_async_remote_copy(
            src_ref=src(i, send_j), dst_ref=o_refs[i].at[slot], send_sem=send_sems.at[3 * i + n],
            recv_sem=recv_sems.at[3 * i + n], device_id=(px, py, c), device_id_type=MESH)

    def owns():
        return [pltpu.make_async_copy(src(i, me), o_refs[i].at[me], local_sems.at[i]) for i in range(k)]

    def sends():
        return [copy(i, n, 2 * peers[n][0] + peers[n][1], me) for n in range(3) for i in range(k)]

    def start():
        for cp in owns() + sends():
            cp.start()

    def finish():
        for n in range(3):
            for i in range(k):
                copy(i, n, me, 2 * peers[n][0] + peers[n][1]).wait_recv()
        for cp in sends():
            cp.wait_send()
        for cp in owns():
            cp.wait()

    return start, finish


def _chip_exchange_shapes(arrs, scatter):
    return [jax.ShapeDtypeStruct((N_CHIPS,) + tuple(a.shape[1:] if scatter else a.shape), a.dtype) for a in arrs]


def _chip_exchange_sems(k):
    return [pltpu.SemaphoreType.DMA((3 * k,)), pltpu.SemaphoreType.DMA((3 * k,)), pltpu.SemaphoreType.DMA((k,))]


def _gather_halves(halves, wholes, name):
    kh, kw = len(halves), len(wholes)
    k = kh + kw

    def body(*refs):
        a_refs, got_refs, oth_refs = refs[:k], refs[k:2 * k], refs[2 * k:2 * k + kh]
        send_sems, recv_sems, local_sems, fwd_send_sems, fwd_recv_sems = refs[2 * k + kh:]
        x, y, c = _place()
        me = 2 * x + y
        start, _ = _chip_exchange_ops(a_refs, got_refs, send_sems, recv_sems, local_sems, False)
        slots = [me] + [2 * px + py for px, py in [(1 - x, y), (x, 1 - y), (1 - x, 1 - y)]]

        def forward(i, r):
            src = a_refs[i] if r == 0 else got_refs[i].at[slots[r]]
            return pltpu.make_async_remote_copy(
                src_ref=src, dst_ref=oth_refs[i].at[slots[r]], send_sem=fwd_send_sems.at[4 * i + r],
                recv_sem=fwd_recv_sems.at[4 * i + r], device_id=(x, y, 1 - c), device_id_type=MESH)

        def arrival(i, n):
            px, py = [(1 - x, y), (x, 1 - y), (1 - x, 1 - y)][n]
            return pltpu.make_async_remote_copy(
                src_ref=a_refs[i], dst_ref=got_refs[i].at[slots[n + 1]], send_sem=send_sems.at[3 * i + n],
                recv_sem=recv_sems.at[3 * i + n], device_id=(px, py, c), device_id_type=MESH)

        start()
        for i in range(kh):
            forward(i, 0).start()
        for n in range(3):
            for i in range(k):
                arrival(i, n).wait_recv()
                if i < kh:
                    forward(i, n + 1).start()
        for i in range(kh):
            for r in range(4):
                forward(i, r).wait()
        for n in range(3):
            for i in range(k):
                arrival(i, n).wait_send()
        for i in range(k):
            pltpu.make_async_copy(a_refs[i], got_refs[i].at[me], local_sems.at[i]).wait()

    arrs = list(halves) + list(wholes)
    shapes = _chip_exchange_shapes(arrs, False)
    out = pl.pallas_call(
        body, name=name, in_specs=_any_specs(k), out_specs=_any_specs(k + kh),
        out_shape=shapes + shapes[:kh],
        scratch_shapes=_chip_exchange_sems(k) + [pltpu.SemaphoreType.DMA((4 * kh,)), pltpu.SemaphoreType.DMA((4 * kh,))],
    )(*arrs)
    return [(out[i], out[k + i]) for i in range(kh)], out[kh:k]


def _pair_sum(g_all, got, c_arr, name, out_dtype, axis):
    rows, w = got.shape[1:]
    if axis == -2:
        blk, steps = (1, rows // 4, w), 4
        pick = lambda j, i, c_ref: (j, c_ref[0] * steps + i, 0)
        mine = lambda j, i, c_ref: (j, i, 0)
    else:
        blk, steps = (1, rows, LANE), w // LANE
        pick = lambda j, i, c_ref: (j, 0, c_ref[0] * steps + i)
        mine = lambda j, i, c_ref: (j, 0, i)

    def body(c_ref, a_ref, b_ref, o_ref):
        o_ref[...] = (a_ref[...] + b_ref[...]).astype(out_dtype)

    return pl.pallas_call(
        body, name=name,
        grid_spec=pltpu.PrefetchScalarGridSpec(
            num_scalar_prefetch=1, grid=(N_CHIPS, steps),
            in_specs=[pl.BlockSpec(blk, pick), pl.BlockSpec(blk, mine)], out_specs=pl.BlockSpec(blk, mine)),
        out_shape=jax.ShapeDtypeStruct(got.shape, out_dtype),
        compiler_params=_cparams("parallel", "parallel"),
    )(c_arr, g_all, got)


def _sum_slots(a, name):
    n, w = a.shape[1:]
    by_rows = n % 64 == 0
    in_blk = pl.BlockSpec((N_CHIPS, n // 4, w), lambda i: (0, i, 0)) if by_rows else pl.BlockSpec((N_CHIPS, n, LANE), lambda i: (0, 0, i))
    out_blk = pl.BlockSpec((n // 4, w), lambda i: (i, 0)) if by_rows else pl.BlockSpec((n, LANE), lambda i: (0, i))

    def body(a_ref, o_ref):
        o_ref[...] = ((a_ref[0].astype(f32) + a_ref[1].astype(f32)) + a_ref[2].astype(f32)) + a_ref[3].astype(f32)

    return pl.pallas_call(
        body, name=name, grid=(4 if by_rows else w // LANE,),
        in_specs=[in_blk], out_specs=out_blk,
        out_shape=jax.ShapeDtypeStruct((n, w), f32),
        compiler_params=_cparams("parallel"),
    )(a)


def _adamw(w, g, m, v, name):
    rows, width = w.shape
    by_rows = rows % 64 == 0
    c1 = 1.0 / (1.0 - ADAM_B1 ** ADAM_STEP)
    c2 = 1.0 / (1.0 - ADAM_B2 ** ADAM_STEP)

    def body(w_ref, g_ref, m_ref, v_ref, d_ref, mo_ref, vo_ref):
        g_v = g_ref[...]
        m_new = ADAM_B1 * m_ref[...] + (1.0 - ADAM_B1) * g_v
        v_new = ADAM_B2 * v_ref[...] + (1.0 - ADAM_B2) * (g_v * g_v)
        mo_ref[...] = m_new
        vo_ref[...] = v_new
        d_ref[...] = -ADAM_LR * ((m_new * c1) / (jnp.sqrt(v_new * c2) + ADAM_EPS) + ADAM_WD * w_ref[...])

    blk = pl.BlockSpec((rows // 8, width), lambda i: (i, 0)) if by_rows else pl.BlockSpec((rows, LANE), lambda i: (0, i))
    sds = jax.ShapeDtypeStruct((rows, width), f32)
    return pl.pallas_call(
        body, name=name, grid=(8 if by_rows else width // LANE,),
        in_specs=[blk] * 4, out_specs=[blk] * 3, out_shape=[sds] * 3,
        compiler_params=_cparams("parallel"),
    )(w, g, m, v)


R_DW = 3 * SQ_BLK
R_CW = R_DW + 8
R_VEC = R_CW + 8
R_SMALL = R_VEC + 8
REST_ROWS = 896


def _pack_small(conf_dw_w, gdn_conv_w, vecs, a_log, dt_bias, norm_g):
    dw = jnp.pad(conf_dw_w.reshape(-1), (0, 8 * D - KC * SQ_BLK)).reshape(8, D)
    cw = jnp.pad(gdn_conv_w.reshape(-1), (0, 5 * D)).reshape(8, D)
    vec = jnp.pad(jnp.stack(vecs), ((0, 3), (0, 0)))
    small = jnp.pad(jnp.concatenate([a_log, dt_bias, norm_g]), (0, D - 2 * NH - HD)).reshape(1, D)
    return jnp.pad(jnp.concatenate([dw, cw, vec, small], axis=0), ((0, REST_ROWS - R_SMALL - 1), (0, 0)))


def _pack_rest(conf_w_out, gdn_w_out, w_o, small):
    return jnp.concatenate([conf_w_out, gdn_w_out, w_o, small], axis=0)


def _unpack_rest(p):
    conf_dw_w = p[R_DW:R_DW + 8].reshape(-1)[:KC * SQ_BLK].reshape(KC, SQ_BLK)
    gdn_conv_w = p[R_CW:R_CW + 3].reshape(KG, 3 * SQ_BLK)
    small = p[R_SMALL]
    return dict(conf_w_out=p[0:SQ_BLK], gdn_w_out=p[SQ_BLK:2 * SQ_BLK], w_o=p[2 * SQ_BLK:R_DW],
                conf_dw_w=conf_dw_w, gdn_conv_w=gdn_conv_w, conf_dw_b=p[R_VEC], conf_ln_g=p[R_VEC + 1],
                conf_ln_b=p[R_VEC + 2], post_ln_g=p[R_VEC + 3], post_ln_b=p[R_VEC + 4],
                gdn_A_log=small[0:NH], gdn_dt_bias=small[NH:2 * NH], gdn_norm_g=small[2 * NH:2 * NH + HD])


_WEIGHT_ORDER = ("w_in", "conf_dw_w", "conf_dw_b", "conf_ln_g", "conf_ln_b", "conf_w_out", "gdn_conv_w",
                 "gdn_A_log", "gdn_dt_bias", "gdn_norm_g", "gdn_w_out", "w_o", "post_ln_g", "post_ln_b")


def _gather_weights(w_in, conf_w_out, gdn_w_out, w_o, conf_dw_w, gdn_conv_w):
    c = lax.axis_index("c")
    sq = jnp.concatenate([conf_w_out, gdn_w_out, w_o], axis=0).astype(bf16)
    w_half = lax.dynamic_slice_in_dim(w_in.T.astype(bf16), c * (D // 2), D // 2, axis=1)
    sq_half = lax.dynamic_slice_in_dim(sq, c * (sq.shape[0] // 2), sq.shape[0] // 2, axis=0)
    small = jnp.concatenate([jnp.pad(conf_dw_w.reshape(-1), (0, 8 * D - KC * SQ_BLK)).reshape(8, D),
                             jnp.pad(gdn_conv_w.reshape(-1), (0, 5 * D)).reshape(8, D)], axis=0)
    ((w_mine, w_other), (sq_mine, sq_other)), (small_all,) = _gather_halves([w_half, sq_half], [small], "weight_gather")
    w_t = _join_halves(w_mine, w_other, axis=-1).reshape(W_IN_COLS, D)
    sq4 = _join_halves(sq_mine, sq_other)
    sq_full = [sq4[:, n * SQ_BLK:(n + 1) * SQ_BLK].reshape(D, D) for n in range(3)]
    dw_full = small_all[:, 0:8].reshape(N_CHIPS, 8 * D)[:, :KC * SQ_BLK].reshape(N_CHIPS, KC, SQ_BLK)
    dw_full = dw_full.transpose(1, 0, 2).reshape(KC, D)
    cw_full = small_all[:, 8:11].reshape(N_CHIPS, KG, 3 * SQ_BLK).transpose(1, 0, 2).reshape(KG, 3 * D)
    return w_t, sq_full[0], sq_full[1], sq_full[2], dw_full, cw_full


def _pair_sums(g_w, g_rest):
    c_arr = lax.axis_index("c").astype(jnp.int32).reshape(1)
    got_w, got_r = _sibling_merge([g_w, g_rest], "grad_sibling_halves", half_axes=[-1, -2])
    pair_w = _pair_sum(g_w, got_w, c_arr, "grad_pair_sum_w_in", bf16, -1)
    pair_r = _pair_sum(g_rest, got_r, c_arr, "grad_pair_sum_rest", f32, -2)
    return pair_w, pair_r


def _chip_sums(all_w, all_r):
    tot_w, tot_r = _sum_slots(all_w, "grad_chip_sum_w_in"), _sum_slots(all_r, "grad_chip_sum_rest")
    oth_w, oth_r = _sibling_merge([tot_w, tot_r], "grad_sibling_result")
    return _join_halves(tot_w, oth_w, axis=-1), _join_halves(tot_r, oth_r)


def _local_step(x2, tgt, w_t, wc_out, wg_out, wo_full, dw_full, cw_full, conf_dw_b, conf_ln_g, conf_ln_b,
                gdn_A_log, gdn_dt_bias, gdn_norm_g, post_ln_g, post_ln_b):
    t = x2.shape[0]
    tt = min(TOKEN_TILE, t)
    tm = min(MM_TILE, t)

    w_conv, w_qkv, w_gz = w_t[0:3 * D], w_t[3 * D:6 * D], w_t[6 * D:7 * D]
    w_gates = w_t[7 * D + 2 * NH:]
    dw_pad = jnp.pad(dw_full, ((0, HALO_C - KC), (0, 0)))
    cw_pad = jnp.pad(cw_full, ((0, 8 - KG), (0, 0)))
    row = lambda v: v.reshape(1, D)
    alog_v = jnp.pad(gdn_A_log, (NH, HD - 2 * NH)).reshape(1, HD)
    dt_v = jnp.pad(gdn_dt_bias, (NH, HD - 2 * NH)).reshape(1, HD)
    ng_b = row(jnp.tile(gdn_norm_g, NH))
    w_ba = jnp.pad(w_t[7 * D:7 * D + 2 * NH], ((0, HD - 2 * NH), (0, 0)))

    x_b = x2.astype(bf16)

    p_conv = _mm_multi([x_b], [w_conv], out_dtype=f32, tm=tm, tn=MM_TILE, name="proj_conv", rhs_t=True)
    p_qkv = _mm_multi([x_b], [w_qkv], out_dtype=f32, tm=tm, tn=MM_TILE, name="proj_qkv", rhs_t=True)
    p_gz = _mm_multi([x_b], [w_gz], out_dtype=f32, tm=tm, tn=MM_TILE, name="proj_gz", rhs_t=True)
    p_gates = _mm_multi([x_b], [w_gates], out_dtype=f32, tm=tm, tn=MM_TILE, name="proj_gates", rhs_t=True)
    p_ba = _mm_multi([x_b], [w_ba], out_dtype=f32, tm=tm, tn=HD, name="proj_ba", rhs_t=True)

    u, a1, y_conf = _conv_fwd(p_conv, dw_pad, row(conf_dw_b), row(conf_ln_g), row(conf_ln_b), wc_out, tt=tt)

    qn, kn, va, gates, pre_qkv = _gdn_pre_fwd(p_qkv, p_ba, cw_pad, alog_v, dt_v, tt=tt)
    o, states, inverses = _gdn_scan_fwd(qn, kn, va, gates, tt=tt)
    og, y_gdn = _gdn_post_fwd(o, p_gz, ng_b, wg_out, tt=min(2 * TOKEN_TILE, t))

    loss_blk, dxd, dyc, dyg, dp_gates, h, dz, dpost = _merge(
        x2, y_conf, y_gdn, p_gates, tgt, wo_full, row(post_ln_g), row(post_ln_b), tt=tt)

    d_wo = _mm_kloop(h, dz, tm=D, tn=MM_TILE, tk=min(MM_K_TILE, t), name="grad_w_o")
    du = _mm_multi([dyc], [wc_out], out_dtype=f32, tm=tm, tn=MM_TILE, name="conf_out_bwd", rhs_t=True)
    d_wc = _mm_kloop(u, dyc, tm=D, tn=MM_TILE, tk=min(MM_K_TILE, t), name="grad_conf_w_out")
    d_wg = _mm_kloop(og, dyg, tm=D, tn=MM_TILE, tk=min(MM_K_TILE, t), name="grad_gdn_w_out")

    dp_conv, d_dww, dconv_vec = _conv_bwd(p_conv, a1, du, dw_pad, row(conf_ln_g), row(conf_ln_b), tt=tt)

    do, dp_gz, dng = _gdn_post_bwd(o, p_gz, ng_b, wg_out, dyg, tt=min(2 * TOKEN_TILE, t))
    dqn, dkn, dva, dgates = _gdn_scan_bwd(qn, kn, va, gates, states, inverses, do, tt=tt)
    dp_qkv, dp_ba, d_cw, d_ad = _gdn_pre_bwd(p_qkv, p_ba, pre_qkv, cw_pad, alog_v, dt_v, dqn, dkn, dva, dgates, tt=tt)
    dp_ba_b = dp_ba.astype(bf16)

    grad_x_factors = ([dp_conv, dp_qkv, dp_gz, dp_gates, dp_ba_b], [w_conv, w_qkv, w_gz, w_gates, w_ba], dxd)

    tk = min(MM_K_TILE, t)
    d_w_conv = _mm_kloop(dp_conv, x_b, tm=MM_TILE, tn=D, tk=tk, name="grad_w_in_conv")
    d_w_qkv = _mm_kloop(dp_qkv, x_b, tm=MM_TILE, tn=D, tk=tk, name="grad_w_in_qkv")
    d_w_gz = _mm_kloop(dp_gz, x_b, tm=MM_TILE, tn=D, tk=tk, name="grad_w_in_gz")
    d_w_gates = _mm_kloop(dp_gates, x_b, tm=MM_TILE, tn=D, tk=tk, name="grad_w_in_gates")
    d_w_ba = _mm_kloop(dp_ba_b, x_b, tm=HD, tn=D, tk=tk, name="grad_w_in_ba")
    d_w_in = jnp.concatenate([d_w_conv, d_w_qkv, d_w_gz, d_w_ba[:2 * NH], d_w_gates], axis=0).reshape(
        N_CHIPS, W_IN_BLK, D)

    return (loss_blk[0, 0], grad_x_factors, d_w_in, d_wc, d_wg, d_wo, d_dww, d_cw, dconv_vec, dpost, d_ad, dng)


def kernel(x, w_in, conf_dw_w, conf_dw_b, conf_ln_g, conf_ln_b, conf_w_out, gdn_conv_w, gdn_A_log, gdn_dt_bias, gdn_norm_g, gdn_w_out, w_o, post_ln_g, post_ln_b, loss_target, m_w_in, m_conf_dw_w, m_conf_dw_b, m_conf_ln_g, m_conf_ln_b, m_conf_w_out, m_gdn_conv_w, m_gdn_A_log, m_gdn_dt_bias, m_gdn_norm_g, m_gdn_w_out, m_w_o, m_post_ln_g, m_post_ln_b, v_w_in, v_conf_dw_w, v_conf_dw_b, v_conf_ln_g, v_conf_ln_b, v_conf_w_out, v_gdn_conv_w, v_gdn_A_log, v_gdn_dt_bias, v_gdn_norm_g, v_gdn_w_out, v_w_o, v_post_ln_g, v_post_ln_b):
    x2 = x.reshape(x.shape[-2], D)
    tgt = loss_target.reshape(x2.shape)
    w_t, wc_out, wg_out, wo_full, dw_full, cw_full = _gather_weights(
        w_in, conf_w_out, gdn_w_out, w_o, conf_dw_w, gdn_conv_w)
    (loss_part, grad_x_factors, d_w_in, d_wc, d_wg, d_wo, d_dww, d_cw, dconv_vec, dpost, d_ad, dng) = _local_step(
        x2, tgt, w_t, wc_out, wg_out, wo_full, dw_full, cw_full, conf_dw_b, conf_ln_g, conf_ln_b,
        gdn_A_log, gdn_dt_bias, gdn_norm_g, post_ln_g, post_ln_b)
    loss = lax.psum(loss_part, ("x", "y", "c"))

    dww_c = d_dww[:KC].reshape(KC, N_CHIPS, SQ_BLK)
    dcw_c = d_cw[:KG].reshape(KG, N_CHIPS, 3 * SQ_BLK)
    vecs = [dconv_vec[0], dconv_vec[1], dconv_vec[2], dpost[0], dpost[1]]
    g_rest = jnp.stack([
        _pack_rest(d_wc[j * SQ_BLK:(j + 1) * SQ_BLK], d_wg[j * SQ_BLK:(j + 1) * SQ_BLK], d_wo[j * SQ_BLK:(j + 1) * SQ_BLK],
                   _pack_small(dww_c[:, j], dcw_c[:, j], vecs, d_ad[0, NH:2 * NH], d_ad[1, NH:2 * NH], dng[0]))
        for j in range(N_CHIPS)])
    pair_w, pair_r = _pair_sums(d_w_in, g_rest)
    grad_x, all_w, all_r = _mm_multi(*grad_x_factors, out_dtype=f32, tm=min(GRAD_X_TILE[0], x2.shape[0]), tn=GRAD_X_TILE[1],
                                     name="grad_x_and_chip_scatter", scatter=[pair_w, pair_r])
    g_w_in, g_rest = _chip_sums(all_w, all_r)

    def rest_of(w_c, w_g, w_oo, dw, cw, b1, g1, b2, g2, b3, a_log, dt_bias, norm_g):
        return _pack_rest(w_c, w_g, w_oo, _pack_small(dw, cw, [b1, g1, b2, g2, b3], a_log, dt_bias, norm_g))

    w_r = rest_of(conf_w_out, gdn_w_out, w_o, conf_dw_w, gdn_conv_w, conf_dw_b, conf_ln_g, conf_ln_b,
                  post_ln_g, post_ln_b, gdn_A_log, gdn_dt_bias, gdn_norm_g)
    m_r = rest_of(m_conf_w_out, m_gdn_w_out, m_w_o, m_conf_dw_w, m_gdn_conv_w, m_conf_dw_b, m_conf_ln_g, m_conf_ln_b,
                  m_post_ln_g, m_post_ln_b, m_gdn_A_log, m_gdn_dt_bias, m_gdn_norm_g)
    v_r = rest_of(v_conf_w_out, v_gdn_w_out, v_w_o, v_conf_dw_w, v_gdn_conv_w, v_conf_dw_b, v_conf_ln_g, v_conf_ln_b,
                  v_post_ln_g, v_post_ln_b, v_gdn_A_log, v_gdn_dt_bias, v_gdn_norm_g)
    upd_w_in = _adamw(w_in.T, g_w_in, m_w_in.T, v_w_in.T, "adamw_w_in")
    upd_rest = _adamw(w_r, g_rest, m_r, v_r, "adamw_rest")

    out = [loss, grad_x.reshape(x.shape)]
    for big, rest in zip((g_w_in,) + tuple(upd_w_in), (g_rest,) + tuple(upd_rest)):
        d = dict(_unpack_rest(rest), w_in=big.T)
        out += [d[n] for n in _WEIGHT_ORDER]
    return tuple(out)
```
